```python
import math
import jax, jax.numpy as jnp
from jax import lax
import numpy as np

D_MODEL = 1024
BATCH = 16
SEQ = 256
DEPTH = 2
DEC_BATCH = 2
DEC_SEQ = 2048
PAST_LEN = 256

GRID_W = 64
CHUNK = 64
EPS = 1e-6
FFN_DIM = 2816
N_MOD = 9
GLA_HEADS = 4
GLA_DK = 32
GLA_DV = 64
GLA_RANK = 16
GLA_GATE_NORM = 16.0
GDN_HEADS = 4
GDN_DK = 128
GDN_DV = 128
GDN_CONV = 5
S5_GROUPS = 16
S5_CH = 16
S5_P = 64
GLA_W = GLA_HEADS * GLA_DV
GDN_W = GDN_HEADS * GDN_DV
S5_W = S5_GROUPS * S5_CH
MIX_W = GLA_W + GDN_W + S5_W
IN_SIZES = (GLA_HEADS * GLA_DK, GLA_HEADS * GLA_DK, GLA_W, GLA_RANK, GLA_RANK, GLA_W,
            GDN_HEADS * GDN_DK, GDN_HEADS * GDN_DK, GDN_W, GDN_HEADS, GDN_HEADS, GDN_HEADS, GDN_HEADS, GDN_W,
            S5_W)
IN_OFFSETS = tuple(sum(IN_SIZES[:i + 1]) for i in range(len(IN_SIZES) - 1))
IN_W = sum(IN_SIZES)

kernel_name = 'hymba_gla_gdn_s5_diffusion_step'


def _f32(a):
    return a.astype(jnp.float32)


def _rmsnorm(x, w):
    x32 = _f32(x)
    y = x32 * lax.rsqrt(jnp.mean(x32 * x32, axis=-1, keepdims=True) + EPS)
    return y * _f32(w)


def _l2norm(x):
    return x * lax.rsqrt(jnp.sum(x * x, axis=-1, keepdims=True) + EPS)


def _flip(t):
    return jnp.flip(t, axis=1)


def _chunk(t):
    b, n, h, d = t.shape
    return t.reshape(b, n // CHUNK, CHUNK, h, d).transpose(1, 0, 3, 2, 4)


def _unchunk(t):
    n, b, h, c, d = t.shape
    return t.transpose(1, 0, 3, 2, 4).reshape(b, n * c, h, d)


def _swiglu(h, w_gate, w_up, w_down):
    return (jax.nn.silu(h @ _f32(w_gate)) * (h @ _f32(w_up))) @ _f32(w_down)


def _short_conv(x, w, rows):
    b, n, ch = x.shape
    xr = x.reshape(b * rows, n // rows, ch)
    y = lax.conv_general_dilated(xr, _f32(w)[:, None, :], window_strides=(1,),
                                 padding=[(GDN_CONV // 2, GDN_CONV // 2)],
                                 dimension_numbers=('NWC', 'WIO', 'NWC'), feature_group_count=ch)
    return y.reshape(b, n, ch)


def _gla_chunked(q, k, v, log_g, s0):
    qc, kc, vc, gc = (_chunk(t) for t in (q, k, v, log_g))
    bcum = jnp.cumsum(gc, axis=-2)
    incl = jnp.tril(jnp.ones((CHUNK, CHUNK), bool))
    rel = jnp.where(incl[:, :, None], bcum[..., :, None, :] - bcum[..., None, :, :], -jnp.inf)
    att = jnp.einsum('nbhtd,nbhsd,nbhtsd->nbhts', qc, kc, jnp.exp(rel))
    o_intra = jnp.einsum('nbhts,nbhse->nbhte', att, vc)
    q_dec = qc * jnp.exp(bcum)
    k_dec = kc * jnp.exp(bcum[..., -1:, :] - bcum)
    dec_last = jnp.exp(bcum[..., -1, :])

    def step(s, inp):
        qd, kd, vv, dl = inp
        o = jnp.einsum('bhtd,bhde->bhte', qd, s)
        s = dl[..., None] * s + jnp.einsum('bhtd,bhte->bhde', kd, vv)
        return s, o

    s_fin, o_inter = lax.scan(step, s0, (q_dec, k_dec, vc, dec_last))
    return _unchunk(o_inter + o_intra), s_fin


def _gdn_chunked(q, k, v, log_a, beta, s0):
    qc, kc, vc = (_chunk(t) for t in (q, k, v))
    g = jnp.cumsum(_chunk(log_a[..., None])[..., 0], axis=-1)
    bt = _chunk(beta[..., None])[..., 0]
    incl = jnp.tril(jnp.ones((CHUNK, CHUNK), bool))
    strict = jnp.tril(jnp.ones((CHUNK, CHUNK), bool), -1)
    decay = jnp.exp(jnp.where(incl, g[..., :, None] - g[..., None, :], -jnp.inf))
    lmat = jnp.where(strict, decay, 0.0) * jnp.einsum('nbhtd,nbhsd->nbhts', kc, kc) * bt[..., :, None]
    eg = jnp.exp(g)
    rhs = jnp.concatenate([(bt * eg)[..., None] * kc, bt[..., None] * vc], axis=-1)
    sol = lax.linalg.triangular_solve(jnp.eye(CHUNK, dtype=lmat.dtype) + lmat, rhs,
                                      left_side=True, lower=True)
    w, u0 = jnp.split(sol, [kc.shape[-1]], axis=-1)
    qk = jnp.einsum('nbhtd,nbhsd->nbhts', qc, kc) * decay
    k_dec = kc * jnp.exp(g[..., -1:] - g)[..., None]
    eg_last = eg[..., -1]

    def step(s, inp):
        w_c, u0_c, q_c, kd_c, qk_c, eg_c, egl_c = inp
        u = u0_c - jnp.einsum('bhtd,bhde->bhte', w_c, s)
        o = eg_c[..., None] * jnp.einsum('bhtd,bhde->bhte', q_c, s) + jnp.einsum('bhts,bhse->bhte', qk_c, u)
        s = egl_c[..., None, None] * s + jnp.einsum('bhtd,bhte->bhde', kd_c, u)
        return s, o

    s_fin, o = lax.scan(step, s0, (w, u0, qc, k_dec, qk, eg, eg_last))
    return _unchunk(o), s_fin


def _s5_scan(u, lam_re, lam_im, log_step, b_re, b_im, c_re, c_im, h0_re, h0_im):
    lam = lax.complex(_f32(lam_re), _f32(lam_im))
    lam_dt = lam * jnp.exp(_f32(log_step))[:, None]
    lam_bar = jnp.exp(lam_dt)
    b_bar = ((lam_bar - 1.0) / lam)[..., None] * lax.complex(_f32(b_re), _f32(b_im))
    c_mat = lax.complex(_f32(c_re), _f32(c_im))
    bu = jnp.einsum('gpc,bngc->bngp', b_bar, u.astype(jnp.complex64))
    a = jnp.broadcast_to(lam_bar, bu.shape)

    def combine(left, right):
        a_l, b_l = left
        a_r, b_r = right
        return a_r * a_l, a_r * b_l + b_r

    _, h = lax.associative_scan(combine, (a, bu), axis=1)
    n = u.shape[1]
    pos = jnp.arange(1, n + 1, dtype=jnp.float32)[:, None, None]
    h = h + jnp.exp(lam_dt[None] * pos)[None] * lax.complex(_f32(h0_re), _f32(h0_im))[:, None]
    y = jnp.real(jnp.einsum('gcp,bngp->bngc', c_mat, h))
    return y, h[:, -1]


def _token_mix(h, rows, states, p, l):
    st_gla, st_gdn, st_s5_re, st_s5_im = (_f32(s) for s in states)
    b, n, _ = h.shape
    (gq, gk, gv, glr_f, glr_b, gog, dq, dk, dv, da_f, da_b, db_f, db_b, dog, su) = jnp.split(
        h @ _f32(p['w_in'][l]), IN_OFFSETS, axis=-1)

    q = gq.reshape(b, n, GLA_HEADS, GLA_DK) * GLA_DK ** -0.5
    k = gk.reshape(b, n, GLA_HEADS, GLA_DK)
    v = gv.reshape(b, n, GLA_HEADS, GLA_DV)
    o_gla = jnp.zeros((b, n, GLA_HEADS, GLA_DV), jnp.float32)
    fin_gla = []
    for d, lowrank in enumerate((glr_f, glr_b)):
        lg = jax.nn.log_sigmoid(lowrank @ _f32(p['gla_gk_up'][l, d]) + _f32(p['gla_gk_bias'][l, d])) / GLA_GATE_NORM
        lg = lg.reshape(b, n, GLA_HEADS, GLA_DK)
        args = (q, k, v, lg) if d == 0 else tuple(_flip(t) for t in (q, k, v, lg))
        o, s = _gla_chunked(*args, st_gla[:, d])
        o_gla = o_gla + (o if d == 0 else _flip(o))
        fin_gla.append(s)
    o_gla = _rmsnorm(o_gla, p['gla_norm_w'][l]) * jax.nn.silu(gog.reshape(b, n, GLA_HEADS, GLA_DV))
    o_gla = o_gla.reshape(b, n, GLA_W)

    qkv = jax.nn.silu(_short_conv(jnp.concatenate([dq, dk, dv], axis=-1), p['gdn_conv_w'][l], rows))
    q, k, v = jnp.split(qkv, [GDN_HEADS * GDN_DK, 2 * GDN_HEADS * GDN_DK], axis=-1)
    q = _l2norm(q.reshape(b, n, GDN_HEADS, GDN_DK)) * GDN_DK ** -0.5
    k = _l2norm(k.reshape(b, n, GDN_HEADS, GDN_DK))
    v = v.reshape(b, n, GDN_HEADS, GDN_DV)
    o_gdn = jnp.zeros((b, n, GDN_HEADS, GDN_DV), jnp.float32)
    fin_gdn = []
    for d, (a_raw, b_raw) in enumerate(((da_f, db_f), (da_b, db_b))):
        log_a = -jnp.exp(_f32(p['gdn_a_log'][l, d])) * jax.nn.softplus(a_raw + _f32(p['gdn_dt_bias'][l, d]))
        beta = jax.nn.sigmoid(b_raw)
        args = (q, k, v, log_a, beta) if d == 0 else tuple(_flip(t) for t in (q, k, v, log_a, beta))
        o, s = _gdn_chunked(*args, st_gdn[:, d])
        o_gdn = o_gdn + (o if d == 0 else _flip(o))
        fin_gdn.append(s)
    o_gdn = _rmsnorm(o_gdn, p['gdn_norm_w'][l]) * jax.nn.silu(dog.reshape(b, n, GDN_HEADS, GDN_DV))
    o_gdn = o_gdn.reshape(b, n, GDN_W)

    u = su.reshape(b, n, S5_GROUPS, S5_CH)
    y = u * _f32(p['s5_d'][l]).reshape(S5_GROUPS, S5_CH)
    fin_s5 = []
    for d in range(2):
        ud = u if d == 0 else _flip(u)
        yd, hf = _s5_scan(ud, p['s5_lam_re'][l, d], p['s5_lam_im'][l, d], p['s5_log_step'][l, d],
                          p['s5_b_re'][l, d], p['s5_b_im'][l, d], p['s5_c_re'][l, d], p['s5_c_im'][l, d],
                          st_s5_re[:, d], st_s5_im[:, d])
        y = y + (yd if d == 0 else _flip(yd))
        fin_s5.append(hf)
    g = jax.nn.gelu(y.reshape(b, n, S5_W))
    o_s5 = g * jax.nn.sigmoid(g @ _f32(p['s5_w_glu'][l]) + _f32(p['s5_b_glu'][l]))

    out = jnp.concatenate([o_gla, o_gdn, o_s5], axis=-1) @ _f32(p['w_out'][l])
    h_s5 = jnp.stack(fin_s5, axis=1)
    new = (jnp.stack(fin_gla, axis=1), jnp.stack(fin_gdn, axis=1), jnp.real(h_s5), jnp.imag(h_s5))
    return out, new


def _trunk(x, cond, rows, states, p):
    act = x.dtype
    sc = jax.nn.silu(_f32(cond))
    finals = []
    for l in range(DEPTH):
        mod = (sc @ _f32(p['w_ada'][l]) + _f32(p['b_ada'][l]))[:, None, :]
        sh1, sc1, g1, sh2, sc2, g2, sh3, sc3, g3 = jnp.split(mod, N_MOD, axis=-1)
        h = _rmsnorm(x, p['norm_w'][l, 0]) * (1.0 + sc1) + sh1
        x = x + (0.5 * g1 * _swiglu(h, p['ffn_w_gate'][l, 0], p['ffn_w_up'][l, 0], p['ffn_w_down'][l, 0])).astype(act)
        h = _rmsnorm(x, p['norm_w'][l, 1]) * (1.0 + sc2) + sh2
        y, st = _token_mix(h, rows, tuple(s[:, l] for s in states), p, l)
        x = x + (g2 * y).astype(act)
        h = _rmsnorm(x, p['norm_w'][l, 2]) * (1.0 + sc3) + sh3
        x = x + (0.5 * g3 * _swiglu(h, p['ffn_w_gate'][l, 1], p['ffn_w_up'][l, 1], p['ffn_w_down'][l, 1])).astype(act)
        finals.append(st)
    return _rmsnorm(x, p['final_norm_w']).astype(act), finals


def setup_inputs(seed: int = 0) -> dict:
    key = jax.random.key(seed)
    keys = jax.random.split(key, 40)

    def nrm(i, shape, s):
        return jax.random.normal(keys[i], shape, jnp.float32) * s

    def unif(i, shape, lo, hi):
        return jax.random.uniform(keys[i], shape, jnp.float32, minval=lo, maxval=hi)

    L = DEPTH
    dt = jnp.exp(unif(14, (L, 2, GDN_HEADS), math.log(1e-3), math.log(1e-1)))
    n_idx = jnp.arange(S5_P, dtype=jnp.float32)
    return {
        'x_prompt': nrm(0, (BATCH, SEQ, D_MODEL), 1.0),
        'x_sample': nrm(1, (DEC_BATCH, DEC_SEQ, D_MODEL), 1.0),
        'c': nrm(2, (DEC_BATCH, D_MODEL), 1.0),
        'state_gla': nrm(3, (DEC_BATCH, L, 2, GLA_HEADS, GLA_DK, GLA_DV), 0.1),
        'state_gdn': nrm(4, (DEC_BATCH, L, 2, GDN_HEADS, GDN_DK, GDN_DV), 0.1),
        'state_s5_re': nrm(5, (DEC_BATCH, L, 2, S5_GROUPS, S5_P), 0.5),
        'state_s5_im': nrm(6, (DEC_BATCH, L, 2, S5_GROUPS, S5_P), 0.5),
        'c_ctx': nrm(7, (D_MODEL,), 1.0),
        'w_ada': nrm(8, (L, D_MODEL, N_MOD * D_MODEL), D_MODEL ** -0.5),
        'b_ada': nrm(9, (L, N_MOD * D_MODEL), 0.02),
        'norm_w': 1.0 + nrm(10, (L, 3, D_MODEL), 0.05),
        'ffn_w_gate': nrm(11, (L, 2, D_MODEL, FFN_DIM), D_MODEL ** -0.5),
        'ffn_w_up': nrm(12, (L, 2, D_MODEL, FFN_DIM), D_MODEL ** -0.5),
        'ffn_w_down': nrm(13, (L, 2, FFN_DIM, D_MODEL), FFN_DIM ** -0.5),
        'w_in': nrm(15, (L, D_MODEL, IN_W), D_MODEL ** -0.5),
        'gla_gk_up': nrm(16, (L, 2, GLA_RANK, GLA_HEADS * GLA_DK), GLA_RANK ** -0.5),
        'gla_gk_bias': nrm(17, (L, 2, GLA_HEADS * GLA_DK), 0.1),
        'gla_norm_w': 1.0 + nrm(18, (L, GLA_DV), 0.05),
        'gdn_conv_w': nrm(19, (L, GDN_CONV, 2 * GDN_HEADS * GDN_DK + GDN_W), GDN_CONV ** -0.5),
        'gdn_a_log': jnp.log(unif(20, (L, 2, GDN_HEADS), 1.0, 16.0)),
        'gdn_dt_bias': dt + jnp.log(-jnp.expm1(-dt)),
        'gdn_norm_w': 1.0 + nrm(21, (L, GDN_DV), 0.05),
        's5_lam_re': -0.5 + nrm(22, (L, 2, S5_GROUPS, S5_P), 0.01),
        's5_lam_im': math.pi * n_idx + nrm(23, (L, 2, S5_GROUPS, S5_P), 0.01),
        's5_log_step': unif(24, (L, 2, S5_GROUPS), math.log(1e-3), math.log(1e-1)),
        's5_b_re': nrm(25, (L, 2, S5_GROUPS, S5_P, S5_CH), (2 * S5_CH) ** -0.5),
        's5_b_im': nrm(26, (L, 2, S5_GROUPS, S5_P, S5_CH), (2 * S5_CH) ** -0.5),
        's5_c_re': nrm(27, (L, 2, S5_GROUPS, S5_CH, S5_P), (2 * S5_P) ** -0.5),
        's5_c_im': nrm(28, (L, 2, S5_GROUPS, S5_CH, S5_P), (2 * S5_P) ** -0.5),
        's5_d': nrm(29, (L, S5_W), 1.0),
        's5_w_glu': nrm(30, (L, S5_W, S5_W), S5_W ** -0.5),
        's5_b_glu': nrm(31, (L, S5_W), 0.02),
        'w_out': nrm(32, (L, MIX_W, D_MODEL), MIX_W ** -0.5),
        'final_norm_w': 1.0 + nrm(33, (D_MODEL,), 0.05),
    }


def reference(x_prompt, x_sample, c, state_gla, state_gdn, state_s5_re, state_s5_im, c_ctx,
              w_ada, b_ada, norm_w, ffn_w_gate, ffn_w_up, ffn_w_down, w_in,
              gla_gk_up, gla_gk_bias, gla_norm_w, gdn_conv_w, gdn_a_log, gdn_dt_bias, gdn_norm_w,
              s5_lam_re, s5_lam_im, s5_log_step, s5_b_re, s5_b_im, s5_c_re, s5_c_im, s5_d,
              s5_w_glu, s5_b_glu, w_out, final_norm_w):
    p = dict(w_ada=w_ada, b_ada=b_ada, norm_w=norm_w, ffn_w_gate=ffn_w_gate, ffn_w_up=ffn_w_up,
             ffn_w_down=ffn_w_down, w_in=w_in, gla_gk_up=gla_gk_up, gla_gk_bias=gla_gk_bias,
             gla_norm_w=gla_norm_w, gdn_conv_w=gdn_conv_w, gdn_a_log=gdn_a_log, gdn_dt_bias=gdn_dt_bias,
             gdn_norm_w=gdn_norm_w, s5_lam_re=s5_lam_re, s5_lam_im=s5_lam_im, s5_log_step=s5_log_step,
             s5_b_re=s5_b_re, s5_b_im=s5_b_im, s5_c_re=s5_c_re, s5_c_im=s5_c_im, s5_d=s5_d,
             s5_w_glu=s5_w_glu, s5_b_glu=s5_b_glu, w_out=w_out, final_norm_w=final_norm_w)

    b_ctx = x_prompt.shape[0]
    zero_states = (jnp.zeros((b_ctx, DEPTH, 2, GLA_HEADS, GLA_DK, GLA_DV), jnp.float32),
                   jnp.zeros((b_ctx, DEPTH, 2, GDN_HEADS, GDN_DK, GDN_DV), jnp.float32),
                   jnp.zeros((b_ctx, DEPTH, 2, S5_GROUPS, S5_P), jnp.float32),
                   jnp.zeros((b_ctx, DEPTH, 2, S5_GROUPS, S5_P), jnp.float32))
    y_prompt, ctx_finals = _trunk(x_prompt, c_ctx[None, :], 1, zero_states, p)
    new_state_gla = jnp.stack([f[0] for f in ctx_finals], axis=1).astype(state_gla.dtype)
    new_state_gdn = jnp.stack([f[1] for f in ctx_finals], axis=1).astype(state_gdn.dtype)
    new_state_s5_re = jnp.stack([f[2] for f in ctx_finals], axis=1).astype(state_s5_re.dtype)
    new_state_s5_im = jnp.stack([f[3] for f in ctx_finals], axis=1).astype(state_s5_im.dtype)

    rows = x_sample.shape[1] // GRID_W
    y_sample, _ = _trunk(x_sample, c, rows, (state_gla, state_gdn, state_s5_re, state_s5_im), p)
    return (y_prompt, y_sample, new_state_gla, new_state_gdn, new_state_s5_re, new_state_s5_im)
```

```python
import functools
import math

import numpy as np
import jax
import jax.numpy as jnp
from jax import lax
from jax.experimental import pallas as pl
from jax.experimental.pallas import tpu as pltpu

F32 = jnp.float32
BF16 = jnp.bfloat16

D_MODEL = 1024
DEPTH = 2
FFN_DIM = 2816
N_MOD = 9
EPS = 1e-6
CHUNK = 64
GLA_HEADS, GLA_DK, GLA_DV, GLA_RANK = 4, 32, 64, 16
GLA_GATE_NORM = 16.0
GDN_HEADS, GDN_DK, GDN_DV, GDN_CONV = 4, 128, 128, 5
S5_GROUPS, S5_CH, S5_P = 16, 16, 64
GLA_W = GLA_HEADS * GLA_DV
GLA_KW = GLA_HEADS * GLA_DK
GDN_W = GDN_HEADS * GDN_DV
S5_W = S5_GROUPS * S5_CH
S5_N = S5_GROUPS * S5_P
MIX_W = GLA_W + GDN_W + S5_W

_IN_SIZES = (GLA_KW, GLA_KW, GLA_W, GLA_RANK, GLA_RANK, GLA_W,
             GDN_W, GDN_W, GDN_W, GDN_HEADS, GDN_HEADS, GDN_HEADS, GDN_HEADS, GDN_W, S5_W)
_IN_OFF = tuple(int(v) for v in np.cumsum((0,) + _IN_SIZES))
(_GQ, _GK, _GV, _GLRF, _GLRB, _GOG, _DQ, _DK, _DV, _DAF, _DAB, _DBF, _DBB, _DOG, _SU) = range(15)

P_GQKV = 0
P_GOG = 512
P_SU = 768
P_DOG = 1024
P_DQKV = 1536
P_SMALL = 3072
PROJ_W = 3328
SM_A = GLA_RANK
SM_B = GLA_RANK + GDN_HEADS
SUB = 256
VMEM_LIMIT = 56 * 1024 * 1024


def _cparams(sem):
    return pltpu.CompilerParams(dimension_semantics=sem, vmem_limit_bytes=VMEM_LIMIT)


def _bdot(a, b):
    return jnp.dot(a.astype(BF16), b.astype(BF16), preferred_element_type=F32)


def _bdot_nt(a, b):
    return lax.dot_general(a.astype(BF16), b.astype(BF16), (((1,), (1,)), ((), ())),
                           preferred_element_type=F32)


def _split3(x):
    hi = x.astype(BF16)
    r1 = x - hi.astype(F32)
    mid = r1.astype(BF16)
    lo = (r1 - mid.astype(F32)).astype(BF16)
    return hi, mid, lo


def _sel_dot(m01, x):
    hi, mid, lo = _split3(x)
    return (jnp.dot(m01, hi, preferred_element_type=F32) + jnp.dot(m01, mid, preferred_element_type=F32)
            + jnp.dot(m01, lo, preferred_element_type=F32))


def _silu(x):
    return x * jax.nn.sigmoid(x)


def _softplus(x):
    return jnp.maximum(x, 0.0) + jnp.log1p(jnp.exp(-jnp.abs(x)))


def _log_sigmoid(x):
    return -_softplus(-x)


def _norm_mod(x, nw, scale, shift):
    y = x * lax.rsqrt(jnp.mean(x * x, axis=-1, keepdims=True) + EPS)
    return y * nw * (1.0 + scale) + shift


def _pick_tile(n, cap):
    t = cap
    while n % t:
        t //= 2
    return t


def _ada_kernel(c_ref, w_ref, b_ref, o_ref):
    o_ref[...] = _bdot(_silu(c_ref[...]), w_ref[...]) + b_ref[...]


def _ada(cond8, w_ada, b_ada):
    depth, d, nout = w_ada.shape
    tn = 1024
    return pl.pallas_call(
        _ada_kernel,
        grid=(depth, nout // tn),
        in_specs=[pl.BlockSpec((8, d), lambda l, j: (0, 0)),
                  pl.BlockSpec((None, d, tn), lambda l, j: (l, 0, j)),
                  pl.BlockSpec((None, 1, tn), lambda l, j: (l, 0, j))],
        out_specs=pl.BlockSpec((None, 8, tn), lambda l, j: (l, 0, j)),
        out_shape=jax.ShapeDtypeStruct((depth, 8, nout), F32),
        compiler_params=_cparams(("parallel", "parallel")),
        name="ada",
    )(cond8, w_ada, b_ada.reshape(depth, 1, nout))


def _ffn_kernel(x_ref, mod_ref, nw_ref, wg_ref, wu_ref, wd_ref, fw_ref, o_ref, h_ref, acc_ref,
                *, mod_base, final):
    j = pl.program_id(1)

    @pl.when(j == 0)
    def _():
        m = mod_ref[...]
        h = _norm_mod(x_ref[...], nw_ref[...], m[mod_base + 1:mod_base + 2], m[mod_base:mod_base + 1])
        h_ref[...] = h.astype(BF16)
        acc_ref[...] = jnp.zeros_like(acc_ref)

    h = h_ref[...]
    g = jnp.dot(h, wg_ref[...].astype(BF16), preferred_element_type=F32)
    u = jnp.dot(h, wu_ref[...].astype(BF16), preferred_element_type=F32)
    acc_ref[...] += _bdot(_silu(g) * u, wd_ref[...])

    @pl.when(j == pl.num_programs(1) - 1)
    def _():
        m = mod_ref[...]
        y = x_ref[...] + 0.5 * m[mod_base + 2:mod_base + 3] * acc_ref[...]
        if final:
            y = y * lax.rsqrt(jnp.mean(y * y, axis=-1, keepdims=True) + EPS) * fw_ref[...]
        o_ref[...] = y


def _mod_index(nb, tm, n_seq):
    if nb == 1:
        return lambda i: 0
    return lambda i: (i * tm) // n_seq


def _ffn(x, mods, norm_w, w_gate, w_up, w_down, final_w, *, layer, which, n_seq, final):
    t, d = x.shape
    f = w_gate.shape[-1]
    nb = mods.shape[0]
    tm = _pick_tile(n_seq if nb > 1 else t, 1024)
    tf = 256
    midx = _mod_index(nb, tm, n_seq)
    mod_base = 0 if which == 0 else 6
    return pl.pallas_call(
        functools.partial(_ffn_kernel, mod_base=mod_base, final=final),
        grid=(t // tm, f // tf),
        in_specs=[pl.BlockSpec((tm, d), lambda i, j: (i, 0)),
                  pl.BlockSpec((None, N_MOD, d), lambda i, j: (midx(i), 0, 0)),
                  pl.BlockSpec((None, None, 1, d), lambda i, j: (layer, 2 * which, 0, 0)),
                  pl.BlockSpec((None, None, d, tf), lambda i, j: (layer, which, 0, j)),
                  pl.BlockSpec((None, None, d, tf), lambda i, j: (layer, which, 0, j)),
                  pl.BlockSpec((None, None, tf, d), lambda i, j: (layer, which, j, 0)),
                  pl.BlockSpec((1, d), lambda i, j: (0, 0))],
        out_specs=pl.BlockSpec((tm, d), lambda i, j: (i, 0)),
        out_shape=jax.ShapeDtypeStruct((t, d), F32),
        scratch_shapes=[pltpu.VMEM((tm, d), BF16), pltpu.VMEM((tm, d), F32)],
        compiler_params=_cparams(("parallel", "arbitrary")),
        name="ffn",
    )(x, mods, norm_w.reshape(DEPTH, 3, 1, d), w_gate, w_up, w_down, final_w.reshape(1, d))


def _proj_kernel(x_ref, mod_ref, nw_ref, w_ref, o_ref, h_ref):
    @pl.when(pl.program_id(1) == 0)
    def _():
        m = mod_ref[...]
        h_ref[...] = _norm_mod(x_ref[...], nw_ref[...], m[4:5], m[3:4]).astype(BF16)

    o_ref[...] = jnp.dot(h_ref[...], w_ref[...].astype(BF16), preferred_element_type=F32)


def _proj(x, mods, norm_w, w_in_p, *, layer, n_seq):
    t, d = x.shape
    nb = mods.shape[0]
    tm = _pick_tile(n_seq if nb > 1 else t, 512)
    tn = PROJ_W // 2
    midx = _mod_index(nb, tm, n_seq)
    return pl.pallas_call(
        _proj_kernel,
        grid=(t // tm, PROJ_W // tn),
        in_specs=[pl.BlockSpec((tm, d), lambda i, j: (i, 0)),
                  pl.BlockSpec((None, N_MOD, d), lambda i, j: (midx(i), 0, 0)),
                  pl.BlockSpec((None, None, 1, d), lambda i, j: (layer, 1, 0, 0)),
                  pl.BlockSpec((None, d, tn), lambda i, j: (layer, 0, j))],
        out_specs=pl.BlockSpec((tm, tn), lambda i, j: (i, j)),
        out_shape=jax.ShapeDtypeStruct((t, PROJ_W), F32),
        scratch_shapes=[pltpu.VMEM((tm, d), BF16)],
        compiler_params=_cparams(("parallel", "arbitrary")),
        name="proj",
    )(x, mods, norm_w.reshape(DEPTH, 3, 1, d), w_in_p)


def _permute_w_in(w_in):
    def seg(i):
        return w_in[:, :, _IN_OFF[i]:_IN_OFF[i + 1]]

    def pad(n):
        return jnp.zeros(w_in.shape[:2] + (n,), w_in.dtype)

    sm_pad = 128 - GLA_RANK - 2 * GDN_HEADS
    return jnp.concatenate(
        [seg(_GQ), seg(_GK), seg(_GV), seg(_GOG), seg(_SU), seg(_DOG), seg(_DQ), seg(_DK), seg(_DV),
         seg(_GLRF), seg(_DAF), seg(_DBF), pad(sm_pad),
         seg(_GLRB), seg(_DAB), seg(_DBB), pad(sm_pad)], axis=-1)


def _np_consts():
    t = np.arange(CHUNK)
    lower = (t[:, None] >= t[None, :])
    incl = np.stack([lower, lower.T])
    strict = np.stack([t[:, None] > t[None, :], t[:, None] < t[None, :]])
    eye4 = np.eye(4, dtype=bool)
    blk = np.kron(eye4, np.ones((CHUNK, CHUNK), bool))
    incl_bd = np.stack([np.kron(eye4, incl[d]) for d in range(2)])
    strict_bd = np.stack([np.kron(eye4, strict[d]) for d in range(2)])
    levels = []
    for d in range(2):
        per = []
        for lv in range(6):
            s = 1 << lv
            same = (t[:, None] // (2 * s)) == (t[None, :] // (2 * s))
            hi = (t % (2 * s)) >= s
            m = same & hi[:, None] & (~hi)[None, :]
            if d == 1:
                m = m.T
            per.append(np.kron(eye4, m))
        levels.append(np.stack(per))
    levels = np.stack(levels)
    assert (levels.sum(1) == strict_bd).all() and blk.any()
    gla_mask = np.stack([np.tile(incl[d], (1, 4)) for d in range(2)])
    hm_k = np.kron(eye4, np.ones((1, GLA_DK)))
    hm_v = np.kron(eye4, np.ones((1, GLA_DV)))
    st_mask = np.kron(eye4, np.ones((GLA_DV, GLA_DK)))
    ind64 = np.kron(eye4, np.full((GLA_DV, GLA_DV), 1.0 / GLA_DV))
    return dict(incl=incl, incl_bd=incl_bd, strict_bd=strict_bd, levels=levels, gla_mask=gla_mask,
                hm_k=hm_k, hm_v=hm_v, st_mask=st_mask, ind64=ind64, eye256=np.eye(256))


_C = _np_consts()


def _row_block_index(nblk):
    return lambda d, b, i: b * nblk + i + d * (nblk - 1 - 2 * i)


def _gla_kernel(p_ref, sm_ref, up_ref, bias_ref, tri_ref, mask_ref, hmk_ref, hmv_ref, stm_ref, st0_ref,
                o_ref, stf_ref, st_ref, *, nchunk):
    d = pl.program_id(0)

    @pl.when(pl.program_id(2) == 0)
    def _():
        st_ref[...] = st0_ref[...]

    tri = tri_ref[...]
    mask = mask_ref[...]
    hmk = hmk_ref[...]
    hmv = hmv_ref[...]
    stm = stm_ref[...]

    def body(ci, carry):
        c = ci + d * (nchunk - 1 - 2 * ci)
        r0 = pl.multiple_of(c * CHUNK, CHUNK)
        p = p_ref[pl.ds(r0, CHUNK), :]
        q = p[:, 0:GLA_KW] * (GLA_DK ** -0.5)
        k = p[:, GLA_KW:2 * GLA_KW]
        v = p[:, 2 * GLA_KW:2 * GLA_KW + GLA_W]
        x = _bdot(sm_ref[pl.ds(r0, CHUNK), :], up_ref[...]) + bias_ref[...]
        lg = _log_sigmoid(x) * (1.0 / GLA_GATE_NORM)
        b = _sel_dot(tri, lg)
        btot = jnp.sum(lg, axis=0, keepdims=True)
        qd = q * jnp.exp(b)
        ki = k * jnp.exp(-b)
        kd = k * jnp.exp(btot - b)
        kexp = jnp.concatenate([ki * hmk[h:h + 1] for h in range(GLA_HEADS)], axis=0)
        att = _bdot_nt(qd, kexp) * mask
        vexp = jnp.concatenate([v * hmv[h:h + 1] for h in range(GLA_HEADS)], axis=0)
        st = st_ref[...]
        o_ref[pl.ds(r0, CHUNK), :] = _bdot(att, vexp) + _bdot_nt(qd, st)
        st_ref[...] = st * jnp.exp(btot) + _bdot(v.T, kd) * stm
        return carry

    lax.fori_loop(0, nchunk, body, 0)
    stf_ref[...] = st_ref[...]


def _gla(proj, up_p, bias_p, st0, *, nseq, n_seq):
    t = proj.shape[0]
    nblk = n_seq // SUB
    rbi = _row_block_index(nblk)
    c = _C
    consts = [jnp.asarray(c["incl"], BF16), jnp.asarray(c["gla_mask"], F32), jnp.asarray(c["hm_k"], F32),
              jnp.asarray(c["hm_v"], F32), jnp.asarray(c["st_mask"], F32)]
    return pl.pallas_call(
        functools.partial(_gla_kernel, nchunk=SUB // CHUNK),
        grid=(2, nseq, nblk),
        in_specs=[pl.BlockSpec((SUB, 512), lambda d, b, i: (rbi(d, b, i), 0)),
                  pl.BlockSpec((SUB, 128), lambda d, b, i: (rbi(d, b, i), P_SMALL // 128 + d)),
                  pl.BlockSpec((None, 128, GLA_KW), lambda d, b, i: (d, 0, 0)),
                  pl.BlockSpec((None, 1, GLA_KW), lambda d, b, i: (d, 0, 0)),
                  pl.BlockSpec((None, CHUNK, CHUNK), lambda d, b, i: (d, 0, 0)),
                  pl.BlockSpec((None, CHUNK, 256), lambda d, b, i: (d, 0, 0)),
                  pl.BlockSpec((GLA_HEADS, GLA_KW), lambda d, b, i: (0, 0)),
                  pl.BlockSpec((GLA_HEADS, GLA_W), lambda d, b, i: (0, 0)),
                  pl.BlockSpec((GLA_W, GLA_KW), lambda d, b, i: (0, 0)),
                  pl.BlockSpec((None, None, GLA_W, GLA_KW), lambda d, b, i: (d, b, 0, 0))],
        out_specs=[pl.BlockSpec((None, SUB, GLA_W), lambda d, b, i: (d, rbi(d, b, i), 0)),
                   pl.BlockSpec((None, None, GLA_W, GLA_KW), lambda d, b, i: (d, b, 0, 0))],
        out_shape=[jax.ShapeDtypeStruct((2, t, GLA_W), F32),
                   jax.ShapeDtypeStruct((2, nseq, GLA_W, GLA_KW), F32)],
        scratch_shapes=[pltpu.VMEM((GLA_W, GLA_KW), F32)],
        compiler_params=_cparams(("arbitrary", "arbitrary", "arbitrary")),
        name="gla",
    )(proj, proj, up_p, bias_p, *consts, st0)


def _gdn_kernel(p_ref, sm_ref, cw_ref, arow_ref, dtb_ref, tri_ref, incl_ref, strict_ref, lvl_ref, eye_ref,
                s0_ref, o_ref, sf_ref, s_ref, qkv_ref, la_ref, be_ref, *, nchunk, rowlen):
    d = pl.program_id(0)

    @pl.when(pl.program_id(2) == 0)
    def _():
        s_ref[...] = s0_ref[...]

    x = p_ref[...]
    pos = lax.broadcasted_iota(jnp.int32, (SUB, 1), 0) % rowlen
    acc = jnp.zeros_like(x)
    for j in range(GDN_CONV):
        s = j - GDN_CONV // 2
        xs = x if s == 0 else pltpu.roll(x, (-s) % SUB, axis=0)
        valid = jnp.logical_and(pos + s >= 0, pos + s < rowlen)
        acc = acc + jnp.where(valid, xs, 0.0) * cw_ref[j:j + 1, :]
    y = _silu(acc)
    for h in range(GDN_HEADS):
        for part, scale in ((0, GDN_DK ** -0.5), (1, 1.0)):
            lo = part * GDN_W + h * GDN_DK
            z = y[:, lo:lo + GDN_DK]
            z = z * lax.rsqrt(jnp.sum(z * z, axis=-1, keepdims=True) + EPS)
            qkv_ref[:, lo:lo + GDN_DK] = z * scale
    qkv_ref[:, 2 * GDN_W:] = y[:, 2 * GDN_W:]
    sm = sm_ref[...]
    la_ref[...] = -jnp.exp(arow_ref[...]) * _softplus(sm + dtb_ref[...])
    be_ref[...] = jax.nn.sigmoid(sm)

    tri = tri_ref[...]
    incl = incl_ref[...]
    strict = strict_ref[...]
    eye = eye_ref[...]

    def stack(ref, r0, lo, width):
        return jnp.concatenate([ref[pl.ds(r0, CHUNK), lo + h * width:lo + (h + 1) * width]
                                for h in range(GDN_HEADS)], axis=0)

    def colstack(a, lo):
        return jnp.concatenate([a[:, lo + h:lo + h + 1] for h in range(GDN_HEADS)], axis=0)

    def body(ci, carry):
        c = ci + d * (nchunk - 1 - 2 * ci)
        r0 = pl.multiple_of(c * CHUNK, CHUNK)
        qs = stack(qkv_ref, r0, 0, GDN_DK)
        ks = stack(qkv_ref, r0, GDN_W, GDN_DK)
        vs = stack(qkv_ref, r0, 2 * GDN_W, GDN_DV)
        la = la_ref[pl.ds(r0, CHUNK), :]
        g = _sel_dot(tri, la)
        gtot = jnp.sum(la, axis=0, keepdims=True)
        gcol = colstack(g, SM_A)
        kfcol = colstack(jnp.exp(gtot - g), SM_A)
        egcol = colstack(jnp.exp(g), SM_A)
        bcol = colstack(be_ref[pl.ds(r0, CHUNK), :], SM_B)
        grow = jnp.sum(eye * gcol, axis=0, keepdims=True)
        dec = incl * jnp.exp(jnp.minimum(gcol - grow, 0.0))
        lmat = strict * dec * _bdot_nt(ks, ks) * bcol
        tinv = eye - lmat * lvl_ref[0]
        for lv in range(1, 6):
            tinv = tinv - _bdot(tinv, _bdot(lmat * lvl_ref[lv], tinv))
        rhs = jnp.concatenate([ks * (bcol * egcol), vs * bcol], axis=1)
        sol = _bdot(tinv, rhs)
        w = sol[:, :GDN_DK]
        u0 = sol[:, GDN_DK:]
        qkd = _bdot_nt(qs, ks) * dec
        kd = ks * kfcol
        us = []
        for h in range(GDN_HEADS):
            rows = slice(h * CHUNK, (h + 1) * CHUNK)
            us.append(u0[rows] - _bdot(w[rows], s_ref[h]))
        u = jnp.concatenate(us, axis=0)
        o2 = _bdot(qkd, u)
        egl = jnp.exp(gtot)
        for h in range(GDN_HEADS):
            rows = slice(h * CHUNK, (h + 1) * CHUNK)
            s_h = s_ref[h]
            o_ref[pl.ds(r0, CHUNK), h * GDN_DV:(h + 1) * GDN_DV] = egcol[rows] * _bdot(qs[rows], s_h) + o2[rows]
            s_ref[h] = egl[:, SM_A + h:SM_A + h + 1] * s_h + _bdot(kd[rows].T, us[h])
        return carry

    lax.fori_loop(0, nchunk, body, 0)
    sf_ref[...] = s_ref[...]


def _gdn(proj, conv_w, arow, dtb, s0, *, nseq, n_seq, rowlen):
    t = proj.shape[0]
    nblk = n_seq // SUB
    rbi = _row_block_index(nblk)
    c = _C
    consts = [jnp.asarray(c["incl"], BF16), jnp.asarray(c["incl_bd"], F32), jnp.asarray(c["strict_bd"], F32),
              jnp.asarray(c["levels"], F32), jnp.asarray(c["eye256"], F32)]
    hs = (GDN_HEADS, GDN_DK, GDN_DV)
    return pl.pallas_call(
        functools.partial(_gdn_kernel, nchunk=SUB // CHUNK, rowlen=rowlen),
        grid=(2, nseq, nblk),
        in_specs=[pl.BlockSpec((SUB, 3 * GDN_W), lambda d, b, i: (rbi(d, b, i), P_DQKV // (3 * GDN_W))),
                  pl.BlockSpec((SUB, 128), lambda d, b, i: (rbi(d, b, i), P_SMALL // 128 + d)),
                  pl.BlockSpec((GDN_CONV, 3 * GDN_W), lambda d, b, i: (0, 0)),
                  pl.BlockSpec((None, 1, 128), lambda d, b, i: (d, 0, 0)),
                  pl.BlockSpec((None, 1, 128), lambda d, b, i: (d, 0, 0)),
                  pl.BlockSpec((None, CHUNK, CHUNK), lambda d, b, i: (d, 0, 0)),
                  pl.BlockSpec((None, 256, 256), lambda d, b, i: (d, 0, 0)),
                  pl.BlockSpec((None, 256, 256), lambda d, b, i: (d, 0, 0)),
                  pl.BlockSpec((None, 6, 256, 256), lambda d, b, i: (d, 0, 0, 0)),
                  pl.BlockSpec((256, 256), lambda d, b, i: (0, 0)),
                  pl.BlockSpec((None, None) + hs, lambda d, b, i: (d, b, 0, 0, 0))],
        out_specs=[pl.BlockSpec((None, SUB, GDN_W), lambda d, b, i: (d, rbi(d, b, i), 0)),
                   pl.BlockSpec((None, None) + hs, lambda d, b, i: (d, b, 0, 0, 0))],
        out_shape=[jax.ShapeDtypeStruct((2, t, GDN_W), F32),
                   jax.ShapeDtypeStruct((2, nseq) + hs, F32)],
        scratch_shapes=[pltpu.VMEM(hs, F32), pltpu.VMEM((SUB, 3 * GDN_W), F32),
                        pltpu.VMEM((SUB, 128), F32), pltpu.VMEM((SUB, 128), F32)],
        compiler_params=_cparams(("arbitrary", "arbitrary", "arbitrary")),
        name="gdn",
    )(proj, proj, conv_w, arow, dtb, *consts, s0)


def _s5_param_kernel(lr_ref, li_ref, st_ref, br_ref, bi_ref, lam_ref, bbr_ref, bbi_ref):
    lr = lr_ref[...]
    li = li_ref[...]
    step = jnp.exp(st_ref[...])
    mag = jnp.exp(lr * step)
    lbr = mag * jnp.cos(li * step)
    lbi = mag * jnp.sin(li * step)
    lam_ref[0:1, :] = lbr
    lam_ref[1:2, :] = lbi
    nr = lbr - 1.0
    den = lr * lr + li * li
    cr = (nr * lr + lbi * li) / den
    ci = (lbi * lr - nr * li) / den
    br = br_ref[...]
    bi = bi_ref[...]
    bbr_ref[...] = cr * br - ci * bi
    bbi_ref[...] = cr * bi + ci * br


def _s5_params(lam_re, lam_im, log_step, b_re, b_im):
    n = DEPTH * 2
    lr = lam_re.reshape(n, 1, S5_N)
    li = lam_im.reshape(n, 1, S5_N)
    st = jnp.repeat(log_step.reshape(n, S5_GROUPS), S5_P, axis=-1).reshape(n, 1, S5_N)
    br = b_re.reshape(n, S5_N, S5_CH).transpose(0, 2, 1)
    bi = b_im.reshape(n, S5_N, S5_CH).transpose(0, 2, 1)
    row = pl.BlockSpec((None, 1, S5_N), lambda i: (i, 0, 0))
    mat = pl.BlockSpec((None, S5_CH, S5_N), lambda i: (i, 0, 0))
    return pl.pallas_call(
        _s5_param_kernel,
        grid=(n,),
        in_specs=[row, row, row, mat, mat],
        out_specs=[pl.BlockSpec((None, 2, S5_N), lambda i: (i, 0, 0)), mat, mat],
        out_shape=[jax.ShapeDtypeStruct((n, 2, S5_N), F32),
                   jax.ShapeDtypeStruct((n, S5_CH, S5_N), F32),
                   jax.ShapeDtypeStruct((n, S5_CH, S5_N), F32)],
        compiler_params=_cparams(("parallel",)),
        name="s5_params",
    )(lr, li, st, br, bi)


def _s5_kernel(u_ref, bbd_ref, cre_ref, cim_ref, lam_ref, h0_ref, y_ref, hf_ref, h_ref, bu_ref, hs_ref,
               *, nseq, tb):
    d = pl.program_id(0)
    nt = S5_N // 128

    @pl.when(pl.program_id(1) == 0)
    def _():
        h_ref[...] = h0_ref[...]

    u = u_ref[...].reshape(nseq * tb, S5_W)
    bu = _bdot(u, bbd_ref[...])
    for k in range(2 * nt):
        bu_ref[k] = bu[:, k * 128:(k + 1) * 128]

    def step(tt, carry):
        t = tt + d * (tb - 1 - 2 * tt)
        rows = pl.ds(t, nseq, stride=tb)
        out = []
        for k in range(nt):
            lr = lam_ref[0:1, k * 128:(k + 1) * 128]
            li = lam_ref[1:2, k * 128:(k + 1) * 128]
            hr, hi = carry[k], carry[nt + k]
            nr = lr * hr - li * hi + bu_ref[k, rows, :]
            ni = lr * hi + li * hr + bu_ref[nt + k, rows, :]
            hs_ref[k, rows, :] = nr
            hs_ref[nt + k, rows, :] = ni
            out.append((nr, ni))
        return tuple(o[0] for o in out) + tuple(o[1] for o in out)

    init = tuple(h_ref[:, k * 128:(k + 1) * 128] for k in range(2 * nt))
    fin = lax.fori_loop(0, tb, step, init)
    for k in range(2 * nt):
        h_ref[:, k * 128:(k + 1) * 128] = fin[k]
        hf_ref[:, k * 128:(k + 1) * 128] = fin[k]
    hre = jnp.concatenate([hs_ref[k] for k in range(nt)], axis=1)
    him = jnp.concatenate([hs_ref[nt + k] for k in range(nt)], axis=1)
    y = _bdot(hre, cre_ref[...]) - _bdot(him, cim_ref[...])
    y_ref[...] = y.reshape(nseq, tb, S5_W)


def _s5(proj, bbd, cre, cim, lam, h0, *, nseq, n_seq):
    tb = 64 if nseq > 2 else 256
    tb = min(tb, n_seq)
    nblk = n_seq // tb
    tidx = lambda d, i: i + d * (nblk - 1 - 2 * i)
    return pl.pallas_call(
        functools.partial(_s5_kernel, nseq=nseq, tb=tb),
        grid=(2, nblk),
        in_specs=[pl.BlockSpec((nseq, tb, S5_W), lambda d, i: (0, tidx(d, i), P_SU // S5_W)),
                  pl.BlockSpec((None, S5_W, 2 * S5_N), lambda d, i: (d, 0, 0)),
                  pl.BlockSpec((None, S5_N, S5_W), lambda d, i: (d, 0, 0)),
                  pl.BlockSpec((None, S5_N, S5_W), lambda d, i: (d, 0, 0)),
                  pl.BlockSpec((None, 2, S5_N), lambda d, i: (d, 0, 0)),
                  pl.BlockSpec((None, nseq, 2 * S5_N), lambda d, i: (d, 0, 0))],
        out_specs=[pl.BlockSpec((None, nseq, tb, S5_W), lambda d, i: (d, 0, tidx(d, i), 0)),
                   pl.BlockSpec((None, nseq, 2 * S5_N), lambda d, i: (d, 0, 0))],
        out_shape=[jax.ShapeDtypeStruct((2, nseq, n_seq, S5_W), F32),
                   jax.ShapeDtypeStruct((2, nseq, 2 * S5_N), F32)],
        scratch_shapes=[pltpu.VMEM((nseq, 2 * S5_N), F32), pltpu.VMEM((2 * S5_N // 128, nseq * tb, 128), F32),
                        pltpu.VMEM((2 * S5_N // 128, nseq * tb, 128), F32)],
        compiler_params=_cparams(("arbitrary", "arbitrary")),
        name="s5",
    )(proj.reshape(nseq, n_seq, PROJ_W), bbd, cre, cim, lam, h0)


def _out_kernel(x_ref, mod_ref, ogf_ref, ogb_ref, odf_ref, odb_ref, ysf_ref, ysb_ref, gog_ref, su_ref, dog_ref,
                ind_ref, gw_ref, dw_ref, sd_ref, wglu_ref, bglu_ref, wout_ref, o_ref):
    og = ogf_ref[...] + ogb_ref[...]
    hi, mid, _ = _split3(og * og)
    ms = jnp.dot(hi, ind_ref[...], preferred_element_type=F32) + jnp.dot(mid, ind_ref[...], preferred_element_type=F32)
    og = og * lax.rsqrt(ms + EPS) * gw_ref[...] * _silu(gog_ref[...])
    od = odf_ref[...] + odb_ref[...]
    dog = dog_ref[...]
    parts = []
    for h in range(GDN_HEADS):
        z = od[:, h * GDN_DV:(h + 1) * GDN_DV]
        z = z * lax.rsqrt(jnp.mean(z * z, axis=-1, keepdims=True) + EPS) * dw_ref[...]
        parts.append(z * _silu(dog[:, h * GDN_DV:(h + 1) * GDN_DV]))
    y = su_ref[...] * sd_ref[...] + ysf_ref[...] + ysb_ref[...]
    g = 0.5 * y * (1.0 + jnp.tanh(math.sqrt(2.0 / math.pi) * (y + 0.044715 * (y * y * y))))
    os_ = g * jax.nn.sigmoid(_bdot(g, wglu_ref[...]) + bglu_ref[...])
    out = _bdot(og, wout_ref[0:GLA_W, :]) + _bdot(os_, wout_ref[GLA_W + GDN_W:MIX_W, :])
    for h in range(GDN_HEADS):
        lo = GLA_W + h * GDN_DV
        out = out + _bdot(parts[h], wout_ref[lo:lo + GDN_DV, :])
    o_ref[...] = x_ref[...] + mod_ref[5:6, :] * out


def _mix_out(x, mods, proj, og, od, ys, gla_nw, gdn_nw, s5_d, w_glu, b_glu, w_out, *, layer, n_seq):
    t, d = x.shape
    nb = mods.shape[0]
    tm = _pick_tile(n_seq if nb > 1 else t, 512)
    midx = _mod_index(nb, tm, n_seq)
    ys2 = ys.reshape(2, t, S5_W)

    def dirspec(w, dd):
        return pl.BlockSpec((None, tm, w), lambda i: (dd, i, 0))

    def vec(w):
        return pl.BlockSpec((None, 1, w), lambda i: (layer, 0, 0))

    return pl.pallas_call(
        _out_kernel,
        grid=(t // tm,),
        in_specs=[pl.BlockSpec((tm, d), lambda i: (i, 0)),
                  pl.BlockSpec((None, N_MOD, d), lambda i: (midx(i), 0, 0)),
                  dirspec(GLA_W, 0), dirspec(GLA_W, 1), dirspec(GDN_W, 0), dirspec(GDN_W, 1),
                  dirspec(S5_W, 0), dirspec(S5_W, 1),
                  pl.BlockSpec((tm, GLA_W), lambda i: (i, P_GOG // GLA_W)),
                  pl.BlockSpec((tm, S5_W), lambda i: (i, P_SU // S5_W)),
                  pl.BlockSpec((tm, GDN_W), lambda i: (i, P_DOG // GDN_W)),
                  pl.BlockSpec((GLA_W, GLA_W), lambda i: (0, 0)),
                  vec(GLA_W), vec(GDN_DV), vec(S5_W),
                  pl.BlockSpec((None, S5_W, S5_W), lambda i: (layer, 0, 0)),
                  vec(S5_W),
                  pl.BlockSpec((None, MIX_W, d), lambda i: (layer, 0, 0))],
        out_specs=pl.BlockSpec((tm, d), lambda i: (i, 0)),
        out_shape=jax.ShapeDtypeStruct((t, d), F32),
        compiler_params=_cparams(("parallel",)),
        name="mix_out",
    )(x, mods, og, og, od, od, ys2, ys2, proj, proj, proj,
      jnp.asarray(_C["ind64"], BF16),
      jnp.tile(gla_nw, (1, GLA_HEADS)).reshape(DEPTH, 1, GLA_W), gdn_nw.reshape(DEPTH, 1, GDN_DV),
      s5_d.reshape(DEPTH, 1, S5_W), w_glu, b_glu.reshape(DEPTH, 1, S5_W), w_out)


def _gla_params(gk_up, gk_bias, layer):
    up = jnp.zeros((2, 128, GLA_KW), F32).at[:, :GLA_RANK, :].set(gk_up[layer])
    return up, gk_bias[layer].reshape(2, 1, GLA_KW)


def _gdn_params(a_log, dt_bias, layer):
    arow = jnp.zeros((2, 1, 128), F32).at[:, 0, SM_A:SM_A + GDN_HEADS].set(a_log[layer])
    dtb = jnp.zeros((2, 1, 128), F32).at[:, 0, SM_A:SM_A + GDN_HEADS].set(dt_bias[layer])
    return arow, dtb


def _gla_state_in(s):
    eye = jnp.eye(GLA_HEADS, dtype=F32)
    st = jnp.einsum('bzhde,hg->zbhegd', s.astype(F32), eye)
    return st.reshape(2, s.shape[0], GLA_W, GLA_KW)


def _gla_state_out(st):
    b = st.shape[1]
    s = st.reshape(2, b, GLA_HEADS, GLA_DV, GLA_HEADS, GLA_DK)
    s = jnp.stack([s[:, :, h, :, h, :] for h in range(GLA_HEADS)], axis=2)
    return s.transpose(1, 0, 2, 4, 3)


def _s5_mats(bbr, bbi, c_re, c_im, layer):
    eye = jnp.eye(S5_GROUPS, dtype=F32)

    def place_b(bb):
        bb = bb.reshape(DEPTH, 2, S5_CH, S5_GROUPS, S5_P)[layer]
        return jnp.einsum('zcgp,hg->zhcgp', bb, eye).reshape(2, S5_W, S5_N)

    def place_c(cc):
        return jnp.einsum('zgcp,gh->zgphc', cc[layer].astype(F32), eye).reshape(2, S5_N, S5_W)

    bbd = jnp.concatenate([place_b(bbr), place_b(bbi)], axis=-1)
    return bbd, place_c(c_re), place_c(c_im)


def _trunk(x3, mods, rows, states, p, s5p, w_in_p):
    nseq, n_seq, d = x3.shape
    x = x3.reshape(nseq * n_seq, d)
    st_gla, st_gdn, st_re, st_im = states
    lam, bbr, bbi = s5p
    fin = []
    for l in range(DEPTH):
        m = mods[l]
        x = _ffn(x, m, p['norm_w'], p['ffn_w_gate'], p['ffn_w_up'], p['ffn_w_down'], p['final_norm_w'],
                 layer=l, which=0, n_seq=n_seq, final=False)
        proj = _proj(x, m, p['norm_w'], w_in_p, layer=l, n_seq=n_seq)
        up_p, bias_p = _gla_params(p['gla_gk_up'], p['gla_gk_bias'], l)
        og, gla_f = _gla(proj, up_p, bias_p, _gla_state_in(st_gla[:, l]), nseq=nseq, n_seq=n_seq)
        arow, dtb = _gdn_params(p['gdn_a_log'], p['gdn_dt_bias'], l)
        od, gdn_f = _gdn(proj, p['gdn_conv_w'][l], arow, dtb, st_gdn[:, l].transpose(1, 0, 2, 3, 4),
                         nseq=nseq, n_seq=n_seq, rowlen=n_seq // rows)
        bbd, cre, cim = _s5_mats(bbr, bbi, p['s5_c_re'], p['s5_c_im'], l)
        h0 = jnp.concatenate([st_re[:, l].reshape(nseq, 2, S5_N), st_im[:, l].reshape(nseq, 2, S5_N)],
                             axis=-1).transpose(1, 0, 2)
        ys, s5_f = _s5(proj, bbd, cre, cim, lam.reshape(DEPTH, 2, 2, S5_N)[l], h0, nseq=nseq, n_seq=n_seq)
        x = _mix_out(x, m, proj, og, od, ys, p['gla_norm_w'], p['gdn_norm_w'], p['s5_d'], p['s5_w_glu'],
                     p['s5_b_glu'], p['w_out'], layer=l, n_seq=n_seq)
        x = _ffn(x, m, p['norm_w'], p['ffn_w_gate'], p['ffn_w_up'], p['ffn_w_down'], p['final_norm_w'],
                 layer=l, which=1, n_seq=n_seq, final=(l == DEPTH - 1))
        fin.append((gla_f, gdn_f, s5_f))
    return x.reshape(nseq, n_seq, d), fin


def kernel(x_prompt, x_sample, c, state_gla, state_gdn, state_s5_re, state_s5_im, c_ctx, w_ada, b_ada, norm_w, ffn_w_gate, ffn_w_up, ffn_w_down, w_in, gla_gk_up, gla_gk_bias, gla_norm_w, gdn_conv_w, gdn_a_log, gdn_dt_bias, gdn_norm_w, s5_lam_re, s5_lam_im, s5_log_step, s5_b_re, s5_b_im, s5_c_re, s5_c_im, s5_d, s5_w_glu, s5_b_glu, w_out, final_norm_w):
    p = dict(norm_w=norm_w, ffn_w_gate=ffn_w_gate, ffn_w_up=ffn_w_up, ffn_w_down=ffn_w_down,
             gla_gk_up=gla_gk_up, gla_gk_bias=gla_gk_bias, gla_norm_w=gla_norm_w, gdn_conv_w=gdn_conv_w,
             gdn_a_log=gdn_a_log, gdn_dt_bias=gdn_dt_bias, gdn_norm_w=gdn_norm_w, s5_c_re=s5_c_re,
             s5_c_im=s5_c_im, s5_d=s5_d, s5_w_glu=s5_w_glu, s5_b_glu=s5_b_glu, w_out=w_out,
             final_norm_w=final_norm_w)
    b_ctx = x_prompt.shape[0]
    b_dec = x_sample.shape[0]
    cond8 = jnp.zeros((8, D_MODEL), F32).at[0].set(c_ctx).at[1:1 + b_dec].set(c)
    mods = _ada(cond8, w_ada, b_ada)
    mods_ctx = mods[:, 0:1].reshape(DEPTH, 1, N_MOD, D_MODEL)
    mods_dec = mods[:, 1:1 + b_dec].reshape(DEPTH, b_dec, N_MOD, D_MODEL)
    s5p = _s5_params(s5_lam_re, s5_lam_im, s5_log_step, s5_b_re, s5_b_im)
    w_in_p = _permute_w_in(w_in)

    zero_states = (jnp.zeros((b_ctx, DEPTH, 2, GLA_HEADS, GLA_DK, GLA_DV), F32),
                   jnp.zeros((b_ctx, DEPTH, 2, GDN_HEADS, GDN_DK, GDN_DV), F32),
                   jnp.zeros((b_ctx, DEPTH, 2, S5_GROUPS, S5_P), F32),
                   jnp.zeros((b_ctx, DEPTH, 2, S5_GROUPS, S5_P), F32))
    y_prompt, fin = _trunk(x_prompt, mods_ctx, 1, zero_states, p, s5p, w_in_p)
    new_gla = jnp.stack([_gla_state_out(f[0]) for f in fin], axis=1)
    new_gdn = jnp.stack([f[1].transpose(1, 0, 2, 3, 4) for f in fin], axis=1)
    s5f = jnp.stack([f[2].transpose(1, 0, 2) for f in fin], axis=1)
    new_re = s5f[..., :S5_N].reshape(b_ctx, DEPTH, 2, S5_GROUPS, S5_P)
    new_im = s5f[..., S5_N:].reshape(b_ctx, DEPTH, 2, S5_GROUPS, S5_P)

    rows = x_sample.shape[1] // 64
    y_sample, _ = _trunk(x_sample, mods_dec, rows, (state_gla, state_gdn, state_s5_re, state_s5_im),
                         p, s5p, w_in_p)
    return (y_prompt, y_sample, new_gla.astype(state_gla.dtype), new_gdn.astype(state_gdn.dtype),
            new_re.astype(state_s5_re.dtype), new_im.astype(state_s5_im.dtype))
```

```python
import functools
import math

import numpy as np
import jax
import jax.numpy as jnp
from jax import lax
from jax.experimental import pallas as pl
from jax.experimental.pallas import tpu as pltpu

F32 = jnp.float32
BF16 = jnp.bfloat16

D_MODEL = 1024
DEPTH = 2
FFN_DIM = 2816
N_MOD = 9
EPS = 1e-6
CHUNK = 64
GLA_HEADS, GLA_DK, GLA_DV, GLA_RANK = 4, 32, 64, 16
GLA_GATE_NORM = 16.0
GDN_HEADS, GDN_DK, GDN_DV, GDN_CONV = 4, 128, 128, 5
S5_GROUPS, S5_CH, S5_P = 16, 16, 64
GLA_W = GLA_HEADS * GLA_DV
GLA_KW = GLA_HEADS * GLA_DK
GDN_W = GDN_HEADS * GDN_DV
S5_W = S5_GROUPS * S5_CH
S5_N = S5_GROUPS * S5_P
MIX_W = GLA_W + GDN_W + S5_W

_IN_SIZES = (GLA_KW, GLA_KW, GLA_W, GLA_RANK, GLA_RANK, GLA_W,
             GDN_W, GDN_W, GDN_W, GDN_HEADS, GDN_HEADS, GDN_HEADS, GDN_HEADS, GDN_W, S5_W)
_IN_OFF = tuple(int(v) for v in np.cumsum((0,) + _IN_SIZES))
(_GQ, _GK, _GV, _GLRF, _GLRB, _GOG, _DQ, _DK, _DV, _DAF, _DAB, _DBF, _DBB, _DOG, _SU) = range(15)

P_GQKV = 0
P_GOG = 512
P_SU = 768
P_DOG = 1024
P_DQKV = 1536
P_SMALL = 3072
PROJ_W = 3328
SM_A = GLA_RANK
SM_B = GLA_RANK + GDN_HEADS
SUB = 256
VMEM_LIMIT = 56 * 1024 * 1024


def _cparams(sem):
    return pltpu.CompilerParams(dimension_semantics=sem, vmem_limit_bytes=VMEM_LIMIT)


def _bdot(a, b):
    return jnp.dot(a.astype(BF16), b.astype(BF16), preferred_element_type=F32)


def _bdot_nt(a, b):
    return lax.dot_general(a.astype(BF16), b.astype(BF16), (((1,), (1,)), ((), ())),
                           preferred_element_type=F32)


def _split3(x):
    hi = x.astype(BF16)
    r1 = x - hi.astype(F32)
    mid = r1.astype(BF16)
    lo = (r1 - mid.astype(F32)).astype(BF16)
    return hi, mid, lo


def _sel_dot(m01, x):
    hi, mid, lo = _split3(x)
    return (jnp.dot(m01, hi, preferred_element_type=F32) + jnp.dot(m01, mid, preferred_element_type=F32)
            + jnp.dot(m01, lo, preferred_element_type=F32))


def _silu(x):
    return x * jax.nn.sigmoid(x)


def _softplus(x):
    return jnp.maximum(x, 0.0) + jnp.log1p(jnp.exp(-jnp.abs(x)))


def _log_sigmoid(x):
    return -_softplus(-x)


def _norm_mod(x, nw, scale, shift):
    y = x * lax.rsqrt(jnp.mean(x * x, axis=-1, keepdims=True) + EPS)
    return y * nw * (1.0 + scale) + shift


def _pick_tile(n, cap):
    t = cap
    while n % t:
        t //= 2
    return t


def _ada_kernel(c_ref, w_ref, b_ref, o_ref):
    o_ref[...] = _bdot(_silu(c_ref[...]), w_ref[...]) + b_ref[...]


def _ada(cond8, w_ada, b_ada):
    depth, d, nout = w_ada.shape
    tn = 1024
    return pl.pallas_call(
        _ada_kernel,
        grid=(depth, nout // tn),
        in_specs=[pl.BlockSpec((8, d), lambda l, j: (0, 0)),
                  pl.BlockSpec((None, d, tn), lambda l, j: (l, 0, j)),
                  pl.BlockSpec((None, 1, tn), lambda l, j: (l, 0, j))],
        out_specs=pl.BlockSpec((None, 8, tn), lambda l, j: (l, 0, j)),
        out_shape=jax.ShapeDtypeStruct((depth, 8, nout), F32),
        compiler_params=_cparams(("parallel", "parallel")),
        name="ada",
    )(cond8, w_ada, b_ada.reshape(depth, 1, nout))


def _ffn_kernel(x_ref, mod_ref, nw_ref, wg_ref, wu_ref, wd_ref, fw_ref, o_ref, h_ref, acc_ref,
                *, mod_base, final):
    j = pl.program_id(1)

    @pl.when(j == 0)
    def _():
        m = mod_ref[...]
        h = _norm_mod(x_ref[...], nw_ref[...], m[mod_base + 1:mod_base + 2], m[mod_base:mod_base + 1])
        h_ref[...] = h.astype(BF16)
        acc_ref[...] = jnp.zeros_like(acc_ref)

    h = h_ref[...]
    g = jnp.dot(h, wg_ref[...].astype(BF16), preferred_element_type=F32)
    u = jnp.dot(h, wu_ref[...].astype(BF16), preferred_element_type=F32)
    acc_ref[...] += _bdot(_silu(g) * u, wd_ref[...])

    @pl.when(j == pl.num_programs(1) - 1)
    def _():
        m = mod_ref[...]
        y = x_ref[...] + 0.5 * m[mod_base + 2:mod_base + 3] * acc_ref[...]
        if final:
            y = y * lax.rsqrt(jnp.mean(y * y, axis=-1, keepdims=True) + EPS) * fw_ref[...]
        o_ref[...] = y


def _mod_index(nb, tm, n_seq):
    if nb == 1:
        return lambda i: 0
    return lambda i: (i * tm) // n_seq


def _ffn(x, mods, norm_w, w_gate, w_up, w_down, final_w, *, layer, which, n_seq, final):
    t, d = x.shape
    f = w_gate.shape[-1]
    nb = mods.shape[0]
    tm = _pick_tile(n_seq if nb > 1 else t, 1024)
    tf = 256
    midx = _mod_index(nb, tm, n_seq)
    mod_base = 0 if which == 0 else 6
    return pl.pallas_call(
        functools.partial(_ffn_kernel, mod_base=mod_base, final=final),
        grid=(t // tm, f // tf),
        in_specs=[pl.BlockSpec((tm, d), lambda i, j: (i, 0)),
                  pl.BlockSpec((None, N_MOD, d), lambda i, j: (midx(i), 0, 0)),
                  pl.BlockSpec((None, None, 1, d), lambda i, j: (layer, 2 * which, 0, 0)),
                  pl.BlockSpec((None, None, d, tf), lambda i, j: (layer, which, 0, j)),
                  pl.BlockSpec((None, None, d, tf), lambda i, j: (layer, which, 0, j)),
                  pl.BlockSpec((None, None, tf, d), lambda i, j: (layer, which, j, 0)),
                  pl.BlockSpec((1, d), lambda i, j: (0, 0))],
        out_specs=pl.BlockSpec((tm, d), lambda i, j: (i, 0)),
        out_shape=jax.ShapeDtypeStruct((t, d), F32),
        scratch_shapes=[pltpu.VMEM((tm, d), BF16), pltpu.VMEM((tm, d), F32)],
        compiler_params=_cparams(("parallel", "arbitrary")),
        name="ffn",
    )(x, mods, norm_w.reshape(DEPTH, 3, 1, d), w_gate, w_up, w_down, final_w.reshape(1, d))


def _proj_kernel(x_ref, mod_ref, nw_ref, w_ref, o_ref, h_ref):
    @pl.when(pl.program_id(1) == 0)
    def _():
        m = mod_ref[...]
        h_ref[...] = _norm_mod(x_ref[...], nw_ref[...], m[4:5], m[3:4]).astype(BF16)

    o_ref[...] = jnp.dot(h_ref[...], w_ref[...].astype(BF16), preferred_element_type=F32)


def _proj(x, mods, norm_w, w_in_p, *, layer, n_seq):
    t, d = x.shape
    nb = mods.shape[0]
    tm = _pick_tile(n_seq if nb > 1 else t, 512)
    tn = PROJ_W // 2
    midx = _mod_index(nb, tm, n_seq)
    return pl.pallas_call(
        _proj_kernel,
        grid=(t // tm, PROJ_W // tn),
        in_specs=[pl.BlockSpec((tm, d), lambda i, j: (i, 0)),
                  pl.BlockSpec((None, N_MOD, d), lambda i, j: (midx(i), 0, 0)),
                  pl.BlockSpec((None, None, 1, d), lambda i, j: (layer, 1, 0, 0)),
                  pl.BlockSpec((None, d, tn), lambda i, j: (layer, 0, j))],
        out_specs=pl.BlockSpec((tm, tn), lambda i, j: (i, j)),
        out_shape=jax.ShapeDtypeStruct((t, PROJ_W), F32),
        scratch_shapes=[pltpu.VMEM((tm, d), BF16)],
        compiler_params=_cparams(("parallel", "arbitrary")),
        name="proj",
    )(x, mods, norm_w.reshape(DEPTH, 3, 1, d), w_in_p)


def _permute_w_in(w_in):
    def seg(i):
        return w_in[:, :, _IN_OFF[i]:_IN_OFF[i + 1]]

    def pad(n):
        return jnp.zeros(w_in.shape[:2] + (n,), w_in.dtype)

    sm_pad = 128 - GLA_RANK - 2 * GDN_HEADS
    return jnp.concatenate(
        [seg(_GQ), seg(_GK), seg(_GV), seg(_GOG), seg(_SU), seg(_DOG), seg(_DQ), seg(_DK), seg(_DV),
         seg(_GLRF), seg(_DAF), seg(_DBF), pad(sm_pad),
         seg(_GLRB), seg(_DAB), seg(_DBB), pad(sm_pad)], axis=-1)


def _np_consts():
    t = np.arange(CHUNK)
    lower = (t[:, None] >= t[None, :])
    incl = np.stack([lower, lower.T])
    strict = np.stack([t[:, None] > t[None, :], t[:, None] < t[None, :]])
    eye4 = np.eye(4, dtype=bool)
    blk = np.kron(eye4, np.ones((CHUNK, CHUNK), bool))
    incl_bd = np.stack([np.kron(eye4, incl[d]) for d in range(2)])
    strict_bd = np.stack([np.kron(eye4, strict[d]) for d in range(2)])
    levels = []
    for d in range(2):
        per = []
        for lv in range(6):
            s = 1 << lv
            same = (t[:, None] // (2 * s)) == (t[None, :] // (2 * s))
            hi = (t % (2 * s)) >= s
            m = same & hi[:, None] & (~hi)[None, :]
            if d == 1:
                m = m.T
            per.append(np.kron(eye4, m))
        levels.append(np.stack(per))
    levels = np.stack(levels)
    assert (levels.sum(1) == strict_bd).all() and blk.any()
    gla_mask = np.stack([np.tile(incl[d], (1, 4)) for d in range(2)])
    hm_k = np.kron(eye4, np.ones((1, GLA_DK)))
    hm_v = np.kron(eye4, np.ones((1, GLA_DV)))
    st_mask = np.kron(eye4, np.ones((GLA_DV, GLA_DK)))
    ind64 = np.kron(eye4, np.full((GLA_DV, GLA_DV), 1.0 / GLA_DV))
    return dict(incl=incl, incl_bd=incl_bd, strict_bd=strict_bd, levels=levels, gla_mask=gla_mask,
                hm_k=hm_k, hm_v=hm_v, st_mask=st_mask, ind64=ind64, eye256=np.eye(256))


_C = _np_consts()


def _row_block_index(nblk):
    return lambda d, b, i: b * nblk + i + d * (nblk - 1 - 2 * i)


def _gla_kernel(p_ref, sm_ref, up_ref, bias_ref, tri_ref, mask_ref, hmk_ref, hmv_ref, stm_ref, st0_ref,
                o_ref, stf_ref, st_ref, *, nchunk):
    d = pl.program_id(0)

    @pl.when(pl.program_id(2) == 0)
    def _():
        st_ref[...] = st0_ref[...]

    tri = tri_ref[...]
    mask = mask_ref[...]
    hmk = hmk_ref[...]
    hmv = hmv_ref[...]
    stm = stm_ref[...]

    def body(ci, carry):
        c = ci + d * (nchunk - 1 - 2 * ci)
        r0 = pl.multiple_of(c * CHUNK, CHUNK)
        p = p_ref[pl.ds(r0, CHUNK), :]
        q = p[:, 0:GLA_KW] * (GLA_DK ** -0.5)
        k = p[:, GLA_KW:2 * GLA_KW]
        v = p[:, 2 * GLA_KW:2 * GLA_KW + GLA_W]
        x = _bdot(sm_ref[pl.ds(r0, CHUNK), :], up_ref[...]) + bias_ref[...]
        lg = _log_sigmoid(x) * (1.0 / GLA_GATE_NORM)
        b = _sel_dot(tri, lg)
        btot = jnp.sum(lg, axis=0, keepdims=True)
        qd = q * jnp.exp(b)
        ki = k * jnp.exp(-b)
        kd = k * jnp.exp(btot - b)
        kexp = jnp.concatenate([ki * hmk[h:h + 1] for h in range(GLA_HEADS)], axis=0)
        att = _bdot_nt(qd, kexp) * mask
        vexp = jnp.concatenate([v * hmv[h:h + 1] for h in range(GLA_HEADS)], axis=0)
        st = st_ref[...]
        o_ref[pl.ds(r0, CHUNK), :] = _bdot(att, vexp) + _bdot_nt(qd, st)
        st_ref[...] = st * jnp.exp(btot) + _bdot(v.T, kd) * stm
        return carry

    lax.fori_loop(0, nchunk, body, 0)
    stf_ref[...] = st_ref[...]


def _gla(proj, up_p, bias_p, st0, *, nseq, n_seq):
    t = proj.shape[0]
    nblk = n_seq // SUB
    rbi = _row_block_index(nblk)
    c = _C
    consts = [jnp.asarray(c["incl"], BF16), jnp.asarray(c["gla_mask"], F32), jnp.asarray(c["hm_k"], F32),
              jnp.asarray(c["hm_v"], F32), jnp.asarray(c["st_mask"], F32)]
    return pl.pallas_call(
        functools.partial(_gla_kernel, nchunk=SUB // CHUNK),
        grid=(2, nseq, nblk),
        in_specs=[pl.BlockSpec((SUB, 512), lambda d, b, i: (rbi(d, b, i), 0)),
                  pl.BlockSpec((SUB, 128), lambda d, b, i: (rbi(d, b, i), P_SMALL // 128 + d)),
                  pl.BlockSpec((None, 128, GLA_KW), lambda d, b, i: (d, 0, 0)),
                  pl.BlockSpec((None, 1, GLA_KW), lambda d, b, i: (d, 0, 0)),
                  pl.BlockSpec((None, CHUNK, CHUNK), lambda d, b, i: (d, 0, 0)),
                  pl.BlockSpec((None, CHUNK, 256), lambda d, b, i: (d, 0, 0)),
                  pl.BlockSpec((GLA_HEADS, GLA_KW), lambda d, b, i: (0, 0)),
                  pl.BlockSpec((GLA_HEADS, GLA_W), lambda d, b, i: (0, 0)),
                  pl.BlockSpec((GLA_W, GLA_KW), lambda d, b, i: (0, 0)),
                  pl.BlockSpec((None, None, GLA_W, GLA_KW), lambda d, b, i: (d, b, 0, 0))],
        out_specs=[pl.BlockSpec((None, SUB, GLA_W), lambda d, b, i: (d, rbi(d, b, i), 0)),
                   pl.BlockSpec((None, None, GLA_W, GLA_KW), lambda d, b, i: (d, b, 0, 0))],
        out_shape=[jax.ShapeDtypeStruct((2, t, GLA_W), F32),
                   jax.ShapeDtypeStruct((2, nseq, GLA_W, GLA_KW), F32)],
        scratch_shapes=[pltpu.VMEM((GLA_W, GLA_KW), F32)],
        compiler_params=_cparams(("arbitrary", "arbitrary", "arbitrary")),
        name="gla",
    )(proj, proj, up_p, bias_p, *consts, st0)


def _gdn_kernel(p_ref, sm_ref, cw_ref, arow_ref, dtb_ref, tri_ref, incl_ref, strict_ref, lvl_ref, eye_ref,
                s0_ref, o_ref, sf_ref, s_ref, qkv_ref, la_ref, be_ref, w_sc, u0_sc, qkd_sc, qe_sc, kdt_sc, egl_sc,
                *, nchunk, rowlen):
    d = pl.program_id(0)

    @pl.when(pl.program_id(2) == 0)
    def _():
        s_ref[...] = s0_ref[...]

    x = p_ref[...]
    pos = lax.broadcasted_iota(jnp.int32, (SUB, 1), 0) % rowlen
    acc = jnp.zeros_like(x)
    for j in range(GDN_CONV):
        s = j - GDN_CONV // 2
        xs = x if s == 0 else pltpu.roll(x, (-s) % SUB, axis=0)
        valid = jnp.logical_and(pos + s >= 0, pos + s < rowlen)
        acc = acc + jnp.where(valid, xs, 0.0) * cw_ref[j:j + 1, :]
    y = _silu(acc)
    for h in range(GDN_HEADS):
        for part, scale in ((0, GDN_DK ** -0.5), (1, 1.0)):
            lo = part * GDN_W + h * GDN_DK
            z = y[:, lo:lo + GDN_DK]
            z = z * lax.rsqrt(jnp.sum(z * z, axis=-1, keepdims=True) + EPS)
            qkv_ref[:, lo:lo + GDN_DK] = z * scale
    qkv_ref[:, 2 * GDN_W:] = y[:, 2 * GDN_W:]
    sm = sm_ref[...]
    la_ref[...] = -jnp.exp(arow_ref[...]) * _softplus(sm + dtb_ref[...])
    be_ref[...] = jax.nn.sigmoid(sm)

    tri = tri_ref[...]
    incl = incl_ref[...]
    strict = strict_ref[...]
    eye = eye_ref[...]

    def stack(ref, r0, lo, width):
        return jnp.concatenate([ref[pl.ds(r0, CHUNK), lo + h * width:lo + (h + 1) * width]
                                for h in range(GDN_HEADS)], axis=0)

    def colstack(a, lo):
        return jnp.concatenate([a[:, lo + h:lo + h + 1] for h in range(GDN_HEADS)], axis=0)

    lvl = [lvl_ref[lv] for lv in range(6)]

    loc = []
    for c in range(nchunk):
        r0 = c * CHUNK
        qs = stack(qkv_ref, r0, 0, GDN_DK)
        ks = stack(qkv_ref, r0, GDN_W, GDN_DK)
        vs = stack(qkv_ref, r0, 2 * GDN_W, GDN_DV)
        la = la_ref[pl.ds(r0, CHUNK), :]
        g = _sel_dot(tri, la)
        gtot = jnp.sum(la, axis=0, keepdims=True)
        gcol = colstack(g, SM_A)
        kfcol = colstack(jnp.exp(gtot - g), SM_A)
        egcol = colstack(jnp.exp(g), SM_A)
        bcol = colstack(be_ref[pl.ds(r0, CHUNK), :], SM_B)
        grow = jnp.sum(eye * gcol, axis=0, keepdims=True)
        dec = incl * jnp.exp(jnp.minimum(gcol - grow, 0.0))
        ksb = ks.astype(BF16)
        lmat = strict * dec * _bdot_nt(ksb, ksb) * bcol
        rhs = jnp.concatenate([ks * (bcol * egcol), vs * bcol], axis=1).astype(BF16)
        qkd_sc[c] = (_bdot_nt(qs, ksb) * dec).astype(BF16)
        qe_sc[c] = (qs * egcol).astype(BF16)
        kd = ks * kfcol
        for h in range(GDN_HEADS):
            kdt_sc[c * GDN_HEADS + h] = kd[h * CHUNK:(h + 1) * CHUNK].T.astype(BF16)
        egl_sc[c] = jnp.broadcast_to(jnp.exp(gtot), (8, 128))
        loc.append((lmat.astype(BF16), rhs))
    tinv = [eye - (lm * lvl[0]).astype(F32) for lm, _ in loc]
    for lv in range(1, 6):
        tb = [t_.astype(BF16) for t_ in tinv]
        cs = [jnp.dot(lm * lvl[lv], t_, preferred_element_type=F32) for (lm, _), t_ in zip(loc, tb)]
        tinv = [t32 - jnp.dot(t_, c_.astype(BF16), preferred_element_type=F32)
                for t32, t_, c_ in zip(tinv, tb, cs)]
    for c in range(nchunk):
        sol = jnp.dot(tinv[c].astype(BF16), loc[c][1], preferred_element_type=F32)
        w_sc[c] = sol[:, :GDN_DK].astype(BF16)
        u0_sc[c] = sol[:, GDN_DK:]

    s_cur = [s_ref[h] for h in range(GDN_HEADS)]
    for ci in range(nchunk):
        c = ci + d * (nchunk - 1 - 2 * ci)
        r0 = pl.multiple_of(c * CHUNK, CHUNK)
        sb = [s_.astype(BF16) for s_ in s_cur]
        us = []
        for h in range(GDN_HEADS):
            rows = pl.ds(h * CHUNK, CHUNK)
            us.append(u0_sc[c, rows, :] - jnp.dot(w_sc[c, rows, :], sb[h], preferred_element_type=F32))
        ub = [u_.astype(BF16) for u_ in us]
        o2 = jnp.dot(qkd_sc[c], jnp.concatenate(ub, axis=0), preferred_element_type=F32)
        egl = egl_sc[c]
        for h in range(GDN_HEADS):
            rows = pl.ds(h * CHUNK, CHUNK)
            o_ref[pl.ds(r0, CHUNK), h * GDN_DV:(h + 1) * GDN_DV] = (
                jnp.dot(qe_sc[c, rows, :], sb[h], preferred_element_type=F32) + o2[h * CHUNK:(h + 1) * CHUNK])
            s_cur[h] = (egl[0:1, SM_A + h:SM_A + h + 1] * s_cur[h]
                        + jnp.dot(kdt_sc[c * GDN_HEADS + h], ub[h], preferred_element_type=F32))
    for h in range(GDN_HEADS):
        s_ref[h] = s_cur[h]
    sf_ref[...] = s_ref[...]


def _gdn(proj, conv_w, arow, dtb, s0, *, nseq, n_seq, rowlen):
    t = proj.shape[0]
    nblk = n_seq // SUB
    rbi = _row_block_index(nblk)
    c = _C
    consts = [jnp.asarray(c["incl"], BF16), jnp.asarray(c["incl_bd"], F32), jnp.asarray(c["strict_bd"], F32),
              jnp.asarray(c["levels"], BF16), jnp.asarray(c["eye256"], F32)]
    hs = (GDN_HEADS, GDN_DK, GDN_DV)
    nck = SUB // CHUNK
    return pl.pallas_call(
        functools.partial(_gdn_kernel, nchunk=nck, rowlen=rowlen),
        grid=(2, nseq, nblk),
        in_specs=[pl.BlockSpec((SUB, 3 * GDN_W), lambda d, b, i: (rbi(d, b, i), P_DQKV // (3 * GDN_W))),
                  pl.BlockSpec((SUB, 128), lambda d, b, i: (rbi(d, b, i), P_SMALL // 128 + d)),
                  pl.BlockSpec((GDN_CONV, 3 * GDN_W), lambda d, b, i: (0, 0)),
                  pl.BlockSpec((None, 1, 128), lambda d, b, i: (d, 0, 0)),
                  pl.BlockSpec((None, 1, 128), lambda d, b, i: (d, 0, 0)),
                  pl.BlockSpec((None, CHUNK, CHUNK), lambda d, b, i: (d, 0, 0)),
                  pl.BlockSpec((None, 256, 256), lambda d, b, i: (d, 0, 0)),
                  pl.BlockSpec((None, 256, 256), lambda d, b, i: (d, 0, 0)),
                  pl.BlockSpec((None, 6, 256, 256), lambda d, b, i: (d, 0, 0, 0)),
                  pl.BlockSpec((256, 256), lambda d, b, i: (0, 0)),
                  pl.BlockSpec((None, None) + hs, lambda d, b, i: (d, b, 0, 0, 0))],
        out_specs=[pl.BlockSpec((None, SUB, GDN_W), lambda d, b, i: (d, rbi(d, b, i), 0)),
                   pl.BlockSpec((None, None) + hs, lambda d, b, i: (d, b, 0, 0, 0))],
        out_shape=[jax.ShapeDtypeStruct((2, t, GDN_W), F32),
                   jax.ShapeDtypeStruct((2, nseq) + hs, F32)],
        scratch_shapes=[pltpu.VMEM(hs, F32), pltpu.VMEM((SUB, 3 * GDN_W), F32),
                        pltpu.VMEM((SUB, 128), F32), pltpu.VMEM((SUB, 128), F32),
                        pltpu.VMEM((nck, 256, GDN_DK), BF16), pltpu.VMEM((nck, 256, GDN_DV), F32),
                        pltpu.VMEM((nck, 256, 256), BF16), pltpu.VMEM((nck, 256, GDN_DK), BF16),
                        pltpu.VMEM((nck * GDN_HEADS, GDN_DK, CHUNK), BF16), pltpu.VMEM((nck, 8, 128), F32)],
        compiler_params=_cparams(("arbitrary", "arbitrary", "arbitrary")),
        name="gdn",
    )(proj, proj, conv_w, arow, dtb, *consts, s0)


def _s5_param_kernel(lr_ref, li_ref, st_ref, br_ref, bi_ref, lam_ref, bbr_ref, bbi_ref):
    lr = lr_ref[...]
    li = li_ref[...]
    step = jnp.exp(st_ref[...])
    mag = jnp.exp(lr * step)
    lbr = mag * jnp.cos(li * step)
    lbi = mag * jnp.sin(li * step)
    lam_ref[0:1, :] = lbr
    lam_ref[1:2, :] = lbi
    nr = lbr - 1.0
    den = lr * lr + li * li
    cr = (nr * lr + lbi * li) / den
    ci = (lbi * lr - nr * li) / den
    br = br_ref[...]
    bi = bi_ref[...]
    bbr_ref[...] = cr * br - ci * bi
    bbi_ref[...] = cr * bi + ci * br


def _s5_params(lam_re, lam_im, log_step, b_re, b_im):
    n = DEPTH * 2
    lr = lam_re.reshape(n, 1, S5_N)
    li = lam_im.reshape(n, 1, S5_N)
    st = jnp.repeat(log_step.reshape(n, S5_GROUPS), S5_P, axis=-1).reshape(n, 1, S5_N)
    br = b_re.reshape(n, S5_N, S5_CH).transpose(0, 2, 1)
    bi = b_im.reshape(n, S5_N, S5_CH).transpose(0, 2, 1)
    row = pl.BlockSpec((None, 1, S5_N), lambda i: (i, 0, 0))
    mat = pl.BlockSpec((None, S5_CH, S5_N), lambda i: (i, 0, 0))
    return pl.pallas_call(
        _s5_param_kernel,
        grid=(n,),
        in_specs=[row, row, row, mat, mat],
        out_specs=[pl.BlockSpec((None, 2, S5_N), lambda i: (i, 0, 0)), mat, mat],
        out_shape=[jax.ShapeDtypeStruct((n, 2, S5_N), F32),
                   jax.ShapeDtypeStruct((n, S5_CH, S5_N), F32),
                   jax.ShapeDtypeStruct((n, S5_CH, S5_N), F32)],
        compiler_params=_cparams(("parallel",)),
        name="s5_params",
    )(lr, li, st, br, bi)


def _s5_kernel(u_ref, bbd_ref, cre_ref, cim_ref, lam_ref, h0_ref, y_ref, hf_ref, h_ref, bu_ref, hs_ref,
               *, nseq, tb):
    d = pl.program_id(0)
    nt = S5_N // 128

    @pl.when(pl.program_id(1) == 0)
    def _():
        h_ref[...] = h0_ref[...]

    u = u_ref[...].reshape(nseq * tb, S5_W)
    bu = _bdot(u, bbd_ref[...])
    for k in range(2 * nt):
        bu_ref[k] = bu[:, k * 128:(k + 1) * 128]

    def step(tt, carry):
        t = tt + d * (tb - 1 - 2 * tt)
        rows = pl.ds(t, nseq, stride=tb)
        out = []
        for k in range(nt):
            lr = lam_ref[0:1, k * 128:(k + 1) * 128]
            li = lam_ref[1:2, k * 128:(k + 1) * 128]
            hr, hi = carry[k], carry[nt + k]
            nr = lr * hr - li * hi + bu_ref[k, rows, :]
            ni = lr * hi + li * hr + bu_ref[nt + k, rows, :]
            hs_ref[k, rows, :] = nr
            hs_ref[nt + k, rows, :] = ni
            out.append((nr, ni))
        return tuple(o[0] for o in out) + tuple(o[1] for o in out)

    init = tuple(h_ref[:, k * 128:(k + 1) * 128] for k in range(2 * nt))
    fin = lax.fori_loop(0, tb, step, init)
    for k in range(2 * nt):
        h_ref[:, k * 128:(k + 1) * 128] = fin[k]
        hf_ref[:, k * 128:(k + 1) * 128] = fin[k]
    hre = jnp.concatenate([hs_ref[k] for k in range(nt)], axis=1)
    him = jnp.concatenate([hs_ref[nt + k] for k in range(nt)], axis=1)
    y = _bdot(hre, cre_ref[...]) - _bdot(him, cim_ref[...])
    y_ref[...] = y.reshape(nseq, tb, S5_W)


def _s5(proj, bbd, cre, cim, lam, h0, *, nseq, n_seq):
    tb = 64 if nseq > 2 else 256
    tb = min(tb, n_seq)
    nblk = n_seq // tb
    tidx = lambda d, i: i + d * (nblk - 1 - 2 * i)
    return pl.pallas_call(
        functools.partial(_s5_kernel, nseq=nseq, tb=tb),
        grid=(2, nblk),
        in_specs=[pl.BlockSpec((nseq, tb, S5_W), lambda d, i: (0, tidx(d, i), P_SU // S5_W)),
                  pl.BlockSpec((None, S5_W, 2 * S5_N), lambda d, i: (d, 0, 0)),
                  pl.BlockSpec((None, S5_N, S5_W), lambda d, i: (d, 0, 0)),
                  pl.BlockSpec((None, S5_N, S5_W), lambda d, i: (d, 0, 0)),
                  pl.BlockSpec((None, 2, S5_N), lambda d, i: (d, 0, 0)),
                  pl.BlockSpec((None, nseq, 2 * S5_N), lambda d, i: (d, 0, 0))],
        out_specs=[pl.BlockSpec((None, nseq, tb, S5_W), lambda d, i: (d, 0, tidx(d, i), 0)),
                   pl.BlockSpec((None, nseq, 2 * S5_N), lambda d, i: (d, 0, 0))],
        out_shape=[jax.ShapeDtypeStruct((2, nseq, n_seq, S5_W), F32),
                   jax.ShapeDtypeStruct((2, nseq, 2 * S5_N), F32)],
        scratch_shapes=[pltpu.VMEM((nseq, 2 * S5_N), F32), pltpu.VMEM((2 * S5_N // 128, nseq * tb, 128), F32),
                        pltpu.VMEM((2 * S5_N // 128, nseq * tb, 128), F32)],
        compiler_params=_cparams(("arbitrary", "arbitrary")),
        name="s5",
    )(proj.reshape(nseq, n_seq, PROJ_W), bbd, cre, cim, lam, h0)


def _out_kernel(x_ref, mod_ref, ogf_ref, ogb_ref, odf_ref, odb_ref, ysf_ref, ysb_ref, gog_ref, su_ref, dog_ref,
                ind_ref, gw_ref, dw_ref, sd_ref, wglu_ref, bglu_ref, wout_ref, o_ref):
    og = ogf_ref[...] + ogb_ref[...]
    hi, mid, _ = _split3(og * og)
    ms = jnp.dot(hi, ind_ref[...], preferred_element_type=F32) + jnp.dot(mid, ind_ref[...], preferred_element_type=F32)
    og = og * lax.rsqrt(ms + EPS) * gw_ref[...] * _silu(gog_ref[...])
    od = odf_ref[...] + odb_ref[...]
    dog = dog_ref[...]
    parts = []
    for h in range(GDN_HEADS):
        z = od[:, h * GDN_DV:(h + 1) * GDN_DV]
        z = z * lax.rsqrt(jnp.mean(z * z, axis=-1, keepdims=True) + EPS) * dw_ref[...]
        parts.append(z * _silu(dog[:, h * GDN_DV:(h + 1) * GDN_DV]))
    y = su_ref[...] * sd_ref[...] + ysf_ref[...] + ysb_ref[...]
    g = 0.5 * y * (1.0 + jnp.tanh(math.sqrt(2.0 / math.pi) * (y + 0.044715 * (y * y * y))))
    os_ = g * jax.nn.sigmoid(_bdot(g, wglu_ref[...]) + bglu_ref[...])
    out = _bdot(og, wout_ref[0:GLA_W, :]) + _bdot(os_, wout_ref[GLA_W + GDN_W:MIX_W, :])
    for h in range(GDN_HEADS):
        lo = GLA_W + h * GDN_DV
        out = out + _bdot(parts[h], wout_ref[lo:lo + GDN_DV, :])
    o_ref[...] = x_ref[...] + mod_ref[5:6, :] * out


def _mix_out(x, mods, proj, og, od, ys, gla_nw, gdn_nw, s5_d, w_glu, b_glu, w_out, *, layer, n_seq):
    t, d = x.shape
    nb = mods.shape[0]
    tm = _pick_tile(n_seq if nb > 1 else t, 512)
    midx = _mod_index(nb, tm, n_seq)
    ys2 = ys.reshape(2, t, S5_W)

    def dirspec(w, dd):
        return pl.BlockSpec((None, tm, w), lambda i: (dd, i, 0))

    def vec(w):
        return pl.BlockSpec((None, 1, w), lambda i: (layer, 0, 0))

    return pl.pallas_call(
        _out_kernel,
        grid=(t // tm,),
        in_specs=[pl.BlockSpec((tm, d), lambda i: (i, 0)),
                  pl.BlockSpec((None, N_MOD, d), lambda i: (midx(i), 0, 0)),
                  dirspec(GLA_W, 0), dirspec(GLA_W, 1), dirspec(GDN_W, 0), dirspec(GDN_W, 1),
                  dirspec(S5_W, 0), dirspec(S5_W, 1),
                  pl.BlockSpec((tm, GLA_W), lambda i: (i, P_GOG // GLA_W)),
                  pl.BlockSpec((tm, S5_W), lambda i: (i, P_SU // S5_W)),
                  pl.BlockSpec((tm, GDN_W), lambda i: (i, P_DOG // GDN_W)),
                  pl.BlockSpec((GLA_W, GLA_W), lambda i: (0, 0)),
                  vec(GLA_W), vec(GDN_DV), vec(S5_W),
                  pl.BlockSpec((None, S5_W, S5_W), lambda i: (layer, 0, 0)),
                  vec(S5_W),
                  pl.BlockSpec((None, MIX_W, d), lambda i: (layer, 0, 0))],
        out_specs=pl.BlockSpec((tm, d), lambda i: (i, 0)),
        out_shape=jax.ShapeDtypeStruct((t, d), F32),
        compiler_params=_cparams(("parallel",)),
        name="mix_out",
    )(x, mods, og, og, od, od, ys2, ys2, proj, proj, proj,
      jnp.asarray(_C["ind64"], BF16),
      jnp.tile(gla_nw, (1, GLA_HEADS)).reshape(DEPTH, 1, GLA_W), gdn_nw.reshape(DEPTH, 1, GDN_DV),
      s5_d.reshape(DEPTH, 1, S5_W), w_glu, b_glu.reshape(DEPTH, 1, S5_W), w_out)


def _gla_params(gk_up, gk_bias, layer):
    up = jnp.zeros((2, 128, GLA_KW), F32).at[:, :GLA_RANK, :].set(gk_up[layer])
    return up, gk_bias[layer].reshape(2, 1, GLA_KW)


def _gdn_params(a_log, dt_bias, layer):
    arow = jnp.zeros((2, 1, 128), F32).at[:, 0, SM_A:SM_A + GDN_HEADS].set(a_log[layer])
    dtb = jnp.zeros((2, 1, 128), F32).at[:, 0, SM_A:SM_A + GDN_HEADS].set(dt_bias[layer])
    return arow, dtb


def _gla_state_in(s):
    eye = jnp.eye(GLA_HEADS, dtype=F32)
    st = jnp.einsum('bzhde,hg->zbhegd', s.astype(F32), eye)
    return st.reshape(2, s.shape[0], GLA_W, GLA_KW)


def _gla_state_out(st):
    b = st.shape[1]
    s = st.reshape(2, b, GLA_HEADS, GLA_DV, GLA_HEADS, GLA_DK)
    s = jnp.stack([s[:, :, h, :, h, :] for h in range(GLA_HEADS)], axis=2)
    return s.transpose(1, 0, 2, 4, 3)


def _s5_mats(bbr, bbi, c_re, c_im, layer):
    eye = jnp.eye(S5_GROUPS, dtype=F32)

    def place_b(bb):
        bb = bb.reshape(DEPTH, 2, S5_CH, S5_GROUPS, S5_P)[layer]
        return jnp.einsum('zcgp,hg->zhcgp', bb, eye).reshape(2, S5_W, S5_N)

    def place_c(cc):
        return jnp.einsum('zgcp,gh->zgphc', cc[layer].astype(F32), eye).reshape(2, S5_N, S5_W)

    bbd = jnp.concatenate([place_b(bbr), place_b(bbi)], axis=-1)
    return bbd, place_c(c_re), place_c(c_im)


def _trunk(x3, mods, rows, states, p, s5p, w_in_p):
    nseq, n_seq, d = x3.shape
    x = x3.reshape(nseq * n_seq, d)
    st_gla, st_gdn, st_re, st_im = states
    lam, bbr, bbi = s5p
    fin = []
    for l in range(DEPTH):
        m = mods[l]
        x = _ffn(x, m, p['norm_w'], p['ffn_w_gate'], p['ffn_w_up'], p['ffn_w_down'], p['final_norm_w'],
                 layer=l, which=0, n_seq=n_seq, final=False)
        proj = _proj(x, m, p['norm_w'], w_in_p, layer=l, n_seq=n_seq)
        up_p, bias_p = _gla_params(p['gla_gk_up'], p['gla_gk_bias'], l)
        og, gla_f = _gla(proj, up_p, bias_p, _gla_state_in(st_gla[:, l]), nseq=nseq, n_seq=n_seq)
        arow, dtb = _gdn_params(p['gdn_a_log'], p['gdn_dt_bias'], l)
        od, gdn_f = _gdn(proj, p['gdn_conv_w'][l], arow, dtb, st_gdn[:, l].transpose(1, 0, 2, 3, 4),
                         nseq=nseq, n_seq=n_seq, rowlen=n_seq // rows)
        bbd, cre, cim = _s5_mats(bbr, bbi, p['s5_c_re'], p['s5_c_im'], l)
        h0 = jnp.concatenate([st_re[:, l].reshape(nseq, 2, S5_N), st_im[:, l].reshape(nseq, 2, S5_N)],
                             axis=-1).transpose(1, 0, 2)
        ys, s5_f = _s5(proj, bbd, cre, cim, lam.reshape(DEPTH, 2, 2, S5_N)[l], h0, nseq=nseq, n_seq=n_seq)
        x = _mix_out(x, m, proj, og, od, ys, p['gla_norm_w'], p['gdn_norm_w'], p['s5_d'], p['s5_w_glu'],
                     p['s5_b_glu'], p['w_out'], layer=l, n_seq=n_seq)
        x = _ffn(x, m, p['norm_w'], p['ffn_w_gate'], p['ffn_w_up'], p['ffn_w_down'], p['final_norm_w'],
                 layer=l, which=1, n_seq=n_seq, final=(l == DEPTH - 1))
        fin.append((gla_f, gdn_f, s5_f))
    return x.reshape(nseq, n_seq, d), fin


def kernel(x_prompt, x_sample, c, state_gla, state_gdn, state_s5_re, state_s5_im, c_ctx, w_ada, b_ada, norm_w, ffn_w_gate, ffn_w_up, ffn_w_down, w_in, gla_gk_up, gla_gk_bias, gla_norm_w, gdn_conv_w, gdn_a_log, gdn_dt_bias, gdn_norm_w, s5_lam_re, s5_lam_im, s5_log_step, s5_b_re, s5_b_im, s5_c_re, s5_c_im, s5_d, s5_w_glu, s5_b_glu, w_out, final_norm_w):
    p = dict(norm_w=norm_w, ffn_w_gate=ffn_w_gate, ffn_w_up=ffn_w_up, ffn_w_down=ffn_w_down,
             gla_gk_up=gla_gk_up, gla_gk_bias=gla_gk_bias, gla_norm_w=gla_norm_w, gdn_conv_w=gdn_conv_w,
             gdn_a_log=gdn_a_log, gdn_dt_bias=gdn_dt_bias, gdn_norm_w=gdn_norm_w, s5_c_re=s5_c_re,
             s5_c_im=s5_c_im, s5_d=s5_d, s5_w_glu=s5_w_glu, s5_b_glu=s5_b_glu, w_out=w_out,
             final_norm_w=final_norm_w)
    b_ctx = x_prompt.shape[0]
    b_dec = x_sample.shape[0]
    cond8 = jnp.zeros((8, D_MODEL), F32).at[0].set(c_ctx).at[1:1 + b_dec].set(c)
    mods = _ada(cond8, w_ada, b_ada)
    mods_ctx = mods[:, 0:1].reshape(DEPTH, 1, N_MOD, D_MODEL)
    mods_dec = mods[:, 1:1 + b_dec].reshape(DEPTH, b_dec, N_MOD, D_MODEL)
    s5p = _s5_params(s5_lam_re, s5_lam_im, s5_log_step, s5_b_re, s5_b_im)
    w_in_p = _permute_w_in(w_in)

    zero_states = (jnp.zeros((b_ctx, DEPTH, 2, GLA_HEADS, GLA_DK, GLA_DV), F32),
                   jnp.zeros((b_ctx, DEPTH, 2, GDN_HEADS, GDN_DK, GDN_DV), F32),
                   jnp.zeros((b_ctx, DEPTH, 2, S5_GROUPS, S5_P), F32),
                   jnp.zeros((b_ctx, DEPTH, 2, S5_GROUPS, S5_P), F32))
    y_prompt, fin = _trunk(x_prompt, mods_ctx, 1, zero_states, p, s5p, w_in_p)
    new_gla = jnp.stack([_gla_state_out(f[0]) for f in fin], axis=1)
    new_gdn = jnp.stack([f[1].transpose(1, 0, 2, 3, 4) for f in fin], axis=1)
    s5f = jnp.stack([f[2].transpose(1, 0, 2) for f in fin], axis=1)
    new_re = s5f[..., :S5_N].reshape(b_ctx, DEPTH, 2, S5_GROUPS, S5_P)
    new_im = s5f[..., S5_N:].reshape(b_ctx, DEPTH, 2, S5_GROUPS, S5_P)

    rows = x_sample.shape[1] // 64
    y_sample, _ = _trunk(x_sample, mods_dec, rows, (state_gla, state_gdn, state_s5_re, state_s5_im),
                         p, s5p, w_in_p)
    return (y_prompt, y_sample, new_gla.astype(state_gla.dtype), new_gdn.astype(state_gdn.dtype),
            new_re.astype(state_s5_re.dtype), new_im.astype(state_s5_im.dtype))
```

```python
import functools
import math

import numpy as np
import jax
import jax.numpy as jnp
from jax import lax
from jax.experimental import pallas as pl
from jax.experimental.pallas import tpu as pltpu

F32 = jnp.float32
BF16 = jnp.bfloat16

D_MODEL = 1024
DEPTH = 2
FFN_DIM = 2816
N_MOD = 9
EPS = 1e-6
CHUNK = 64
GLA_HEADS, GLA_DK, GLA_DV, GLA_RANK = 4, 32, 64, 16
GLA_GATE_NORM = 16.0
GDN_HEADS, GDN_DK, GDN_DV, GDN_CONV = 4, 128, 128, 5
S5_GROUPS, S5_CH, S5_P = 16, 16, 64
GLA_W = GLA_HEADS * GLA_DV
GLA_KW = GLA_HEADS * GLA_DK
GDN_W = GDN_HEADS * GDN_DV
S5_W = S5_GROUPS * S5_CH
S5_N = S5_GROUPS * S5_P
MIX_W = GLA_W + GDN_W + S5_W

_IN_SIZES = (GLA_KW, GLA_KW, GLA_W, GLA_RANK, GLA_RANK, GLA_W,
             GDN_W, GDN_W, GDN_W, GDN_HEADS, GDN_HEADS, GDN_HEADS, GDN_HEADS, GDN_W, S5_W)
_IN_OFF = tuple(int(v) for v in np.cumsum((0,) + _IN_SIZES))
(_GQ, _GK, _GV, _GLRF, _GLRB, _GOG, _DQ, _DK, _DV, _DAF, _DAB, _DBF, _DBB, _DOG, _SU) = range(15)

P_DQKV = 0
P_SMALL = 1536
P_GQK = 1792
P_GV = 2048
P_GOG = 2304
P_DOG = 2560
P_SU = 3072
PROJ_W = 3328
PROJ_TN = PROJ_W // 2
SM_A = GLA_RANK
SM_B = GLA_RANK + GDN_HEADS
SUB = 256
VMEM_LIMIT = 56 * 1024 * 1024


def _cparams(sem):
    return pltpu.CompilerParams(dimension_semantics=sem, vmem_limit_bytes=VMEM_LIMIT)


def _bdot(a, b):
    return jnp.dot(a.astype(BF16), b.astype(BF16), preferred_element_type=F32)


def _bdot_nt(a, b):
    return lax.dot_general(a.astype(BF16), b.astype(BF16), (((1,), (1,)), ((), ())),
                           preferred_element_type=F32)


def _split3(x):
    hi = x.astype(BF16)
    r1 = x - hi.astype(F32)
    mid = r1.astype(BF16)
    lo = (r1 - mid.astype(F32)).astype(BF16)
    return hi, mid, lo


def _sel_dot(m01, x):
    hi, mid, lo = _split3(x)
    return (jnp.dot(m01, hi, preferred_element_type=F32) + jnp.dot(m01, mid, preferred_element_type=F32)
            + jnp.dot(m01, lo, preferred_element_type=F32))


def _silu(x):
    return x * jax.nn.sigmoid(x)


def _softplus(x):
    return jnp.maximum(x, 0.0) + jnp.log1p(jnp.exp(-jnp.abs(x)))


def _log_sigmoid(x):
    return -_softplus(-x)


def _norm_mod(x, nw, scale, shift):
    y = x * lax.rsqrt(jnp.mean(x * x, axis=-1, keepdims=True) + EPS)
    return y * nw * (1.0 + scale) + shift


def _pick_tile(n, cap):
    t = cap
    while n % t:
        t //= 2
    return t


def _ada_kernel(c_ref, w_ref, b_ref, o_ref):
    o_ref[...] = _bdot(_silu(c_ref[...]), w_ref[...]) + b_ref[...]


def _ada(cond8, w_ada, b_ada):
    depth, d, nout = w_ada.shape
    tn = 1024
    return pl.pallas_call(
        _ada_kernel,
        grid=(depth, nout // tn),
        in_specs=[pl.BlockSpec((8, d), lambda l, j: (0, 0)),
                  pl.BlockSpec((None, d, tn), lambda l, j: (l, 0, j)),
                  pl.BlockSpec((None, 1, tn), lambda l, j: (l, 0, j))],
        out_specs=pl.BlockSpec((None, 8, tn), lambda l, j: (l, 0, j)),
        out_shape=jax.ShapeDtypeStruct((depth, 8, nout), F32),
        compiler_params=_cparams(("parallel", "parallel")),
        name="ada",
    )(cond8, w_ada, b_ada.reshape(depth, 1, nout))


def _ffn_kernel(x_ref, mod_ref, nw_ref, wg_ref, wu_ref, wd_ref, fw_ref, o_ref, h_ref, acc_ref,
                *, mod_base, final):
    j = pl.program_id(1)

    @pl.when(j == 0)
    def _():
        m = mod_ref[...]
        h = _norm_mod(x_ref[...], nw_ref[...], m[mod_base + 1:mod_base + 2], m[mod_base:mod_base + 1])
        h_ref[...] = h.astype(BF16)
        acc_ref[...] = jnp.zeros_like(acc_ref)

    h = h_ref[...]
    g = jnp.dot(h, wg_ref[...].astype(BF16), preferred_element_type=F32)
    u = jnp.dot(h, wu_ref[...].astype(BF16), preferred_element_type=F32)
    acc_ref[...] += _bdot(_silu(g) * u, wd_ref[...])

    @pl.when(j == pl.num_programs(1) - 1)
    def _():
        m = mod_ref[...]
        y = x_ref[...] + 0.5 * m[mod_base + 2:mod_base + 3] * acc_ref[...]
        if final:
            y = y * lax.rsqrt(jnp.mean(y * y, axis=-1, keepdims=True) + EPS) * fw_ref[...]
        o_ref[...] = y


def _mod_index(nb, tm, n_seq):
    if nb == 1:
        return lambda i: 0
    return lambda i: (i * tm) // n_seq


def _ffn(x, mods, norm_w, w_gate, w_up, w_down, final_w, *, layer, which, n_seq, final):
    t, d = x.shape
    f = w_gate.shape[-1]
    nb = mods.shape[0]
    tm = _pick_tile(n_seq if nb > 1 else t, 1024)
    tf = 256
    midx = _mod_index(nb, tm, n_seq)
    mod_base = 0 if which == 0 else 6
    return pl.pallas_call(
        functools.partial(_ffn_kernel, mod_base=mod_base, final=final),
        grid=(t // tm, f // tf),
        in_specs=[pl.BlockSpec((tm, d), lambda i, j: (i, 0)),
                  pl.BlockSpec((None, N_MOD, d), lambda i, j: (midx(i), 0, 0)),
                  pl.BlockSpec((None, None, 1, d), lambda i, j: (layer, 2 * which, 0, 0)),
                  pl.BlockSpec((None, None, d, tf), lambda i, j: (layer, which, 0, j)),
                  pl.BlockSpec((None, None, d, tf), lambda i, j: (layer, which, 0, j)),
                  pl.BlockSpec((None, None, tf, d), lambda i, j: (layer, which, j, 0)),
                  pl.BlockSpec((1, d), lambda i, j: (0, 0))],
        out_specs=pl.BlockSpec((tm, d), lambda i, j: (i, 0)),
        out_shape=jax.ShapeDtypeStruct((t, d), F32),
        scratch_shapes=[pltpu.VMEM((tm, d), BF16), pltpu.VMEM((tm, d), F32)],
        compiler_params=_cparams(("parallel", "arbitrary")),
        name="ffn",
    )(x, mods, norm_w.reshape(DEPTH, 3, 1, d), w_gate, w_up, w_down, final_w.reshape(1, d))


def _proj_kernel(x_ref, mod_ref, nw_ref, w_ref, cw_ref, o_ref, h_ref, *, rowlen):
    j = pl.program_id(1)

    @pl.when(j == 0)
    def _():
        m = mod_ref[...]
        h_ref[...] = _norm_mod(x_ref[...], nw_ref[...], m[4:5], m[3:4]).astype(BF16)
        o_ref[...] = jnp.dot(h_ref[...], w_ref[...].astype(BF16), preferred_element_type=F32)
        tm = o_ref.shape[0]
        pos = lax.broadcasted_iota(jnp.int32, (tm, 1), 0) % rowlen
        valid = [jnp.logical_and(pos + (t - GDN_CONV // 2) >= 0, pos + (t - GDN_CONV // 2) < rowlen)
                 for t in range(GDN_CONV)]
        for g in range(3 * GDN_HEADS):
            cols = slice(P_DQKV + g * GDN_DK, P_DQKV + (g + 1) * GDN_DK)
            xg = o_ref[:, cols]
            acc = xg * cw_ref[GDN_CONV // 2:GDN_CONV // 2 + 1, cols]
            for t in range(GDN_CONV):
                s = t - GDN_CONV // 2
                if s != 0:
                    xs = pltpu.roll(xg, (-s) % tm, axis=0)
                    acc = acc + jnp.where(valid[t], xs, 0.0) * cw_ref[t:t + 1, cols]
            y = _silu(acc)
            if g < 2 * GDN_HEADS:
                y = y * lax.rsqrt(jnp.sum(y * y, axis=-1, keepdims=True) + EPS)
                if g < GDN_HEADS:
                    y = y * (GDN_DK ** -0.5)
            o_ref[:, cols] = y

    @pl.when(j == 1)
    def _():
        o_ref[...] = jnp.dot(h_ref[...], w_ref[...].astype(BF16), preferred_element_type=F32)


def _proj(x, mods, norm_w, w_in_p, conv_w, *, layer, n_seq, rowlen):
    t, d = x.shape
    nb = mods.shape[0]
    tm = _pick_tile(n_seq if nb > 1 else t, 512)
    midx = _mod_index(nb, tm, n_seq)
    return pl.pallas_call(
        functools.partial(_proj_kernel, rowlen=rowlen),
        grid=(t // tm, PROJ_W // PROJ_TN),
        in_specs=[pl.BlockSpec((tm, d), lambda i, j: (i, 0)),
                  pl.BlockSpec((None, N_MOD, d), lambda i, j: (midx(i), 0, 0)),
                  pl.BlockSpec((None, None, 1, d), lambda i, j: (layer, 1, 0, 0)),
                  pl.BlockSpec((None, d, PROJ_TN), lambda i, j: (layer, 0, j)),
                  pl.BlockSpec((None, GDN_CONV, 3 * GDN_W), lambda i, j: (layer, 0, 0))],
        out_specs=pl.BlockSpec((tm, PROJ_TN), lambda i, j: (i, j)),
        out_shape=jax.ShapeDtypeStruct((t, PROJ_W), F32),
        scratch_shapes=[pltpu.VMEM((tm, d), BF16)],
        compiler_params=_cparams(("parallel", "arbitrary")),
        name="proj",
    )(x, mods, norm_w.reshape(DEPTH, 3, 1, d), w_in_p, conv_w)


def _permute_w_in(w_in):
    def seg(i):
        return w_in[:, :, _IN_OFF[i]:_IN_OFF[i + 1]]

    def pad(n):
        return jnp.zeros(w_in.shape[:2] + (n,), w_in.dtype)

    sm_pad = 128 - GLA_RANK - 2 * GDN_HEADS
    return jnp.concatenate(
        [seg(_DQ), seg(_DK), seg(_DV),
         seg(_GLRF), seg(_DAF), seg(_DBF), pad(sm_pad),
         seg(_GLRB), seg(_DAB), seg(_DBB), pad(sm_pad),
         seg(_GQ), seg(_GK), seg(_GV), seg(_GOG), seg(_DOG), seg(_SU)], axis=-1)


def _np_consts():
    t = np.arange(CHUNK)
    lower = (t[:, None] >= t[None, :])
    incl = np.stack([lower, lower.T])
    strict = np.stack([t[:, None] > t[None, :], t[:, None] < t[None, :]])
    eye4 = np.eye(4, dtype=bool)
    blk = np.kron(eye4, np.ones((CHUNK, CHUNK), bool))
    incl_bd = np.stack([np.kron(eye4, incl[d]) for d in range(2)])
    strict_bd = np.stack([np.kron(eye4, strict[d]) for d in range(2)])
    levels = []
    for d in range(2):
        per = []
        for lv in range(6):
            s = 1 << lv
            same = (t[:, None] // (2 * s)) == (t[None, :] // (2 * s))
            hi = (t % (2 * s)) >= s
            m = same & hi[:, None] & (~hi)[None, :]
            if d == 1:
                m = m.T
            per.append(np.kron(eye4, m))
        levels.append(np.stack(per))
    levels = np.stack(levels)
    assert (levels.sum(1) == strict_bd).all() and blk.any()
    gla_mask = np.stack([np.tile(incl[d], (1, 4)) for d in range(2)])
    hm_k = np.kron(eye4, np.ones((1, GLA_DK)))
    hm_v = np.kron(eye4, np.ones((1, GLA_DV)))
    st_mask = np.kron(eye4, np.ones((GLA_DV, GLA_DK)))
    ind64 = np.kron(eye4, np.full((GLA_DV, GLA_DV), 1.0 / GLA_DV))
    return dict(incl=incl, incl_bd=incl_bd, strict_bd=strict_bd, levels=levels, gla_mask=gla_mask,
                hm_k=hm_k, hm_v=hm_v, st_mask=st_mask, ind64=ind64, eye256=np.eye(256))


_C = _np_consts()


def _row_block_index(nblk):
    return lambda d, b, i: b * nblk + i + d * (nblk - 1 - 2 * i)


def _gla_kernel(qk_ref, v_ref, sm_ref, up_ref, bias_ref, tri_ref, mask_ref, hmk_ref, hmv_ref, stm_ref, *rest,
                nchunk, zero_init):
    if zero_init:
        o_ref, stf_ref, st_ref, qd_sc, m_sc, dec_sc = rest
    else:
        st0_ref, o_ref, stf_ref, st_ref, qd_sc, m_sc, dec_sc = rest
    d = pl.program_id(0)

    @pl.when(pl.program_id(2) == 0)
    def _():
        st_ref[...] = jnp.zeros_like(st_ref) if zero_init else st0_ref[...]

    tri = tri_ref[...]
    mask = mask_ref[...]
    hmk = hmk_ref[...]
    hmv = hmv_ref[...]
    stm = stm_ref[...]

    for c in range(nchunk):
        rows = pl.ds(c * CHUNK, CHUNK)
        qk = qk_ref[rows, :]
        q = qk[:, 0:GLA_KW] * (GLA_DK ** -0.5)
        k = qk[:, GLA_KW:2 * GLA_KW]
        v = v_ref[rows, :]
        x = _bdot(sm_ref[rows, :], up_ref[...]) + bias_ref[...]
        lg = _log_sigmoid(x) * (1.0 / GLA_GATE_NORM)
        b = _sel_dot(tri, lg)
        btot = jnp.sum(lg, axis=0, keepdims=True)
        qd = (q * jnp.exp(b)).astype(BF16)
        ki = k * jnp.exp(-b)
        kd = k * jnp.exp(btot - b)
        kexp = jnp.concatenate([ki * hmk[h:h + 1] for h in range(GLA_HEADS)], axis=0)
        att = _bdot_nt(qd, kexp) * mask
        vexp = jnp.concatenate([v * hmv[h:h + 1] for h in range(GLA_HEADS)], axis=0)
        o_ref[rows, :] = _bdot(att, vexp)
        qd_sc[c] = qd
        m_sc[c] = _bdot(v.T, kd) * stm
        dec_sc[c] = jnp.broadcast_to(jnp.exp(btot), (8, GLA_KW))

    st = st_ref[...]
    for ci in range(nchunk):
        c = ci + d * (nchunk - 1 - 2 * ci)
        rows = pl.ds(pl.multiple_of(c * CHUNK, CHUNK), CHUNK)
        o_ref[rows, :] += lax.dot_general(qd_sc[c], st.astype(BF16), (((1,), (1,)), ((), ())),
                                          preferred_element_type=F32)
        st = st * dec_sc[c][0:1] + m_sc[c]
    st_ref[...] = st
    stf_ref[...] = st


def _gla(proj, up_p, bias_p, st0, *, nseq, n_seq):
    t = proj.shape[0]
    nblk = n_seq // SUB
    nck = SUB // CHUNK
    rbi = _row_block_index(nblk)
    c = _C
    consts = [jnp.asarray(c["incl"], BF16), jnp.asarray(c["gla_mask"], F32), jnp.asarray(c["hm_k"], F32),
              jnp.asarray(c["hm_v"], F32), jnp.asarray(c["st_mask"], F32)]
    st_spec = pl.BlockSpec((None, None, GLA_W, GLA_KW), lambda d, b, i: (d, b, 0, 0))
    zero_init = st0 is None
    return pl.pallas_call(
        functools.partial(_gla_kernel, nchunk=nck, zero_init=zero_init),
        grid=(2, nseq, nblk),
        in_specs=[pl.BlockSpec((SUB, 2 * GLA_KW), lambda d, b, i: (rbi(d, b, i), P_GQK // (2 * GLA_KW))),
                  pl.BlockSpec((SUB, GLA_W), lambda d, b, i: (rbi(d, b, i), P_GV // GLA_W)),
                  pl.BlockSpec((SUB, 128), lambda d, b, i: (rbi(d, b, i), P_SMALL // 128 + d)),
                  pl.BlockSpec((None, 128, GLA_KW), lambda d, b, i: (d, 0, 0)),
                  pl.BlockSpec((None, 1, GLA_KW), lambda d, b, i: (d, 0, 0)),
                  pl.BlockSpec((None, CHUNK, CHUNK), lambda d, b, i: (d, 0, 0)),
                  pl.BlockSpec((None, CHUNK, 256), lambda d, b, i: (d, 0, 0)),
                  pl.BlockSpec((GLA_HEADS, GLA_KW), lambda d, b, i: (0, 0)),
                  pl.BlockSpec((GLA_HEADS, GLA_W), lambda d, b, i: (0, 0)),
                  pl.BlockSpec((GLA_W, GLA_KW), lambda d, b, i: (0, 0))] + ([] if zero_init else [st_spec]),
        out_specs=[pl.BlockSpec((None, SUB, GLA_W), lambda d, b, i: (d, rbi(d, b, i), 0)), st_spec],
        out_shape=[jax.ShapeDtypeStruct((2, t, GLA_W), F32),
                   jax.ShapeDtypeStruct((2, nseq, GLA_W, GLA_KW), F32)],
        scratch_shapes=[pltpu.VMEM((GLA_W, GLA_KW), F32), pltpu.VMEM((nck, CHUNK, GLA_KW), BF16),
                        pltpu.VMEM((nck, GLA_W, GLA_KW), F32), pltpu.VMEM((nck, 8, GLA_KW), F32)],
        compiler_params=_cparams(("arbitrary", "arbitrary", "arbitrary")),
        name="gla",
    )(proj, proj, proj, up_p, bias_p, *consts, *([] if zero_init else [st0]))


def _gdn_kernel(qkv_ref, sm_ref, arow_ref, dtb_ref, tri_ref, incl_ref, strict_ref, lvl_ref, eye_ref, *rest,
                nchunk, zero_init):
    if zero_init:
        o_ref, sf_ref, s_ref, la_ref, be_ref, w_sc, u0_sc, qkd_sc, qe_sc, kdt_sc, egl_sc = rest
    else:
        s0_ref, o_ref, sf_ref, s_ref, la_ref, be_ref, w_sc, u0_sc, qkd_sc, qe_sc, kdt_sc, egl_sc = rest
    d = pl.program_id(0)

    @pl.when(pl.program_id(2) == 0)
    def _():
        s_ref[...] = jnp.zeros_like(s_ref) if zero_init else s0_ref[...]

    sm = sm_ref[...]
    la_ref[...] = -jnp.exp(arow_ref[...]) * _softplus(sm + dtb_ref[...])
    be_ref[...] = jax.nn.sigmoid(sm)

    tri = tri_ref[...]
    incl = incl_ref[...]
    strict = strict_ref[...]
    eye = eye_ref[...]

    def stack(ref, r0, lo, width):
        return jnp.concatenate([ref[pl.ds(r0, CHUNK), lo + h * width:lo + (h + 1) * width]
                                for h in range(GDN_HEADS)], axis=0)

    def colstack(a, lo):
        return jnp.concatenate([a[:, lo + h:lo + h + 1] for h in range(GDN_HEADS)], axis=0)

    lvl = [lvl_ref[lv] for lv in range(6)]

    loc = []
    for c in range(nchunk):
        r0 = c * CHUNK
        qs = stack(qkv_ref, r0, 0, GDN_DK)
        ks = stack(qkv_ref, r0, GDN_W, GDN_DK)
        vs = stack(qkv_ref, r0, 2 * GDN_W, GDN_DV)
        la = la_ref[pl.ds(r0, CHUNK), :]
        g = _sel_dot(tri, la)
        gtot = jnp.sum(la, axis=0, keepdims=True)
        gcol = colstack(g, SM_A)
        kfcol = colstack(jnp.exp(gtot - g), SM_A)
        egcol = colstack(jnp.exp(g), SM_A)
        bcol = colstack(be_ref[pl.ds(r0, CHUNK), :], SM_B)
        grow = jnp.sum(eye * gcol, axis=0, keepdims=True)
        dec = incl * jnp.exp(jnp.minimum(gcol - grow, 0.0))
        ksb = ks.astype(BF16)
        lmat = strict * dec * _bdot_nt(ksb, ksb) * bcol
        rhs = jnp.concatenate([ks * (bcol * egcol), vs * bcol], axis=1).astype(BF16)
        qkd_sc[c] = (_bdot_nt(qs, ksb) * dec).astype(BF16)
        qe_sc[c] = (qs * egcol).astype(BF16)
        kd = ks * kfcol
        for h in range(GDN_HEADS):
            kdt_sc[c * GDN_HEADS + h] = kd[h * CHUNK:(h + 1) * CHUNK].T.astype(BF16)
        egl_sc[c] = jnp.broadcast_to(jnp.exp(gtot), (8, 128))
        loc.append((lmat.astype(BF16), rhs))
    tinv = [eye - (lm * lvl[0]).astype(F32) for lm, _ in loc]
    for lv in range(1, 6):
        tb = [t_.astype(BF16) for t_ in tinv]
        cs = [jnp.dot(lm * lvl[lv], t_, preferred_element_type=F32) for (lm, _), t_ in zip(loc, tb)]
        tinv = [t32 - jnp.dot(t_, c_.astype(BF16), preferred_element_type=F32)
                for t32, t_, c_ in zip(tinv, tb, cs)]
    for c in range(nchunk):
        sol = jnp.dot(tinv[c].astype(BF16), loc[c][1], preferred_element_type=F32)
        w_sc[c] = sol[:, :GDN_DK].astype(BF16)
        u0_sc[c] = sol[:, GDN_DK:]

    s_cur = [s_ref[h] for h in range(GDN_HEADS)]
    for ci in range(nchunk):
        c = ci + d * (nchunk - 1 - 2 * ci)
        r0 = pl.multiple_of(c * CHUNK, CHUNK)
        sb = [s_.astype(BF16) for s_ in s_cur]
        us = []
        for h in range(GDN_HEADS):
            rows = pl.ds(h * CHUNK, CHUNK)
            us.append(u0_sc[c, rows, :] - jnp.dot(w_sc[c, rows, :], sb[h], preferred_element_type=F32))
        ub = [u_.astype(BF16) for u_ in us]
        o2 = jnp.dot(qkd_sc[c], jnp.concatenate(ub, axis=0), preferred_element_type=F32)
        egl = egl_sc[c]
        for h in range(GDN_HEADS):
            rows = pl.ds(h * CHUNK, CHUNK)
            o_ref[pl.ds(r0, CHUNK), h * GDN_DV:(h + 1) * GDN_DV] = (
                jnp.dot(qe_sc[c, rows, :], sb[h], preferred_element_type=F32) + o2[h * CHUNK:(h + 1) * CHUNK])
            s_cur[h] = (egl[0:1, SM_A + h:SM_A + h + 1] * s_cur[h]
                        + jnp.dot(kdt_sc[c * GDN_HEADS + h], ub[h], preferred_element_type=F32))
    for h in range(GDN_HEADS):
        s_ref[h] = s_cur[h]
    sf_ref[...] = s_ref[...]


def _gdn(proj, arow, dtb, s0, *, nseq, n_seq):
    t = proj.shape[0]
    nblk = n_seq // SUB
    rbi = _row_block_index(nblk)
    c = _C
    consts = [jnp.asarray(c["incl"], BF16), jnp.asarray(c["incl_bd"], F32), jnp.asarray(c["strict_bd"], F32),
              jnp.asarray(c["levels"], BF16), jnp.asarray(c["eye256"], F32)]
    hs = (GDN_HEADS, GDN_DK, GDN_DV)
    nck = SUB // CHUNK
    zero_init = s0 is None
    s_spec = pl.BlockSpec((None, None) + hs, lambda d, b, i: (d, b, 0, 0, 0))
    return pl.pallas_call(
        functools.partial(_gdn_kernel, nchunk=nck, zero_init=zero_init),
        grid=(2, nseq, nblk),
        in_specs=[pl.BlockSpec((SUB, 3 * GDN_W), lambda d, b, i: (rbi(d, b, i), P_DQKV // (3 * GDN_W))),
                  pl.BlockSpec((SUB, 128), lambda d, b, i: (rbi(d, b, i), P_SMALL // 128 + d)),
                  pl.BlockSpec((None, 1, 128), lambda d, b, i: (d, 0, 0)),
                  pl.BlockSpec((None, 1, 128), lambda d, b, i: (d, 0, 0)),
                  pl.BlockSpec((None, CHUNK, CHUNK), lambda d, b, i: (d, 0, 0)),
                  pl.BlockSpec((None, 256, 256), lambda d, b, i: (d, 0, 0)),
                  pl.BlockSpec((None, 256, 256), lambda d, b, i: (d, 0, 0)),
                  pl.BlockSpec((None, 6, 256, 256), lambda d, b, i: (d, 0, 0, 0)),
                  pl.BlockSpec((256, 256), lambda d, b, i: (0, 0))] + ([] if zero_init else [s_spec]),
        out_specs=[pl.BlockSpec((None, SUB, GDN_W), lambda d, b, i: (d, rbi(d, b, i), 0)), s_spec],
        out_shape=[jax.ShapeDtypeStruct((2, t, GDN_W), F32),
                   jax.ShapeDtypeStruct((2, nseq) + hs, F32)],
        scratch_shapes=[pltpu.VMEM(hs, F32),
                        pltpu.VMEM((SUB, 128), F32), pltpu.VMEM((SUB, 128), F32),
                        pltpu.VMEM((nck, 256, GDN_DK), BF16), pltpu.VMEM((nck, 256, GDN_DV), F32),
                        pltpu.VMEM((nck, 256, 256), BF16), pltpu.VMEM((nck, 256, GDN_DK), BF16),
                        pltpu.VMEM((nck * GDN_HEADS, GDN_DK, CHUNK), BF16), pltpu.VMEM((nck, 8, 128), F32)],
        compiler_params=_cparams(("arbitrary", "arbitrary", "arbitrary")),
        name="gdn",
    )(proj, proj, arow, dtb, *consts, *([] if zero_init else [s0]))


def _s5_param_kernel(lr_ref, li_ref, st_ref, br_ref, bi_ref, lam_ref, bbr_ref, bbi_ref):
    lr = lr_ref[...]
    li = li_ref[...]
    step = jnp.exp(st_ref[...])
    mag = jnp.exp(lr * step)
    lbr = mag * jnp.cos(li * step)
    lbi = mag * jnp.sin(li * step)
    lam_ref[0:1, :] = lbr
    lam_ref[1:2, :] = lbi
    nr = lbr - 1.0
    den = lr * lr + li * li
    cr = (nr * lr + lbi * li) / den
    ci = (lbi * lr - nr * li) / den
    br = br_ref[...]
    bi = bi_ref[...]
    bbr_ref[...] = cr * br - ci * bi
    bbi_ref[...] = cr * bi + ci * br


def _s5_params(lam_re, lam_im, log_step, b_re, b_im):
    n = DEPTH * 2
    lr = lam_re.reshape(n, 1, S5_N)
    li = lam_im.reshape(n, 1, S5_N)
    st = jnp.repeat(log_step.reshape(n, S5_GROUPS), S5_P, axis=-1).reshape(n, 1, S5_N)
    br = b_re.reshape(n, S5_N, S5_CH).transpose(0, 2, 1)
    bi = b_im.reshape(n, S5_N, S5_CH).transpose(0, 2, 1)
    row = pl.BlockSpec((None, 1, S5_N), lambda i: (i, 0, 0))
    mat = pl.BlockSpec((None, S5_CH, S5_N), lambda i: (i, 0, 0))
    return pl.pallas_call(
        _s5_param_kernel,
        grid=(n,),
        in_specs=[row, row, row, mat, mat],
        out_specs=[pl.BlockSpec((None, 2, S5_N), lambda i: (i, 0, 0)), mat, mat],
        out_shape=[jax.ShapeDtypeStruct((n, 2, S5_N), F32),
                   jax.ShapeDtypeStruct((n, S5_CH, S5_N), F32),
                   jax.ShapeDtypeStruct((n, S5_CH, S5_N), F32)],
        compiler_params=_cparams(("parallel",)),
        name="s5_params",
    )(lr, li, st, br, bi)


def _s5_kernel(u_ref, bbd_ref, cre_ref, cim_ref, lam_ref, *rest, nseq, tb, zero_init):
    if zero_init:
        y_ref, hf_ref, h_ref, bu_ref, hs_ref = rest
    else:
        h0_ref, y_ref, hf_ref, h_ref, bu_ref, hs_ref = rest
    d = pl.program_id(0)
    nt = S5_N // 128

    @pl.when(pl.program_id(1) == 0)
    def _():
        h_ref[...] = jnp.zeros_like(h_ref) if zero_init else h0_ref[...]

    bu_ref[...] = _bdot(u_ref[...], bbd_ref[...])

    def step(tt, carry):
        t = tt + d * (tb - 1 - 2 * tt)
        rows = pl.ds(pl.multiple_of(t * nseq, nseq), nseq)
        out = []
        for k in range(nt):
            re = slice(k * 128, (k + 1) * 128)
            im = slice(S5_N + k * 128, S5_N + (k + 1) * 128)
            lr = lam_ref[0:1, re]
            li = lam_ref[1:2, re]
            hr, hi = carry[k], carry[nt + k]
            nr = lr * hr - li * hi + bu_ref[rows, re]
            ni = lr * hi + li * hr + bu_ref[rows, im]
            hs_ref[rows, re] = nr
            hs_ref[rows, im] = ni
            out.append((nr, ni))
        return tuple(o[0] for o in out) + tuple(o[1] for o in out)

    init = tuple(h_ref[:, k * 128:(k + 1) * 128] for k in range(2 * nt))
    fin = lax.fori_loop(0, tb, step, init, unroll=2)
    for k in range(2 * nt):
        h_ref[:, k * 128:(k + 1) * 128] = fin[k]
        hf_ref[:, k * 128:(k + 1) * 128] = fin[k]
    y_ref[...] = _bdot(hs_ref[:, 0:S5_N], cre_ref[...]) - _bdot(hs_ref[:, S5_N:2 * S5_N], cim_ref[...])


def _s5(u_tm, bbd, cre, cim, lam, h0, *, nseq, n_seq):
    tb = min(n_seq, 1024 // nseq)
    nblk = n_seq // tb
    tidx = lambda d, i: i + d * (nblk - 1 - 2 * i)
    zero_init = h0 is None
    h_spec = pl.BlockSpec((None, nseq, 2 * S5_N), lambda d, i: (d, 0, 0))
    return pl.pallas_call(
        functools.partial(_s5_kernel, nseq=nseq, tb=tb, zero_init=zero_init),
        grid=(2, nblk),
        in_specs=[pl.BlockSpec((tb * nseq, S5_W), lambda d, i: (tidx(d, i), 0)),
                  pl.BlockSpec((None, S5_W, 2 * S5_N), lambda d, i: (d, 0, 0)),
                  pl.BlockSpec((None, S5_N, S5_W), lambda d, i: (d, 0, 0)),
                  pl.BlockSpec((None, S5_N, S5_W), lambda d, i: (d, 0, 0)),
                  pl.BlockSpec((None, 2, S5_N), lambda d, i: (d, 0, 0))] + ([] if zero_init else [h_spec]),
        out_specs=[pl.BlockSpec((None, tb * nseq, S5_W), lambda d, i: (d, tidx(d, i), 0)), h_spec],
        out_shape=[jax.ShapeDtypeStruct((2, n_seq * nseq, S5_W), F32),
                   jax.ShapeDtypeStruct((2, nseq, 2 * S5_N), F32)],
        scratch_shapes=[pltpu.VMEM((nseq, 2 * S5_N), F32), pltpu.VMEM((tb * nseq, 2 * S5_N), F32),
                        pltpu.VMEM((tb * nseq, 2 * S5_N), F32)],
        compiler_params=_cparams(("arbitrary", "arbitrary")),
        name="s5",
    )(u_tm, bbd, cre, cim, lam, *([] if zero_init else [h0]))


def _out_kernel(x_ref, mod_ref, ogf_ref, ogb_ref, odf_ref, odb_ref, ysf_ref, ysb_ref, gog_ref, su_ref, dog_ref,
                ind_ref, gw_ref, dw_ref, sd_ref, wglu_ref, bglu_ref, wout_ref, o_ref):
    og = ogf_ref[...] + ogb_ref[...]
    hi, mid, _ = _split3(og * og)
    ms = jnp.dot(hi, ind_ref[...], preferred_element_type=F32) + jnp.dot(mid, ind_ref[...], preferred_element_type=F32)
    og = og * lax.rsqrt(ms + EPS) * gw_ref[...] * _silu(gog_ref[...])
    od = odf_ref[...] + odb_ref[...]
    dog = dog_ref[...]
    parts = []
    for h in range(GDN_HEADS):
        z = od[:, h * GDN_DV:(h + 1) * GDN_DV]
        z = z * lax.rsqrt(jnp.mean(z * z, axis=-1, keepdims=True) + EPS) * dw_ref[...]
        parts.append(z * _silu(dog[:, h * GDN_DV:(h + 1) * GDN_DV]))
    y = su_ref[...] * sd_ref[...] + ysf_ref[...] + ysb_ref[...]
    g = 0.5 * y * (1.0 + jnp.tanh(math.sqrt(2.0 / math.pi) * (y + 0.044715 * (y * y * y))))
    os_ = g * jax.nn.sigmoid(_bdot(g, wglu_ref[...]) + bglu_ref[...])
    out = _bdot(og, wout_ref[0:GLA_W, :]) + _bdot(os_, wout_ref[GLA_W + GDN_W:MIX_W, :])
    for h in range(GDN_HEADS):
        lo = GLA_W + h * GDN_DV
        out = out + _bdot(parts[h], wout_ref[lo:lo + GDN_DV, :])
    o_ref[...] = x_ref[...] + mod_ref[5:6, :] * out


def _mix_out(x, mods, proj, og, od, ys, gla_nw, gdn_nw, s5_d, w_glu, b_glu, w_out, *, layer, n_seq):
    t, d = x.shape
    nb = mods.shape[0]
    tm = _pick_tile(n_seq if nb > 1 else t, 512)
    midx = _mod_index(nb, tm, n_seq)
    ys2 = ys

    def dirspec(w, dd):
        return pl.BlockSpec((None, tm, w), lambda i: (dd, i, 0))

    def vec(w):
        return pl.BlockSpec((None, 1, w), lambda i: (layer, 0, 0))

    return pl.pallas_call(
        _out_kernel,
        grid=(t // tm,),
        in_specs=[pl.BlockSpec((tm, d), lambda i: (i, 0)),
                  pl.BlockSpec((None, N_MOD, d), lambda i: (midx(i), 0, 0)),
                  dirspec(GLA_W, 0), dirspec(GLA_W, 1), dirspec(GDN_W, 0), dirspec(GDN_W, 1),
                  dirspec(S5_W, 0), dirspec(S5_W, 1),
                  pl.BlockSpec((tm, GLA_W), lambda i: (i, P_GOG // GLA_W)),
                  pl.BlockSpec((tm, S5_W), lambda i: (i, P_SU // S5_W)),
                  pl.BlockSpec((tm, GDN_W), lambda i: (i, P_DOG // GDN_W)),
                  pl.BlockSpec((GLA_W, GLA_W), lambda i: (0, 0)),
                  vec(GLA_W), vec(GDN_DV), vec(S5_W),
                  pl.BlockSpec((None, S5_W, S5_W), lambda i: (layer, 0, 0)),
                  vec(S5_W),
                  pl.BlockSpec((None, MIX_W, d), lambda i: (layer, 0, 0))],
        out_specs=pl.BlockSpec((tm, d), lambda i: (i, 0)),
        out_shape=jax.ShapeDtypeStruct((t, d), F32),
        compiler_params=_cparams(("parallel",)),
        name="mix_out",
    )(x, mods, og, og, od, od, ys2, ys2, proj, proj, proj,
      jnp.asarray(_C["ind64"], BF16),
      jnp.tile(gla_nw, (1, GLA_HEADS)).reshape(DEPTH, 1, GLA_W), gdn_nw.reshape(DEPTH, 1, GDN_DV),
      s5_d.reshape(DEPTH, 1, S5_W), w_glu, b_glu.reshape(DEPTH, 1, S5_W), w_out)


def _gla_params(gk_up, gk_bias, layer):
    up = jnp.zeros((2, 128, GLA_KW), F32).at[:, :GLA_RANK, :].set(gk_up[layer])
    return up, gk_bias[layer].reshape(2, 1, GLA_KW)


def _gdn_params(a_log, dt_bias, layer):
    arow = jnp.zeros((2, 1, 128), F32).at[:, 0, SM_A:SM_A + GDN_HEADS].set(a_log[layer])
    dtb = jnp.zeros((2, 1, 128), F32).at[:, 0, SM_A:SM_A + GDN_HEADS].set(dt_bias[layer])
    return arow, dtb


def _gla_state_in(s):
    eye = jnp.eye(GLA_HEADS, dtype=F32)
    st = jnp.einsum('bzhde,hg->zbhegd', s.astype(F32), eye)
    return st.reshape(2, s.shape[0], GLA_W, GLA_KW)


def _gla_state_out(st):
    b = st.shape[1]
    s = st.reshape(2, b, GLA_HEADS, GLA_DV, GLA_HEADS, GLA_DK)
    s = jnp.stack([s[:, :, h, :, h, :] for h in range(GLA_HEADS)], axis=2)
    return s.transpose(1, 0, 2, 4, 3)


def _s5_mats(bbr, bbi, c_re, c_im, layer):
    eye = jnp.eye(S5_GROUPS, dtype=F32)

    def place_b(bb):
        bb = bb.reshape(DEPTH, 2, S5_CH, S5_GROUPS, S5_P)[layer]
        return jnp.einsum('zcgp,hg->zhcgp', bb, eye).reshape(2, S5_W, S5_N)

    def place_c(cc):
        return jnp.einsum('zgcp,gh->zgphc', cc[layer].astype(F32), eye).reshape(2, S5_N, S5_W)

    bbd = jnp.concatenate([place_b(bbr), place_b(bbi)], axis=-1)
    return bbd, place_c(c_re), place_c(c_im)


def _trunk(x3, mods, rows, states, p, s5p, w_in_p):
    nseq, n_seq, d = x3.shape
    x = x3.reshape(nseq * n_seq, d)
    lam, bbr, bbi = s5p
    fin = []
    for l in range(DEPTH):
        m = mods[l]
        x = _ffn(x, m, p['norm_w'], p['ffn_w_gate'], p['ffn_w_up'], p['ffn_w_down'], p['final_norm_w'],
                 layer=l, which=0, n_seq=n_seq, final=False)
        proj = _proj(x, m, p['norm_w'], w_in_p, p['gdn_conv_w'], layer=l, n_seq=n_seq, rowlen=n_seq // rows)
        if states is None:
            st0 = s0 = h0 = None
        else:
            st_gla, st_gdn, st_re, st_im = states
            st0 = _gla_state_in(st_gla[:, l])
            s0 = st_gdn[:, l].transpose(1, 0, 2, 3, 4)
            h0 = jnp.concatenate([st_re[:, l].reshape(nseq, 2, S5_N), st_im[:, l].reshape(nseq, 2, S5_N)],
                                 axis=-1).transpose(1, 0, 2)
        up_p, bias_p = _gla_params(p['gla_gk_up'], p['gla_gk_bias'], l)
        og, gla_f = _gla(proj, up_p, bias_p, st0, nseq=nseq, n_seq=n_seq)
        arow, dtb = _gdn_params(p['gdn_a_log'], p['gdn_dt_bias'], l)
        od, gdn_f = _gdn(proj, arow, dtb, s0, nseq=nseq, n_seq=n_seq)
        bbd, cre, cim = _s5_mats(bbr, bbi, p['s5_c_re'], p['s5_c_im'], l)
        nsp = -(-nseq // 8) * 8
        u_tm = proj[:, P_SU:P_SU + S5_W].reshape(nseq, n_seq, S5_W).transpose(1, 0, 2)
        u_tm = jnp.pad(u_tm, ((0, 0), (0, nsp - nseq), (0, 0))).reshape(n_seq * nsp, S5_W)
        if h0 is not None:
            h0 = jnp.pad(h0, ((0, 0), (0, nsp - nseq), (0, 0)))
        ys_tm, s5_f = _s5(u_tm, bbd, cre, cim, lam.reshape(DEPTH, 2, 2, S5_N)[l], h0, nseq=nsp, n_seq=n_seq)
        ys = ys_tm.reshape(2, n_seq, nsp, S5_W)[:, :, :nseq].transpose(0, 2, 1, 3).reshape(2, nseq * n_seq, S5_W)
        s5_f = s5_f[:, :nseq]
        x = _mix_out(x, m, proj, og, od, ys, p['gla_norm_w'], p['gdn_norm_w'], p['s5_d'], p['s5_w_glu'],
                     p['s5_b_glu'], p['w_out'], layer=l, n_seq=n_seq)
        x = _ffn(x, m, p['norm_w'], p['ffn_w_gate'], p['ffn_w_up'], p['ffn_w_down'], p['final_norm_w'],
                 layer=l, which=1, n_seq=n_seq, final=(l == DEPTH - 1))
        fin.append((gla_f, gdn_f, s5_f))
    return x.reshape(nseq, n_seq, d), fin


def kernel(x_prompt, x_sample, c, state_gla, state_gdn, state_s5_re, state_s5_im, c_ctx, w_ada, b_ada, norm_w, ffn_w_gate, ffn_w_up, ffn_w_down, w_in, gla_gk_up, gla_gk_bias, gla_norm_w, gdn_conv_w, gdn_a_log, gdn_dt_bias, gdn_norm_w, s5_lam_re, s5_lam_im, s5_log_step, s5_b_re, s5_b_im, s5_c_re, s5_c_im, s5_d, s5_w_glu, s5_b_glu, w_out, final_norm_w):
    p = dict(norm_w=norm_w, ffn_w_gate=ffn_w_gate, ffn_w_up=ffn_w_up, ffn_w_down=ffn_w_down,
             gla_gk_up=gla_gk_up, gla_gk_bias=gla_gk_bias, gla_norm_w=gla_norm_w, gdn_conv_w=gdn_conv_w,
             gdn_a_log=gdn_a_log, gdn_dt_bias=gdn_dt_bias, gdn_norm_w=gdn_norm_w, s5_c_re=s5_c_re,
             s5_c_im=s5_c_im, s5_d=s5_d, s5_w_glu=s5_w_glu, s5_b_glu=s5_b_glu, w_out=w_out,
             final_norm_w=final_norm_w)
    b_ctx = x_prompt.shape[0]
    b_dec = x_sample.shape[0]
    cond8 = jnp.zeros((8, D_MODEL), F32).at[0].set(c_ctx).at[1:1 + b_dec].set(c)
    mods = _ada(cond8, w_ada, b_ada)
    mods_ctx = mods[:, 0:1].reshape(DEPTH, 1, N_MOD, D_MODEL)
    mods_dec = mods[:, 1:1 + b_dec].reshape(DEPTH, b_dec, N_MOD, D_MODEL)
    s5p = _s5_params(s5_lam_re, s5_lam_im, s5_log_step, s5_b_re, s5_b_im)
    w_in_p = _permute_w_in(w_in)

    y_prompt, fin = _trunk(x_prompt, mods_ctx, 1, None, p, s5p, w_in_p)
    new_gla = jnp.stack([_gla_state_out(f[0]) for f in fin], axis=1)
    new_gdn = jnp.stack([f[1].transpose(1, 0, 2, 3, 4) for f in fin], axis=1)
    s5f = jnp.stack([f[2].transpose(1, 0, 2) for f in fin], axis=1)
    new_re = s5f[..., :S5_N].reshape(b_ctx, DEPTH, 2, S5_GROUPS, S5_P)
    new_im = s5f[..., S5_N:].reshape(b_ctx, DEPTH, 2, S5_GROUPS, S5_P)

    rows = x_sample.shape[1] // 64
    y_sample, _ = _trunk(x_sample, mods_dec, rows, (state_gla, state_gdn, state_s5_re, state_s5_im),
                         p, s5p, w_in_p)
    return (y_prompt, y_sample, new_gla.astype(state_gla.dtype), new_gdn.astype(state_gdn.dtype),
            new_re.astype(state_s5_re.dtype), new_im.astype(state_s5_im.dtype))
```

```python
import functools
import math

import numpy as np
import jax
import jax.numpy as jnp
from jax import lax
from jax.experimental import pallas as pl
from jax.experimental.pallas import tpu as pltpu

F32 = jnp.float32
BF16 = jnp.bfloat16

D_MODEL = 1024
DEPTH = 2
FFN_DIM = 2816
N_MOD = 9
EPS = 1e-6
CHUNK = 64
GLA_HEADS, GLA_DK, GLA_DV, GLA_RANK = 4, 32, 64, 16
GLA_GATE_NORM = 16.0
GDN_HEADS, GDN_DK, GDN_DV, GDN_CONV = 4, 128, 128, 5
S5_GROUPS, S5_CH, S5_P = 16, 16, 64
GLA_W = GLA_HEADS * GLA_DV
GLA_KW = GLA_HEADS * GLA_DK
GDN_W = GDN_HEADS * GDN_DV
S5_W = S5_GROUPS * S5_CH
S5_N = S5_GROUPS * S5_P
S5_L = 16
S5_PAIRS = S5_GROUPS // 2
MIX_W = GLA_W + GDN_W + S5_W

_IN_SIZES = (GLA_KW, GLA_KW, GLA_W, GLA_RANK, GLA_RANK, GLA_W,
             GDN_W, GDN_W, GDN_W, GDN_HEADS, GDN_HEADS, GDN_HEADS, GDN_HEADS, GDN_W, S5_W)
_IN_OFF = tuple(int(v) for v in np.cumsum((0,) + _IN_SIZES))
(_GQ, _GK, _GV, _GLRF, _GLRB, _GOG, _DQ, _DK, _DV, _DAF, _DAB, _DBF, _DBB, _DOG, _SU) = range(15)

P_DQKV = 0
P_SMALL = 1536
P_GQK = 1792
P_GV = 2048
P_GOG = 2304
P_DOG = 2560
P_SU = 3072
PROJ_W = 3328
PROJ_TN = PROJ_W // 2
SM_A = GLA_RANK
SM_B = GLA_RANK + GDN_HEADS
SUB = 256
VMEM_LIMIT = 56 * 1024 * 1024


def _cparams(sem):
    return pltpu.CompilerParams(dimension_semantics=sem, vmem_limit_bytes=VMEM_LIMIT)


def _bdot(a, b):
    return jnp.dot(a.astype(BF16), b.astype(BF16), preferred_element_type=F32)


def _bdot_nt(a, b):
    return lax.dot_general(a.astype(BF16), b.astype(BF16), (((1,), (1,)), ((), ())),
                           preferred_element_type=F32)


def _split3(x):
    hi = x.astype(BF16)
    r1 = x - hi.astype(F32)
    mid = r1.astype(BF16)
    lo = (r1 - mid.astype(F32)).astype(BF16)
    return hi, mid, lo


def _sel_dot(m01, x):
    hi, mid, lo = _split3(x)
    return (jnp.dot(m01, hi, preferred_element_type=F32) + jnp.dot(m01, mid, preferred_element_type=F32)
            + jnp.dot(m01, lo, preferred_element_type=F32))


def _silu(x):
    return x * jax.nn.sigmoid(x)


def _softplus(x):
    return jnp.maximum(x, 0.0) + jnp.log1p(jnp.exp(-jnp.abs(x)))


def _log_sigmoid(x):
    return -_softplus(-x)


def _norm_mod(x, nw, scale, shift):
    y = x * lax.rsqrt(jnp.mean(x * x, axis=-1, keepdims=True) + EPS)
    return y * nw * (1.0 + scale) + shift


def _pick_tile(n, cap):
    t = cap
    while n % t:
        t //= 2
    return t


def _ada_kernel(c_ref, w_ref, b_ref, o_ref):
    o_ref[...] = _bdot(_silu(c_ref[...]), w_ref[...]) + b_ref[...]


def _ada(cond8, w_ada, b_ada):
    depth, d, nout = w_ada.shape
    tn = 1024
    return pl.pallas_call(
        _ada_kernel,
        grid=(depth, nout // tn),
        in_specs=[pl.BlockSpec((8, d), lambda l, j: (0, 0)),
                  pl.BlockSpec((None, d, tn), lambda l, j: (l, 0, j)),
                  pl.BlockSpec((None, 1, tn), lambda l, j: (l, 0, j))],
        out_specs=pl.BlockSpec((None, 8, tn), lambda l, j: (l, 0, j)),
        out_shape=jax.ShapeDtypeStruct((depth, 8, nout), F32),
        compiler_params=_cparams(("parallel", "parallel")),
        name="ada",
    )(cond8, w_ada, b_ada.reshape(depth, 1, nout))


def _ffn_kernel(x_ref, mod_ref, nw_ref, wg_ref, wu_ref, wd_ref, fw_ref, o_ref, h_ref, acc_ref,
                *, mod_base, final):
    j = pl.program_id(1)

    @pl.when(j == 0)
    def _():
        m = mod_ref[...]
        h = _norm_mod(x_ref[...], nw_ref[...], m[mod_base + 1:mod_base + 2], m[mod_base:mod_base + 1])
        h_ref[...] = h.astype(BF16)
        acc_ref[...] = jnp.zeros_like(acc_ref)

    h = h_ref[...]
    g = jnp.dot(h, wg_ref[...].astype(BF16), preferred_element_type=F32)
    u = jnp.dot(h, wu_ref[...].astype(BF16), preferred_element_type=F32)
    acc_ref[...] += _bdot(_silu(g) * u, wd_ref[...])

    @pl.when(j == pl.num_programs(1) - 1)
    def _():
        m = mod_ref[...]
        y = x_ref[...] + 0.5 * m[mod_base + 2:mod_base + 3] * acc_ref[...]
        if final:
            y = y * lax.rsqrt(jnp.mean(y * y, axis=-1, keepdims=True) + EPS) * fw_ref[...]
        o_ref[...] = y


def _mod_index(nb, tm, n_seq):
    if nb == 1:
        return lambda i: 0
    return lambda i: (i * tm) // n_seq


def _ffn(x, mods, norm_w, w_gate, w_up, w_down, final_w, *, layer, which, n_seq, final):
    t, d = x.shape
    f = w_gate.shape[-1]
    nb = mods.shape[0]
    tm = _pick_tile(n_seq if nb > 1 else t, 1024)
    tf = 256
    midx = _mod_index(nb, tm, n_seq)
    mod_base = 0 if which == 0 else 6
    return pl.pallas_call(
        functools.partial(_ffn_kernel, mod_base=mod_base, final=final),
        grid=(t // tm, f // tf),
        in_specs=[pl.BlockSpec((tm, d), lambda i, j: (i, 0)),
                  pl.BlockSpec((None, N_MOD, d), lambda i, j: (midx(i), 0, 0)),
                  pl.BlockSpec((None, None, 1, d), lambda i, j: (layer, 2 * which, 0, 0)),
                  pl.BlockSpec((None, None, d, tf), lambda i, j: (layer, which, 0, j)),
                  pl.BlockSpec((None, None, d, tf), lambda i, j: (layer, which, 0, j)),
                  pl.BlockSpec((None, None, tf, d), lambda i, j: (layer, which, j, 0)),
                  pl.BlockSpec((1, d), lambda i, j: (0, 0))],
        out_specs=pl.BlockSpec((tm, d), lambda i, j: (i, 0)),
        out_shape=jax.ShapeDtypeStruct((t, d), F32),
        scratch_shapes=[pltpu.VMEM((tm, d), BF16), pltpu.VMEM((tm, d), F32)],
        compiler_params=_cparams(("parallel", "arbitrary")),
        name="ffn",
    )(x, mods, norm_w.reshape(DEPTH, 3, 1, d), w_gate, w_up, w_down, final_w.reshape(1, d))


def _proj_kernel(x_ref, mod_ref, nw_ref, w_ref, cw_ref, o_ref, h_ref, wb_ref, *, rowlen):
    j = pl.program_id(0)
    i = pl.program_id(1)

    @pl.when(i == 0)
    def _():
        wb_ref[...] = w_ref[...].astype(BF16)

    @pl.when(j == 0)
    def _():
        m = mod_ref[...]
        h_ref[i] = _norm_mod(x_ref[...], nw_ref[...], m[4:5], m[3:4]).astype(BF16)
        o_ref[...] = jnp.dot(h_ref[i], wb_ref[...], preferred_element_type=F32)
        tm = o_ref.shape[0]
        pos = lax.broadcasted_iota(jnp.int32, (tm, 1), 0) % rowlen
        valid = [jnp.logical_and(pos + (t - GDN_CONV // 2) >= 0, pos + (t - GDN_CONV // 2) < rowlen)
                 for t in range(GDN_CONV)]
        for g in range(3 * GDN_HEADS):
            cols = slice(P_DQKV + g * GDN_DK, P_DQKV + (g + 1) * GDN_DK)
            xg = o_ref[:, cols]
            acc = xg * cw_ref[GDN_CONV // 2:GDN_CONV // 2 + 1, cols]
            for t in range(GDN_CONV):
                s = t - GDN_CONV // 2
                if s != 0:
                    xs = pltpu.roll(xg, (-s) % tm, axis=0)
                    acc = acc + jnp.where(valid[t], xs, 0.0) * cw_ref[t:t + 1, cols]
            y = _silu(acc)
            if g < 2 * GDN_HEADS:
                y = y * lax.rsqrt(jnp.sum(y * y, axis=-1, keepdims=True) + EPS)
                if g < GDN_HEADS:
                    y = y * (GDN_DK ** -0.5)
            o_ref[:, cols] = y

    @pl.when(j == 1)
    def _():
        o_ref[...] = jnp.dot(h_ref[i], wb_ref[...], preferred_element_type=F32)


def _proj(x, mods, norm_w, w_in_p, conv_w, *, layer, n_seq, rowlen):
    t, d = x.shape
    nb = mods.shape[0]
    tm = _pick_tile(n_seq if nb > 1 else t, 512)
    midx = _mod_index(nb, tm, n_seq)
    return pl.pallas_call(
        functools.partial(_proj_kernel, rowlen=rowlen),
        grid=(PROJ_W // PROJ_TN, t // tm),
        in_specs=[pl.BlockSpec((tm, d), lambda j, i: (i * (1 - j), 0)),
                  pl.BlockSpec((None, N_MOD, d), lambda j, i: (midx(i), 0, 0)),
                  pl.BlockSpec((None, None, 1, d), lambda j, i: (layer, 1, 0, 0)),
                  pl.BlockSpec((None, d, PROJ_TN), lambda j, i: (layer, 0, j)),
                  pl.BlockSpec((None, GDN_CONV, 3 * GDN_W), lambda j, i: (layer, 0, 0))],
        out_specs=pl.BlockSpec((tm, PROJ_TN), lambda j, i: (i, j)),
        out_shape=jax.ShapeDtypeStruct((t, PROJ_W), F32),
        scratch_shapes=[pltpu.VMEM((t // tm, tm, d), BF16), pltpu.VMEM((d, PROJ_TN), BF16)],
        compiler_params=_cparams(("arbitrary", "arbitrary")),
        name="proj",
    )(x, mods, norm_w.reshape(DEPTH, 3, 1, d), w_in_p, conv_w)


def _permute_w_in(w_in):
    def seg(i):
        return w_in[:, :, _IN_OFF[i]:_IN_OFF[i + 1]]

    def pad(n):
        return jnp.zeros(w_in.shape[:2] + (n,), w_in.dtype)

    sm_pad = 128 - GLA_RANK - 2 * GDN_HEADS
    return jnp.concatenate(
        [seg(_DQ), seg(_DK), seg(_DV),
         seg(_GLRF), seg(_DAF), seg(_DBF), pad(sm_pad),
         seg(_GLRB), seg(_DAB), seg(_DBB), pad(sm_pad),
         seg(_GQ), seg(_GK), seg(_GV), seg(_GOG), seg(_DOG), seg(_SU)], axis=-1)


def _np_consts():
    t = np.arange(CHUNK)
    lower = (t[:, None] >= t[None, :])
    incl = np.stack([lower, lower.T])
    strict = np.stack([t[:, None] > t[None, :], t[:, None] < t[None, :]])
    eye4 = np.eye(4, dtype=bool)
    blk = np.kron(eye4, np.ones((CHUNK, CHUNK), bool))
    incl_bd = np.stack([np.kron(eye4, incl[d]) for d in range(2)])
    strict_bd = np.stack([np.kron(eye4, strict[d]) for d in range(2)])
    levels = []
    for d in range(2):
        per = []
        for lv in range(6):
            s = 1 << lv
            same = (t[:, None] // (2 * s)) == (t[None, :] // (2 * s))
            hi = (t % (2 * s)) >= s
            m = same & hi[:, None] & (~hi)[None, :]
            if d == 1:
                m = m.T
            per.append(np.kron(eye4, m))
        levels.append(np.stack(per))
    levels = np.stack(levels)
    assert (levels.sum(1) == strict_bd).all() and blk.any()
    gla_mask = np.stack([np.tile(incl[d], (1, 4)) for d in range(2)])
    hm_k = np.kron(eye4, np.ones((1, GLA_DK)))
    hm_v = np.kron(eye4, np.ones((1, GLA_DV)))
    st_mask = np.kron(eye4, np.ones((GLA_DV, GLA_DK)))
    ind64 = np.kron(eye4, np.full((GLA_DV, GLA_DV), 1.0 / GLA_DV))
    return dict(incl=incl, incl_bd=incl_bd, strict_bd=strict_bd, levels=levels, gla_mask=gla_mask,
                hm_k=hm_k, hm_v=hm_v, st_mask=st_mask, ind64=ind64, eye256=np.eye(256))


_C = _np_consts()


def _row_block_index(nblk):
    return lambda d, b, i: b * nblk + i + d * (nblk - 1 - 2 * i)


def _gla_kernel(qk_ref, v_ref, sm_ref, up_ref, bias_ref, tri_ref, mask_ref, hmk_ref, hmv_ref, stm_ref, *rest,
                nchunk, zero_init):
    if zero_init:
        o_ref, stf_ref, st_ref, qd_sc, m_sc, dec_sc = rest
    else:
        st0_ref, o_ref, stf_ref, st_ref, qd_sc, m_sc, dec_sc = rest
    d = pl.program_id(0)

    @pl.when(pl.program_id(2) == 0)
    def _():
        st_ref[...] = jnp.zeros_like(st_ref) if zero_init else st0_ref[...]

    tri = tri_ref[...]
    mask = mask_ref[...]
    hmk = hmk_ref[...]
    hmv = hmv_ref[...]
    stm = stm_ref[...]

    for c in range(nchunk):
        rows = pl.ds(c * CHUNK, CHUNK)
        qk = qk_ref[rows, :]
        q = qk[:, 0:GLA_KW] * (GLA_DK ** -0.5)
        k = qk[:, GLA_KW:2 * GLA_KW]
        v = v_ref[rows, :]
        x = _bdot(sm_ref[rows, :], up_ref[...]) + bias_ref[...]
        lg = _log_sigmoid(x) * (1.0 / GLA_GATE_NORM)
        b = _sel_dot(tri, lg)
        btot = jnp.sum(lg, axis=0, keepdims=True)
        qd = (q * jnp.exp(b)).astype(BF16)
        ki = k * jnp.exp(-b)
        kd = k * jnp.exp(btot - b)
        kexp = jnp.concatenate([ki * hmk[h:h + 1] for h in range(GLA_HEADS)], axis=0)
        att = _bdot_nt(qd, kexp) * mask
        vexp = jnp.concatenate([v * hmv[h:h + 1] for h in range(GLA_HEADS)], axis=0)
        o_ref[rows, :] = _bdot(att, vexp)
        qd_sc[c] = qd
        m_sc[c] = _bdot(v.T, kd) * stm
        dec_sc[c] = jnp.broadcast_to(jnp.exp(btot), (8, GLA_KW))

    st = st_ref[...]
    for ci in range(nchunk):
        c = ci + d * (nchunk - 1 - 2 * ci)
        rows = pl.ds(pl.multiple_of(c * CHUNK, CHUNK), CHUNK)
        o_ref[rows, :] += lax.dot_general(qd_sc[c], st.astype(BF16), (((1,), (1,)), ((), ())),
                                          preferred_element_type=F32)
        st = st * dec_sc[c][0:1] + m_sc[c]
    st_ref[...] = st
    stf_ref[...] = st


def _gla(proj, up_p, bias_p, st0, *, nseq, n_seq):
    t = proj.shape[0]
    nblk = n_seq // SUB
    nck = SUB // CHUNK
    rbi = _row_block_index(nblk)
    c = _C
    consts = [jnp.asarray(c["incl"], BF16), jnp.asarray(c["gla_mask"], F32), jnp.asarray(c["hm_k"], F32),
              jnp.asarray(c["hm_v"], F32), jnp.asarray(c["st_mask"], F32)]
    st_spec = pl.BlockSpec((None, None, GLA_W, GLA_KW), lambda d, b, i: (d, b, 0, 0))
    zero_init = st0 is None
    return pl.pallas_call(
        functools.partial(_gla_kernel, nchunk=nck, zero_init=zero_init),
        grid=(2, nseq, nblk),
        in_specs=[pl.BlockSpec((SUB, 2 * GLA_KW), lambda d, b, i: (rbi(d, b, i), P_GQK // (2 * GLA_KW))),
                  pl.BlockSpec((SUB, GLA_W), lambda d, b, i: (rbi(d, b, i), P_GV // GLA_W)),
                  pl.BlockSpec((SUB, 128), lambda d, b, i: (rbi(d, b, i), P_SMALL // 128 + d)),
                  pl.BlockSpec((None, 128, GLA_KW), lambda d, b, i: (d, 0, 0)),
                  pl.BlockSpec((None, 1, GLA_KW), lambda d, b, i: (d, 0, 0)),
                  pl.BlockSpec((None, CHUNK, CHUNK), lambda d, b, i: (d, 0, 0)),
                  pl.BlockSpec((None, CHUNK, 256), lambda d, b, i: (d, 0, 0)),
                  pl.BlockSpec((GLA_HEADS, GLA_KW), lambda d, b, i: (0, 0)),
                  pl.BlockSpec((GLA_HEADS, GLA_W), lambda d, b, i: (0, 0)),
                  pl.BlockSpec((GLA_W, GLA_KW), lambda d, b, i: (0, 0))] + ([] if zero_init else [st_spec]),
        out_specs=[pl.BlockSpec((None, SUB, GLA_W), lambda d, b, i: (d, rbi(d, b, i), 0)), st_spec],
        out_shape=[jax.ShapeDtypeStruct((2, t, GLA_W), F32),
                   jax.ShapeDtypeStruct((2, nseq, GLA_W, GLA_KW), F32)],
        scratch_shapes=[pltpu.VMEM((GLA_W, GLA_KW), F32), pltpu.VMEM((nck, CHUNK, GLA_KW), BF16),
                        pltpu.VMEM((nck, GLA_W, GLA_KW), F32), pltpu.VMEM((nck, 8, GLA_KW), F32)],
        compiler_params=_cparams(("arbitrary", "arbitrary", "arbitrary")),
        name="gla",
    )(proj, proj, proj, up_p, bias_p, *consts, *([] if zero_init else [st0]))


def _gdn_kernel(qkv_ref, sm_ref, arow_ref, dtb_ref, tri_ref, incl_ref, strict_ref, lvl_ref, eye_ref, *rest,
                nchunk, zero_init):
    if zero_init:
        o_ref, sf_ref, s_ref, la_ref, be_ref, w_sc, u0_sc, qkd_sc, qe_sc, kdt_sc, egl_sc = rest
    else:
        s0_ref, o_ref, sf_ref, s_ref, la_ref, be_ref, w_sc, u0_sc, qkd_sc, qe_sc, kdt_sc, egl_sc = rest
    d = pl.program_id(0)

    @pl.when(pl.program_id(2) == 0)
    def _():
        s_ref[...] = jnp.zeros_like(s_ref) if zero_init else s0_ref[...]

    sm = sm_ref[...]
    la_ref[...] = -jnp.exp(arow_ref[...]) * _softplus(sm + dtb_ref[...])
    be_ref[...] = jax.nn.sigmoid(sm)

    tri = tri_ref[...]
    incl = incl_ref[...]
    strict = strict_ref[...]
    eye = eye_ref[...]

    def stack(ref, r0, lo, width):
        return jnp.concatenate([ref[pl.ds(r0, CHUNK), lo + h * width:lo + (h + 1) * width]
                                for h in range(GDN_HEADS)], axis=0)

    def colstack(a, lo):
        return jnp.concatenate([a[:, lo + h:lo + h + 1] for h in range(GDN_HEADS)], axis=0)

    lvl = [lvl_ref[lv] for lv in range(6)]

    loc = []
    for c in range(nchunk):
        r0 = c * CHUNK
        qs = stack(qkv_ref, r0, 0, GDN_DK)
        ks = stack(qkv_ref, r0, GDN_W, GDN_DK)
        vs = stack(qkv_ref, r0, 2 * GDN_W, GDN_DV)
        la = la_ref[pl.ds(r0, CHUNK), :]
        g = _sel_dot(tri, la)
        gtot = jnp.sum(la, axis=0, keepdims=True)
        gcol = colstack(g, SM_A)
        kfcol = colstack(jnp.exp(gtot - g), SM_A)
        egcol = colstack(jnp.exp(g), SM_A)
        bcol = colstack(be_ref[pl.ds(r0, CHUNK), :], SM_B)
        grow = jnp.sum(eye * gcol, axis=0, keepdims=True)
        dec = incl * jnp.exp(jnp.minimum(gcol - grow, 0.0))
        ksb = ks.astype(BF16)
        lmat = strict * dec * _bdot_nt(ksb, ksb) * bcol
        rhs = jnp.concatenate([ks * (bcol * egcol), vs * bcol], axis=1).astype(BF16)
        qkd_sc[c] = (_bdot_nt(qs, ksb) * dec).astype(BF16)
        qe_sc[c] = (qs * egcol).astype(BF16)
        kd = ks * kfcol
        for h in range(GDN_HEADS):
            kdt_sc[c * GDN_HEADS + h] = kd[h * CHUNK:(h + 1) * CHUNK].T.astype(BF16)
        egl_sc[c] = jnp.broadcast_to(jnp.exp(gtot), (8, 128))
        loc.append((lmat.astype(BF16), rhs))
    tinv = [eye - (lm * lvl[0]).astype(F32) for lm, _ in loc]
    for lv in range(1, 6):
        tb = [t_.astype(BF16) for t_ in tinv]
        cs = [jnp.dot(lm * lvl[lv], t_, preferred_element_type=F32) for (lm, _), t_ in zip(loc, tb)]
        tinv = [t32 - jnp.dot(t_, c_.astype(BF16), preferred_element_type=F32)
                for t32, t_, c_ in zip(tinv, tb, cs)]
    for c in range(nchunk):
        sol = jnp.dot(tinv[c].astype(BF16), loc[c][1], preferred_element_type=F32)
        w_sc[c] = sol[:, :GDN_DK].astype(BF16)
        u0_sc[c] = sol[:, GDN_DK:]

    s_cur = [s_ref[h] for h in range(GDN_HEADS)]
    for ci in range(nchunk):
        c = ci + d * (nchunk - 1 - 2 * ci)
        r0 = pl.multiple_of(c * CHUNK, CHUNK)
        sb = [s_.astype(BF16) for s_ in s_cur]
        us = []
        for h in range(GDN_HEADS):
            rows = pl.ds(h * CHUNK, CHUNK)
            us.append(u0_sc[c, rows, :] - jnp.dot(w_sc[c, rows, :], sb[h], preferred_element_type=F32))
        ub = [u_.astype(BF16) for u_ in us]
        o2 = jnp.dot(qkd_sc[c], jnp.concatenate(ub, axis=0), preferred_element_type=F32)
        egl = egl_sc[c]
        for h in range(GDN_HEADS):
            rows = pl.ds(h * CHUNK, CHUNK)
            o_ref[pl.ds(r0, CHUNK), h * GDN_DV:(h + 1) * GDN_DV] = (
                jnp.dot(qe_sc[c, rows, :], sb[h], preferred_element_type=F32) + o2[h * CHUNK:(h + 1) * CHUNK])
            s_cur[h] = (egl[0:1, SM_A + h:SM_A + h + 1] * s_cur[h]
                        + jnp.dot(kdt_sc[c * GDN_HEADS + h], ub[h], preferred_element_type=F32))
    for h in range(GDN_HEADS):
        s_ref[h] = s_cur[h]
    sf_ref[...] = s_ref[...]


def _gdn(proj, arow, dtb, s0, *, nseq, n_seq):
    t = proj.shape[0]
    nblk = n_seq // SUB
    rbi = _row_block_index(nblk)
    c = _C
    consts = [jnp.asarray(c["incl"], BF16), jnp.asarray(c["incl_bd"], F32), jnp.asarray(c["strict_bd"], F32),
              jnp.asarray(c["levels"], BF16), jnp.asarray(c["eye256"], F32)]
    hs = (GDN_HEADS, GDN_DK, GDN_DV)
    nck = SUB // CHUNK
    zero_init = s0 is None
    s_spec = pl.BlockSpec((None, None) + hs, lambda d, b, i: (d, b, 0, 0, 0))
    return pl.pallas_call(
        functools.partial(_gdn_kernel, nchunk=nck, zero_init=zero_init),
        grid=(2, nseq, nblk),
        in_specs=[pl.BlockSpec((SUB, 3 * GDN_W), lambda d, b, i: (rbi(d, b, i), P_DQKV // (3 * GDN_W))),
                  pl.BlockSpec((SUB, 128), lambda d, b, i: (rbi(d, b, i), P_SMALL // 128 + d)),
                  pl.BlockSpec((None, 1, 128), lambda d, b, i: (d, 0, 0)),
                  pl.BlockSpec((None, 1, 128), lambda d, b, i: (d, 0, 0)),
                  pl.BlockSpec((None, CHUNK, CHUNK), lambda d, b, i: (d, 0, 0)),
                  pl.BlockSpec((None, 256, 256), lambda d, b, i: (d, 0, 0)),
                  pl.BlockSpec((None, 256, 256), lambda d, b, i: (d, 0, 0)),
                  pl.BlockSpec((None, 6, 256, 256), lambda d, b, i: (d, 0, 0, 0)),
                  pl.BlockSpec((256, 256), lambda d, b, i: (0, 0))] + ([] if zero_init else [s_spec]),
        out_specs=[pl.BlockSpec((None, SUB, GDN_W), lambda d, b, i: (d, rbi(d, b, i), 0)), s_spec],
        out_shape=[jax.ShapeDtypeStruct((2, t, GDN_W), F32),
                   jax.ShapeDtypeStruct((2, nseq) + hs, F32)],
        scratch_shapes=[pltpu.VMEM(hs, F32),
                        pltpu.VMEM((SUB, 128), F32), pltpu.VMEM((SUB, 128), F32),
                        pltpu.VMEM((nck, 256, GDN_DK), BF16), pltpu.VMEM((nck, 256, GDN_DV), F32),
                        pltpu.VMEM((nck, 256, 256), BF16), pltpu.VMEM((nck, 256, GDN_DK), BF16),
                        pltpu.VMEM((nck * GDN_HEADS, GDN_DK, CHUNK), BF16), pltpu.VMEM((nck, 8, 128), F32)],
        compiler_params=_cparams(("arbitrary", "arbitrary", "arbitrary")),
        name="gdn",
    )(proj, proj, arow, dtb, *consts, *([] if zero_init else [s0]))


def _cmul(ar, ai, br, bi):
    return ar * br - ai * bi, ar * bi + ai * br


def _split2(x):
    hi = x.astype(BF16)
    return hi, (x - hi.astype(F32)).astype(BF16)


def _s5_consts_kernel(lr_ref, li_ref, st_ref, br_ref, bi_ref, cr_ref, ci_ref, gm_ref,
                      laml_ref, w_ref, v_ref, k_ref):
    lr = lr_ref[...]
    li = li_ref[...]
    step = jnp.exp(st_ref[...])
    mag = jnp.exp(lr * step)
    lbr = mag * jnp.cos(li * step)
    lbi = mag * jnp.sin(li * step)
    nr = lbr - 1.0
    den = lr * lr + li * li
    fr = (nr * lr + lbi * li) / den
    fi = (lbi * lr - nr * li) / den
    bbr, bbi = _cmul(fr, fi, br_ref[...], bi_ref[...])
    cr = cr_ref[...]
    ci = ci_ref[...]
    pr = jnp.ones_like(lr)
    pi = jnp.zeros_like(lr)
    a_r, a_i = [], []
    for s in range(S5_L):
        rows = slice(s * S5_CH, (s + 1) * S5_CH)
        ar, ai = _cmul(cr, ci, pr, pi)
        a_r.append(ar)
        a_i.append(ai)
        wr, wi = _cmul(pr, pi, bbr, bbi)
        wrows = slice((S5_L - 1 - s) * S5_CH, (S5_L - s) * S5_CH)
        w_ref[0, wrows, :] = wr
        w_ref[1, wrows, :] = wi
        pr, pi = _cmul(pr, pi, lbr, lbi)
        vr, vi = _cmul(cr, ci, pr, pi)
        v_ref[0, rows, :] = vr
        v_ref[1, rows, :] = -vi
    laml_ref[0:1, :] = pr
    laml_ref[1:2, :] = pi
    gm = gm_ref[...]
    acc = None
    for a, b, sign in ((a_r, bbr, 1.0), (a_i, bbi, -1.0)):
        ah, al = _split2(jnp.concatenate(a, axis=0))
        bh, bl = _split2(jnp.concatenate([b] * S5_GROUPS, axis=0) * gm)
        nt = lambda x, y: lax.dot_general(x, y, (((1,), (1,)), ((), ())), preferred_element_type=F32)
        term = (nt(ah, bh) + nt(ah, bl) + nt(al, bh)) * sign
        acc = term if acc is None else acc + term
    k_ref[...] = acc


def _s5_consts(lam_re, lam_im, log_step, b_re, b_im, c_re, c_im):
    n = DEPTH * 2
    lc = S5_L * S5_CH
    lr = lam_re.reshape(n, 1, S5_N)
    li = lam_im.reshape(n, 1, S5_N)
    st = jnp.repeat(log_step.reshape(n, S5_GROUPS), S5_P, axis=-1).reshape(n, 1, S5_N)
    br = b_re.reshape(n, S5_N, S5_CH).transpose(0, 2, 1)
    bi = b_im.reshape(n, S5_N, S5_CH).transpose(0, 2, 1)
    cr = c_re.reshape(n, S5_GROUPS, S5_CH, S5_P).transpose(0, 2, 1, 3).reshape(n, S5_CH, S5_N)
    ci = c_im.reshape(n, S5_GROUPS, S5_CH, S5_P).transpose(0, 2, 1, 3).reshape(n, S5_CH, S5_N)
    gm = jnp.asarray(np.kron(np.eye(S5_GROUPS), np.ones((S5_CH, S5_P))), F32)
    row = pl.BlockSpec((None, 1, S5_N), lambda i: (i, 0, 0))
    mat = pl.BlockSpec((None, S5_CH, S5_N), lambda i: (i, 0, 0))
    big = pl.BlockSpec((None, 2, lc, S5_N), lambda i: (i, 0, 0, 0))
    return pl.pallas_call(
        _s5_consts_kernel,
        grid=(n,),
        in_specs=[row, row, row, mat, mat, mat, mat, pl.BlockSpec((S5_W, S5_N), lambda i: (0, 0))],
        out_specs=[pl.BlockSpec((None, 2, S5_N), lambda i: (i, 0, 0)), big, big,
                   pl.BlockSpec((None, lc, S5_W), lambda i: (i, 0, 0))],
        out_shape=[jax.ShapeDtypeStruct((n, 2, S5_N), F32),
                   jax.ShapeDtypeStruct((n, 2, lc, S5_N), F32),
                   jax.ShapeDtypeStruct((n, 2, lc, S5_N), F32),
                   jax.ShapeDtypeStruct((n, lc, S5_W), F32)],
        compiler_params=_cparams(("parallel",)),
        name="s5_consts",
    )(lr, li, st, br, bi, cr, ci, gm)


def _s5_assemble(laml, wall, vall, kall):
    lc = S5_L * S5_CH
    eye2 = jnp.eye(2, dtype=F32)

    def time_order(a, axis):
        a = a.reshape((DEPTH, 2) + a.shape[1:])
        return jnp.stack([a[:, 0], jnp.flip(a[:, 1], axis=axis)], axis=1)

    w = time_order(wall.reshape(DEPTH * 2, 2, S5_L, S5_CH, S5_PAIRS, 2, S5_P), 2)
    wp = jnp.einsum('ldrscgjp,ij->ldrgiscjp', w, eye2).reshape(DEPTH, 2, 2, S5_PAIRS, 2 * lc, 128)
    v = time_order(vall.reshape(DEPTH * 2, 2, S5_L, S5_CH, S5_PAIRS, 2, S5_P), 2)
    vp = jnp.einsum('ldrscgjp,ji->ldrgjpisc', v, eye2).reshape(DEPTH, 2, 2, S5_PAIRS, 128, 2 * lc)
    k = kall.reshape(DEPTH, 2, S5_L, S5_CH, S5_GROUPS, S5_CH)
    s = np.arange(S5_L)
    ts = []
    for d in range(2):
        lag = (s[None, :] - s[:, None]) if d == 0 else (s[:, None] - s[None, :])
        kt = jnp.take(k[:, d], jnp.asarray(np.clip(lag, 0, S5_L - 1).reshape(-1)), axis=1)
        kt = kt.reshape(DEPTH, S5_L, S5_L, S5_CH, S5_GROUPS, S5_CH) * jnp.asarray(
            (lag >= 0).astype(np.float32))[None, :, :, None, None, None]
        ts.append(kt.transpose(0, 4, 1, 5, 2, 3).reshape(DEPTH, S5_GROUPS, lc, lc))
    t = jnp.stack(ts, axis=1)
    return (laml.reshape(DEPTH, 2, 2, S5_PAIRS, 128), wp.astype(BF16), vp.astype(BF16), t.astype(BF16))


def _s5_kernel(u_ref, t_ref, wre_ref, wim_ref, vre_ref, vim_ref, laml_ref, *rest, nseq, nblk, zero_init):
    if zero_init:
        y_ref, hf_ref, hin_re, hin_im, hp_re, hp_im = rest
    else:
        h0_ref, y_ref, hf_ref, hin_re, hin_im, hp_re, hp_im = rest
    d = pl.program_id(0)
    r = nblk * nseq

    for gp in range(S5_PAIRS):
        up = jnp.concatenate([u_ref[2 * gp], u_ref[2 * gp + 1]], axis=1).astype(BF16)
        rows = pl.ds(gp, r, stride=S5_PAIRS)
        hin_re[rows, :] = jnp.dot(up, wre_ref[gp], preferred_element_type=F32)
        hin_im[rows, :] = jnp.dot(up, wim_ref[gp], preferred_element_type=F32)

    bs = nseq * S5_PAIRS
    lr = jnp.concatenate([laml_ref[0]] * nseq, axis=0)
    li = jnp.concatenate([laml_ref[1]] * nseq, axis=0)

    def step(kk, carry):
        hr, hi = carry
        k = kk + d * (nblk - 1 - 2 * kk)
        rows = pl.ds(pl.multiple_of(k * bs, bs), bs)
        hp_re[rows, :] = hr
        hp_im[rows, :] = hi
        return lr * hr - li * hi + hin_re[rows, :], lr * hi + li * hr + hin_im[rows, :]

    if zero_init:
        init = (jnp.zeros((bs, 128), F32), jnp.zeros((bs, 128), F32))
    else:
        init = (h0_ref[0], h0_ref[1])
    hr, hi = lax.fori_loop(0, nblk, step, init)
    hf_ref[0] = hr
    hf_ref[1] = hi

    for gp in range(S5_PAIRS):
        rows = pl.ds(gp, r, stride=S5_PAIRS)
        ys = (jnp.dot(hp_re[rows, :].astype(BF16), vre_ref[gp], preferred_element_type=F32)
              + jnp.dot(hp_im[rows, :].astype(BF16), vim_ref[gp], preferred_element_type=F32))
        for j in range(2):
            g = 2 * gp + j
            lc = S5_L * S5_CH
            y_ref[g] = (jnp.dot(u_ref[g].astype(BF16), t_ref[g], preferred_element_type=F32)
                        + ys[:, j * lc:(j + 1) * lc])


def _s5(u_g, consts, h0, *, layer, nseq, n_seq):
    laml, wp, vp, t = consts
    nblk = n_seq // S5_L
    r = nblk * nseq
    lc = S5_L * S5_CH
    bs = nseq * S5_PAIRS
    zero_init = h0 is None
    h_spec = pl.BlockSpec((None, 2, bs, 128), lambda d: (d, 0, 0, 0))

    def ri(k, shape):
        return pl.BlockSpec((None, None, None) + shape, lambda d: (layer, d, k) + (0,) * len(shape))

    return pl.pallas_call(
        functools.partial(_s5_kernel, nseq=nseq, nblk=nblk, zero_init=zero_init),
        grid=(2,),
        in_specs=[pl.BlockSpec((S5_GROUPS, r, lc), lambda d: (0, 0, 0)),
                  pl.BlockSpec((None, None, S5_GROUPS, lc, lc), lambda d: (layer, d, 0, 0, 0)),
                  ri(0, (S5_PAIRS, 2 * lc, 128)), ri(1, (S5_PAIRS, 2 * lc, 128)),
                  ri(0, (S5_PAIRS, 128, 2 * lc)), ri(1, (S5_PAIRS, 128, 2 * lc)),
                  pl.BlockSpec((None, None, 2, S5_PAIRS, 128), lambda d: (layer, d, 0, 0, 0))]
                 + ([] if zero_init else [h_spec]),
        out_specs=[pl.BlockSpec((None, S5_GROUPS, r, lc), lambda d: (d, 0, 0, 0)), h_spec],
        out_shape=[jax.ShapeDtypeStruct((2, S5_GROUPS, r, lc), F32),
                   jax.ShapeDtypeStruct((2, 2, bs, 128), F32)],
        scratch_shapes=[pltpu.VMEM((r * S5_PAIRS, 128), F32) for _ in range(4)],
        compiler_params=_cparams(("arbitrary",)),
        name="s5",
    )(u_g, t, wp, wp, vp, vp, laml, *([] if zero_init else [h0]))


def _out_kernel(x_ref, mod_ref, ogf_ref, ogb_ref, odf_ref, odb_ref, ysf_ref, ysb_ref, gog_ref, su_ref, dog_ref,
                ind_ref, gw_ref, dw_ref, sd_ref, wglu_ref, bglu_ref, wout_ref, o_ref):
    og = ogf_ref[...] + ogb_ref[...]
    hi, mid, _ = _split3(og * og)
    ms = jnp.dot(hi, ind_ref[...], preferred_element_type=F32) + jnp.dot(mid, ind_ref[...], preferred_element_type=F32)
    og = og * lax.rsqrt(ms + EPS) * gw_ref[...] * _silu(gog_ref[...])
    od = odf_ref[...] + odb_ref[...]
    dog = dog_ref[...]
    parts = []
    for h in range(GDN_HEADS):
        z = od[:, h * GDN_DV:(h + 1) * GDN_DV]
        z = z * lax.rsqrt(jnp.mean(z * z, axis=-1, keepdims=True) + EPS) * dw_ref[...]
        parts.append(z * _silu(dog[:, h * GDN_DV:(h + 1) * GDN_DV]))
    y = su_ref[...] * sd_ref[...] + ysf_ref[...] + ysb_ref[...]
    g = 0.5 * y * (1.0 + jnp.tanh(math.sqrt(2.0 / math.pi) * (y + 0.044715 * (y * y * y))))
    os_ = g * jax.nn.sigmoid(_bdot(g, wglu_ref[...]) + bglu_ref[...])
    out = _bdot(og, wout_ref[0:GLA_W, :]) + _bdot(os_, wout_ref[GLA_W + GDN_W:MIX_W, :])
    for h in range(GDN_HEADS):
        lo = GLA_W + h * GDN_DV
        out = out + _bdot(parts[h], wout_ref[lo:lo + GDN_DV, :])
    o_ref[...] = x_ref[...] + mod_ref[5:6, :] * out


def _mix_out(x, mods, proj, og, od, ys, gla_nw, gdn_nw, s5_d, w_glu, b_glu, w_out, *, layer, n_seq):
    t, d = x.shape
    nb = mods.shape[0]
    tm = _pick_tile(n_seq if nb > 1 else t, 512)
    midx = _mod_index(nb, tm, n_seq)
    ys2 = ys

    def dirspec(w, dd):
        return pl.BlockSpec((None, tm, w), lambda i: (dd, i, 0))

    def vec(w):
        return pl.BlockSpec((None, 1, w), lambda i: (layer, 0, 0))

    return pl.pallas_call(
        _out_kernel,
        grid=(t // tm,),
        in_specs=[pl.BlockSpec((tm, d), lambda i: (i, 0)),
                  pl.BlockSpec((None, N_MOD, d), lambda i: (midx(i), 0, 0)),
                  dirspec(GLA_W, 0), dirspec(GLA_W, 1), dirspec(GDN_W, 0), dirspec(GDN_W, 1),
                  dirspec(S5_W, 0), dirspec(S5_W, 1),
                  pl.BlockSpec((tm, GLA_W), lambda i: (i, P_GOG // GLA_W)),
                  pl.BlockSpec((tm, S5_W), lambda i: (i, P_SU // S5_W)),
                  pl.BlockSpec((tm, GDN_W), lambda i: (i, P_DOG // GDN_W)),
                  pl.BlockSpec((GLA_W, GLA_W), lambda i: (0, 0)),
                  vec(GLA_W), vec(GDN_DV), vec(S5_W),
                  pl.BlockSpec((None, S5_W, S5_W), lambda i: (layer, 0, 0)),
                  vec(S5_W),
                  pl.BlockSpec((None, MIX_W, d), lambda i: (layer, 0, 0))],
        out_specs=pl.BlockSpec((tm, d), lambda i: (i, 0)),
        out_shape=jax.ShapeDtypeStruct((t, d), F32),
        compiler_params=_cparams(("parallel",)),
        name="mix_out",
    )(x, mods, og, og, od, od, ys2, ys2, proj, proj, proj,
      jnp.asarray(_C["ind64"], BF16),
      jnp.tile(gla_nw, (1, GLA_HEADS)).reshape(DEPTH, 1, GLA_W), gdn_nw.reshape(DEPTH, 1, GDN_DV),
      s5_d.reshape(DEPTH, 1, S5_W), w_glu, b_glu.reshape(DEPTH, 1, S5_W), w_out)


def _gla_params(gk_up, gk_bias, layer):
    up = jnp.zeros((2, 128, GLA_KW), F32).at[:, :GLA_RANK, :].set(gk_up[layer])
    return up, gk_bias[layer].reshape(2, 1, GLA_KW)


def _gdn_params(a_log, dt_bias, layer):
    arow = jnp.zeros((2, 1, 128), F32).at[:, 0, SM_A:SM_A + GDN_HEADS].set(a_log[layer])
    dtb = jnp.zeros((2, 1, 128), F32).at[:, 0, SM_A:SM_A + GDN_HEADS].set(dt_bias[layer])
    return arow, dtb


def _gla_state_in(s):
    eye = jnp.eye(GLA_HEADS, dtype=F32)
    st = jnp.einsum('bzhde,hg->zbhegd', s.astype(F32), eye)
    return st.reshape(2, s.shape[0], GLA_W, GLA_KW)


def _gla_state_out(st):
    b = st.shape[1]
    s = st.reshape(2, b, GLA_HEADS, GLA_DV, GLA_HEADS, GLA_DK)
    s = jnp.stack([s[:, :, h, :, h, :] for h in range(GLA_HEADS)], axis=2)
    return s.transpose(1, 0, 2, 4, 3)


def _trunk(x3, mods, rows, states, p, s5c, w_in_p):
    nseq, n_seq, d = x3.shape
    x = x3.reshape(nseq * n_seq, d)
    fin = []
    for l in range(DEPTH):
        m = mods[l]
        x = _ffn(x, m, p['norm_w'], p['ffn_w_gate'], p['ffn_w_up'], p['ffn_w_down'], p['final_norm_w'],
                 layer=l, which=0, n_seq=n_seq, final=False)
        proj = _proj(x, m, p['norm_w'], w_in_p, p['gdn_conv_w'], layer=l, n_seq=n_seq, rowlen=n_seq // rows)
        if states is None:
            st0 = s0 = h0 = None
        else:
            st_gla, st_gdn, st_re, st_im = states
            st0 = _gla_state_in(st_gla[:, l])
            s0 = st_gdn[:, l].transpose(1, 0, 2, 3, 4)
            h0 = jnp.stack([st_re[:, l].reshape(nseq, 2, S5_N), st_im[:, l].reshape(nseq, 2, S5_N)], axis=0)
            h0 = h0.transpose(2, 0, 1, 3).reshape(2, 2, nseq * S5_PAIRS, 128)
        up_p, bias_p = _gla_params(p['gla_gk_up'], p['gla_gk_bias'], l)
        og, gla_f = _gla(proj, up_p, bias_p, st0, nseq=nseq, n_seq=n_seq)
        arow, dtb = _gdn_params(p['gdn_a_log'], p['gdn_dt_bias'], l)
        od, gdn_f = _gdn(proj, arow, dtb, s0, nseq=nseq, n_seq=n_seq)
        nblk = n_seq // S5_L
        u_g = proj[:, P_SU:P_SU + S5_W].reshape(nseq, nblk, S5_L, S5_GROUPS, S5_CH)
        u_g = u_g.transpose(3, 1, 0, 2, 4).reshape(S5_GROUPS, nblk * nseq, S5_L * S5_CH)
        ys_g, s5_f = _s5(u_g, s5c, h0, layer=l, nseq=nseq, n_seq=n_seq)
        ys = ys_g.reshape(2, S5_GROUPS, nblk, nseq, S5_L, S5_CH).transpose(0, 3, 2, 4, 1, 5)
        ys = ys.reshape(2, nseq * n_seq, S5_W)
        x = _mix_out(x, m, proj, og, od, ys, p['gla_norm_w'], p['gdn_norm_w'], p['s5_d'], p['s5_w_glu'],
                     p['s5_b_glu'], p['w_out'], layer=l, n_seq=n_seq)
        x = _ffn(x, m, p['norm_w'], p['ffn_w_gate'], p['ffn_w_up'], p['ffn_w_down'], p['final_norm_w'],
                 layer=l, which=1, n_seq=n_seq, final=(l == DEPTH - 1))
        fin.append((gla_f, gdn_f, s5_f))
    return x.reshape(nseq, n_seq, d), fin


def kernel(x_prompt, x_sample, c, state_gla, state_gdn, state_s5_re, state_s5_im, c_ctx, w_ada, b_ada, norm_w, ffn_w_gate, ffn_w_up, ffn_w_down, w_in, gla_gk_up, gla_gk_bias, gla_norm_w, gdn_conv_w, gdn_a_log, gdn_dt_bias, gdn_norm_w, s5_lam_re, s5_lam_im, s5_log_step, s5_b_re, s5_b_im, s5_c_re, s5_c_im, s5_d, s5_w_glu, s5_b_glu, w_out, final_norm_w):
    p = dict(norm_w=norm_w, ffn_w_gate=ffn_w_gate, ffn_w_up=ffn_w_up, ffn_w_down=ffn_w_down,
             gla_gk_up=gla_gk_up, gla_gk_bias=gla_gk_bias, gla_norm_w=gla_norm_w, gdn_conv_w=gdn_conv_w,
             gdn_a_log=gdn_a_log, gdn_dt_bias=gdn_dt_bias, gdn_norm_w=gdn_norm_w, s5_c_re=s5_c_re,
             s5_c_im=s5_c_im, s5_d=s5_d, s5_w_glu=s5_w_glu, s5_b_glu=s5_b_glu, w_out=w_out,
             final_norm_w=final_norm_w)
    b_ctx = x_prompt.shape[0]
    b_dec = x_sample.shape[0]
    cond8 = jnp.zeros((8, D_MODEL), F32).at[0].set(c_ctx).at[1:1 + b_dec].set(c)
    mods = _ada(cond8, w_ada, b_ada)
    mods_ctx = mods[:, 0:1].reshape(DEPTH, 1, N_MOD, D_MODEL)
    mods_dec = mods[:, 1:1 + b_dec].reshape(DEPTH, b_dec, N_MOD, D_MODEL)
    s5c = _s5_assemble(*_s5_consts(s5_lam_re, s5_lam_im, s5_log_step, s5_b_re, s5_b_im, s5_c_re, s5_c_im))
    w_in_p = _permute_w_in(w_in)

    y_prompt, fin = _trunk(x_prompt, mods_ctx, 1, None, p, s5c, w_in_p)
    new_gla = jnp.stack([_gla_state_out(f[0]) for f in fin], axis=1)
    new_gdn = jnp.stack([f[1].transpose(1, 0, 2, 3, 4) for f in fin], axis=1)
    s5f = jnp.stack([f[2].reshape(2, 2, b_ctx, S5_GROUPS, S5_P).transpose(1, 2, 0, 3, 4) for f in fin], axis=2)
    new_re, new_im = s5f[0], s5f[1]

    rows = x_sample.shape[1] // 64
    y_sample, _ = _trunk(x_sample, mods_dec, rows, (state_gla, state_gdn, state_s5_re, state_s5_im),
                         p, s5c, w_in_p)
    return (y_prompt, y_sample, new_gla.astype(state_gla.dtype), new_gdn.astype(state_gdn.dtype),
            new_re.astype(state_s5_re.dtype), new_im.astype(state_s5_im.dtype))
```

```python
import functools
import math

import numpy as np
import jax
import jax.numpy as jnp
from jax import lax
from jax.experimental import pallas as pl
from jax.experimental.pallas import tpu as pltpu

F32 = jnp.float32
BF16 = jnp.bfloat16

D_MODEL = 1024
DEPTH = 2
FFN_DIM = 2816
N_MOD = 9
EPS = 1e-6
CHUNK = 64
GLA_HEADS, GLA_DK, GLA_DV, GLA_RANK = 4, 32, 64, 16
GLA_GATE_NORM = 16.0
GDN_HEADS, GDN_DK, GDN_DV, GDN_CONV = 4, 128, 128, 5
S5_GROUPS, S5_CH, S5_P = 16, 16, 64
GLA_W = GLA_HEADS * GLA_DV
GLA_KW = GLA_HEADS * GLA_DK
GDN_W = GDN_HEADS * GDN_DV
S5_W = S5_GROUPS * S5_CH
S5_N = S5_GROUPS * S5_P
S5_L = 16
S5_PAIRS = S5_GROUPS // 2
MIX_W = GLA_W + GDN_W + S5_W

_IN_SIZES = (GLA_KW, GLA_KW, GLA_W, GLA_RANK, GLA_RANK, GLA_W,
             GDN_W, GDN_W, GDN_W, GDN_HEADS, GDN_HEADS, GDN_HEADS, GDN_HEADS, GDN_W, S5_W)
_IN_OFF = tuple(int(v) for v in np.cumsum((0,) + _IN_SIZES))
(_GQ, _GK, _GV, _GLRF, _GLRB, _GOG, _DQ, _DK, _DV, _DAF, _DAB, _DBF, _DBB, _DOG, _SU) = range(15)

P_DQKV = 0
P_SMALL = 1536
P_GQK = 1792
P_GV = 2048
P_GOG = 2304
P_DOG = 2560
P_SU = 3072
PROJ_W = 3328
PROJ_TN = PROJ_W // 2
SM_A = GLA_RANK
SM_B = GLA_RANK + GDN_HEADS
SUB = 256
VMEM_LIMIT = 56 * 1024 * 1024


def _cparams(sem):
    return pltpu.CompilerParams(dimension_semantics=sem, vmem_limit_bytes=VMEM_LIMIT)


def _bdot(a, b):
    return jnp.dot(a.astype(BF16), b.astype(BF16), preferred_element_type=F32)


def _bdot_nt(a, b):
    return lax.dot_general(a.astype(BF16), b.astype(BF16), (((1,), (1,)), ((), ())),
                           preferred_element_type=F32)


def _split3(x):
    hi = x.astype(BF16)
    r1 = x - hi.astype(F32)
    mid = r1.astype(BF16)
    lo = (r1 - mid.astype(F32)).astype(BF16)
    return hi, mid, lo


def _sel_dot(m01, x):
    hi, mid, lo = _split3(x)
    return (jnp.dot(m01, hi, preferred_element_type=F32) + jnp.dot(m01, mid, preferred_element_type=F32)
            + jnp.dot(m01, lo, preferred_element_type=F32))


def _silu(x):
    return x * jax.nn.sigmoid(x)


def _softplus(x):
    return jnp.maximum(x, 0.0) + jnp.log1p(jnp.exp(-jnp.abs(x)))


def _log_sigmoid(x):
    return -_softplus(-x)


def _norm_mod(x, nw, scale, shift):
    y = x * lax.rsqrt(jnp.mean(x * x, axis=-1, keepdims=True) + EPS)
    return y * nw * (1.0 + scale) + shift


def _pick_tile(n, cap):
    t = cap
    while n % t:
        t //= 2
    return t


def _ada_kernel(c_ref, w_ref, b_ref, o_ref):
    o_ref[...] = _bdot(_silu(c_ref[...]), w_ref[...]) + b_ref[...]


def _ada(cond8, w_ada, b_ada):
    depth, d, nout = w_ada.shape
    tn = 1024
    return pl.pallas_call(
        _ada_kernel,
        grid=(depth, nout // tn),
        in_specs=[pl.BlockSpec((8, d), lambda l, j: (0, 0)),
                  pl.BlockSpec((None, d, tn), lambda l, j: (l, 0, j)),
                  pl.BlockSpec((None, 1, tn), lambda l, j: (l, 0, j))],
        out_specs=pl.BlockSpec((None, 8, tn), lambda l, j: (l, 0, j)),
        out_shape=jax.ShapeDtypeStruct((depth, 8, nout), F32),
        compiler_params=_cparams(("parallel", "parallel")),
        name="ada",
    )(cond8, w_ada, b_ada.reshape(depth, 1, nout))


def _ffn_kernel(x_ref, mod_ref, nw_ref, wg_ref, wu_ref, wd_ref, fw_ref, o_ref, h_ref, acc_ref,
                *, mod_base, final):
    j = pl.program_id(1)

    @pl.when(j == 0)
    def _():
        m = mod_ref[...]
        h = _norm_mod(x_ref[...], nw_ref[...], m[mod_base + 1:mod_base + 2], m[mod_base:mod_base + 1])
        h_ref[...] = h.astype(BF16)
        acc_ref[...] = jnp.zeros_like(acc_ref)

    h = h_ref[...]
    g = jnp.dot(h, wg_ref[...].astype(BF16), preferred_element_type=F32)
    u = jnp.dot(h, wu_ref[...].astype(BF16), preferred_element_type=F32)
    acc_ref[...] += _bdot(_silu(g) * u, wd_ref[...])

    @pl.when(j == pl.num_programs(1) - 1)
    def _():
        m = mod_ref[...]
        y = x_ref[...] + 0.5 * m[mod_base + 2:mod_base + 3] * acc_ref[...]
        if final:
            y = y * lax.rsqrt(jnp.mean(y * y, axis=-1, keepdims=True) + EPS) * fw_ref[...]
        o_ref[...] = y


def _mod_index(nb, tm, n_seq):
    if nb == 1:
        return lambda i: 0
    return lambda i: (i * tm) // n_seq


def _ffn(x, mods, norm_w, w_gate, w_up, w_down, final_w, *, layer, which, n_seq, final):
    t, d = x.shape
    f = w_gate.shape[-1]
    nb = mods.shape[0]
    tm = _pick_tile(n_seq if nb > 1 else t, 1024)
    tf = 256
    midx = _mod_index(nb, tm, n_seq)
    mod_base = 0 if which == 0 else 6
    return pl.pallas_call(
        functools.partial(_ffn_kernel, mod_base=mod_base, final=final),
        grid=(t // tm, f // tf),
        in_specs=[pl.BlockSpec((tm, d), lambda i, j: (i, 0)),
                  pl.BlockSpec((None, N_MOD, d), lambda i, j: (midx(i), 0, 0)),
                  pl.BlockSpec((None, None, 1, d), lambda i, j: (layer, 2 * which, 0, 0)),
                  pl.BlockSpec((None, None, d, tf), lambda i, j: (layer, which, 0, j)),
                  pl.BlockSpec((None, None, d, tf), lambda i, j: (layer, which, 0, j)),
                  pl.BlockSpec((None, None, tf, d), lambda i, j: (layer, which, j, 0)),
                  pl.BlockSpec((1, d), lambda i, j: (0, 0))],
        out_specs=pl.BlockSpec((tm, d), lambda i, j: (i, 0)),
        out_shape=jax.ShapeDtypeStruct((t, d), F32),
        scratch_shapes=[pltpu.VMEM((tm, d), BF16), pltpu.VMEM((tm, d), F32)],
        compiler_params=_cparams(("parallel", "arbitrary")),
        name="ffn",
    )(x, mods, norm_w.reshape(DEPTH, 3, 1, d), w_gate, w_up, w_down, final_w.reshape(1, d))


def _proj_kernel(x_ref, mod_ref, nw_ref, w_ref, cw_ref, o_ref, h_ref, wb_ref, *, rowlen):
    j = pl.program_id(0)
    i = pl.program_id(1)

    @pl.when(i == 0)
    def _():
        wb_ref[...] = w_ref[...].astype(BF16)

    @pl.when(j == 0)
    def _():
        m = mod_ref[...]
        h_ref[i] = _norm_mod(x_ref[...], nw_ref[...], m[4:5], m[3:4]).astype(BF16)
        o_ref[...] = jnp.dot(h_ref[i], wb_ref[...], preferred_element_type=F32)
        tm = o_ref.shape[0]
        pos = lax.broadcasted_iota(jnp.int32, (tm, 1), 0) % rowlen
        valid = [jnp.logical_and(pos + (t - GDN_CONV // 2) >= 0, pos + (t - GDN_CONV // 2) < rowlen)
                 for t in range(GDN_CONV)]
        for g in range(3 * GDN_HEADS):
            cols = slice(P_DQKV + g * GDN_DK, P_DQKV + (g + 1) * GDN_DK)
            xg = o_ref[:, cols]
            acc = xg * cw_ref[GDN_CONV // 2:GDN_CONV // 2 + 1, cols]
            for t in range(GDN_CONV):
                s = t - GDN_CONV // 2
                if s != 0:
                    xs = pltpu.roll(xg, (-s) % tm, axis=0)
                    acc = acc + jnp.where(valid[t], xs, 0.0) * cw_ref[t:t + 1, cols]
            y = _silu(acc)
            if g < 2 * GDN_HEADS:
                y = y * lax.rsqrt(jnp.sum(y * y, axis=-1, keepdims=True) + EPS)
                if g < GDN_HEADS:
                    y = y * (GDN_DK ** -0.5)
            o_ref[:, cols] = y

    @pl.when(j == 1)
    def _():
        o_ref[...] = jnp.dot(h_ref[i], wb_ref[...], preferred_element_type=F32)


def _proj(x, mods, norm_w, w_in_p, conv_w, *, layer, n_seq, rowlen):
    t, d = x.shape
    nb = mods.shape[0]
    tm = _pick_tile(n_seq if nb > 1 else t, 512)
    midx = _mod_index(nb, tm, n_seq)
    return pl.pallas_call(
        functools.partial(_proj_kernel, rowlen=rowlen),
        grid=(PROJ_W // PROJ_TN, t // tm),
        in_specs=[pl.BlockSpec((tm, d), lambda j, i: (i * (1 - j), 0)),
                  pl.BlockSpec((None, N_MOD, d), lambda j, i: (midx(i), 0, 0)),
                  pl.BlockSpec((None, None, 1, d), lambda j, i: (layer, 1, 0, 0)),
                  pl.BlockSpec((None, d, PROJ_TN), lambda j, i: (layer, 0, j)),
                  pl.BlockSpec((None, GDN_CONV, 3 * GDN_W), lambda j, i: (layer, 0, 0))],
        out_specs=pl.BlockSpec((tm, PROJ_TN), lambda j, i: (i, j)),
        out_shape=jax.ShapeDtypeStruct((t, PROJ_W), F32),
        scratch_shapes=[pltpu.VMEM((t // tm, tm, d), BF16), pltpu.VMEM((d, PROJ_TN), BF16)],
        compiler_params=_cparams(("arbitrary", "arbitrary")),
        name="proj",
    )(x, mods, norm_w.reshape(DEPTH, 3, 1, d), w_in_p, conv_w)


def _permute_w_in(w_in):
    def seg(i):
        return w_in[:, :, _IN_OFF[i]:_IN_OFF[i + 1]]

    def pad(n):
        return jnp.zeros(w_in.shape[:2] + (n,), w_in.dtype)

    sm_pad = 128 - GLA_RANK - 2 * GDN_HEADS
    return jnp.concatenate(
        [seg(_DQ), seg(_DK), seg(_DV),
         seg(_GLRF), seg(_DAF), seg(_DBF), pad(sm_pad),
         seg(_GLRB), seg(_DAB), seg(_DBB), pad(sm_pad),
         seg(_GQ), seg(_GK), seg(_GV), seg(_GOG), seg(_DOG), seg(_SU)], axis=-1)


def _np_consts():
    t = np.arange(CHUNK)
    lower = (t[:, None] >= t[None, :])
    incl = np.stack([lower, lower.T])
    strict = np.stack([t[:, None] > t[None, :], t[:, None] < t[None, :]])
    eye4 = np.eye(4, dtype=bool)
    blk = np.kron(eye4, np.ones((CHUNK, CHUNK), bool))
    incl_bd = np.stack([np.kron(eye4, incl[d]) for d in range(2)])
    strict_bd = np.stack([np.kron(eye4, strict[d]) for d in range(2)])
    levels = []
    for d in range(2):
        per = []
        for lv in range(6):
            s = 1 << lv
            same = (t[:, None] // (2 * s)) == (t[None, :] // (2 * s))
            hi = (t % (2 * s)) >= s
            m = same & hi[:, None] & (~hi)[None, :]
            if d == 1:
                m = m.T
            per.append(np.kron(eye4, m))
        levels.append(np.stack(per))
    levels = np.stack(levels)
    assert (levels.sum(1) == strict_bd).all() and blk.any()
    gla_mask = np.stack([np.tile(incl[d], (1, 4)) for d in range(2)])
    hm_k = np.kron(eye4, np.ones((1, GLA_DK)))
    hm_v = np.kron(eye4, np.ones((1, GLA_DV)))
    st_mask = np.kron(eye4, np.ones((GLA_DV, GLA_DK)))
    ind64 = np.kron(eye4, np.full((GLA_DV, GLA_DV), 1.0 / GLA_DV))
    return dict(incl=incl, incl_bd=incl_bd, strict_bd=strict_bd, levels=levels, gla_mask=gla_mask,
                hm_k=hm_k, hm_v=hm_v, st_mask=st_mask, ind64=ind64, eye256=np.eye(256))


_C = _np_consts()


def _row_block_index(nblk):
    return lambda d, b, i: b * nblk + i + d * (nblk - 1 - 2 * i)


def _gla_kernel(qk_ref, v_ref, sm_ref, up_ref, bias_ref, tri_ref, mask_ref, hmk_ref, hmv_ref, stm_ref, *rest,
                nchunk, zero_init):
    if zero_init:
        o_ref, stf_ref, st_ref, qd_sc, m_sc, dec_sc = rest
    else:
        st0_ref, o_ref, stf_ref, st_ref, qd_sc, m_sc, dec_sc = rest
    d = pl.program_id(0)

    @pl.when(pl.program_id(2) == 0)
    def _():
        st_ref[...] = jnp.zeros_like(st_ref) if zero_init else st0_ref[...]

    tri = tri_ref[...]
    mask = mask_ref[...]
    hmk = hmk_ref[...]
    hmv = hmv_ref[...]
    stm = stm_ref[...]

    for c in range(nchunk):
        rows = pl.ds(c * CHUNK, CHUNK)
        qk = qk_ref[rows, :]
        q = qk[:, 0:GLA_KW] * (GLA_DK ** -0.5)
        k = qk[:, GLA_KW:2 * GLA_KW]
        v = v_ref[rows, :]
        x = _bdot(sm_ref[rows, :], up_ref[...]) + bias_ref[...]
        lg = _log_sigmoid(x) * (1.0 / GLA_GATE_NORM)
        b = _sel_dot(tri, lg)
        btot = jnp.sum(lg, axis=0, keepdims=True)
        qd = (q * jnp.exp(b)).astype(BF16)
        ki = k * jnp.exp(-b)
        kd = k * jnp.exp(btot - b)
        kexp = jnp.concatenate([ki * hmk[h:h + 1] for h in range(GLA_HEADS)], axis=0)
        att = _bdot_nt(qd, kexp) * mask
        vexp = jnp.concatenate([v * hmv[h:h + 1] for h in range(GLA_HEADS)], axis=0)
        o_ref[rows, :] = _bdot(att, vexp)
        qd_sc[c] = qd
        m_sc[c] = _bdot(v.T, kd) * stm
        dec_sc[c] = jnp.broadcast_to(jnp.exp(btot), (8, GLA_KW))

    st = st_ref[...]
    for ci in range(nchunk):
        c = ci + d * (nchunk - 1 - 2 * ci)
        rows = pl.ds(pl.multiple_of(c * CHUNK, CHUNK), CHUNK)
        o_ref[rows, :] += lax.dot_general(qd_sc[c], st.astype(BF16), (((1,), (1,)), ((), ())),
                                          preferred_element_type=F32)
        st = st * dec_sc[c][0:1] + m_sc[c]
    st_ref[...] = st
    stf_ref[...] = st


def _gla(proj, up_p, bias_p, st0, *, nseq, n_seq):
    t = proj.shape[0]
    nblk = n_seq // SUB
    nck = SUB // CHUNK
    rbi = _row_block_index(nblk)
    c = _C
    consts = [jnp.asarray(c["incl"], BF16), jnp.asarray(c["gla_mask"], F32), jnp.asarray(c["hm_k"], F32),
              jnp.asarray(c["hm_v"], F32), jnp.asarray(c["st_mask"], F32)]
    st_spec = pl.BlockSpec((None, None, GLA_W, GLA_KW), lambda d, b, i: (d, b, 0, 0))
    zero_init = st0 is None
    return pl.pallas_call(
        functools.partial(_gla_kernel, nchunk=nck, zero_init=zero_init),
        grid=(2, nseq, nblk),
        in_specs=[pl.BlockSpec((SUB, 2 * GLA_KW), lambda d, b, i: (rbi(d, b, i), P_GQK // (2 * GLA_KW))),
                  pl.BlockSpec((SUB, GLA_W), lambda d, b, i: (rbi(d, b, i), P_GV // GLA_W)),
                  pl.BlockSpec((SUB, 128), lambda d, b, i: (rbi(d, b, i), P_SMALL // 128 + d)),
                  pl.BlockSpec((None, 128, GLA_KW), lambda d, b, i: (d, 0, 0)),
                  pl.BlockSpec((None, 1, GLA_KW), lambda d, b, i: (d, 0, 0)),
                  pl.BlockSpec((None, CHUNK, CHUNK), lambda d, b, i: (d, 0, 0)),
                  pl.BlockSpec((None, CHUNK, 256), lambda d, b, i: (d, 0, 0)),
                  pl.BlockSpec((GLA_HEADS, GLA_KW), lambda d, b, i: (0, 0)),
                  pl.BlockSpec((GLA_HEADS, GLA_W), lambda d, b, i: (0, 0)),
                  pl.BlockSpec((GLA_W, GLA_KW), lambda d, b, i: (0, 0))] + ([] if zero_init else [st_spec]),
        out_specs=[pl.BlockSpec((None, SUB, GLA_W), lambda d, b, i: (d, rbi(d, b, i), 0)), st_spec],
        out_shape=[jax.ShapeDtypeStruct((2, t, GLA_W), F32),
                   jax.ShapeDtypeStruct((2, nseq, GLA_W, GLA_KW), F32)],
        scratch_shapes=[pltpu.VMEM((GLA_W, GLA_KW), F32), pltpu.VMEM((nck, CHUNK, GLA_KW), BF16),
                        pltpu.VMEM((nck, GLA_W, GLA_KW), F32), pltpu.VMEM((nck, 8, GLA_KW), F32)],
        compiler_params=_cparams(("arbitrary", "arbitrary", "arbitrary")),
        name="gla",
    )(proj, proj, proj, up_p, bias_p, *consts, *([] if zero_init else [st0]))


def _gdn_kernel(qkv_ref, sm_ref, arow_ref, dtb_ref, tri_ref, incl_ref, strict_ref, lvl_ref, eye_ref, *rest,
                nchunk, zero_init):
    if zero_init:
        o_ref, sf_ref, s_ref, la_ref, be_ref, w_sc, u0_sc, qkd_sc, qe_sc, kdt_sc, egl_sc = rest
    else:
        s0_ref, o_ref, sf_ref, s_ref, la_ref, be_ref, w_sc, u0_sc, qkd_sc, qe_sc, kdt_sc, egl_sc = rest
    d = pl.program_id(0)

    @pl.when(pl.program_id(2) == 0)
    def _():
        s_ref[...] = jnp.zeros_like(s_ref) if zero_init else s0_ref[...]

    sm = sm_ref[...]
    la_ref[...] = -jnp.exp(arow_ref[...]) * _softplus(sm + dtb_ref[...])
    be_ref[...] = jax.nn.sigmoid(sm)

    tri = tri_ref[...]
    incl = incl_ref[...]
    strict = strict_ref[...]
    eye = eye_ref[...]

    def stack(ref, r0, lo, width):
        return jnp.concatenate([ref[pl.ds(r0, CHUNK), lo + h * width:lo + (h + 1) * width]
                                for h in range(GDN_HEADS)], axis=0)

    def colstack(a, lo):
        return jnp.concatenate([a[:, lo + h:lo + h + 1] for h in range(GDN_HEADS)], axis=0)

    lvl = [lvl_ref[lv] for lv in range(6)]

    loc = []
    for c in range(nchunk):
        r0 = c * CHUNK
        qs = stack(qkv_ref, r0, 0, GDN_DK)
        ks = stack(qkv_ref, r0, GDN_W, GDN_DK)
        vs = stack(qkv_ref, r0, 2 * GDN_W, GDN_DV)
        la = la_ref[pl.ds(r0, CHUNK), :]
        g = _sel_dot(tri, la)
        gtot = jnp.sum(la, axis=0, keepdims=True)
        gcol = colstack(g, SM_A)
        kfcol = colstack(jnp.exp(gtot - g), SM_A)
        egcol = colstack(jnp.exp(g), SM_A)
        bcol = colstack(be_ref[pl.ds(r0, CHUNK), :], SM_B)
        grow = jnp.sum(eye * gcol, axis=0, keepdims=True)
        dec = incl * jnp.exp(jnp.minimum(gcol - grow, 0.0))
        ksb = ks.astype(BF16)
        lmat = strict * dec * _bdot_nt(ksb, ksb) * bcol
        rhs = jnp.concatenate([ks * (bcol * egcol), vs * bcol], axis=1).astype(BF16)
        qkd_sc[c] = (_bdot_nt(qs, ksb) * dec).astype(BF16)
        qe_sc[c] = (qs * egcol).astype(BF16)
        kd = ks * kfcol
        for h in range(GDN_HEADS):
            kdt_sc[c * GDN_HEADS + h] = kd[h * CHUNK:(h + 1) * CHUNK].T.astype(BF16)
        egl_sc[c] = jnp.broadcast_to(jnp.exp(gtot), (8, 128))
        loc.append((lmat.astype(BF16), rhs))
    tinv = [eye - (lm * lvl[0]).astype(F32) for lm, _ in loc]
    for lv in range(1, 6):
        tb = [t_.astype(BF16) for t_ in tinv]
        cs = [jnp.dot(lm * lvl[lv], t_, preferred_element_type=F32) for (lm, _), t_ in zip(loc, tb)]
        tinv = [t32 - jnp.dot(t_, c_.astype(BF16), preferred_element_type=F32)
                for t32, t_, c_ in zip(tinv, tb, cs)]
    for c in range(nchunk):
        sol = jnp.dot(tinv[c].astype(BF16), loc[c][1], preferred_element_type=F32)
        w_sc[c] = sol[:, :GDN_DK].astype(BF16)
        u0_sc[c] = sol[:, GDN_DK:]

    s_cur = [s_ref[h] for h in range(GDN_HEADS)]
    for ci in range(nchunk):
        c = ci + d * (nchunk - 1 - 2 * ci)
        r0 = pl.multiple_of(c * CHUNK, CHUNK)
        sb = [s_.astype(BF16) for s_ in s_cur]
        us = []
        for h in range(GDN_HEADS):
            rows = pl.ds(h * CHUNK, CHUNK)
            us.append(u0_sc[c, rows, :] - jnp.dot(w_sc[c, rows, :], sb[h], preferred_element_type=F32))
        ub = [u_.astype(BF16) for u_ in us]
        o2 = jnp.dot(qkd_sc[c], jnp.concatenate(ub, axis=0), preferred_element_type=F32)
        egl = egl_sc[c]
        for h in range(GDN_HEADS):
            rows = pl.ds(h * CHUNK, CHUNK)
            o_ref[pl.ds(r0, CHUNK), h * GDN_DV:(h + 1) * GDN_DV] = (
                jnp.dot(qe_sc[c, rows, :], sb[h], preferred_element_type=F32) + o2[h * CHUNK:(h + 1) * CHUNK])
            s_cur[h] = (egl[0:1, SM_A + h:SM_A + h + 1] * s_cur[h]
                        + jnp.dot(kdt_sc[c * GDN_HEADS + h], ub[h], preferred_element_type=F32))
    for h in range(GDN_HEADS):
        s_ref[h] = s_cur[h]
    sf_ref[...] = s_ref[...]


def _gdn(proj, arow, dtb, s0, *, nseq, n_seq):
    t = proj.shape[0]
    nblk = n_seq // SUB
    rbi = _row_block_index(nblk)
    c = _C
    consts = [jnp.asarray(c["incl"], BF16), jnp.asarray(c["incl_bd"], F32), jnp.asarray(c["strict_bd"], F32),
              jnp.asarray(c["levels"], BF16), jnp.asarray(c["eye256"], F32)]
    hs = (GDN_HEADS, GDN_DK, GDN_DV)
    nck = SUB // CHUNK
    zero_init = s0 is None
    s_spec = pl.BlockSpec((None, None) + hs, lambda d, b, i: (d, b, 0, 0, 0))
    return pl.pallas_call(
        functools.partial(_gdn_kernel, nchunk=nck, zero_init=zero_init),
        grid=(2, nseq, nblk),
        in_specs=[pl.BlockSpec((SUB, 3 * GDN_W), lambda d, b, i: (rbi(d, b, i), P_DQKV // (3 * GDN_W))),
                  pl.BlockSpec((SUB, 128), lambda d, b, i: (rbi(d, b, i), P_SMALL // 128 + d)),
                  pl.BlockSpec((None, 1, 128), lambda d, b, i: (d, 0, 0)),
                  pl.BlockSpec((None, 1, 128), lambda d, b, i: (d, 0, 0)),
                  pl.BlockSpec((None, CHUNK, CHUNK), lambda d, b, i: (d, 0, 0)),
                  pl.BlockSpec((None, 256, 256), lambda d, b, i: (d, 0, 0)),
                  pl.BlockSpec((None, 256, 256), lambda d, b, i: (d, 0, 0)),
                  pl.BlockSpec((None, 6, 256, 256), lambda d, b, i: (d, 0, 0, 0)),
                  pl.BlockSpec((256, 256), lambda d, b, i: (0, 0))] + ([] if zero_init else [s_spec]),
        out_specs=[pl.BlockSpec((None, SUB, GDN_W), lambda d, b, i: (d, rbi(d, b, i), 0)), s_spec],
        out_shape=[jax.ShapeDtypeStruct((2, t, GDN_W), F32),
                   jax.ShapeDtypeStruct((2, nseq) + hs, F32)],
        scratch_shapes=[pltpu.VMEM(hs, F32),
                        pltpu.VMEM((SUB, 128), F32), pltpu.VMEM((SUB, 128), F32),
                        pltpu.VMEM((nck, 256, GDN_DK), BF16), pltpu.VMEM((nck, 256, GDN_DV), F32),
                        pltpu.VMEM((nck, 256, 256), BF16), pltpu.VMEM((nck, 256, GDN_DK), BF16),
                        pltpu.VMEM((nck * GDN_HEADS, GDN_DK, CHUNK), BF16), pltpu.VMEM((nck, 8, 128), F32)],
        compiler_params=_cparams(("arbitrary", "arbitrary", "arbitrary")),
        name="gdn",
    )(proj, proj, arow, dtb, *consts, *([] if zero_init else [s0]))


def _cmul(ar, ai, br, bi):
    return ar * br - ai * bi, ar * bi + ai * br


def _split2(x):
    hi = x.astype(BF16)
    return hi, (x - hi.astype(F32)).astype(BF16)


def _s5_consts_kernel(lr_ref, li_ref, st_ref, br_ref, bi_ref, cr_ref, ci_ref, gm_ref, bm_ref,
                      laml_ref, wexp_ref, vexp_ref, tt_ref, w_ref, v_ref, a_ref, q_ref):
    d = pl.program_id(0) % 2
    lc = S5_L * S5_CH

    def trows(s):
        return pl.ds(pl.multiple_of((s + d * (S5_L - 1 - 2 * s)) * S5_CH, S5_CH), S5_CH)

    lr = lr_ref[...]
    li = li_ref[...]
    step = jnp.exp(st_ref[...])
    mag = jnp.exp(lr * step)
    lbr = mag * jnp.cos(li * step)
    lbi = mag * jnp.sin(li * step)
    nr = lbr - 1.0
    den = lr * lr + li * li
    fr = (nr * lr + lbi * li) / den
    fi = (lbi * lr - nr * li) / den
    bbr, bbi = _cmul(fr, fi, br_ref[...], bi_ref[...])
    cr = cr_ref[...]
    ci = ci_ref[...]
    pr = jnp.ones_like(lr)
    pi = jnp.zeros_like(lr)
    for s in range(S5_L):
        ar, ai = _cmul(cr, ci, pr, pi)
        a_ref[0, trows(s), :] = ar
        a_ref[1, trows(s), :] = ai
        wr, wi = _cmul(pr, pi, bbr, bbi)
        w_ref[0, trows(S5_L - 1 - s), :] = wr
        w_ref[1, trows(S5_L - 1 - s), :] = wi
        pr, pi = _cmul(pr, pi, lbr, lbi)
        vr, vi = _cmul(cr, ci, pr, pi)
        v_ref[0, trows(s), :] = vr
        v_ref[1, trows(s), :] = -vi
    laml_ref[0:1, :] = pr
    laml_ref[1:2, :] = pi

    lane = lax.broadcasted_iota(jnp.int32, (1, 128), 1)
    for gp in range(S5_PAIRS):
        lanes = slice(gp * 128, (gp + 1) * 128)
        for r in range(2):
            for src, dst in ((w_ref, wexp_ref), (v_ref, vexp_ref)):
                slab = src[r, :, lanes]
                dst[r, gp] = jnp.concatenate([jnp.where(lane < S5_P, slab, 0.0),
                                              jnp.where(lane >= S5_P, slab, 0.0)], axis=0).astype(BF16)

    gm = gm_ref[...]
    kall = None
    for r, b, sign in ((0, bbr, 1.0), (1, bbi, -1.0)):
        ah, al = _split2(a_ref[r])
        bh, bl = _split2(jnp.concatenate([b] * S5_GROUPS, axis=0) * gm)
        nt = lambda x, y: lax.dot_general(x, y, (((1,), (1,)), ((), ())), preferred_element_type=F32)
        term = (nt(ah, bh) + nt(ah, bl) + nt(al, bh)) * sign
        kall = term if kall is None else kall + term

    zeros = jnp.zeros((lc, S5_W), F32)
    for delta in range(S5_GROUPS):
        q_ref[delta, 0:lc, :] = zeros
        q_ref[delta, lc:2 * lc, :] = kall if delta == 0 else pltpu.roll(kall, delta * S5_CH, axis=1)
        q_ref[delta, 2 * lc:3 * lc, :] = zeros
    for g in range(S5_GROUPS):
        acc = zeros
        for s in range(S5_L):
            start = pl.multiple_of(lc - S5_CH * s + d * (S5_CH * (S5_L - 1)), S5_CH)
            blk = q_ref[(s - g) % S5_GROUPS, pl.ds(start, lc), :]
            acc = jnp.where(bm_ref[s:s + 1, :] > 0.5, blk, acc)
        tt_ref[g] = acc.astype(BF16)


def _s5_consts(lam_re, lam_im, log_step, b_re, b_im, c_re, c_im):
    n = DEPTH * 2
    lc = S5_L * S5_CH
    lr = lam_re.reshape(n, 1, S5_N)
    li = lam_im.reshape(n, 1, S5_N)
    st = jnp.repeat(log_step.reshape(n, S5_GROUPS), S5_P, axis=-1).reshape(n, 1, S5_N)
    br = b_re.reshape(n, S5_N, S5_CH).transpose(0, 2, 1)
    bi = b_im.reshape(n, S5_N, S5_CH).transpose(0, 2, 1)
    cr = c_re.reshape(n, S5_GROUPS, S5_CH, S5_P).transpose(0, 2, 1, 3).reshape(n, S5_CH, S5_N)
    ci = c_im.reshape(n, S5_GROUPS, S5_CH, S5_P).transpose(0, 2, 1, 3).reshape(n, S5_CH, S5_N)
    gm = jnp.asarray(np.kron(np.eye(S5_GROUPS), np.ones((S5_CH, S5_P))), F32)
    row = pl.BlockSpec((None, 1, S5_N), lambda i: (i, 0, 0))
    mat = pl.BlockSpec((None, S5_CH, S5_N), lambda i: (i, 0, 0))
    exp = pl.BlockSpec((None, 2, S5_PAIRS, 2 * lc, 128), lambda i: (i, 0, 0, 0, 0))
    laml, wexp, vexp, tt = pl.pallas_call(
        _s5_consts_kernel,
        grid=(n,),
        in_specs=[row, row, row, mat, mat, mat, mat, pl.BlockSpec((S5_W, S5_N), lambda i: (0, 0)),
                  pl.BlockSpec((S5_GROUPS, S5_W), lambda i: (0, 0))],
        out_specs=[pl.BlockSpec((None, 2, S5_N), lambda i: (i, 0, 0)), exp, exp,
                   pl.BlockSpec((None, S5_GROUPS, lc, lc), lambda i: (i, 0, 0, 0))],
        out_shape=[jax.ShapeDtypeStruct((n, 2, S5_N), F32),
                   jax.ShapeDtypeStruct((n, 2, S5_PAIRS, 2 * lc, 128), BF16),
                   jax.ShapeDtypeStruct((n, 2, S5_PAIRS, 2 * lc, 128), BF16),
                   jax.ShapeDtypeStruct((n, S5_GROUPS, lc, lc), BF16)],
        scratch_shapes=[pltpu.VMEM((2, lc, S5_N), F32), pltpu.VMEM((2, lc, S5_N), F32),
                        pltpu.VMEM((2, lc, S5_N), F32), pltpu.VMEM((S5_GROUPS, 3 * lc, S5_W), F32)],
        compiler_params=_cparams(("arbitrary",)),
        name="s5_consts",
    )(lr, li, st, br, bi, cr, ci, gm, _lane_block_masks())
    return laml.reshape(n, 2, S5_PAIRS, 128), wexp, vexp, tt


def _lane_block_masks():
    return jnp.asarray(np.kron(np.eye(S5_GROUPS), np.ones((1, S5_CH))), F32)


def _nt(a, b):
    return lax.dot_general(a, b, (((1,), (1,)), ((), ())), preferred_element_type=F32)


def _lane_block_transpose(arrs):
    n = len(arrs)
    width = arrs[0].shape[1]
    blk = lax.broadcasted_iota(jnp.int32, (1, width), 1) // (width // n)
    arrs = list(arrs)
    h = n // 2
    while h >= 1:
        hi = (blk & h) != 0
        for a in range(n):
            if a & h:
                continue
            lo_arr, hi_arr = arrs[a], arrs[a + h]
            arrs[a] = jnp.where(hi, pltpu.roll(hi_arr, h * (width // n), axis=1), lo_arr)
            arrs[a + h] = jnp.where(hi, hi_arr, pltpu.roll(lo_arr, width - h * (width // n), axis=1))
        h //= 2
    return arrs


def _s5_kernel(u0_ref, u1_ref, tt_ref, wexp_ref, vexp_ref, laml_ref, *rest, nseq, nblk, zero_init):
    if zero_init:
        y_ref, hf_ref, ug_sc, yg_sc, hin_re, hin_im, hp_re, hp_im = rest
    else:
        h0_ref, y_ref, hf_ref, ug_sc, yg_sc, hin_re, hin_im, hp_re, hp_im = rest
    d = pl.program_id(0)
    r = nblk * nseq
    lc = S5_L * S5_CH

    @pl.when(d == 0)
    def _():
        xs = []
        for s in range(S5_L):
            rows = pl.ds(s, r, stride=S5_L)
            xs.append(jnp.concatenate([u0_ref[rows, :], u1_ref[rows, :]], axis=1))
        for g, ug in enumerate(_lane_block_transpose(xs)):
            ug_sc[g] = ug.astype(BF16)

    for gp in range(S5_PAIRS):
        up = jnp.concatenate([ug_sc[2 * gp], ug_sc[2 * gp + 1]], axis=1)
        rows = pl.ds(gp, r, stride=S5_PAIRS)
        hin_re[rows, :] = jnp.dot(up, wexp_ref[0, gp], preferred_element_type=F32)
        hin_im[rows, :] = jnp.dot(up, wexp_ref[1, gp], preferred_element_type=F32)

    lr = laml_ref[0]
    li = laml_ref[1]

    def step(kk, carry):
        k = kk + d * (nblk - 1 - 2 * kk)
        new = []
        for b in range(nseq):
            rows = pl.ds(pl.multiple_of((b * nblk + k) * S5_PAIRS, S5_PAIRS), S5_PAIRS)
            hr, hi = carry[2 * b], carry[2 * b + 1]
            hp_re[rows, :] = hr
            hp_im[rows, :] = hi
            new += [lr * hr - li * hi + hin_re[rows, :], lr * hi + li * hr + hin_im[rows, :]]
        return tuple(new)

    init = []
    for b in range(nseq):
        rows = slice(b * S5_PAIRS, (b + 1) * S5_PAIRS)
        init += ([jnp.zeros((S5_PAIRS, 128), F32)] * 2 if zero_init else [h0_ref[0, rows, :], h0_ref[1, rows, :]])
    fin = lax.fori_loop(0, nblk, step, tuple(init))
    for b in range(nseq):
        rows = slice(b * S5_PAIRS, (b + 1) * S5_PAIRS)
        hf_ref[0, rows, :] = fin[2 * b]
        hf_ref[1, rows, :] = fin[2 * b + 1]

    for gp in range(S5_PAIRS):
        rows = pl.ds(gp, r, stride=S5_PAIRS)
        ys = (_nt(hp_re[rows, :].astype(BF16), vexp_ref[0, gp])
              + _nt(hp_im[rows, :].astype(BF16), vexp_ref[1, gp]))
        for j in range(2):
            g = 2 * gp + j
            yg = _nt(ug_sc[g], tt_ref[g]) + ys[:, j * lc:(j + 1) * lc]

            @pl.when(d == 0)
            def _():
                yg_sc[g] = yg

            @pl.when(d == 1)
            def _():
                yg_sc[g] += yg

    @pl.when(d == 1)
    def _():
        for t, yt in enumerate(_lane_block_transpose([yg_sc[g] for g in range(S5_GROUPS)])):
            rows = pl.ds(t, r, stride=S5_L)
            y_ref[0, rows, :] = yt[:, 0:128]
            y_ref[1, rows, :] = yt[:, 128:2 * 128]


def _s5(proj, consts, h0, *, layer, nseq, n_seq):
    laml, wexp, vexp, tt = consts
    t = proj.shape[0]
    nblk = n_seq // S5_L
    r = nblk * nseq
    lc = S5_L * S5_CH
    bs = nseq * S5_PAIRS
    zero_init = h0 is None
    h_spec = pl.BlockSpec((None, 2, bs, 128), lambda d: (d, 0, 0, 0))
    return pl.pallas_call(
        functools.partial(_s5_kernel, nseq=nseq, nblk=nblk, zero_init=zero_init),
        grid=(2,),
        in_specs=[pl.BlockSpec((t, 128), lambda d: (0, P_SU // 128)),
                  pl.BlockSpec((t, 128), lambda d: (0, P_SU // 128 + 1)),
                  pl.BlockSpec((None, S5_GROUPS, lc, lc), lambda d: (2 * layer + d, 0, 0, 0)),
                  pl.BlockSpec((None, 2, S5_PAIRS, 2 * lc, 128), lambda d: (2 * layer + d, 0, 0, 0, 0)),
                  pl.BlockSpec((None, 2, S5_PAIRS, 2 * lc, 128), lambda d: (2 * layer + d, 0, 0, 0, 0)),
                  pl.BlockSpec((None, 2, S5_PAIRS, 128), lambda d: (2 * layer + d, 0, 0, 0))]
                 + ([] if zero_init else [h_spec]),
        out_specs=[pl.BlockSpec((2, t, 128), lambda d: (0, 0, 0)), h_spec],
        out_shape=[jax.ShapeDtypeStruct((2, t, 128), F32),
                   jax.ShapeDtypeStruct((2, 2, bs, 128), F32)],
        scratch_shapes=[pltpu.VMEM((S5_GROUPS, r, lc), BF16), pltpu.VMEM((S5_GROUPS, r, lc), F32)]
                       + [pltpu.VMEM((r * S5_PAIRS, 128), F32) for _ in range(4)],
        compiler_params=_cparams(("arbitrary",)),
        name="s5",
    )(proj, proj, tt, wexp, vexp, laml, *([] if zero_init else [h0]))


def _out_kernel(x_ref, mod_ref, ogf_ref, ogb_ref, odf_ref, odb_ref, ysf_ref, ysb_ref, gog_ref, su_ref, dog_ref,
                ind_ref, gw_ref, dw_ref, sd_ref, wglu_ref, bglu_ref, wout_ref, o_ref):
    og = ogf_ref[...] + ogb_ref[...]
    hi, mid, _ = _split3(og * og)
    ms = jnp.dot(hi, ind_ref[...], preferred_element_type=F32) + jnp.dot(mid, ind_ref[...], preferred_element_type=F32)
    og = og * lax.rsqrt(ms + EPS) * gw_ref[...] * _silu(gog_ref[...])
    od = odf_ref[...] + odb_ref[...]
    dog = dog_ref[...]
    parts = []
    for h in range(GDN_HEADS):
        z = od[:, h * GDN_DV:(h + 1) * GDN_DV]
        z = z * lax.rsqrt(jnp.mean(z * z, axis=-1, keepdims=True) + EPS) * dw_ref[...]
        parts.append(z * _silu(dog[:, h * GDN_DV:(h + 1) * GDN_DV]))
    y = su_ref[...] * sd_ref[...] + jnp.concatenate([ysf_ref[...], ysb_ref[...]], axis=1)
    g = 0.5 * y * (1.0 + jnp.tanh(math.sqrt(2.0 / math.pi) * (y + 0.044715 * (y * y * y))))
    os_ = g * jax.nn.sigmoid(_bdot(g, wglu_ref[...]) + bglu_ref[...])
    out = _bdot(og, wout_ref[0:GLA_W, :]) + _bdot(os_, wout_ref[GLA_W + GDN_W:MIX_W, :])
    for h in range(GDN_HEADS):
        lo = GLA_W + h * GDN_DV
        out = out + _bdot(parts[h], wout_ref[lo:lo + GDN_DV, :])
    o_ref[...] = x_ref[...] + mod_ref[5:6, :] * out


def _mix_out(x, mods, proj, og, od, ys, gla_nw, gdn_nw, s5_d, w_glu, b_glu, w_out, *, layer, n_seq):
    t, d = x.shape
    nb = mods.shape[0]
    tm = _pick_tile(n_seq if nb > 1 else t, 512)
    midx = _mod_index(nb, tm, n_seq)
    ys2 = ys

    def dirspec(w, dd):
        return pl.BlockSpec((None, tm, w), lambda i: (dd, i, 0))

    def vec(w):
        return pl.BlockSpec((None, 1, w), lambda i: (layer, 0, 0))

    return pl.pallas_call(
        _out_kernel,
        grid=(t // tm,),
        in_specs=[pl.BlockSpec((tm, d), lambda i: (i, 0)),
                  pl.BlockSpec((None, N_MOD, d), lambda i: (midx(i), 0, 0)),
                  dirspec(GLA_W, 0), dirspec(GLA_W, 1), dirspec(GDN_W, 0), dirspec(GDN_W, 1),
                  dirspec(128, 0), dirspec(128, 1),
                  pl.BlockSpec((tm, GLA_W), lambda i: (i, P_GOG // GLA_W)),
                  pl.BlockSpec((tm, S5_W), lambda i: (i, P_SU // S5_W)),
                  pl.BlockSpec((tm, GDN_W), lambda i: (i, P_DOG // GDN_W)),
                  pl.BlockSpec((GLA_W, GLA_W), lambda i: (0, 0)),
                  vec(GLA_W), vec(GDN_DV), vec(S5_W),
                  pl.BlockSpec((None, S5_W, S5_W), lambda i: (layer, 0, 0)),
                  vec(S5_W),
                  pl.BlockSpec((None, MIX_W, d), lambda i: (layer, 0, 0))],
        out_specs=pl.BlockSpec((tm, d), lambda i: (i, 0)),
        out_shape=jax.ShapeDtypeStruct((t, d), F32),
        compiler_params=_cparams(("parallel",)),
        name="mix_out",
    )(x, mods, og, og, od, od, ys2, ys2, proj, proj, proj,
      jnp.asarray(_C["ind64"], BF16),
      jnp.tile(gla_nw, (1, GLA_HEADS)).reshape(DEPTH, 1, GLA_W), gdn_nw.reshape(DEPTH, 1, GDN_DV),
      s5_d.reshape(DEPTH, 1, S5_W), w_glu, b_glu.reshape(DEPTH, 1, S5_W), w_out)


def _gla_params(gk_up, gk_bias, layer):
    up = jnp.zeros((2, 128, GLA_KW), F32).at[:, :GLA_RANK, :].set(gk_up[layer])
    return up, gk_bias[layer].reshape(2, 1, GLA_KW)


def _gdn_params(a_log, dt_bias, layer):
    arow = jnp.zeros((2, 1, 128), F32).at[:, 0, SM_A:SM_A + GDN_HEADS].set(a_log[layer])
    dtb = jnp.zeros((2, 1, 128), F32).at[:, 0, SM_A:SM_A + GDN_HEADS].set(dt_bias[layer])
    return arow, dtb


def _gla_state_in(s):
    eye = jnp.eye(GLA_HEADS, dtype=F32)
    st = jnp.einsum('bzhde,hg->zbhegd', s.astype(F32), eye)
    return st.reshape(2, s.shape[0], GLA_W, GLA_KW)


def _gla_state_out(st):
    b = st.shape[1]
    s = st.reshape(2, b, GLA_HEADS, GLA_DV, GLA_HEADS, GLA_DK)
    s = jnp.stack([s[:, :, h, :, h, :] for h in range(GLA_HEADS)], axis=2)
    return s.transpose(1, 0, 2, 4, 3)


def _trunk(x3, mods, rows, states, p, s5c, w_in_p):
    nseq, n_seq, d = x3.shape
    x = x3.reshape(nseq * n_seq, d)
    fin = []
    for l in range(DEPTH):
        m = mods[l]
        x = _ffn(x, m, p['norm_w'], p['ffn_w_gate'], p['ffn_w_up'], p['ffn_w_down'], p['final_norm_w'],
                 layer=l, which=0, n_seq=n_seq, final=False)
        proj = _proj(x, m, p['norm_w'], w_in_p, p['gdn_conv_w'], layer=l, n_seq=n_seq, rowlen=n_seq // rows)
        if states is None:
            st0 = s0 = h0 = None
        else:
            st_gla, st_gdn, st_re, st_im = states
            st0 = _gla_state_in(st_gla[:, l])
            s0 = st_gdn[:, l].transpose(1, 0, 2, 3, 4)
            h0 = jnp.stack([st_re[:, l].reshape(nseq, 2, S5_N), st_im[:, l].reshape(nseq, 2, S5_N)], axis=0)
            h0 = h0.transpose(2, 0, 1, 3).reshape(2, 2, nseq * S5_PAIRS, 128)
        up_p, bias_p = _gla_params(p['gla_gk_up'], p['gla_gk_bias'], l)
        og, gla_f = _gla(proj, up_p, bias_p, st0, nseq=nseq, n_seq=n_seq)
        arow, dtb = _gdn_params(p['gdn_a_log'], p['gdn_dt_bias'], l)
        od, gdn_f = _gdn(proj, arow, dtb, s0, nseq=nseq, n_seq=n_seq)
        ys, s5_f = _s5(proj, s5c, h0, layer=l, nseq=nseq, n_seq=n_seq)
        x = _mix_out(x, m, proj, og, od, ys, p['gla_norm_w'], p['gdn_norm_w'], p['s5_d'], p['s5_w_glu'],
                     p['s5_b_glu'], p['w_out'], layer=l, n_seq=n_seq)
        x = _ffn(x, m, p['norm_w'], p['ffn_w_gate'], p['ffn_w_up'], p['ffn_w_down'], p['final_norm_w'],
                 layer=l, which=1, n_seq=n_seq, final=(l == DEPTH - 1))
        fin.append((gla_f, gdn_f, s5_f))
    return x.reshape(nseq, n_seq, d), fin


def kernel(x_prompt, x_sample, c, state_gla, state_gdn, state_s5_re, state_s5_im, c_ctx, w_ada, b_ada, norm_w, ffn_w_gate, ffn_w_up, ffn_w_down, w_in, gla_gk_up, gla_gk_bias, gla_norm_w, gdn_conv_w, gdn_a_log, gdn_dt_bias, gdn_norm_w, s5_lam_re, s5_lam_im, s5_log_step, s5_b_re, s5_b_im, s5_c_re, s5_c_im, s5_d, s5_w_glu, s5_b_glu, w_out, final_norm_w):
    p = dict(norm_w=norm_w, ffn_w_gate=ffn_w_gate, ffn_w_up=ffn_w_up, ffn_w_down=ffn_w_down,
             gla_gk_up=gla_gk_up, gla_gk_bias=gla_gk_bias, gla_norm_w=gla_norm_w, gdn_conv_w=gdn_conv_w,
             gdn_a_log=gdn_a_log, gdn_dt_bias=gdn_dt_bias, gdn_norm_w=gdn_norm_w, s5_c_re=s5_c_re,
             s5_c_im=s5_c_im, s5_d=s5_d, s5_w_glu=s5_w_glu, s5_b_glu=s5_b_glu, w_out=w_out,
             final_norm_w=final_norm_w)
    b_ctx = x_prompt.shape[0]
    b_dec = x_sample.shape[0]
    cond8 = jnp.zeros((8, D_MODEL), F32).at[0].set(c_ctx).at[1:1 + b_dec].set(c)
    mods = _ada(cond8, w_ada, b_ada)
    mods_ctx = mods[:, 0:1].reshape(DEPTH, 1, N_MOD, D_MODEL)
    mods_dec = mods[:, 1:1 + b_dec].reshape(DEPTH, b_dec, N_MOD, D_MODEL)
    s5c = _s5_consts(s5_lam_re, s5_lam_im, s5_log_step, s5_b_re, s5_b_im, s5_c_re, s5_c_im)
    w_in_p = _permute_w_in(w_in)

    y_prompt, fin = _trunk(x_prompt, mods_ctx, 1, None, p, s5c, w_in_p)
    new_gla = jnp.stack([_gla_state_out(f[0]) for f in fin], axis=1)
    new_gdn = jnp.stack([f[1].transpose(1, 0, 2, 3, 4) for f in fin], axis=1)
    s5f = jnp.stack([f[2].reshape(2, 2, b_ctx, S5_GROUPS, S5_P).transpose(1, 2, 0, 3, 4) for f in fin], axis=2)
    new_re, new_im = s5f[0], s5f[1]

    rows = x_sample.shape[1] // 64
    y_sample, _ = _trunk(x_sample, mods_dec, rows, (state_gla, state_gdn, state_s5_re, state_s5_im),
                         p, s5c, w_in_p)
    return (y_prompt, y_sample, new_gla.astype(state_gla.dtype), new_gdn.astype(state_gdn.dtype),
            new_re.astype(state_s5_re.dtype), new_im.astype(state_s5_im.dtype))
```

```python
import functools
import math

import numpy as np
import jax
import jax.numpy as jnp
from jax import lax
from jax.experimental import pallas as pl
from jax.experimental.pallas import tpu as pltpu

F32 = jnp.float32
BF16 = jnp.bfloat16

D_MODEL = 1024
DEPTH = 2
FFN_DIM = 2816
N_MOD = 9
EPS = 1e-6
CHUNK = 64
GLA_HEADS, GLA_DK, GLA_DV, GLA_RANK = 4, 32, 64, 16
GLA_GATE_NORM = 16.0
GDN_HEADS, GDN_DK, GDN_DV, GDN_CONV = 4, 128, 128, 5
S5_GROUPS, S5_CH, S5_P = 16, 16, 64
GLA_W = GLA_HEADS * GLA_DV
GLA_KW = GLA_HEADS * GLA_DK
GDN_W = GDN_HEADS * GDN_DV
S5_W = S5_GROUPS * S5_CH
S5_N = S5_GROUPS * S5_P
S5_L = 16
S5_PAIRS = S5_GROUPS // 2
MIX_W = GLA_W + GDN_W + S5_W

_IN_SIZES = (GLA_KW, GLA_KW, GLA_W, GLA_RANK, GLA_RANK, GLA_W,
             GDN_W, GDN_W, GDN_W, GDN_HEADS, GDN_HEADS, GDN_HEADS, GDN_HEADS, GDN_W, S5_W)
_IN_OFF = tuple(int(v) for v in np.cumsum((0,) + _IN_SIZES))
(_GQ, _GK, _GV, _GLRF, _GLRB, _GOG, _DQ, _DK, _DV, _DAF, _DAB, _DBF, _DBB, _DOG, _SU) = range(15)

P_DQKV = 0
P_SMALL = 1536
P_GQK = 1792
P_GV = 2048
P_GOG = 2304
P_DOG = 2560
P_SU = 3072
PROJ_W = 3328
PROJ_TN = PROJ_W // 2
SM_A = GLA_RANK
SM_B = GLA_RANK + GDN_HEADS
SUB = 256
VMEM_LIMIT = 56 * 1024 * 1024


def _cparams(sem):
    return pltpu.CompilerParams(dimension_semantics=sem, vmem_limit_bytes=VMEM_LIMIT)


def _bdot(a, b):
    return jnp.dot(a.astype(BF16), b.astype(BF16), preferred_element_type=F32)


def _bdot_nt(a, b):
    return lax.dot_general(a.astype(BF16), b.astype(BF16), (((1,), (1,)), ((), ())),
                           preferred_element_type=F32)


def _split3(x):
    hi = x.astype(BF16)
    r1 = x - hi.astype(F32)
    mid = r1.astype(BF16)
    lo = (r1 - mid.astype(F32)).astype(BF16)
    return hi, mid, lo


def _sel_dot(m01, x):
    hi, mid, lo = _split3(x)
    return (jnp.dot(m01, hi, preferred_element_type=F32) + jnp.dot(m01, mid, preferred_element_type=F32)
            + jnp.dot(m01, lo, preferred_element_type=F32))


def _silu(x):
    return x * jax.nn.sigmoid(x)


def _softplus(x):
    return jnp.maximum(x, 0.0) + jnp.log1p(jnp.exp(-jnp.abs(x)))


def _log_sigmoid(x):
    return -_softplus(-x)


def _norm_mod(x, nw, scale, shift):
    y = x * lax.rsqrt(jnp.mean(x * x, axis=-1, keepdims=True) + EPS)
    return y * nw * (1.0 + scale) + shift


def _pick_tile(n, cap):
    t = cap
    while n % t:
        t //= 2
    return t


def _ada_kernel(c_ref, w_ref, b_ref, o_ref):
    o_ref[...] = _bdot(_silu(c_ref[...]), w_ref[...]) + b_ref[...]


def _ada(cond8, w_ada, b_ada):
    depth, d, nout = w_ada.shape
    tn = 1024
    return pl.pallas_call(
        _ada_kernel,
        grid=(depth, nout // tn),
        in_specs=[pl.BlockSpec((8, d), lambda l, j: (0, 0)),
                  pl.BlockSpec((None, d, tn), lambda l, j: (l, 0, j)),
                  pl.BlockSpec((None, 1, tn), lambda l, j: (l, 0, j))],
        out_specs=pl.BlockSpec((None, 8, tn), lambda l, j: (l, 0, j)),
        out_shape=jax.ShapeDtypeStruct((depth, 8, nout), F32),
        compiler_params=_cparams(("parallel", "parallel")),
        name="ada",
    )(cond8, w_ada, b_ada.reshape(depth, 1, nout))


def _ffn_kernel(x_ref, mod_ref, nw_ref, wg_ref, wu_ref, wd_ref, fw_ref, o_ref, h_ref, *, mod_base, final):
    j = pl.program_id(1)

    @pl.when(j == 0)
    def _():
        m = mod_ref[...]
        h = _norm_mod(x_ref[...], nw_ref[...], m[mod_base + 1:mod_base + 2], m[mod_base:mod_base + 1])
        h_ref[...] = h.astype(BF16)
        o_ref[...] = jnp.zeros_like(o_ref)

    h = h_ref[...]
    g = jnp.dot(h, wg_ref[...].astype(BF16), preferred_element_type=F32)
    u = jnp.dot(h, wu_ref[...].astype(BF16), preferred_element_type=F32)
    o_ref[...] += _bdot(_silu(g) * u, wd_ref[...])

    @pl.when(j == pl.num_programs(1) - 1)
    def _():
        m = mod_ref[...]
        y = x_ref[...] + 0.5 * m[mod_base + 2:mod_base + 3] * o_ref[...]
        if final:
            y = y * lax.rsqrt(jnp.mean(y * y, axis=-1, keepdims=True) + EPS) * fw_ref[...]
        o_ref[...] = y


def _mod_index(nb, tm, n_seq):
    if nb == 1:
        return lambda i: 0
    return lambda i: (i * tm) // n_seq


def _ffn(x, mods, norm_w, w_gate, w_up, w_down, final_w, *, layer, which, n_seq, final):
    t, d = x.shape
    f = w_gate.shape[-1]
    nb = mods.shape[0]
    tm = _pick_tile(n_seq if nb > 1 else t, 2048)
    tf = 256
    midx = _mod_index(nb, tm, n_seq)
    mod_base = 0 if which == 0 else 6
    return pl.pallas_call(
        functools.partial(_ffn_kernel, mod_base=mod_base, final=final),
        grid=(t // tm, f // tf),
        in_specs=[pl.BlockSpec((tm, d), lambda i, j: (i, 0), pipeline_mode=pl.Buffered(1)),
                  pl.BlockSpec((None, N_MOD, d), lambda i, j: (midx(i), 0, 0)),
                  pl.BlockSpec((None, None, 1, d), lambda i, j: (layer, 2 * which, 0, 0)),
                  pl.BlockSpec((None, None, d, tf), lambda i, j: (layer, which, 0, j)),
                  pl.BlockSpec((None, None, d, tf), lambda i, j: (layer, which, 0, j)),
                  pl.BlockSpec((None, None, tf, d), lambda i, j: (layer, which, j, 0)),
                  pl.BlockSpec((1, d), lambda i, j: (0, 0))],
        out_specs=pl.BlockSpec((tm, d), lambda i, j: (i, 0)),
        out_shape=jax.ShapeDtypeStruct((t, d), F32),
        scratch_shapes=[pltpu.VMEM((tm, d), BF16)],
        compiler_params=_cparams(("parallel", "arbitrary")),
        name="ffn",
    )(x, mods, norm_w.reshape(DEPTH, 3, 1, d), w_gate, w_up, w_down, final_w.reshape(1, d))


def _proj_kernel(x_ref, mod_ref, nw_ref, w_ref, cw_ref, o_ref, h_ref, wb_ref, *, rowlen):
    j = pl.program_id(0)
    i = pl.program_id(1)

    @pl.when(i == 0)
    def _():
        wb_ref[...] = w_ref[...].astype(BF16)

    @pl.when(j == 0)
    def _():
        m = mod_ref[...]
        h_ref[i] = _norm_mod(x_ref[...], nw_ref[...], m[4:5], m[3:4]).astype(BF16)
        o_ref[...] = jnp.dot(h_ref[i], wb_ref[...], preferred_element_type=F32)
        tm = o_ref.shape[0]
        pos = lax.broadcasted_iota(jnp.int32, (tm, 1), 0) % rowlen
        valid = [jnp.logical_and(pos + (t - GDN_CONV // 2) >= 0, pos + (t - GDN_CONV // 2) < rowlen)
                 for t in range(GDN_CONV)]
        for g in range(3 * GDN_HEADS):
            cols = slice(P_DQKV + g * GDN_DK, P_DQKV + (g + 1) * GDN_DK)
            xg = o_ref[:, cols]
            acc = xg * cw_ref[GDN_CONV // 2:GDN_CONV // 2 + 1, cols]
            for t in range(GDN_CONV):
                s = t - GDN_CONV // 2
                if s != 0:
                    xs = pltpu.roll(xg, (-s) % tm, axis=0)
                    acc = acc + jnp.where(valid[t], xs, 0.0) * cw_ref[t:t + 1, cols]
            y = _silu(acc)
            if g < 2 * GDN_HEADS:
                y = y * lax.rsqrt(jnp.sum(y * y, axis=-1, keepdims=True) + EPS)
                if g < GDN_HEADS:
                    y = y * (GDN_DK ** -0.5)
            o_ref[:, cols] = y

    @pl.when(j == 1)
    def _():
        o_ref[...] = jnp.dot(h_ref[i], wb_ref[...], preferred_element_type=F32)


def _proj(x, mods, norm_w, w_in_p, conv_w, *, layer, n_seq, rowlen):
    t, d = x.shape
    nb = mods.shape[0]
    tm = _pick_tile(n_seq if nb > 1 else t, 512)
    midx = _mod_index(nb, tm, n_seq)
    return pl.pallas_call(
        functools.partial(_proj_kernel, rowlen=rowlen),
        grid=(PROJ_W // PROJ_TN, t // tm),
        in_specs=[pl.BlockSpec((tm, d), lambda j, i: (i * (1 - j), 0)),
                  pl.BlockSpec((None, N_MOD, d), lambda j, i: (midx(i), 0, 0)),
                  pl.BlockSpec((None, None, 1, d), lambda j, i: (layer, 1, 0, 0)),
                  pl.BlockSpec((None, d, PROJ_TN), lambda j, i: (layer, 0, j)),
                  pl.BlockSpec((None, GDN_CONV, 3 * GDN_W), lambda j, i: (layer, 0, 0))],
        out_specs=pl.BlockSpec((tm, PROJ_TN), lambda j, i: (i, j)),
        out_shape=jax.ShapeDtypeStruct((t, PROJ_W), F32),
        scratch_shapes=[pltpu.VMEM((t // tm, tm, d), BF16), pltpu.VMEM((d, PROJ_TN), BF16)],
        compiler_params=_cparams(("arbitrary", "arbitrary")),
        name="proj",
    )(x, mods, norm_w.reshape(DEPTH, 3, 1, d), w_in_p, conv_w)


def _permute_w_in(w_in):
    def seg(i):
        return w_in[:, :, _IN_OFF[i]:_IN_OFF[i + 1]]

    def pad(n):
        return jnp.zeros(w_in.shape[:2] + (n,), w_in.dtype)

    sm_pad = 128 - GLA_RANK - 2 * GDN_HEADS
    return jnp.concatenate(
        [seg(_DQ), seg(_DK), seg(_DV),
         seg(_GLRF), seg(_DAF), seg(_DBF), pad(sm_pad),
         seg(_GLRB), seg(_DAB), seg(_DBB), pad(sm_pad),
         seg(_GQ), seg(_GK), seg(_GV), seg(_GOG), seg(_DOG), seg(_SU)], axis=-1)


def _np_consts():
    t = np.arange(CHUNK)
    lower = (t[:, None] >= t[None, :])
    incl = np.stack([lower, lower.T])
    strict = np.stack([t[:, None] > t[None, :], t[:, None] < t[None, :]])
    eye4 = np.eye(4, dtype=bool)
    blk = np.kron(eye4, np.ones((CHUNK, CHUNK), bool))
    incl_bd = np.stack([np.kron(eye4, incl[d]) for d in range(2)])
    strict_bd = np.stack([np.kron(eye4, strict[d]) for d in range(2)])
    levels = []
    for d in range(2):
        per = []
        for lv in range(6):
            s = 1 << lv
            same = (t[:, None] // (2 * s)) == (t[None, :] // (2 * s))
            hi = (t % (2 * s)) >= s
            m = same & hi[:, None] & (~hi)[None, :]
            if d == 1:
                m = m.T
            per.append(np.kron(eye4, m))
        levels.append(np.stack(per))
    levels = np.stack(levels)
    assert (levels.sum(1) == strict_bd).all() and blk.any()
    gla_mask = np.stack([np.tile(incl[d], (1, 4)) for d in range(2)])
    hm_k = np.kron(eye4, np.ones((1, GLA_DK)))
    hm_v = np.kron(eye4, np.ones((1, GLA_DV)))
    st_mask = np.kron(eye4, np.ones((GLA_DV, GLA_DK)))
    ind64 = np.kron(eye4, np.full((GLA_DV, GLA_DV), 1.0 / GLA_DV))
    return dict(incl=incl, incl_bd=incl_bd, strict_bd=strict_bd, levels=levels, gla_mask=gla_mask,
                hm_k=hm_k, hm_v=hm_v, st_mask=st_mask, ind64=ind64, eye256=np.eye(256))


_C = _np_consts()


def _row_block_index(nblk):
    return lambda d, b, i: b * nblk + i + d * (nblk - 1 - 2 * i)


def _gla_kernel(qk_ref, v_ref, sm_ref, up_ref, bias_ref, tri_ref, mask_ref, hmk_ref, hmv_ref, stm_ref, *rest,
                nchunk, zero_init):
    if zero_init:
        o_ref, stf_ref, st_ref, qd_sc, m_sc, dec_sc = rest
    else:
        st0_ref, o_ref, stf_ref, st_ref, qd_sc, m_sc, dec_sc = rest
    d = pl.program_id(0)

    @pl.when(pl.program_id(2) == 0)
    def _():
        st_ref[...] = jnp.zeros_like(st_ref) if zero_init else st0_ref[...]

    tri = tri_ref[...]
    mask = mask_ref[...]
    hmk = hmk_ref[...]
    hmv = hmv_ref[...]
    stm = stm_ref[...]

    for c in range(nchunk):
        rows = pl.ds(c * CHUNK, CHUNK)
        qk = qk_ref[rows, :]
        q = qk[:, 0:GLA_KW] * (GLA_DK ** -0.5)
        k = qk[:, GLA_KW:2 * GLA_KW]
        v = v_ref[rows, :]
        x = _bdot(sm_ref[rows, :], up_ref[...]) + bias_ref[...]
        lg = _log_sigmoid(x) * (1.0 / GLA_GATE_NORM)
        b = _sel_dot(tri, lg)
        btot = jnp.sum(lg, axis=0, keepdims=True)
        qd = (q * jnp.exp(b)).astype(BF16)
        ki = k * jnp.exp(-b)
        kd = k * jnp.exp(btot - b)
        kexp = jnp.concatenate([ki * hmk[h:h + 1] for h in range(GLA_HEADS)], axis=0)
        att = _bdot_nt(qd, kexp) * mask
        vexp = jnp.concatenate([v * hmv[h:h + 1] for h in range(GLA_HEADS)], axis=0)
        o_ref[rows, :] = _bdot(att, vexp)
        qd_sc[c] = qd
        m_sc[c] = _bdot(v.T, kd) * stm
        dec_sc[c] = jnp.broadcast_to(jnp.exp(btot), (8, GLA_KW))

    st = st_ref[...]
    for ci in range(nchunk):
        c = ci + d * (nchunk - 1 - 2 * ci)
        rows = pl.ds(pl.multiple_of(c * CHUNK, CHUNK), CHUNK)
        o_ref[rows, :] += lax.dot_general(qd_sc[c], st.astype(BF16), (((1,), (1,)), ((), ())),
                                          preferred_element_type=F32)
        st = st * dec_sc[c][0:1] + m_sc[c]
    st_ref[...] = st
    stf_ref[...] = st


def _gla(proj, up_p, bias_p, st0, *, nseq, n_seq):
    t = proj.shape[0]
    nblk = n_seq // SUB
    nck = SUB // CHUNK
    rbi = _row_block_index(nblk)
    c = _C
    consts = [jnp.asarray(c["incl"], BF16), jnp.asarray(c["gla_mask"], F32), jnp.asarray(c["hm_k"], F32),
              jnp.asarray(c["hm_v"], F32), jnp.asarray(c["st_mask"], F32)]
    st_spec = pl.BlockSpec((None, None, GLA_W, GLA_KW), lambda d, b, i: (d, b, 0, 0))
    zero_init = st0 is None
    return pl.pallas_call(
        functools.partial(_gla_kernel, nchunk=nck, zero_init=zero_init),
        grid=(2, nseq, nblk),
        in_specs=[pl.BlockSpec((SUB, 2 * GLA_KW), lambda d, b, i: (rbi(d, b, i), P_GQK // (2 * GLA_KW))),
                  pl.BlockSpec((SUB, GLA_W), lambda d, b, i: (rbi(d, b, i), P_GV // GLA_W)),
                  pl.BlockSpec((SUB, 128), lambda d, b, i: (rbi(d, b, i), P_SMALL // 128 + d)),
                  pl.BlockSpec((None, 128, GLA_KW), lambda d, b, i: (d, 0, 0)),
                  pl.BlockSpec((None, 1, GLA_KW), lambda d, b, i: (d, 0, 0)),
                  pl.BlockSpec((None, CHUNK, CHUNK), lambda d, b, i: (d, 0, 0)),
                  pl.BlockSpec((None, CHUNK, 256), lambda d, b, i: (d, 0, 0)),
                  pl.BlockSpec((GLA_HEADS, GLA_KW), lambda d, b, i: (0, 0)),
                  pl.BlockSpec((GLA_HEADS, GLA_W), lambda d, b, i: (0, 0)),
                  pl.BlockSpec((GLA_W, GLA_KW), lambda d, b, i: (0, 0))] + ([] if zero_init else [st_spec]),
        out_specs=[pl.BlockSpec((None, SUB, GLA_W), lambda d, b, i: (d, rbi(d, b, i), 0)), st_spec],
        out_shape=[jax.ShapeDtypeStruct((2, t, GLA_W), F32),
                   jax.ShapeDtypeStruct((2, nseq, GLA_W, GLA_KW), F32)],
        scratch_shapes=[pltpu.VMEM((GLA_W, GLA_KW), F32), pltpu.VMEM((nck, CHUNK, GLA_KW), BF16),
                        pltpu.VMEM((nck, GLA_W, GLA_KW), F32), pltpu.VMEM((nck, 8, GLA_KW), F32)],
        compiler_params=_cparams(("arbitrary", "arbitrary", "arbitrary")),
        name="gla",
    )(proj, proj, proj, up_p, bias_p, *consts, *([] if zero_init else [st0]))


def _gdn_kernel(qkvf_ref, qkvb_ref, smf_ref, smb_ref, arow_ref, dtb_ref, tri_ref, incl_ref, strict_ref, lvl_ref,
                eye_ref, *rest, nchunk, zero_init):
    if zero_init:
        of_ref, ob_ref, sf_ref, s_ref, la_ref, be_ref, w_sc, u0_sc, qkd_sc, qe_sc, kdt_sc, egl_sc = rest
    else:
        s0_ref, of_ref, ob_ref, sf_ref, s_ref, la_ref, be_ref, w_sc, u0_sc, qkd_sc, qe_sc, kdt_sc, egl_sc = rest
    qkv_refs = (qkvf_ref, qkvb_ref)
    o_refs = (of_ref, ob_ref)

    @pl.when(pl.program_id(1) == 0)
    def _():
        s_ref[...] = jnp.zeros_like(s_ref) if zero_init else s0_ref[...]

    for dd, sm_ref in enumerate((smf_ref, smb_ref)):
        sm = sm_ref[...]
        la_ref[dd] = -jnp.exp(arow_ref[dd]) * _softplus(sm + dtb_ref[dd])
        be_ref[dd] = jax.nn.sigmoid(sm)

    eye = eye_ref[...]

    def stack(ref, r0, lo, width):
        return jnp.concatenate([ref[pl.ds(r0, CHUNK), lo + h * width:lo + (h + 1) * width]
                                for h in range(GDN_HEADS)], axis=0)

    def colstack(a, lo):
        return jnp.concatenate([a[:, lo + h:lo + h + 1] for h in range(GDN_HEADS)], axis=0)

    units = [(dd, c) for c in range(nchunk) for dd in range(2)]
    loc = []
    for dd, c in units:
        r0 = c * CHUNK
        n = dd * nchunk + c
        qkv_ref = qkv_refs[dd]
        qs = stack(qkv_ref, r0, 0, GDN_DK)
        ks = stack(qkv_ref, r0, GDN_W, GDN_DK)
        vs = stack(qkv_ref, r0, 2 * GDN_W, GDN_DV)
        la = la_ref[dd, pl.ds(r0, CHUNK), :]
        g = _sel_dot(tri_ref[dd], la)
        gtot = jnp.sum(la, axis=0, keepdims=True)
        gcol = colstack(g, SM_A)
        kfcol = colstack(jnp.exp(gtot - g), SM_A)
        egcol = colstack(jnp.exp(g), SM_A)
        bcol = colstack(be_ref[dd, pl.ds(r0, CHUNK), :], SM_B)
        grow = jnp.sum(eye * gcol, axis=0, keepdims=True)
        dec = incl_ref[dd] * jnp.exp(jnp.minimum(gcol - grow, 0.0))
        ksb = ks.astype(BF16)
        lmat = strict_ref[dd] * dec * _bdot_nt(ksb, ksb) * bcol
        rhs = jnp.concatenate([ks * (bcol * egcol), vs * bcol], axis=1).astype(BF16)
        qkd_sc[n] = (_bdot_nt(qs, ksb) * dec).astype(BF16)
        qe_sc[n] = (qs * egcol).astype(BF16)
        kd = ks * kfcol
        for h in range(GDN_HEADS):
            kdt_sc[n * GDN_HEADS + h] = kd[h * CHUNK:(h + 1) * CHUNK].T.astype(BF16)
        egl_sc[n] = jnp.broadcast_to(jnp.exp(gtot), (8, 128))
        loc.append((lmat.astype(BF16), rhs))
    tinv = [eye - (lm * lvl_ref[dd, 0]).astype(F32) for (dd, _), (lm, _) in zip(units, loc)]
    for lv in range(1, 6):
        tb = [t_.astype(BF16) for t_ in tinv]
        cs = [jnp.dot(lm * lvl_ref[dd, lv], t_, preferred_element_type=F32)
              for (dd, _), (lm, _), t_ in zip(units, loc, tb)]
        tinv = [t32 - jnp.dot(t_, c_.astype(BF16), preferred_element_type=F32)
                for t32, t_, c_ in zip(tinv, tb, cs)]
    for (dd, c), (_, rhs), t_ in zip(units, loc, tinv):
        sol = jnp.dot(t_.astype(BF16), rhs, preferred_element_type=F32)
        w_sc[dd * nchunk + c] = sol[:, :GDN_DK].astype(BF16)
        u0_sc[dd * nchunk + c] = sol[:, GDN_DK:]

    s_cur = [[s_ref[dd, h] for h in range(GDN_HEADS)] for dd in range(2)]
    for ci in range(nchunk):
        for dd in range(2):
            c = ci if dd == 0 else nchunk - 1 - ci
            n = dd * nchunk + c
            sb = [s_.astype(BF16) for s_ in s_cur[dd]]
            us = []
            for h in range(GDN_HEADS):
                rows = pl.ds(h * CHUNK, CHUNK)
                us.append(u0_sc[n, rows, :] - jnp.dot(w_sc[n, rows, :], sb[h], preferred_element_type=F32))
            ub = [u_.astype(BF16) for u_ in us]
            o2 = jnp.dot(qkd_sc[n], jnp.concatenate(ub, axis=0), preferred_element_type=F32)
            egl = egl_sc[n]
            for h in range(GDN_HEADS):
                rows = pl.ds(h * CHUNK, CHUNK)
                o_refs[dd][pl.ds(c * CHUNK, CHUNK), h * GDN_DV:(h + 1) * GDN_DV] = (
                    jnp.dot(qe_sc[n, rows, :], sb[h], preferred_element_type=F32) + o2[h * CHUNK:(h + 1) * CHUNK])
                s_cur[dd][h] = (egl[0:1, SM_A + h:SM_A + h + 1] * s_cur[dd][h]
                                + jnp.dot(kdt_sc[n * GDN_HEADS + h], ub[h], preferred_element_type=F32))
    for dd in range(2):
        for h in range(GDN_HEADS):
            s_ref[dd, h] = s_cur[dd][h]
    sf_ref[...] = s_ref[...]


def _gdn(proj, arow, dtb, s0, *, nseq, n_seq):
    t = proj.shape[0]
    nblk = n_seq // SUB
    rbi = _row_block_index(nblk)
    c = _C
    consts = [jnp.asarray(c["incl"], BF16), jnp.asarray(c["incl_bd"], F32), jnp.asarray(c["strict_bd"], F32),
              jnp.asarray(c["levels"], BF16), jnp.asarray(c["eye256"], F32)]
    hs = (GDN_HEADS, GDN_DK, GDN_DV)
    nck = SUB // CHUNK
    nu = 2 * nck
    zero_init = s0 is None
    s_spec = pl.BlockSpec((2, None) + hs, lambda b, i: (0, b, 0, 0, 0))
    full = lambda shape: pl.BlockSpec(shape, lambda b, i: (0,) * len(shape))
    rows = (lambda b, i: rbi(0, b, i), lambda b, i: rbi(1, b, i))
    return pl.pallas_call(
        functools.partial(_gdn_kernel, nchunk=nck, zero_init=zero_init),
        grid=(nseq, nblk),
        in_specs=[pl.BlockSpec((SUB, 3 * GDN_W), lambda b, i: (rows[0](b, i), P_DQKV // (3 * GDN_W))),
                  pl.BlockSpec((SUB, 3 * GDN_W), lambda b, i: (rows[1](b, i), P_DQKV // (3 * GDN_W))),
                  pl.BlockSpec((SUB, 128), lambda b, i: (rows[0](b, i), P_SMALL // 128)),
                  pl.BlockSpec((SUB, 128), lambda b, i: (rows[1](b, i), P_SMALL // 128 + 1)),
                  full((2, 1, 128)), full((2, 1, 128)), full((2, CHUNK, CHUNK)),
                  full((2, 256, 256)), full((2, 256, 256)), full((2, 6, 256, 256)), full((256, 256))]
                 + ([] if zero_init else [s_spec]),
        out_specs=[pl.BlockSpec((SUB, GDN_W), lambda b, i: (rows[0](b, i), 0)),
                   pl.BlockSpec((SUB, GDN_W), lambda b, i: (rows[1](b, i), 0)), s_spec],
        out_shape=[jax.ShapeDtypeStruct((t, GDN_W), F32), jax.ShapeDtypeStruct((t, GDN_W), F32),
                   jax.ShapeDtypeStruct((2, nseq) + hs, F32)],
        scratch_shapes=[pltpu.VMEM((2,) + hs, F32),
                        pltpu.VMEM((2, SUB, 128), F32), pltpu.VMEM((2, SUB, 128), F32),
                        pltpu.VMEM((nu, 256, GDN_DK), BF16), pltpu.VMEM((nu, 256, GDN_DV), F32),
                        pltpu.VMEM((nu, 256, 256), BF16), pltpu.VMEM((nu, 256, GDN_DK), BF16),
                        pltpu.VMEM((nu * GDN_HEADS, GDN_DK, CHUNK), BF16), pltpu.VMEM((nu, 8, 128), F32)],
        compiler_params=_cparams(("arbitrary", "arbitrary")),
        name="gdn",
    )(proj, proj, proj, proj, arow, dtb, *consts, *([] if zero_init else [s0]))


def _cmul(ar, ai, br, bi):
    return ar * br - ai * bi, ar * bi + ai * br


def _split2(x):
    hi = x.astype(BF16)
    return hi, (x - hi.astype(F32)).astype(BF16)


def _s5_consts_kernel(lr_ref, li_ref, st_ref, br_ref, bi_ref, cr_ref, ci_ref, gm_ref, bm_ref,
                      laml_ref, wexp_ref, vexp_ref, tt_ref, w_ref, v_ref, a_ref, q_ref):
    d = pl.program_id(0) % 2
    lc = S5_L * S5_CH

    def trows(s):
        return pl.ds(pl.multiple_of((s + d * (S5_L - 1 - 2 * s)) * S5_CH, S5_CH), S5_CH)

    lr = lr_ref[...]
    li = li_ref[...]
    step = jnp.exp(st_ref[...])
    mag = jnp.exp(lr * step)
    lbr = mag * jnp.cos(li * step)
    lbi = mag * jnp.sin(li * step)
    nr = lbr - 1.0
    den = lr * lr + li * li
    fr = (nr * lr + lbi * li) / den
    fi = (lbi * lr - nr * li) / den
    bbr, bbi = _cmul(fr, fi, br_ref[...], bi_ref[...])
    cr = cr_ref[...]
    ci = ci_ref[...]
    pr = jnp.ones_like(lr)
    pi = jnp.zeros_like(lr)
    for s in range(S5_L):
        ar, ai = _cmul(cr, ci, pr, pi)
        a_ref[0, trows(s), :] = ar
        a_ref[1, trows(s), :] = ai
        wr, wi = _cmul(pr, pi, bbr, bbi)
        w_ref[0, trows(S5_L - 1 - s), :] = wr
        w_ref[1, trows(S5_L - 1 - s), :] = wi
        pr, pi = _cmul(pr, pi, lbr, lbi)
        vr, vi = _cmul(cr, ci, pr, pi)
        v_ref[0, trows(s), :] = vr
        v_ref[1, trows(s), :] = -vi
    laml_ref[0:1, :] = pr
    laml_ref[1:2, :] = pi

    lane = lax.broadcasted_iota(jnp.int32, (1, 128), 1)
    for gp in range(S5_PAIRS):
        lanes = slice(gp * 128, (gp + 1) * 128)
        for r in range(2):
            for src, dst in ((w_ref, wexp_ref), (v_ref, vexp_ref)):
                slab = src[r, :, lanes]
                dst[r, gp] = jnp.concatenate([jnp.where(lane < S5_P, slab, 0.0),
                                              jnp.where(lane >= S5_P, slab, 0.0)], axis=0).astype(BF16)

    gm = gm_ref[...]
    kall = None
    for r, b, sign in ((0, bbr, 1.0), (1, bbi, -1.0)):
        ah, al = _split2(a_ref[r])
        bh, bl = _split2(jnp.concatenate([b] * S5_GROUPS, axis=0) * gm)
        nt = lambda x, y: lax.dot_general(x, y, (((1,), (1,)), ((), ())), preferred_element_type=F32)
        term = (nt(ah, bh) + nt(ah, bl) + nt(al, bh)) * sign
        kall = term if kall is None else kall + term

    zeros = jnp.zeros((lc, S5_W), F32)
    for delta in range(S5_GROUPS):
        q_ref[delta, 0:lc, :] = zeros
        q_ref[delta, lc:2 * lc, :] = kall if delta == 0 else pltpu.roll(kall, delta * S5_CH, axis=1)
        q_ref[delta, 2 * lc:3 * lc, :] = zeros
    for g in range(S5_GROUPS):
        acc = zeros
        for s in range(S5_L):
            start = pl.multiple_of(lc - S5_CH * s + d * (S5_CH * (S5_L - 1)), S5_CH)
            blk = q_ref[(s - g) % S5_GROUPS, pl.ds(start, lc), :]
            acc = jnp.where(bm_ref[s:s + 1, :] > 0.5, blk, acc)
        tt_ref[g] = acc.astype(BF16)


def _s5_consts(lam_re, lam_im, log_step, b_re, b_im, c_re, c_im):
    n = DEPTH * 2
    lc = S5_L * S5_CH
    lr = lam_re.reshape(n, 1, S5_N)
    li = lam_im.reshape(n, 1, S5_N)
    st = jnp.repeat(log_step.reshape(n, S5_GROUPS), S5_P, axis=-1).reshape(n, 1, S5_N)
    br = b_re.reshape(n, S5_N, S5_CH).transpose(0, 2, 1)
    bi = b_im.reshape(n, S5_N, S5_CH).transpose(0, 2, 1)
    cr = c_re.reshape(n, S5_GROUPS, S5_CH, S5_P).transpose(0, 2, 1, 3).reshape(n, S5_CH, S5_N)
    ci = c_im.reshape(n, S5_GROUPS, S5_CH, S5_P).transpose(0, 2, 1, 3).reshape(n, S5_CH, S5_N)
    gm = jnp.asarray(np.kron(np.eye(S5_GROUPS), np.ones((S5_CH, S5_P))), F32)
    row = pl.BlockSpec((None, 1, S5_N), lambda i: (i, 0, 0))
    mat = pl.BlockSpec((None, S5_CH, S5_N), lambda i: (i, 0, 0))
    exp = pl.BlockSpec((None, 2, S5_PAIRS, 2 * lc, 128), lambda i: (i, 0, 0, 0, 0))
    laml, wexp, vexp, tt = pl.pallas_call(
        _s5_consts_kernel,
        grid=(n,),
        in_specs=[row, row, row, mat, mat, mat, mat, pl.BlockSpec((S5_W, S5_N), lambda i: (0, 0)),
                  pl.BlockSpec((S5_GROUPS, S5_W), lambda i: (0, 0))],
        out_specs=[pl.BlockSpec((None, 2, S5_N), lambda i: (i, 0, 0)), exp, exp,
                   pl.BlockSpec((None, S5_GROUPS, lc, lc), lambda i: (i, 0, 0, 0))],
        out_shape=[jax.ShapeDtypeStruct((n, 2, S5_N), F32),
                   jax.ShapeDtypeStruct((n, 2, S5_PAIRS, 2 * lc, 128), BF16),
                   jax.ShapeDtypeStruct((n, 2, S5_PAIRS, 2 * lc, 128), BF16),
                   jax.ShapeDtypeStruct((n, S5_GROUPS, lc, lc), BF16)],
        scratch_shapes=[pltpu.VMEM((2, lc, S5_N), F32), pltpu.VMEM((2, lc, S5_N), F32),
                        pltpu.VMEM((2, lc, S5_N), F32), pltpu.VMEM((S5_GROUPS, 3 * lc, S5_W), F32)],
        compiler_params=_cparams(("arbitrary",)),
        name="s5_consts",
    )(lr, li, st, br, bi, cr, ci, gm, _lane_block_masks())
    return laml.reshape(n, 2, S5_PAIRS, 128), wexp, vexp, tt


def _lane_block_masks():
    return jnp.asarray(np.kron(np.eye(S5_GROUPS), np.ones((1, S5_CH))), F32)


def _nt(a, b):
    return lax.dot_general(a, b, (((1,), (1,)), ((), ())), preferred_element_type=F32)


def _lane_block_transpose(arrs):
    n = len(arrs)
    width = arrs[0].shape[1]
    blk = lax.broadcasted_iota(jnp.int32, (1, width), 1) // (width // n)
    arrs = list(arrs)
    h = n // 2
    while h >= 1:
        hi = (blk & h) != 0
        for a in range(n):
            if a & h:
                continue
            lo_arr, hi_arr = arrs[a], arrs[a + h]
            arrs[a] = jnp.where(hi, pltpu.roll(hi_arr, h * (width // n), axis=1), lo_arr)
            arrs[a + h] = jnp.where(hi, hi_arr, pltpu.roll(lo_arr, width - h * (width // n), axis=1))
        h //= 2
    return arrs


def _s5_kernel(u0_ref, u1_ref, tt_ref, wexp_ref, vexp_ref, laml_ref, *rest, nseq, nblk, zero_init):
    if zero_init:
        y_ref, hf_ref, ug_sc, yg_sc, hin_re, hin_im, hp_re, hp_im = rest
    else:
        h0_ref, y_ref, hf_ref, ug_sc, yg_sc, hin_re, hin_im, hp_re, hp_im = rest
    d = pl.program_id(0)
    r = nblk * nseq
    lc = S5_L * S5_CH

    @pl.when(d == 0)
    def _():
        xs = []
        for s in range(S5_L):
            rows = pl.ds(s, r, stride=S5_L)
            xs.append(jnp.concatenate([u0_ref[rows, :], u1_ref[rows, :]], axis=1))
        for g, ug in enumerate(_lane_block_transpose(xs)):
            ug_sc[g] = ug.astype(BF16)

    for gp in range(S5_PAIRS):
        up = jnp.concatenate([ug_sc[2 * gp], ug_sc[2 * gp + 1]], axis=1)
        rows = pl.ds(gp, r, stride=S5_PAIRS)
        hin_re[rows, :] = jnp.dot(up, wexp_ref[0, gp], preferred_element_type=F32)
        hin_im[rows, :] = jnp.dot(up, wexp_ref[1, gp], preferred_element_type=F32)

    lr = laml_ref[0]
    li = laml_ref[1]

    def step(kk, carry):
        k = kk + d * (nblk - 1 - 2 * kk)
        new = []
        for b in range(nseq):
            rows = pl.ds(pl.multiple_of((b * nblk + k) * S5_PAIRS, S5_PAIRS), S5_PAIRS)
            hr, hi = carry[2 * b], carry[2 * b + 1]
            hp_re[rows, :] = hr
            hp_im[rows, :] = hi
            new += [lr * hr - li * hi + hin_re[rows, :], lr * hi + li * hr + hin_im[rows, :]]
        return tuple(new)

    init = []
    for b in range(nseq):
        rows = slice(b * S5_PAIRS, (b + 1) * S5_PAIRS)
        init += ([jnp.zeros((S5_PAIRS, 128), F32)] * 2 if zero_init else [h0_ref[0, rows, :], h0_ref[1, rows, :]])
    fin = lax.fori_loop(0, nblk, step, tuple(init))
    for b in range(nseq):
        rows = slice(b * S5_PAIRS, (b + 1) * S5_PAIRS)
        hf_ref[0, rows, :] = fin[2 * b]
        hf_ref[1, rows, :] = fin[2 * b + 1]

    for gp in range(S5_PAIRS):
        rows = pl.ds(gp, r, stride=S5_PAIRS)
        ys = (_nt(hp_re[rows, :].astype(BF16), vexp_ref[0, gp])
              + _nt(hp_im[rows, :].astype(BF16), vexp_ref[1, gp]))
        for j in range(2):
            g = 2 * gp + j
            yg = _nt(ug_sc[g], tt_ref[g]) + ys[:, j * lc:(j + 1) * lc]

            @pl.when(d == 0)
            def _():
                yg_sc[g] = yg

            @pl.when(d == 1)
            def _():
                yg_sc[g] += yg

    @pl.when(d == 1)
    def _():
        for t, yt in enumerate(_lane_block_transpose([yg_sc[g] for g in range(S5_GROUPS)])):
            rows = pl.ds(t, r, stride=S5_L)
            y_ref[0, rows, :] = yt[:, 0:128]
            y_ref[1, rows, :] = yt[:, 128:2 * 128]


def _s5(proj, consts, h0, *, layer, nseq, n_seq):
    laml, wexp, vexp, tt = consts
    t = proj.shape[0]
    nblk = n_seq // S5_L
    r = nblk * nseq
    lc = S5_L * S5_CH
    bs = nseq * S5_PAIRS
    zero_init = h0 is None
    h_spec = pl.BlockSpec((None, 2, bs, 128), lambda d: (d, 0, 0, 0))
    return pl.pallas_call(
        functools.partial(_s5_kernel, nseq=nseq, nblk=nblk, zero_init=zero_init),
        grid=(2,),
        in_specs=[pl.BlockSpec((t, 128), lambda d: (0, P_SU // 128)),
                  pl.BlockSpec((t, 128), lambda d: (0, P_SU // 128 + 1)),
                  pl.BlockSpec((None, S5_GROUPS, lc, lc), lambda d: (2 * layer + d, 0, 0, 0)),
                  pl.BlockSpec((None, 2, S5_PAIRS, 2 * lc, 128), lambda d: (2 * layer + d, 0, 0, 0, 0)),
                  pl.BlockSpec((None, 2, S5_PAIRS, 2 * lc, 128), lambda d: (2 * layer + d, 0, 0, 0, 0)),
                  pl.BlockSpec((None, 2, S5_PAIRS, 128), lambda d: (2 * layer + d, 0, 0, 0))]
                 + ([] if zero_init else [h_spec]),
        out_specs=[pl.BlockSpec((2, t, 128), lambda d: (0, 0, 0)), h_spec],
        out_shape=[jax.ShapeDtypeStruct((2, t, 128), F32),
                   jax.ShapeDtypeStruct((2, 2, bs, 128), F32)],
        scratch_shapes=[pltpu.VMEM((S5_GROUPS, r, lc), BF16), pltpu.VMEM((S5_GROUPS, r, lc), F32)]
                       + [pltpu.VMEM((r * S5_PAIRS, 128), F32) for _ in range(4)],
        compiler_params=_cparams(("arbitrary",)),
        name="s5",
    )(proj, proj, tt, wexp, vexp, laml, *([] if zero_init else [h0]))


def _out_kernel(x_ref, mod_ref, ogf_ref, ogb_ref, odf_ref, odb_ref, ysf_ref, ysb_ref, gog_ref, su_ref, dog_ref,
                ind_ref, gw_ref, dw_ref, sd_ref, wglu_ref, bglu_ref, wout_ref, o_ref):
    og = ogf_ref[...] + ogb_ref[...]
    hi, mid, _ = _split3(og * og)
    ms = jnp.dot(hi, ind_ref[...], preferred_element_type=F32) + jnp.dot(mid, ind_ref[...], preferred_element_type=F32)
    og = og * lax.rsqrt(ms + EPS) * gw_ref[...] * _silu(gog_ref[...])
    od = odf_ref[...] + odb_ref[...]
    dog = dog_ref[...]
    parts = []
    for h in range(GDN_HEADS):
        z = od[:, h * GDN_DV:(h + 1) * GDN_DV]
        z = z * lax.rsqrt(jnp.mean(z * z, axis=-1, keepdims=True) + EPS) * dw_ref[...]
        parts.append(z * _silu(dog[:, h * GDN_DV:(h + 1) * GDN_DV]))
    y = su_ref[...] * sd_ref[...] + jnp.concatenate([ysf_ref[...], ysb_ref[...]], axis=1)
    g = 0.5 * y * (1.0 + jnp.tanh(math.sqrt(2.0 / math.pi) * (y + 0.044715 * (y * y * y))))
    os_ = g * jax.nn.sigmoid(_bdot(g, wglu_ref[...]) + bglu_ref[...])
    out = _bdot(og, wout_ref[0:GLA_W, :]) + _bdot(os_, wout_ref[GLA_W + GDN_W:MIX_W, :])
    for h in range(GDN_HEADS):
        lo = GLA_W + h * GDN_DV
        out = out + _bdot(parts[h], wout_ref[lo:lo + GDN_DV, :])
    o_ref[...] = x_ref[...] + mod_ref[5:6, :] * out


def _mix_out(x, mods, proj, og, od, ys, gla_nw, gdn_nw, s5_d, w_glu, b_glu, w_out, *, layer, n_seq):
    t, d = x.shape
    nb = mods.shape[0]
    tm = _pick_tile(n_seq if nb > 1 else t, 512)
    midx = _mod_index(nb, tm, n_seq)
    ys2 = ys

    def dirspec(w, dd):
        return pl.BlockSpec((None, tm, w), lambda i: (dd, i, 0))

    def vec(w):
        return pl.BlockSpec((None, 1, w), lambda i: (layer, 0, 0))

    return pl.pallas_call(
        _out_kernel,
        grid=(t // tm,),
        in_specs=[pl.BlockSpec((tm, d), lambda i: (i, 0)),
                  pl.BlockSpec((None, N_MOD, d), lambda i: (midx(i), 0, 0)),
                  dirspec(GLA_W, 0), dirspec(GLA_W, 1),
                  pl.BlockSpec((tm, GDN_W), lambda i: (i, 0)), pl.BlockSpec((tm, GDN_W), lambda i: (i, 0)),
                  dirspec(128, 0), dirspec(128, 1),
                  pl.BlockSpec((tm, GLA_W), lambda i: (i, P_GOG // GLA_W)),
                  pl.BlockSpec((tm, S5_W), lambda i: (i, P_SU // S5_W)),
                  pl.BlockSpec((tm, GDN_W), lambda i: (i, P_DOG // GDN_W)),
                  pl.BlockSpec((GLA_W, GLA_W), lambda i: (0, 0)),
                  vec(GLA_W), vec(GDN_DV), vec(S5_W),
                  pl.BlockSpec((None, S5_W, S5_W), lambda i: (layer, 0, 0)),
                  vec(S5_W),
                  pl.BlockSpec((None, MIX_W, d), lambda i: (layer, 0, 0))],
        out_specs=pl.BlockSpec((tm, d), lambda i: (i, 0)),
        out_shape=jax.ShapeDtypeStruct((t, d), F32),
        compiler_params=_cparams(("parallel",)),
        name="mix_out",
    )(x, mods, og, og, od[0], od[1], ys2, ys2, proj, proj, proj,
      jnp.asarray(_C["ind64"], BF16),
      jnp.tile(gla_nw, (1, GLA_HEADS)).reshape(DEPTH, 1, GLA_W), gdn_nw.reshape(DEPTH, 1, GDN_DV),
      s5_d.reshape(DEPTH, 1, S5_W), w_glu, b_glu.reshape(DEPTH, 1, S5_W), w_out)


def _gla_params(gk_up, gk_bias, layer):
    up = jnp.zeros((2, 128, GLA_KW), F32).at[:, :GLA_RANK, :].set(gk_up[layer])
    return up, gk_bias[layer].reshape(2, 1, GLA_KW)


def _gdn_params(a_log, dt_bias, layer):
    arow = jnp.zeros((2, 1, 128), F32).at[:, 0, SM_A:SM_A + GDN_HEADS].set(a_log[layer])
    dtb = jnp.zeros((2, 1, 128), F32).at[:, 0, SM_A:SM_A + GDN_HEADS].set(dt_bias[layer])
    return arow, dtb


def _gla_state_in(s):
    eye = jnp.eye(GLA_HEADS, dtype=F32)
    st = jnp.einsum('bzhde,hg->zbhegd', s.astype(F32), eye)
    return st.reshape(2, s.shape[0], GLA_W, GLA_KW)


def _gla_state_out(st):
    b = st.shape[1]
    s = st.reshape(2, b, GLA_HEADS, GLA_DV, GLA_HEADS, GLA_DK)
    s = jnp.stack([s[:, :, h, :, h, :] for h in range(GLA_HEADS)], axis=2)
    return s.transpose(1, 0, 2, 4, 3)


def _trunk(x3, mods, rows, states, p, s5c, w_in_p):
    nseq, n_seq, d = x3.shape
    x = x3.reshape(nseq * n_seq, d)
    fin = []
    for l in range(DEPTH):
        m = mods[l]
        x = _ffn(x, m, p['norm_w'], p['ffn_w_gate'], p['ffn_w_up'], p['ffn_w_down'], p['final_norm_w'],
                 layer=l, which=0, n_seq=n_seq, final=False)
        proj = _proj(x, m, p['norm_w'], w_in_p, p['gdn_conv_w'], layer=l, n_seq=n_seq, rowlen=n_seq // rows)
        if states is None:
            st0 = s0 = h0 = None
        else:
            st_gla, st_gdn, st_re, st_im = states
            st0 = _gla_state_in(st_gla[:, l])
            s0 = st_gdn[:, l].transpose(1, 0, 2, 3, 4)
            h0 = jnp.stack([st_re[:, l].reshape(nseq, 2, S5_N), st_im[:, l].reshape(nseq, 2, S5_N)], axis=0)
            h0 = h0.transpose(2, 0, 1, 3).reshape(2, 2, nseq * S5_PAIRS, 128)
        up_p, bias_p = _gla_params(p['gla_gk_up'], p['gla_gk_bias'], l)
        og, gla_f = _gla(proj, up_p, bias_p, st0, nseq=nseq, n_seq=n_seq)
        arow, dtb = _gdn_params(p['gdn_a_log'], p['gdn_dt_bias'], l)
        od_f, od_b, gdn_f = _gdn(proj, arow, dtb, s0, nseq=nseq, n_seq=n_seq)
        od = (od_f, od_b)
        ys, s5_f = _s5(proj, s5c, h0, layer=l, nseq=nseq, n_seq=n_seq)
        x = _mix_out(x, m, proj, og, od, ys, p['gla_norm_w'], p['gdn_norm_w'], p['s5_d'], p['s5_w_glu'],
                     p['s5_b_glu'], p['w_out'], layer=l, n_seq=n_seq)
        x = _ffn(x, m, p['norm_w'], p['ffn_w_gate'], p['ffn_w_up'], p['ffn_w_down'], p['final_norm_w'],
                 layer=l, which=1, n_seq=n_seq, final=(l == DEPTH - 1))
        fin.append((gla_f, gdn_f, s5_f))
    return x.reshape(nseq, n_seq, d), fin


def kernel(x_prompt, x_sample, c, state_gla, state_gdn, state_s5_re, state_s5_im, c_ctx, w_ada, b_ada, norm_w, ffn_w_gate, ffn_w_up, ffn_w_down, w_in, gla_gk_up, gla_gk_bias, gla_norm_w, gdn_conv_w, gdn_a_log, gdn_dt_bias, gdn_norm_w, s5_lam_re, s5_lam_im, s5_log_step, s5_b_re, s5_b_im, s5_c_re, s5_c_im, s5_d, s5_w_glu, s5_b_glu, w_out, final_norm_w):
    p = dict(norm_w=norm_w, ffn_w_gate=ffn_w_gate, ffn_w_up=ffn_w_up, ffn_w_down=ffn_w_down,
             gla_gk_up=gla_gk_up, gla_gk_bias=gla_gk_bias, gla_norm_w=gla_norm_w, gdn_conv_w=gdn_conv_w,
             gdn_a_log=gdn_a_log, gdn_dt_bias=gdn_dt_bias, gdn_norm_w=gdn_norm_w, s5_c_re=s5_c_re,
             s5_c_im=s5_c_im, s5_d=s5_d, s5_w_glu=s5_w_glu, s5_b_glu=s5_b_glu, w_out=w_out,
             final_norm_w=final_norm_w)
    b_ctx = x_prompt.shape[0]
    b_dec = x_sample.shape[0]
    cond8 = jnp.zeros((8, D_MODEL), F32).at[0].set(c_ctx).at[1:1 + b_dec].set(c)
    mods = _ada(cond8, w_ada, b_ada)
    mods_ctx = mods[:, 0:1].reshape(DEPTH, 1, N_MOD, D_MODEL)
    mods_dec = mods[:, 1:1 + b_dec].reshape(DEPTH, b_dec, N_MOD, D_MODEL)
    s5c = _s5_consts(s5_lam_re, s5_lam_im, s5_log_step, s5_b_re, s5_b_im, s5_c_re, s5_c_im)
    w_in_p = _permute_w_in(w_in)

    y_prompt, fin = _trunk(x_prompt, mods_ctx, 1, None, p, s5c, w_in_p)
    new_gla = jnp.stack([_gla_state_out(f[0]) for f in fin], axis=1)
    new_gdn = jnp.stack([f[1].transpose(1, 0, 2, 3, 4) for f in fin], axis=1)
    s5f = jnp.stack([f[2].reshape(2, 2, b_ctx, S5_GROUPS, S5_P).transpose(1, 2, 0, 3, 4) for f in fin], axis=2)
    new_re, new_im = s5f[0], s5f[1]

    rows = x_sample.shape[1] // 64
    y_sample, _ = _trunk(x_sample, mods_dec, rows, (state_gla, state_gdn, state_s5_re, state_s5_im),
                         p, s5c, w_in_p)
    return (y_prompt, y_sample, new_gla.astype(state_gla.dtype), new_gdn.astype(state_gdn.dtype),
            new_re.astype(state_s5_re.dtype), new_im.astype(state_s5_im.dtype))
```

```python
import functools
import math

import numpy as np
import jax
import jax.numpy as jnp
from jax import lax
from jax.experimental import pallas as pl
from jax.experimental.pallas import tpu as pltpu

F32 = jnp.float32
BF16 = jnp.bfloat16

D_MODEL = 1024
DEPTH = 2
FFN_DIM = 2816
N_MOD = 9
EPS = 1e-6
CHUNK = 64
GLA_HEADS, GLA_DK, GLA_DV, GLA_RANK = 4, 32, 64, 16
GLA_GATE_NORM = 16.0
GDN_HEADS, GDN_DK, GDN_DV, GDN_CONV = 4, 128, 128, 5
S5_GROUPS, S5_CH, S5_P = 16, 16, 64
GLA_W = GLA_HEADS * GLA_DV
GLA_KW = GLA_HEADS * GLA_DK
GDN_W = GDN_HEADS * GDN_DV
S5_W = S5_GROUPS * S5_CH
S5_N = S5_GROUPS * S5_P
S5_L = 16
S5_PAIRS = S5_GROUPS // 2
MIX_W = GLA_W + GDN_W + S5_W

_IN_SIZES = (GLA_KW, GLA_KW, GLA_W, GLA_RANK, GLA_RANK, GLA_W,
             GDN_W, GDN_W, GDN_W, GDN_HEADS, GDN_HEADS, GDN_HEADS, GDN_HEADS, GDN_W, S5_W)
_IN_OFF = tuple(int(v) for v in np.cumsum((0,) + _IN_SIZES))
(_GQ, _GK, _GV, _GLRF, _GLRB, _GOG, _DQ, _DK, _DV, _DAF, _DAB, _DBF, _DBB, _DOG, _SU) = range(15)

P_DQKV = 0
P_SMALL = 1536
P_GQK = 1792
P_GV = 2048
P_GOG = 2304
P_DOG = 2560
P_SU = 3072
PROJ_W = 3328
PROJ_TN = PROJ_W // 2
SM_A = GLA_RANK
SM_B = GLA_RANK + GDN_HEADS
SUB = 256
VMEM_LIMIT = 56 * 1024 * 1024


def _cparams(sem):
    return pltpu.CompilerParams(dimension_semantics=sem, vmem_limit_bytes=VMEM_LIMIT)


def _bdot(a, b):
    return jnp.dot(a.astype(BF16), b.astype(BF16), preferred_element_type=F32)


def _bdot_nt(a, b):
    return lax.dot_general(a.astype(BF16), b.astype(BF16), (((1,), (1,)), ((), ())),
                           preferred_element_type=F32)


def _split3(x):
    hi = x.astype(BF16)
    r1 = x - hi.astype(F32)
    mid = r1.astype(BF16)
    lo = (r1 - mid.astype(F32)).astype(BF16)
    return hi, mid, lo


def _sel_dot(m01, x):
    hi, mid, lo = _split3(x)
    return (jnp.dot(m01, hi, preferred_element_type=F32) + jnp.dot(m01, mid, preferred_element_type=F32)
            + jnp.dot(m01, lo, preferred_element_type=F32))


def _silu(x):
    return x * jax.nn.sigmoid(x)


def _softplus(x):
    return jnp.maximum(x, 0.0) + jnp.log1p(jnp.exp(-jnp.abs(x)))


def _log_sigmoid(x):
    return -_softplus(-x)


def _norm_mod(x, nw, scale, shift):
    y = x * lax.rsqrt(jnp.mean(x * x, axis=-1, keepdims=True) + EPS)
    return y * nw * (1.0 + scale) + shift


def _pick_tile(n, cap):
    t = cap
    while n % t:
        t //= 2
    return t


def _ada_kernel(c_ref, w_ref, b_ref, o_ref):
    o_ref[...] = _bdot(_silu(c_ref[...]), w_ref[...]) + b_ref[...]


def _ada(cond8, w_ada, b_ada):
    depth, d, nout = w_ada.shape
    tn = 1024
    return pl.pallas_call(
        _ada_kernel,
        grid=(depth, nout // tn),
        in_specs=[pl.BlockSpec((8, d), lambda l, j: (0, 0)),
                  pl.BlockSpec((None, d, tn), lambda l, j: (l, 0, j)),
                  pl.BlockSpec((None, 1, tn), lambda l, j: (l, 0, j))],
        out_specs=pl.BlockSpec((None, 8, tn), lambda l, j: (l, 0, j)),
        out_shape=jax.ShapeDtypeStruct((depth, 8, nout), F32),
        compiler_params=_cparams(("parallel", "parallel")),
        name="ada",
    )(cond8, w_ada, b_ada.reshape(depth, 1, nout))


def _ffn_kernel(x_ref, mod_ref, nw_ref, wg_ref, wu_ref, wd_ref, fw_ref, o_ref, h_ref, *, mod_base, final):
    j = pl.program_id(1)

    @pl.when(j == 0)
    def _():
        m = mod_ref[...]
        h = _norm_mod(x_ref[...], nw_ref[...], m[mod_base + 1:mod_base + 2], m[mod_base:mod_base + 1])
        h_ref[...] = h.astype(BF16)
        o_ref[...] = jnp.zeros_like(o_ref)

    h = h_ref[...]
    g = jnp.dot(h, wg_ref[...].astype(BF16), preferred_element_type=F32)
    u = jnp.dot(h, wu_ref[...].astype(BF16), preferred_element_type=F32)
    o_ref[...] += _bdot(_silu(g) * u, wd_ref[...])

    @pl.when(j == pl.num_programs(1) - 1)
    def _():
        m = mod_ref[...]
        y = x_ref[...] + 0.5 * m[mod_base + 2:mod_base + 3] * o_ref[...]
        if final:
            y = y * lax.rsqrt(jnp.mean(y * y, axis=-1, keepdims=True) + EPS) * fw_ref[...]
        o_ref[...] = y


def _mod_index(nb, tm, n_seq):
    if nb == 1:
        return lambda i: 0
    return lambda i: (i * tm) // n_seq


def _ffn(x, mods, norm_w, w_gate, w_up, w_down, final_w, *, layer, which, n_seq, final):
    t, d = x.shape
    f = w_gate.shape[-1]
    nb = mods.shape[0]
    tm = _pick_tile(n_seq if nb > 1 else t, 2048)
    tf = 256
    midx = _mod_index(nb, tm, n_seq)
    mod_base = 0 if which == 0 else 6
    return pl.pallas_call(
        functools.partial(_ffn_kernel, mod_base=mod_base, final=final),
        grid=(t // tm, f // tf),
        in_specs=[pl.BlockSpec((tm, d), lambda i, j: (i, 0), pipeline_mode=pl.Buffered(1)),
                  pl.BlockSpec((None, N_MOD, d), lambda i, j: (midx(i), 0, 0)),
                  pl.BlockSpec((None, None, 1, d), lambda i, j: (layer, 2 * which, 0, 0)),
                  pl.BlockSpec((None, None, d, tf), lambda i, j: (layer, which, 0, j)),
                  pl.BlockSpec((None, None, d, tf), lambda i, j: (layer, which, 0, j)),
                  pl.BlockSpec((None, None, tf, d), lambda i, j: (layer, which, j, 0)),
                  pl.BlockSpec((1, d), lambda i, j: (0, 0))],
        out_specs=pl.BlockSpec((tm, d), lambda i, j: (i, 0)),
        out_shape=jax.ShapeDtypeStruct((t, d), F32),
        scratch_shapes=[pltpu.VMEM((tm, d), BF16)],
        compiler_params=_cparams(("parallel", "arbitrary")),
        name="ffn",
    )(x, mods, norm_w.reshape(DEPTH, 3, 1, d), w_gate, w_up, w_down, final_w.reshape(1, d))


def _proj_kernel(x_ref, mod_ref, nw_ref, w_ref, cw_ref, o_ref, h_ref, wb_ref, *, rowlen):
    j = pl.program_id(0)
    i = pl.program_id(1)

    @pl.when(i == 0)
    def _():
        wb_ref[...] = w_ref[...].astype(BF16)

    @pl.when(j == 0)
    def _():
        m = mod_ref[...]
        h_ref[i] = _norm_mod(x_ref[...], nw_ref[...], m[4:5], m[3:4]).astype(BF16)
        o_ref[...] = jnp.dot(h_ref[i], wb_ref[...], preferred_element_type=F32)
        tm = o_ref.shape[0]
        pos = lax.broadcasted_iota(jnp.int32, (tm, 1), 0) % rowlen
        valid = [jnp.logical_and(pos + (t - GDN_CONV // 2) >= 0, pos + (t - GDN_CONV // 2) < rowlen)
                 for t in range(GDN_CONV)]
        for g in range(3 * GDN_HEADS):
            cols = slice(P_DQKV + g * GDN_DK, P_DQKV + (g + 1) * GDN_DK)
            xg = o_ref[:, cols]
            acc = xg * cw_ref[GDN_CONV // 2:GDN_CONV // 2 + 1, cols]
            for t in range(GDN_CONV):
                s = t - GDN_CONV // 2
                if s != 0:
                    xs = pltpu.roll(xg, (-s) % tm, axis=0)
                    acc = acc + jnp.where(valid[t], xs, 0.0) * cw_ref[t:t + 1, cols]
            y = _silu(acc)
            if g < 2 * GDN_HEADS:
                y = y * lax.rsqrt(jnp.sum(y * y, axis=-1, keepdims=True) + EPS)
                if g < GDN_HEADS:
                    y = y * (GDN_DK ** -0.5)
            o_ref[:, cols] = y

    @pl.when(j == 1)
    def _():
        o_ref[...] = jnp.dot(h_ref[i], wb_ref[...], preferred_element_type=F32)


def _proj(x, mods, norm_w, w_in_p, conv_w, *, layer, n_seq, rowlen):
    t, d = x.shape
    nb = mods.shape[0]
    tm = _pick_tile(n_seq if nb > 1 else t, 512)
    midx = _mod_index(nb, tm, n_seq)
    return pl.pallas_call(
        functools.partial(_proj_kernel, rowlen=rowlen),
        grid=(PROJ_W // PROJ_TN, t // tm),
        in_specs=[pl.BlockSpec((tm, d), lambda j, i: (i * (1 - j), 0)),
                  pl.BlockSpec((None, N_MOD, d), lambda j, i: (midx(i), 0, 0)),
                  pl.BlockSpec((None, None, 1, d), lambda j, i: (layer, 1, 0, 0)),
                  pl.BlockSpec((None, d, PROJ_TN), lambda j, i: (layer, 0, j)),
                  pl.BlockSpec((None, GDN_CONV, 3 * GDN_W), lambda j, i: (layer, 0, 0))],
        out_specs=pl.BlockSpec((tm, PROJ_TN), lambda j, i: (i, j)),
        out_shape=jax.ShapeDtypeStruct((t, PROJ_W), F32),
        scratch_shapes=[pltpu.VMEM((t // tm, tm, d), BF16), pltpu.VMEM((d, PROJ_TN), BF16)],
        compiler_params=_cparams(("arbitrary", "arbitrary")),
        name="proj",
    )(x, mods, norm_w.reshape(DEPTH, 3, 1, d), w_in_p, conv_w)


def _permute_w_in(w_in):
    def seg(i):
        return w_in[:, :, _IN_OFF[i]:_IN_OFF[i + 1]]

    def pad(n):
        return jnp.zeros(w_in.shape[:2] + (n,), w_in.dtype)

    sm_pad = 128 - GLA_RANK - 2 * GDN_HEADS
    return jnp.concatenate(
        [seg(_DQ), seg(_DK), seg(_DV),
         seg(_GLRF), seg(_DAF), seg(_DBF), pad(sm_pad),
         seg(_GLRB), seg(_DAB), seg(_DBB), pad(sm_pad),
         seg(_GQ), seg(_GK), seg(_GV), seg(_GOG), seg(_DOG), seg(_SU)], axis=-1)


def _np_consts():
    t = np.arange(CHUNK)
    lower = (t[:, None] >= t[None, :])
    incl = np.stack([lower, lower.T])
    strict = np.stack([t[:, None] > t[None, :], t[:, None] < t[None, :]])
    eye4 = np.eye(4, dtype=bool)
    blk = np.kron(eye4, np.ones((CHUNK, CHUNK), bool))
    incl_bd = np.stack([np.kron(eye4, incl[d]) for d in range(2)])
    strict_bd = np.stack([np.kron(eye4, strict[d]) for d in range(2)])
    levels = []
    for d in range(2):
        per = []
        for lv in range(6):
            s = 1 << lv
            same = (t[:, None] // (2 * s)) == (t[None, :] // (2 * s))
            hi = (t % (2 * s)) >= s
            m = same & hi[:, None] & (~hi)[None, :]
            if d == 1:
                m = m.T
            per.append(np.kron(eye4, m))
        levels.append(np.stack(per))
    levels = np.stack(levels)
    assert (levels.sum(1) == strict_bd).all() and blk.any()
    gla_mask = np.stack([np.tile(incl[d], (1, 4)) for d in range(2)])
    hm_k = np.kron(eye4, np.ones((1, GLA_DK)))
    hm_v = np.kron(eye4, np.ones((1, GLA_DV)))
    st_mask = np.kron(eye4, np.ones((GLA_DV, GLA_DK)))
    ind64 = np.kron(eye4, np.full((GLA_DV, GLA_DV), 1.0 / GLA_DV))
    gla_sel, gla_later, gla_earlier, gla_pair = [], [], [], []
    for d in range(2):
        pos = t if d == 0 else CHUNK - 1 - t
        cum = (pos[None, :] <= pos[:, None]).astype(np.float64)
        sel, later, earlier, pair = [cum], [], [], []
        for lv in range(6):
            s = 1 << lv
            ref_pos = (pos // (2 * s)) * (2 * s) + s - 1
            pick = (pos[None, :] == ref_pos[:, None]).astype(np.float64)
            sel.append(pick @ cum)
            is_later = (pos % (2 * s)) >= s
            same = (pos[:, None] // (2 * s)) == (pos[None, :] // (2 * s))
            later.append(np.repeat(is_later[:, None], GLA_KW, axis=1))
            earlier.append(np.repeat(~is_later[:, None], GLA_KW, axis=1))
            pair.append(np.tile(same & is_later[:, None] & (~is_later)[None, :], (1, 4)))
        gla_sel.append(np.concatenate(sel, axis=0))
        gla_later.append(np.stack(later))
        gla_earlier.append(np.stack(earlier))
        gla_pair.append(np.stack(pair))
        assert (np.stack(pair).sum(0) == np.tile(pos[:, None] > pos[None, :], (1, 4))).all()
    head_sum = np.kron(eye4, np.ones((GLA_DK, GLA_DV)))
    return dict(incl=incl, incl_bd=incl_bd, strict_bd=strict_bd, levels=levels, gla_mask=gla_mask,
                hm_k=hm_k, hm_v=hm_v, st_mask=st_mask, ind64=ind64, eye256=np.eye(256),
                gla_sel=np.stack(gla_sel), gla_later=np.stack(gla_later), gla_earlier=np.stack(gla_earlier),
                gla_pair=np.stack(gla_pair), head_sum=head_sum)


_C = _np_consts()


def _row_block_index(nblk):
    return lambda d, b, i: b * nblk + i + d * (nblk - 1 - 2 * i)


def _gla_kernel(qkf_ref, qkb_ref, vf_ref, vb_ref, smf_ref, smb_ref, up_ref, bias_ref, sel_ref, later_ref,
                earlier_ref, pair_ref, hmk_ref, hmv_ref, stm_ref, hsum_ref, *rest, nchunk, zero_init):
    if zero_init:
        of_ref, ob_ref, stf_ref, st_ref, qd_sc, m_sc, dec_sc = rest
    else:
        st0_ref, of_ref, ob_ref, stf_ref, st_ref, qd_sc, m_sc, dec_sc = rest
    ins = ((qkf_ref, vf_ref, smf_ref, of_ref), (qkb_ref, vb_ref, smb_ref, ob_ref))

    @pl.when(pl.program_id(1) == 0)
    def _():
        st_ref[...] = jnp.zeros_like(st_ref) if zero_init else st0_ref[...]

    hmk = hmk_ref[...]
    hmv = hmv_ref[...]
    stm = stm_ref[...]

    units = [(dd, c) for c in range(nchunk) for dd in range(2)]
    st1 = []
    for dd, c in units:
        qk_ref, v_ref, sm_ref, _ = ins[dd]
        rows = pl.ds(c * CHUNK, CHUNK)
        qk = qk_ref[rows, :]
        q = qk[:, 0:GLA_KW] * (GLA_DK ** -0.5)
        k = qk[:, GLA_KW:2 * GLA_KW]
        x = _bdot(sm_ref[rows, :], up_ref[dd]) + bias_ref[dd]
        lg = _log_sigmoid(x) * (1.0 / GLA_GATE_NORM)
        ball = _sel_dot(sel_ref[dd], lg)
        st1.append((q, k, v_ref[rows, :], ball, jnp.sum(lg, axis=0, keepdims=True)))
    prods = []
    for lv in range(6):
        for (dd, c), (q, k, _, ball, _) in zip(units, st1):
            b = ball[0:CHUNK]
            ref = ball[(lv + 1) * CHUNK:(lv + 2) * CHUNK]
            ql = q * jnp.where(later_ref[dd, lv] > 0.5, jnp.exp(jnp.minimum(b - ref, 0.0)), 0.0)
            kl = k * jnp.where(earlier_ref[dd, lv] > 0.5, jnp.exp(jnp.minimum(ref - b, 0.0)), 0.0)
            kexp = jnp.concatenate([kl * hmk[h:h + 1] for h in range(GLA_HEADS)], axis=0)
            prods.append(pair_ref[dd, lv] * _bdot_nt(ql, kexp))
    for i, ((dd, c), (q, k, v, ball, btot)) in enumerate(zip(units, st1)):
        o_ref = ins[dd][3]
        n = dd * nchunk + c
        b = ball[0:CHUNK]
        att = prods[i]
        for lv in range(1, 6):
            att = att + prods[lv * len(units) + i]
        vexp = jnp.concatenate([v * hmv[h:h + 1] for h in range(GLA_HEADS)], axis=0)
        o_ref[pl.ds(c * CHUNK, CHUNK), :] = _bdot(att, vexp) + _bdot(q * k, hsum_ref[...]) * v
        qd_sc[n] = (q * jnp.exp(b)).astype(BF16)
        kd = k * jnp.exp(btot - b)
        m_sc[n] = _bdot(v.T, kd) * stm
        dec_sc[n] = jnp.broadcast_to(jnp.exp(btot), (8, GLA_KW))

    for dd in range(2):
        o_ref = ins[dd][3]
        st = st_ref[dd]
        for ci in range(nchunk):
            c = ci if dd == 0 else nchunk - 1 - ci
            n = dd * nchunk + c
            rows = pl.ds(c * CHUNK, CHUNK)
            o_ref[rows, :] += _nt(qd_sc[n], st.astype(BF16))
            st = st * dec_sc[n][0:1] + m_sc[n]
        st_ref[dd] = st
    stf_ref[...] = st_ref[...]


def _gla(proj, up_p, bias_p, st0, *, nseq, n_seq):
    t = proj.shape[0]
    nblk = n_seq // SUB
    nck = SUB // CHUNK
    rbi = _row_block_index(nblk)
    c = _C
    consts = [jnp.asarray(c["gla_sel"], BF16), jnp.asarray(c["gla_later"], F32), jnp.asarray(c["gla_earlier"], F32),
              jnp.asarray(c["gla_pair"], F32), jnp.asarray(c["hm_k"], F32), jnp.asarray(c["hm_v"], F32),
              jnp.asarray(c["st_mask"], F32), jnp.asarray(c["head_sum"], BF16)]
    st_spec = pl.BlockSpec((2, None, GLA_W, GLA_KW), lambda b, i: (0, b, 0, 0))
    full = lambda a: pl.BlockSpec(a.shape, lambda b, i: (0,) * a.ndim)
    zero_init = st0 is None
    nu = 2 * nck
    rows = (lambda b, i: rbi(0, b, i), lambda b, i: rbi(1, b, i))

    def blk(width, col, dd):
        return pl.BlockSpec((SUB, width), lambda b, i: (rows[dd](b, i), col // width))

    return pl.pallas_call(
        functools.partial(_gla_kernel, nchunk=nck, zero_init=zero_init),
        grid=(nseq, nblk),
        in_specs=[blk(2 * GLA_KW, P_GQK, 0), blk(2 * GLA_KW, P_GQK, 1), blk(GLA_W, P_GV, 0), blk(GLA_W, P_GV, 1),
                  blk(128, P_SMALL, 0), blk(128, P_SMALL + 128, 1), full(up_p), full(bias_p)]
                 + [full(a) for a in consts] + ([] if zero_init else [st_spec]),
        out_specs=[pl.BlockSpec((SUB, GLA_W), lambda b, i: (rows[0](b, i), 0)),
                   pl.BlockSpec((SUB, GLA_W), lambda b, i: (rows[1](b, i), 0)), st_spec],
        out_shape=[jax.ShapeDtypeStruct((t, GLA_W), F32), jax.ShapeDtypeStruct((t, GLA_W), F32),
                   jax.ShapeDtypeStruct((2, nseq, GLA_W, GLA_KW), F32)],
        scratch_shapes=[pltpu.VMEM((2, GLA_W, GLA_KW), F32), pltpu.VMEM((nu, CHUNK, GLA_KW), BF16),
                        pltpu.VMEM((nu, GLA_W, GLA_KW), F32), pltpu.VMEM((nu, 8, GLA_KW), F32)],
        compiler_params=_cparams(("arbitrary", "arbitrary")),
        name="gla",
    )(proj, proj, proj, proj, proj, proj, up_p, bias_p, *consts, *([] if zero_init else [st0]))


def _gdn_kernel(qkvf_ref, qkvb_ref, smf_ref, smb_ref, arow_ref, dtb_ref, tri_ref, incl_ref, strict_ref, lvl_ref,
                eye_ref, *rest, nchunk, zero_init):
    if zero_init:
        of_ref, ob_ref, sf_ref, s_ref, la_ref, be_ref, w_sc, u0_sc, qkd_sc, qe_sc, kdt_sc, egl_sc = rest
    else:
        s0_ref, of_ref, ob_ref, sf_ref, s_ref, la_ref, be_ref, w_sc, u0_sc, qkd_sc, qe_sc, kdt_sc, egl_sc = rest
    qkv_refs = (qkvf_ref, qkvb_ref)
    o_refs = (of_ref, ob_ref)

    @pl.when(pl.program_id(1) == 0)
    def _():
        s_ref[...] = jnp.zeros_like(s_ref) if zero_init else s0_ref[...]

    for dd, sm_ref in enumerate((smf_ref, smb_ref)):
        sm = sm_ref[...]
        la_ref[dd] = -jnp.exp(arow_ref[dd]) * _softplus(sm + dtb_ref[dd])
        be_ref[dd] = jax.nn.sigmoid(sm)

    eye = eye_ref[...]

    def stack(ref, r0, lo, width):
        return jnp.concatenate([ref[pl.ds(r0, CHUNK), lo + h * width:lo + (h + 1) * width]
                                for h in range(GDN_HEADS)], axis=0)

    def colstack(a, lo):
        return jnp.concatenate([a[:, lo + h:lo + h + 1] for h in range(GDN_HEADS)], axis=0)

    units = [(dd, c) for c in range(nchunk) for dd in range(2)]
    st1 = []
    for dd, c in units:
        r0 = c * CHUNK
        qkv_ref = qkv_refs[dd]
        qs = stack(qkv_ref, r0, 0, GDN_DK)
        ks = stack(qkv_ref, r0, GDN_W, GDN_DK)
        la = la_ref[dd, pl.ds(r0, CHUNK), :]
        g = _sel_dot(tri_ref[dd], la)
        ksb = ks.astype(BF16)
        st1.append((qs, ks, la, g, _bdot_nt(ksb, ksb), _bdot_nt(qs, ksb)))
    loc = []
    for (dd, c), (qs, ks, la, g, kk, qk) in zip(units, st1):
        r0 = c * CHUNK
        n = dd * nchunk + c
        vs = stack(qkv_refs[dd], r0, 2 * GDN_W, GDN_DV)
        gtot = jnp.sum(la, axis=0, keepdims=True)
        gcol = colstack(g, SM_A)
        kfcol = colstack(jnp.exp(gtot - g), SM_A)
        egcol = colstack(jnp.exp(g), SM_A)
        bcol = colstack(be_ref[dd, pl.ds(r0, CHUNK), :], SM_B)
        grow = jnp.sum(eye * gcol, axis=0, keepdims=True)
        dec = incl_ref[dd] * jnp.exp(jnp.minimum(gcol - grow, 0.0))
        lmat = strict_ref[dd] * dec * kk * bcol
        rhs = jnp.concatenate([ks * (bcol * egcol), vs * bcol], axis=1).astype(BF16)
        qkd_sc[n] = (qk * dec).astype(BF16)
        qe_sc[n] = (qs * egcol).astype(BF16)
        kd = ks * kfcol
        for h in range(GDN_HEADS):
            kdt_sc[n * GDN_HEADS + h] = kd[h * CHUNK:(h + 1) * CHUNK].T.astype(BF16)
        egl_sc[n] = jnp.broadcast_to(jnp.exp(gtot), (8, 128))
        loc.append((lmat.astype(BF16), rhs))
    tinv = [eye - (lm * lvl_ref[dd, 0]).astype(F32) for (dd, _), (lm, _) in zip(units, loc)]
    for lv in range(1, 6):
        tb = [t_.astype(BF16) for t_ in tinv]
        cs = [jnp.dot(lm * lvl_ref[dd, lv], t_, preferred_element_type=F32)
              for (dd, _), (lm, _), t_ in zip(units, loc, tb)]
        tinv = [t32 - jnp.dot(t_, c_.astype(BF16), preferred_element_type=F32)
                for t32, t_, c_ in zip(tinv, tb, cs)]
    for (dd, c), (_, rhs), t_ in zip(units, loc, tinv):
        sol = jnp.dot(t_.astype(BF16), rhs, preferred_element_type=F32)
        w_sc[dd * nchunk + c] = sol[:, :GDN_DK].astype(BF16)
        u0_sc[dd * nchunk + c] = sol[:, GDN_DK:]

    s_cur = [[s_ref[dd, h] for h in range(GDN_HEADS)] for dd in range(2)]
    for ci in range(nchunk):
        step = []
        for dd in range(2):
            c = ci if dd == 0 else nchunk - 1 - ci
            n = dd * nchunk + c
            sb = [s_.astype(BF16) for s_ in s_cur[dd]]
            ub = []
            for h in range(GDN_HEADS):
                rows = pl.ds(h * CHUNK, CHUNK)
                u = u0_sc[n, rows, :] - jnp.dot(w_sc[n, rows, :], sb[h], preferred_element_type=F32)
                ub.append(u.astype(BF16))
            step.append((c, n, sb, ub))
        for dd, (c, n, sb, ub) in enumerate(step):
            egl = egl_sc[n]
            for h in range(GDN_HEADS):
                s_cur[dd][h] = (egl[0:1, SM_A + h:SM_A + h + 1] * s_cur[dd][h]
                                + jnp.dot(kdt_sc[n * GDN_HEADS + h], ub[h], preferred_element_type=F32))
        for dd, (c, n, sb, ub) in enumerate(step):
            o2 = jnp.dot(qkd_sc[n], jnp.concatenate(ub, axis=0), preferred_element_type=F32)
            for h in range(GDN_HEADS):
                rows = pl.ds(h * CHUNK, CHUNK)
                o_refs[dd][pl.ds(c * CHUNK, CHUNK), h * GDN_DV:(h + 1) * GDN_DV] = (
                    jnp.dot(qe_sc[n, rows, :], sb[h], preferred_element_type=F32) + o2[h * CHUNK:(h + 1) * CHUNK])
    for dd in range(2):
        for h in range(GDN_HEADS):
            s_ref[dd, h] = s_cur[dd][h]
    sf_ref[...] = s_ref[...]


def _gdn(proj, arow, dtb, s0, *, nseq, n_seq):
    t = proj.shape[0]
    nblk = n_seq // SUB
    rbi = _row_block_index(nblk)
    c = _C
    consts = [jnp.asarray(c["incl"], BF16), jnp.asarray(c["incl_bd"], F32), jnp.asarray(c["strict_bd"], F32),
              jnp.asarray(c["levels"], BF16), jnp.asarray(c["eye256"], F32)]
    hs = (GDN_HEADS, GDN_DK, GDN_DV)
    nck = SUB // CHUNK
    nu = 2 * nck
    zero_init = s0 is None
    s_spec = pl.BlockSpec((2, None) + hs, lambda b, i: (0, b, 0, 0, 0))
    full = lambda shape: pl.BlockSpec(shape, lambda b, i: (0,) * len(shape))
    rows = (lambda b, i: rbi(0, b, i), lambda b, i: rbi(1, b, i))
    return pl.pallas_call(
        functools.partial(_gdn_kernel, nchunk=nck, zero_init=zero_init),
        grid=(nseq, nblk),
        in_specs=[pl.BlockSpec((SUB, 3 * GDN_W), lambda b, i: (rows[0](b, i), P_DQKV // (3 * GDN_W))),
                  pl.BlockSpec((SUB, 3 * GDN_W), lambda b, i: (rows[1](b, i), P_DQKV // (3 * GDN_W))),
                  pl.BlockSpec((SUB, 128), lambda b, i: (rows[0](b, i), P_SMALL // 128)),
                  pl.BlockSpec((SUB, 128), lambda b, i: (rows[1](b, i), P_SMALL // 128 + 1)),
                  full((2, 1, 128)), full((2, 1, 128)), full((2, CHUNK, CHUNK)),
                  full((2, 256, 256)), full((2, 256, 256)), full((2, 6, 256, 256)), full((256, 256))]
                 + ([] if zero_init else [s_spec]),
        out_specs=[pl.BlockSpec((SUB, GDN_W), lambda b, i: (rows[0](b, i), 0)),
                   pl.BlockSpec((SUB, GDN_W), lambda b, i: (rows[1](b, i), 0)), s_spec],
        out_shape=[jax.ShapeDtypeStruct((t, GDN_W), F32), jax.ShapeDtypeStruct((t, GDN_W), F32),
                   jax.ShapeDtypeStruct((2, nseq) + hs, F32)],
        scratch_shapes=[pltpu.VMEM((2,) + hs, F32),
                        pltpu.VMEM((2, SUB, 128), F32), pltpu.VMEM((2, SUB, 128), F32),
                        pltpu.VMEM((nu, 256, GDN_DK), BF16), pltpu.VMEM((nu, 256, GDN_DV), F32),
                        pltpu.VMEM((nu, 256, 256), BF16), pltpu.VMEM((nu, 256, GDN_DK), BF16),
                        pltpu.VMEM((nu * GDN_HEADS, GDN_DK, CHUNK), BF16), pltpu.VMEM((nu, 8, 128), F32)],
        compiler_params=_cparams(("arbitrary", "arbitrary")),
        name="gdn",
    )(proj, proj, proj, proj, arow, dtb, *consts, *([] if zero_init else [s0]))


def _cmul(ar, ai, br, bi):
    return ar * br - ai * bi, ar * bi + ai * br


def _split2(x):
    hi = x.astype(BF16)
    return hi, (x - hi.astype(F32)).astype(BF16)


def _s5_consts_kernel(lr_ref, li_ref, st_ref, br_ref, bi_ref, cr_ref, ci_ref, gm_ref, bm_ref,
                      laml_ref, wexp_ref, vexp_ref, tt_ref, w_ref, v_ref, a_ref, q_ref):
    d = pl.program_id(0) % 2
    lc = S5_L * S5_CH

    def trows(s):
        return pl.ds(pl.multiple_of((s + d * (S5_L - 1 - 2 * s)) * S5_CH, S5_CH), S5_CH)

    lr = lr_ref[...]
    li = li_ref[...]
    step = jnp.exp(st_ref[...])
    mag = jnp.exp(lr * step)
    lbr = mag * jnp.cos(li * step)
    lbi = mag * jnp.sin(li * step)
    nr = lbr - 1.0
    den = lr * lr + li * li
    fr = (nr * lr + lbi * li) / den
    fi = (lbi * lr - nr * li) / den
    bbr, bbi = _cmul(fr, fi, br_ref[...], bi_ref[...])
    cr = cr_ref[...]
    ci = ci_ref[...]
    pr = jnp.ones_like(lr)
    pi = jnp.zeros_like(lr)
    for s in range(S5_L):
        ar, ai = _cmul(cr, ci, pr, pi)
        a_ref[0, trows(s), :] = ar
        a_ref[1, trows(s), :] = ai
        wr, wi = _cmul(pr, pi, bbr, bbi)
        w_ref[0, trows(S5_L - 1 - s), :] = wr
        w_ref[1, trows(S5_L - 1 - s), :] = wi
        pr, pi = _cmul(pr, pi, lbr, lbi)
        vr, vi = _cmul(cr, ci, pr, pi)
        v_ref[0, trows(s), :] = vr
        v_ref[1, trows(s), :] = -vi
    laml_ref[0:1, :] = pr
    laml_ref[1:2, :] = pi

    lane = lax.broadcasted_iota(jnp.int32, (1, 128), 1)
    for gp in range(S5_PAIRS):
        lanes = slice(gp * 128, (gp + 1) * 128)
        for r in range(2):
            for src, dst in ((w_ref, wexp_ref), (v_ref, vexp_ref)):
                slab = src[r, :, lanes]
                dst[r, gp] = jnp.concatenate([jnp.where(lane < S5_P, slab, 0.0),
                                              jnp.where(lane >= S5_P, slab, 0.0)], axis=0).astype(BF16)

    gm = gm_ref[...]
    kall = None
    for r, b, sign in ((0, bbr, 1.0), (1, bbi, -1.0)):
        ah, al = _split2(a_ref[r])
        bh, bl = _split2(jnp.concatenate([b] * S5_GROUPS, axis=0) * gm)
        nt = lambda x, y: lax.dot_general(x, y, (((1,), (1,)), ((), ())), preferred_element_type=F32)
        term = (nt(ah, bh) + nt(ah, bl) + nt(al, bh)) * sign
        kall = term if kall is None else kall + term

    zeros = jnp.zeros((lc, S5_W), F32)
    for delta in range(S5_GROUPS):
        q_ref[delta, 0:lc, :] = zeros
        q_ref[delta, lc:2 * lc, :] = kall if delta == 0 else pltpu.roll(kall, delta * S5_CH, axis=1)
        q_ref[delta, 2 * lc:3 * lc, :] = zeros
    for g in range(S5_GROUPS):
        acc = zeros
        for s in range(S5_L):
            start = pl.multiple_of(lc - S5_CH * s + d * (S5_CH * (S5_L - 1)), S5_CH)
            blk = q_ref[(s - g) % S5_GROUPS, pl.ds(start, lc), :]
            acc = jnp.where(bm_ref[s:s + 1, :] > 0.5, blk, acc)
        tt_ref[g] = acc.astype(BF16)


def _s5_consts(lam_re, lam_im, log_step, b_re, b_im, c_re, c_im):
    n = DEPTH * 2
    lc = S5_L * S5_CH
    lr = lam_re.reshape(n, 1, S5_N)
    li = lam_im.reshape(n, 1, S5_N)
    st = jnp.repeat(log_step.reshape(n, S5_GROUPS), S5_P, axis=-1).reshape(n, 1, S5_N)
    br = b_re.reshape(n, S5_N, S5_CH).transpose(0, 2, 1)
    bi = b_im.reshape(n, S5_N, S5_CH).transpose(0, 2, 1)
    cr = c_re.reshape(n, S5_GROUPS, S5_CH, S5_P).transpose(0, 2, 1, 3).reshape(n, S5_CH, S5_N)
    ci = c_im.reshape(n, S5_GROUPS, S5_CH, S5_P).transpose(0, 2, 1, 3).reshape(n, S5_CH, S5_N)
    gm = jnp.asarray(np.kron(np.eye(S5_GROUPS), np.ones((S5_CH, S5_P))), F32)
    row = pl.BlockSpec((None, 1, S5_N), lambda i: (i, 0, 0))
    mat = pl.BlockSpec((None, S5_CH, S5_N), lambda i: (i, 0, 0))
    exp = pl.BlockSpec((None, 2, S5_PAIRS, 2 * lc, 128), lambda i: (i, 0, 0, 0, 0))
    laml, wexp, vexp, tt = pl.pallas_call(
        _s5_consts_kernel,
        grid=(n,),
        in_specs=[row, row, row, mat, mat, mat, mat, pl.BlockSpec((S5_W, S5_N), lambda i: (0, 0)),
                  pl.BlockSpec((S5_GROUPS, S5_W), lambda i: (0, 0))],
        out_specs=[pl.BlockSpec((None, 2, S5_N), lambda i: (i, 0, 0)), exp, exp,
                   pl.BlockSpec((None, S5_GROUPS, lc, lc), lambda i: (i, 0, 0, 0))],
        out_shape=[jax.ShapeDtypeStruct((n, 2, S5_N), F32),
                   jax.ShapeDtypeStruct((n, 2, S5_PAIRS, 2 * lc, 128), BF16),
                   jax.ShapeDtypeStruct((n, 2, S5_PAIRS, 2 * lc, 128), BF16),
                   jax.ShapeDtypeStruct((n, S5_GROUPS, lc, lc), BF16)],
        scratch_shapes=[pltpu.VMEM((2, lc, S5_N), F32), pltpu.VMEM((2, lc, S5_N), F32),
                        pltpu.VMEM((2, lc, S5_N), F32), pltpu.VMEM((S5_GROUPS, 3 * lc, S5_W), F32)],
        compiler_params=_cparams(("arbitrary",)),
        name="s5_consts",
    )(lr, li, st, br, bi, cr, ci, gm, _lane_block_masks())
    return laml.reshape(n, 2, S5_PAIRS, 128), wexp, vexp, tt


def _lane_block_masks():
    return jnp.asarray(np.kron(np.eye(S5_GROUPS), np.ones((1, S5_CH))), F32)


def _nt(a, b):
    return lax.dot_general(a, b, (((1,), (1,)), ((), ())), preferred_element_type=F32)


def _lane_block_transpose(arrs):
    n = len(arrs)
    width = arrs[0].shape[1]
    blk = lax.broadcasted_iota(jnp.int32, (1, width), 1) // (width // n)
    arrs = list(arrs)
    h = n // 2
    while h >= 1:
        hi = (blk & h) != 0
        for a in range(n):
            if a & h:
                continue
            lo_arr, hi_arr = arrs[a], arrs[a + h]
            arrs[a] = jnp.where(hi, pltpu.roll(hi_arr, h * (width // n), axis=1), lo_arr)
            arrs[a + h] = jnp.where(hi, hi_arr, pltpu.roll(lo_arr, width - h * (width // n), axis=1))
        h //= 2
    return arrs


def _s5_kernel(u0_ref, u1_ref, tt_ref, wexp_ref, vexp_ref, laml_ref, *rest, nseq, nblk, zero_init):
    if zero_init:
        y_ref, hf_ref, ug_sc, yg_sc, hin_re, hin_im, hp_re, hp_im = rest
    else:
        h0_ref, y_ref, hf_ref, ug_sc, yg_sc, hin_re, hin_im, hp_re, hp_im = rest
    d = pl.program_id(0)
    r = nblk * nseq
    lc = S5_L * S5_CH

    @pl.when(d == 0)
    def _():
        xs = []
        for s in range(S5_L):
            rows = pl.ds(s, r, stride=S5_L)
            xs.append(jnp.concatenate([u0_ref[rows, :], u1_ref[rows, :]], axis=1))
        for g, ug in enumerate(_lane_block_transpose(xs)):
            ug_sc[g] = ug.astype(BF16)

    for gp in range(S5_PAIRS):
        up = jnp.concatenate([ug_sc[2 * gp], ug_sc[2 * gp + 1]], axis=1)
        rows = pl.ds(gp, r, stride=S5_PAIRS)
        hin_re[rows, :] = jnp.dot(up, wexp_ref[0, gp], preferred_element_type=F32)
        hin_im[rows, :] = jnp.dot(up, wexp_ref[1, gp], preferred_element_type=F32)

    lr = laml_ref[0]
    li = laml_ref[1]

    def step(kk, carry):
        k = kk + d * (nblk - 1 - 2 * kk)
        new = []
        for b in range(nseq):
            rows = pl.ds(pl.multiple_of((b * nblk + k) * S5_PAIRS, S5_PAIRS), S5_PAIRS)
            hr, hi = carry[2 * b], carry[2 * b + 1]
            hp_re[rows, :] = hr
            hp_im[rows, :] = hi
            new += [lr * hr - li * hi + hin_re[rows, :], lr * hi + li * hr + hin_im[rows, :]]
        return tuple(new)

    init = []
    for b in range(nseq):
        rows = slice(b * S5_PAIRS, (b + 1) * S5_PAIRS)
        init += ([jnp.zeros((S5_PAIRS, 128), F32)] * 2 if zero_init else [h0_ref[0, rows, :], h0_ref[1, rows, :]])
    fin = lax.fori_loop(0, nblk, step, tuple(init))
    for b in range(nseq):
        rows = slice(b * S5_PAIRS, (b + 1) * S5_PAIRS)
        hf_ref[0, rows, :] = fin[2 * b]
        hf_ref[1, rows, :] = fin[2 * b + 1]

    for gp in range(S5_PAIRS):
        rows = pl.ds(gp, r, stride=S5_PAIRS)
        ys = (_nt(hp_re[rows, :].astype(BF16), vexp_ref[0, gp])
              + _nt(hp_im[rows, :].astype(BF16), vexp_ref[1, gp]))
        for j in range(2):
            g = 2 * gp + j
            yg = _nt(ug_sc[g], tt_ref[g]) + ys[:, j * lc:(j + 1) * lc]

            @pl.when(d == 0)
            def _():
                yg_sc[g] = yg

            @pl.when(d == 1)
            def _():
                yg_sc[g] += yg

    @pl.when(d == 1)
    def _():
        for t, yt in enumerate(_lane_block_transpose([yg_sc[g] for g in range(S5_GROUPS)])):
            rows = pl.ds(t, r, stride=S5_L)
            y_ref[0, rows, :] = yt[:, 0:128]
            y_ref[1, rows, :] = yt[:, 128:2 * 128]


def _s5(proj, consts, h0, *, layer, nseq, n_seq):
    laml, wexp, vexp, tt = consts
    t = proj.shape[0]
    nblk = n_seq // S5_L
    r = nblk * nseq
    lc = S5_L * S5_CH
    bs = nseq * S5_PAIRS
    zero_init = h0 is None
    h_spec = pl.BlockSpec((None, 2, bs, 128), lambda d: (d, 0, 0, 0))
    return pl.pallas_call(
        functools.partial(_s5_kernel, nseq=nseq, nblk=nblk, zero_init=zero_init),
        grid=(2,),
        in_specs=[pl.BlockSpec((t, 128), lambda d: (0, P_SU // 128)),
                  pl.BlockSpec((t, 128), lambda d: (0, P_SU // 128 + 1)),
                  pl.BlockSpec((None, S5_GROUPS, lc, lc), lambda d: (2 * layer + d, 0, 0, 0)),
                  pl.BlockSpec((None, 2, S5_PAIRS, 2 * lc, 128), lambda d: (2 * layer + d, 0, 0, 0, 0)),
                  pl.BlockSpec((None, 2, S5_PAIRS, 2 * lc, 128), lambda d: (2 * layer + d, 0, 0, 0, 0)),
                  pl.BlockSpec((None, 2, S5_PAIRS, 128), lambda d: (2 * layer + d, 0, 0, 0))]
                 + ([] if zero_init else [h_spec]),
        out_specs=[pl.BlockSpec((2, t, 128), lambda d: (0, 0, 0)), h_spec],
        out_shape=[jax.ShapeDtypeStruct((2, t, 128), F32),
                   jax.ShapeDtypeStruct((2, 2, bs, 128), F32)],
        scratch_shapes=[pltpu.VMEM((S5_GROUPS, r, lc), BF16), pltpu.VMEM((S5_GROUPS, r, lc), F32)]
                       + [pltpu.VMEM((r * S5_PAIRS, 128), F32) for _ in range(4)],
        compiler_params=_cparams(("arbitrary",)),
        name="s5",
    )(proj, proj, tt, wexp, vexp, laml, *([] if zero_init else [h0]))


def _out_kernel(x_ref, mod_ref, ogf_ref, ogb_ref, odf_ref, odb_ref, ysf_ref, ysb_ref, gog_ref, su_ref, dog_ref,
                ind_ref, gw_ref, dw_ref, sd_ref, wglu_ref, bglu_ref, wout_ref, o_ref):
    og = ogf_ref[...] + ogb_ref[...]
    hi, mid, _ = _split3(og * og)
    ms = jnp.dot(hi, ind_ref[...], preferred_element_type=F32) + jnp.dot(mid, ind_ref[...], preferred_element_type=F32)
    og = og * lax.rsqrt(ms + EPS) * gw_ref[...] * _silu(gog_ref[...])
    od = odf_ref[...] + odb_ref[...]
    dog = dog_ref[...]
    parts = []
    for h in range(GDN_HEADS):
        z = od[:, h * GDN_DV:(h + 1) * GDN_DV]
        z = z * lax.rsqrt(jnp.mean(z * z, axis=-1, keepdims=True) + EPS) * dw_ref[...]
        parts.append(z * _silu(dog[:, h * GDN_DV:(h + 1) * GDN_DV]))
    y = su_ref[...] * sd_ref[...] + jnp.concatenate([ysf_ref[...], ysb_ref[...]], axis=1)
    g = 0.5 * y * (1.0 + jnp.tanh(math.sqrt(2.0 / math.pi) * (y + 0.044715 * (y * y * y))))
    os_ = g * jax.nn.sigmoid(_bdot(g, wglu_ref[...]) + bglu_ref[...])
    out = _bdot(og, wout_ref[0:GLA_W, :]) + _bdot(os_, wout_ref[GLA_W + GDN_W:MIX_W, :])
    for h in range(GDN_HEADS):
        lo = GLA_W + h * GDN_DV
        out = out + _bdot(parts[h], wout_ref[lo:lo + GDN_DV, :])
    o_ref[...] = x_ref[...] + mod_ref[5:6, :] * out


def _mix_out(x, mods, proj, og, od, ys, gla_nw, gdn_nw, s5_d, w_glu, b_glu, w_out, *, layer, n_seq):
    t, d = x.shape
    nb = mods.shape[0]
    tm = _pick_tile(n_seq if nb > 1 else t, 512)
    midx = _mod_index(nb, tm, n_seq)
    ys2 = ys

    def dirspec(w, dd):
        return pl.BlockSpec((None, tm, w), lambda i: (dd, i, 0))

    def vec(w):
        return pl.BlockSpec((None, 1, w), lambda i: (layer, 0, 0))

    return pl.pallas_call(
        _out_kernel,
        grid=(t // tm,),
        in_specs=[pl.BlockSpec((tm, d), lambda i: (i, 0)),
                  pl.BlockSpec((None, N_MOD, d), lambda i: (midx(i), 0, 0)),
                  pl.BlockSpec((tm, GLA_W), lambda i: (i, 0)), pl.BlockSpec((tm, GLA_W), lambda i: (i, 0)),
                  pl.BlockSpec((tm, GDN_W), lambda i: (i, 0)), pl.BlockSpec((tm, GDN_W), lambda i: (i, 0)),
                  dirspec(128, 0), dirspec(128, 1),
                  pl.BlockSpec((tm, GLA_W), lambda i: (i, P_GOG // GLA_W)),
                  pl.BlockSpec((tm, S5_W), lambda i: (i, P_SU // S5_W)),
                  pl.BlockSpec((tm, GDN_W), lambda i: (i, P_DOG // GDN_W)),
                  pl.BlockSpec((GLA_W, GLA_W), lambda i: (0, 0)),
                  vec(GLA_W), vec(GDN_DV), vec(S5_W),
                  pl.BlockSpec((None, S5_W, S5_W), lambda i: (layer, 0, 0)),
                  vec(S5_W),
                  pl.BlockSpec((None, MIX_W, d), lambda i: (layer, 0, 0))],
        out_specs=pl.BlockSpec((tm, d), lambda i: (i, 0)),
        out_shape=jax.ShapeDtypeStruct((t, d), F32),
        compiler_params=_cparams(("parallel",)),
        name="mix_out",
    )(x, mods, og[0], og[1], od[0], od[1], ys2, ys2, proj, proj, proj,
      jnp.asarray(_C["ind64"], BF16),
      jnp.tile(gla_nw, (1, GLA_HEADS)).reshape(DEPTH, 1, GLA_W), gdn_nw.reshape(DEPTH, 1, GDN_DV),
      s5_d.reshape(DEPTH, 1, S5_W), w_glu, b_glu.reshape(DEPTH, 1, S5_W), w_out)


def _gla_params(gk_up, gk_bias, layer):
    up = jnp.zeros((2, 128, GLA_KW), F32).at[:, :GLA_RANK, :].set(gk_up[layer])
    return up, gk_bias[layer].reshape(2, 1, GLA_KW)


def _gdn_params(a_log, dt_bias, layer):
    arow = jnp.zeros((2, 1, 128), F32).at[:, 0, SM_A:SM_A + GDN_HEADS].set(a_log[layer])
    dtb = jnp.zeros((2, 1, 128), F32).at[:, 0, SM_A:SM_A + GDN_HEADS].set(dt_bias[layer])
    return arow, dtb


def _gla_state_in(s):
    eye = jnp.eye(GLA_HEADS, dtype=F32)
    st = jnp.einsum('bzhde,hg->zbhegd', s.astype(F32), eye)
    return st.reshape(2, s.shape[0], GLA_W, GLA_KW)


def _gla_state_out(st):
    b = st.shape[1]
    s = st.reshape(2, b, GLA_HEADS, GLA_DV, GLA_HEADS, GLA_DK)
    s = jnp.stack([s[:, :, h, :, h, :] for h in range(GLA_HEADS)], axis=2)
    return s.transpose(1, 0, 2, 4, 3)


def _trunk(x3, mods, rows, states, p, s5c, w_in_p):
    nseq, n_seq, d = x3.shape
    x = x3.reshape(nseq * n_seq, d)
    fin = []
    for l in range(DEPTH):
        m = mods[l]
        x = _ffn(x, m, p['norm_w'], p['ffn_w_gate'], p['ffn_w_up'], p['ffn_w_down'], p['final_norm_w'],
                 layer=l, which=0, n_seq=n_seq, final=False)
        proj = _proj(x, m, p['norm_w'], w_in_p, p['gdn_conv_w'], layer=l, n_seq=n_seq, rowlen=n_seq // rows)
        if states is None:
            st0 = s0 = h0 = None
        else:
            st_gla, st_gdn, st_re, st_im = states
            st0 = _gla_state_in(st_gla[:, l])
            s0 = st_gdn[:, l].transpose(1, 0, 2, 3, 4)
            h0 = jnp.stack([st_re[:, l].reshape(nseq, 2, S5_N), st_im[:, l].reshape(nseq, 2, S5_N)], axis=0)
            h0 = h0.transpose(2, 0, 1, 3).reshape(2, 2, nseq * S5_PAIRS, 128)
        up_p, bias_p = _gla_params(p['gla_gk_up'], p['gla_gk_bias'], l)
        og_f, og_b, gla_f = _gla(proj, up_p, bias_p, st0, nseq=nseq, n_seq=n_seq)
        og = (og_f, og_b)
        arow, dtb = _gdn_params(p['gdn_a_log'], p['gdn_dt_bias'], l)
        od_f, od_b, gdn_f = _gdn(proj, arow, dtb, s0, nseq=nseq, n_seq=n_seq)
        od = (od_f, od_b)
        ys, s5_f = _s5(proj, s5c, h0, layer=l, nseq=nseq, n_seq=n_seq)
        x = _mix_out(x, m, proj, og, od, ys, p['gla_norm_w'], p['gdn_norm_w'], p['s5_d'], p['s5_w_glu'],
                     p['s5_b_glu'], p['w_out'], layer=l, n_seq=n_seq)
        x = _ffn(x, m, p['norm_w'], p['ffn_w_gate'], p['ffn_w_up'], p['ffn_w_down'], p['final_norm_w'],
                 layer=l, which=1, n_seq=n_seq, final=(l == DEPTH - 1))
        fin.append((gla_f, gdn_f, s5_f))
    return x.reshape(nseq, n_seq, d), fin


def kernel(x_prompt, x_sample, c, state_gla, state_gdn, state_s5_re, state_s5_im, c_ctx, w_ada, b_ada, norm_w, ffn_w_gate, ffn_w_up, ffn_w_down, w_in, gla_gk_up, gla_gk_bias, gla_norm_w, gdn_conv_w, gdn_a_log, gdn_dt_bias, gdn_norm_w, s5_lam_re, s5_lam_im, s5_log_step, s5_b_re, s5_b_im, s5_c_re, s5_c_im, s5_d, s5_w_glu, s5_b_glu, w_out, final_norm_w):
    p = dict(norm_w=norm_w, ffn_w_gate=ffn_w_gate, ffn_w_up=ffn_w_up, ffn_w_down=ffn_w_down,
             gla_gk_up=gla_gk_up, gla_gk_bias=gla_gk_bias, gla_norm_w=gla_norm_w, gdn_conv_w=gdn_conv_w,
             gdn_a_log=gdn_a_log, gdn_dt_bias=gdn_dt_bias, gdn_norm_w=gdn_norm_w, s5_c_re=s5_c_re,
             s5_c_im=s5_c_im, s5_d=s5_d, s5_w_glu=s5_w_glu, s5_b_glu=s5_b_glu, w_out=w_out,
             final_norm_w=final_norm_w)
    b_ctx = x_prompt.shape[0]
    b_dec = x_sample.shape[0]
    cond8 = jnp.zeros((8, D_MODEL), F32).at[0].set(c_ctx).at[1:1 + b_dec].set(c)
    mods = _ada(cond8, w_ada, b_ada)
    mods_ctx = mods[:, 0:1].reshape(DEPTH, 1, N_MOD, D_MODEL)
    mods_dec = mods[:, 1:1 + b_dec].reshape(DEPTH, b_dec, N_MOD, D_MODEL)
    s5c = _s5_consts(s5_lam_re, s5_lam_im, s5_log_step, s5_b_re, s5_b_im, s5_c_re, s5_c_im)
    w_in_p = _permute_w_in(w_in)

    y_prompt, fin = _trunk(x_prompt, mods_ctx, 1, None, p, s5c, w_in_p)
    new_gla = jnp.stack([_gla_state_out(f[0]) for f in fin], axis=1)
    new_gdn = jnp.stack([f[1].transpose(1, 0, 2, 3, 4) for f in fin], axis=1)
    s5f = jnp.stack([f[2].reshape(2, 2, b_ctx, S5_GROUPS, S5_P).transpose(1, 2, 0, 3, 4) for f in fin], axis=2)
    new_re, new_im = s5f[0], s5f[1]

    rows = x_sample.shape[1] // 64
    y_sample, _ = _trunk(x_sample, mods_dec, rows, (state_gla, state_gdn, state_s5_re, state_s5_im),
                         p, s5c, w_in_p)
    return (y_prompt, y_sample, new_gla.astype(state_gla.dtype), new_gdn.astype(state_gdn.dtype),
            new_re.astype(state_s5_re.dtype), new_im.astype(state_s5_im.dtype))
```

```python
import functools
import math

import numpy as np
import jax
import jax.numpy as jnp
from jax import lax
from jax.experimental import pallas as pl
from jax.experimental.pallas import tpu as pltpu

F32 = jnp.float32
BF16 = jnp.bfloat16

D_MODEL = 1024
DEPTH = 2
FFN_DIM = 2816
N_MOD = 9
EPS = 1e-6
CHUNK = 64
GLA_HEADS, GLA_DK, GLA_DV, GLA_RANK = 4, 32, 64, 16
GLA_GATE_NORM = 16.0
GDN_HEADS, GDN_DK, GDN_DV, GDN_CONV = 4, 128, 128, 5
S5_GROUPS, S5_CH, S5_P = 16, 16, 64
GLA_W = GLA_HEADS * GLA_DV
GLA_KW = GLA_HEADS * GLA_DK
GDN_W = GDN_HEADS * GDN_DV
S5_W = S5_GROUPS * S5_CH
S5_N = S5_GROUPS * S5_P
S5_L = 16
S5_PAIRS = S5_GROUPS // 2
MIX_W = GLA_W + GDN_W + S5_W

_IN_SIZES = (GLA_KW, GLA_KW, GLA_W, GLA_RANK, GLA_RANK, GLA_W,
             GDN_W, GDN_W, GDN_W, GDN_HEADS, GDN_HEADS, GDN_HEADS, GDN_HEADS, GDN_W, S5_W)
_IN_OFF = tuple(int(v) for v in np.cumsum((0,) + _IN_SIZES))
(_GQ, _GK, _GV, _GLRF, _GLRB, _GOG, _DQ, _DK, _DV, _DAF, _DAB, _DBF, _DBB, _DOG, _SU) = range(15)

P_DQKV = 0
P_SMALL = 1536
P_GQK = 1792
P_GV = 2048
P_GOG = 2304
P_DOG = 2560
P_SU = 3072
PROJ_W = 3328
PROJ_TN = PROJ_W // 2
SM_A = GLA_RANK
SM_B = GLA_RANK + GDN_HEADS
SUB = 256
VMEM_LIMIT = 56 * 1024 * 1024


def _cparams(sem):
    return pltpu.CompilerParams(dimension_semantics=sem, vmem_limit_bytes=VMEM_LIMIT)


def _bdot(a, b):
    return jnp.dot(a.astype(BF16), b.astype(BF16), preferred_element_type=F32)


def _bdot_nt(a, b):
    return lax.dot_general(a.astype(BF16), b.astype(BF16), (((1,), (1,)), ((), ())),
                           preferred_element_type=F32)


def _split3(x):
    hi = x.astype(BF16)
    r1 = x - hi.astype(F32)
    mid = r1.astype(BF16)
    lo = (r1 - mid.astype(F32)).astype(BF16)
    return hi, mid, lo


def _sel_dot(m01, x):
    hi, mid, lo = _split3(x)
    return (jnp.dot(m01, hi, preferred_element_type=F32) + jnp.dot(m01, mid, preferred_element_type=F32)
            + jnp.dot(m01, lo, preferred_element_type=F32))


def _silu(x):
    return x * jax.nn.sigmoid(x)


def _softplus(x):
    return jnp.maximum(x, 0.0) + jnp.log1p(jnp.exp(-jnp.abs(x)))


def _log_sigmoid(x):
    return -_softplus(-x)


def _norm_mod(x, nw, scale, shift):
    y = x * lax.rsqrt(jnp.mean(x * x, axis=-1, keepdims=True) + EPS)
    return y * nw * (1.0 + scale) + shift


def _pick_tile(n, cap):
    t = cap
    while n % t:
        t //= 2
    return t


def _ada_kernel(c_ref, w_ref, b_ref, o_ref):
    o_ref[...] = _bdot(_silu(c_ref[...]), w_ref[...]) + b_ref[...]


def _ada(cond8, w_ada, b_ada):
    depth, d, nout = w_ada.shape
    tn = 1024
    return pl.pallas_call(
        _ada_kernel,
        grid=(depth, nout // tn),
        in_specs=[pl.BlockSpec((8, d), lambda l, j: (0, 0)),
                  pl.BlockSpec((None, d, tn), lambda l, j: (l, 0, j)),
                  pl.BlockSpec((None, 1, tn), lambda l, j: (l, 0, j))],
        out_specs=pl.BlockSpec((None, 8, tn), lambda l, j: (l, 0, j)),
        out_shape=jax.ShapeDtypeStruct((depth, 8, nout), F32),
        compiler_params=_cparams(("parallel", "parallel")),
        name="ada",
    )(cond8, w_ada, b_ada.reshape(depth, 1, nout))


def _ffn_kernel(x_ref, mod_ref, nw_ref, wg_ref, wu_ref, wd_ref, fw_ref, o_ref, h_ref, *, mod_base, final):
    j = pl.program_id(1)

    @pl.when(j == 0)
    def _():
        m = mod_ref[...]
        h = _norm_mod(x_ref[...], nw_ref[...], m[mod_base + 1:mod_base + 2], m[mod_base:mod_base + 1])
        h_ref[...] = h.astype(BF16)
        o_ref[...] = jnp.zeros_like(o_ref)

    h = h_ref[...]
    g = jnp.dot(h, wg_ref[...].astype(BF16), preferred_element_type=F32)
    u = jnp.dot(h, wu_ref[...].astype(BF16), preferred_element_type=F32)
    o_ref[...] += _bdot(_silu(g) * u, wd_ref[...])

    @pl.when(j == pl.num_programs(1) - 1)
    def _():
        m = mod_ref[...]
        y = x_ref[...] + 0.5 * m[mod_base + 2:mod_base + 3] * o_ref[...]
        if final:
            y = y * lax.rsqrt(jnp.mean(y * y, axis=-1, keepdims=True) + EPS) * fw_ref[...]
        o_ref[...] = y


def _mod_index(nb, tm, n_seq):
    if nb == 1:
        return lambda i: 0
    return lambda i: (i * tm) // n_seq


def _ffn(x, mods, norm_w, w_gate, w_up, w_down, final_w, *, layer, which, n_seq, final):
    t, d = x.shape
    f = w_gate.shape[-1]
    nb = mods.shape[0]
    tm = _pick_tile(n_seq if nb > 1 else t, 2048)
    tf = 256
    midx = _mod_index(nb, tm, n_seq)
    mod_base = 0 if which == 0 else 6
    return pl.pallas_call(
        functools.partial(_ffn_kernel, mod_base=mod_base, final=final),
        grid=(t // tm, f // tf),
        in_specs=[pl.BlockSpec((tm, d), lambda i, j: (i, 0), pipeline_mode=pl.Buffered(1)),
                  pl.BlockSpec((None, N_MOD, d), lambda i, j: (midx(i), 0, 0)),
                  pl.BlockSpec((None, None, 1, d), lambda i, j: (layer, 2 * which, 0, 0)),
                  pl.BlockSpec((None, None, d, tf), lambda i, j: (layer, which, 0, j)),
                  pl.BlockSpec((None, None, d, tf), lambda i, j: (layer, which, 0, j)),
                  pl.BlockSpec((None, None, tf, d), lambda i, j: (layer, which, j, 0)),
                  pl.BlockSpec((1, d), lambda i, j: (0, 0))],
        out_specs=pl.BlockSpec((tm, d), lambda i, j: (i, 0)),
        out_shape=jax.ShapeDtypeStruct((t, d), F32),
        scratch_shapes=[pltpu.VMEM((tm, d), BF16)],
        compiler_params=_cparams(("parallel", "arbitrary")),
        name="ffn",
    )(x, mods, norm_w.reshape(DEPTH, 3, 1, d), w_gate, w_up, w_down, final_w.reshape(1, d))


def _proj_kernel(x_ref, mod_ref, nw_ref, w_ref, cw_ref, o_ref, h_ref, wb_ref, *, rowlen):
    j = pl.program_id(0)
    i = pl.program_id(1)

    for jt, segs in enumerate(_W_SEGS):
        @pl.when(jnp.logical_and(i == 0, j == jt))
        def _():
            for lo, src, width in segs:
                if src is None:
                    wb_ref[:, lo:lo + width] = jnp.zeros((wb_ref.shape[0], width), BF16)
                else:
                    wb_ref[:, lo:lo + width] = w_ref[:, src:src + width].astype(BF16)

    @pl.when(j == 0)
    def _():
        m = mod_ref[...]
        h_ref[i] = _norm_mod(x_ref[...], nw_ref[...], m[4:5], m[3:4]).astype(BF16)
        o_ref[...] = jnp.dot(h_ref[i], wb_ref[...], preferred_element_type=F32)
        tm = o_ref.shape[0]
        pos = lax.broadcasted_iota(jnp.int32, (tm, 1), 0) % rowlen
        valid = [jnp.logical_and(pos + (t - GDN_CONV // 2) >= 0, pos + (t - GDN_CONV // 2) < rowlen)
                 for t in range(GDN_CONV)]
        for g in range(3 * GDN_HEADS):
            cols = slice(P_DQKV + g * GDN_DK, P_DQKV + (g + 1) * GDN_DK)
            xg = o_ref[:, cols]
            acc = xg * cw_ref[GDN_CONV // 2:GDN_CONV // 2 + 1, cols]
            for t in range(GDN_CONV):
                s = t - GDN_CONV // 2
                if s != 0:
                    xs = pltpu.roll(xg, (-s) % tm, axis=0)
                    acc = acc + jnp.where(valid[t], xs, 0.0) * cw_ref[t:t + 1, cols]
            y = _silu(acc)
            if g < 2 * GDN_HEADS:
                y = y * lax.rsqrt(jnp.sum(y * y, axis=-1, keepdims=True) + EPS)
                if g < GDN_HEADS:
                    y = y * (GDN_DK ** -0.5)
            o_ref[:, cols] = y

    @pl.when(j == 1)
    def _():
        o_ref[...] = jnp.dot(h_ref[i], wb_ref[...], preferred_element_type=F32)


def _proj(x, mods, norm_w, w_in, conv_w, *, layer, n_seq, rowlen):
    t, d = x.shape
    nb = mods.shape[0]
    tm = _pick_tile(n_seq if nb > 1 else t, 512)
    midx = _mod_index(nb, tm, n_seq)
    return pl.pallas_call(
        functools.partial(_proj_kernel, rowlen=rowlen),
        grid=(PROJ_W // PROJ_TN, t // tm),
        in_specs=[pl.BlockSpec((tm, d), lambda j, i: (i * (1 - j), 0)),
                  pl.BlockSpec((None, N_MOD, d), lambda j, i: (midx(i), 0, 0)),
                  pl.BlockSpec((None, None, 1, d), lambda j, i: (layer, 1, 0, 0)),
                  pl.BlockSpec((None, d, w_in.shape[-1]), lambda j, i: (layer, 0, 0), pipeline_mode=pl.Buffered(1)),
                  pl.BlockSpec((None, GDN_CONV, 3 * GDN_W), lambda j, i: (layer, 0, 0))],
        out_specs=pl.BlockSpec((tm, PROJ_TN), lambda j, i: (i, j)),
        out_shape=jax.ShapeDtypeStruct((t, PROJ_W), F32),
        scratch_shapes=[pltpu.VMEM((t // tm, tm, d), BF16), pltpu.VMEM((d, PROJ_TN), BF16)],
        compiler_params=_cparams(("arbitrary", "arbitrary")),
        name="proj",
    )(x, mods, norm_w.reshape(DEPTH, 3, 1, d), w_in, conv_w)


def _w_in_segments():
    order = [_DQ, _DK, _DV, _GLRF, _DAF, _DBF, None, _GLRB, _DAB, _DBB, None,
             _GQ, _GK, _GV, _GOG, _DOG, _SU]
    sm_pad = 128 - GLA_RANK - 2 * GDN_HEADS
    tiles = [[] for _ in range(PROJ_W // PROJ_TN)]
    dst = 0
    for s in order:
        width = sm_pad if s is None else _IN_SIZES[s]
        src = None if s is None else _IN_OFF[s]
        tile = tiles[dst // PROJ_TN]
        lo = dst % PROJ_TN
        if tile and src is not None and tile[-1][1] is not None and (
                tile[-1][0] + tile[-1][2] == lo and tile[-1][1] + tile[-1][2] == src):
            tile[-1] = (tile[-1][0], tile[-1][1], tile[-1][2] + width)
        else:
            tile.append((lo, src, width))
        dst += width
        assert (dst - 1) // PROJ_TN == (dst - width) // PROJ_TN, "a segment may not straddle column tiles"
    assert dst == PROJ_W
    return tiles


_W_SEGS = _w_in_segments()


def _np_consts():
    t = np.arange(CHUNK)
    lower = (t[:, None] >= t[None, :])
    incl = np.stack([lower, lower.T])
    strict = np.stack([t[:, None] > t[None, :], t[:, None] < t[None, :]])
    eye4 = np.eye(4, dtype=bool)
    blk = np.kron(eye4, np.ones((CHUNK, CHUNK), bool))
    incl_bd = np.stack([np.kron(eye4, incl[d]) for d in range(2)])
    strict_bd = np.stack([np.kron(eye4, strict[d]) for d in range(2)])
    levels = []
    for d in range(2):
        per = []
        for lv in range(6):
            s = 1 << lv
            same = (t[:, None] // (2 * s)) == (t[None, :] // (2 * s))
            hi = (t % (2 * s)) >= s
            m = same & hi[:, None] & (~hi)[None, :]
            if d == 1:
                m = m.T
            per.append(np.kron(eye4, m))
        levels.append(np.stack(per))
    levels = np.stack(levels)
    assert (levels.sum(1) == strict_bd).all() and blk.any()
    gla_mask = np.stack([np.tile(incl[d], (1, 4)) for d in range(2)])
    hm_k = np.kron(eye4, np.ones((1, GLA_DK)))
    hm_v = np.kron(eye4, np.ones((1, GLA_DV)))
    st_mask = np.kron(eye4, np.ones((GLA_DV, GLA_DK)))
    ind64 = np.kron(eye4, np.full((GLA_DV, GLA_DV), 1.0 / GLA_DV))
    gla_sel, gla_later, gla_earlier, gla_pair = [], [], [], []
    for d in range(2):
        pos = t if d == 0 else CHUNK - 1 - t
        cum = (pos[None, :] <= pos[:, None]).astype(np.float64)
        sel, later, earlier, pair = [cum], [], [], []
        for lv in range(6):
            s = 1 << lv
            ref_pos = (pos // (2 * s)) * (2 * s) + s - 1
            pick = (pos[None, :] == ref_pos[:, None]).astype(np.float64)
            sel.append(pick @ cum)
            is_later = (pos % (2 * s)) >= s
            same = (pos[:, None] // (2 * s)) == (pos[None, :] // (2 * s))
            later.append(np.repeat(is_later[:, None], GLA_KW, axis=1))
            earlier.append(np.repeat(~is_later[:, None], GLA_KW, axis=1))
            pair.append(np.tile(same & is_later[:, None] & (~is_later)[None, :], (1, 4)))
        gla_sel.append(np.concatenate(sel, axis=0))
        gla_later.append(np.stack(later))
        gla_earlier.append(np.stack(earlier))
        gla_pair.append(np.stack(pair))
        assert (np.stack(pair).sum(0) == np.tile(pos[:, None] > pos[None, :], (1, 4))).all()
    head_sum = np.kron(eye4, np.ones((GLA_DK, GLA_DV)))
    return dict(incl=incl, incl_bd=incl_bd, strict_bd=strict_bd, levels=levels, gla_mask=gla_mask,
                hm_k=hm_k, hm_v=hm_v, st_mask=st_mask, ind64=ind64, eye256=np.eye(256),
                gla_sel=np.stack(gla_sel), gla_later=np.stack(gla_later), gla_earlier=np.stack(gla_earlier),
                gla_pair=np.stack(gla_pair), head_sum=head_sum)


_C = _np_consts()


def _row_block_index(nblk):
    return lambda d, b, i: b * nblk + i + d * (nblk - 1 - 2 * i)


def _gla_kernel(qkf_ref, qkb_ref, vf_ref, vb_ref, smf_ref, smb_ref, up_ref, bias_ref, sel_ref, later_ref,
                earlier_ref, pair_ref, hmk_ref, hmv_ref, stm_ref, hsum_ref, *rest, nchunk, zero_init):
    if zero_init:
        of_ref, ob_ref, stf_ref, st_ref, qd_sc, m_sc, dec_sc = rest
    else:
        st0_ref, of_ref, ob_ref, stf_ref, st_ref, qd_sc, m_sc, dec_sc = rest
    ins = ((qkf_ref, vf_ref, smf_ref, of_ref), (qkb_ref, vb_ref, smb_ref, ob_ref))

    @pl.when(pl.program_id(1) == 0)
    def _():
        st_ref[...] = jnp.zeros_like(st_ref) if zero_init else st0_ref[...]

    hmk = hmk_ref[...]
    hmv = hmv_ref[...]
    stm = stm_ref[...]

    units = [(dd, c) for c in range(nchunk) for dd in range(2)]
    st1 = []
    for dd, c in units:
        qk_ref, v_ref, sm_ref, _ = ins[dd]
        rows = pl.ds(c * CHUNK, CHUNK)
        qk = qk_ref[rows, :]
        q = qk[:, 0:GLA_KW] * (GLA_DK ** -0.5)
        k = qk[:, GLA_KW:2 * GLA_KW]
        x = _bdot(sm_ref[rows, :], up_ref[dd]) + bias_ref[dd]
        lg = _log_sigmoid(x) * (1.0 / GLA_GATE_NORM)
        ball = _sel_dot(sel_ref[dd], lg)
        st1.append((q, k, v_ref[rows, :], ball, jnp.sum(lg, axis=0, keepdims=True)))
    prods = []
    for lv in range(6):
        for (dd, c), (q, k, _, ball, _) in zip(units, st1):
            b = ball[0:CHUNK]
            ref = ball[(lv + 1) * CHUNK:(lv + 2) * CHUNK]
            ql = q * jnp.where(later_ref[dd, lv] > 0.5, jnp.exp(jnp.minimum(b - ref, 0.0)), 0.0)
            kl = k * jnp.where(earlier_ref[dd, lv] > 0.5, jnp.exp(jnp.minimum(ref - b, 0.0)), 0.0)
            kexp = jnp.concatenate([kl * hmk[h:h + 1] for h in range(GLA_HEADS)], axis=0)
            prods.append(pair_ref[dd, lv] * _bdot_nt(ql, kexp))
    for i, ((dd, c), (q, k, v, ball, btot)) in enumerate(zip(units, st1)):
        o_ref = ins[dd][3]
        n = dd * nchunk + c
        b = ball[0:CHUNK]
        att = prods[i]
        for lv in range(1, 6):
            att = att + prods[lv * len(units) + i]
        vexp = jnp.concatenate([v * hmv[h:h + 1] for h in range(GLA_HEADS)], axis=0)
        o_ref[pl.ds(c * CHUNK, CHUNK), :] = _bdot(att, vexp) + _bdot(q * k, hsum_ref[...]) * v
        qd_sc[n] = (q * jnp.exp(b)).astype(BF16)
        kd = k * jnp.exp(btot - b)
        m_sc[n] = _bdot(v.T, kd) * stm
        dec_sc[n] = jnp.broadcast_to(jnp.exp(btot), (8, GLA_KW))

    for dd in range(2):
        o_ref = ins[dd][3]
        st = st_ref[dd]
        for ci in range(nchunk):
            c = ci if dd == 0 else nchunk - 1 - ci
            n = dd * nchunk + c
            rows = pl.ds(c * CHUNK, CHUNK)
            o_ref[rows, :] += _nt(qd_sc[n], st.astype(BF16))
            st = st * dec_sc[n][0:1] + m_sc[n]
        st_ref[dd] = st
    stf_ref[...] = st_ref[...]


def _gla(proj, up_p, bias_p, st0, *, nseq, n_seq):
    t = proj.shape[0]
    nblk = n_seq // SUB
    nck = SUB // CHUNK
    rbi = _row_block_index(nblk)
    c = _C
    consts = [jnp.asarray(c["gla_sel"], BF16), jnp.asarray(c["gla_later"], F32), jnp.asarray(c["gla_earlier"], F32),
              jnp.asarray(c["gla_pair"], F32), jnp.asarray(c["hm_k"], F32), jnp.asarray(c["hm_v"], F32),
              jnp.asarray(c["st_mask"], F32), jnp.asarray(c["head_sum"], BF16)]
    st_spec = pl.BlockSpec((2, None, GLA_W, GLA_KW), lambda b, i: (0, b, 0, 0))
    full = lambda a: pl.BlockSpec(a.shape, lambda b, i: (0,) * a.ndim)
    zero_init = st0 is None
    nu = 2 * nck
    rows = (lambda b, i: rbi(0, b, i), lambda b, i: rbi(1, b, i))

    def blk(width, col, dd):
        return pl.BlockSpec((SUB, width), lambda b, i: (rows[dd](b, i), col // width))

    return pl.pallas_call(
        functools.partial(_gla_kernel, nchunk=nck, zero_init=zero_init),
        grid=(nseq, nblk),
        in_specs=[blk(2 * GLA_KW, P_GQK, 0), blk(2 * GLA_KW, P_GQK, 1), blk(GLA_W, P_GV, 0), blk(GLA_W, P_GV, 1),
                  blk(128, P_SMALL, 0), blk(128, P_SMALL + 128, 1), full(up_p), full(bias_p)]
                 + [full(a) for a in consts] + ([] if zero_init else [st_spec]),
        out_specs=[pl.BlockSpec((SUB, GLA_W), lambda b, i: (rows[0](b, i), 0)),
                   pl.BlockSpec((SUB, GLA_W), lambda b, i: (rows[1](b, i), 0)), st_spec],
        out_shape=[jax.ShapeDtypeStruct((t, GLA_W), F32), jax.ShapeDtypeStruct((t, GLA_W), F32),
                   jax.ShapeDtypeStruct((2, nseq, GLA_W, GLA_KW), F32)],
        scratch_shapes=[pltpu.VMEM((2, GLA_W, GLA_KW), F32), pltpu.VMEM((nu, CHUNK, GLA_KW), BF16),
                        pltpu.VMEM((nu, GLA_W, GLA_KW), F32), pltpu.VMEM((nu, 8, GLA_KW), F32)],
        compiler_params=_cparams(("arbitrary", "arbitrary")),
        name="gla",
    )(proj, proj, proj, proj, proj, proj, up_p, bias_p, *consts, *([] if zero_init else [st0]))


def _gdn_kernel(qkvf_ref, qkvb_ref, smf_ref, smb_ref, arow_ref, dtb_ref, tri_ref, incl_ref, strict_ref, lvl_ref,
                eye_ref, *rest, nchunk, zero_init):
    if zero_init:
        of_ref, ob_ref, sf_ref, s_ref, la_ref, be_ref, w_sc, u0_sc, qkd_sc, qe_sc, kdt_sc, egl_sc = rest
    else:
        s0_ref, of_ref, ob_ref, sf_ref, s_ref, la_ref, be_ref, w_sc, u0_sc, qkd_sc, qe_sc, kdt_sc, egl_sc = rest
    qkv_refs = (qkvf_ref, qkvb_ref)
    o_refs = (of_ref, ob_ref)

    @pl.when(pl.program_id(1) == 0)
    def _():
        s_ref[...] = jnp.zeros_like(s_ref) if zero_init else s0_ref[...]

    for dd, sm_ref in enumerate((smf_ref, smb_ref)):
        sm = sm_ref[...]
        la_ref[dd] = -jnp.exp(arow_ref[dd]) * _softplus(sm + dtb_ref[dd])
        be_ref[dd] = jax.nn.sigmoid(sm)

    eye = eye_ref[...]

    def stack(ref, r0, lo, width):
        return jnp.concatenate([ref[pl.ds(r0, CHUNK), lo + h * width:lo + (h + 1) * width]
                                for h in range(GDN_HEADS)], axis=0)

    def colstack(a, lo):
        return jnp.concatenate([a[:, lo + h:lo + h + 1] for h in range(GDN_HEADS)], axis=0)

    units = [(dd, c) for c in range(nchunk) for dd in range(2)]
    st1 = []
    for dd, c in units:
        r0 = c * CHUNK
        qkv_ref = qkv_refs[dd]
        qs = stack(qkv_ref, r0, 0, GDN_DK)
        ks = stack(qkv_ref, r0, GDN_W, GDN_DK)
        la = la_ref[dd, pl.ds(r0, CHUNK), :]
        g = _sel_dot(tri_ref[dd], la)
        ksb = ks.astype(BF16)
        st1.append((qs, ks, la, g, _bdot_nt(ksb, ksb), _bdot_nt(qs, ksb)))
    loc = []
    for (dd, c), (qs, ks, la, g, kk, qk) in zip(units, st1):
        r0 = c * CHUNK
        n = dd * nchunk + c
        vs = stack(qkv_refs[dd], r0, 2 * GDN_W, GDN_DV)
        gtot = jnp.sum(la, axis=0, keepdims=True)
        gcol = colstack(g, SM_A)
        kfcol = colstack(jnp.exp(gtot - g), SM_A)
        egcol = colstack(jnp.exp(g), SM_A)
        bcol = colstack(be_ref[dd, pl.ds(r0, CHUNK), :], SM_B)
        grow = jnp.sum(eye * gcol, axis=0, keepdims=True)
        dec = incl_ref[dd] * jnp.exp(jnp.minimum(gcol - grow, 0.0))
        lmat = strict_ref[dd] * dec * kk * bcol
        rhs = jnp.concatenate([ks * (bcol * egcol), vs * bcol], axis=1).astype(BF16)
        qkd_sc[n] = (qk * dec).astype(BF16)
        qe_sc[n] = (qs * egcol).astype(BF16)
        kd = ks * kfcol
        for h in range(GDN_HEADS):
            kdt_sc[n * GDN_HEADS + h] = kd[h * CHUNK:(h + 1) * CHUNK].T.astype(BF16)
        egl_sc[n] = jnp.broadcast_to(jnp.exp(gtot), (8, 128))
        loc.append((lmat.astype(BF16), rhs))
    tinv = [eye - (lm * lvl_ref[dd, 0]).astype(F32) for (dd, _), (lm, _) in zip(units, loc)]
    for lv in range(1, 6):
        tb = [t_.astype(BF16) for t_ in tinv]
        cs = [jnp.dot(lm * lvl_ref[dd, lv], t_, preferred_element_type=F32)
              for (dd, _), (lm, _), t_ in zip(units, loc, tb)]
        tinv = [t32 - jnp.dot(t_, c_.astype(BF16), preferred_element_type=F32)
                for t32, t_, c_ in zip(tinv, tb, cs)]
    for (dd, c), (_, rhs), t_ in zip(units, loc, tinv):
        sol = jnp.dot(t_.astype(BF16), rhs, preferred_element_type=F32)
        w_sc[dd * nchunk + c] = sol[:, :GDN_DK].astype(BF16)
        u0_sc[dd * nchunk + c] = sol[:, GDN_DK:]

    s_cur = [[s_ref[dd, h] for h in range(GDN_HEADS)] for dd in range(2)]
    for ci in range(nchunk):
        step = []
        for dd in range(2):
            c = ci if dd == 0 else nchunk - 1 - ci
            n = dd * nchunk + c
            sb = [s_.astype(BF16) for s_ in s_cur[dd]]
            ub = []
            for h in range(GDN_HEADS):
                rows = pl.ds(h * CHUNK, CHUNK)
                u = u0_sc[n, rows, :] - jnp.dot(w_sc[n, rows, :], sb[h], preferred_element_type=F32)
                ub.append(u.astype(BF16))
            step.append((c, n, sb, ub))
        for dd, (c, n, sb, ub) in enumerate(step):
            egl = egl_sc[n]
            for h in range(GDN_HEADS):
                s_cur[dd][h] = (egl[0:1, SM_A + h:SM_A + h + 1] * s_cur[dd][h]
                                + jnp.dot(kdt_sc[n * GDN_HEADS + h], ub[h], preferred_element_type=F32))
        for dd, (c, n, sb, ub) in enumerate(step):
            o2 = jnp.dot(qkd_sc[n], jnp.concatenate(ub, axis=0), preferred_element_type=F32)
            for h in range(GDN_HEADS):
                rows = pl.ds(h * CHUNK, CHUNK)
                o_refs[dd][pl.ds(c * CHUNK, CHUNK), h * GDN_DV:(h + 1) * GDN_DV] = (
                    jnp.dot(qe_sc[n, rows, :], sb[h], preferred_element_type=F32) + o2[h * CHUNK:(h + 1) * CHUNK])
    for dd in range(2):
        for h in range(GDN_HEADS):
            s_ref[dd, h] = s_cur[dd][h]
    sf_ref[...] = s_ref[...]


def _gdn(proj, arow, dtb, s0, *, nseq, n_seq):
    t = proj.shape[0]
    nblk = n_seq // SUB
    rbi = _row_block_index(nblk)
    c = _C
    consts = [jnp.asarray(c["incl"], BF16), jnp.asarray(c["incl_bd"], F32), jnp.asarray(c["strict_bd"], F32),
              jnp.asarray(c["levels"], BF16), jnp.asarray(c["eye256"], F32)]
    hs = (GDN_HEADS, GDN_DK, GDN_DV)
    nck = SUB // CHUNK
    nu = 2 * nck
    zero_init = s0 is None
    s_spec = pl.BlockSpec((2, None) + hs, lambda b, i: (0, b, 0, 0, 0))
    full = lambda shape: pl.BlockSpec(shape, lambda b, i: (0,) * len(shape))
    rows = (lambda b, i: rbi(0, b, i), lambda b, i: rbi(1, b, i))
    return pl.pallas_call(
        functools.partial(_gdn_kernel, nchunk=nck, zero_init=zero_init),
        grid=(nseq, nblk),
        in_specs=[pl.BlockSpec((SUB, 3 * GDN_W), lambda b, i: (rows[0](b, i), P_DQKV // (3 * GDN_W))),
                  pl.BlockSpec((SUB, 3 * GDN_W), lambda b, i: (rows[1](b, i), P_DQKV // (3 * GDN_W))),
                  pl.BlockSpec((SUB, 128), lambda b, i: (rows[0](b, i), P_SMALL // 128)),
                  pl.BlockSpec((SUB, 128), lambda b, i: (rows[1](b, i), P_SMALL // 128 + 1)),
                  full((2, 1, 128)), full((2, 1, 128)), full((2, CHUNK, CHUNK)),
                  full((2, 256, 256)), full((2, 256, 256)), full((2, 6, 256, 256)), full((256, 256))]
                 + ([] if zero_init else [s_spec]),
        out_specs=[pl.BlockSpec((SUB, GDN_W), lambda b, i: (rows[0](b, i), 0)),
                   pl.BlockSpec((SUB, GDN_W), lambda b, i: (rows[1](b, i), 0)), s_spec],
        out_shape=[jax.ShapeDtypeStruct((t, GDN_W), F32), jax.ShapeDtypeStruct((t, GDN_W), F32),
                   jax.ShapeDtypeStruct((2, nseq) + hs, F32)],
        scratch_shapes=[pltpu.VMEM((2,) + hs, F32),
                        pltpu.VMEM((2, SUB, 128), F32), pltpu.VMEM((2, SUB, 128), F32),
                        pltpu.VMEM((nu, 256, GDN_DK), BF16), pltpu.VMEM((nu, 256, GDN_DV), F32),
                        pltpu.VMEM((nu, 256, 256), BF16), pltpu.VMEM((nu, 256, GDN_DK), BF16),
                        pltpu.VMEM((nu * GDN_HEADS, GDN_DK, CHUNK), BF16), pltpu.VMEM((nu, 8, 128), F32)],
        compiler_params=_cparams(("arbitrary", "arbitrary")),
        name="gdn",
    )(proj, proj, proj, proj, arow, dtb, *consts, *([] if zero_init else [s0]))


def _cmul(ar, ai, br, bi):
    return ar * br - ai * bi, ar * bi + ai * br


def _split2(x):
    hi = x.astype(BF16)
    return hi, (x - hi.astype(F32)).astype(BF16)


def _s5_consts_kernel(lr_ref, li_ref, st_ref, br_ref, bi_ref, cr_ref, ci_ref, gm_ref, bm_ref,
                      laml_ref, wexp_ref, vexp_ref, tt_ref, w_ref, v_ref, a_ref, q_ref):
    d = pl.program_id(0) % 2
    lc = S5_L * S5_CH

    def trows(s):
        return pl.ds(pl.multiple_of((s + d * (S5_L - 1 - 2 * s)) * S5_CH, S5_CH), S5_CH)

    lr = lr_ref[...]
    li = li_ref[...]
    step = jnp.exp(st_ref[...])
    mag = jnp.exp(lr * step)
    lbr = mag * jnp.cos(li * step)
    lbi = mag * jnp.sin(li * step)
    nr = lbr - 1.0
    den = lr * lr + li * li
    fr = (nr * lr + lbi * li) / den
    fi = (lbi * lr - nr * li) / den
    bbr, bbi = _cmul(fr, fi, br_ref[...], bi_ref[...])
    cr = cr_ref[...]
    ci = ci_ref[...]
    pr = jnp.ones_like(lr)
    pi = jnp.zeros_like(lr)
    for s in range(S5_L):
        ar, ai = _cmul(cr, ci, pr, pi)
        a_ref[0, trows(s), :] = ar
        a_ref[1, trows(s), :] = ai
        wr, wi = _cmul(pr, pi, bbr, bbi)
        w_ref[0, trows(S5_L - 1 - s), :] = wr
        w_ref[1, trows(S5_L - 1 - s), :] = wi
        pr, pi = _cmul(pr, pi, lbr, lbi)
        vr, vi = _cmul(cr, ci, pr, pi)
        v_ref[0, trows(s), :] = vr
        v_ref[1, trows(s), :] = -vi
    laml_ref[0:1, :] = pr
    laml_ref[1:2, :] = pi

    lane = lax.broadcasted_iota(jnp.int32, (1, 128), 1)
    for gp in range(S5_PAIRS):
        lanes = slice(gp * 128, (gp + 1) * 128)
        for r in range(2):
            for src, dst in ((w_ref, wexp_ref), (v_ref, vexp_ref)):
                slab = src[r, :, lanes]
                dst[r, gp] = jnp.concatenate([jnp.where(lane < S5_P, slab, 0.0),
                                              jnp.where(lane >= S5_P, slab, 0.0)], axis=0).astype(BF16)

    gm = gm_ref[...]
    kall = None
    for r, b, sign in ((0, bbr, 1.0), (1, bbi, -1.0)):
        ah, al = _split2(a_ref[r])
        bh, bl = _split2(jnp.concatenate([b] * S5_GROUPS, axis=0) * gm)
        nt = lambda x, y: lax.dot_general(x, y, (((1,), (1,)), ((), ())), preferred_element_type=F32)
        term = (nt(ah, bh) + nt(ah, bl) + nt(al, bh)) * sign
        kall = term if kall is None else kall + term

    zeros = jnp.zeros((lc, S5_W), F32)
    for delta in range(S5_GROUPS):
        q_ref[delta, 0:lc, :] = zeros
        q_ref[delta, lc:2 * lc, :] = kall if delta == 0 else pltpu.roll(kall, delta * S5_CH, axis=1)
        q_ref[delta, 2 * lc:3 * lc, :] = zeros
    for g in range(S5_GROUPS):
        acc = zeros
        for s in range(S5_L):
            start = pl.multiple_of(lc - S5_CH * s + d * (S5_CH * (S5_L - 1)), S5_CH)
            blk = q_ref[(s - g) % S5_GROUPS, pl.ds(start, lc), :]
            acc = jnp.where(bm_ref[s:s + 1, :] > 0.5, blk, acc)
        tt_ref[g] = acc.astype(BF16)


def _s5_consts(lam_re, lam_im, log_step, b_re, b_im, c_re, c_im):
    n = DEPTH * 2
    lc = S5_L * S5_CH
    lr = lam_re.reshape(n, 1, S5_N)
    li = lam_im.reshape(n, 1, S5_N)
    st = jnp.repeat(log_step.reshape(n, S5_GROUPS), S5_P, axis=-1).reshape(n, 1, S5_N)
    br = b_re.reshape(n, S5_N, S5_CH).transpose(0, 2, 1)
    bi = b_im.reshape(n, S5_N, S5_CH).transpose(0, 2, 1)
    cr = c_re.reshape(n, S5_GROUPS, S5_CH, S5_P).transpose(0, 2, 1, 3).reshape(n, S5_CH, S5_N)
    ci = c_im.reshape(n, S5_GROUPS, S5_CH, S5_P).transpose(0, 2, 1, 3).reshape(n, S5_CH, S5_N)
    gm = jnp.asarray(np.kron(np.eye(S5_GROUPS), np.ones((S5_CH, S5_P))), F32)
    row = pl.BlockSpec((None, 1, S5_N), lambda i: (i, 0, 0))
    mat = pl.BlockSpec((None, S5_CH, S5_N), lambda i: (i, 0, 0))
    exp = pl.BlockSpec((None, 2, S5_PAIRS, 2 * lc, 128), lambda i: (i, 0, 0, 0, 0))
    laml, wexp, vexp, tt = pl.pallas_call(
        _s5_consts_kernel,
        grid=(n,),
        in_specs=[row, row, row, mat, mat, mat, mat, pl.BlockSpec((S5_W, S5_N), lambda i: (0, 0)),
                  pl.BlockSpec((S5_GROUPS, S5_W), lambda i: (0, 0))],
        out_specs=[pl.BlockSpec((None, 2, S5_N), lambda i: (i, 0, 0)), exp, exp,
                   pl.BlockSpec((None, S5_GROUPS, lc, lc), lambda i: (i, 0, 0, 0))],
        out_shape=[jax.ShapeDtypeStruct((n, 2, S5_N), F32),
                   jax.ShapeDtypeStruct((n, 2, S5_PAIRS, 2 * lc, 128), BF16),
                   jax.ShapeDtypeStruct((n, 2, S5_PAIRS, 2 * lc, 128), BF16),
                   jax.ShapeDtypeStruct((n, S5_GROUPS, lc, lc), BF16)],
        scratch_shapes=[pltpu.VMEM((2, lc, S5_N), F32), pltpu.VMEM((2, lc, S5_N), F32),
                        pltpu.VMEM((2, lc, S5_N), F32), pltpu.VMEM((S5_GROUPS, 3 * lc, S5_W), F32)],
        compiler_params=_cparams(("arbitrary",)),
        name="s5_consts",
    )(lr, li, st, br, bi, cr, ci, gm, _lane_block_masks())
    return laml.reshape(n, 2, S5_PAIRS, 128), wexp, vexp, tt


def _lane_block_masks():
    return jnp.asarray(np.kron(np.eye(S5_GROUPS), np.ones((1, S5_CH))), F32)


def _nt(a, b):
    return lax.dot_general(a, b, (((1,), (1,)), ((), ())), preferred_element_type=F32)


def _lane_block_transpose(arrs):
    n = len(arrs)
    width = arrs[0].shape[1]
    blk = lax.broadcasted_iota(jnp.int32, (1, width), 1) // (width // n)
    arrs = list(arrs)
    h = n // 2
    while h >= 1:
        hi = (blk & h) != 0
        for a in range(n):
            if a & h:
                continue
            lo_arr, hi_arr = arrs[a], arrs[a + h]
            arrs[a] = jnp.where(hi, pltpu.roll(hi_arr, h * (width // n), axis=1), lo_arr)
            arrs[a + h] = jnp.where(hi, hi_arr, pltpu.roll(lo_arr, width - h * (width // n), axis=1))
        h //= 2
    return arrs


def _s5_kernel(u0_ref, u1_ref, tt_ref, wexp_ref, vexp_ref, laml_ref, *rest, nseq, nblk, zero_init):
    if zero_init:
        y_ref, hf_ref, ug_sc, yg_sc, hin_re, hin_im, hp_re, hp_im = rest
    else:
        h0_ref, y_ref, hf_ref, ug_sc, yg_sc, hin_re, hin_im, hp_re, hp_im = rest
    d = pl.program_id(0)
    r = nblk * nseq
    lc = S5_L * S5_CH

    @pl.when(d == 0)
    def _():
        xs = []
        for s in range(S5_L):
            rows = pl.ds(s, r, stride=S5_L)
            xs.append(jnp.concatenate([u0_ref[rows, :], u1_ref[rows, :]], axis=1))
        for g, ug in enumerate(_lane_block_transpose(xs)):
            ug_sc[g] = ug.astype(BF16)

    for gp in range(S5_PAIRS):
        up = jnp.concatenate([ug_sc[2 * gp], ug_sc[2 * gp + 1]], axis=1)
        rows = pl.ds(gp, r, stride=S5_PAIRS)
        hin_re[rows, :] = jnp.dot(up, wexp_ref[0, gp], preferred_element_type=F32)
        hin_im[rows, :] = jnp.dot(up, wexp_ref[1, gp], preferred_element_type=F32)

    lr = laml_ref[0]
    li = laml_ref[1]

    def step(kk, carry):
        k = kk + d * (nblk - 1 - 2 * kk)
        new = []
        for b in range(nseq):
            rows = pl.ds(pl.multiple_of((b * nblk + k) * S5_PAIRS, S5_PAIRS), S5_PAIRS)
            hr, hi = carry[2 * b], carry[2 * b + 1]
            hp_re[rows, :] = hr
            hp_im[rows, :] = hi
            new += [lr * hr - li * hi + hin_re[rows, :], lr * hi + li * hr + hin_im[rows, :]]
        return tuple(new)

    init = []
    for b in range(nseq):
        rows = slice(b * S5_PAIRS, (b + 1) * S5_PAIRS)
        init += ([jnp.zeros((S5_PAIRS, 128), F32)] * 2 if zero_init else [h0_ref[0, rows, :], h0_ref[1, rows, :]])
    fin = lax.fori_loop(0, nblk, step, tuple(init))
    for b in range(nseq):
        rows = slice(b * S5_PAIRS, (b + 1) * S5_PAIRS)
        hf_ref[0, rows, :] = fin[2 * b]
        hf_ref[1, rows, :] = fin[2 * b + 1]

    for gp in range(S5_PAIRS):
        rows = pl.ds(gp, r, stride=S5_PAIRS)
        ys = (_nt(hp_re[rows, :].astype(BF16), vexp_ref[0, gp])
              + _nt(hp_im[rows, :].astype(BF16), vexp_ref[1, gp]))
        for j in range(2):
            g = 2 * gp + j
            yg = _nt(ug_sc[g], tt_ref[g]) + ys[:, j * lc:(j + 1) * lc]

            @pl.when(d == 0)
            def _():
                yg_sc[g] = yg

            @pl.when(d == 1)
            def _():
                yg_sc[g] += yg

    @pl.when(d == 1)
    def _():
        for t, yt in enumerate(_lane_block_transpose([yg_sc[g] for g in range(S5_GROUPS)])):
            rows = pl.ds(t, r, stride=S5_L)
            y_ref[0, rows, :] = yt[:, 0:128]
            y_ref[1, rows, :] = yt[:, 128:2 * 128]


def _s5(proj, consts, h0, *, layer, nseq, n_seq):
    laml, wexp, vexp, tt = consts
    t = proj.shape[0]
    nblk = n_seq // S5_L
    r = nblk * nseq
    lc = S5_L * S5_CH
    bs = nseq * S5_PAIRS
    zero_init = h0 is None
    h_spec = pl.BlockSpec((None, 2, bs, 128), lambda d: (d, 0, 0, 0))
    return pl.pallas_call(
        functools.partial(_s5_kernel, nseq=nseq, nblk=nblk, zero_init=zero_init),
        grid=(2,),
        in_specs=[pl.BlockSpec((t, 128), lambda d: (0, P_SU // 128)),
                  pl.BlockSpec((t, 128), lambda d: (0, P_SU // 128 + 1)),
                  pl.BlockSpec((None, S5_GROUPS, lc, lc), lambda d: (2 * layer + d, 0, 0, 0)),
                  pl.BlockSpec((None, 2, S5_PAIRS, 2 * lc, 128), lambda d: (2 * layer + d, 0, 0, 0, 0)),
                  pl.BlockSpec((None, 2, S5_PAIRS, 2 * lc, 128), lambda d: (2 * layer + d, 0, 0, 0, 0)),
                  pl.BlockSpec((None, 2, S5_PAIRS, 128), lambda d: (2 * layer + d, 0, 0, 0))]
                 + ([] if zero_init else [h_spec]),
        out_specs=[pl.BlockSpec((2, t, 128), lambda d: (0, 0, 0)), h_spec],
        out_shape=[jax.ShapeDtypeStruct((2, t, 128), F32),
                   jax.ShapeDtypeStruct((2, 2, bs, 128), F32)],
        scratch_shapes=[pltpu.VMEM((S5_GROUPS, r, lc), BF16), pltpu.VMEM((S5_GROUPS, r, lc), F32)]
                       + [pltpu.VMEM((r * S5_PAIRS, 128), F32) for _ in range(4)],
        compiler_params=_cparams(("arbitrary",)),
        name="s5",
    )(proj, proj, tt, wexp, vexp, laml, *([] if zero_init else [h0]))


def _out_kernel(x_ref, mod_ref, ogf_ref, ogb_ref, odf_ref, odb_ref, ysf_ref, ysb_ref, gog_ref, su_ref, dog_ref,
                ind_ref, gw_ref, dw_ref, sd_ref, wglu_ref, bglu_ref, wout32_ref, o_ref, wout_ref):
    @pl.when(pl.program_id(0) == 0)
    def _():
        wout_ref[...] = wout32_ref[...].astype(BF16)

    og = ogf_ref[...] + ogb_ref[...]
    hi, mid, _ = _split3(og * og)
    ms = jnp.dot(hi, ind_ref[...], preferred_element_type=F32) + jnp.dot(mid, ind_ref[...], preferred_element_type=F32)
    og = og * lax.rsqrt(ms + EPS) * gw_ref[...] * _silu(gog_ref[...])
    od = odf_ref[...] + odb_ref[...]
    dog = dog_ref[...]
    parts = []
    for h in range(GDN_HEADS):
        z = od[:, h * GDN_DV:(h + 1) * GDN_DV]
        z = z * lax.rsqrt(jnp.mean(z * z, axis=-1, keepdims=True) + EPS) * dw_ref[...]
        parts.append(z * _silu(dog[:, h * GDN_DV:(h + 1) * GDN_DV]))
    y = su_ref[...] * sd_ref[...] + jnp.concatenate([ysf_ref[...], ysb_ref[...]], axis=1)
    g = 0.5 * y * (1.0 + jnp.tanh(math.sqrt(2.0 / math.pi) * (y + 0.044715 * (y * y * y))))
    os_ = g * jax.nn.sigmoid(_bdot(g, wglu_ref[...]) + bglu_ref[...])
    out = _bdot(og, wout_ref[0:GLA_W, :]) + _bdot(os_, wout_ref[GLA_W + GDN_W:MIX_W, :])
    for h in range(GDN_HEADS):
        lo = GLA_W + h * GDN_DV
        out = out + _bdot(parts[h], wout_ref[lo:lo + GDN_DV, :])
    o_ref[...] = x_ref[...] + mod_ref[5:6, :] * out


def _mix_out(x, mods, proj, og, od, ys, gla_nw, gdn_nw, s5_d, w_glu, b_glu, w_out, *, layer, n_seq):
    t, d = x.shape
    nb = mods.shape[0]
    tm = _pick_tile(n_seq if nb > 1 else t, 512)
    midx = _mod_index(nb, tm, n_seq)
    ys2 = ys

    def dirspec(w, dd):
        return pl.BlockSpec((None, tm, w), lambda i: (dd, i, 0))

    def vec(w):
        return pl.BlockSpec((None, 1, w), lambda i: (layer, 0, 0))

    return pl.pallas_call(
        _out_kernel,
        grid=(t // tm,),
        in_specs=[pl.BlockSpec((tm, d), lambda i: (i, 0)),
                  pl.BlockSpec((None, N_MOD, d), lambda i: (midx(i), 0, 0)),
                  pl.BlockSpec((tm, GLA_W), lambda i: (i, 0)), pl.BlockSpec((tm, GLA_W), lambda i: (i, 0)),
                  pl.BlockSpec((tm, GDN_W), lambda i: (i, 0)), pl.BlockSpec((tm, GDN_W), lambda i: (i, 0)),
                  dirspec(128, 0), dirspec(128, 1),
                  pl.BlockSpec((tm, GLA_W), lambda i: (i, P_GOG // GLA_W)),
                  pl.BlockSpec((tm, S5_W), lambda i: (i, P_SU // S5_W)),
                  pl.BlockSpec((tm, GDN_W), lambda i: (i, P_DOG // GDN_W)),
                  pl.BlockSpec((GLA_W, GLA_W), lambda i: (0, 0)),
                  vec(GLA_W), vec(GDN_DV), vec(S5_W),
                  pl.BlockSpec((None, S5_W, S5_W), lambda i: (layer, 0, 0)),
                  vec(S5_W),
                  pl.BlockSpec((None, MIX_W, d), lambda i: (layer, 0, 0), pipeline_mode=pl.Buffered(1))],
        out_specs=pl.BlockSpec((tm, d), lambda i: (i, 0)),
        out_shape=jax.ShapeDtypeStruct((t, d), F32),
        scratch_shapes=[pltpu.VMEM((MIX_W, d), BF16)],
        compiler_params=_cparams(("arbitrary",)),
        name="mix_out",
    )(x, mods, og[0], og[1], od[0], od[1], ys2, ys2, proj, proj, proj,
      jnp.asarray(_C["ind64"], BF16),
      jnp.tile(gla_nw, (1, GLA_HEADS)).reshape(DEPTH, 1, GLA_W), gdn_nw.reshape(DEPTH, 1, GDN_DV),
      s5_d.reshape(DEPTH, 1, S5_W), w_glu, b_glu.reshape(DEPTH, 1, S5_W), w_out)


def _gla_params(gk_up, gk_bias, layer):
    up = jnp.zeros((2, 128, GLA_KW), F32).at[:, :GLA_RANK, :].set(gk_up[layer])
    return up, gk_bias[layer].reshape(2, 1, GLA_KW)


def _gdn_params(a_log, dt_bias, layer):
    arow = jnp.zeros((2, 1, 128), F32).at[:, 0, SM_A:SM_A + GDN_HEADS].set(a_log[layer])
    dtb = jnp.zeros((2, 1, 128), F32).at[:, 0, SM_A:SM_A + GDN_HEADS].set(dt_bias[layer])
    return arow, dtb


def _gla_state_in(s):
    eye = jnp.eye(GLA_HEADS, dtype=F32)
    st = jnp.einsum('bzhde,hg->zbhegd', s.astype(F32), eye)
    return st.reshape(2, s.shape[0], GLA_W, GLA_KW)


def _gla_state_out(st):
    b = st.shape[1]
    s = st.reshape(2, b, GLA_HEADS, GLA_DV, GLA_HEADS, GLA_DK)
    s = jnp.stack([s[:, :, h, :, h, :] for h in range(GLA_HEADS)], axis=2)
    return s.transpose(1, 0, 2, 4, 3)


def _trunk(x3, mods, rows, states, p, s5c, w_in):
    nseq, n_seq, d = x3.shape
    x = x3.reshape(nseq * n_seq, d)
    fin = []
    for l in range(DEPTH):
        m = mods[l]
        x = _ffn(x, m, p['norm_w'], p['ffn_w_gate'], p['ffn_w_up'], p['ffn_w_down'], p['final_norm_w'],
                 layer=l, which=0, n_seq=n_seq, final=False)
        proj = _proj(x, m, p['norm_w'], w_in, p['gdn_conv_w'], layer=l, n_seq=n_seq, rowlen=n_seq // rows)
        if states is None:
            st0 = s0 = h0 = None
        else:
            st_gla, st_gdn, st_re, st_im = states
            st0 = _gla_state_in(st_gla[:, l])
            s0 = st_gdn[:, l].transpose(1, 0, 2, 3, 4)
            h0 = jnp.stack([st_re[:, l].reshape(nseq, 2, S5_N), st_im[:, l].reshape(nseq, 2, S5_N)], axis=0)
            h0 = h0.transpose(2, 0, 1, 3).reshape(2, 2, nseq * S5_PAIRS, 128)
        up_p, bias_p = _gla_params(p['gla_gk_up'], p['gla_gk_bias'], l)
        og_f, og_b, gla_f = _gla(proj, up_p, bias_p, st0, nseq=nseq, n_seq=n_seq)
        og = (og_f, og_b)
        arow, dtb = _gdn_params(p['gdn_a_log'], p['gdn_dt_bias'], l)
        od_f, od_b, gdn_f = _gdn(proj, arow, dtb, s0, nseq=nseq, n_seq=n_seq)
        od = (od_f, od_b)
        ys, s5_f = _s5(proj, s5c, h0, layer=l, nseq=nseq, n_seq=n_seq)
        x = _mix_out(x, m, proj, og, od, ys, p['gla_norm_w'], p['gdn_norm_w'], p['s5_d'], p['s5_w_glu'],
                     p['s5_b_glu'], p['w_out'], layer=l, n_seq=n_seq)
        x = _ffn(x, m, p['norm_w'], p['ffn_w_gate'], p['ffn_w_up'], p['ffn_w_down'], p['final_norm_w'],
                 layer=l, which=1, n_seq=n_seq, final=(l == DEPTH - 1))
        fin.append((gla_f, gdn_f, s5_f))
    return x.reshape(nseq, n_seq, d), fin


def kernel(x_prompt, x_sample, c, state_gla, state_gdn, state_s5_re, state_s5_im, c_ctx, w_ada, b_ada, norm_w, ffn_w_gate, ffn_w_up, ffn_w_down, w_in, gla_gk_up, gla_gk_bias, gla_norm_w, gdn_conv_w, gdn_a_log, gdn_dt_bias, gdn_norm_w, s5_lam_re, s5_lam_im, s5_log_step, s5_b_re, s5_b_im, s5_c_re, s5_c_im, s5_d, s5_w_glu, s5_b_glu, w_out, final_norm_w):
    p = dict(norm_w=norm_w, ffn_w_gate=ffn_w_gate, ffn_w_up=ffn_w_up, ffn_w_down=ffn_w_down,
             gla_gk_up=gla_gk_up, gla_gk_bias=gla_gk_bias, gla_norm_w=gla_norm_w, gdn_conv_w=gdn_conv_w,
             gdn_a_log=gdn_a_log, gdn_dt_bias=gdn_dt_bias, gdn_norm_w=gdn_norm_w, s5_c_re=s5_c_re,
             s5_c_im=s5_c_im, s5_d=s5_d, s5_w_glu=s5_w_glu, s5_b_glu=s5_b_glu, w_out=w_out,
             final_norm_w=final_norm_w)
    b_ctx = x_prompt.shape[0]
    b_dec = x_sample.shape[0]
    cond8 = jnp.zeros((8, D_MODEL), F32).at[0].set(c_ctx).at[1:1 + b_dec].set(c)
    mods = _ada(cond8, w_ada, b_ada)
    mods_ctx = mods[:, 0:1].reshape(DEPTH, 1, N_MOD, D_MODEL)
    mods_dec = mods[:, 1:1 + b_dec].reshape(DEPTH, b_dec, N_MOD, D_MODEL)
    s5c = _s5_consts(s5_lam_re, s5_lam_im, s5_log_step, s5_b_re, s5_b_im, s5_c_re, s5_c_im)

    y_prompt, fin = _trunk(x_prompt, mods_ctx, 1, None, p, s5c, w_in)
    new_gla = jnp.stack([_gla_state_out(f[0]) for f in fin], axis=1)
    new_gdn = jnp.stack([f[1].transpose(1, 0, 2, 3, 4) for f in fin], axis=1)
    s5f = jnp.stack([f[2].reshape(2, 2, b_ctx, S5_GROUPS, S5_P).transpose(1, 2, 0, 3, 4) for f in fin], axis=2)
    new_re, new_im = s5f[0], s5f[1]

    rows = x_sample.shape[1] // 64
    y_sample, _ = _trunk(x_sample, mods_dec, rows, (state_gla, state_gdn, state_s5_re, state_s5_im),
                         p, s5c, w_in)
    return (y_prompt, y_sample, new_gla.astype(state_gla.dtype), new_gdn.astype(state_gdn.dtype),
            new_re.astype(state_s5_re.dtype), new_im.astype(state_s5_im.dtype))
```

```python
import functools
import math

import numpy as np
import jax
import jax.numpy as jnp
from jax import lax
from jax.experimental import pallas as pl
from jax.experimental.pallas import tpu as pltpu

F32 = jnp.float32
BF16 = jnp.bfloat16

D_MODEL = 1024
DEPTH = 2
FFN_DIM = 2816
N_MOD = 9
EPS = 1e-6
CHUNK = 64
GLA_HEADS, GLA_DK, GLA_DV, GLA_RANK = 4, 32, 64, 16
GLA_GATE_NORM = 16.0
GDN_HEADS, GDN_DK, GDN_DV, GDN_CONV = 4, 128, 128, 5
S5_GROUPS, S5_CH, S5_P = 16, 16, 64
GLA_W = GLA_HEADS * GLA_DV
GLA_KW = GLA_HEADS * GLA_DK
GDN_W = GDN_HEADS * GDN_DV
S5_W = S5_GROUPS * S5_CH
S5_N = S5_GROUPS * S5_P
S5_L = 16
S5_PAIRS = S5_GROUPS // 2
MIX_W = GLA_W + GDN_W + S5_W

_IN_SIZES = (GLA_KW, GLA_KW, GLA_W, GLA_RANK, GLA_RANK, GLA_W,
             GDN_W, GDN_W, GDN_W, GDN_HEADS, GDN_HEADS, GDN_HEADS, GDN_HEADS, GDN_W, S5_W)
_IN_OFF = tuple(int(v) for v in np.cumsum((0,) + _IN_SIZES))
(_GQ, _GK, _GV, _GLRF, _GLRB, _GOG, _DQ, _DK, _DV, _DAF, _DAB, _DBF, _DBB, _DOG, _SU) = range(15)

P_DQKV = 0
P_SMALL = 1536
P_GQK = 1792
P_GV = 2048
P_GOG = 2304
P_DOG = 2560
P_SU = 3072
PROJ_W = 3328
PROJ_TN = PROJ_W // 2
SM_A = GLA_RANK
SM_B = GLA_RANK + GDN_HEADS
SUB = 256
VMEM_LIMIT = 56 * 1024 * 1024


def _cparams(sem):
    return pltpu.CompilerParams(dimension_semantics=sem, vmem_limit_bytes=VMEM_LIMIT)


def _bdot(a, b):
    return jnp.dot(a.astype(BF16), b.astype(BF16), preferred_element_type=F32)


def _bdot_nt(a, b):
    return lax.dot_general(a.astype(BF16), b.astype(BF16), (((1,), (1,)), ((), ())),
                           preferred_element_type=F32)


def _split3(x):
    hi = x.astype(BF16)
    r1 = x - hi.astype(F32)
    mid = r1.astype(BF16)
    lo = (r1 - mid.astype(F32)).astype(BF16)
    return hi, mid, lo


def _sel_dot(m01, x):
    hi, mid, lo = _split3(x)
    return (jnp.dot(m01, hi, preferred_element_type=F32) + jnp.dot(m01, mid, preferred_element_type=F32)
            + jnp.dot(m01, lo, preferred_element_type=F32))


def _silu(x):
    return x * jax.nn.sigmoid(x)


def _softplus(x):
    return jnp.maximum(x, 0.0) + jnp.log1p(jnp.exp(-jnp.abs(x)))


def _log_sigmoid(x):
    return -_softplus(-x)


def _norm_mod(x, nw, scale, shift):
    y = x * lax.rsqrt(jnp.mean(x * x, axis=-1, keepdims=True) + EPS)
    return y * nw * (1.0 + scale) + shift


def _pick_tile(n, cap):
    t = cap
    while n % t:
        t //= 2
    return t


def _ada_kernel(c_ref, w_ref, b_ref, o_ref):
    o_ref[...] = _bdot(_silu(c_ref[...]), w_ref[...]) + b_ref[...]


def _ada(cond8, w_ada, b_ada):
    depth, d, nout = w_ada.shape
    tn = 1024
    return pl.pallas_call(
        _ada_kernel,
        grid=(depth, nout // tn),
        in_specs=[pl.BlockSpec((8, d), lambda l, j: (0, 0)),
                  pl.BlockSpec((None, d, tn), lambda l, j: (l, 0, j)),
                  pl.BlockSpec((None, 1, tn), lambda l, j: (l, 0, j))],
        out_specs=pl.BlockSpec((None, 8, tn), lambda l, j: (l, 0, j)),
        out_shape=jax.ShapeDtypeStruct((depth, 8, nout), F32),
        compiler_params=_cparams(("parallel", "parallel")),
        name="ada",
    )(cond8, w_ada, b_ada.reshape(depth, 1, nout))


def _ffn_kernel(x_ref, mod_ref, nw_ref, wg_ref, wu_ref, wd_ref, fw_ref, o_ref, h_ref, *, mod_base, final):
    j = pl.program_id(1)

    @pl.when(j == 0)
    def _():
        m = mod_ref[...]
        h = _norm_mod(x_ref[...], nw_ref[...], m[mod_base + 1:mod_base + 2], m[mod_base:mod_base + 1])
        h_ref[...] = h.astype(BF16)
        o_ref[...] = jnp.zeros_like(o_ref)

    h = h_ref[...]
    g = jnp.dot(h, wg_ref[...].astype(BF16), preferred_element_type=F32)
    u = jnp.dot(h, wu_ref[...].astype(BF16), preferred_element_type=F32)
    o_ref[...] += _bdot(_silu(g) * u, wd_ref[...])

    @pl.when(j == pl.num_programs(1) - 1)
    def _():
        m = mod_ref[...]
        y = x_ref[...] + 0.5 * m[mod_base + 2:mod_base + 3] * o_ref[...]
        if final:
            y = y * lax.rsqrt(jnp.mean(y * y, axis=-1, keepdims=True) + EPS) * fw_ref[...]
        o_ref[...] = y


def _mod_index(nb, tm, n_seq):
    if nb == 1:
        return lambda i: 0
    return lambda i: (i * tm) // n_seq


def _ffn(x, mods, norm_w, w_gate, w_up, w_down, final_w, *, layer, which, n_seq, final):
    t, d = x.shape
    f = w_gate.shape[-1]
    nb = mods.shape[0]
    tm = _pick_tile(n_seq if nb > 1 else t, 2048)
    tf = 256
    midx = _mod_index(nb, tm, n_seq)
    mod_base = 0 if which == 0 else 6
    return pl.pallas_call(
        functools.partial(_ffn_kernel, mod_base=mod_base, final=final),
        grid=(t // tm, f // tf),
        in_specs=[pl.BlockSpec((tm, d), lambda i, j: (i, 0)),
                  pl.BlockSpec((None, N_MOD, d), lambda i, j: (midx(i), 0, 0)),
                  pl.BlockSpec((None, None, 1, d), lambda i, j: (layer, 2 * which, 0, 0)),
                  pl.BlockSpec((None, None, d, tf), lambda i, j: (layer, which, 0, j)),
                  pl.BlockSpec((None, None, d, tf), lambda i, j: (layer, which, 0, j)),
                  pl.BlockSpec((None, None, tf, d), lambda i, j: (layer, which, j, 0)),
                  pl.BlockSpec((1, d), lambda i, j: (0, 0))],
        out_specs=pl.BlockSpec((tm, d), lambda i, j: (i, 0)),
        out_shape=jax.ShapeDtypeStruct((t, d), F32),
        scratch_shapes=[pltpu.VMEM((tm, d), BF16)],
        compiler_params=_cparams(("parallel", "arbitrary")),
        name="ffn",
    )(x, mods, norm_w.reshape(DEPTH, 3, 1, d), w_gate, w_up, w_down, final_w.reshape(1, d))


def _proj_kernel(x_ref, mod_ref, nw_ref, w_ref, cw_ref, o_ref, h_ref, wb_ref, *, rowlen):
    j = pl.program_id(0)
    i = pl.program_id(1)

    for jt, segs in enumerate(_W_SEGS):
        @pl.when(jnp.logical_and(i == 0, j == jt))
        def _():
            for lo, src, width in segs:
                if src is None:
                    wb_ref[:, lo:lo + width] = jnp.zeros((wb_ref.shape[0], width), BF16)
                else:
                    wb_ref[:, lo:lo + width] = w_ref[:, src:src + width].astype(BF16)

    @pl.when(j == 0)
    def _():
        m = mod_ref[...]
        h_ref[i] = _norm_mod(x_ref[...], nw_ref[...], m[4:5], m[3:4]).astype(BF16)
        o_ref[...] = jnp.dot(h_ref[i], wb_ref[...], preferred_element_type=F32)
        tm = o_ref.shape[0]
        pos = lax.broadcasted_iota(jnp.int32, (tm, 1), 0) % rowlen
        valid = [jnp.logical_and(pos + (t - GDN_CONV // 2) >= 0, pos + (t - GDN_CONV // 2) < rowlen)
                 for t in range(GDN_CONV)]
        for g in range(3 * GDN_HEADS):
            cols = slice(P_DQKV + g * GDN_DK, P_DQKV + (g + 1) * GDN_DK)
            xg = o_ref[:, cols]
            acc = xg * cw_ref[GDN_CONV // 2:GDN_CONV // 2 + 1, cols]
            for t in range(GDN_CONV):
                s = t - GDN_CONV // 2
                if s != 0:
                    xs = pltpu.roll(xg, (-s) % tm, axis=0)
                    acc = acc + jnp.where(valid[t], xs, 0.0) * cw_ref[t:t + 1, cols]
            y = _silu(acc)
            if g < 2 * GDN_HEADS:
                y = y * lax.rsqrt(jnp.sum(y * y, axis=-1, keepdims=True) + EPS)
                if g < GDN_HEADS:
                    y = y * (GDN_DK ** -0.5)
            o_ref[:, cols] = y

    @pl.when(j == 1)
    def _():
        o_ref[...] = jnp.dot(h_ref[i], wb_ref[...], preferred_element_type=F32)


def _proj(x, mods, norm_w, w_in, conv_w, *, layer, n_seq, rowlen):
    t, d = x.shape
    nb = mods.shape[0]
    tm = _pick_tile(n_seq if nb > 1 else t, 512)
    midx = _mod_index(nb, tm, n_seq)
    return pl.pallas_call(
        functools.partial(_proj_kernel, rowlen=rowlen),
        grid=(PROJ_W // PROJ_TN, t // tm),
        in_specs=[pl.BlockSpec((tm, d), lambda j, i: (i * (1 - j), 0)),
                  pl.BlockSpec((None, N_MOD, d), lambda j, i: (midx(i), 0, 0)),
                  pl.BlockSpec((None, None, 1, d), lambda j, i: (layer, 1, 0, 0)),
                  pl.BlockSpec((None, d, w_in.shape[-1]), lambda j, i: (layer, 0, 0), pipeline_mode=pl.Buffered(1)),
                  pl.BlockSpec((None, GDN_CONV, 3 * GDN_W), lambda j, i: (layer, 0, 0))],
        out_specs=pl.BlockSpec((tm, PROJ_TN), lambda j, i: (i, j)),
        out_shape=jax.ShapeDtypeStruct((t, PROJ_W), F32),
        scratch_shapes=[pltpu.VMEM((t // tm, tm, d), BF16), pltpu.VMEM((d, PROJ_TN), BF16)],
        compiler_params=_cparams(("arbitrary", "arbitrary")),
        name="proj",
    )(x, mods, norm_w.reshape(DEPTH, 3, 1, d), w_in, conv_w)


def _w_in_segments():
    order = [_DQ, _DK, _DV, _GLRF, _DAF, _DBF, None, _GLRB, _DAB, _DBB, None,
             _GQ, _GK, _GV, _GOG, _DOG, _SU]
    sm_pad = 128 - GLA_RANK - 2 * GDN_HEADS
    tiles = [[] for _ in range(PROJ_W // PROJ_TN)]
    dst = 0
    for s in order:
        width = sm_pad if s is None else _IN_SIZES[s]
        src = None if s is None else _IN_OFF[s]
        tile = tiles[dst // PROJ_TN]
        lo = dst % PROJ_TN
        if tile and src is not None and tile[-1][1] is not None and (
                tile[-1][0] + tile[-1][2] == lo and tile[-1][1] + tile[-1][2] == src):
            tile[-1] = (tile[-1][0], tile[-1][1], tile[-1][2] + width)
        else:
            tile.append((lo, src, width))
        dst += width
        assert (dst - 1) // PROJ_TN == (dst - width) // PROJ_TN, "a segment may not straddle column tiles"
    assert dst == PROJ_W
    return tiles


_W_SEGS = _w_in_segments()


def _np_consts():
    t = np.arange(CHUNK)
    lower = (t[:, None] >= t[None, :])
    incl = np.stack([lower, lower.T])
    strict = np.stack([t[:, None] > t[None, :], t[:, None] < t[None, :]])
    eye4 = np.eye(4, dtype=bool)
    blk = np.kron(eye4, np.ones((CHUNK, CHUNK), bool))
    incl_bd = np.stack([np.kron(eye4, incl[d]) for d in range(2)])
    strict_bd = np.stack([np.kron(eye4, strict[d]) for d in range(2)])
    levels = []
    for d in range(2):
        per = []
        for lv in range(6):
            s = 1 << lv
            same = (t[:, None] // (2 * s)) == (t[None, :] // (2 * s))
            hi = (t % (2 * s)) >= s
            m = same & hi[:, None] & (~hi)[None, :]
            if d == 1:
                m = m.T
            per.append(np.kron(eye4, m))
        levels.append(np.stack(per))
    levels = np.stack(levels)
    assert (levels.sum(1) == strict_bd).all() and blk.any()
    gla_mask = np.stack([np.tile(incl[d], (1, 4)) for d in range(2)])
    hm_k = np.kron(eye4, np.ones((1, GLA_DK)))
    hm_v = np.kron(eye4, np.ones((1, GLA_DV)))
    st_mask = np.kron(eye4, np.ones((GLA_DV, GLA_DK)))
    ind64 = np.kron(eye4, np.full((GLA_DV, GLA_DV), 1.0 / GLA_DV))
    gla_sel, gla_later, gla_earlier, gla_pair = [], [], [], []
    for d in range(2):
        pos = t if d == 0 else CHUNK - 1 - t
        cum = (pos[None, :] <= pos[:, None]).astype(np.float64)
        sel, later, earlier, pair = [cum], [], [], []
        for lv in range(6):
            s = 1 << lv
            ref_pos = (pos // (2 * s)) * (2 * s) + s - 1
            pick = (pos[None, :] == ref_pos[:, None]).astype(np.float64)
            sel.append(pick @ cum)
            is_later = (pos % (2 * s)) >= s
            same = (pos[:, None] // (2 * s)) == (pos[None, :] // (2 * s))
            later.append(np.repeat(is_later[:, None], GLA_KW, axis=1))
            earlier.append(np.repeat(~is_later[:, None], GLA_KW, axis=1))
            pair.append(np.tile(same & is_later[:, None] & (~is_later)[None, :], (1, 4)))
        gla_sel.append(np.concatenate(sel, axis=0))
        gla_later.append(np.stack(later))
        gla_earlier.append(np.stack(earlier))
        gla_pair.append(np.stack(pair))
        assert (np.stack(pair).sum(0) == np.tile(pos[:, None] > pos[None, :], (1, 4))).all()
    head_sum = np.kron(eye4, np.ones((GLA_DK, GLA_DV)))
    return dict(incl=incl, incl_bd=incl_bd, strict_bd=strict_bd, levels=levels, gla_mask=gla_mask,
                hm_k=hm_k, hm_v=hm_v, st_mask=st_mask, ind64=ind64, eye256=np.eye(256),
                gla_sel=np.stack(gla_sel), gla_later=np.stack(gla_later), gla_earlier=np.stack(gla_earlier),
                gla_pair=np.stack(gla_pair), head_sum=head_sum)


_C = _np_consts()


def _row_block_index(nblk):
    return lambda d, b, i: b * nblk + i + d * (nblk - 1 - 2 * i)


def _gla_kernel(qkf_ref, qkb_ref, vf_ref, vb_ref, smf_ref, smb_ref, up_ref, bias_ref, sel_ref, later_ref,
                earlier_ref, pair_ref, hmk_ref, hmv_ref, stm_ref, hsum_ref, *rest, nchunk, zero_init):
    if zero_init:
        of_ref, ob_ref, stf_ref, st_ref, qd_sc, m_sc, dec_sc = rest
    else:
        st0_ref, of_ref, ob_ref, stf_ref, st_ref, qd_sc, m_sc, dec_sc = rest
    ins = ((qkf_ref, vf_ref, smf_ref, of_ref), (qkb_ref, vb_ref, smb_ref, ob_ref))

    @pl.when(pl.program_id(1) == 0)
    def _():
        st_ref[...] = jnp.zeros_like(st_ref) if zero_init else st0_ref[...]

    hmkb = hmk_ref[...].astype(BF16)
    hmv = hmv_ref[...]
    stm = stm_ref[...]

    units = [(dd, c) for c in range(nchunk) for dd in range(2)]
    st1 = []
    for dd, c in units:
        qk_ref, v_ref, sm_ref, _ = ins[dd]
        rows = pl.ds(c * CHUNK, CHUNK)
        qk = qk_ref[rows, :]
        q = qk[:, 0:GLA_KW] * (GLA_DK ** -0.5)
        k = qk[:, GLA_KW:2 * GLA_KW]
        x = _bdot(sm_ref[rows, :], up_ref[dd]) + bias_ref[dd]
        lg = _log_sigmoid(x) * (1.0 / GLA_GATE_NORM)
        ball = _sel_dot(sel_ref[dd], lg)
        st1.append((q, k, v_ref[rows, :], ball, jnp.sum(lg, axis=0, keepdims=True)))
    prods = []
    for lv in range(6):
        for (dd, c), (q, k, _, ball, _) in zip(units, st1):
            b = ball[0:CHUNK]
            ref = ball[(lv + 1) * CHUNK:(lv + 2) * CHUNK]
            ql = q * jnp.where(later_ref[dd, lv] > 0.5, jnp.exp(jnp.minimum(b - ref, 0.0)), 0.0)
            kl = (k * jnp.where(earlier_ref[dd, lv] > 0.5, jnp.exp(jnp.minimum(ref - b, 0.0)), 0.0)).astype(BF16)
            kexp = jnp.concatenate([kl * hmkb[h:h + 1] for h in range(GLA_HEADS)], axis=0)
            prods.append(_bdot_nt(ql, kexp))
    for i, ((dd, c), (q, k, v, ball, btot)) in enumerate(zip(units, st1)):
        o_ref = ins[dd][3]
        n = dd * nchunk + c
        b = ball[0:CHUNK]
        att = jnp.zeros((CHUNK, GLA_HEADS * CHUNK), F32)
        for lv in range(6):
            att = jnp.where(pair_ref[dd, lv] > 0.5, prods[lv * len(units) + i], att)
        vexp = jnp.concatenate([v * hmv[h:h + 1] for h in range(GLA_HEADS)], axis=0)
        o_ref[pl.ds(c * CHUNK, CHUNK), :] = _bdot(att, vexp) + _bdot(q * k, hsum_ref[...]) * v
        qd_sc[n] = (q * jnp.exp(b)).astype(BF16)
        kd = k * jnp.exp(btot - b)
        m_sc[n] = _bdot(v.T, kd) * stm
        dec_sc[n] = jnp.broadcast_to(jnp.exp(btot), (8, GLA_KW))

    for dd in range(2):
        o_ref = ins[dd][3]
        st = st_ref[dd]
        for ci in range(nchunk):
            c = ci if dd == 0 else nchunk - 1 - ci
            n = dd * nchunk + c
            rows = pl.ds(c * CHUNK, CHUNK)
            o_ref[rows, :] += _nt(qd_sc[n], st.astype(BF16))
            st = st * dec_sc[n][0:1] + m_sc[n]
        st_ref[dd] = st
    stf_ref[...] = st_ref[...]


def _gla(proj, up_p, bias_p, st0, *, nseq, n_seq):
    t = proj.shape[0]
    nblk = n_seq // SUB
    nck = SUB // CHUNK
    rbi = _row_block_index(nblk)
    c = _C
    consts = [jnp.asarray(c["gla_sel"], BF16), jnp.asarray(c["gla_later"], F32), jnp.asarray(c["gla_earlier"], F32),
              jnp.asarray(c["gla_pair"], F32), jnp.asarray(c["hm_k"], F32), jnp.asarray(c["hm_v"], F32),
              jnp.asarray(c["st_mask"], F32), jnp.asarray(c["head_sum"], BF16)]
    st_spec = pl.BlockSpec((2, None, GLA_W, GLA_KW), lambda b, i: (0, b, 0, 0))
    full = lambda a: pl.BlockSpec(a.shape, lambda b, i: (0,) * a.ndim)
    zero_init = st0 is None
    nu = 2 * nck
    rows = (lambda b, i: rbi(0, b, i), lambda b, i: rbi(1, b, i))

    def blk(width, col, dd):
        return pl.BlockSpec((SUB, width), lambda b, i: (rows[dd](b, i), col // width))

    return pl.pallas_call(
        functools.partial(_gla_kernel, nchunk=nck, zero_init=zero_init),
        grid=(nseq, nblk),
        in_specs=[blk(2 * GLA_KW, P_GQK, 0), blk(2 * GLA_KW, P_GQK, 1), blk(GLA_W, P_GV, 0), blk(GLA_W, P_GV, 1),
                  blk(128, P_SMALL, 0), blk(128, P_SMALL + 128, 1), full(up_p), full(bias_p)]
                 + [full(a) for a in consts] + ([] if zero_init else [st_spec]),
        out_specs=[pl.BlockSpec((SUB, GLA_W), lambda b, i: (rows[0](b, i), 0)),
                   pl.BlockSpec((SUB, GLA_W), lambda b, i: (rows[1](b, i), 0)), st_spec],
        out_shape=[jax.ShapeDtypeStruct((t, GLA_W), F32), jax.ShapeDtypeStruct((t, GLA_W), F32),
                   jax.ShapeDtypeStruct((2, nseq, GLA_W, GLA_KW), F32)],
        scratch_shapes=[pltpu.VMEM((2, GLA_W, GLA_KW), F32), pltpu.VMEM((nu, CHUNK, GLA_KW), BF16),
                        pltpu.VMEM((nu, GLA_W, GLA_KW), F32), pltpu.VMEM((nu, 8, GLA_KW), F32)],
        compiler_params=_cparams(("arbitrary", "arbitrary")),
        name="gla",
    )(proj, proj, proj, proj, proj, proj, up_p, bias_p, *consts, *([] if zero_init else [st0]))


def _gdn_kernel(qkvf_ref, qkvb_ref, smf_ref, smb_ref, arow_ref, dtb_ref, tri_ref, incl_ref, strict_ref, lvl_ref,
                eye_ref, *rest, nchunk, zero_init):
    if zero_init:
        of_ref, ob_ref, sf_ref, s_ref, la_ref, be_ref, w_sc, u0_sc, qkd_sc, qe_sc, kdt_sc, egl_sc = rest
    else:
        s0_ref, of_ref, ob_ref, sf_ref, s_ref, la_ref, be_ref, w_sc, u0_sc, qkd_sc, qe_sc, kdt_sc, egl_sc = rest
    qkv_refs = (qkvf_ref, qkvb_ref)
    o_refs = (of_ref, ob_ref)

    @pl.when(pl.program_id(1) == 0)
    def _():
        s_ref[...] = jnp.zeros_like(s_ref) if zero_init else s0_ref[...]

    for dd, sm_ref in enumerate((smf_ref, smb_ref)):
        sm = sm_ref[...]
        la_ref[dd] = -jnp.exp(arow_ref[dd]) * _softplus(sm + dtb_ref[dd])
        be_ref[dd] = jax.nn.sigmoid(sm)

    eye = eye_ref[...]

    def stack(ref, r0, lo, width):
        return jnp.concatenate([ref[pl.ds(r0, CHUNK), lo + h * width:lo + (h + 1) * width]
                                for h in range(GDN_HEADS)], axis=0)

    def colstack(a, lo):
        return jnp.concatenate([a[:, lo + h:lo + h + 1] for h in range(GDN_HEADS)], axis=0)

    units = [(dd, c) for c in range(nchunk) for dd in range(2)]
    st1 = []
    for dd, c in units:
        r0 = c * CHUNK
        qkv_ref = qkv_refs[dd]
        qs = stack(qkv_ref, r0, 0, GDN_DK)
        ks = stack(qkv_ref, r0, GDN_W, GDN_DK)
        la = la_ref[dd, pl.ds(r0, CHUNK), :]
        g = _sel_dot(tri_ref[dd], la)
        ksb = ks.astype(BF16)
        st1.append((qs, ks, la, g, _bdot_nt(ksb, ksb), _bdot_nt(qs, ksb)))
    loc = []
    for (dd, c), (qs, ks, la, g, kk, qk) in zip(units, st1):
        r0 = c * CHUNK
        n = dd * nchunk + c
        vs = stack(qkv_refs[dd], r0, 2 * GDN_W, GDN_DV)
        gtot = jnp.sum(la, axis=0, keepdims=True)
        gcol = colstack(g, SM_A)
        kfcol = colstack(jnp.exp(gtot - g), SM_A)
        egcol = colstack(jnp.exp(g), SM_A)
        bcol = colstack(be_ref[dd, pl.ds(r0, CHUNK), :], SM_B)
        grow = jnp.sum(eye * gcol, axis=0, keepdims=True)
        lm_rows, qkd_rows = [], []
        for h in range(GDN_HEADS):
            rs = slice(h * CHUNK, (h + 1) * CHUNK)
            lt = (h * CHUNK) // 128
            ls = slice(lt * 128, (lt + 1) * 128)
            dec = incl_ref[dd, rs, ls] * jnp.exp(jnp.minimum(gcol[rs] - grow[:, ls], 0.0))
            lm = strict_ref[dd, rs, ls] * dec * kk[rs, ls] * bcol[rs]
            qd = qk[rs, ls] * dec
            zero = jnp.zeros_like(lm)
            lm_rows.append(jnp.concatenate([lm, zero] if lt == 0 else [zero, lm], axis=1))
            qkd_rows.append(jnp.concatenate([qd, zero] if lt == 0 else [zero, qd], axis=1))
        lmat = jnp.concatenate(lm_rows, axis=0)
        rhs = jnp.concatenate([ks * (bcol * egcol), vs * bcol], axis=1).astype(BF16)
        qkd_sc[n] = jnp.concatenate(qkd_rows, axis=0).astype(BF16)
        qe_sc[n] = (qs * egcol).astype(BF16)
        kd = ks * kfcol
        for h in range(GDN_HEADS):
            kdt_sc[n * GDN_HEADS + h] = kd[h * CHUNK:(h + 1) * CHUNK].T.astype(BF16)
        egl_sc[n] = jnp.broadcast_to(jnp.exp(gtot), (8, 128))
        loc.append((lmat.astype(BF16), rhs))
    tinv = [eye - (lm * lvl_ref[dd, 0]).astype(F32) for (dd, _), (lm, _) in zip(units, loc)]
    for lv in range(1, 6):
        tb = [t_.astype(BF16) for t_ in tinv]
        cs = [jnp.dot(lm * lvl_ref[dd, lv], t_, preferred_element_type=F32)
              for (dd, _), (lm, _), t_ in zip(units, loc, tb)]
        tinv = [t32 - jnp.dot(t_, c_.astype(BF16), preferred_element_type=F32)
                for t32, t_, c_ in zip(tinv, tb, cs)]
    for (dd, c), (_, rhs), t_ in zip(units, loc, tinv):
        sol = jnp.dot(t_.astype(BF16), rhs, preferred_element_type=F32)
        w_sc[dd * nchunk + c] = sol[:, :GDN_DK].astype(BF16)
        u0_sc[dd * nchunk + c] = sol[:, GDN_DK:]

    s_cur = [[s_ref[dd, h] for h in range(GDN_HEADS)] for dd in range(2)]
    for ci in range(nchunk):
        step = []
        for dd in range(2):
            c = ci if dd == 0 else nchunk - 1 - ci
            n = dd * nchunk + c
            sb = [s_.astype(BF16) for s_ in s_cur[dd]]
            ub = []
            for h in range(GDN_HEADS):
                rows = pl.ds(h * CHUNK, CHUNK)
                u = u0_sc[n, rows, :] - jnp.dot(w_sc[n, rows, :], sb[h], preferred_element_type=F32)
                ub.append(u.astype(BF16))
            step.append((c, n, sb, ub))
        for dd, (c, n, sb, ub) in enumerate(step):
            egl = egl_sc[n]
            for h in range(GDN_HEADS):
                s_cur[dd][h] = (egl[0:1, SM_A + h:SM_A + h + 1] * s_cur[dd][h]
                                + jnp.dot(kdt_sc[n * GDN_HEADS + h], ub[h], preferred_element_type=F32))
        for dd, (c, n, sb, ub) in enumerate(step):
            o2 = jnp.dot(qkd_sc[n], jnp.concatenate(ub, axis=0), preferred_element_type=F32)
            for h in range(GDN_HEADS):
                rows = pl.ds(h * CHUNK, CHUNK)
                o_refs[dd][pl.ds(c * CHUNK, CHUNK), h * GDN_DV:(h + 1) * GDN_DV] = (
                    jnp.dot(qe_sc[n, rows, :], sb[h], preferred_element_type=F32) + o2[h * CHUNK:(h + 1) * CHUNK])
    for dd in range(2):
        for h in range(GDN_HEADS):
            s_ref[dd, h] = s_cur[dd][h]
    sf_ref[...] = s_ref[...]


def _gdn(proj, arow, dtb, s0, *, nseq, n_seq):
    t = proj.shape[0]
    nblk = n_seq // SUB
    rbi = _row_block_index(nblk)
    c = _C
    consts = [jnp.asarray(c["incl"], BF16), jnp.asarray(c["incl_bd"], F32), jnp.asarray(c["strict_bd"], F32),
              jnp.asarray(c["levels"], BF16), jnp.asarray(c["eye256"], F32)]
    hs = (GDN_HEADS, GDN_DK, GDN_DV)
    nck = SUB // CHUNK
    nu = 2 * nck
    zero_init = s0 is None
    s_spec = pl.BlockSpec((2, None) + hs, lambda b, i: (0, b, 0, 0, 0))
    full = lambda shape: pl.BlockSpec(shape, lambda b, i: (0,) * len(shape))
    rows = (lambda b, i: rbi(0, b, i), lambda b, i: rbi(1, b, i))
    return pl.pallas_call(
        functools.partial(_gdn_kernel, nchunk=nck, zero_init=zero_init),
        grid=(nseq, nblk),
        in_specs=[pl.BlockSpec((SUB, 3 * GDN_W), lambda b, i: (rows[0](b, i), P_DQKV // (3 * GDN_W))),
                  pl.BlockSpec((SUB, 3 * GDN_W), lambda b, i: (rows[1](b, i), P_DQKV // (3 * GDN_W))),
                  pl.BlockSpec((SUB, 128), lambda b, i: (rows[0](b, i), P_SMALL // 128)),
                  pl.BlockSpec((SUB, 128), lambda b, i: (rows[1](b, i), P_SMALL // 128 + 1)),
                  full((2, 1, 128)), full((2, 1, 128)), full((2, CHUNK, CHUNK)),
                  full((2, 256, 256)), full((2, 256, 256)), full((2, 6, 256, 256)), full((256, 256))]
                 + ([] if zero_init else [s_spec]),
        out_specs=[pl.BlockSpec((SUB, GDN_W), lambda b, i: (rows[0](b, i), 0)),
                   pl.BlockSpec((SUB, GDN_W), lambda b, i: (rows[1](b, i), 0)), s_spec],
        out_shape=[jax.ShapeDtypeStruct((t, GDN_W), F32), jax.ShapeDtypeStruct((t, GDN_W), F32),
                   jax.ShapeDtypeStruct((2, nseq) + hs, F32)],
        scratch_shapes=[pltpu.VMEM((2,) + hs, F32),
                        pltpu.VMEM((2, SUB, 128), F32), pltpu.VMEM((2, SUB, 128), F32),
                        pltpu.VMEM((nu, 256, GDN_DK), BF16), pltpu.VMEM((nu, 256, GDN_DV), F32),
                        pltpu.VMEM((nu, 256, 256), BF16), pltpu.VMEM((nu, 256, GDN_DK), BF16),
                        pltpu.VMEM((nu * GDN_HEADS, GDN_DK, CHUNK), BF16), pltpu.VMEM((nu, 8, 128), F32)],
        compiler_params=_cparams(("arbitrary", "arbitrary")),
        name="gdn",
    )(proj, proj, proj, proj, arow, dtb, *consts, *([] if zero_init else [s0]))


def _cmul(ar, ai, br, bi):
    return ar * br - ai * bi, ar * bi + ai * br


def _split2(x):
    hi = x.astype(BF16)
    return hi, (x - hi.astype(F32)).astype(BF16)


def _s5_consts_kernel(lr_ref, li_ref, st_ref, br_ref, bi_ref, cr_ref, ci_ref, gm_ref, bm_ref,
                      laml_ref, wexp_ref, vexp_ref, tt_ref, w_ref, v_ref, a_ref, q_ref):
    d = pl.program_id(0) % 2
    lc = S5_L * S5_CH

    def trows(s):
        return pl.ds(pl.multiple_of((s + d * (S5_L - 1 - 2 * s)) * S5_CH, S5_CH), S5_CH)

    lr = lr_ref[...]
    li = li_ref[...]
    step = jnp.exp(st_ref[...])
    mag = jnp.exp(lr * step)
    lbr = mag * jnp.cos(li * step)
    lbi = mag * jnp.sin(li * step)
    nr = lbr - 1.0
    den = lr * lr + li * li
    fr = (nr * lr + lbi * li) / den
    fi = (lbi * lr - nr * li) / den
    bbr, bbi = _cmul(fr, fi, br_ref[...], bi_ref[...])
    cr = cr_ref[...]
    ci = ci_ref[...]
    pr = jnp.ones_like(lr)
    pi = jnp.zeros_like(lr)
    for s in range(S5_L):
        ar, ai = _cmul(cr, ci, pr, pi)
        a_ref[0, trows(s), :] = ar
        a_ref[1, trows(s), :] = ai
        wr, wi = _cmul(pr, pi, bbr, bbi)
        w_ref[0, trows(S5_L - 1 - s), :] = wr
        w_ref[1, trows(S5_L - 1 - s), :] = wi
        pr, pi = _cmul(pr, pi, lbr, lbi)
        vr, vi = _cmul(cr, ci, pr, pi)
        v_ref[0, trows(s), :] = vr
        v_ref[1, trows(s), :] = -vi
    laml_ref[0:1, :] = pr
    laml_ref[1:2, :] = pi

    lane = lax.broadcasted_iota(jnp.int32, (1, 128), 1)
    for gp in range(S5_PAIRS):
        lanes = slice(gp * 128, (gp + 1) * 128)
        for r in range(2):
            for src, dst in ((w_ref, wexp_ref), (v_ref, vexp_ref)):
                slab = src[r, :, lanes]
                dst[r, gp] = jnp.concatenate([jnp.where(lane < S5_P, slab, 0.0),
                                              jnp.where(lane >= S5_P, slab, 0.0)], axis=0).astype(BF16)

    gm = gm_ref[...]
    kall = None
    for r, b, sign in ((0, bbr, 1.0), (1, bbi, -1.0)):
        ah, al = _split2(a_ref[r])
        bh, bl = _split2(jnp.concatenate([b] * S5_GROUPS, axis=0) * gm)
        nt = lambda x, y: lax.dot_general(x, y, (((1,), (1,)), ((), ())), preferred_element_type=F32)
        term = (nt(ah, bh) + nt(ah, bl) + nt(al, bh)) * sign
        kall = term if kall is None else kall + term

    zeros = jnp.zeros((lc, S5_W), F32)
    for delta in range(S5_GROUPS):
        q_ref[delta, 0:lc, :] = zeros
        q_ref[delta, lc:2 * lc, :] = kall if delta == 0 else pltpu.roll(kall, delta * S5_CH, axis=1)
        q_ref[delta, 2 * lc:3 * lc, :] = zeros
    for g in range(S5_GROUPS):
        acc = zeros
        for s in range(S5_L):
            start = pl.multiple_of(lc - S5_CH * s + d * (S5_CH * (S5_L - 1)), S5_CH)
            blk = q_ref[(s - g) % S5_GROUPS, pl.ds(start, lc), :]
            acc = jnp.where(bm_ref[s:s + 1, :] > 0.5, blk, acc)
        tt_ref[g] = acc.astype(BF16)


def _s5_consts(lam_re, lam_im, log_step, b_re, b_im, c_re, c_im):
    n = DEPTH * 2
    lc = S5_L * S5_CH
    lr = lam_re.reshape(n, 1, S5_N)
    li = lam_im.reshape(n, 1, S5_N)
    st = jnp.repeat(log_step.reshape(n, S5_GROUPS), S5_P, axis=-1).reshape(n, 1, S5_N)
    br = b_re.reshape(n, S5_N, S5_CH).transpose(0, 2, 1)
    bi = b_im.reshape(n, S5_N, S5_CH).transpose(0, 2, 1)
    cr = c_re.reshape(n, S5_GROUPS, S5_CH, S5_P).transpose(0, 2, 1, 3).reshape(n, S5_CH, S5_N)
    ci = c_im.reshape(n, S5_GROUPS, S5_CH, S5_P).transpose(0, 2, 1, 3).reshape(n, S5_CH, S5_N)
    gm = jnp.asarray(np.kron(np.eye(S5_GROUPS), np.ones((S5_CH, S5_P))), F32)
    row = pl.BlockSpec((None, 1, S5_N), lambda i: (i, 0, 0))
    mat = pl.BlockSpec((None, S5_CH, S5_N), lambda i: (i, 0, 0))
    exp = pl.BlockSpec((None, 2, S5_PAIRS, 2 * lc, 128), lambda i: (i, 0, 0, 0, 0))
    laml, wexp, vexp, tt = pl.pallas_call(
        _s5_consts_kernel,
        grid=(n,),
        in_specs=[row, row, row, mat, mat, mat, mat, pl.BlockSpec((S5_W, S5_N), lambda i: (0, 0)),
                  pl.BlockSpec((S5_GROUPS, S5_W), lambda i: (0, 0))],
        out_specs=[pl.BlockSpec((None, 2, S5_N), lambda i: (i, 0, 0)), exp, exp,
                   pl.BlockSpec((None, S5_GROUPS, lc, lc), lambda i: (i, 0, 0, 0))],
        out_shape=[jax.ShapeDtypeStruct((n, 2, S5_N), F32),
                   jax.ShapeDtypeStruct((n, 2, S5_PAIRS, 2 * lc, 128), BF16),
                   jax.ShapeDtypeStruct((n, 2, S5_PAIRS, 2 * lc, 128), BF16),
                   jax.ShapeDtypeStruct((n, S5_GROUPS, lc, lc), BF16)],
        scratch_shapes=[pltpu.VMEM((2, lc, S5_N), F32), pltpu.VMEM((2, lc, S5_N), F32),
                        pltpu.VMEM((2, lc, S5_N), F32), pltpu.VMEM((S5_GROUPS, 3 * lc, S5_W), F32)],
        compiler_params=_cparams(("arbitrary",)),
        name="s5_consts",
    )(lr, li, st, br, bi, cr, ci, gm, _lane_block_masks())
    return laml.reshape(n, 2, S5_PAIRS, 128), wexp, vexp, tt


def _lane_block_masks():
    return jnp.asarray(np.kron(np.eye(S5_GROUPS), np.ones((1, S5_CH))), F32)


def _nt(a, b):
    return lax.dot_general(a, b, (((1,), (1,)), ((), ())), preferred_element_type=F32)


def _lane_block_transpose(arrs):
    n = len(arrs)
    width = arrs[0].shape[1]
    blk = lax.broadcasted_iota(jnp.int32, (1, width), 1) // (width // n)
    arrs = list(arrs)
    h = n // 2
    while h >= 1:
        hi = (blk & h) != 0
        for a in range(n):
            if a & h:
                continue
            lo_arr, hi_arr = arrs[a], arrs[a + h]
            arrs[a] = jnp.where(hi, pltpu.roll(hi_arr, h * (width // n), axis=1), lo_arr)
            arrs[a + h] = jnp.where(hi, hi_arr, pltpu.roll(lo_arr, width - h * (width // n), axis=1))
        h //= 2
    return arrs


def _s5_kernel(u0_ref, u1_ref, tt_ref, wexp_ref, vexp_ref, laml_ref, *rest, nseq, nblk, zero_init):
    if zero_init:
        y_ref, hf_ref, ug_sc, yg_sc, hin_re, hin_im, hp_re, hp_im = rest
    else:
        h0_ref, y_ref, hf_ref, ug_sc, yg_sc, hin_re, hin_im, hp_re, hp_im = rest
    d = pl.program_id(0)
    r = nblk * nseq
    lc = S5_L * S5_CH

    @pl.when(d == 0)
    def _():
        xs = []
        for s in range(S5_L):
            rows = pl.ds(s, r, stride=S5_L)
            xs.append(jnp.concatenate([u0_ref[rows, :], u1_ref[rows, :]], axis=1))
        for g, ug in enumerate(_lane_block_transpose(xs)):
            ug_sc[g] = ug.astype(BF16)

    for gp in range(S5_PAIRS):
        up = jnp.concatenate([ug_sc[2 * gp], ug_sc[2 * gp + 1]], axis=1)
        rows = pl.ds(gp, r, stride=S5_PAIRS)
        hin_re[rows, :] = jnp.dot(up, wexp_ref[0, gp], preferred_element_type=F32)
        hin_im[rows, :] = jnp.dot(up, wexp_ref[1, gp], preferred_element_type=F32)

    lr = laml_ref[0]
    li = laml_ref[1]

    def step(kk, carry):
        k = kk + d * (nblk - 1 - 2 * kk)
        new = []
        for b in range(nseq):
            rows = pl.ds(pl.multiple_of((b * nblk + k) * S5_PAIRS, S5_PAIRS), S5_PAIRS)
            hr, hi = carry[2 * b], carry[2 * b + 1]
            hp_re[rows, :] = hr
            hp_im[rows, :] = hi
            new += [lr * hr - li * hi + hin_re[rows, :], lr * hi + li * hr + hin_im[rows, :]]
        return tuple(new)

    init = []
    for b in range(nseq):
        rows = slice(b * S5_PAIRS, (b + 1) * S5_PAIRS)
        init += ([jnp.zeros((S5_PAIRS, 128), F32)] * 2 if zero_init else [h0_ref[0, rows, :], h0_ref[1, rows, :]])
    fin = lax.fori_loop(0, nblk, step, tuple(init))
    for b in range(nseq):
        rows = slice(b * S5_PAIRS, (b + 1) * S5_PAIRS)
        hf_ref[0, rows, :] = fin[2 * b]
        hf_ref[1, rows, :] = fin[2 * b + 1]

    for gp in range(S5_PAIRS):
        rows = pl.ds(gp, r, stride=S5_PAIRS)
        ys = (_nt(hp_re[rows, :].astype(BF16), vexp_ref[0, gp])
              + _nt(hp_im[rows, :].astype(BF16), vexp_ref[1, gp]))
        for j in range(2):
            g = 2 * gp + j
            yg = _nt(ug_sc[g], tt_ref[g]) + ys[:, j * lc:(j + 1) * lc]

            @pl.when(d == 0)
            def _():
                yg_sc[g] = yg

            @pl.when(d == 1)
            def _():
                yg_sc[g] += yg

    @pl.when(d == 1)
    def _():
        for t, yt in enumerate(_lane_block_transpose([yg_sc[g] for g in range(S5_GROUPS)])):
            rows = pl.ds(t, r, stride=S5_L)
            y_ref[0, rows, :] = yt[:, 0:128]
            y_ref[1, rows, :] = yt[:, 128:2 * 128]


def _s5(proj, consts, h0, *, layer, nseq, n_seq):
    laml, wexp, vexp, tt = consts
    t = proj.shape[0]
    nblk = n_seq // S5_L
    r = nblk * nseq
    lc = S5_L * S5_CH
    bs = nseq * S5_PAIRS
    zero_init = h0 is None
    h_spec = pl.BlockSpec((None, 2, bs, 128), lambda d: (d, 0, 0, 0))
    return pl.pallas_call(
        functools.partial(_s5_kernel, nseq=nseq, nblk=nblk, zero_init=zero_init),
        grid=(2,),
        in_specs=[pl.BlockSpec((t, 128), lambda d: (0, P_SU // 128)),
                  pl.BlockSpec((t, 128), lambda d: (0, P_SU // 128 + 1)),
                  pl.BlockSpec((None, S5_GROUPS, lc, lc), lambda d: (2 * layer + d, 0, 0, 0)),
                  pl.BlockSpec((None, 2, S5_PAIRS, 2 * lc, 128), lambda d: (2 * layer + d, 0, 0, 0, 0)),
                  pl.BlockSpec((None, 2, S5_PAIRS, 2 * lc, 128), lambda d: (2 * layer + d, 0, 0, 0, 0)),
                  pl.BlockSpec((None, 2, S5_PAIRS, 128), lambda d: (2 * layer + d, 0, 0, 0))]
                 + ([] if zero_init else [h_spec]),
        out_specs=[pl.BlockSpec((2, t, 128), lambda d: (0, 0, 0)), h_spec],
        out_shape=[jax.ShapeDtypeStruct((2, t, 128), F32),
                   jax.ShapeDtypeStruct((2, 2, bs, 128), F32)],
        scratch_shapes=[pltpu.VMEM((S5_GROUPS, r, lc), BF16), pltpu.VMEM((S5_GROUPS, r, lc), F32)]
                       + [pltpu.VMEM((r * S5_PAIRS, 128), F32) for _ in range(4)],
        compiler_params=_cparams(("arbitrary",)),
        name="s5",
    )(proj, proj, tt, wexp, vexp, laml, *([] if zero_init else [h0]))


def _out_kernel(x_ref, mod_ref, ogf_ref, ogb_ref, odf_ref, odb_ref, ysf_ref, ysb_ref, gog_ref, su_ref, dog_ref,
                ind_ref, gw_ref, dw_ref, sd_ref, wglu_ref, bglu_ref, wout32_ref, o_ref, wout_ref):
    @pl.when(pl.program_id(0) == 0)
    def _():
        wout_ref[...] = wout32_ref[...].astype(BF16)

    og = ogf_ref[...] + ogb_ref[...]
    hi, mid, _ = _split3(og * og)
    ms = jnp.dot(hi, ind_ref[...], preferred_element_type=F32) + jnp.dot(mid, ind_ref[...], preferred_element_type=F32)
    og = og * lax.rsqrt(ms + EPS) * gw_ref[...] * _silu(gog_ref[...])
    od = odf_ref[...] + odb_ref[...]
    dog = dog_ref[...]
    parts = []
    for h in range(GDN_HEADS):
        z = od[:, h * GDN_DV:(h + 1) * GDN_DV]
        z = z * lax.rsqrt(jnp.mean(z * z, axis=-1, keepdims=True) + EPS) * dw_ref[...]
        parts.append(z * _silu(dog[:, h * GDN_DV:(h + 1) * GDN_DV]))
    y = su_ref[...] * sd_ref[...] + jnp.concatenate([ysf_ref[...], ysb_ref[...]], axis=1)
    g = 0.5 * y * (1.0 + jnp.tanh(math.sqrt(2.0 / math.pi) * (y + 0.044715 * (y * y * y))))
    os_ = g * jax.nn.sigmoid(_bdot(g, wglu_ref[...]) + bglu_ref[...])
    out = _bdot(og, wout_ref[0:GLA_W, :]) + _bdot(os_, wout_ref[GLA_W + GDN_W:MIX_W, :])
    for h in range(GDN_HEADS):
        lo = GLA_W + h * GDN_DV
        out = out + _bdot(parts[h], wout_ref[lo:lo + GDN_DV, :])
    o_ref[...] = x_ref[...] + mod_ref[5:6, :] * out


def _mix_out(x, mods, proj, og, od, ys, gla_nw, gdn_nw, s5_d, w_glu, b_glu, w_out, *, layer, n_seq):
    t, d = x.shape
    nb = mods.shape[0]
    tm = _pick_tile(n_seq if nb > 1 else t, 1024)
    midx = _mod_index(nb, tm, n_seq)
    ys2 = ys

    def dirspec(w, dd):
        return pl.BlockSpec((None, tm, w), lambda i: (dd, i, 0))

    def vec(w):
        return pl.BlockSpec((None, 1, w), lambda i: (layer, 0, 0))

    return pl.pallas_call(
        _out_kernel,
        grid=(t // tm,),
        in_specs=[pl.BlockSpec((tm, d), lambda i: (i, 0)),
                  pl.BlockSpec((None, N_MOD, d), lambda i: (midx(i), 0, 0)),
                  pl.BlockSpec((tm, GLA_W), lambda i: (i, 0)), pl.BlockSpec((tm, GLA_W), lambda i: (i, 0)),
                  pl.BlockSpec((tm, GDN_W), lambda i: (i, 0)), pl.BlockSpec((tm, GDN_W), lambda i: (i, 0)),
                  dirspec(128, 0), dirspec(128, 1),
                  pl.BlockSpec((tm, GLA_W), lambda i: (i, P_GOG // GLA_W)),
                  pl.BlockSpec((tm, S5_W), lambda i: (i, P_SU // S5_W)),
                  pl.BlockSpec((tm, GDN_W), lambda i: (i, P_DOG // GDN_W)),
                  pl.BlockSpec((GLA_W, GLA_W), lambda i: (0, 0)),
                  vec(GLA_W), vec(GDN_DV), vec(S5_W),
                  pl.BlockSpec((None, S5_W, S5_W), lambda i: (layer, 0, 0)),
                  vec(S5_W),
                  pl.BlockSpec((None, MIX_W, d), lambda i: (layer, 0, 0), pipeline_mode=pl.Buffered(1))],
        out_specs=pl.BlockSpec((tm, d), lambda i: (i, 0)),
        out_shape=jax.ShapeDtypeStruct((t, d), F32),
        scratch_shapes=[pltpu.VMEM((MIX_W, d), BF16)],
        compiler_params=_cparams(("arbitrary",)),
        name="mix_out",
    )(x, mods, og[0], og[1], od[0], od[1], ys2, ys2, proj, proj, proj,
      jnp.asarray(_C["ind64"], BF16),
      jnp.tile(gla_nw, (1, GLA_HEADS)).reshape(DEPTH, 1, GLA_W), gdn_nw.reshape(DEPTH, 1, GDN_DV),
      s5_d.reshape(DEPTH, 1, S5_W), w_glu, b_glu.reshape(DEPTH, 1, S5_W), w_out)


def _gla_params(gk_up, gk_bias, layer):
    up = jnp.zeros((2, 128, GLA_KW), F32).at[:, :GLA_RANK, :].set(gk_up[layer])
    return up, gk_bias[layer].reshape(2, 1, GLA_KW)


def _gdn_params(a_log, dt_bias, layer):
    arow = jnp.zeros((2, 1, 128), F32).at[:, 0, SM_A:SM_A + GDN_HEADS].set(a_log[layer])
    dtb = jnp.zeros((2, 1, 128), F32).at[:, 0, SM_A:SM_A + GDN_HEADS].set(dt_bias[layer])
    return arow, dtb


def _gla_state_in(s):
    eye = jnp.eye(GLA_HEADS, dtype=F32)
    st = jnp.einsum('bzhde,hg->zbhegd', s.astype(F32), eye)
    return st.reshape(2, s.shape[0], GLA_W, GLA_KW)


def _gla_state_out(st):
    b = st.shape[1]
    s = st.reshape(2, b, GLA_HEADS, GLA_DV, GLA_HEADS, GLA_DK)
    s = jnp.stack([s[:, :, h, :, h, :] for h in range(GLA_HEADS)], axis=2)
    return s.transpose(1, 0, 2, 4, 3)


def _trunk(x3, mods, rows, states, p, s5c, w_in):
    nseq, n_seq, d = x3.shape
    x = x3.reshape(nseq * n_seq, d)
    fin = []
    for l in range(DEPTH):
        m = mods[l]
        x = _ffn(x, m, p['norm_w'], p['ffn_w_gate'], p['ffn_w_up'], p['ffn_w_down'], p['final_norm_w'],
                 layer=l, which=0, n_seq=n_seq, final=False)
        proj = _proj(x, m, p['norm_w'], w_in, p['gdn_conv_w'], layer=l, n_seq=n_seq, rowlen=n_seq // rows)
        if states is None:
            st0 = s0 = h0 = None
        else:
            st_gla, st_gdn, st_re, st_im = states
            st0 = _gla_state_in(st_gla[:, l])
            s0 = st_gdn[:, l].transpose(1, 0, 2, 3, 4)
            h0 = jnp.stack([st_re[:, l].reshape(nseq, 2, S5_N), st_im[:, l].reshape(nseq, 2, S5_N)], axis=0)
            h0 = h0.transpose(2, 0, 1, 3).reshape(2, 2, nseq * S5_PAIRS, 128)
        up_p, bias_p = _gla_params(p['gla_gk_up'], p['gla_gk_bias'], l)
        og_f, og_b, gla_f = _gla(proj, up_p, bias_p, st0, nseq=nseq, n_seq=n_seq)
        og = (og_f, og_b)
        arow, dtb = _gdn_params(p['gdn_a_log'], p['gdn_dt_bias'], l)
        od_f, od_b, gdn_f = _gdn(proj, arow, dtb, s0, nseq=nseq, n_seq=n_seq)
        od = (od_f, od_b)
        ys, s5_f = _s5(proj, s5c, h0, layer=l, nseq=nseq, n_seq=n_seq)
        x = _mix_out(x, m, proj, og, od, ys, p['gla_norm_w'], p['gdn_norm_w'], p['s5_d'], p['s5_w_glu'],
                     p['s5_b_glu'], p['w_out'], layer=l, n_seq=n_seq)
        x = _ffn(x, m, p['norm_w'], p['ffn_w_gate'], p['ffn_w_up'], p['ffn_w_down'], p['final_norm_w'],
                 layer=l, which=1, n_seq=n_seq, final=(l == DEPTH - 1))
        fin.append((gla_f, gdn_f, s5_f))
    return x.reshape(nseq, n_seq, d), fin


def kernel(x_prompt, x_sample, c, state_gla, state_gdn, state_s5_re, state_s5_im, c_ctx, w_ada, b_ada, norm_w, ffn_w_gate, ffn_w_up, ffn_w_down, w_in, gla_gk_up, gla_gk_bias, gla_norm_w, gdn_conv_w, gdn_a_log, gdn_dt_bias, gdn_norm_w, s5_lam_re, s5_lam_im, s5_log_step, s5_b_re, s5_b_im, s5_c_re, s5_c_im, s5_d, s5_w_glu, s5_b_glu, w_out, final_norm_w):
    p = dict(norm_w=norm_w, ffn_w_gate=ffn_w_gate, ffn_w_up=ffn_w_up, ffn_w_down=ffn_w_down,
             gla_gk_up=gla_gk_up, gla_gk_bias=gla_gk_bias, gla_norm_w=gla_norm_w, gdn_conv_w=gdn_conv_w,
             gdn_a_log=gdn_a_log, gdn_dt_bias=gdn_dt_bias, gdn_norm_w=gdn_norm_w, s5_c_re=s5_c_re,
             s5_c_im=s5_c_im, s5_d=s5_d, s5_w_glu=s5_w_glu, s5_b_glu=s5_b_glu, w_out=w_out,
             final_norm_w=final_norm_w)
    b_ctx = x_prompt.shape[0]
    b_dec = x_sample.shape[0]
    cond8 = jnp.zeros((8, D_MODEL), F32).at[0].set(c_ctx).at[1:1 + b_dec].set(c)
    mods = _ada(cond8, w_ada, b_ada)
    mods_ctx = mods[:, 0:1].reshape(DEPTH, 1, N_MOD, D_MODEL)
    mods_dec = mods[:, 1:1 + b_dec].reshape(DEPTH, b_dec, N_MOD, D_MODEL)
    s5c = _s5_consts(s5_lam_re, s5_lam_im, s5_log_step, s5_b_re, s5_b_im, s5_c_re, s5_c_im)

    y_prompt, fin = _trunk(x_prompt, mods_ctx, 1, None, p, s5c, w_in)
    new_gla = jnp.stack([_gla_state_out(f[0]) for f in fin], axis=1)
    new_gdn = jnp.stack([f[1].transpose(1, 0, 2, 3, 4) for f in fin], axis=1)
    s5f = jnp.stack([f[2].reshape(2, 2, b_ctx, S5_GROUPS, S5_P).transpose(1, 2, 0, 3, 4) for f in fin], axis=2)
    new_re, new_im = s5f[0], s5f[1]

    rows = x_sample.shape[1] // 64
    y_sample, _ = _trunk(x_sample, mods_dec, rows, (state_gla, state_gdn, state_s5_re, state_s5_im),
                         p, s5c, w_in)
    return (y_prompt, y_sample, new_gla.astype(state_gla.dtype), new_gdn.astype(state_gdn.dtype),
            new_re.astype(state_s5_re.dtype), new_im.astype(state_s5_im.dtype))
```

```python
import functools
import math

import numpy as np
import jax
import jax.numpy as jnp
from jax import lax
from jax.experimental import pallas as pl
from jax.experimental.pallas import tpu as pltpu

F32 = jnp.float32
BF16 = jnp.bfloat16

D_MODEL = 1024
DEPTH = 2
FFN_DIM = 2816
N_MOD = 9
EPS = 1e-6
CHUNK = 64
GLA_HEADS, GLA_DK, GLA_DV, GLA_RANK = 4, 32, 64, 16
GLA_GATE_NORM = 16.0
GDN_HEADS, GDN_DK, GDN_DV, GDN_CONV = 4, 128, 128, 5
S5_GROUPS, S5_CH, S5_P = 16, 16, 64
GLA_W = GLA_HEADS * GLA_DV
GLA_KW = GLA_HEADS * GLA_DK
GDN_W = GDN_HEADS * GDN_DV
S5_W = S5_GROUPS * S5_CH
S5_N = S5_GROUPS * S5_P
S5_L = 16
S5_PAIRS = S5_GROUPS // 2
MIX_W = GLA_W + GDN_W + S5_W

_IN_SIZES = (GLA_KW, GLA_KW, GLA_W, GLA_RANK, GLA_RANK, GLA_W,
             GDN_W, GDN_W, GDN_W, GDN_HEADS, GDN_HEADS, GDN_HEADS, GDN_HEADS, GDN_W, S5_W)
_IN_OFF = tuple(int(v) for v in np.cumsum((0,) + _IN_SIZES))
(_GQ, _GK, _GV, _GLRF, _GLRB, _GOG, _DQ, _DK, _DV, _DAF, _DAB, _DBF, _DBB, _DOG, _SU) = range(15)

P_DQKV = 0
P_SMALL = 1536
P_GQK = 1792
P_GV = 2048
P_GOG = 2304
P_DOG = 2560
P_SU = 3072
PROJ_W = 3328
PROJ_TN = PROJ_W // 2
SM_A = GLA_RANK
SM_B = GLA_RANK + GDN_HEADS
SUB = 256
VMEM_LIMIT = 56 * 1024 * 1024


def _cparams(sem):
    return pltpu.CompilerParams(dimension_semantics=sem, vmem_limit_bytes=VMEM_LIMIT)


def _bdot(a, b):
    return jnp.dot(a.astype(BF16), b.astype(BF16), preferred_element_type=F32)


def _bdot_nt(a, b):
    return lax.dot_general(a.astype(BF16), b.astype(BF16), (((1,), (1,)), ((), ())),
                           preferred_element_type=F32)


def _split3(x):
    hi = x.astype(BF16)
    r1 = x - hi.astype(F32)
    mid = r1.astype(BF16)
    lo = (r1 - mid.astype(F32)).astype(BF16)
    return hi, mid, lo


def _sel_dot(m01, x):
    hi, mid, lo = _split3(x)
    return (jnp.dot(m01, hi, preferred_element_type=F32) + jnp.dot(m01, mid, preferred_element_type=F32)
            + jnp.dot(m01, lo, preferred_element_type=F32))


def _silu(x):
    return x * jax.nn.sigmoid(x)


def _softplus(x):
    return jnp.maximum(x, 0.0) + jnp.log1p(jnp.exp(-jnp.abs(x)))


def _log_sigmoid(x):
    return -_softplus(-x)


def _norm_mod(x, nw, scale, shift):
    y = x * lax.rsqrt(jnp.mean(x * x, axis=-1, keepdims=True) + EPS)
    return y * nw * (1.0 + scale) + shift


def _pick_tile(n, cap):
    t = cap
    while n % t:
        t //= 2
    return t


def _ada_kernel(c_ref, w_ref, b_ref, o_ref):
    o_ref[...] = _bdot(_silu(c_ref[...]), w_ref[...]) + b_ref[...]


def _ada(cond8, w_ada, b_ada):
    depth, d, nout = w_ada.shape
    tn = 1024
    return pl.pallas_call(
        _ada_kernel,
        grid=(depth, nout // tn),
        in_specs=[pl.BlockSpec((8, d), lambda l, j: (0, 0)),
                  pl.BlockSpec((None, d, tn), lambda l, j: (l, 0, j)),
                  pl.BlockSpec((None, 1, tn), lambda l, j: (l, 0, j))],
        out_specs=pl.BlockSpec((None, 8, tn), lambda l, j: (l, 0, j)),
        out_shape=jax.ShapeDtypeStruct((depth, 8, nout), F32),
        compiler_params=_cparams(("parallel", "parallel")),
        name="ada",
    )(cond8, w_ada, b_ada.reshape(depth, 1, nout))


def _ffn_kernel(x_ref, mod_ref, nw_ref, wg_ref, wu_ref, wd_ref, fw_ref, o_ref, h_ref, *, mod_base, final):
    j = pl.program_id(1)

    @pl.when(j == 0)
    def _():
        m = mod_ref[...]
        h = _norm_mod(x_ref[...], nw_ref[...], m[mod_base + 1:mod_base + 2], m[mod_base:mod_base + 1])
        h_ref[...] = h.astype(BF16)
        o_ref[...] = jnp.zeros_like(o_ref)

    h = h_ref[...]
    g = jnp.dot(h, wg_ref[...].astype(BF16), preferred_element_type=F32)
    u = jnp.dot(h, wu_ref[...].astype(BF16), preferred_element_type=F32)
    o_ref[...] += _bdot(_silu(g) * u, wd_ref[...])

    @pl.when(j == pl.num_programs(1) - 1)
    def _():
        m = mod_ref[...]
        y = x_ref[...] + 0.5 * m[mod_base + 2:mod_base + 3] * o_ref[...]
        if final:
            y = y * lax.rsqrt(jnp.mean(y * y, axis=-1, keepdims=True) + EPS) * fw_ref[...]
        o_ref[...] = y


def _mod_index(nb, tm, n_seq):
    if nb == 1:
        return lambda i: 0
    return lambda i: (i * tm) // n_seq


def _ffn(x, mods, norm_w, w_gate, w_up, w_down, final_w, *, layer, which, n_seq, final):
    t, d = x.shape
    f = w_gate.shape[-1]
    nb = mods.shape[0]
    tm = _pick_tile(n_seq if nb > 1 else t, 2048)
    tf = 256
    midx = _mod_index(nb, tm, n_seq)
    mod_base = 0 if which == 0 else 6
    return pl.pallas_call(
        functools.partial(_ffn_kernel, mod_base=mod_base, final=final),
        grid=(t // tm, f // tf),
        in_specs=[pl.BlockSpec((tm, d), lambda i, j: (i, 0)),
                  pl.BlockSpec((None, N_MOD, d), lambda i, j: (midx(i), 0, 0)),
                  pl.BlockSpec((None, None, 1, d), lambda i, j: (layer, 2 * which, 0, 0)),
                  pl.BlockSpec((None, None, d, tf), lambda i, j: (layer, which, 0, j)),
                  pl.BlockSpec((None, None, d, tf), lambda i, j: (layer, which, 0, j)),
                  pl.BlockSpec((None, None, tf, d), lambda i, j: (layer, which, j, 0)),
                  pl.BlockSpec((1, d), lambda i, j: (0, 0))],
        out_specs=pl.BlockSpec((tm, d), lambda i, j: (i, 0)),
        out_shape=jax.ShapeDtypeStruct((t, d), F32),
        scratch_shapes=[pltpu.VMEM((tm, d), BF16)],
        compiler_params=_cparams(("parallel", "arbitrary")),
        name="ffn",
    )(x, mods, norm_w.reshape(DEPTH, 3, 1, d), w_gate, w_up, w_down, final_w.reshape(1, d))


def _proj_kernel(x_ref, mod_ref, nw_ref, w_ref, cw_ref, o_ref, h_ref, wb_ref, st_ref, *, rowlen):
    j = pl.program_id(0)
    i = pl.program_id(1)

    for jt, segs in enumerate(_W_SEGS):
        @pl.when(jnp.logical_and(i == 0, j == jt))
        def _():
            for lo, src, width in segs:
                if width % 128 == 0:
                    wb_ref[lo:lo + width, :] = w_ref[src:src + width, :].astype(BF16)
                else:
                    base = lo // 128 * 128
                    if src is None:
                        st_ref[lo - base:lo - base + width, :] = jnp.zeros((width, st_ref.shape[1]), F32)
                        wb_ref[base:base + 128, :] = st_ref[...].astype(BF16)
                    else:
                        st_ref[lo - base:lo - base + width, :] = w_ref[src:src + width, :]

    @pl.when(j == 0)
    def _():
        m = mod_ref[...]
        h_ref[i] = _norm_mod(x_ref[...], nw_ref[...], m[4:5], m[3:4]).astype(BF16)
        o_ref[...] = _nt(h_ref[i], wb_ref[...])
        tm = o_ref.shape[0]
        pos = lax.broadcasted_iota(jnp.int32, (tm, 1), 0) % rowlen
        valid = [jnp.logical_and(pos + (t - GDN_CONV // 2) >= 0, pos + (t - GDN_CONV // 2) < rowlen)
                 for t in range(GDN_CONV)]
        for g in range(3 * GDN_HEADS):
            cols = slice(P_DQKV + g * GDN_DK, P_DQKV + (g + 1) * GDN_DK)
            xg = o_ref[:, cols]
            acc = xg * cw_ref[GDN_CONV // 2:GDN_CONV // 2 + 1, cols]
            for t in range(GDN_CONV):
                s = t - GDN_CONV // 2
                if s != 0:
                    xs = pltpu.roll(xg, (-s) % tm, axis=0)
                    acc = acc + jnp.where(valid[t], xs, 0.0) * cw_ref[t:t + 1, cols]
            y = _silu(acc)
            if g < 2 * GDN_HEADS:
                y = y * lax.rsqrt(jnp.sum(y * y, axis=-1, keepdims=True) + EPS)
                if g < GDN_HEADS:
                    y = y * (GDN_DK ** -0.5)
            o_ref[:, cols] = y

    @pl.when(j == 1)
    def _():
        o_ref[...] = _nt(h_ref[i], wb_ref[...])


def _proj(x, mods, norm_w, w_in, conv_w, *, layer, n_seq, rowlen):
    t, d = x.shape
    nb = mods.shape[0]
    tm = _pick_tile(n_seq if nb > 1 else t, 512)
    midx = _mod_index(nb, tm, n_seq)
    return pl.pallas_call(
        functools.partial(_proj_kernel, rowlen=rowlen),
        grid=(PROJ_W // PROJ_TN, t // tm),
        in_specs=[pl.BlockSpec((tm, d), lambda j, i: (i * (1 - j), 0)),
                  pl.BlockSpec((None, N_MOD, d), lambda j, i: (midx(i), 0, 0)),
                  pl.BlockSpec((None, None, 1, d), lambda j, i: (layer, 1, 0, 0)),
                  pl.BlockSpec((None, w_in.shape[-1], d), lambda j, i: (layer, 0, 0), pipeline_mode=pl.Buffered(1)),
                  pl.BlockSpec((None, GDN_CONV, 3 * GDN_W), lambda j, i: (layer, 0, 0))],
        out_specs=pl.BlockSpec((tm, PROJ_TN), lambda j, i: (i, j)),
        out_shape=jax.ShapeDtypeStruct((t, PROJ_W), F32),
        scratch_shapes=[pltpu.VMEM((t // tm, tm, d), BF16), pltpu.VMEM((PROJ_TN, d), BF16),
                        pltpu.VMEM((128, d), F32)],
        compiler_params=_cparams(("arbitrary", "arbitrary")),
        name="proj",
    )(x, mods, norm_w.reshape(DEPTH, 3, 1, d), jnp.swapaxes(w_in, 1, 2), conv_w)


def _w_in_segments():
    order = [_DQ, _DK, _DV, _GLRF, _DAF, _DBF, None, _GLRB, _DAB, _DBB, None,
             _GQ, _GK, _GV, _GOG, _DOG, _SU]
    sm_pad = 128 - GLA_RANK - 2 * GDN_HEADS
    tiles = [[] for _ in range(PROJ_W // PROJ_TN)]
    dst = 0
    for s in order:
        width = sm_pad if s is None else _IN_SIZES[s]
        src = None if s is None else _IN_OFF[s]
        tile = tiles[dst // PROJ_TN]
        lo = dst % PROJ_TN
        if tile and src is not None and tile[-1][1] is not None and (
                tile[-1][0] + tile[-1][2] == lo and tile[-1][1] + tile[-1][2] == src):
            tile[-1] = (tile[-1][0], tile[-1][1], tile[-1][2] + width)
        else:
            tile.append((lo, src, width))
        dst += width
        assert (dst - 1) // PROJ_TN == (dst - width) // PROJ_TN, "a segment may not straddle column tiles"
    assert dst == PROJ_W
    return tiles


_W_SEGS = _w_in_segments()


def _np_consts():
    t = np.arange(CHUNK)
    lower = (t[:, None] >= t[None, :])
    incl = np.stack([lower, lower.T])
    strict = np.stack([t[:, None] > t[None, :], t[:, None] < t[None, :]])
    eye4 = np.eye(4, dtype=bool)
    blk = np.kron(eye4, np.ones((CHUNK, CHUNK), bool))
    incl_bd = np.stack([np.kron(eye4, incl[d]) for d in range(2)])
    strict_bd = np.stack([np.kron(eye4, strict[d]) for d in range(2)])
    levels = []
    for d in range(2):
        per = []
        for lv in range(6):
            s = 1 << lv
            same = (t[:, None] // (2 * s)) == (t[None, :] // (2 * s))
            hi = (t % (2 * s)) >= s
            m = same & hi[:, None] & (~hi)[None, :]
            if d == 1:
                m = m.T
            per.append(np.kron(eye4, m))
        levels.append(np.stack(per))
    levels = np.stack(levels)
    assert (levels.sum(1) == strict_bd).all() and blk.any()
    gla_mask = np.stack([np.tile(incl[d], (1, 4)) for d in range(2)])
    hm_k = np.kron(eye4, np.ones((1, GLA_DK)))
    hm_v = np.kron(eye4, np.ones((1, GLA_DV)))
    st_mask = np.kron(eye4, np.ones((GLA_DV, GLA_DK)))
    ind64 = np.kron(eye4, np.full((GLA_DV, GLA_DV), 1.0 / GLA_DV))
    gla_sel, gla_later, gla_earlier, gla_pair = [], [], [], []
    for d in range(2):
        pos = t if d == 0 else CHUNK - 1 - t
        cum = (pos[None, :] <= pos[:, None]).astype(np.float64)
        sel, later, earlier, pair = [cum], [], [], []
        for lv in range(6):
            s = 1 << lv
            ref_pos = (pos // (2 * s)) * (2 * s) + s - 1
            pick = (pos[None, :] == ref_pos[:, None]).astype(np.float64)
            sel.append(pick @ cum)
            is_later = (pos % (2 * s)) >= s
            same = (pos[:, None] // (2 * s)) == (pos[None, :] // (2 * s))
            later.append(np.repeat(is_later[:, None], GLA_KW, axis=1))
            earlier.append(np.repeat(~is_later[:, None], GLA_KW, axis=1))
            pair.append(np.tile(same & is_later[:, None] & (~is_later)[None, :], (1, 4)))
        gla_sel.append(np.concatenate(sel, axis=0))
        gla_later.append(np.stack(later))
        gla_earlier.append(np.stack(earlier))
        gla_pair.append(np.stack(pair))
        assert (np.stack(pair).sum(0) == np.tile(pos[:, None] > pos[None, :], (1, 4))).all()
    head_sum = np.kron(eye4, np.ones((GLA_DK, GLA_DV)))
    return dict(incl=incl, incl_bd=incl_bd, strict_bd=strict_bd, levels=levels, gla_mask=gla_mask,
                hm_k=hm_k, hm_v=hm_v, st_mask=st_mask, ind64=ind64, eye256=np.eye(256),
                gla_sel=np.stack(gla_sel), gla_later=np.stack(gla_later), gla_earlier=np.stack(gla_earlier),
                gla_pair=np.stack(gla_pair), head_sum=head_sum)


_C = _np_consts()


def _row_block_index(nblk):
    return lambda d, b, i: b * nblk + i + d * (nblk - 1 - 2 * i)


def _gla_kernel(qkf_ref, qkb_ref, vf_ref, vb_ref, smf_ref, smb_ref, up_ref, bias_ref, sel_ref, later_ref,
                earlier_ref, pair_ref, hmk_ref, hmv_ref, stm_ref, hsum_ref, *rest, nchunk, zero_init):
    if zero_init:
        of_ref, ob_ref, stf_ref, st_ref, qd_sc, m_sc, dec_sc = rest
    else:
        st0_ref, of_ref, ob_ref, stf_ref, st_ref, qd_sc, m_sc, dec_sc = rest
    ins = ((qkf_ref, vf_ref, smf_ref, of_ref), (qkb_ref, vb_ref, smb_ref, ob_ref))

    @pl.when(pl.program_id(1) == 0)
    def _():
        st_ref[...] = jnp.zeros_like(st_ref) if zero_init else st0_ref[...]

    hmkb = hmk_ref[...].astype(BF16)
    hmv = hmv_ref[...]
    stm = stm_ref[...]

    units = [(dd, c) for c in range(nchunk) for dd in range(2)]
    st1 = []
    for dd, c in units:
        qk_ref, v_ref, sm_ref, _ = ins[dd]
        rows = pl.ds(c * CHUNK, CHUNK)
        qk = qk_ref[rows, :]
        q = qk[:, 0:GLA_KW] * (GLA_DK ** -0.5)
        k = qk[:, GLA_KW:2 * GLA_KW]
        x = _bdot(sm_ref[rows, :], up_ref[dd]) + bias_ref[dd]
        lg = _log_sigmoid(x) * (1.0 / GLA_GATE_NORM)
        ball = _sel_dot(sel_ref[dd], lg)
        st1.append((q, k, v_ref[rows, :], ball, jnp.sum(lg, axis=0, keepdims=True)))
    prods = []
    for lv in range(6):
        for (dd, c), (q, k, _, ball, _) in zip(units, st1):
            b = ball[0:CHUNK]
            ref = ball[(lv + 1) * CHUNK:(lv + 2) * CHUNK]
            ql = q * jnp.where(later_ref[dd, lv] > 0.5, jnp.exp(jnp.minimum(b - ref, 0.0)), 0.0)
            kl = (k * jnp.where(earlier_ref[dd, lv] > 0.5, jnp.exp(jnp.minimum(ref - b, 0.0)), 0.0)).astype(BF16)
            kexp = jnp.concatenate([kl * hmkb[h:h + 1] for h in range(GLA_HEADS)], axis=0)
            prods.append(_bdot_nt(ql, kexp))
    for i, ((dd, c), (q, k, v, ball, btot)) in enumerate(zip(units, st1)):
        o_ref = ins[dd][3]
        n = dd * nchunk + c
        b = ball[0:CHUNK]
        att = jnp.zeros((CHUNK, GLA_HEADS * CHUNK), F32)
        for lv in range(6):
            att = jnp.where(pair_ref[dd, lv] > 0.5, prods[lv * len(units) + i], att)
        vexp = jnp.concatenate([v * hmv[h:h + 1] for h in range(GLA_HEADS)], axis=0)
        o_ref[pl.ds(c * CHUNK, CHUNK), :] = _bdot(att, vexp) + _bdot(q * k, hsum_ref[...]) * v
        qd_sc[n] = (q * jnp.exp(b)).astype(BF16)
        kd = k * jnp.exp(btot - b)
        m_sc[n] = _bdot(v.T, kd) * stm
        dec_sc[n] = jnp.broadcast_to(jnp.exp(btot), (8, GLA_KW))

    for dd in range(2):
        o_ref = ins[dd][3]
        st = st_ref[dd]
        for ci in range(nchunk):
            c = ci if dd == 0 else nchunk - 1 - ci
            n = dd * nchunk + c
            rows = pl.ds(c * CHUNK, CHUNK)
            o_ref[rows, :] += _nt(qd_sc[n], st.astype(BF16))
            st = st * dec_sc[n][0:1] + m_sc[n]
        st_ref[dd] = st
    stf_ref[...] = st_ref[...]


def _gla(proj, up_p, bias_p, st0, *, nseq, n_seq):
    t = proj.shape[0]
    nblk = n_seq // SUB
    nck = SUB // CHUNK
    rbi = _row_block_index(nblk)
    c = _C
    consts = [jnp.asarray(c["gla_sel"], BF16), jnp.asarray(c["gla_later"], F32), jnp.asarray(c["gla_earlier"], F32),
              jnp.asarray(c["gla_pair"], F32), jnp.asarray(c["hm_k"], F32), jnp.asarray(c["hm_v"], F32),
              jnp.asarray(c["st_mask"], F32), jnp.asarray(c["head_sum"], BF16)]
    st_spec = pl.BlockSpec((2, None, GLA_W, GLA_KW), lambda b, i: (0, b, 0, 0))
    full = lambda a: pl.BlockSpec(a.shape, lambda b, i: (0,) * a.ndim)
    zero_init = st0 is None
    nu = 2 * nck
    rows = (lambda b, i: rbi(0, b, i), lambda b, i: rbi(1, b, i))

    def blk(width, col, dd):
        return pl.BlockSpec((SUB, width), lambda b, i: (rows[dd](b, i), col // width))

    return pl.pallas_call(
        functools.partial(_gla_kernel, nchunk=nck, zero_init=zero_init),
        grid=(nseq, nblk),
        in_specs=[blk(2 * GLA_KW, P_GQK, 0), blk(2 * GLA_KW, P_GQK, 1), blk(GLA_W, P_GV, 0), blk(GLA_W, P_GV, 1),
                  blk(128, P_SMALL, 0), blk(128, P_SMALL + 128, 1), full(up_p), full(bias_p)]
                 + [full(a) for a in consts] + ([] if zero_init else [st_spec]),
        out_specs=[pl.BlockSpec((SUB, GLA_W), lambda b, i: (rows[0](b, i), 0)),
                   pl.BlockSpec((SUB, GLA_W), lambda b, i: (rows[1](b, i), 0)), st_spec],
        out_shape=[jax.ShapeDtypeStruct((t, GLA_W), F32), jax.ShapeDtypeStruct((t, GLA_W), F32),
                   jax.ShapeDtypeStruct((2, nseq, GLA_W, GLA_KW), F32)],
        scratch_shapes=[pltpu.VMEM((2, GLA_W, GLA_KW), F32), pltpu.VMEM((nu, CHUNK, GLA_KW), BF16),
                        pltpu.VMEM((nu, GLA_W, GLA_KW), F32), pltpu.VMEM((nu, 8, GLA_KW), F32)],
        compiler_params=_cparams(("arbitrary", "arbitrary")),
        name="gla",
    )(proj, proj, proj, proj, proj, proj, up_p, bias_p, *consts, *([] if zero_init else [st0]))


def _gdn_kernel(qkvf_ref, qkvb_ref, smf_ref, smb_ref, arow_ref, dtb_ref, tri_ref, incl_ref, strict_ref, lvl_ref,
                eye_ref, *rest, nchunk, zero_init):
    if zero_init:
        of_ref, ob_ref, sf_ref, s_ref, la_ref, be_ref, w_sc, u0_sc, qkd_sc, qe_sc, kdt_sc, egl_sc = rest
    else:
        s0_ref, of_ref, ob_ref, sf_ref, s_ref, la_ref, be_ref, w_sc, u0_sc, qkd_sc, qe_sc, kdt_sc, egl_sc = rest
    qkv_refs = (qkvf_ref, qkvb_ref)
    o_refs = (of_ref, ob_ref)

    @pl.when(pl.program_id(1) == 0)
    def _():
        s_ref[...] = jnp.zeros_like(s_ref) if zero_init else s0_ref[...]

    for dd, sm_ref in enumerate((smf_ref, smb_ref)):
        sm = sm_ref[...]
        la_ref[dd] = -jnp.exp(arow_ref[dd]) * _softplus(sm + dtb_ref[dd])
        be_ref[dd] = jax.nn.sigmoid(sm)

    eye = eye_ref[...]

    def stack(ref, r0, lo, width):
        return jnp.concatenate([ref[pl.ds(r0, CHUNK), lo + h * width:lo + (h + 1) * width]
                                for h in range(GDN_HEADS)], axis=0)

    def colstack(a, lo):
        return jnp.concatenate([a[:, lo + h:lo + h + 1] for h in range(GDN_HEADS)], axis=0)

    units = [(dd, ci if dd == 0 else nchunk - 1 - ci) for ci in range(nchunk) for dd in range(2)]
    st1 = []
    for dd, c in units:
        r0 = c * CHUNK
        qkv_ref = qkv_refs[dd]
        qs = stack(qkv_ref, r0, 0, GDN_DK)
        ks = stack(qkv_ref, r0, GDN_W, GDN_DK)
        la = la_ref[dd, pl.ds(r0, CHUNK), :]
        g = _sel_dot(tri_ref[dd], la)
        ksb = ks.astype(BF16)
        st1.append((qs, ks, la, g, _bdot_nt(ksb, ksb), _bdot_nt(qs, ksb)))
    loc = []
    for (dd, c), (qs, ks, la, g, kk, qk) in zip(units, st1):
        r0 = c * CHUNK
        n = dd * nchunk + c
        vs = stack(qkv_refs[dd], r0, 2 * GDN_W, GDN_DV)
        gtot = jnp.sum(la, axis=0, keepdims=True)
        gcol = colstack(g, SM_A)
        kfcol = colstack(jnp.exp(gtot - g), SM_A)
        egcol = colstack(jnp.exp(g), SM_A)
        bcol = colstack(be_ref[dd, pl.ds(r0, CHUNK), :], SM_B)
        grow = jnp.sum(eye * gcol, axis=0, keepdims=True)
        lm_rows, qkd_rows = [], []
        for h in range(GDN_HEADS):
            rs = slice(h * CHUNK, (h + 1) * CHUNK)
            lt = (h * CHUNK) // 128
            ls = slice(lt * 128, (lt + 1) * 128)
            dec = incl_ref[dd, rs, ls] * jnp.exp(jnp.minimum(gcol[rs] - grow[:, ls], 0.0))
            lm = strict_ref[dd, rs, ls] * dec * kk[rs, ls] * bcol[rs]
            qd = qk[rs, ls] * dec
            zero = jnp.zeros_like(lm)
            lm_rows.append(jnp.concatenate([lm, zero] if lt == 0 else [zero, lm], axis=1))
            qkd_rows.append(jnp.concatenate([qd, zero] if lt == 0 else [zero, qd], axis=1))
        lmat = jnp.concatenate(lm_rows, axis=0)
        rhs = jnp.concatenate([ks * (bcol * egcol), vs * bcol], axis=1).astype(BF16)
        qkd_sc[n] = jnp.concatenate(qkd_rows, axis=0).astype(BF16)
        qe_sc[n] = (qs * egcol).astype(BF16)
        kd = ks * kfcol
        for h in range(GDN_HEADS):
            kdt_sc[n * GDN_HEADS + h] = kd[h * CHUNK:(h + 1) * CHUNK].T.astype(BF16)
        egl_sc[n] = jnp.broadcast_to(jnp.exp(gtot), (8, 128))
        loc.append((lmat.astype(BF16), rhs))
    def inv_start(wave):
        return [eye - (loc[i][0] * lvl_ref[units[i][0], 0]).astype(F32) for i in wave]

    def inv_level(wave, tinv, lv):
        tb = [t_.astype(BF16) for t_ in tinv]
        cs = [jnp.dot(loc[i][0] * lvl_ref[units[i][0], lv], t_, preferred_element_type=F32)
              for i, t_ in zip(wave, tb)]
        return [t32 - jnp.dot(t_, c_.astype(BF16), preferred_element_type=F32)
                for t32, t_, c_ in zip(tinv, tb, cs)]

    def solve(wave, tinv):
        for i, t_ in zip(wave, tinv):
            dd, c = units[i]
            sol = jnp.dot(t_.astype(BF16), loc[i][1], preferred_element_type=F32)
            w_sc[dd * nchunk + c] = sol[:, :GDN_DK].astype(BF16)
            u0_sc[dd * nchunk + c] = sol[:, GDN_DK:]

    s_cur = [[s_ref[dd, h] for h in range(GDN_HEADS)] for dd in range(2)]

    def recur(ci):
        step = []
        for dd in range(2):
            c = ci if dd == 0 else nchunk - 1 - ci
            n = dd * nchunk + c
            sb = [s_.astype(BF16) for s_ in s_cur[dd]]
            ub = []
            for h in range(GDN_HEADS):
                rows = pl.ds(h * CHUNK, CHUNK)
                u = u0_sc[n, rows, :] - jnp.dot(w_sc[n, rows, :], sb[h], preferred_element_type=F32)
                ub.append(u.astype(BF16))
            step.append((c, n, sb, ub))
        for dd, (c, n, sb, ub) in enumerate(step):
            egl = egl_sc[n]
            for h in range(GDN_HEADS):
                s_cur[dd][h] = (egl[0:1, SM_A + h:SM_A + h + 1] * s_cur[dd][h]
                                + jnp.dot(kdt_sc[n * GDN_HEADS + h], ub[h], preferred_element_type=F32))
        for dd, (c, n, sb, ub) in enumerate(step):
            o2 = jnp.dot(qkd_sc[n], jnp.concatenate(ub, axis=0), preferred_element_type=F32)
            for h in range(GDN_HEADS):
                rows = pl.ds(h * CHUNK, CHUNK)
                o_refs[dd][pl.ds(c * CHUNK, CHUNK), h * GDN_DV:(h + 1) * GDN_DV] = (
                    jnp.dot(qe_sc[n, rows, :], sb[h], preferred_element_type=F32) + o2[h * CHUNK:(h + 1) * CHUNK])

    half = len(units) // 2
    wave1, wave2 = list(range(half)), list(range(half, len(units)))
    tinv = inv_start(wave1)
    for lv in range(1, 6):
        tinv = inv_level(wave1, tinv, lv)
    solve(wave1, tinv)
    first_steps = list(range(nchunk // 2))
    tinv = inv_start(wave2)
    for lv in range(1, 6):
        tinv = inv_level(wave2, tinv, lv)
        if lv % 2 == 0 and first_steps:
            recur(first_steps.pop(0))
    for ci in first_steps:
        recur(ci)
    solve(wave2, tinv)
    for ci in range(nchunk // 2, nchunk):
        recur(ci)
    for dd in range(2):
        for h in range(GDN_HEADS):
            s_ref[dd, h] = s_cur[dd][h]
    sf_ref[...] = s_ref[...]


def _gdn(proj, arow, dtb, s0, *, nseq, n_seq):
    t = proj.shape[0]
    nblk = n_seq // SUB
    rbi = _row_block_index(nblk)
    c = _C
    consts = [jnp.asarray(c["incl"], BF16), jnp.asarray(c["incl_bd"], F32), jnp.asarray(c["strict_bd"], F32),
              jnp.asarray(c["levels"], BF16), jnp.asarray(c["eye256"], F32)]
    hs = (GDN_HEADS, GDN_DK, GDN_DV)
    nck = SUB // CHUNK
    nu = 2 * nck
    zero_init = s0 is None
    s_spec = pl.BlockSpec((2, None) + hs, lambda b, i: (0, b, 0, 0, 0))
    full = lambda shape: pl.BlockSpec(shape, lambda b, i: (0,) * len(shape))
    rows = (lambda b, i: rbi(0, b, i), lambda b, i: rbi(1, b, i))
    return pl.pallas_call(
        functools.partial(_gdn_kernel, nchunk=nck, zero_init=zero_init),
        grid=(nseq, nblk),
        in_specs=[pl.BlockSpec((SUB, 3 * GDN_W), lambda b, i: (rows[0](b, i), P_DQKV // (3 * GDN_W))),
                  pl.BlockSpec((SUB, 3 * GDN_W), lambda b, i: (rows[1](b, i), P_DQKV // (3 * GDN_W))),
                  pl.BlockSpec((SUB, 128), lambda b, i: (rows[0](b, i), P_SMALL // 128)),
                  pl.BlockSpec((SUB, 128), lambda b, i: (rows[1](b, i), P_SMALL // 128 + 1)),
                  full((2, 1, 128)), full((2, 1, 128)), full((2, CHUNK, CHUNK)),
                  full((2, 256, 256)), full((2, 256, 256)), full((2, 6, 256, 256)), full((256, 256))]
                 + ([] if zero_init else [s_spec]),
        out_specs=[pl.BlockSpec((SUB, GDN_W), lambda b, i: (rows[0](b, i), 0)),
                   pl.BlockSpec((SUB, GDN_W), lambda b, i: (rows[1](b, i), 0)), s_spec],
        out_shape=[jax.ShapeDtypeStruct((t, GDN_W), F32), jax.ShapeDtypeStruct((t, GDN_W), F32),
                   jax.ShapeDtypeStruct((2, nseq) + hs, F32)],
        scratch_shapes=[pltpu.VMEM((2,) + hs, F32),
                        pltpu.VMEM((2, SUB, 128), F32), pltpu.VMEM((2, SUB, 128), F32),
                        pltpu.VMEM((nu, 256, GDN_DK), BF16), pltpu.VMEM((nu, 256, GDN_DV), F32),
                        pltpu.VMEM((nu, 256, 256), BF16), pltpu.VMEM((nu, 256, GDN_DK), BF16),
                        pltpu.VMEM((nu * GDN_HEADS, GDN_DK, CHUNK), BF16), pltpu.VMEM((nu, 8, 128), F32)],
        compiler_params=_cparams(("arbitrary", "arbitrary")),
        name="gdn",
    )(proj, proj, proj, proj, arow, dtb, *consts, *([] if zero_init else [s0]))


def _cmul(ar, ai, br, bi):
    return ar * br - ai * bi, ar * bi + ai * br


def _split2(x):
    hi = x.astype(BF16)
    return hi, (x - hi.astype(F32)).astype(BF16)


def _s5_consts_kernel(lr_ref, li_ref, st_ref, br_ref, bi_ref, cr_ref, ci_ref, gm_ref, bm_ref,
                      laml_ref, wexp_ref, vexp_ref, tt_ref, w_ref, v_ref, a_ref, q_ref):
    d = pl.program_id(0) % 2
    lc = S5_L * S5_CH

    def trows(s):
        return pl.ds(pl.multiple_of((s + d * (S5_L - 1 - 2 * s)) * S5_CH, S5_CH), S5_CH)

    lr = lr_ref[...]
    li = li_ref[...]
    step = jnp.exp(st_ref[...])
    mag = jnp.exp(lr * step)
    lbr = mag * jnp.cos(li * step)
    lbi = mag * jnp.sin(li * step)
    nr = lbr - 1.0
    den = lr * lr + li * li
    fr = (nr * lr + lbi * li) / den
    fi = (lbi * lr - nr * li) / den
    bbr, bbi = _cmul(fr, fi, br_ref[...], bi_ref[...])
    cr = cr_ref[...]
    ci = ci_ref[...]
    pr = jnp.ones_like(lr)
    pi = jnp.zeros_like(lr)
    for s in range(S5_L):
        ar, ai = _cmul(cr, ci, pr, pi)
        a_ref[0, trows(s), :] = ar
        a_ref[1, trows(s), :] = ai
        wr, wi = _cmul(pr, pi, bbr, bbi)
        w_ref[0, trows(S5_L - 1 - s), :] = wr
        w_ref[1, trows(S5_L - 1 - s), :] = wi
        pr, pi = _cmul(pr, pi, lbr, lbi)
        vr, vi = _cmul(cr, ci, pr, pi)
        v_ref[0, trows(s), :] = vr
        v_ref[1, trows(s), :] = -vi
    laml_ref[0:1, :] = pr
    laml_ref[1:2, :] = pi

    lane = lax.broadcasted_iota(jnp.int32, (1, 128), 1)
    for gp in range(S5_PAIRS):
        lanes = slice(gp * 128, (gp + 1) * 128)
        for r in range(2):
            for src, dst in ((w_ref, wexp_ref), (v_ref, vexp_ref)):
                slab = src[r, :, lanes]
                dst[r, gp] = jnp.concatenate([jnp.where(lane < S5_P, slab, 0.0),
                                              jnp.where(lane >= S5_P, slab, 0.0)], axis=0).astype(BF16)

    gm = gm_ref[...]
    kall = None
    for r, b, sign in ((0, bbr, 1.0), (1, bbi, -1.0)):
        ah, al = _split2(a_ref[r])
        bh, bl = _split2(jnp.concatenate([b] * S5_GROUPS, axis=0) * gm)
        nt = lambda x, y: lax.dot_general(x, y, (((1,), (1,)), ((), ())), preferred_element_type=F32)
        term = (nt(ah, bh) + nt(ah, bl) + nt(al, bh)) * sign
        kall = term if kall is None else kall + term

    zeros = jnp.zeros((lc, S5_W), F32)
    for delta in range(S5_GROUPS):
        q_ref[delta, 0:lc, :] = zeros
        q_ref[delta, lc:2 * lc, :] = kall if delta == 0 else pltpu.roll(kall, delta * S5_CH, axis=1)
        q_ref[delta, 2 * lc:3 * lc, :] = zeros
    for g in range(S5_GROUPS):
        acc = zeros
        for s in range(S5_L):
            start = pl.multiple_of(lc - S5_CH * s + d * (S5_CH * (S5_L - 1)), S5_CH)
            blk = q_ref[(s - g) % S5_GROUPS, pl.ds(start, lc), :]
            acc = jnp.where(bm_ref[s:s + 1, :] > 0.5, blk, acc)
        tt_ref[g] = acc.astype(BF16)


def _s5_consts(lam_re, lam_im, log_step, b_re, b_im, c_re, c_im):
    n = DEPTH * 2
    lc = S5_L * S5_CH
    lr = lam_re.reshape(n, 1, S5_N)
    li = lam_im.reshape(n, 1, S5_N)
    st = jnp.repeat(log_step.reshape(n, S5_GROUPS), S5_P, axis=-1).reshape(n, 1, S5_N)
    br = b_re.reshape(n, S5_N, S5_CH).transpose(0, 2, 1)
    bi = b_im.reshape(n, S5_N, S5_CH).transpose(0, 2, 1)
    cr = c_re.reshape(n, S5_GROUPS, S5_CH, S5_P).transpose(0, 2, 1, 3).reshape(n, S5_CH, S5_N)
    ci = c_im.reshape(n, S5_GROUPS, S5_CH, S5_P).transpose(0, 2, 1, 3).reshape(n, S5_CH, S5_N)
    gm = jnp.asarray(np.kron(np.eye(S5_GROUPS), np.ones((S5_CH, S5_P))), F32)
    row = pl.BlockSpec((None, 1, S5_N), lambda i: (i, 0, 0))
    mat = pl.BlockSpec((None, S5_CH, S5_N), lambda i: (i, 0, 0))
    exp = pl.BlockSpec((None, 2, S5_PAIRS, 2 * lc, 128), lambda i: (i, 0, 0, 0, 0))
    laml, wexp, vexp, tt = pl.pallas_call(
        _s5_consts_kernel,
        grid=(n,),
        in_specs=[row, row, row, mat, mat, mat, mat, pl.BlockSpec((S5_W, S5_N), lambda i: (0, 0)),
                  pl.BlockSpec((S5_GROUPS, S5_W), lambda i: (0, 0))],
        out_specs=[pl.BlockSpec((None, 2, S5_N), lambda i: (i, 0, 0)), exp, exp,
                   pl.BlockSpec((None, S5_GROUPS, lc, lc), lambda i: (i, 0, 0, 0))],
        out_shape=[jax.ShapeDtypeStruct((n, 2, S5_N), F32),
                   jax.ShapeDtypeStruct((n, 2, S5_PAIRS, 2 * lc, 128), BF16),
                   jax.ShapeDtypeStruct((n, 2, S5_PAIRS, 2 * lc, 128), BF16),
                   jax.ShapeDtypeStruct((n, S5_GROUPS, lc, lc), BF16)],
        scratch_shapes=[pltpu.VMEM((2, lc, S5_N), F32), pltpu.VMEM((2, lc, S5_N), F32),
                        pltpu.VMEM((2, lc, S5_N), F32), pltpu.VMEM((S5_GROUPS, 3 * lc, S5_W), F32)],
        compiler_params=_cparams(("arbitrary",)),
        name="s5_consts",
    )(lr, li, st, br, bi, cr, ci, gm, _lane_block_masks())
    return laml.reshape(n, 2, S5_PAIRS, 128), wexp, vexp, tt


def _lane_block_masks():
    return jnp.asarray(np.kron(np.eye(S5_GROUPS), np.ones((1, S5_CH))), F32)


def _nt(a, b):
    return lax.dot_general(a, b, (((1,), (1,)), ((), ())), preferred_element_type=F32)


def _lane_block_transpose(arrs):
    n = len(arrs)
    width = arrs[0].shape[1]
    blk = lax.broadcasted_iota(jnp.int32, (1, width), 1) // (width // n)
    arrs = list(arrs)
    h = n // 2
    while h >= 1:
        hi = (blk & h) != 0
        for a in range(n):
            if a & h:
                continue
            lo_arr, hi_arr = arrs[a], arrs[a + h]
            arrs[a] = jnp.where(hi, pltpu.roll(hi_arr, h * (width // n), axis=1), lo_arr)
            arrs[a + h] = jnp.where(hi, hi_arr, pltpu.roll(lo_arr, width - h * (width // n), axis=1))
        h //= 2
    return arrs


def _s5_kernel(u0_ref, u1_ref, tt_ref, wexp_ref, vexp_ref, laml_ref, *rest, nseq, nblk, zero_init):
    if zero_init:
        y_ref, hf_ref, ug_sc, yg_sc, hin_re, hin_im, hp_re, hp_im = rest
    else:
        h0_ref, y_ref, hf_ref, ug_sc, yg_sc, hin_re, hin_im, hp_re, hp_im = rest
    d = pl.program_id(0)
    r = nblk * nseq
    lc = S5_L * S5_CH

    @pl.when(d == 0)
    def _():
        xs = []
        for s in range(S5_L):
            rows = pl.ds(s, r, stride=S5_L)
            xs.append(jnp.concatenate([u0_ref[rows, :], u1_ref[rows, :]], axis=1))
        for g, ug in enumerate(_lane_block_transpose(xs)):
            ug_sc[g] = ug.astype(BF16)

    for gp in range(S5_PAIRS):
        up = jnp.concatenate([ug_sc[2 * gp], ug_sc[2 * gp + 1]], axis=1)
        rows = pl.ds(gp, r, stride=S5_PAIRS)
        hin_re[rows, :] = jnp.dot(up, wexp_ref[0, gp], preferred_element_type=F32)
        hin_im[rows, :] = jnp.dot(up, wexp_ref[1, gp], preferred_element_type=F32)

    lr = laml_ref[0]
    li = laml_ref[1]

    def step(kk, carry):
        k = kk + d * (nblk - 1 - 2 * kk)
        new = []
        for b in range(nseq):
            rows = pl.ds(pl.multiple_of((b * nblk + k) * S5_PAIRS, S5_PAIRS), S5_PAIRS)
            hr, hi = carry[2 * b], carry[2 * b + 1]
            hp_re[rows, :] = hr
            hp_im[rows, :] = hi
            new += [lr * hr - li * hi + hin_re[rows, :], lr * hi + li * hr + hin_im[rows, :]]
        return tuple(new)

    init = []
    for b in range(nseq):
        rows = slice(b * S5_PAIRS, (b + 1) * S5_PAIRS)
        init += ([jnp.zeros((S5_PAIRS, 128), F32)] * 2 if zero_init else [h0_ref[0, rows, :], h0_ref[1, rows, :]])
    fin = lax.fori_loop(0, nblk, step, tuple(init))
    for b in range(nseq):
        rows = slice(b * S5_PAIRS, (b + 1) * S5_PAIRS)
        hf_ref[0, rows, :] = fin[2 * b]
        hf_ref[1, rows, :] = fin[2 * b + 1]

    for gp in range(S5_PAIRS):
        rows = pl.ds(gp, r, stride=S5_PAIRS)
        ys = (_nt(hp_re[rows, :].astype(BF16), vexp_ref[0, gp])
              + _nt(hp_im[rows, :].astype(BF16), vexp_ref[1, gp]))
        for j in range(2):
            g = 2 * gp + j
            yg = _nt(ug_sc[g], tt_ref[g]) + ys[:, j * lc:(j + 1) * lc]

            @pl.when(d == 0)
            def _():
                yg_sc[g] = yg

            @pl.when(d == 1)
            def _():
                yg_sc[g] += yg

    @pl.when(d == 1)
    def _():
        for t, yt in enumerate(_lane_block_transpose([yg_sc[g] for g in range(S5_GROUPS)])):
            rows = pl.ds(t, r, stride=S5_L)
            y_ref[0, rows, :] = yt[:, 0:128]
            y_ref[1, rows, :] = yt[:, 128:2 * 128]


def _s5(proj, consts, h0, *, layer, nseq, n_seq):
    laml, wexp, vexp, tt = consts
    t = proj.shape[0]
    nblk = n_seq // S5_L
    r = nblk * nseq
    lc = S5_L * S5_CH
    bs = nseq * S5_PAIRS
    zero_init = h0 is None
    h_spec = pl.BlockSpec((None, 2, bs, 128), lambda d: (d, 0, 0, 0))
    return pl.pallas_call(
        functools.partial(_s5_kernel, nseq=nseq, nblk=nblk, zero_init=zero_init),
        grid=(2,),
        in_specs=[pl.BlockSpec((t, 128), lambda d: (0, P_SU // 128)),
                  pl.BlockSpec((t, 128), lambda d: (0, P_SU // 128 + 1)),
                  pl.BlockSpec((None, S5_GROUPS, lc, lc), lambda d: (2 * layer + d, 0, 0, 0)),
                  pl.BlockSpec((None, 2, S5_PAIRS, 2 * lc, 128), lambda d: (2 * layer + d, 0, 0, 0, 0)),
                  pl.BlockSpec((None, 2, S5_PAIRS, 2 * lc, 128), lambda d: (2 * layer + d, 0, 0, 0, 0)),
                  pl.BlockSpec((None, 2, S5_PAIRS, 128), lambda d: (2 * layer + d, 0, 0, 0))]
                 + ([] if zero_init else [h_spec]),
        out_specs=[pl.BlockSpec((2, t, 128), lambda d: (0, 0, 0)), h_spec],
        out_shape=[jax.ShapeDtypeStruct((2, t, 128), F32),
                   jax.ShapeDtypeStruct((2, 2, bs, 128), F32)],
        scratch_shapes=[pltpu.VMEM((S5_GROUPS, r, lc), BF16), pltpu.VMEM((S5_GROUPS, r, lc), F32)]
                       + [pltpu.VMEM((r * S5_PAIRS, 128), F32) for _ in range(4)],
        compiler_params=_cparams(("arbitrary",)),
        name="s5",
    )(proj, proj, tt, wexp, vexp, laml, *([] if zero_init else [h0]))


def _out_kernel(x_ref, mod_ref, ogf_ref, ogb_ref, odf_ref, odb_ref, ysf_ref, ysb_ref, gog_ref, su_ref, dog_ref,
                ind_ref, gw_ref, dw_ref, sd_ref, wglu_ref, bglu_ref, wout32_ref, o_ref, wout_ref):
    @pl.when(pl.program_id(0) == 0)
    def _():
        wout_ref[...] = wout32_ref[...].astype(BF16)

    og = ogf_ref[...] + ogb_ref[...]
    hi, mid, _ = _split3(og * og)
    ms = jnp.dot(hi, ind_ref[...], preferred_element_type=F32) + jnp.dot(mid, ind_ref[...], preferred_element_type=F32)
    og = og * lax.rsqrt(ms + EPS) * gw_ref[...] * _silu(gog_ref[...])
    od = odf_ref[...] + odb_ref[...]
    dog = dog_ref[...]
    parts = []
    for h in range(GDN_HEADS):
        z = od[:, h * GDN_DV:(h + 1) * GDN_DV]
        z = z * lax.rsqrt(jnp.mean(z * z, axis=-1, keepdims=True) + EPS) * dw_ref[...]
        parts.append(z * _silu(dog[:, h * GDN_DV:(h + 1) * GDN_DV]))
    y = su_ref[...] * sd_ref[...] + jnp.concatenate([ysf_ref[...], ysb_ref[...]], axis=1)
    g = 0.5 * y * (1.0 + jnp.tanh(math.sqrt(2.0 / math.pi) * (y + 0.044715 * (y * y * y))))
    os_ = g * jax.nn.sigmoid(_bdot(g, wglu_ref[...]) + bglu_ref[...])
    out = _bdot(og, wout_ref[0:GLA_W, :]) + _bdot(os_, wout_ref[GLA_W + GDN_W:MIX_W, :])
    for h in range(GDN_HEADS):
        lo = GLA_W + h * GDN_DV
        out = out + _bdot(parts[h], wout_ref[lo:lo + GDN_DV, :])
    o_ref[...] = x_ref[...] + mod_ref[5:6, :] * out


def _mix_out(x, mods, proj, og, od, ys, gla_nw, gdn_nw, s5_d, w_glu, b_glu, w_out, *, layer, n_seq):
    t, d = x.shape
    nb = mods.shape[0]
    tm = _pick_tile(n_seq if nb > 1 else t, 512)
    midx = _mod_index(nb, tm, n_seq)
    ys2 = ys

    def dirspec(w, dd):
        return pl.BlockSpec((None, tm, w), lambda i: (dd, i, 0))

    def vec(w):
        return pl.BlockSpec((None, 1, w), lambda i: (layer, 0, 0))

    return pl.pallas_call(
        _out_kernel,
        grid=(t // tm,),
        in_specs=[pl.BlockSpec((tm, d), lambda i: (i, 0)),
                  pl.BlockSpec((None, N_MOD, d), lambda i: (midx(i), 0, 0)),
                  pl.BlockSpec((tm, GLA_W), lambda i: (i, 0)), pl.BlockSpec((tm, GLA_W), lambda i: (i, 0)),
                  pl.BlockSpec((tm, GDN_W), lambda i: (i, 0)), pl.BlockSpec((tm, GDN_W), lambda i: (i, 0)),
                  dirspec(128, 0), dirspec(128, 1),
                  pl.BlockSpec((tm, GLA_W), lambda i: (i, P_GOG // GLA_W)),
                  pl.BlockSpec((tm, S5_W), lambda i: (i, P_SU // S5_W)),
                  pl.BlockSpec((tm, GDN_W), lambda i: (i, P_DOG // GDN_W)),
                  pl.BlockSpec((GLA_W, GLA_W), lambda i: (0, 0)),
                  vec(GLA_W), vec(GDN_DV), vec(S5_W),
                  pl.BlockSpec((None, S5_W, S5_W), lambda i: (layer, 0, 0)),
                  vec(S5_W),
                  pl.BlockSpec((None, MIX_W, d), lambda i: (layer, 0, 0), pipeline_mode=pl.Buffered(1))],
        out_specs=pl.BlockSpec((tm, d), lambda i: (i, 0)),
        out_shape=jax.ShapeDtypeStruct((t, d), F32),
        scratch_shapes=[pltpu.VMEM((MIX_W, d), BF16)],
        compiler_params=_cparams(("arbitrary",)),
        name="mix_out",
    )(x, mods, og[0], og[1], od[0], od[1], ys2, ys2, proj, proj, proj,
      jnp.asarray(_C["ind64"], BF16),
      jnp.tile(gla_nw, (1, GLA_HEADS)).reshape(DEPTH, 1, GLA_W), gdn_nw.reshape(DEPTH, 1, GDN_DV),
      s5_d.reshape(DEPTH, 1, S5_W), w_glu, b_glu.reshape(DEPTH, 1, S5_W), w_out)


def _gla_params(gk_up, gk_bias, layer):
    up = jnp.zeros((2, 128, GLA_KW), F32).at[:, :GLA_RANK, :].set(gk_up[layer])
    return up, gk_bias[layer].reshape(2, 1, GLA_KW)


def _gdn_params(a_log, dt_bias, layer):
    arow = jnp.zeros((2, 1, 128), F32).at[:, 0, SM_A:SM_A + GDN_HEADS].set(a_log[layer])
    dtb = jnp.zeros((2, 1, 128), F32).at[:, 0, SM_A:SM_A + GDN_HEADS].set(dt_bias[layer])
    return arow, dtb


def _gla_state_in(s):
    eye = jnp.eye(GLA_HEADS, dtype=F32)
    st = jnp.einsum('bzhde,hg->zbhegd', s.astype(F32), eye)
    return st.reshape(2, s.shape[0], GLA_W, GLA_KW)


def _gla_state_out(st):
    b = st.shape[1]
    s = st.reshape(2, b, GLA_HEADS, GLA_DV, GLA_HEADS, GLA_DK)
    s = jnp.stack([s[:, :, h, :, h, :] for h in range(GLA_HEADS)], axis=2)
    return s.transpose(1, 0, 2, 4, 3)


def _trunk(x3, mods, rows, states, p, s5c, w_in):
    nseq, n_seq, d = x3.shape
    x = x3.reshape(nseq * n_seq, d)
    fin = []
    for l in range(DEPTH):
        m = mods[l]
        x = _ffn(x, m, p['norm_w'], p['ffn_w_gate'], p['ffn_w_up'], p['ffn_w_down'], p['final_norm_w'],
                 layer=l, which=0, n_seq=n_seq, final=False)
        proj = _proj(x, m, p['norm_w'], w_in, p['gdn_conv_w'], layer=l, n_seq=n_seq, rowlen=n_seq // rows)
        if states is None:
            st0 = s0 = h0 = None
        else:
            st_gla, st_gdn, st_re, st_im = states
            st0 = _gla_state_in(st_gla[:, l])
            s0 = st_gdn[:, l].transpose(1, 0, 2, 3, 4)
            h0 = jnp.stack([st_re[:, l].reshape(nseq, 2, S5_N), st_im[:, l].reshape(nseq, 2, S5_N)], axis=0)
            h0 = h0.transpose(2, 0, 1, 3).reshape(2, 2, nseq * S5_PAIRS, 128)
        up_p, bias_p = _gla_params(p['gla_gk_up'], p['gla_gk_bias'], l)
        og_f, og_b, gla_f = _gla(proj, up_p, bias_p, st0, nseq=nseq, n_seq=n_seq)
        og = (og_f, og_b)
        arow, dtb = _gdn_params(p['gdn_a_log'], p['gdn_dt_bias'], l)
        od_f, od_b, gdn_f = _gdn(proj, arow, dtb, s0, nseq=nseq, n_seq=n_seq)
        od = (od_f, od_b)
        ys, s5_f = _s5(proj, s5c, h0, layer=l, nseq=nseq, n_seq=n_seq)
        x = _mix_out(x, m, proj, og, od, ys, p['gla_norm_w'], p['gdn_norm_w'], p['s5_d'], p['s5_w_glu'],
                     p['s5_b_glu'], p['w_out'], layer=l, n_seq=n_seq)
        x = _ffn(x, m, p['norm_w'], p['ffn_w_gate'], p['ffn_w_up'], p['ffn_w_down'], p['final_norm_w'],
                 layer=l, which=1, n_seq=n_seq, final=(l == DEPTH - 1))
        fin.append((gla_f, gdn_f, s5_f))
    return x.reshape(nseq, n_seq, d), fin


def kernel(x_prompt, x_sample, c, state_gla, state_gdn, state_s5_re, state_s5_im, c_ctx, w_ada, b_ada, norm_w, ffn_w_gate, ffn_w_up, ffn_w_down, w_in, gla_gk_up, gla_gk_bias, gla_norm_w, gdn_conv_w, gdn_a_log, gdn_dt_bias, gdn_norm_w, s5_lam_re, s5_lam_im, s5_log_step, s5_b_re, s5_b_im, s5_c_re, s5_c_im, s5_d, s5_w_glu, s5_b_glu, w_out, final_norm_w):
    p = dict(norm_w=norm_w, ffn_w_gate=ffn_w_gate, ffn_w_up=ffn_w_up, ffn_w_down=ffn_w_down,
             gla_gk_up=gla_gk_up, gla_gk_bias=gla_gk_bias, gla_norm_w=gla_norm_w, gdn_conv_w=gdn_conv_w,
             gdn_a_log=gdn_a_log, gdn_dt_bias=gdn_dt_bias, gdn_norm_w=gdn_norm_w, s5_c_re=s5_c_re,
             s5_c_im=s5_c_im, s5_d=s5_d, s5_w_glu=s5_w_glu, s5_b_glu=s5_b_glu, w_out=w_out,
             final_norm_w=final_norm_w)
    b_ctx = x_prompt.shape[0]
    b_dec = x_sample.shape[0]
    cond8 = jnp.zeros((8, D_MODEL), F32).at[0].set(c_ctx).at[1:1 + b_dec].set(c)
    mods = _ada(cond8, w_ada, b_ada)
    mods_ctx = mods[:, 0:1].reshape(DEPTH, 1, N_MOD, D_MODEL)
    mods_dec = mods[:, 1:1 + b_dec].reshape(DEPTH, b_dec, N_MOD, D_MODEL)
    s5c = _s5_consts(s5_lam_re, s5_lam_im, s5_log_step, s5_b_re, s5_b_im, s5_c_re, s5_c_im)

    y_prompt, fin = _trunk(x_prompt, mods_ctx, 1, None, p, s5c, w_in)
    new_gla = jnp.stack([_gla_state_out(f[0]) for f in fin], axis=1)
    new_gdn = jnp.stack([f[1].transpose(1, 0, 2, 3, 4) for f in fin], axis=1)
    s5f = jnp.stack([f[2].reshape(2, 2, b_ctx, S5_GROUPS, S5_P).transpose(1, 2, 0, 3, 4) for f in fin], axis=2)
    new_re, new_im = s5f[0], s5f[1]

    rows = x_sample.shape[1] // 64
    y_sample, _ = _trunk(x_sample, mods_dec, rows, (state_gla, state_gdn, state_s5_re, state_s5_im),
                         p, s5c, w_in)
    return (y_prompt, y_sample, new_gla.astype(state_gla.dtype), new_gdn.astype(state_gdn.dtype),
            new_re.astype(state_s5_re.dtype), new_im.astype(state_s5_im.dtype))
```

```python
import functools
import math

import numpy as np
import jax
import jax.numpy as jnp
from jax import lax
from jax.experimental import pallas as pl
from jax.experimental.pallas import tpu as pltpu

F32 = jnp.float32
BF16 = jnp.bfloat16

D_MODEL = 1024
DEPTH = 2
FFN_DIM = 2816
N_MOD = 9
EPS = 1e-6
CHUNK = 64
GLA_HEADS, GLA_DK, GLA_DV, GLA_RANK = 4, 32, 64, 16
GLA_GATE_NORM = 16.0
GDN_HEADS, GDN_DK, GDN_DV, GDN_CONV = 4, 128, 128, 5
S5_GROUPS, S5_CH, S5_P = 16, 16, 64
GLA_W = GLA_HEADS * GLA_DV
GLA_KW = GLA_HEADS * GLA_DK
GDN_W = GDN_HEADS * GDN_DV
S5_W = S5_GROUPS * S5_CH
S5_N = S5_GROUPS * S5_P
S5_L = 16
S5_PAIRS = S5_GROUPS // 2
MIX_W = GLA_W + GDN_W + S5_W

_IN_SIZES = (GLA_KW, GLA_KW, GLA_W, GLA_RANK, GLA_RANK, GLA_W,
             GDN_W, GDN_W, GDN_W, GDN_HEADS, GDN_HEADS, GDN_HEADS, GDN_HEADS, GDN_W, S5_W)
_IN_OFF = tuple(int(v) for v in np.cumsum((0,) + _IN_SIZES))
(_GQ, _GK, _GV, _GLRF, _GLRB, _GOG, _DQ, _DK, _DV, _DAF, _DAB, _DBF, _DBB, _DOG, _SU) = range(15)

P_DQKV = 0
P_SMALL = 1536
P_GQK = 1792
P_GV = 2048
P_GOG = 2304
P_DOG = 2560
P_SU = 3072
PROJ_W = 3328
PROJ_TN = PROJ_W // 2
SM_A = GLA_RANK
SM_B = GLA_RANK + GDN_HEADS
SUB = 256
VMEM_LIMIT = 56 * 1024 * 1024


def _cparams(sem):
    return pltpu.CompilerParams(dimension_semantics=sem, vmem_limit_bytes=VMEM_LIMIT)


def _bdot(a, b):
    return jnp.dot(a.astype(BF16), b.astype(BF16), preferred_element_type=F32)


def _bdot_nt(a, b):
    return lax.dot_general(a.astype(BF16), b.astype(BF16), (((1,), (1,)), ((), ())),
                           preferred_element_type=F32)


def _split3(x):
    hi = x.astype(BF16)
    r1 = x - hi.astype(F32)
    mid = r1.astype(BF16)
    lo = (r1 - mid.astype(F32)).astype(BF16)
    return hi, mid, lo


def _sel_dot(m01, x):
    hi, mid, lo = _split3(x)
    return (jnp.dot(m01, hi, preferred_element_type=F32) + jnp.dot(m01, mid, preferred_element_type=F32)
            + jnp.dot(m01, lo, preferred_element_type=F32))


def _silu(x):
    return x * jax.nn.sigmoid(x)


def _softplus(x):
    return jnp.maximum(x, 0.0) + jnp.log1p(jnp.exp(-jnp.abs(x)))


def _log_sigmoid(x):
    return -_softplus(-x)


def _norm_mod(x, nw, scale, shift):
    y = x * lax.rsqrt(jnp.mean(x * x, axis=-1, keepdims=True) + EPS)
    return y * nw * (1.0 + scale) + shift


def _pick_tile(n, cap):
    t = cap
    while n % t:
        t //= 2
    return t


def _ada_kernel(c_ref, w_ref, b_ref, o_ref):
    o_ref[...] = _bdot(_silu(c_ref[...]), w_ref[...]) + b_ref[...]


def _ada(cond8, w_ada, b_ada):
    depth, d, nout = w_ada.shape
    tn = 1024
    return pl.pallas_call(
        _ada_kernel,
        grid=(depth, nout // tn),
        in_specs=[pl.BlockSpec((8, d), lambda l, j: (0, 0)),
                  pl.BlockSpec((None, d, tn), lambda l, j: (l, 0, j)),
                  pl.BlockSpec((None, 1, tn), lambda l, j: (l, 0, j))],
        out_specs=pl.BlockSpec((None, 8, tn), lambda l, j: (l, 0, j)),
        out_shape=jax.ShapeDtypeStruct((depth, 8, nout), F32),
        compiler_params=_cparams(("parallel", "parallel")),
        name="ada",
    )(cond8, w_ada, b_ada.reshape(depth, 1, nout))


def _ffn_kernel(x_ref, mod_ref, nw_ref, wg_ref, wu_ref, wd_ref, fw_ref, o_ref, h_ref, *, mod_base, final):
    j = pl.program_id(1)

    @pl.when(j == 0)
    def _():
        m = mod_ref[...]
        h = _norm_mod(x_ref[...], nw_ref[...], m[mod_base + 1:mod_base + 2], m[mod_base:mod_base + 1])
        h_ref[...] = h.astype(BF16)
        o_ref[...] = jnp.zeros_like(o_ref)

    h = h_ref[...]
    g = jnp.dot(h, wg_ref[...].astype(BF16), preferred_element_type=F32)
    u = jnp.dot(h, wu_ref[...].astype(BF16), preferred_element_type=F32)
    o_ref[...] += _bdot(_silu(g) * u, wd_ref[...])

    @pl.when(j == pl.num_programs(1) - 1)
    def _():
        m = mod_ref[...]
        y = x_ref[...] + 0.5 * m[mod_base + 2:mod_base + 3] * o_ref[...]
        if final:
            y = y * lax.rsqrt(jnp.mean(y * y, axis=-1, keepdims=True) + EPS) * fw_ref[...]
        o_ref[...] = y


def _mod_index(nb, tm, n_seq):
    if nb == 1:
        return lambda i: 0
    return lambda i: (i * tm) // n_seq


def _ffn(x, mods, norm_w, w_gate, w_up, w_down, final_w, *, layer, which, n_seq, final):
    t, d = x.shape
    f = w_gate.shape[-1]
    nb = mods.shape[0]
    tm = _pick_tile(n_seq if nb > 1 else t, 2048)
    tf = 256
    midx = _mod_index(nb, tm, n_seq)
    mod_base = 0 if which == 0 else 6
    return pl.pallas_call(
        functools.partial(_ffn_kernel, mod_base=mod_base, final=final),
        grid=(t // tm, f // tf),
        in_specs=[pl.BlockSpec((tm, d), lambda i, j: (i, 0)),
                  pl.BlockSpec((None, N_MOD, d), lambda i, j: (midx(i), 0, 0)),
                  pl.BlockSpec((None, None, 1, d), lambda i, j: (layer, 2 * which, 0, 0)),
                  pl.BlockSpec((None, None, d, tf), lambda i, j: (layer, which, 0, j)),
                  pl.BlockSpec((None, None, d, tf), lambda i, j: (layer, which, 0, j)),
                  pl.BlockSpec((None, None, tf, d), lambda i, j: (layer, which, j, 0)),
                  pl.BlockSpec((1, d), lambda i, j: (0, 0))],
        out_specs=pl.BlockSpec((tm, d), lambda i, j: (i, 0)),
        out_shape=jax.ShapeDtypeStruct((t, d), F32),
        scratch_shapes=[pltpu.VMEM((tm, d), BF16)],
        compiler_params=_cparams(("parallel", "arbitrary")),
        name="ffn",
    )(x, mods, norm_w.reshape(DEPTH, 3, 1, d), w_gate, w_up, w_down, final_w.reshape(1, d))


def _proj_kernel(x_ref, mod_ref, nw_ref, w_ref, cw_ref, o_ref, h_ref, wb_ref, st_ref, *, rowlen):
    j = pl.program_id(0)
    i = pl.program_id(1)

    for jt, segs in enumerate(_W_SEGS):
        @pl.when(jnp.logical_and(i == 0, j == jt))
        def _():
            for lo, src, width in segs:
                if width % 128 == 0:
                    wb_ref[lo:lo + width, :] = w_ref[src:src + width, :].astype(BF16)
                else:
                    base = lo // 128 * 128
                    if src is None:
                        st_ref[lo - base:lo - base + width, :] = jnp.zeros((width, st_ref.shape[1]), F32)
                        wb_ref[base:base + 128, :] = st_ref[...].astype(BF16)
                    else:
                        st_ref[lo - base:lo - base + width, :] = w_ref[src:src + width, :]

    @pl.when(j == 0)
    def _():
        m = mod_ref[...]
        h_ref[i] = _norm_mod(x_ref[...], nw_ref[...], m[4:5], m[3:4]).astype(BF16)
        o_ref[...] = _nt(h_ref[i], wb_ref[...])
        tm = o_ref.shape[0]
        pos = lax.broadcasted_iota(jnp.int32, (tm, 1), 0) % rowlen
        valid = [jnp.logical_and(pos + (t - GDN_CONV // 2) >= 0, pos + (t - GDN_CONV // 2) < rowlen)
                 for t in range(GDN_CONV)]
        for g in range(3 * GDN_HEADS):
            cols = slice(P_DQKV + g * GDN_DK, P_DQKV + (g + 1) * GDN_DK)
            xg = o_ref[:, cols]
            acc = xg * cw_ref[GDN_CONV // 2:GDN_CONV // 2 + 1, cols]
            for t in range(GDN_CONV):
                s = t - GDN_CONV // 2
                if s != 0:
                    xs = pltpu.roll(xg, (-s) % tm, axis=0)
                    acc = acc + jnp.where(valid[t], xs, 0.0) * cw_ref[t:t + 1, cols]
            y = _silu(acc)
            if g < 2 * GDN_HEADS:
                y = y * lax.rsqrt(jnp.sum(y * y, axis=-1, keepdims=True) + EPS)
                if g < GDN_HEADS:
                    y = y * (GDN_DK ** -0.5)
            o_ref[:, cols] = y

    @pl.when(j == 1)
    def _():
        o_ref[...] = _nt(h_ref[i], wb_ref[...])


def _proj(x, mods, norm_w, w_in, conv_w, *, layer, n_seq, rowlen):
    t, d = x.shape
    nb = mods.shape[0]
    tm = _pick_tile(n_seq if nb > 1 else t, 512)
    midx = _mod_index(nb, tm, n_seq)
    return pl.pallas_call(
        functools.partial(_proj_kernel, rowlen=rowlen),
        grid=(PROJ_W // PROJ_TN, t // tm),
        in_specs=[pl.BlockSpec((tm, d), lambda j, i: (i * (1 - j), 0)),
                  pl.BlockSpec((None, N_MOD, d), lambda j, i: (midx(i), 0, 0)),
                  pl.BlockSpec((None, None, 1, d), lambda j, i: (layer, 1, 0, 0)),
                  pl.BlockSpec((None, w_in.shape[-1], d), lambda j, i: (layer, 0, 0), pipeline_mode=pl.Buffered(1)),
                  pl.BlockSpec((None, GDN_CONV, 3 * GDN_W), lambda j, i: (layer, 0, 0))],
        out_specs=pl.BlockSpec((tm, PROJ_TN), lambda j, i: (i, j)),
        out_shape=jax.ShapeDtypeStruct((t, PROJ_W), F32),
        scratch_shapes=[pltpu.VMEM((t // tm, tm, d), BF16), pltpu.VMEM((PROJ_TN, d), BF16),
                        pltpu.VMEM((128, d), F32)],
        compiler_params=_cparams(("arbitrary", "arbitrary")),
        name="proj",
    )(x, mods, norm_w.reshape(DEPTH, 3, 1, d), jnp.swapaxes(w_in, 1, 2), conv_w)


def _w_in_segments():
    order = [_DQ, _DK, _DV, _GLRF, _DAF, _DBF, None, _GLRB, _DAB, _DBB, None,
             _GQ, _GK, _GV, _GOG, _DOG, _SU]
    sm_pad = 128 - GLA_RANK - 2 * GDN_HEADS
    tiles = [[] for _ in range(PROJ_W // PROJ_TN)]
    dst = 0
    for s in order:
        width = sm_pad if s is None else _IN_SIZES[s]
        src = None if s is None else _IN_OFF[s]
        tile = tiles[dst // PROJ_TN]
        lo = dst % PROJ_TN
        if tile and src is not None and tile[-1][1] is not None and (
                tile[-1][0] + tile[-1][2] == lo and tile[-1][1] + tile[-1][2] == src):
            tile[-1] = (tile[-1][0], tile[-1][1], tile[-1][2] + width)
        else:
            tile.append((lo, src, width))
        dst += width
        assert (dst - 1) // PROJ_TN == (dst - width) // PROJ_TN, "a segment may not straddle column tiles"
    assert dst == PROJ_W
    return tiles


_W_SEGS = _w_in_segments()


def _np_consts():
    t = np.arange(CHUNK)
    lower = (t[:, None] >= t[None, :])
    incl = np.stack([lower, lower.T])
    strict = np.stack([t[:, None] > t[None, :], t[:, None] < t[None, :]])
    eye4 = np.eye(4, dtype=bool)
    blk = np.kron(eye4, np.ones((CHUNK, CHUNK), bool))
    incl_bd = np.stack([np.kron(eye4, incl[d]) for d in range(2)])
    strict_bd = np.stack([np.kron(eye4, strict[d]) for d in range(2)])
    levels = []
    for d in range(2):
        per = []
        for lv in range(6):
            s = 1 << lv
            same = (t[:, None] // (2 * s)) == (t[None, :] // (2 * s))
            hi = (t % (2 * s)) >= s
            m = same & hi[:, None] & (~hi)[None, :]
            if d == 1:
                m = m.T
            per.append(np.kron(eye4, m))
        levels.append(np.stack(per))
    levels = np.stack(levels)
    assert (levels.sum(1) == strict_bd).all() and blk.any()
    gla_mask = np.stack([np.tile(incl[d], (1, 4)) for d in range(2)])
    hm_k = np.kron(eye4, np.ones((1, GLA_DK)))
    hm_v = np.kron(eye4, np.ones((1, GLA_DV)))
    st_mask = np.kron(eye4, np.ones((GLA_DV, GLA_DK)))
    ind64 = np.kron(eye4, np.full((GLA_DV, GLA_DV), 1.0 / GLA_DV))
    gla_sel, gla_later, gla_earlier, gla_pair = [], [], [], []
    for d in range(2):
        pos = t if d == 0 else CHUNK - 1 - t
        cum = (pos[None, :] <= pos[:, None]).astype(np.float64)
        sel, later, earlier, pair = [cum], [], [], []
        for lv in range(6):
            s = 1 << lv
            ref_pos = (pos // (2 * s)) * (2 * s) + s - 1
            pick = (pos[None, :] == ref_pos[:, None]).astype(np.float64)
            sel.append(pick @ cum)
            is_later = (pos % (2 * s)) >= s
            same = (pos[:, None] // (2 * s)) == (pos[None, :] // (2 * s))
            later.append(np.repeat(is_later[:, None], GLA_KW, axis=1))
            earlier.append(np.repeat(~is_later[:, None], GLA_KW, axis=1))
            pair.append(np.tile(same & is_later[:, None] & (~is_later)[None, :], (1, 4)))
        gla_sel.append(np.concatenate(sel, axis=0))
        gla_later.append(np.stack(later))
        gla_earlier.append(np.stack(earlier))
        gla_pair.append(np.stack(pair))
        assert (np.stack(pair).sum(0) == np.tile(pos[:, None] > pos[None, :], (1, 4))).all()
    head_sum = np.kron(eye4, np.ones((GLA_DK, GLA_DV)))
    return dict(incl=incl, incl_bd=incl_bd, strict_bd=strict_bd, levels=levels, gla_mask=gla_mask,
                hm_k=hm_k, hm_v=hm_v, st_mask=st_mask, ind64=ind64, eye256=np.eye(256),
                gla_sel=np.stack(gla_sel), gla_later=np.stack(gla_later), gla_earlier=np.stack(gla_earlier),
                gla_pair=np.stack(gla_pair), head_sum=head_sum)


_C = _np_consts()


def _row_block_index(nblk):
    return lambda d, b, i: b * nblk + i + d * (nblk - 1 - 2 * i)


def _gla_kernel(qkf_ref, qkb_ref, vf_ref, vb_ref, smf_ref, smb_ref, up_ref, bias_ref, sel_ref, later_ref,
                earlier_ref, pair_ref, hmk_ref, hmv_ref, stm_ref, hsum_ref, *rest, nchunk, zero_init, chained):
    rest = list(rest)
    st0_ref = None if zero_init else rest.pop(0)
    if chained:
        rest.pop(0)
    of_ref, ob_ref, stf_ref, st_ref, qd_sc, m_sc, dec_sc = rest
    ins =((qkf_ref, vf_ref, smf_ref, of_ref), (qkb_ref, vb_ref, smb_ref, ob_ref))

    @pl.when(pl.program_id(1) == 0)
    def _():
        st_ref[...] = jnp.zeros_like(st_ref) if zero_init else st0_ref[...]

    hmkb = hmk_ref[...].astype(BF16)
    hmv = hmv_ref[...]
    stm = stm_ref[...]

    units = [(dd, c) for c in range(nchunk) for dd in range(2)]
    st1 = []
    for dd, c in units:
        qk_ref, v_ref, sm_ref, _ = ins[dd]
        rows = pl.ds(c * CHUNK, CHUNK)
        qk = qk_ref[rows, :]
        q = qk[:, 0:GLA_KW] * (GLA_DK ** -0.5)
        k = qk[:, GLA_KW:2 * GLA_KW]
        x = _bdot(sm_ref[rows, :], up_ref[dd]) + bias_ref[dd]
        lg = _log_sigmoid(x) * (1.0 / GLA_GATE_NORM)
        ball = _sel_dot(sel_ref[dd], lg)
        st1.append((q, k, v_ref[rows, :], ball, jnp.sum(lg, axis=0, keepdims=True)))
    prods = []
    for lv in range(6):
        for (dd, c), (q, k, _, ball, _) in zip(units, st1):
            b = ball[0:CHUNK]
            ref = ball[(lv + 1) * CHUNK:(lv + 2) * CHUNK]
            ql = q * jnp.where(later_ref[dd, lv] > 0.5, jnp.exp(jnp.minimum(b - ref, 0.0)), 0.0)
            kl = (k * jnp.where(earlier_ref[dd, lv] > 0.5, jnp.exp(jnp.minimum(ref - b, 0.0)), 0.0)).astype(BF16)
            kexp = jnp.concatenate([kl * hmkb[h:h + 1] for h in range(GLA_HEADS)], axis=0)
            prods.append(_bdot_nt(ql, kexp))
    for i, ((dd, c), (q, k, v, ball, btot)) in enumerate(zip(units, st1)):
        o_ref = ins[dd][3]
        n = dd * nchunk + c
        b = ball[0:CHUNK]
        att = jnp.zeros((CHUNK, GLA_HEADS * CHUNK), F32)
        for lv in range(6):
            att = jnp.where(pair_ref[dd, lv] > 0.5, prods[lv * len(units) + i], att)
        vexp = jnp.concatenate([v * hmv[h:h + 1] for h in range(GLA_HEADS)], axis=0)
        o_ref[pl.ds(c * CHUNK, CHUNK), :] = _bdot(att, vexp) + _bdot(q * k, hsum_ref[...]) * v
        qd_sc[n] = (q * jnp.exp(b)).astype(BF16)
        kd = k * jnp.exp(btot - b)
        m_sc[n] = _bdot(v.T, kd) * stm
        dec_sc[n] = jnp.broadcast_to(jnp.exp(btot), (8, GLA_KW))

    for dd in range(2):
        o_ref = ins[dd][3]
        st = st_ref[dd]
        for ci in range(nchunk):
            c = ci if dd == 0 else nchunk - 1 - ci
            n = dd * nchunk + c
            rows = pl.ds(c * CHUNK, CHUNK)
            o_ref[rows, :] += _nt(qd_sc[n], st.astype(BF16))
            st = st * dec_sc[n][0:1] + m_sc[n]
        st_ref[dd] = st
        for h in range(GLA_HEADS):
            stf_ref[dd, h] = st[h * GLA_DV:(h + 1) * GLA_DV, h * GLA_DK:(h + 1) * GLA_DK].T


def _gla(proj, up_p, bias_p, st0, fin, *, layer, nseq, n_seq):
    t = proj.shape[0]
    nblk = n_seq // SUB
    nck = SUB // CHUNK
    rbi = _row_block_index(nblk)
    c = _C
    consts = [jnp.asarray(c["gla_sel"], BF16), jnp.asarray(c["gla_later"], F32), jnp.asarray(c["gla_earlier"], F32),
              jnp.asarray(c["gla_pair"], F32), jnp.asarray(c["hm_k"], F32), jnp.asarray(c["hm_v"], F32),
              jnp.asarray(c["st_mask"], F32), jnp.asarray(c["head_sum"], BF16)]
    st_spec = pl.BlockSpec((2, None, GLA_W, GLA_KW), lambda b, i: (0, b, 0, 0))
    full = lambda a: pl.BlockSpec(a.shape, lambda b, i: (0,) * a.ndim)
    zero_init = st0 is None
    nu = 2 * nck
    rows = (lambda b, i: rbi(0, b, i), lambda b, i: rbi(1, b, i))

    def blk(width, col, dd):
        return pl.BlockSpec((SUB, width), lambda b, i: (rows[dd](b, i), col // width))

    hs = (GLA_HEADS, GLA_DK, GLA_DV)
    args = [proj, proj, proj, proj, proj, proj, up_p, bias_p, *consts] + ([] if zero_init else [st0])
    in_specs = ([blk(2 * GLA_KW, P_GQK, 0), blk(2 * GLA_KW, P_GQK, 1), blk(GLA_W, P_GV, 0), blk(GLA_W, P_GV, 1),
                 blk(128, P_SMALL, 0), blk(128, P_SMALL + 128, 1), full(up_p), full(bias_p)]
                + [full(a) for a in consts] + ([] if zero_init else [st_spec]))
    return pl.pallas_call(
        functools.partial(_gla_kernel, nchunk=nck, zero_init=zero_init, chained=fin is not None),
        grid=(nseq, nblk),
        in_specs=in_specs + _chain_in(fin),
        out_specs=[pl.BlockSpec((SUB, GLA_W), lambda b, i: (rows[0](b, i), 0)),
                   pl.BlockSpec((SUB, GLA_W), lambda b, i: (rows[1](b, i), 0)), _fin_spec(hs, layer)],
        out_shape=[jax.ShapeDtypeStruct((t, GLA_W), F32), jax.ShapeDtypeStruct((t, GLA_W), F32),
                   jax.ShapeDtypeStruct((nseq, DEPTH, 2) + hs, F32)],
        scratch_shapes=[pltpu.VMEM((2, GLA_W, GLA_KW), F32), pltpu.VMEM((nu, CHUNK, GLA_KW), BF16),
                        pltpu.VMEM((nu, GLA_W, GLA_KW), F32), pltpu.VMEM((nu, 8, GLA_KW), F32)],
        input_output_aliases={} if fin is None else {len(args): 2},
        compiler_params=_cparams(("arbitrary", "arbitrary")),
        name="gla",
    )(*args, *([] if fin is None else [fin]))


def _fin_spec(hs, layer):
    return pl.BlockSpec((None, None, 2) + hs, lambda b, i: (b, layer, 0) + (0,) * len(hs))


def _chain_in(fin):
    return [] if fin is None else [pl.BlockSpec(memory_space=pl.ANY)]


def _gdn_kernel(qkvf_ref, qkvb_ref, smf_ref, smb_ref, arow_ref, dtb_ref, tri_ref, incl_ref, strict_ref, lvl_ref,
                eye_ref, *rest, nchunk, zero_init, chained):
    rest = list(rest)
    s0_ref = None if zero_init else rest.pop(0)
    if chained:
        rest.pop(0)
    of_ref, ob_ref, sf_ref, s_ref, la_ref, be_ref, w_sc, u0_sc, qkd_sc, qe_sc, kdt_sc, egl_sc = rest
    qkv_refs = (qkvf_ref, qkvb_ref)
    o_refs = (of_ref, ob_ref)

    @pl.when(pl.program_id(1) == 0)
    def _():
        s_ref[...] = jnp.zeros_like(s_ref) if zero_init else s0_ref[...]

    for dd, sm_ref in enumerate((smf_ref, smb_ref)):
        sm = sm_ref[...]
        la_ref[dd] = -jnp.exp(arow_ref[dd]) * _softplus(sm + dtb_ref[dd])
        be_ref[dd] = jax.nn.sigmoid(sm)

    eye = eye_ref[...]

    def stack(ref, r0, lo, width):
        return jnp.concatenate([ref[pl.ds(r0, CHUNK), lo + h * width:lo + (h + 1) * width]
                                for h in range(GDN_HEADS)], axis=0)

    def colstack(a, lo):
        return jnp.concatenate([a[:, lo + h:lo + h + 1] for h in range(GDN_HEADS)], axis=0)

    units = [(dd, c) for c in range(nchunk) for dd in range(2)]
    st1 = []
    for dd, c in units:
        r0 = c * CHUNK
        qkv_ref = qkv_refs[dd]
        qs = stack(qkv_ref, r0, 0, GDN_DK)
        ks = stack(qkv_ref, r0, GDN_W, GDN_DK)
        la = la_ref[dd, pl.ds(r0, CHUNK), :]
        g = _sel_dot(tri_ref[dd], la)
        ksb = ks.astype(BF16)
        st1.append((qs, ks, la, g, _bdot_nt(ksb, ksb), _bdot_nt(qs, ksb)))
    loc = []
    for (dd, c), (qs, ks, la, g, kk, qk) in zip(units, st1):
        r0 = c * CHUNK
        n = dd * nchunk + c
        vs = stack(qkv_refs[dd], r0, 2 * GDN_W, GDN_DV)
        gtot = jnp.sum(la, axis=0, keepdims=True)
        gcol = colstack(g, SM_A)
        kfcol = colstack(jnp.exp(gtot - g), SM_A)
        egcol = colstack(jnp.exp(g), SM_A)
        bcol = colstack(be_ref[dd, pl.ds(r0, CHUNK), :], SM_B)
        grow = jnp.sum(eye * gcol, axis=0, keepdims=True)
        lm_rows, qkd_rows = [], []
        for h in range(GDN_HEADS):
            rs = slice(h * CHUNK, (h + 1) * CHUNK)
            lt = (h * CHUNK) // 128
            ls = slice(lt * 128, (lt + 1) * 128)
            dec = incl_ref[dd, rs, ls] * jnp.exp(jnp.minimum(gcol[rs] - grow[:, ls], 0.0))
            lm = strict_ref[dd, rs, ls] * dec * kk[rs, ls] * bcol[rs]
            qd = qk[rs, ls] * dec
            zero = jnp.zeros_like(lm)
            lm_rows.append(jnp.concatenate([lm, zero] if lt == 0 else [zero, lm], axis=1))
            qkd_rows.append(jnp.concatenate([qd, zero] if lt == 0 else [zero, qd], axis=1))
        lmat = jnp.concatenate(lm_rows, axis=0)
        rhs = jnp.concatenate([ks * (bcol * egcol), vs * bcol], axis=1).astype(BF16)
        qkd_sc[n] = jnp.concatenate(qkd_rows, axis=0).astype(BF16)
        qe_sc[n] = (qs * egcol).astype(BF16)
        kd = ks * kfcol
        for h in range(GDN_HEADS):
            kdt_sc[n * GDN_HEADS + h] = kd[h * CHUNK:(h + 1) * CHUNK].T.astype(BF16)
        egl_sc[n] = jnp.broadcast_to(jnp.exp(gtot), (8, 128))
        loc.append((lmat.astype(BF16), rhs))
    def inv_start(wave):
        return [eye - (loc[i][0] * lvl_ref[units[i][0], 0]).astype(F32) for i in wave]

    def inv_level(wave, tinv, lv):
        tb = [t_.astype(BF16) for t_ in tinv]
        cs = [jnp.dot(loc[i][0] * lvl_ref[units[i][0], lv], t_, preferred_element_type=F32)
              for i, t_ in zip(wave, tb)]
        return [t32 - jnp.dot(t_, c_.astype(BF16), preferred_element_type=F32)
                for t32, t_, c_ in zip(tinv, tb, cs)]

    def solve(wave, tinv):
        for i, t_ in zip(wave, tinv):
            dd, c = units[i]
            sol = jnp.dot(t_.astype(BF16), loc[i][1], preferred_element_type=F32)
            w_sc[dd * nchunk + c] = sol[:, :GDN_DK].astype(BF16)
            u0_sc[dd * nchunk + c] = sol[:, GDN_DK:]

    s_cur = [[s_ref[dd, h] for h in range(GDN_HEADS)] for dd in range(2)]

    def recur(ci):
        step = []
        for dd in range(2):
            c = ci if dd == 0 else nchunk - 1 - ci
            n = dd * nchunk + c
            sb = [s_.astype(BF16) for s_ in s_cur[dd]]
            ub = []
            for h in range(GDN_HEADS):
                rows = pl.ds(h * CHUNK, CHUNK)
                u = u0_sc[n, rows, :] - jnp.dot(w_sc[n, rows, :], sb[h], preferred_element_type=F32)
                ub.append(u.astype(BF16))
            step.append((c, n, sb, ub))
        for dd, (c, n, sb, ub) in enumerate(step):
            egl = egl_sc[n]
            for h in range(GDN_HEADS):
                s_cur[dd][h] = (egl[0:1, SM_A + h:SM_A + h + 1] * s_cur[dd][h]
                                + jnp.dot(kdt_sc[n * GDN_HEADS + h], ub[h], preferred_element_type=F32))
        for dd, (c, n, sb, ub) in enumerate(step):
            o2 = jnp.dot(qkd_sc[n], jnp.concatenate(ub, axis=0), preferred_element_type=F32)
            for h in range(GDN_HEADS):
                rows = pl.ds(h * CHUNK, CHUNK)
                o_refs[dd][pl.ds(c * CHUNK, CHUNK), h * GDN_DV:(h + 1) * GDN_DV] = (
                    jnp.dot(qe_sc[n, rows, :], sb[h], preferred_element_type=F32) + o2[h * CHUNK:(h + 1) * CHUNK])

    every = list(range(len(units)))
    tinv = inv_start(every)
    for lv in range(1, 6):
        tinv = inv_level(every, tinv, lv)
    solve(every, tinv)
    for ci in range(nchunk):
        recur(ci)
    for dd in range(2):
        for h in range(GDN_HEADS):
            s_ref[dd, h] = s_cur[dd][h]
    sf_ref[...] = s_ref[...]


def _gdn(proj, arow, dtb, s0, fin, *, layer, nseq, n_seq):
    t = proj.shape[0]
    nblk = n_seq // SUB
    rbi = _row_block_index(nblk)
    c = _C
    consts = [jnp.asarray(c["incl"], BF16), jnp.asarray(c["incl_bd"], F32), jnp.asarray(c["strict_bd"], F32),
              jnp.asarray(c["levels"], BF16), jnp.asarray(c["eye256"], F32)]
    hs = (GDN_HEADS, GDN_DK, GDN_DV)
    nck = SUB // CHUNK
    nu = 2 * nck
    zero_init = s0 is None
    s_spec = pl.BlockSpec((2, None) + hs, lambda b, i: (0, b, 0, 0, 0))
    full = lambda shape: pl.BlockSpec(shape, lambda b, i: (0,) * len(shape))
    rows = (lambda b, i: rbi(0, b, i), lambda b, i: rbi(1, b, i))
    args = [proj, proj, proj, proj, arow, dtb, *consts] + ([] if zero_init else [s0])
    return pl.pallas_call(
        functools.partial(_gdn_kernel, nchunk=nck, zero_init=zero_init, chained=fin is not None),
        grid=(nseq, nblk),
        in_specs=[pl.BlockSpec((SUB, 3 * GDN_W), lambda b, i: (rows[0](b, i), P_DQKV // (3 * GDN_W))),
                  pl.BlockSpec((SUB, 3 * GDN_W), lambda b, i: (rows[1](b, i), P_DQKV // (3 * GDN_W))),
                  pl.BlockSpec((SUB, 128), lambda b, i: (rows[0](b, i), P_SMALL // 128)),
                  pl.BlockSpec((SUB, 128), lambda b, i: (rows[1](b, i), P_SMALL // 128 + 1)),
                  full((2, 1, 128)), full((2, 1, 128)), full((2, CHUNK, CHUNK)),
                  full((2, 256, 256)), full((2, 256, 256)), full((2, 6, 256, 256)), full((256, 256))]
                 + ([] if zero_init else [s_spec]) + _chain_in(fin),
        out_specs=[pl.BlockSpec((SUB, GDN_W), lambda b, i: (rows[0](b, i), 0)),
                   pl.BlockSpec((SUB, GDN_W), lambda b, i: (rows[1](b, i), 0)), _fin_spec(hs, layer)],
        out_shape=[jax.ShapeDtypeStruct((t, GDN_W), F32), jax.ShapeDtypeStruct((t, GDN_W), F32),
                   jax.ShapeDtypeStruct((nseq, DEPTH, 2) + hs, F32)],
        input_output_aliases={} if fin is None else {len(args): 2},
        scratch_shapes=[pltpu.VMEM((2,) + hs, F32),
                        pltpu.VMEM((2, SUB, 128), F32), pltpu.VMEM((2, SUB, 128), F32),
                        pltpu.VMEM((nu, 256, GDN_DK), BF16), pltpu.VMEM((nu, 256, GDN_DV), F32),
                        pltpu.VMEM((nu, 256, 256), BF16), pltpu.VMEM((nu, 256, GDN_DK), BF16),
                        pltpu.VMEM((nu * GDN_HEADS, GDN_DK, CHUNK), BF16), pltpu.VMEM((nu, 8, 128), F32)],
        compiler_params=_cparams(("arbitrary", "arbitrary")),
        name="gdn",
    )(*args, *([] if fin is None else [fin]))


def _cmul(ar, ai, br, bi):
    return ar * br - ai * bi, ar * bi + ai * br


def _split2(x):
    hi = x.astype(BF16)
    return hi, (x - hi.astype(F32)).astype(BF16)


def _s5_consts_kernel(lr_ref, li_ref, st_ref, br_ref, bi_ref, cr_ref, ci_ref, gm_ref, bm_ref,
                      laml_ref, wexp_ref, vexp_ref, tt_ref, w_ref, v_ref, a_ref, q_ref):
    d = pl.program_id(0) % 2
    lc = S5_L * S5_CH

    def trows(s):
        return pl.ds(pl.multiple_of((s + d * (S5_L - 1 - 2 * s)) * S5_CH, S5_CH), S5_CH)

    lr = lr_ref[...]
    li = li_ref[...]
    step = jnp.exp(st_ref[...])
    mag = jnp.exp(lr * step)
    lbr = mag * jnp.cos(li * step)
    lbi = mag * jnp.sin(li * step)
    nr = lbr - 1.0
    den = lr * lr + li * li
    fr = (nr * lr + lbi * li) / den
    fi = (lbi * lr - nr * li) / den
    bbr, bbi = _cmul(fr, fi, br_ref[...], bi_ref[...])
    cr = cr_ref[...]
    ci = ci_ref[...]
    pr = jnp.ones_like(lr)
    pi = jnp.zeros_like(lr)
    for s in range(S5_L):
        ar, ai = _cmul(cr, ci, pr, pi)
        a_ref[0, trows(s), :] = ar
        a_ref[1, trows(s), :] = ai
        wr, wi = _cmul(pr, pi, bbr, bbi)
        w_ref[0, trows(S5_L - 1 - s), :] = wr
        w_ref[1, trows(S5_L - 1 - s), :] = wi
        pr, pi = _cmul(pr, pi, lbr, lbi)
        vr, vi = _cmul(cr, ci, pr, pi)
        v_ref[0, trows(s), :] = vr
        v_ref[1, trows(s), :] = -vi
    laml_ref[0:1, :] = pr
    laml_ref[1:2, :] = pi

    lane = lax.broadcasted_iota(jnp.int32, (1, 128), 1)
    for gp in range(S5_PAIRS):
        lanes = slice(gp * 128, (gp + 1) * 128)
        for r in range(2):
            for src, dst in ((w_ref, wexp_ref), (v_ref, vexp_ref)):
                slab = src[r, :, lanes]
                dst[r, gp] = jnp.concatenate([jnp.where(lane < S5_P, slab, 0.0),
                                              jnp.where(lane >= S5_P, slab, 0.0)], axis=0).astype(BF16)

    gm = gm_ref[...]
    kall = None
    for r, b, sign in ((0, bbr, 1.0), (1, bbi, -1.0)):
        ah, al = _split2(a_ref[r])
        bh, bl = _split2(jnp.concatenate([b] * S5_GROUPS, axis=0) * gm)
        nt = lambda x, y: lax.dot_general(x, y, (((1,), (1,)), ((), ())), preferred_element_type=F32)
        term = (nt(ah, bh) + nt(ah, bl) + nt(al, bh)) * sign
        kall = term if kall is None else kall + term

    zeros = jnp.zeros((lc, S5_W), F32)
    for delta in range(S5_GROUPS):
        q_ref[delta, 0:lc, :] = zeros
        q_ref[delta, lc:2 * lc, :] = kall if delta == 0 else pltpu.roll(kall, delta * S5_CH, axis=1)
        q_ref[delta, 2 * lc:3 * lc, :] = zeros
    for g in range(S5_GROUPS):
        acc = zeros
        for s in range(S5_L):
            start = pl.multiple_of(lc - S5_CH * s + d * (S5_CH * (S5_L - 1)), S5_CH)
            blk = q_ref[(s - g) % S5_GROUPS, pl.ds(start, lc), :]
            acc = jnp.where(bm_ref[s:s + 1, :] > 0.5, blk, acc)
        tt_ref[g] = acc.astype(BF16)


def _s5_consts(lam_re, lam_im, log_step, b_re, b_im, c_re, c_im):
    n = DEPTH * 2
    lc = S5_L * S5_CH
    lr = lam_re.reshape(n, 1, S5_N)
    li = lam_im.reshape(n, 1, S5_N)
    st = jnp.repeat(log_step.reshape(n, S5_GROUPS), S5_P, axis=-1).reshape(n, 1, S5_N)
    br = b_re.reshape(n, S5_N, S5_CH).transpose(0, 2, 1)
    bi = b_im.reshape(n, S5_N, S5_CH).transpose(0, 2, 1)
    cr = c_re.reshape(n, S5_GROUPS, S5_CH, S5_P).transpose(0, 2, 1, 3).reshape(n, S5_CH, S5_N)
    ci = c_im.reshape(n, S5_GROUPS, S5_CH, S5_P).transpose(0, 2, 1, 3).reshape(n, S5_CH, S5_N)
    gm = jnp.asarray(np.kron(np.eye(S5_GROUPS), np.ones((S5_CH, S5_P))), F32)
    row = pl.BlockSpec((None, 1, S5_N), lambda i: (i, 0, 0))
    mat = pl.BlockSpec((None, S5_CH, S5_N), lambda i: (i, 0, 0))
    exp = pl.BlockSpec((None, 2, S5_PAIRS, 2 * lc, 128), lambda i: (i, 0, 0, 0, 0))
    laml, wexp, vexp, tt = pl.pallas_call(
        _s5_consts_kernel,
        grid=(n,),
        in_specs=[row, row, row, mat, mat, mat, mat, pl.BlockSpec((S5_W, S5_N), lambda i: (0, 0)),
                  pl.BlockSpec((S5_GROUPS, S5_W), lambda i: (0, 0))],
        out_specs=[pl.BlockSpec((None, 2, S5_N), lambda i: (i, 0, 0)), exp, exp,
                   pl.BlockSpec((None, S5_GROUPS, lc, lc), lambda i: (i, 0, 0, 0))],
        out_shape=[jax.ShapeDtypeStruct((n, 2, S5_N), F32),
                   jax.ShapeDtypeStruct((n, 2, S5_PAIRS, 2 * lc, 128), BF16),
                   jax.ShapeDtypeStruct((n, 2, S5_PAIRS, 2 * lc, 128), BF16),
                   jax.ShapeDtypeStruct((n, S5_GROUPS, lc, lc), BF16)],
        scratch_shapes=[pltpu.VMEM((2, lc, S5_N), F32), pltpu.VMEM((2, lc, S5_N), F32),
                        pltpu.VMEM((2, lc, S5_N), F32), pltpu.VMEM((S5_GROUPS, 3 * lc, S5_W), F32)],
        compiler_params=_cparams(("arbitrary",)),
        name="s5_consts",
    )(lr, li, st, br, bi, cr, ci, gm, _lane_block_masks())
    return laml.reshape(n, 2, S5_PAIRS, 128), wexp, vexp, tt


def _lane_block_masks():
    return jnp.asarray(np.kron(np.eye(S5_GROUPS), np.ones((1, S5_CH))), F32)


def _nt(a, b):
    return lax.dot_general(a, b, (((1,), (1,)), ((), ())), preferred_element_type=F32)


def _lane_block_transpose(arrs):
    n = len(arrs)
    width = arrs[0].shape[1]
    blk = lax.broadcasted_iota(jnp.int32, (1, width), 1) // (width // n)
    arrs = list(arrs)
    h = n // 2
    while h >= 1:
        hi = (blk & h) != 0
        for a in range(n):
            if a & h:
                continue
            lo_arr, hi_arr = arrs[a], arrs[a + h]
            arrs[a] = jnp.where(hi, pltpu.roll(hi_arr, h * (width // n), axis=1), lo_arr)
            arrs[a + h] = jnp.where(hi, hi_arr, pltpu.roll(lo_arr, width - h * (width // n), axis=1))
        h //= 2
    return arrs


def _s5_kernel(u0_ref, u1_ref, tt_ref, wexp_ref, vexp_ref, laml_ref, *rest, nseq, nblk, zero_init):
    if zero_init:
        y_ref, hf_ref, ug_sc, yg_sc, hin_re, hin_im, hp_re, hp_im = rest
    else:
        h0_ref, y_ref, hf_ref, ug_sc, yg_sc, hin_re, hin_im, hp_re, hp_im = rest
    d = pl.program_id(0)
    r = nblk * nseq
    lc = S5_L * S5_CH

    @pl.when(d == 0)
    def _():
        xs = []
        for s in range(S5_L):
            rows = pl.ds(s, r, stride=S5_L)
            xs.append(jnp.concatenate([u0_ref[rows, :], u1_ref[rows, :]], axis=1).astype(BF16))
        for g, ug in enumerate(_lane_block_transpose(xs)):
            ug_sc[g] = ug

    for gp in range(S5_PAIRS):
        up = jnp.concatenate([ug_sc[2 * gp], ug_sc[2 * gp + 1]], axis=1)
        rows = pl.ds(gp, r, stride=S5_PAIRS)
        hin_re[rows, :] = jnp.dot(up, wexp_ref[0, gp], preferred_element_type=F32)
        hin_im[rows, :] = jnp.dot(up, wexp_ref[1, gp], preferred_element_type=F32)

    lr = laml_ref[0]
    li = laml_ref[1]

    def step(kk, carry):
        k = kk + d * (nblk - 1 - 2 * kk)
        new = []
        for b in range(nseq):
            rows = pl.ds(pl.multiple_of((b * nblk + k) * S5_PAIRS, S5_PAIRS), S5_PAIRS)
            hr, hi = carry[2 * b], carry[2 * b + 1]
            hp_re[rows, :] = hr
            hp_im[rows, :] = hi
            new += [lr * hr - li * hi + hin_re[rows, :], lr * hi + li * hr + hin_im[rows, :]]
        return tuple(new)

    init = []
    for b in range(nseq):
        rows = slice(b * S5_PAIRS, (b + 1) * S5_PAIRS)
        init += ([jnp.zeros((S5_PAIRS, 128), F32)] * 2 if zero_init else [h0_ref[0, rows, :], h0_ref[1, rows, :]])
    fin = lax.fori_loop(0, nblk, step, tuple(init))
    for b in range(nseq):
        rows = slice(b * S5_PAIRS, (b + 1) * S5_PAIRS)
        hf_ref[0, rows, :] = fin[2 * b]
        hf_ref[1, rows, :] = fin[2 * b + 1]

    for gp in range(S5_PAIRS):
        rows = pl.ds(gp, r, stride=S5_PAIRS)
        ys = (_nt(hp_re[rows, :].astype(BF16), vexp_ref[0, gp])
              + _nt(hp_im[rows, :].astype(BF16), vexp_ref[1, gp]))
        for j in range(2):
            g = 2 * gp + j
            yg = _nt(ug_sc[g], tt_ref[g]) + ys[:, j * lc:(j + 1) * lc]

            @pl.when(d == 0)
            def _():
                yg_sc[g] = yg

            @pl.when(d == 1)
            def _():
                yg_sc[g] += yg

    @pl.when(d == 1)
    def _():
        for t, yt in enumerate(_lane_block_transpose([yg_sc[g] for g in range(S5_GROUPS)])):
            rows = pl.ds(t, r, stride=S5_L)
            y_ref[0, rows, :] = yt[:, 0:128]
            y_ref[1, rows, :] = yt[:, 128:2 * 128]


def _s5(proj, consts, h0, *, layer, nseq, n_seq):
    laml, wexp, vexp, tt = consts
    t = proj.shape[0]
    nblk = n_seq // S5_L
    r = nblk * nseq
    lc = S5_L * S5_CH
    bs = nseq * S5_PAIRS
    zero_init = h0 is None
    h_spec = pl.BlockSpec((None, 2, bs, 128), lambda d: (d, 0, 0, 0))
    return pl.pallas_call(
        functools.partial(_s5_kernel, nseq=nseq, nblk=nblk, zero_init=zero_init),
        grid=(2,),
        in_specs=[pl.BlockSpec((t, 128), lambda d: (0, P_SU // 128)),
                  pl.BlockSpec((t, 128), lambda d: (0, P_SU // 128 + 1)),
                  pl.BlockSpec((None, S5_GROUPS, lc, lc), lambda d: (2 * layer + d, 0, 0, 0)),
                  pl.BlockSpec((None, 2, S5_PAIRS, 2 * lc, 128), lambda d: (2 * layer + d, 0, 0, 0, 0)),
                  pl.BlockSpec((None, 2, S5_PAIRS, 2 * lc, 128), lambda d: (2 * layer + d, 0, 0, 0, 0)),
                  pl.BlockSpec((None, 2, S5_PAIRS, 128), lambda d: (2 * layer + d, 0, 0, 0))]
                 + ([] if zero_init else [h_spec]),
        out_specs=[pl.BlockSpec((2, t, 128), lambda d: (0, 0, 0)), h_spec],
        out_shape=[jax.ShapeDtypeStruct((2, t, 128), F32),
                   jax.ShapeDtypeStruct((2, 2, bs, 128), F32)],
        scratch_shapes=[pltpu.VMEM((S5_GROUPS, r, lc), BF16), pltpu.VMEM((S5_GROUPS, r, lc), F32)]
                       + [pltpu.VMEM((r * S5_PAIRS, 128), F32) for _ in range(4)],
        compiler_params=_cparams(("arbitrary",)),
        name="s5",
    )(proj, proj, tt, wexp, vexp, laml, *([] if zero_init else [h0]))


def _out_kernel(x_ref, mod_ref, ogf_ref, ogb_ref, odf_ref, odb_ref, ysf_ref, ysb_ref, gog_ref, su_ref, dog_ref,
                ind_ref, gw_ref, dw_ref, sd_ref, wglu_ref, bglu_ref, wout32_ref, o_ref, wout_ref):
    @pl.when(pl.program_id(0) == 0)
    def _():
        wout_ref[...] = wout32_ref[...].astype(BF16)

    og = ogf_ref[...] + ogb_ref[...]
    hi, mid, _ = _split3(og * og)
    ms = jnp.dot(hi, ind_ref[...], preferred_element_type=F32) + jnp.dot(mid, ind_ref[...], preferred_element_type=F32)
    og = og * lax.rsqrt(ms + EPS) * gw_ref[...] * _silu(gog_ref[...])
    od = odf_ref[...] + odb_ref[...]
    dog = dog_ref[...]
    parts = []
    for h in range(GDN_HEADS):
        z = od[:, h * GDN_DV:(h + 1) * GDN_DV]
        z = z * lax.rsqrt(jnp.mean(z * z, axis=-1, keepdims=True) + EPS) * dw_ref[...]
        parts.append(z * _silu(dog[:, h * GDN_DV:(h + 1) * GDN_DV]))
    y = su_ref[...] * sd_ref[...] + jnp.concatenate([ysf_ref[...], ysb_ref[...]], axis=1)
    g = 0.5 * y * (1.0 + jnp.tanh(math.sqrt(2.0 / math.pi) * (y + 0.044715 * (y * y * y))))
    os_ = g * jax.nn.sigmoid(_bdot(g, wglu_ref[...]) + bglu_ref[...])
    out = _bdot(og, wout_ref[0:GLA_W, :]) + _bdot(os_, wout_ref[GLA_W + GDN_W:MIX_W, :])
    for h in range(GDN_HEADS):
        lo = GLA_W + h * GDN_DV
        out = out + _bdot(parts[h], wout_ref[lo:lo + GDN_DV, :])
    o_ref[...] = x_ref[...] + mod_ref[5:6, :] * out


def _mix_out(x, mods, proj, og, od, ys, gla_nw, gdn_nw, s5_d, w_glu, b_glu, w_out, *, layer, n_seq):
    t, d = x.shape
    nb = mods.shape[0]
    tm = _pick_tile(n_seq if nb > 1 else t, 512)
    midx = _mod_index(nb, tm, n_seq)
    ys2 = ys

    def dirspec(w, dd):
        return pl.BlockSpec((None, tm, w), lambda i: (dd, i, 0))

    def vec(w):
        return pl.BlockSpec((None, 1, w), lambda i: (layer, 0, 0))

    return pl.pallas_call(
        _out_kernel,
        grid=(t // tm,),
        in_specs=[pl.BlockSpec((tm, d), lambda i: (i, 0)),
                  pl.BlockSpec((None, N_MOD, d), lambda i: (midx(i), 0, 0)),
                  pl.BlockSpec((tm, GLA_W), lambda i: (i, 0)), pl.BlockSpec((tm, GLA_W), lambda i: (i, 0)),
                  pl.BlockSpec((tm, GDN_W), lambda i: (i, 0)), pl.BlockSpec((tm, GDN_W), lambda i: (i, 0)),
                  dirspec(128, 0), dirspec(128, 1),
                  pl.BlockSpec((tm, GLA_W), lambda i: (i, P_GOG // GLA_W)),
                  pl.BlockSpec((tm, S5_W), lambda i: (i, P_SU // S5_W)),
                  pl.BlockSpec((tm, GDN_W), lambda i: (i, P_DOG // GDN_W)),
                  pl.BlockSpec((GLA_W, GLA_W), lambda i: (0, 0)),
                  vec(GLA_W), vec(GDN_DV), vec(S5_W),
                  pl.BlockSpec((None, S5_W, S5_W), lambda i: (layer, 0, 0)),
                  vec(S5_W),
                  pl.BlockSpec((None, MIX_W, d), lambda i: (layer, 0, 0), pipeline_mode=pl.Buffered(1))],
        out_specs=pl.BlockSpec((tm, d), lambda i: (i, 0)),
        out_shape=jax.ShapeDtypeStruct((t, d), F32),
        scratch_shapes=[pltpu.VMEM((MIX_W, d), BF16)],
        compiler_params=_cparams(("arbitrary",)),
        name="mix_out",
    )(x, mods, og[0], og[1], od[0], od[1], ys2, ys2, proj, proj, proj,
      jnp.asarray(_C["ind64"], BF16),
      jnp.tile(gla_nw, (1, GLA_HEADS)).reshape(DEPTH, 1, GLA_W), gdn_nw.reshape(DEPTH, 1, GDN_DV),
      s5_d.reshape(DEPTH, 1, S5_W), w_glu, b_glu.reshape(DEPTH, 1, S5_W), w_out)


def _gla_params(gk_up, gk_bias, layer):
    up = jnp.zeros((2, 128, GLA_KW), F32).at[:, :GLA_RANK, :].set(gk_up[layer])
    return up, gk_bias[layer].reshape(2, 1, GLA_KW)


def _gdn_params(a_log, dt_bias, layer):
    arow = jnp.zeros((2, 1, 128), F32).at[:, 0, SM_A:SM_A + GDN_HEADS].set(a_log[layer])
    dtb = jnp.zeros((2, 1, 128), F32).at[:, 0, SM_A:SM_A + GDN_HEADS].set(dt_bias[layer])
    return arow, dtb


def _gla_state_in(s):
    eye = jnp.eye(GLA_HEADS, dtype=F32)
    st = jnp.einsum('bzhde,hg->zbhegd', s.astype(F32), eye)
    return st.reshape(2, s.shape[0], GLA_W, GLA_KW)


def _trunk(x3, mods, rows, states, p, s5c, w_in):
    nseq, n_seq, d = x3.shape
    x = x3.reshape(nseq * n_seq, d)
    fin = []
    gla_f = gdn_f = None
    for l in range(DEPTH):
        m = mods[l]
        x = _ffn(x, m, p['norm_w'], p['ffn_w_gate'], p['ffn_w_up'], p['ffn_w_down'], p['final_norm_w'],
                 layer=l, which=0, n_seq=n_seq, final=False)
        proj = _proj(x, m, p['norm_w'], w_in, p['gdn_conv_w'], layer=l, n_seq=n_seq, rowlen=n_seq // rows)
        if states is None:
            st0 = s0 = h0 = None
        else:
            st_gla, st_gdn, st_re, st_im = states
            st0 = _gla_state_in(st_gla[:, l])
            s0 = st_gdn[:, l].transpose(1, 0, 2, 3, 4)
            h0 = jnp.stack([st_re[:, l].reshape(nseq, 2, S5_N), st_im[:, l].reshape(nseq, 2, S5_N)], axis=0)
            h0 = h0.transpose(2, 0, 1, 3).reshape(2, 2, nseq * S5_PAIRS, 128)
        up_p, bias_p = _gla_params(p['gla_gk_up'], p['gla_gk_bias'], l)
        og_f, og_b, gla_f = _gla(proj, up_p, bias_p, st0, gla_f, layer=l, nseq=nseq, n_seq=n_seq)
        og = (og_f, og_b)
        arow, dtb = _gdn_params(p['gdn_a_log'], p['gdn_dt_bias'], l)
        od_f, od_b, gdn_f = _gdn(proj, arow, dtb, s0, gdn_f, layer=l, nseq=nseq, n_seq=n_seq)
        od = (od_f, od_b)
        ys, s5_f = _s5(proj, s5c, h0, layer=l, nseq=nseq, n_seq=n_seq)
        x = _mix_out(x, m, proj, og, od, ys, p['gla_norm_w'], p['gdn_norm_w'], p['s5_d'], p['s5_w_glu'],
                     p['s5_b_glu'], p['w_out'], layer=l, n_seq=n_seq)
        x = _ffn(x, m, p['norm_w'], p['ffn_w_gate'], p['ffn_w_up'], p['ffn_w_down'], p['final_norm_w'],
                 layer=l, which=1, n_seq=n_seq, final=(l == DEPTH - 1))
        fin.append((gla_f, gdn_f, s5_f))
    return x.reshape(nseq, n_seq, d), fin


def kernel(x_prompt, x_sample, c, state_gla, state_gdn, state_s5_re, state_s5_im, c_ctx, w_ada, b_ada, norm_w, ffn_w_gate, ffn_w_up, ffn_w_down, w_in, gla_gk_up, gla_gk_bias, gla_norm_w, gdn_conv_w, gdn_a_log, gdn_dt_bias, gdn_norm_w, s5_lam_re, s5_lam_im, s5_log_step, s5_b_re, s5_b_im, s5_c_re, s5_c_im, s5_d, s5_w_glu, s5_b_glu, w_out, final_norm_w):
    p = dict(norm_w=norm_w, ffn_w_gate=ffn_w_gate, ffn_w_up=ffn_w_up, ffn_w_down=ffn_w_down,
             gla_gk_up=gla_gk_up, gla_gk_bias=gla_gk_bias, gla_norm_w=gla_norm_w, gdn_conv_w=gdn_conv_w,
             gdn_a_log=gdn_a_log, gdn_dt_bias=gdn_dt_bias, gdn_norm_w=gdn_norm_w, s5_c_re=s5_c_re,
             s5_c_im=s5_c_im, s5_d=s5_d, s5_w_glu=s5_w_glu, s5_b_glu=s5_b_glu, w_out=w_out,
             final_norm_w=final_norm_w)
    b_ctx = x_prompt.shape[0]
    b_dec = x_sample.shape[0]
    cond8 = jnp.zeros((8, D_MODEL), F32).at[0].set(c_ctx).at[1:1 + b_dec].set(c)
    mods = _ada(cond8, w_ada, b_ada)
    mods_ctx = mods[:, 0:1].reshape(DEPTH, 1, N_MOD, D_MODEL)
    mods_dec = mods[:, 1:1 + b_dec].reshape(DEPTH, b_dec, N_MOD, D_MODEL)
    s5c = _s5_consts(s5_lam_re, s5_lam_im, s5_log_step, s5_b_re, s5_b_im, s5_c_re, s5_c_im)

    y_prompt, fin = _trunk(x_prompt, mods_ctx, 1, None, p, s5c, w_in)
    new_gla, new_gdn = fin[-1][0], fin[-1][1]
    s5f = jnp.stack([f[2].reshape(2, 2, b_ctx, S5_GROUPS, S5_P).transpose(1, 2, 0, 3, 4) for f in fin], axis=2)
    new_re, new_im = s5f[0], s5f[1]

    rows = x_sample.shape[1] // 64
    y_sample, _ = _trunk(x_sample, mods_dec, rows, (state_gla, state_gdn, state_s5_re, state_s5_im),
                         p, s5c, w_in)
    return (y_prompt, y_sample, new_gla.astype(state_gla.dtype), new_gdn.astype(state_gdn.dtype),
            new_re.astype(state_s5_re.dtype), new_im.astype(state_s5_im.dtype))
```

```python
import functools
import math

import numpy as np
import jax
import jax.numpy as jnp
from jax import lax
from jax.experimental import pallas as pl
from jax.experimental.pallas import tpu as pltpu

F32 = jnp.float32
BF16 = jnp.bfloat16

D_MODEL = 1024
DEPTH = 2
FFN_DIM = 2816
N_MOD = 9
EPS = 1e-6
CHUNK = 64
GLA_HEADS, GLA_DK, GLA_DV, GLA_RANK = 4, 32, 64, 16
GLA_GATE_NORM = 16.0
GDN_HEADS, GDN_DK, GDN_DV, GDN_CONV = 4, 128, 128, 5
S5_GROUPS, S5_CH, S5_P = 16, 16, 64
GLA_W = GLA_HEADS * GLA_DV
GLA_KW = GLA_HEADS * GLA_DK
GDN_W = GDN_HEADS * GDN_DV
S5_W = S5_GROUPS * S5_CH
S5_N = S5_GROUPS * S5_P
S5_L = 16
S5_PAIRS = S5_GROUPS // 2
MIX_W = GLA_W + GDN_W + S5_W

_IN_SIZES = (GLA_KW, GLA_KW, GLA_W, GLA_RANK, GLA_RANK, GLA_W,
             GDN_W, GDN_W, GDN_W, GDN_HEADS, GDN_HEADS, GDN_HEADS, GDN_HEADS, GDN_W, S5_W)
_IN_OFF = tuple(int(v) for v in np.cumsum((0,) + _IN_SIZES))
(_GQ, _GK, _GV, _GLRF, _GLRB, _GOG, _DQ, _DK, _DV, _DAF, _DAB, _DBF, _DBB, _DOG, _SU) = range(15)

P_DQKV = 0
P_SMALL = 1536
P_GQK = 1792
P_GV = 2048
P_GOG = 2304
P_DOG = 2560
P_SU = 3072
PROJ_W = 3328
PROJ_TN = PROJ_W // 2
SM_A = GLA_RANK
SM_B = GLA_RANK + GDN_HEADS

SUB = 256
GLA_SUB = 256
FFN_TM = 2048
FFN_TF = 256
PROJ_TM = 512
MIX_TM = 512
ADA_TN = 1024
V7X_VMEM_BYTES = 64 * 1024 * 1024
VMEM_LIMIT = V7X_VMEM_BYTES - 8 * 1024 * 1024


def _cparams(sem):
    return pltpu.CompilerParams(dimension_semantics=sem, vmem_limit_bytes=VMEM_LIMIT)


def _bdot(a, b):
    return jnp.dot(a.astype(BF16), b.astype(BF16), preferred_element_type=F32)


def _bdot_nt(a, b):
    return lax.dot_general(a.astype(BF16), b.astype(BF16), (((1,), (1,)), ((), ())),
                           preferred_element_type=F32)


def _split3(x):
    hi = x.astype(BF16)
    r1 = x - hi.astype(F32)
    mid = r1.astype(BF16)
    lo = (r1 - mid.astype(F32)).astype(BF16)
    return hi, mid, lo


def _sel_dot(m01, x):
    hi, mid, lo = _split3(x)
    return (jnp.dot(m01, hi, preferred_element_type=F32) + jnp.dot(m01, mid, preferred_element_type=F32)
            + jnp.dot(m01, lo, preferred_element_type=F32))


def _silu(x):
    return x * jax.nn.sigmoid(x)


def _softplus(x):
    return jnp.maximum(x, 0.0) + jnp.log1p(jnp.exp(-jnp.abs(x)))


def _log_sigmoid(x):
    return -_softplus(-x)


def _norm_mod(x, nw, scale, shift):
    y = x * lax.rsqrt(jnp.mean(x * x, axis=-1, keepdims=True) + EPS)
    return y * nw * (1.0 + scale) + shift


def _pick_tile(n, cap):
    t = cap
    while n % t:
        t //= 2
    return t


def _ada_kernel(c_ref, w_ref, b_ref, o_ref):
    o_ref[...] = _bdot(_silu(c_ref[...]), w_ref[...]) + b_ref[...]


def _ada(cond8, w_ada, b_ada):
    depth, d, nout = w_ada.shape
    tn = ADA_TN
    return pl.pallas_call(
        _ada_kernel,
        grid=(depth, nout // tn),
        in_specs=[pl.BlockSpec((8, d), lambda l, j: (0, 0)),
                  pl.BlockSpec((None, d, tn), lambda l, j: (l, 0, j)),
                  pl.BlockSpec((None, 1, tn), lambda l, j: (l, 0, j))],
        out_specs=pl.BlockSpec((None, 8, tn), lambda l, j: (l, 0, j)),
        out_shape=jax.ShapeDtypeStruct((depth, 8, nout), F32),
        compiler_params=_cparams(("parallel", "parallel")),
        name="ada",
    )(cond8, w_ada, b_ada.reshape(depth, 1, nout))


def _ffn_kernel(x_ref, mod_ref, nw_ref, wg_ref, wu_ref, wd_ref, fw_ref, o_ref, h_ref, *, mod_base, final):
    j = pl.program_id(1)

    @pl.when(j == 0)
    def _():
        m = mod_ref[...]
        h = _norm_mod(x_ref[...], nw_ref[...], m[mod_base + 1:mod_base + 2], m[mod_base:mod_base + 1])
        h_ref[...] = h.astype(BF16)
        o_ref[...] = jnp.zeros_like(o_ref)

    h = h_ref[...]
    g = jnp.dot(h, wg_ref[...].astype(BF16), preferred_element_type=F32)
    u = jnp.dot(h, wu_ref[...].astype(BF16), preferred_element_type=F32)
    o_ref[...] += _bdot(_silu(g) * u, wd_ref[...])

    @pl.when(j == pl.num_programs(1) - 1)
    def _():
        m = mod_ref[...]
        y = x_ref[...] + 0.5 * m[mod_base + 2:mod_base + 3] * o_ref[...]
        if final:
            y = y * lax.rsqrt(jnp.mean(y * y, axis=-1, keepdims=True) + EPS) * fw_ref[...]
        o_ref[...] = y


def _mod_index(nb, tm, n_seq):
    if nb == 1:
        return lambda i: 0
    return lambda i: (i * tm) // n_seq


def _ffn(x, mods, norm_w, w_gate, w_up, w_down, final_w, *, layer, which, n_seq, final):
    t, d = x.shape
    f = w_gate.shape[-1]
    nb = mods.shape[0]
    tm = _pick_tile(n_seq if nb > 1 else t, FFN_TM)
    tf = FFN_TF
    midx = _mod_index(nb, tm, n_seq)
    mod_base = 0 if which == 0 else 6
    return pl.pallas_call(
        functools.partial(_ffn_kernel, mod_base=mod_base, final=final),
        grid=(t // tm, f // tf),
        in_specs=[pl.BlockSpec((tm, d), lambda i, j: (i, 0)),
                  pl.BlockSpec((None, N_MOD, d), lambda i, j: (midx(i), 0, 0)),
                  pl.BlockSpec((None, None, 1, d), lambda i, j: (layer, 2 * which, 0, 0)),
                  pl.BlockSpec((None, None, d, tf), lambda i, j: (layer, which, 0, j)),
                  pl.BlockSpec((None, None, d, tf), lambda i, j: (layer, which, 0, j)),
                  pl.BlockSpec((None, None, tf, d), lambda i, j: (layer, which, j, 0)),
                  pl.BlockSpec((1, d), lambda i, j: (0, 0))],
        out_specs=pl.BlockSpec((tm, d), lambda i, j: (i, 0)),
        out_shape=jax.ShapeDtypeStruct((t, d), F32),
        scratch_shapes=[pltpu.VMEM((tm, d), BF16)],
        compiler_params=_cparams(("parallel", "arbitrary")),
        name="ffn",
    )(x, mods, norm_w.reshape(DEPTH, 3, 1, d), w_gate, w_up, w_down, final_w.reshape(1, d))


def _proj_kernel(x_ref, mod_ref, nw_ref, w_ref, cw_ref, o_ref, h_ref, wb_ref, st_ref, *, rowlen):
    j = pl.program_id(0)
    i = pl.program_id(1)

    for jt, segs in enumerate(_W_SEGS):
        @pl.when(jnp.logical_and(i == 0, j == jt))
        def _():
            for lo, src, width in segs:
                if width % 128 == 0:
                    wb_ref[lo:lo + width, :] = w_ref[src:src + width, :].astype(BF16)
                else:
                    base = lo // 128 * 128
                    if src is None:
                        st_ref[lo - base:lo - base + width, :] = jnp.zeros((width, st_ref.shape[1]), F32)
                        wb_ref[base:base + 128, :] = st_ref[...].astype(BF16)
                    else:
                        st_ref[lo - base:lo - base + width, :] = w_ref[src:src + width, :]

    @pl.when(j == 0)
    def _():
        m = mod_ref[...]
        h_ref[i] = _norm_mod(x_ref[...], nw_ref[...], m[4:5], m[3:4]).astype(BF16)
        o_ref[...] = _nt(h_ref[i], wb_ref[...])
        tm = o_ref.shape[0]
        pos = lax.broadcasted_iota(jnp.int32, (tm, 1), 0) % rowlen
        valid = [jnp.logical_and(pos + (t - GDN_CONV // 2) >= 0, pos + (t - GDN_CONV // 2) < rowlen)
                 for t in range(GDN_CONV)]
        for g in range(3 * GDN_HEADS):
            cols = slice(P_DQKV + g * GDN_DK, P_DQKV + (g + 1) * GDN_DK)
            xg = o_ref[:, cols]
            acc = xg * cw_ref[GDN_CONV // 2:GDN_CONV // 2 + 1, cols]
            for t in range(GDN_CONV):
                s = t - GDN_CONV // 2
                if s != 0:
                    xs = pltpu.roll(xg, (-s) % tm, axis=0)
                    acc = acc + jnp.where(valid[t], xs, 0.0) * cw_ref[t:t + 1, cols]
            y = _silu(acc)
            if g < 2 * GDN_HEADS:
                y = y * lax.rsqrt(jnp.sum(y * y, axis=-1, keepdims=True) + EPS)
                if g < GDN_HEADS:
                    y = y * (GDN_DK ** -0.5)
            o_ref[:, cols] = y

    @pl.when(j == 1)
    def _():
        o_ref[...] = _nt(h_ref[i], wb_ref[...])


def _proj(x, mods, norm_w, w_in, conv_w, *, layer, n_seq, rowlen):
    t, d = x.shape
    nb = mods.shape[0]
    tm = _pick_tile(n_seq if nb > 1 else t, PROJ_TM)
    midx = _mod_index(nb, tm, n_seq)
    return pl.pallas_call(
        functools.partial(_proj_kernel, rowlen=rowlen),
        grid=(PROJ_W // PROJ_TN, t // tm),
        in_specs=[pl.BlockSpec((tm, d), lambda j, i: (i * (1 - j), 0)),
                  pl.BlockSpec((None, N_MOD, d), lambda j, i: (midx(i), 0, 0)),
                  pl.BlockSpec((None, None, 1, d), lambda j, i: (layer, 1, 0, 0)),
                  pl.BlockSpec((None, w_in.shape[-1], d), lambda j, i: (layer, 0, 0), pipeline_mode=pl.Buffered(1)),
                  pl.BlockSpec((None, GDN_CONV, 3 * GDN_W), lambda j, i: (layer, 0, 0))],
        out_specs=pl.BlockSpec((tm, PROJ_TN), lambda j, i: (i, j)),
        out_shape=jax.ShapeDtypeStruct((t, PROJ_W), F32),
        scratch_shapes=[pltpu.VMEM((t // tm, tm, d), BF16), pltpu.VMEM((PROJ_TN, d), BF16),
                        pltpu.VMEM((128, d), F32)],
        compiler_params=_cparams(("arbitrary", "arbitrary")),
        name="proj",
    )(x, mods, norm_w.reshape(DEPTH, 3, 1, d), jnp.swapaxes(w_in, 1, 2), conv_w)


def _w_in_segments():
    order = [_DQ, _DK, _DV, _GLRF, _DAF, _DBF, None, _GLRB, _DAB, _DBB, None,
             _GQ, _GK, _GV, _GOG, _DOG, _SU]
    sm_pad = 128 - GLA_RANK - 2 * GDN_HEADS
    tiles = [[] for _ in range(PROJ_W // PROJ_TN)]
    dst = 0
    for s in order:
        width = sm_pad if s is None else _IN_SIZES[s]
        src = None if s is None else _IN_OFF[s]
        tile = tiles[dst // PROJ_TN]
        lo = dst % PROJ_TN
        if tile and src is not None and tile[-1][1] is not None and (
                tile[-1][0] + tile[-1][2] == lo and tile[-1][1] + tile[-1][2] == src):
            tile[-1] = (tile[-1][0], tile[-1][1], tile[-1][2] + width)
        else:
            tile.append((lo, src, width))
        dst += width
        assert (dst - 1) // PROJ_TN == (dst - width) // PROJ_TN, "a segment may not straddle column tiles"
    assert dst == PROJ_W
    return tiles


_W_SEGS = _w_in_segments()


def _np_consts():
    t = np.arange(CHUNK)
    lower = (t[:, None] >= t[None, :])
    incl = np.stack([lower, lower.T])
    strict = np.stack([t[:, None] > t[None, :], t[:, None] < t[None, :]])
    eye4 = np.eye(4, dtype=bool)
    blk = np.kron(eye4, np.ones((CHUNK, CHUNK), bool))
    incl_bd = np.stack([np.kron(eye4, incl[d]) for d in range(2)])
    strict_bd = np.stack([np.kron(eye4, strict[d]) for d in range(2)])
    levels = []
    for d in range(2):
        per = []
        for lv in range(6):
            s = 1 << lv
            same = (t[:, None] // (2 * s)) == (t[None, :] // (2 * s))
            hi = (t % (2 * s)) >= s
            m = same & hi[:, None] & (~hi)[None, :]
            if d == 1:
                m = m.T
            per.append(np.kron(eye4, m))
        levels.append(np.stack(per))
    levels = np.stack(levels)
    assert (levels.sum(1) == strict_bd).all() and blk.any()
    hm_k = np.kron(eye4, np.ones((1, GLA_DK)))
    hm_v = np.kron(eye4, np.ones((1, GLA_DV)))
    st_mask = np.kron(eye4, np.ones((GLA_DV, GLA_DK)))
    ind64 = np.kron(eye4, np.full((GLA_DV, GLA_DV), 1.0 / GLA_DV))
    gla_sel, gla_later, gla_earlier, gla_pair = [], [], [], []
    for d in range(2):
        pos = t if d == 0 else CHUNK - 1 - t
        cum = (pos[None, :] <= pos[:, None]).astype(np.float64)
        sel, later, earlier, pair = [cum], [], [], []
        for lv in range(6):
            s = 1 << lv
            ref_pos = (pos // (2 * s)) * (2 * s) + s - 1
            pick = (pos[None, :] == ref_pos[:, None]).astype(np.float64)
            sel.append(pick @ cum)
            is_later = (pos % (2 * s)) >= s
            same = (pos[:, None] // (2 * s)) == (pos[None, :] // (2 * s))
            later.append(np.repeat(is_later[:, None], GLA_KW, axis=1))
            earlier.append(np.repeat(~is_later[:, None], GLA_KW, axis=1))
            pair.append(np.tile(same & is_later[:, None] & (~is_later)[None, :], (1, 4)))
        gla_sel.append(np.concatenate(sel, axis=0))
        gla_later.append(np.stack(later))
        gla_earlier.append(np.stack(earlier))
        gla_pair.append(np.stack(pair))
        assert (np.stack(pair).sum(0) == np.tile(pos[:, None] > pos[None, :], (1, 4))).all()
    head_sum = np.kron(eye4, np.ones((GLA_DK, GLA_DV)))
    return dict(incl=incl, incl_bd=incl_bd, strict_bd=strict_bd, levels=levels,
                hm_k=hm_k, hm_v=hm_v, st_mask=st_mask, ind64=ind64, eye256=np.eye(256),
                gla_sel=np.stack(gla_sel), gla_later=np.stack(gla_later), gla_earlier=np.stack(gla_earlier),
                gla_pair=np.stack(gla_pair), head_sum=head_sum)


_C = _np_consts()


def _row_block_index(nblk):
    return lambda d, b, i: b * nblk + i + d * (nblk - 1 - 2 * i)


def _gla_kernel(qkf_ref, qkb_ref, vf_ref, vb_ref, smf_ref, smb_ref, up_ref, bias_ref, sel_ref, later_ref,
                earlier_ref, pair_ref, hmk_ref, hmv_ref, stm_ref, hsum_ref, *rest, nchunk, zero_init, chained):
    rest = list(rest)
    st0_ref = None if zero_init else rest.pop(0)
    if chained:
        rest.pop(0)
    of_ref, ob_ref, stf_ref, st_ref, qd_sc, m_sc, dec_sc = rest
    ins =((qkf_ref, vf_ref, smf_ref, of_ref), (qkb_ref, vb_ref, smb_ref, ob_ref))

    @pl.when(pl.program_id(1) == 0)
    def _():
        st_ref[...] = jnp.zeros_like(st_ref) if zero_init else st0_ref[...]

    hmkb = hmk_ref[...].astype(BF16)
    hmv = hmv_ref[...]
    stm = stm_ref[...]

    units = [(dd, c) for c in range(nchunk) for dd in range(2)]
    st1 = []
    for dd, c in units:
        qk_ref, v_ref, sm_ref, _ = ins[dd]
        rows = pl.ds(c * CHUNK, CHUNK)
        qk = qk_ref[rows, :]
        q = qk[:, 0:GLA_KW] * (GLA_DK ** -0.5)
        k = qk[:, GLA_KW:2 * GLA_KW]
        x = _bdot(sm_ref[rows, :], up_ref[dd]) + bias_ref[dd]
        lg = _log_sigmoid(x) * (1.0 / GLA_GATE_NORM)
        ball = _sel_dot(sel_ref[dd], lg)
        st1.append((q, k, v_ref[rows, :], ball, jnp.sum(lg, axis=0, keepdims=True)))
    prods = []
    for lv in range(6):
        for (dd, c), (q, k, _, ball, _) in zip(units, st1):
            b = ball[0:CHUNK]
            ref = ball[(lv + 1) * CHUNK:(lv + 2) * CHUNK]
            ql = q * jnp.where(later_ref[dd, lv] > 0.5, jnp.exp(jnp.minimum(b - ref, 0.0)), 0.0)
            kl = (k * jnp.where(earlier_ref[dd, lv] > 0.5, jnp.exp(jnp.minimum(ref - b, 0.0)), 0.0)).astype(BF16)
            kexp = jnp.concatenate([kl * hmkb[h:h + 1] for h in range(GLA_HEADS)], axis=0)
            prods.append(_bdot_nt(ql, kexp))
    for i, ((dd, c), (q, k, v, ball, btot)) in enumerate(zip(units, st1)):
        o_ref = ins[dd][3]
        n = dd * nchunk + c
        b = ball[0:CHUNK]
        att = jnp.zeros((CHUNK, GLA_HEADS * CHUNK), F32)
        for lv in range(6):
            att = jnp.where(pair_ref[dd, lv] > 0.5, prods[lv * len(units) + i], att)
        vexp = jnp.concatenate([v * hmv[h:h + 1] for h in range(GLA_HEADS)], axis=0)
        o_ref[pl.ds(c * CHUNK, CHUNK), :] = _bdot(att, vexp) + _bdot(q * k, hsum_ref[...]) * v
        qd_sc[n] = (q * jnp.exp(b)).astype(BF16)
        kd = k * jnp.exp(btot - b)
        m_sc[n] = _bdot(v.T, kd) * stm
        dec_sc[n] = jnp.broadcast_to(jnp.exp(btot), (8, GLA_KW))

    for dd in range(2):
        o_ref = ins[dd][3]
        st = st_ref[dd]
        for ci in range(nchunk):
            c = ci if dd == 0 else nchunk - 1 - ci
            n = dd * nchunk + c
            rows = pl.ds(c * CHUNK, CHUNK)
            o_ref[rows, :] += _nt(qd_sc[n], st.astype(BF16))
            st = st * dec_sc[n][0:1] + m_sc[n]
        st_ref[dd] = st
        for h in range(GLA_HEADS):
            stf_ref[dd, h] = st[h * GLA_DV:(h + 1) * GLA_DV, h * GLA_DK:(h + 1) * GLA_DK].T


def _gla(proj, up_p, bias_p, st0, fin, *, layer, nseq, n_seq):
    t = proj.shape[0]
    sub = min(n_seq, GLA_SUB)
    nblk = n_seq // sub
    nck = sub // CHUNK
    rbi = _row_block_index(nblk)
    c = _C
    consts = [jnp.asarray(c["gla_sel"], BF16), jnp.asarray(c["gla_later"], F32), jnp.asarray(c["gla_earlier"], F32),
              jnp.asarray(c["gla_pair"], F32), jnp.asarray(c["hm_k"], F32), jnp.asarray(c["hm_v"], F32),
              jnp.asarray(c["st_mask"], F32), jnp.asarray(c["head_sum"], BF16)]
    st_spec = pl.BlockSpec((2, None, GLA_W, GLA_KW), lambda b, i: (0, b, 0, 0))
    full = lambda a: pl.BlockSpec(a.shape, lambda b, i: (0,) * a.ndim)
    zero_init = st0 is None
    nu = 2 * nck
    rows = (lambda b, i: rbi(0, b, i), lambda b, i: rbi(1, b, i))

    def blk(width, col, dd):
        return pl.BlockSpec((sub, width), lambda b, i: (rows[dd](b, i), col // width))

    hs = (GLA_HEADS, GLA_DK, GLA_DV)
    args = [proj, proj, proj, proj, proj, proj, up_p, bias_p, *consts] + ([] if zero_init else [st0])
    in_specs = ([blk(2 * GLA_KW, P_GQK, 0), blk(2 * GLA_KW, P_GQK, 1), blk(GLA_W, P_GV, 0), blk(GLA_W, P_GV, 1),
                 blk(128, P_SMALL, 0), blk(128, P_SMALL + 128, 1), full(up_p), full(bias_p)]
                + [full(a) for a in consts] + ([] if zero_init else [st_spec]))
    return pl.pallas_call(
        functools.partial(_gla_kernel, nchunk=nck, zero_init=zero_init, chained=fin is not None),
        grid=(nseq, nblk),
        in_specs=in_specs + _chain_in(fin),
        out_specs=[pl.BlockSpec((sub, GLA_W), lambda b, i: (rows[0](b, i), 0)),
                   pl.BlockSpec((sub, GLA_W), lambda b, i: (rows[1](b, i), 0)), _fin_spec(hs, layer)],
        out_shape=[jax.ShapeDtypeStruct((t, GLA_W), F32), jax.ShapeDtypeStruct((t, GLA_W), F32),
                   jax.ShapeDtypeStruct((nseq, DEPTH, 2) + hs, F32)],
        scratch_shapes=[pltpu.VMEM((2, GLA_W, GLA_KW), F32), pltpu.VMEM((nu, CHUNK, GLA_KW), BF16),
                        pltpu.VMEM((nu, GLA_W, GLA_KW), F32), pltpu.VMEM((nu, 8, GLA_KW), F32)],
        input_output_aliases={} if fin is None else {len(args): 2},
        compiler_params=_cparams(("arbitrary", "arbitrary")),
        name="gla",
    )(*args, *([] if fin is None else [fin]))


def _fin_spec(hs, layer):
    return pl.BlockSpec((None, None, 2) + hs, lambda b, i: (b, layer, 0) + (0,) * len(hs))


def _chain_in(fin):
    return [] if fin is None else [pl.BlockSpec(memory_space=pl.ANY)]


def _gdn_kernel(qkvf_ref, qkvb_ref, smf_ref, smb_ref, arow_ref, dtb_ref, tri_ref, incl_ref, strict_ref, lvl_ref,
                eye_ref, *rest, nchunk, zero_init, chained):
    rest = list(rest)
    s0_ref = None if zero_init else rest.pop(0)
    if chained:
        rest.pop(0)
    of_ref, ob_ref, sf_ref, s_ref, la_ref, be_ref, w_sc, u0_sc, qkd_sc, qe_sc, kdt_sc, egl_sc = rest
    qkv_refs = (qkvf_ref, qkvb_ref)
    o_refs = (of_ref, ob_ref)

    @pl.when(pl.program_id(1) == 0)
    def _():
        s_ref[...] = jnp.zeros_like(s_ref) if zero_init else s0_ref[...]

    for dd, sm_ref in enumerate((smf_ref, smb_ref)):
        sm = sm_ref[...]
        la_ref[dd] = -jnp.exp(arow_ref[dd]) * _softplus(sm + dtb_ref[dd])
        be_ref[dd] = jax.nn.sigmoid(sm)

    eye = eye_ref[...]

    def stack(ref, r0, lo, width):
        return jnp.concatenate([ref[pl.ds(r0, CHUNK), lo + h * width:lo + (h + 1) * width]
                                for h in range(GDN_HEADS)], axis=0)

    def colstack(a, lo):
        return jnp.concatenate([a[:, lo + h:lo + h + 1] for h in range(GDN_HEADS)], axis=0)

    units = [(dd, c) for c in range(nchunk) for dd in range(2)]
    st1 = []
    for dd, c in units:
        r0 = c * CHUNK
        qkv_ref = qkv_refs[dd]
        qs = stack(qkv_ref, r0, 0, GDN_DK)
        ks = stack(qkv_ref, r0, GDN_W, GDN_DK)
        la = la_ref[dd, pl.ds(r0, CHUNK), :]
        g = _sel_dot(tri_ref[dd], la)
        ksb = ks.astype(BF16)
        st1.append((qs, ks, la, g, _bdot_nt(ksb, ksb), _bdot_nt(qs, ksb)))
    loc = []
    for (dd, c), (qs, ks, la, g, kk, qk) in zip(units, st1):
        r0 = c * CHUNK
        n = dd * nchunk + c
        vs = stack(qkv_refs[dd], r0, 2 * GDN_W, GDN_DV)
        gtot = jnp.sum(la, axis=0, keepdims=True)
        gcol = colstack(g, SM_A)
        kfcol = colstack(jnp.exp(gtot - g), SM_A)
        egcol = colstack(jnp.exp(g), SM_A)
        bcol = colstack(be_ref[dd, pl.ds(r0, CHUNK), :], SM_B)
        grow = jnp.sum(eye * gcol, axis=0, keepdims=True)
        lm_rows, qkd_rows = [], []
        for h in range(GDN_HEADS):
            rs = slice(h * CHUNK, (h + 1) * CHUNK)
            lt = (h * CHUNK) // 128
            ls = slice(lt * 128, (lt + 1) * 128)
            dec = incl_ref[dd, rs, ls] * jnp.exp(jnp.minimum(gcol[rs] - grow[:, ls], 0.0))
            lm = strict_ref[dd, rs, ls] * dec * kk[rs, ls] * bcol[rs]
            qd = qk[rs, ls] * dec
            zero = jnp.zeros_like(lm)
            lm_rows.append(jnp.concatenate([lm, zero] if lt == 0 else [zero, lm], axis=1))
            qkd_rows.append(jnp.concatenate([qd, zero] if lt == 0 else [zero, qd], axis=1))
        lmat = jnp.concatenate(lm_rows, axis=0)
        rhs = jnp.concatenate([ks * (bcol * egcol), vs * bcol], axis=1).astype(BF16)
        qkd_sc[n] = jnp.concatenate(qkd_rows, axis=0).astype(BF16)
        qe_sc[n] = (qs * egcol).astype(BF16)
        kd = ks * kfcol
        for h in range(GDN_HEADS):
            kdt_sc[n * GDN_HEADS + h] = kd[h * CHUNK:(h + 1) * CHUNK].T.astype(BF16)
        egl_sc[n] = jnp.broadcast_to(jnp.exp(gtot), (8, 128))
        loc.append((lmat.astype(BF16), rhs))
    def inv_start(wave):
        return [eye - (loc[i][0] * lvl_ref[units[i][0], 0]).astype(F32) for i in wave]

    def inv_level(wave, tinv, lv):
        tb = [t_.astype(BF16) for t_ in tinv]
        cs = [jnp.dot(loc[i][0] * lvl_ref[units[i][0], lv], t_, preferred_element_type=F32)
              for i, t_ in zip(wave, tb)]
        return [t32 - jnp.dot(t_, c_.astype(BF16), preferred_element_type=F32)
                for t32, t_, c_ in zip(tinv, tb, cs)]

    def solve(wave, tinv):
        for i, t_ in zip(wave, tinv):
            dd, c = units[i]
            sol = jnp.dot(t_.astype(BF16), loc[i][1], preferred_element_type=F32)
            w_sc[dd * nchunk + c] = sol[:, :GDN_DK].astype(BF16)
            u0_sc[dd * nchunk + c] = sol[:, GDN_DK:]

    s_cur = [[s_ref[dd, h] for h in range(GDN_HEADS)] for dd in range(2)]

    def recur(ci):
        step = []
        for dd in range(2):
            c = ci if dd == 0 else nchunk - 1 - ci
            n = dd * nchunk + c
            sb = [s_.astype(BF16) for s_ in s_cur[dd]]
            ub = []
            for h in range(GDN_HEADS):
                rows = pl.ds(h * CHUNK, CHUNK)
                u = u0_sc[n, rows, :] - jnp.dot(w_sc[n, rows, :], sb[h], preferred_element_type=F32)
                ub.append(u.astype(BF16))
            step.append((c, n, sb, ub))
        for dd, (c, n, sb, ub) in enumerate(step):
            egl = egl_sc[n]
            for h in range(GDN_HEADS):
                s_cur[dd][h] = (egl[0:1, SM_A + h:SM_A + h + 1] * s_cur[dd][h]
                                + jnp.dot(kdt_sc[n * GDN_HEADS + h], ub[h], preferred_element_type=F32))
        for dd, (c, n, sb, ub) in enumerate(step):
            o2 = jnp.dot(qkd_sc[n], jnp.concatenate(ub, axis=0), preferred_element_type=F32)
            for h in range(GDN_HEADS):
                rows = pl.ds(h * CHUNK, CHUNK)
                o_refs[dd][pl.ds(c * CHUNK, CHUNK), h * GDN_DV:(h + 1) * GDN_DV] = (
                    jnp.dot(qe_sc[n, rows, :], sb[h], preferred_element_type=F32) + o2[h * CHUNK:(h + 1) * CHUNK])

    every = list(range(len(units)))
    tinv = inv_start(every)
    for lv in range(1, 6):
        tinv = inv_level(every, tinv, lv)
    solve(every, tinv)
    for ci in range(nchunk):
        recur(ci)
    for dd in range(2):
        for h in range(GDN_HEADS):
            s_ref[dd, h] = s_cur[dd][h]
    sf_ref[...] = s_ref[...]


def _gdn(proj, arow, dtb, s0, fin, *, layer, nseq, n_seq):
    t = proj.shape[0]
    nblk = n_seq // SUB
    rbi = _row_block_index(nblk)
    c = _C
    consts = [jnp.asarray(c["incl"], BF16), jnp.asarray(c["incl_bd"], F32), jnp.asarray(c["strict_bd"], F32),
              jnp.asarray(c["levels"], BF16), jnp.asarray(c["eye256"], F32)]
    hs = (GDN_HEADS, GDN_DK, GDN_DV)
    nck = SUB // CHUNK
    nu = 2 * nck
    zero_init = s0 is None
    s_spec = pl.BlockSpec((2, None) + hs, lambda b, i: (0, b, 0, 0, 0))
    full = lambda shape: pl.BlockSpec(shape, lambda b, i: (0,) * len(shape))
    rows = (lambda b, i: rbi(0, b, i), lambda b, i: rbi(1, b, i))
    args = [proj, proj, proj, proj, arow, dtb, *consts] + ([] if zero_init else [s0])
    return pl.pallas_call(
        functools.partial(_gdn_kernel, nchunk=nck, zero_init=zero_init, chained=fin is not None),
        grid=(nseq, nblk),
        in_specs=[pl.BlockSpec((SUB, 3 * GDN_W), lambda b, i: (rows[0](b, i), P_DQKV // (3 * GDN_W))),
                  pl.BlockSpec((SUB, 3 * GDN_W), lambda b, i: (rows[1](b, i), P_DQKV // (3 * GDN_W))),
                  pl.BlockSpec((SUB, 128), lambda b, i: (rows[0](b, i), P_SMALL // 128)),
                  pl.BlockSpec((SUB, 128), lambda b, i: (rows[1](b, i), P_SMALL // 128 + 1)),
                  full((2, 1, 128)), full((2, 1, 128)), full((2, CHUNK, CHUNK)),
                  full((2, 256, 256)), full((2, 256, 256)), full((2, 6, 256, 256)), full((256, 256))]
                 + ([] if zero_init else [s_spec]) + _chain_in(fin),
        out_specs=[pl.BlockSpec((SUB, GDN_W), lambda b, i: (rows[0](b, i), 0)),
                   pl.BlockSpec((SUB, GDN_W), lambda b, i: (rows[1](b, i), 0)), _fin_spec(hs, layer)],
        out_shape=[jax.ShapeDtypeStruct((t, GDN_W), F32), jax.ShapeDtypeStruct((t, GDN_W), F32),
                   jax.ShapeDtypeStruct((nseq, DEPTH, 2) + hs, F32)],
        input_output_aliases={} if fin is None else {len(args): 2},
        scratch_shapes=[pltpu.VMEM((2,) + hs, F32),
                        pltpu.VMEM((2, SUB, 128), F32), pltpu.VMEM((2, SUB, 128), F32),
                        pltpu.VMEM((nu, 256, GDN_DK), BF16), pltpu.VMEM((nu, 256, GDN_DV), F32),
                        pltpu.VMEM((nu, 256, 256), BF16), pltpu.VMEM((nu, 256, GDN_DK), BF16),
                        pltpu.VMEM((nu * GDN_HEADS, GDN_DK, CHUNK), BF16), pltpu.VMEM((nu, 8, 128), F32)],
        compiler_params=_cparams(("arbitrary", "arbitrary")),
        name="gdn",
    )(*args, *([] if fin is None else [fin]))


def _cmul(ar, ai, br, bi):
    return ar * br - ai * bi, ar * bi + ai * br


def _split2(x):
    hi = x.astype(BF16)
    return hi, (x - hi.astype(F32)).astype(BF16)


def _s5_consts_kernel(lr_ref, li_ref, st_ref, br_ref, bi_ref, cr_ref, ci_ref, gm_ref, bm_ref,
                      laml_ref, wexp_ref, vexp_ref, tt_ref, w_ref, v_ref, a_ref, q_ref):
    d = pl.program_id(0) % 2
    lc = S5_L * S5_CH

    def trows(s):
        return pl.ds(pl.multiple_of((s + d * (S5_L - 1 - 2 * s)) * S5_CH, S5_CH), S5_CH)

    lr = lr_ref[...]
    li = li_ref[...]
    step = jnp.exp(st_ref[...])
    mag = jnp.exp(lr * step)
    lbr = mag * jnp.cos(li * step)
    lbi = mag * jnp.sin(li * step)
    nr = lbr - 1.0
    den = lr * lr + li * li
    fr = (nr * lr + lbi * li) / den
    fi = (lbi * lr - nr * li) / den
    bbr, bbi = _cmul(fr, fi, br_ref[...], bi_ref[...])
    cr = cr_ref[...]
    ci = ci_ref[...]
    pr = jnp.ones_like(lr)
    pi = jnp.zeros_like(lr)
    for s in range(S5_L):
        ar, ai = _cmul(cr, ci, pr, pi)
        a_ref[0, trows(s), :] = ar
        a_ref[1, trows(s), :] = ai
        wr, wi = _cmul(pr, pi, bbr, bbi)
        w_ref[0, trows(S5_L - 1 - s), :] = wr
        w_ref[1, trows(S5_L - 1 - s), :] = wi
        pr, pi = _cmul(pr, pi, lbr, lbi)
        vr, vi = _cmul(cr, ci, pr, pi)
        v_ref[0, trows(s), :] = vr
        v_ref[1, trows(s), :] = -vi
    laml_ref[0:1, :] = pr
    laml_ref[1:2, :] = pi

    lane = lax.broadcasted_iota(jnp.int32, (1, 128), 1)
    for gp in range(S5_PAIRS):
        lanes = slice(gp * 128, (gp + 1) * 128)
        for r in range(2):
            for src, dst in ((w_ref, wexp_ref), (v_ref, vexp_ref)):
                slab = src[r, :, lanes]
                dst[r, gp] = jnp.concatenate([jnp.where(lane < S5_P, slab, 0.0),
                                              jnp.where(lane >= S5_P, slab, 0.0)], axis=0).astype(BF16)

    gm = gm_ref[...]
    kall = None
    for r, b, sign in ((0, bbr, 1.0), (1, bbi, -1.0)):
        ah, al = _split2(a_ref[r])
        bh, bl = _split2(jnp.concatenate([b] * S5_GROUPS, axis=0) * gm)
        nt = lambda x, y: lax.dot_general(x, y, (((1,), (1,)), ((), ())), preferred_element_type=F32)
        term = (nt(ah, bh) + nt(ah, bl) + nt(al, bh)) * sign
        kall = term if kall is None else kall + term

    zeros = jnp.zeros((lc, S5_W), F32)
    for delta in range(S5_GROUPS):
        q_ref[delta, 0:lc, :] = zeros
        q_ref[delta, lc:2 * lc, :] = kall if delta == 0 else pltpu.roll(kall, delta * S5_CH, axis=1)
        q_ref[delta, 2 * lc:3 * lc, :] = zeros
    for g in range(S5_GROUPS):
        acc = zeros
        for s in range(S5_L):
            start = pl.multiple_of(lc - S5_CH * s + d * (S5_CH * (S5_L - 1)), S5_CH)
            blk = q_ref[(s - g) % S5_GROUPS, pl.ds(start, lc), :]
            acc = jnp.where(bm_ref[s:s + 1, :] > 0.5, blk, acc)
        tt_ref[g] = acc.astype(BF16)


def _s5_consts(lam_re, lam_im, log_step, b_re, b_im, c_re, c_im):
    n = DEPTH * 2
    lc = S5_L * S5_CH
    lr = lam_re.reshape(n, 1, S5_N)
    li = lam_im.reshape(n, 1, S5_N)
    st = jnp.repeat(log_step.reshape(n, S5_GROUPS), S5_P, axis=-1).reshape(n, 1, S5_N)
    br = b_re.reshape(n, S5_N, S5_CH).transpose(0, 2, 1)
    bi = b_im.reshape(n, S5_N, S5_CH).transpose(0, 2, 1)
    cr = c_re.reshape(n, S5_GROUPS, S5_CH, S5_P).transpose(0, 2, 1, 3).reshape(n, S5_CH, S5_N)
    ci = c_im.reshape(n, S5_GROUPS, S5_CH, S5_P).transpose(0, 2, 1, 3).reshape(n, S5_CH, S5_N)
    gm = jnp.asarray(np.kron(np.eye(S5_GROUPS), np.ones((S5_CH, S5_P))), F32)
    row = pl.BlockSpec((None, 1, S5_N), lambda i: (i, 0, 0))
    mat = pl.BlockSpec((None, S5_CH, S5_N), lambda i: (i, 0, 0))
    exp = pl.BlockSpec((None, 2, S5_PAIRS, 2 * lc, 128), lambda i: (i, 0, 0, 0, 0))
    laml, wexp, vexp, tt = pl.pallas_call(
        _s5_consts_kernel,
        grid=(n,),
        in_specs=[row, row, row, mat, mat, mat, mat, pl.BlockSpec((S5_W, S5_N), lambda i: (0, 0)),
                  pl.BlockSpec((S5_GROUPS, S5_W), lambda i: (0, 0))],
        out_specs=[pl.BlockSpec((None, 2, S5_N), lambda i: (i, 0, 0)), exp, exp,
                   pl.BlockSpec((None, S5_GROUPS, lc, lc), lambda i: (i, 0, 0, 0))],
        out_shape=[jax.ShapeDtypeStruct((n, 2, S5_N), F32),
                   jax.ShapeDtypeStruct((n, 2, S5_PAIRS, 2 * lc, 128), BF16),
                   jax.ShapeDtypeStruct((n, 2, S5_PAIRS, 2 * lc, 128), BF16),
                   jax.ShapeDtypeStruct((n, S5_GROUPS, lc, lc), BF16)],
        scratch_shapes=[pltpu.VMEM((2, lc, S5_N), F32), pltpu.VMEM((2, lc, S5_N), F32),
                        pltpu.VMEM((2, lc, S5_N), F32), pltpu.VMEM((S5_GROUPS, 3 * lc, S5_W), F32)],
        compiler_params=_cparams(("arbitrary",)),
        name="s5_consts",
    )(lr, li, st, br, bi, cr, ci, gm, _lane_block_masks())
    return laml.reshape(n, 2, S5_PAIRS, 128), wexp, vexp, tt


def _lane_block_masks():
    return jnp.asarray(np.kron(np.eye(S5_GROUPS), np.ones((1, S5_CH))), F32)


def _nt(a, b):
    return lax.dot_general(a, b, (((1,), (1,)), ((), ())), preferred_element_type=F32)


def _lane_block_transpose(arrs):
    n = len(arrs)
    width = arrs[0].shape[1]
    blk = lax.broadcasted_iota(jnp.int32, (1, width), 1) // (width // n)
    arrs = list(arrs)
    h = n // 2
    while h >= 1:
        hi = (blk & h) != 0
        for a in range(n):
            if a & h:
                continue
            lo_arr, hi_arr = arrs[a], arrs[a + h]
            arrs[a] = jnp.where(hi, pltpu.roll(hi_arr, h * (width // n), axis=1), lo_arr)
            arrs[a + h] = jnp.where(hi, hi_arr, pltpu.roll(lo_arr, width - h * (width // n), axis=1))
        h //= 2
    return arrs


def _s5_kernel(u0_ref, u1_ref, tt_ref, wexp_ref, vexp_ref, laml_ref, *rest, nseq, nblk, zero_init):
    if zero_init:
        y_ref, hf_ref, ug_sc, yg_sc, hin_re, hin_im, hp_re, hp_im = rest
    else:
        h0_ref, y_ref, hf_ref, ug_sc, yg_sc, hin_re, hin_im, hp_re, hp_im = rest
    d = pl.program_id(0)
    r = nblk * nseq
    lc = S5_L * S5_CH

    @pl.when(d == 0)
    def _():
        xs = []
        for s in range(S5_L):
            rows = pl.ds(s, r, stride=S5_L)
            xs.append(jnp.concatenate([u0_ref[rows, :], u1_ref[rows, :]], axis=1).astype(BF16))
        for g, ug in enumerate(_lane_block_transpose(xs)):
            ug_sc[g] = ug

    for gp in range(S5_PAIRS):
        up = jnp.concatenate([ug_sc[2 * gp], ug_sc[2 * gp + 1]], axis=1)
        rows = pl.ds(gp, r, stride=S5_PAIRS)
        hin_re[rows, :] = jnp.dot(up, wexp_ref[0, gp], preferred_element_type=F32)
        hin_im[rows, :] = jnp.dot(up, wexp_ref[1, gp], preferred_element_type=F32)

    lr = laml_ref[0]
    li = laml_ref[1]

    def step(kk, carry):
        k = kk + d * (nblk - 1 - 2 * kk)
        new = []
        for b in range(nseq):
            rows = pl.ds(pl.multiple_of((b * nblk + k) * S5_PAIRS, S5_PAIRS), S5_PAIRS)
            hr, hi = carry[2 * b], carry[2 * b + 1]
            hp_re[rows, :] = hr
            hp_im[rows, :] = hi
            new += [lr * hr - li * hi + hin_re[rows, :], lr * hi + li * hr + hin_im[rows, :]]
        return tuple(new)

    init = []
    for b in range(nseq):
        rows = slice(b * S5_PAIRS, (b + 1) * S5_PAIRS)
        init += ([jnp.zeros((S5_PAIRS, 128), F32)] * 2 if zero_init else [h0_ref[0, rows, :], h0_ref[1, rows, :]])
    fin = lax.fori_loop(0, nblk, step, tuple(init))
    for b in range(nseq):
        rows = slice(b * S5_PAIRS, (b + 1) * S5_PAIRS)
        hf_ref[0, rows, :] = fin[2 * b]
        hf_ref[1, rows, :] = fin[2 * b + 1]

    for gp in range(S5_PAIRS):
        rows = pl.ds(gp, r, stride=S5_PAIRS)
        ys = (_nt(hp_re[rows, :].astype(BF16), vexp_ref[0, gp])
              + _nt(hp_im[rows, :].astype(BF16), vexp_ref[1, gp]))
        for j in range(2):
            g = 2 * gp + j
            yg = _nt(ug_sc[g], tt_ref[g]) + ys[:, j * lc:(j + 1) * lc]

            @pl.when(d == 0)
            def _():
                yg_sc[g] = yg

            @pl.when(d == 1)
            def _():
                yg_sc[g] += yg

    @pl.when(d == 1)
    def _():
        for t, yt in enumerate(_lane_block_transpose([yg_sc[g] for g in range(S5_GROUPS)])):
            rows = pl.ds(t, r, stride=S5_L)
            y_ref[0, rows, :] = yt[:, 0:128]
            y_ref[1, rows, :] = yt[:, 128:2 * 128]


def _s5(proj, consts, h0, *, layer, nseq, n_seq):
    laml, wexp, vexp, tt = consts
    t = proj.shape[0]
    nblk = n_seq // S5_L
    r = nblk * nseq
    lc = S5_L * S5_CH
    bs = nseq * S5_PAIRS
    zero_init = h0 is None
    h_spec = pl.BlockSpec((None, 2, bs, 128), lambda d: (d, 0, 0, 0))
    return pl.pallas_call(
        functools.partial(_s5_kernel, nseq=nseq, nblk=nblk, zero_init=zero_init),
        grid=(2,),
        in_specs=[pl.BlockSpec((t, 128), lambda d: (0, P_SU // 128)),
                  pl.BlockSpec((t, 128), lambda d: (0, P_SU // 128 + 1)),
                  pl.BlockSpec((None, S5_GROUPS, lc, lc), lambda d: (2 * layer + d, 0, 0, 0)),
                  pl.BlockSpec((None, 2, S5_PAIRS, 2 * lc, 128), lambda d: (2 * layer + d, 0, 0, 0, 0)),
                  pl.BlockSpec((None, 2, S5_PAIRS, 2 * lc, 128), lambda d: (2 * layer + d, 0, 0, 0, 0)),
                  pl.BlockSpec((None, 2, S5_PAIRS, 128), lambda d: (2 * layer + d, 0, 0, 0))]
                 + ([] if zero_init else [h_spec]),
        out_specs=[pl.BlockSpec((2, t, 128), lambda d: (0, 0, 0)), h_spec],
        out_shape=[jax.ShapeDtypeStruct((2, t, 128), F32),
                   jax.ShapeDtypeStruct((2, 2, bs, 128), F32)],
        scratch_shapes=[pltpu.VMEM((S5_GROUPS, r, lc), BF16), pltpu.VMEM((S5_GROUPS, r, lc), F32)]
                       + [pltpu.VMEM((r * S5_PAIRS, 128), F32) for _ in range(4)],
        compiler_params=_cparams(("arbitrary",)),
        name="s5",
    )(proj, proj, tt, wexp, vexp, laml, *([] if zero_init else [h0]))


def _out_kernel(x_ref, mod_ref, ogf_ref, ogb_ref, odf_ref, odb_ref, ysf_ref, ysb_ref, gog_ref, su_ref, dog_ref,
                ind_ref, gw_ref, dw_ref, sd_ref, wglu_ref, bglu_ref, wout32_ref, o_ref, wout_ref):
    @pl.when(pl.program_id(0) == 0)
    def _():
        wout_ref[...] = wout32_ref[...].astype(BF16)

    og = ogf_ref[...] + ogb_ref[...]
    hi, mid, _ = _split3(og * og)
    ms = jnp.dot(hi, ind_ref[...], preferred_element_type=F32) + jnp.dot(mid, ind_ref[...], preferred_element_type=F32)
    og = og * lax.rsqrt(ms + EPS) * gw_ref[...] * _silu(gog_ref[...])
    od = odf_ref[...] + odb_ref[...]
    dog = dog_ref[...]
    parts = []
    for h in range(GDN_HEADS):
        z = od[:, h * GDN_DV:(h + 1) * GDN_DV]
        z = z * lax.rsqrt(jnp.mean(z * z, axis=-1, keepdims=True) + EPS) * dw_ref[...]
        parts.append(z * _silu(dog[:, h * GDN_DV:(h + 1) * GDN_DV]))
    y = su_ref[...] * sd_ref[...] + jnp.concatenate([ysf_ref[...], ysb_ref[...]], axis=1)
    g = 0.5 * y * (1.0 + jnp.tanh(math.sqrt(2.0 / math.pi) * (y + 0.044715 * (y * y * y))))
    os_ = g * jax.nn.sigmoid(_bdot(g, wglu_ref[...]) + bglu_ref[...])
    out = _bdot(og, wout_ref[0:GLA_W, :]) + _bdot(os_, wout_ref[GLA_W + GDN_W:MIX_W, :])
    for h in range(GDN_HEADS):
        lo = GLA_W + h * GDN_DV
        out = out + _bdot(parts[h], wout_ref[lo:lo + GDN_DV, :])
    o_ref[...] = x_ref[...] + mod_ref[5:6, :] * out


def _mix_out(x, mods, proj, og, od, ys, gla_nw, gdn_nw, s5_d, w_glu, b_glu, w_out, *, layer, n_seq):
    t, d = x.shape
    nb = mods.shape[0]
    tm = _pick_tile(n_seq if nb > 1 else t, MIX_TM)
    midx = _mod_index(nb, tm, n_seq)
    ys2 = ys

    def dirspec(w, dd):
        return pl.BlockSpec((None, tm, w), lambda i: (dd, i, 0))

    def vec(w):
        return pl.BlockSpec((None, 1, w), lambda i: (layer, 0, 0))

    return pl.pallas_call(
        _out_kernel,
        grid=(t // tm,),
        in_specs=[pl.BlockSpec((tm, d), lambda i: (i, 0)),
                  pl.BlockSpec((None, N_MOD, d), lambda i: (midx(i), 0, 0)),
                  pl.BlockSpec((tm, GLA_W), lambda i: (i, 0)), pl.BlockSpec((tm, GLA_W), lambda i: (i, 0)),
                  pl.BlockSpec((tm, GDN_W), lambda i: (i, 0)), pl.BlockSpec((tm, GDN_W), lambda i: (i, 0)),
                  dirspec(128, 0), dirspec(128, 1),
                  pl.BlockSpec((tm, GLA_W), lambda i: (i, P_GOG // GLA_W)),
                  pl.BlockSpec((tm, S5_W), lambda i: (i, P_SU // S5_W)),
                  pl.BlockSpec((tm, GDN_W), lambda i: (i, P_DOG // GDN_W)),
                  pl.BlockSpec((GLA_W, GLA_W), lambda i: (0, 0)),
                  vec(GLA_W), vec(GDN_DV), vec(S5_W),
                  pl.BlockSpec((None, S5_W, S5_W), lambda i: (layer, 0, 0)),
                  vec(S5_W),
                  pl.BlockSpec((None, MIX_W, d), lambda i: (layer, 0, 0), pipeline_mode=pl.Buffered(1))],
        out_specs=pl.BlockSpec((tm, d), lambda i: (i, 0)),
        out_shape=jax.ShapeDtypeStruct((t, d), F32),
        scratch_shapes=[pltpu.VMEM((MIX_W, d), BF16)],
        compiler_params=_cparams(("arbitrary",)),
        name="mix_out",
    )(x, mods, og[0], og[1], od[0], od[1], ys2, ys2, proj, proj, proj,
      jnp.asarray(_C["ind64"], BF16),
      jnp.tile(gla_nw, (1, GLA_HEADS)).reshape(DEPTH, 1, GLA_W), gdn_nw.reshape(DEPTH, 1, GDN_DV),
      s5_d.reshape(DEPTH, 1, S5_W), w_glu, b_glu.reshape(DEPTH, 1, S5_W), w_out)


def _gla_params(gk_up, gk_bias, layer):
    up = jnp.zeros((2, 128, GLA_KW), F32).at[:, :GLA_RANK, :].set(gk_up[layer])
    return up, gk_bias[layer].reshape(2, 1, GLA_KW)


def _gdn_params(a_log, dt_bias, layer):
    arow = jnp.zeros((2, 1, 128), F32).at[:, 0, SM_A:SM_A + GDN_HEADS].set(a_log[layer])
    dtb = jnp.zeros((2, 1, 128), F32).at[:, 0, SM_A:SM_A + GDN_HEADS].set(dt_bias[layer])
    return arow, dtb


def _gla_state_in(s):
    eye = jnp.eye(GLA_HEADS, dtype=F32)
    st = jnp.einsum('bzhde,hg->zbhegd', s.astype(F32), eye)
    return st.reshape(2, s.shape[0], GLA_W, GLA_KW)


def _trunk(x3, mods, rows, states, p, s5c, w_in):
    nseq, n_seq, d = x3.shape
    x = x3.reshape(nseq * n_seq, d)
    fin = []
    gla_f = gdn_f = None
    for l in range(DEPTH):
        m = mods[l]
        x = _ffn(x, m, p['norm_w'], p['ffn_w_gate'], p['ffn_w_up'], p['ffn_w_down'], p['final_norm_w'],
                 layer=l, which=0, n_seq=n_seq, final=False)
        proj = _proj(x, m, p['norm_w'], w_in, p['gdn_conv_w'], layer=l, n_seq=n_seq, rowlen=n_seq // rows)
        if states is None:
            st0 = s0 = h0 = None
        else:
            st_gla, st_gdn, st_re, st_im = states
            st0 = _gla_state_in(st_gla[:, l])
            s0 = st_gdn[:, l].transpose(1, 0, 2, 3, 4)
            h0 = jnp.stack([st_re[:, l].reshape(nseq, 2, S5_N), st_im[:, l].reshape(nseq, 2, S5_N)], axis=0)
            h0 = h0.transpose(2, 0, 1, 3).reshape(2, 2, nseq * S5_PAIRS, 128)
        up_p, bias_p = _gla_params(p['gla_gk_up'], p['gla_gk_bias'], l)
        og_f, og_b, gla_f = _gla(proj, up_p, bias_p, st0, gla_f, layer=l, nseq=nseq, n_seq=n_seq)
        og = (og_f, og_b)
        arow, dtb = _gdn_params(p['gdn_a_log'], p['gdn_dt_bias'], l)
        od_f, od_b, gdn_f = _gdn(proj, arow, dtb, s0, gdn_f, layer=l, nseq=nseq, n_seq=n_seq)
        od = (od_f, od_b)
        ys, s5_f = _s5(proj, s5c, h0, layer=l, nseq=nseq, n_seq=n_seq)
        x = _mix_out(x, m, proj, og, od, ys, p['gla_norm_w'], p['gdn_norm_w'], p['s5_d'], p['s5_w_glu'],
                     p['s5_b_glu'], p['w_out'], layer=l, n_seq=n_seq)
        x = _ffn(x, m, p['norm_w'], p['ffn_w_gate'], p['ffn_w_up'], p['ffn_w_down'], p['final_norm_w'],
                 layer=l, which=1, n_seq=n_seq, final=(l == DEPTH - 1))
        fin.append((gla_f, gdn_f, s5_f))
    return x.reshape(nseq, n_seq, d), fin


def kernel(x_prompt, x_sample, c, state_gla, state_gdn, state_s5_re, state_s5_im, c_ctx, w_ada, b_ada, norm_w, ffn_w_gate, ffn_w_up, ffn_w_down, w_in, gla_gk_up, gla_gk_bias, gla_norm_w, gdn_conv_w, gdn_a_log, gdn_dt_bias, gdn_norm_w, s5_lam_re, s5_lam_im, s5_log_step, s5_b_re, s5_b_im, s5_c_re, s5_c_im, s5_d, s5_w_glu, s5_b_glu, w_out, final_norm_w):
    p = dict(norm_w=norm_w, ffn_w_gate=ffn_w_gate, ffn_w_up=ffn_w_up, ffn_w_down=ffn_w_down,
             gla_gk_up=gla_gk_up, gla_gk_bias=gla_gk_bias, gla_norm_w=gla_norm_w, gdn_conv_w=gdn_conv_w,
             gdn_a_log=gdn_a_log, gdn_dt_bias=gdn_dt_bias, gdn_norm_w=gdn_norm_w, s5_c_re=s5_c_re,
             s5_c_im=s5_c_im, s5_d=s5_d, s5_w_glu=s5_w_glu, s5_b_glu=s5_b_glu, w_out=w_out,
             final_norm_w=final_norm_w)
    b_ctx = x_prompt.shape[0]
    b_dec = x_sample.shape[0]
    cond8 = jnp.zeros((8, D_MODEL), F32).at[0].set(c_ctx).at[1:1 + b_dec].set(c)
    mods = _ada(cond8, w_ada, b_ada)
    mods_ctx = mods[:, 0:1].reshape(DEPTH, 1, N_MOD, D_MODEL)
    mods_dec = mods[:, 1:1 + b_dec].reshape(DEPTH, b_dec, N_MOD, D_MODEL)
    s5c = _s5_consts(s5_lam_re, s5_lam_im, s5_log_step, s5_b_re, s5_b_im, s5_c_re, s5_c_im)

    y_prompt, fin = _trunk(x_prompt, mods_ctx, 1, None, p, s5c, w_in)
    new_gla, new_gdn = fin[-1][0], fin[-1][1]
    s5f = jnp.stack([f[2].reshape(2, 2, b_ctx, S5_GROUPS, S5_P).transpose(1, 2, 0, 3, 4) for f in fin], axis=2)
    new_re, new_im = s5f[0], s5f[1]

    rows = x_sample.shape[1] // 64
    y_sample, _ = _trunk(x_sample, mods_dec, rows, (state_gla, state_gdn, state_s5_re, state_s5_im),
                         p, s5c, w_in)
    return (y_prompt, y_sample, new_gla.astype(state_gla.dtype), new_gdn.astype(state_gdn.dtype),
            new_re.astype(state_s5_re.dtype), new_im.astype(state_s5_im.dtype))
```

```python
import functools
import math

import numpy as np
import jax
import jax.numpy as jnp
from jax import lax
from jax.experimental import pallas as pl
from jax.experimental.pallas import tpu as pltpu

F32 = jnp.float32
BF16 = jnp.bfloat16

D_MODEL = 1024
DEPTH = 2
FFN_DIM = 2816
N_MOD = 9
EPS = 1e-6
CHUNK = 64
GLA_HEADS, GLA_DK, GLA_DV, GLA_RANK = 4, 32, 64, 16
GLA_GATE_NORM = 16.0
GDN_HEADS, GDN_DK, GDN_DV, GDN_CONV = 4, 128, 128, 5
S5_GROUPS, S5_CH, S5_P = 16, 16, 64
GLA_W = GLA_HEADS * GLA_DV
GLA_KW = GLA_HEADS * GLA_DK
GDN_W = GDN_HEADS * GDN_DV
S5_W = S5_GROUPS * S5_CH
S5_N = S5_GROUPS * S5_P
S5_L = 16
S5_PAIRS = S5_GROUPS // 2
MIX_W = GLA_W + GDN_W + S5_W

_IN_SIZES = (GLA_KW, GLA_KW, GLA_W, GLA_RANK, GLA_RANK, GLA_W,
             GDN_W, GDN_W, GDN_W, GDN_HEADS, GDN_HEADS, GDN_HEADS, GDN_HEADS, GDN_W, S5_W)
_IN_OFF = tuple(int(v) for v in np.cumsum((0,) + _IN_SIZES))
(_GQ, _GK, _GV, _GLRF, _GLRB, _GOG, _DQ, _DK, _DV, _DAF, _DAB, _DBF, _DBB, _DOG, _SU) = range(15)

P_DQKV = 0
P_SMALL = 1536
P_GQK = 1792
P_GV = 2048
P_GOG = 2304
P_DOG = 2560
P_SU = 3072
PROJ_W = 3328
PROJ_TN = PROJ_W // 2
SM_A = GLA_RANK
SM_B = GLA_RANK + GDN_HEADS

SUB = 256
GLA_SUB = 256
FFN_TM = 2048
FFN_TF = 256
PROJ_TM = 512
MIX_TM = 512
ADA_TN = 1024
V7X_VMEM_BYTES = 64 * 1024 * 1024
VMEM_LIMIT = V7X_VMEM_BYTES - 8 * 1024 * 1024


def _cparams(sem):
    return pltpu.CompilerParams(dimension_semantics=sem, vmem_limit_bytes=VMEM_LIMIT)


def _bdot(a, b):
    return jnp.dot(a.astype(BF16), b.astype(BF16), preferred_element_type=F32)


def _bdot_nt(a, b):
    return lax.dot_general(a.astype(BF16), b.astype(BF16), (((1,), (1,)), ((), ())),
                           preferred_element_type=F32)


def _split3(x):
    hi = x.astype(BF16)
    r1 = x - hi.astype(F32)
    mid = r1.astype(BF16)
    lo = (r1 - mid.astype(F32)).astype(BF16)
    return hi, mid, lo


def _sel_dot(m01, x):
    hi, mid, lo = _split3(x)
    return (jnp.dot(m01, hi, preferred_element_type=F32) + jnp.dot(m01, mid, preferred_element_type=F32)
            + jnp.dot(m01, lo, preferred_element_type=F32))


def _silu(x):
    return x * jax.nn.sigmoid(x)


def _softplus(x):
    return jnp.maximum(x, 0.0) + jnp.log1p(jnp.exp(-jnp.abs(x)))


def _log_sigmoid(x):
    return -_softplus(-x)


def _norm_mod(x, nw, scale, shift):
    y = x * lax.rsqrt(jnp.mean(x * x, axis=-1, keepdims=True) + EPS)
    return y * nw * (1.0 + scale) + shift


def _pick_tile(n, cap):
    t = cap
    while n % t:
        t //= 2
    return t


def _ada_kernel(c_ref, w_ref, b_ref, o_ref):
    o_ref[...] = _bdot(_silu(c_ref[...]), w_ref[...]) + b_ref[...]


def _ada(cond8, w_ada, b_ada):
    depth, d, nout = w_ada.shape
    tn = ADA_TN
    return pl.pallas_call(
        _ada_kernel,
        grid=(depth, nout // tn),
        in_specs=[pl.BlockSpec((8, d), lambda l, j: (0, 0)),
                  pl.BlockSpec((None, d, tn), lambda l, j: (l, 0, j)),
                  pl.BlockSpec((None, 1, tn), lambda l, j: (l, 0, j))],
        out_specs=pl.BlockSpec((None, 8, tn), lambda l, j: (l, 0, j)),
        out_shape=jax.ShapeDtypeStruct((depth, 8, nout), F32),
        compiler_params=_cparams(("parallel", "parallel")),
        name="ada",
    )(cond8, w_ada, b_ada.reshape(depth, 1, nout))


def _ffn_kernel(x_ref, mod_ref, nw_ref, wg_ref, wu_ref, wd_ref, fw_ref, o_ref, h_ref, *, mod_base, final):
    j = pl.program_id(1)

    @pl.when(j == 0)
    def _():
        m = mod_ref[...]
        h = _norm_mod(x_ref[...], nw_ref[...], m[mod_base + 1:mod_base + 2], m[mod_base:mod_base + 1])
        h_ref[...] = h.astype(BF16)
        o_ref[...] = jnp.zeros_like(o_ref)

    h = h_ref[...]
    g = jnp.dot(h, wg_ref[...].astype(BF16), preferred_element_type=F32)
    u = jnp.dot(h, wu_ref[...].astype(BF16), preferred_element_type=F32)
    o_ref[...] += _bdot(_silu(g) * u, wd_ref[...])

    @pl.when(j == pl.num_programs(1) - 1)
    def _():
        m = mod_ref[...]
        y = x_ref[...] + 0.5 * m[mod_base + 2:mod_base + 3] * o_ref[...]
        if final:
            y = y * lax.rsqrt(jnp.mean(y * y, axis=-1, keepdims=True) + EPS) * fw_ref[...]
        o_ref[...] = y


def _mod_index(nb, tm, n_seq):
    if nb == 1:
        return lambda i: 0
    return lambda i: (i * tm) // n_seq


def _ffn(x, mods, norm_w, w_gate, w_up, w_down, final_w, *, layer, which, n_seq, final):
    t, d = x.shape
    f = w_gate.shape[-1]
    nb = mods.shape[0]
    tm = _pick_tile(n_seq if nb > 1 else t, FFN_TM)
    tf = FFN_TF
    midx = _mod_index(nb, tm, n_seq)
    mod_base = 0 if which == 0 else 6
    return pl.pallas_call(
        functools.partial(_ffn_kernel, mod_base=mod_base, final=final),
        grid=(t // tm, f // tf),
        in_specs=[pl.BlockSpec((tm, d), lambda i, j: (i, 0)),
                  pl.BlockSpec((None, N_MOD, d), lambda i, j: (midx(i), 0, 0)),
                  pl.BlockSpec((None, None, 1, d), lambda i, j: (layer, 2 * which, 0, 0)),
                  pl.BlockSpec((None, None, d, tf), lambda i, j: (layer, which, 0, j)),
                  pl.BlockSpec((None, None, d, tf), lambda i, j: (layer, which, 0, j)),
                  pl.BlockSpec((None, None, tf, d), lambda i, j: (layer, which, j, 0)),
                  pl.BlockSpec((1, d), lambda i, j: (0, 0))],
        out_specs=pl.BlockSpec((tm, d), lambda i, j: (i, 0)),
        out_shape=jax.ShapeDtypeStruct((t, d), F32),
        scratch_shapes=[pltpu.VMEM((tm, d), BF16)],
        compiler_params=_cparams(("parallel", "arbitrary")),
        name="ffn",
    )(x, mods, norm_w.reshape(DEPTH, 3, 1, d), w_gate, w_up, w_down, final_w.reshape(1, d))


def _proj_kernel(x_ref, mod_ref, nw_ref, w_ref, cw_ref, o_ref, h_ref, wb_ref, st_ref, *, rowlen):
    j = pl.program_id(0)
    i = pl.program_id(1)

    for jt, segs in enumerate(_W_SEGS):
        @pl.when(jnp.logical_and(i == 0, j == jt))
        def _():
            for lo, src, width in segs:
                if width % 128 == 0:
                    wb_ref[lo:lo + width, :] = w_ref[src:src + width, :].astype(BF16)
                else:
                    base = lo // 128 * 128
                    if src is None:
                        st_ref[lo - base:lo - base + width, :] = jnp.zeros((width, st_ref.shape[1]), F32)
                        wb_ref[base:base + 128, :] = st_ref[...].astype(BF16)
                    else:
                        st_ref[lo - base:lo - base + width, :] = w_ref[src:src + width, :]

    @pl.when(j == 0)
    def _():
        m = mod_ref[...]
        h_ref[i] = _norm_mod(x_ref[...], nw_ref[...], m[4:5], m[3:4]).astype(BF16)
        o_ref[...] = _nt(h_ref[i], wb_ref[...])
        tm = o_ref.shape[0]
        pos = lax.broadcasted_iota(jnp.int32, (tm, 1), 0) % rowlen
        valid = [jnp.logical_and(pos + (t - GDN_CONV // 2) >= 0, pos + (t - GDN_CONV // 2) < rowlen)
                 for t in range(GDN_CONV)]
        for g in range(3 * GDN_HEADS):
            cols = slice(P_DQKV + g * GDN_DK, P_DQKV + (g + 1) * GDN_DK)
            xg = o_ref[:, cols]
            acc = xg * cw_ref[GDN_CONV // 2:GDN_CONV // 2 + 1, cols]
            for t in range(GDN_CONV):
                s = t - GDN_CONV // 2
                if s != 0:
                    xs = pltpu.roll(xg, (-s) % tm, axis=0)
                    acc = acc + jnp.where(valid[t], xs, 0.0) * cw_ref[t:t + 1, cols]
            y = _silu(acc)
            if g < 2 * GDN_HEADS:
                y = y * lax.rsqrt(jnp.sum(y * y, axis=-1, keepdims=True) + EPS)
                if g < GDN_HEADS:
                    y = y * (GDN_DK ** -0.5)
            o_ref[:, cols] = y

    @pl.when(j == 1)
    def _():
        o_ref[...] = _nt(h_ref[i], wb_ref[...])


def _proj(x, mods, norm_w, w_in, conv_w, *, layer, n_seq, rowlen):
    t, d = x.shape
    nb = mods.shape[0]
    tm = _pick_tile(n_seq if nb > 1 else t, PROJ_TM)
    midx = _mod_index(nb, tm, n_seq)
    return pl.pallas_call(
        functools.partial(_proj_kernel, rowlen=rowlen),
        grid=(PROJ_W // PROJ_TN, t // tm),
        in_specs=[pl.BlockSpec((tm, d), lambda j, i: (i * (1 - j), 0)),
                  pl.BlockSpec((None, N_MOD, d), lambda j, i: (midx(i), 0, 0)),
                  pl.BlockSpec((None, None, 1, d), lambda j, i: (layer, 1, 0, 0)),
                  pl.BlockSpec((None, w_in.shape[-1], d), lambda j, i: (layer, 0, 0), pipeline_mode=pl.Buffered(1)),
                  pl.BlockSpec((None, GDN_CONV, 3 * GDN_W), lambda j, i: (layer, 0, 0))],
        out_specs=pl.BlockSpec((tm, PROJ_TN), lambda j, i: (i, j)),
        out_shape=jax.ShapeDtypeStruct((t, PROJ_W), F32),
        scratch_shapes=[pltpu.VMEM((t // tm, tm, d), BF16), pltpu.VMEM((PROJ_TN, d), BF16),
                        pltpu.VMEM((128, d), F32)],
        compiler_params=_cparams(("arbitrary", "arbitrary")),
        name="proj",
    )(x, mods, norm_w.reshape(DEPTH, 3, 1, d), jnp.swapaxes(w_in, 1, 2), conv_w)


def _w_in_segments():
    order = [_DQ, _DK, _DV, _GLRF, _DAF, _DBF, None, _GLRB, _DAB, _DBB, None,
             _GQ, _GK, _GV, _GOG, _DOG, _SU]
    sm_pad = 128 - GLA_RANK - 2 * GDN_HEADS
    tiles = [[] for _ in range(PROJ_W // PROJ_TN)]
    dst = 0
    for s in order:
        width = sm_pad if s is None else _IN_SIZES[s]
        src = None if s is None else _IN_OFF[s]
        tile = tiles[dst // PROJ_TN]
        lo = dst % PROJ_TN
        if tile and src is not None and tile[-1][1] is not None and (
                tile[-1][0] + tile[-1][2] == lo and tile[-1][1] + tile[-1][2] == src):
            tile[-1] = (tile[-1][0], tile[-1][1], tile[-1][2] + width)
        else:
            tile.append((lo, src, width))
        dst += width
        assert (dst - 1) // PROJ_TN == (dst - width) // PROJ_TN, "a segment may not straddle column tiles"
    assert dst == PROJ_W
    return tiles


_W_SEGS = _w_in_segments()


def _np_consts():
    t = np.arange(CHUNK)
    lower = (t[:, None] >= t[None, :])
    incl = np.stack([lower, lower.T])
    strict = np.stack([t[:, None] > t[None, :], t[:, None] < t[None, :]])
    eye4 = np.eye(4, dtype=bool)
    blk = np.kron(eye4, np.ones((CHUNK, CHUNK), bool))
    incl_bd = np.stack([np.kron(eye4, incl[d]) for d in range(2)])
    strict_bd = np.stack([np.kron(eye4, strict[d]) for d in range(2)])
    levels = []
    for d in range(2):
        per = []
        for lv in range(6):
            s = 1 << lv
            same = (t[:, None] // (2 * s)) == (t[None, :] // (2 * s))
            hi = (t % (2 * s)) >= s
            m = same & hi[:, None] & (~hi)[None, :]
            if d == 1:
                m = m.T
            per.append(np.kron(eye4, m))
        levels.append(np.stack(per))
    levels = np.stack(levels)
    assert (levels.sum(1) == strict_bd).all() and blk.any()
    hm_k = np.kron(eye4, np.ones((1, GLA_DK)))
    hm_v = np.kron(eye4, np.ones((1, GLA_DV)))
    st_mask = np.kron(eye4, np.ones((GLA_DV, GLA_DK)))
    ind64 = np.kron(eye4, np.full((GLA_DV, GLA_DV), 1.0 / GLA_DV))
    gla_sel, gla_later, gla_earlier, gla_pair = [], [], [], []
    for d in range(2):
        pos = t if d == 0 else CHUNK - 1 - t
        cum = (pos[None, :] <= pos[:, None]).astype(np.float64)
        sel, later, earlier, pair = [cum], [], [], []
        for lv in range(6):
            s = 1 << lv
            ref_pos = (pos // (2 * s)) * (2 * s) + s - 1
            pick = (pos[None, :] == ref_pos[:, None]).astype(np.float64)
            sel.append(pick @ cum)
            is_later = (pos % (2 * s)) >= s
            same = (pos[:, None] // (2 * s)) == (pos[None, :] // (2 * s))
            later.append(np.repeat(is_later[:, None], GLA_KW, axis=1))
            earlier.append(np.repeat(~is_later[:, None], GLA_KW, axis=1))
            pair.append(np.tile(same & is_later[:, None] & (~is_later)[None, :], (1, 4)))
        gla_sel.append(np.concatenate(sel, axis=0))
        gla_later.append(np.stack(later))
        gla_earlier.append(np.stack(earlier))
        gla_pair.append(np.stack(pair))
        assert (np.stack(pair).sum(0) == np.tile(pos[:, None] > pos[None, :], (1, 4))).all()
    head_sum = np.kron(eye4, np.ones((GLA_DK, GLA_DV)))
    return dict(incl=incl, incl_bd=incl_bd, strict_bd=strict_bd, levels=levels,
                hm_k=hm_k, hm_v=hm_v, st_mask=st_mask, ind64=ind64, eye256=np.eye(256),
                gla_sel=np.stack(gla_sel), gla_later=np.stack(gla_later), gla_earlier=np.stack(gla_earlier),
                gla_pair=np.stack(gla_pair), head_sum=head_sum)


_C = _np_consts()


def _row_block_index(nblk):
    return lambda d, b, i: b * nblk + i + d * (nblk - 1 - 2 * i)


def _gla_kernel(qkf_ref, qkb_ref, vf_ref, vb_ref, smf_ref, smb_ref, up_ref, bias_ref, sel_ref, later_ref,
                earlier_ref, pair_ref, hmk_ref, hmv_ref, stm_ref, hsum_ref, *rest, nchunk, zero_init):
    rest = list(rest)
    st0_ref = None if zero_init else rest.pop(0)
    rest.pop(0)
    of_ref, ob_ref, stf_ref, st_ref, qd_sc, m_sc, dec_sc = rest
    ins =((qkf_ref, vf_ref, smf_ref, of_ref), (qkb_ref, vb_ref, smb_ref, ob_ref))

    @pl.when(pl.program_id(1) == 0)
    def _():
        st_ref[...] = jnp.zeros_like(st_ref) if zero_init else st0_ref[...]

    hmkb = hmk_ref[...].astype(BF16)
    hmv = hmv_ref[...]
    stm = stm_ref[...]

    units = [(dd, c) for c in range(nchunk) for dd in range(2)]
    st1 = []
    for dd, c in units:
        qk_ref, v_ref, sm_ref, _ = ins[dd]
        rows = pl.ds(c * CHUNK, CHUNK)
        qk = qk_ref[rows, :]
        q = qk[:, 0:GLA_KW] * (GLA_DK ** -0.5)
        k = qk[:, GLA_KW:2 * GLA_KW]
        x = _bdot(sm_ref[rows, :], up_ref[dd]) + bias_ref[dd]
        lg = _log_sigmoid(x) * (1.0 / GLA_GATE_NORM)
        ball = _sel_dot(sel_ref[dd], lg)
        st1.append((q, k, v_ref[rows, :], ball, jnp.sum(lg, axis=0, keepdims=True)))
    prods = []
    for lv in range(6):
        for (dd, c), (q, k, _, ball, _) in zip(units, st1):
            b = ball[0:CHUNK]
            ref = ball[(lv + 1) * CHUNK:(lv + 2) * CHUNK]
            ql = q * jnp.where(later_ref[dd, lv] > 0.5, jnp.exp(jnp.minimum(b - ref, 0.0)), 0.0)
            kl = (k * jnp.where(earlier_ref[dd, lv] > 0.5, jnp.exp(jnp.minimum(ref - b, 0.0)), 0.0)).astype(BF16)
            kexp = jnp.concatenate([kl * hmkb[h:h + 1] for h in range(GLA_HEADS)], axis=0)
            prods.append(_bdot_nt(ql, kexp))
    for i, ((dd, c), (q, k, v, ball, btot)) in enumerate(zip(units, st1)):
        o_ref = ins[dd][3]
        n = dd * nchunk + c
        b = ball[0:CHUNK]
        att = jnp.zeros((CHUNK, GLA_HEADS * CHUNK), F32)
        for lv in range(6):
            att = jnp.where(pair_ref[dd, lv] > 0.5, prods[lv * len(units) + i], att)
        vexp = jnp.concatenate([v * hmv[h:h + 1] for h in range(GLA_HEADS)], axis=0)
        o_ref[pl.ds(c * CHUNK, CHUNK), :] = _bdot(att, vexp) + _bdot(q * k, hsum_ref[...]) * v
        qd_sc[n] = (q * jnp.exp(b)).astype(BF16)
        kd = k * jnp.exp(btot - b)
        m_sc[n] = _bdot(v.T, kd) * stm
        dec_sc[n] = jnp.broadcast_to(jnp.exp(btot), (8, GLA_KW))

    for dd in range(2):
        o_ref = ins[dd][3]
        st = st_ref[dd]
        for ci in range(nchunk):
            c = ci if dd == 0 else nchunk - 1 - ci
            n = dd * nchunk + c
            rows = pl.ds(c * CHUNK, CHUNK)
            o_ref[rows, :] += _nt(qd_sc[n], st.astype(BF16))
            st = st * dec_sc[n][0:1] + m_sc[n]
        st_ref[dd] = st
        for h in range(GLA_HEADS):
            stf_ref[dd, h] = st[h * GLA_DV:(h + 1) * GLA_DV, h * GLA_DK:(h + 1) * GLA_DK].T


def _gla(proj, up_p, bias_p, st0, fin, *, layer, nseq, n_seq):
    t = proj.shape[0]
    sub = min(n_seq, GLA_SUB)
    nblk = n_seq // sub
    nck = sub // CHUNK
    rbi = _row_block_index(nblk)
    c = _C
    consts = [jnp.asarray(c["gla_sel"], BF16), jnp.asarray(c["gla_later"], F32), jnp.asarray(c["gla_earlier"], F32),
              jnp.asarray(c["gla_pair"], F32), jnp.asarray(c["hm_k"], F32), jnp.asarray(c["hm_v"], F32),
              jnp.asarray(c["st_mask"], F32), jnp.asarray(c["head_sum"], BF16)]
    st_spec = pl.BlockSpec((2, None, GLA_W, GLA_KW), lambda b, i: (0, b, 0, 0))
    full = lambda a: pl.BlockSpec(a.shape, lambda b, i: (0,) * a.ndim)
    zero_init = st0 is None
    nu = 2 * nck
    rows = (lambda b, i: rbi(0, b, i), lambda b, i: rbi(1, b, i))

    def blk(width, col, dd):
        return pl.BlockSpec((sub, width), lambda b, i: (rows[dd](b, i), col // width))

    hs = (GLA_HEADS, GLA_DK, GLA_DV)
    args = [proj, proj, proj, proj, proj, proj, up_p, bias_p, *consts] + ([] if zero_init else [st0])
    in_specs = ([blk(2 * GLA_KW, P_GQK, 0), blk(2 * GLA_KW, P_GQK, 1), blk(GLA_W, P_GV, 0), blk(GLA_W, P_GV, 1),
                 blk(128, P_SMALL, 0), blk(128, P_SMALL + 128, 1), full(up_p), full(bias_p)]
                + [full(a) for a in consts] + ([] if zero_init else [st_spec]))
    return pl.pallas_call(
        functools.partial(_gla_kernel, nchunk=nck, zero_init=zero_init),
        grid=(nseq, nblk),
        in_specs=in_specs + [pl.BlockSpec(memory_space=pl.ANY)],
        out_specs=[pl.BlockSpec((sub, GLA_W), lambda b, i: (rows[0](b, i), 0)),
                   pl.BlockSpec((sub, GLA_W), lambda b, i: (rows[1](b, i), 0)), _fin_spec(hs, layer)],
        out_shape=[jax.ShapeDtypeStruct((t, GLA_W), F32), jax.ShapeDtypeStruct((t, GLA_W), F32),
                   jax.ShapeDtypeStruct((nseq, DEPTH, 2) + hs, F32)],
        scratch_shapes=[pltpu.VMEM((2, GLA_W, GLA_KW), F32), pltpu.VMEM((nu, CHUNK, GLA_KW), BF16),
                        pltpu.VMEM((nu, GLA_W, GLA_KW), F32), pltpu.VMEM((nu, 8, GLA_KW), F32)],
        input_output_aliases={len(args): 2},
        compiler_params=_cparams(("arbitrary", "arbitrary")),
        name="gla",
    )(*args, fin)


def _fin_spec(hs, layer):
    return pl.BlockSpec((None, None, 2) + hs, lambda b, i: (b, layer, 0) + (0,) * len(hs))


def _gdn_kernel(qkvf_ref, qkvb_ref, smf_ref, smb_ref, arow_ref, dtb_ref, tri_ref, incl_ref, strict_ref, lvl_ref,
                eye_ref, *rest, nchunk, zero_init):
    rest = list(rest)
    s0_ref = None if zero_init else rest.pop(0)
    rest.pop(0)
    of_ref, ob_ref, sf_ref, s_ref, la_ref, be_ref, w_sc, u0_sc, qkd_sc, qe_sc, kdt_sc, egl_sc = rest
    qkv_refs = (qkvf_ref, qkvb_ref)
    o_refs = (of_ref, ob_ref)

    @pl.when(pl.program_id(1) == 0)
    def _():
        s_ref[...] = jnp.zeros_like(s_ref) if zero_init else s0_ref[...]

    for dd, sm_ref in enumerate((smf_ref, smb_ref)):
        sm = sm_ref[...]
        la_ref[dd] = -jnp.exp(arow_ref[dd]) * _softplus(sm + dtb_ref[dd])
        be_ref[dd] = jax.nn.sigmoid(sm)

    eye = eye_ref[...]

    def stack(ref, r0, lo, width):
        return jnp.concatenate([ref[pl.ds(r0, CHUNK), lo + h * width:lo + (h + 1) * width]
                                for h in range(GDN_HEADS)], axis=0)

    def colstack(a, lo):
        return jnp.concatenate([a[:, lo + h:lo + h + 1] for h in range(GDN_HEADS)], axis=0)

    units = [(dd, c) for c in range(nchunk) for dd in range(2)]
    st1 = []
    for dd, c in units:
        r0 = c * CHUNK
        qkv_ref = qkv_refs[dd]
        qs = stack(qkv_ref, r0, 0, GDN_DK)
        ks = stack(qkv_ref, r0, GDN_W, GDN_DK)
        la = la_ref[dd, pl.ds(r0, CHUNK), :]
        g = _sel_dot(tri_ref[dd], la)
        ksb = ks.astype(BF16)
        st1.append((qs, ks, la, g, _bdot_nt(ksb, ksb), _bdot_nt(qs, ksb)))
    loc = []
    for (dd, c), (qs, ks, la, g, kk, qk) in zip(units, st1):
        r0 = c * CHUNK
        n = dd * nchunk + c
        vs = stack(qkv_refs[dd], r0, 2 * GDN_W, GDN_DV)
        gtot = jnp.sum(la, axis=0, keepdims=True)
        gcol = colstack(g, SM_A)
        kfcol = colstack(jnp.exp(gtot - g), SM_A)
        egcol = colstack(jnp.exp(g), SM_A)
        bcol = colstack(be_ref[dd, pl.ds(r0, CHUNK), :], SM_B)
        grow = jnp.sum(eye * gcol, axis=0, keepdims=True)
        lm_rows, qkd_rows = [], []
        for h in range(GDN_HEADS):
            rs = slice(h * CHUNK, (h + 1) * CHUNK)
            lt = (h * CHUNK) // 128
            ls = slice(lt * 128, (lt + 1) * 128)
            dec = incl_ref[dd, rs, ls] * jnp.exp(jnp.minimum(gcol[rs] - grow[:, ls], 0.0))
            lm = strict_ref[dd, rs, ls] * dec * kk[rs, ls] * bcol[rs]
            qd = qk[rs, ls] * dec
            zero = jnp.zeros_like(lm)
            lm_rows.append(jnp.concatenate([lm, zero] if lt == 0 else [zero, lm], axis=1))
            qkd_rows.append(jnp.concatenate([qd, zero] if lt == 0 else [zero, qd], axis=1))
        lmat = jnp.concatenate(lm_rows, axis=0)
        rhs = jnp.concatenate([ks * (bcol * egcol), vs * bcol], axis=1).astype(BF16)
        qkd_sc[n] = jnp.concatenate(qkd_rows, axis=0).astype(BF16)
        qe_sc[n] = (qs * egcol).astype(BF16)
        kd = ks * kfcol
        for h in range(GDN_HEADS):
            kdt_sc[n * GDN_HEADS + h] = kd[h * CHUNK:(h + 1) * CHUNK].T.astype(BF16)
        egl_sc[n] = jnp.broadcast_to(jnp.exp(gtot), (8, 128))
        loc.append((lmat.astype(BF16), rhs))
    def inv_start(wave):
        return [eye - (loc[i][0] * lvl_ref[units[i][0], 0]).astype(F32) for i in wave]

    def inv_level(wave, tinv, lv):
        tb = [t_.astype(BF16) for t_ in tinv]
        cs = [jnp.dot(loc[i][0] * lvl_ref[units[i][0], lv], t_, preferred_element_type=F32)
              for i, t_ in zip(wave, tb)]
        return [t32 - jnp.dot(t_, c_.astype(BF16), preferred_element_type=F32)
                for t32, t_, c_ in zip(tinv, tb, cs)]

    def solve(wave, tinv):
        for i, t_ in zip(wave, tinv):
            dd, c = units[i]
            sol = jnp.dot(t_.astype(BF16), loc[i][1], preferred_element_type=F32)
            w_sc[dd * nchunk + c] = sol[:, :GDN_DK].astype(BF16)
            u0_sc[dd * nchunk + c] = sol[:, GDN_DK:]

    s_cur = [[s_ref[dd, h] for h in range(GDN_HEADS)] for dd in range(2)]

    def recur(ci):
        step = []
        for dd in range(2):
            c = ci if dd == 0 else nchunk - 1 - ci
            n = dd * nchunk + c
            sb = [s_.astype(BF16) for s_ in s_cur[dd]]
            ub = []
            for h in range(GDN_HEADS):
                rows = pl.ds(h * CHUNK, CHUNK)
                u = u0_sc[n, rows, :] - jnp.dot(w_sc[n, rows, :], sb[h], preferred_element_type=F32)
                ub.append(u.astype(BF16))
            step.append((c, n, sb, ub))
        for dd, (c, n, sb, ub) in enumerate(step):
            egl = egl_sc[n]
            for h in range(GDN_HEADS):
                s_cur[dd][h] = (egl[0:1, SM_A + h:SM_A + h + 1] * s_cur[dd][h]
                                + jnp.dot(kdt_sc[n * GDN_HEADS + h], ub[h], preferred_element_type=F32))
        for dd, (c, n, sb, ub) in enumerate(step):
            o2 = jnp.dot(qkd_sc[n], jnp.concatenate(ub, axis=0), preferred_element_type=F32)
            for h in range(GDN_HEADS):
                rows = pl.ds(h * CHUNK, CHUNK)
                o_refs[dd][pl.ds(c * CHUNK, CHUNK), h * GDN_DV:(h + 1) * GDN_DV] = (
                    jnp.dot(qe_sc[n, rows, :], sb[h], preferred_element_type=F32) + o2[h * CHUNK:(h + 1) * CHUNK])

    every = list(range(len(units)))
    tinv = inv_start(every)
    for lv in range(1, 6):
        tinv = inv_level(every, tinv, lv)
    solve(every, tinv)
    for ci in range(nchunk):
        recur(ci)
    for dd in range(2):
        for h in range(GDN_HEADS):
            s_ref[dd, h] = s_cur[dd][h]
    sf_ref[...] = s_ref[...]


def _gdn(proj, arow, dtb, s0, fin, *, layer, nseq, n_seq):
    t = proj.shape[0]
    nblk = n_seq // SUB
    rbi = _row_block_index(nblk)
    c = _C
    consts = [jnp.asarray(c["incl"], BF16), jnp.asarray(c["incl_bd"], F32), jnp.asarray(c["strict_bd"], F32),
              jnp.asarray(c["levels"], BF16), jnp.asarray(c["eye256"], F32)]
    hs = (GDN_HEADS, GDN_DK, GDN_DV)
    nck = SUB // CHUNK
    nu = 2 * nck
    zero_init = s0 is None
    s_spec = pl.BlockSpec((2, None) + hs, lambda b, i: (0, b, 0, 0, 0))
    full = lambda shape: pl.BlockSpec(shape, lambda b, i: (0,) * len(shape))
    rows = (lambda b, i: rbi(0, b, i), lambda b, i: rbi(1, b, i))
    args = [proj, proj, proj, proj, arow, dtb, *consts] + ([] if zero_init else [s0])
    return pl.pallas_call(
        functools.partial(_gdn_kernel, nchunk=nck, zero_init=zero_init),
        grid=(nseq, nblk),
        in_specs=[pl.BlockSpec((SUB, 3 * GDN_W), lambda b, i: (rows[0](b, i), P_DQKV // (3 * GDN_W))),
                  pl.BlockSpec((SUB, 3 * GDN_W), lambda b, i: (rows[1](b, i), P_DQKV // (3 * GDN_W))),
                  pl.BlockSpec((SUB, 128), lambda b, i: (rows[0](b, i), P_SMALL // 128)),
                  pl.BlockSpec((SUB, 128), lambda b, i: (rows[1](b, i), P_SMALL // 128 + 1)),
                  full((2, 1, 128)), full((2, 1, 128)), full((2, CHUNK, CHUNK)),
                  full((2, 256, 256)), full((2, 256, 256)), full((2, 6, 256, 256)), full((256, 256))]
                 + ([] if zero_init else [s_spec]) + [pl.BlockSpec(memory_space=pl.ANY)],
        out_specs=[pl.BlockSpec((SUB, GDN_W), lambda b, i: (rows[0](b, i), 0)),
                   pl.BlockSpec((SUB, GDN_W), lambda b, i: (rows[1](b, i), 0)), _fin_spec(hs, layer)],
        out_shape=[jax.ShapeDtypeStruct((t, GDN_W), F32), jax.ShapeDtypeStruct((t, GDN_W), F32),
                   jax.ShapeDtypeStruct((nseq, DEPTH, 2) + hs, F32)],
        input_output_aliases={len(args): 2},
        scratch_shapes=[pltpu.VMEM((2,) + hs, F32),
                        pltpu.VMEM((2, SUB, 128), F32), pltpu.VMEM((2, SUB, 128), F32),
                        pltpu.VMEM((nu, 256, GDN_DK), BF16), pltpu.VMEM((nu, 256, GDN_DV), F32),
                        pltpu.VMEM((nu, 256, 256), BF16), pltpu.VMEM((nu, 256, GDN_DK), BF16),
                        pltpu.VMEM((nu * GDN_HEADS, GDN_DK, CHUNK), BF16), pltpu.VMEM((nu, 8, 128), F32)],
        compiler_params=_cparams(("arbitrary", "arbitrary")),
        name="gdn",
    )(*args, fin)


def _cmul(ar, ai, br, bi):
    return ar * br - ai * bi, ar * bi + ai * br


def _split2(x):
    hi = x.astype(BF16)
    return hi, (x - hi.astype(F32)).astype(BF16)


def _s5_consts_kernel(lr_ref, li_ref, st_ref, br_ref, bi_ref, cr_ref, ci_ref, gm_ref, bm_ref,
                      laml_ref, wexp_ref, vexp_ref, tt_ref, w_ref, v_ref, a_ref, q_ref):
    d = pl.program_id(0) % 2
    lc = S5_L * S5_CH

    def trows(s):
        return pl.ds(pl.multiple_of((s + d * (S5_L - 1 - 2 * s)) * S5_CH, S5_CH), S5_CH)

    lr = lr_ref[...]
    li = li_ref[...]
    step = jnp.exp(st_ref[...])
    mag = jnp.exp(lr * step)
    lbr = mag * jnp.cos(li * step)
    lbi = mag * jnp.sin(li * step)
    nr = lbr - 1.0
    den = lr * lr + li * li
    fr = (nr * lr + lbi * li) / den
    fi = (lbi * lr - nr * li) / den
    bbr, bbi = _cmul(fr, fi, br_ref[...], bi_ref[...])
    cr = cr_ref[...]
    ci = ci_ref[...]
    pr = jnp.ones_like(lr)
    pi = jnp.zeros_like(lr)
    for s in range(S5_L):
        ar, ai = _cmul(cr, ci, pr, pi)
        a_ref[0, trows(s), :] = ar
        a_ref[1, trows(s), :] = ai
        wr, wi = _cmul(pr, pi, bbr, bbi)
        w_ref[0, trows(S5_L - 1 - s), :] = wr
        w_ref[1, trows(S5_L - 1 - s), :] = wi
        pr, pi = _cmul(pr, pi, lbr, lbi)
        vr, vi = _cmul(cr, ci, pr, pi)
        v_ref[0, trows(s), :] = vr
        v_ref[1, trows(s), :] = -vi
    laml_ref[0:1, :] = pr
    laml_ref[1:2, :] = pi

    lane = lax.broadcasted_iota(jnp.int32, (1, 128), 1)
    for gp in range(S5_PAIRS):
        lanes = slice(gp * 128, (gp + 1) * 128)
        for r in range(2):
            for src, dst in ((w_ref, wexp_ref), (v_ref, vexp_ref)):
                slab = src[r, :, lanes]
                dst[r, gp] = jnp.concatenate([jnp.where(lane < S5_P, slab, 0.0),
                                              jnp.where(lane >= S5_P, slab, 0.0)], axis=0).astype(BF16)

    gm = gm_ref[...]
    kall = None
    for r, b, sign in ((0, bbr, 1.0), (1, bbi, -1.0)):
        ah, al = _split2(a_ref[r])
        bh, bl = _split2(jnp.concatenate([b] * S5_GROUPS, axis=0) * gm)
        nt = lambda x, y: lax.dot_general(x, y, (((1,), (1,)), ((), ())), preferred_element_type=F32)
        term = (nt(ah, bh) + nt(ah, bl) + nt(al, bh)) * sign
        kall = term if kall is None else kall + term

    zeros = jnp.zeros((lc, S5_W), F32)
    for delta in range(S5_GROUPS):
        q_ref[delta, 0:lc, :] = zeros
        q_ref[delta, lc:2 * lc, :] = kall if delta == 0 else pltpu.roll(kall, delta * S5_CH, axis=1)
        q_ref[delta, 2 * lc:3 * lc, :] = zeros
    for g in range(S5_GROUPS):
        acc = zeros
        for s in range(S5_L):
            start = pl.multiple_of(lc - S5_CH * s + d * (S5_CH * (S5_L - 1)), S5_CH)
            blk = q_ref[(s - g) % S5_GROUPS, pl.ds(start, lc), :]
            acc = jnp.where(bm_ref[s:s + 1, :] > 0.5, blk, acc)
        tt_ref[g] = acc.astype(BF16)


def _s5_consts(lam_re, lam_im, log_step, b_re, b_im, c_re, c_im):
    n = DEPTH * 2
    lc = S5_L * S5_CH
    lr = lam_re.reshape(n, 1, S5_N)
    li = lam_im.reshape(n, 1, S5_N)
    st = jnp.repeat(log_step.reshape(n, S5_GROUPS), S5_P, axis=-1).reshape(n, 1, S5_N)
    br = b_re.reshape(n, S5_N, S5_CH).transpose(0, 2, 1)
    bi = b_im.reshape(n, S5_N, S5_CH).transpose(0, 2, 1)
    cr = c_re.reshape(n, S5_GROUPS, S5_CH, S5_P).transpose(0, 2, 1, 3).reshape(n, S5_CH, S5_N)
    ci = c_im.reshape(n, S5_GROUPS, S5_CH, S5_P).transpose(0, 2, 1, 3).reshape(n, S5_CH, S5_N)
    gm = jnp.asarray(np.kron(np.eye(S5_GROUPS), np.ones((S5_CH, S5_P))), F32)
    row = pl.BlockSpec((None, 1, S5_N), lambda i: (i, 0, 0))
    mat = pl.BlockSpec((None, S5_CH, S5_N), lambda i: (i, 0, 0))
    exp = pl.BlockSpec((None, 2, S5_PAIRS, 2 * lc, 128), lambda i: (i, 0, 0, 0, 0))
    laml, wexp, vexp, tt = pl.pallas_call(
        _s5_consts_kernel,
        grid=(n,),
        in_specs=[row, row, row, mat, mat, mat, mat, pl.BlockSpec((S5_W, S5_N), lambda i: (0, 0)),
                  pl.BlockSpec((S5_GROUPS, S5_W), lambda i: (0, 0))],
        out_specs=[pl.BlockSpec((None, 2, S5_N), lambda i: (i, 0, 0)), exp, exp,
                   pl.BlockSpec((None, S5_GROUPS, lc, lc), lambda i: (i, 0, 0, 0))],
        out_shape=[jax.ShapeDtypeStruct((n, 2, S5_N), F32),
                   jax.ShapeDtypeStruct((n, 2, S5_PAIRS, 2 * lc, 128), BF16),
                   jax.ShapeDtypeStruct((n, 2, S5_PAIRS, 2 * lc, 128), BF16),
                   jax.ShapeDtypeStruct((n, S5_GROUPS, lc, lc), BF16)],
        scratch_shapes=[pltpu.VMEM((2, lc, S5_N), F32), pltpu.VMEM((2, lc, S5_N), F32),
                        pltpu.VMEM((2, lc, S5_N), F32), pltpu.VMEM((S5_GROUPS, 3 * lc, S5_W), F32)],
        compiler_params=_cparams(("arbitrary",)),
        name="s5_consts",
    )(lr, li, st, br, bi, cr, ci, gm, _lane_block_masks())
    return laml.reshape(n, 2, S5_PAIRS, 128), wexp, vexp, tt


def _lane_block_masks():
    return jnp.asarray(np.kron(np.eye(S5_GROUPS), np.ones((1, S5_CH))), F32)


def _nt(a, b):
    return lax.dot_general(a, b, (((1,), (1,)), ((), ())), preferred_element_type=F32)


def _lane_block_transpose(arrs):
    n = len(arrs)
    width = arrs[0].shape[1]
    blk = lax.broadcasted_iota(jnp.int32, (1, width), 1) // (width // n)
    arrs = list(arrs)
    h = n // 2
    while h >= 1:
        hi = (blk & h) != 0
        for a in range(n):
            if a & h:
                continue
            lo_arr, hi_arr = arrs[a], arrs[a + h]
            arrs[a] = jnp.where(hi, pltpu.roll(hi_arr, h * (width // n), axis=1), lo_arr)
            arrs[a + h] = jnp.where(hi, hi_arr, pltpu.roll(lo_arr, width - h * (width // n), axis=1))
        h //= 2
    return arrs


def _s5_kernel(u0_ref, u1_ref, tt_ref, wexp_ref, vexp_ref, laml_ref, *rest, nseq, nblk, zero_init):
    if zero_init:
        y_ref, hf_ref, ug_sc, yg_sc, hin_re, hin_im, hp_re, hp_im = rest
    else:
        h0_ref, y_ref, hf_ref, ug_sc, yg_sc, hin_re, hin_im, hp_re, hp_im = rest
    d = pl.program_id(0)
    r = nblk * nseq
    lc = S5_L * S5_CH

    @pl.when(d == 0)
    def _():
        xs = []
        for s in range(S5_L):
            rows = pl.ds(s, r, stride=S5_L)
            xs.append(jnp.concatenate([u0_ref[rows, :], u1_ref[rows, :]], axis=1).astype(BF16))
        for g, ug in enumerate(_lane_block_transpose(xs)):
            ug_sc[g] = ug

    for gp in range(S5_PAIRS):
        up = jnp.concatenate([ug_sc[2 * gp], ug_sc[2 * gp + 1]], axis=1)
        rows = pl.ds(gp, r, stride=S5_PAIRS)
        hin_re[rows, :] = jnp.dot(up, wexp_ref[0, gp], preferred_element_type=F32)
        hin_im[rows, :] = jnp.dot(up, wexp_ref[1, gp], preferred_element_type=F32)

    lr = laml_ref[0]
    li = laml_ref[1]

    def step(kk, carry):
        k = kk + d * (nblk - 1 - 2 * kk)
        new = []
        for b in range(nseq):
            rows = pl.ds(pl.multiple_of((b * nblk + k) * S5_PAIRS, S5_PAIRS), S5_PAIRS)
            hr, hi = carry[2 * b], carry[2 * b + 1]
            hp_re[rows, :] = hr
            hp_im[rows, :] = hi
            new += [lr * hr - li * hi + hin_re[rows, :], lr * hi + li * hr + hin_im[rows, :]]
        return tuple(new)

    init = []
    for b in range(nseq):
        rows = slice(b * S5_PAIRS, (b + 1) * S5_PAIRS)
        init += ([jnp.zeros((S5_PAIRS, 128), F32)] * 2 if zero_init else [h0_ref[0, rows, :], h0_ref[1, rows, :]])
    fin = lax.fori_loop(0, nblk, step, tuple(init))
    for b in range(nseq):
        rows = slice(b * S5_PAIRS, (b + 1) * S5_PAIRS)
        hf_ref[0, rows, :] = fin[2 * b]
        hf_ref[1, rows, :] = fin[2 * b + 1]

    for gp in range(S5_PAIRS):
        rows = pl.ds(gp, r, stride=S5_PAIRS)
        ys = (_nt(hp_re[rows, :].astype(BF16), vexp_ref[0, gp])
              + _nt(hp_im[rows, :].astype(BF16), vexp_ref[1, gp]))
        for j in range(2):
            g = 2 * gp + j
            yg = _nt(ug_sc[g], tt_ref[g]) + ys[:, j * lc:(j + 1) * lc]

            @pl.when(d == 0)
            def _():
                yg_sc[g] = yg

            @pl.when(d == 1)
            def _():
                yg_sc[g] += yg

    @pl.when(d == 1)
    def _():
        for t, yt in enumerate(_lane_block_transpose([yg_sc[g] for g in range(S5_GROUPS)])):
            rows = pl.ds(t, r, stride=S5_L)
            y_ref[0, rows, :] = yt[:, 0:128]
            y_ref[1, rows, :] = yt[:, 128:2 * 128]


def _s5(proj, consts, h0, *, layer, nseq, n_seq):
    laml, wexp, vexp, tt = consts
    t = proj.shape[0]
    nblk = n_seq // S5_L
    r = nblk * nseq
    lc = S5_L * S5_CH
    bs = nseq * S5_PAIRS
    zero_init = h0 is None
    h_spec = pl.BlockSpec((None, 2, bs, 128), lambda d: (d, 0, 0, 0))
    return pl.pallas_call(
        functools.partial(_s5_kernel, nseq=nseq, nblk=nblk, zero_init=zero_init),
        grid=(2,),
        in_specs=[pl.BlockSpec((t, 128), lambda d: (0, P_SU // 128)),
                  pl.BlockSpec((t, 128), lambda d: (0, P_SU // 128 + 1)),
                  pl.BlockSpec((None, S5_GROUPS, lc, lc), lambda d: (2 * layer + d, 0, 0, 0)),
                  pl.BlockSpec((None, 2, S5_PAIRS, 2 * lc, 128), lambda d: (2 * layer + d, 0, 0, 0, 0)),
                  pl.BlockSpec((None, 2, S5_PAIRS, 2 * lc, 128), lambda d: (2 * layer + d, 0, 0, 0, 0)),
                  pl.BlockSpec((None, 2, S5_PAIRS, 128), lambda d: (2 * layer + d, 0, 0, 0))]
                 + ([] if zero_init else [h_spec]),
        out_specs=[pl.BlockSpec((2, t, 128), lambda d: (0, 0, 0)), h_spec],
        out_shape=[jax.ShapeDtypeStruct((2, t, 128), F32),
                   jax.ShapeDtypeStruct((2, 2, bs, 128), F32)],
        scratch_shapes=[pltpu.VMEM((S5_GROUPS, r, lc), BF16), pltpu.VMEM((S5_GROUPS, r, lc), F32)]
                       + [pltpu.VMEM((r * S5_PAIRS, 128), F32) for _ in range(4)],
        compiler_params=_cparams(("arbitrary",)),
        name="s5",
    )(proj, proj, tt, wexp, vexp, laml, *([] if zero_init else [h0]))


def _out_kernel(x_ref, mod_ref, ogf_ref, ogb_ref, odf_ref, odb_ref, ysf_ref, ysb_ref, gog_ref, su_ref, dog_ref,
                ind_ref, gw_ref, dw_ref, sd_ref, wglu_ref, bglu_ref, wout32_ref, o_ref, wout_ref):
    @pl.when(pl.program_id(0) == 0)
    def _():
        wout_ref[...] = wout32_ref[...].astype(BF16)

    og = ogf_ref[...] + ogb_ref[...]
    hi, mid, _ = _split3(og * og)
    ms = jnp.dot(hi, ind_ref[...], preferred_element_type=F32) + jnp.dot(mid, ind_ref[...], preferred_element_type=F32)
    og = og * lax.rsqrt(ms + EPS) * gw_ref[...] * _silu(gog_ref[...])
    od = odf_ref[...] + odb_ref[...]
    dog = dog_ref[...]
    parts = []
    for h in range(GDN_HEADS):
        z = od[:, h * GDN_DV:(h + 1) * GDN_DV]
        z = z * lax.rsqrt(jnp.mean(z * z, axis=-1, keepdims=True) + EPS) * dw_ref[...]
        parts.append(z * _silu(dog[:, h * GDN_DV:(h + 1) * GDN_DV]))
    y = su_ref[...] * sd_ref[...] + jnp.concatenate([ysf_ref[...], ysb_ref[...]], axis=1)
    g = 0.5 * y * (1.0 + jnp.tanh(math.sqrt(2.0 / math.pi) * (y + 0.044715 * (y * y * y))))
    os_ = g * jax.nn.sigmoid(_bdot(g, wglu_ref[...]) + bglu_ref[...])
    out = _bdot(og, wout_ref[0:GLA_W, :]) + _bdot(os_, wout_ref[GLA_W + GDN_W:MIX_W, :])
    for h in range(GDN_HEADS):
        lo = GLA_W + h * GDN_DV
        out = out + _bdot(parts[h], wout_ref[lo:lo + GDN_DV, :])
    o_ref[...] = x_ref[...] + mod_ref[5:6, :] * out


def _mix_out(x, mods, proj, og, od, ys, gla_nw, gdn_nw, s5_d, w_glu, b_glu, w_out, *, layer, n_seq):
    t, d = x.shape
    nb = mods.shape[0]
    tm = _pick_tile(n_seq if nb > 1 else t, MIX_TM)
    midx = _mod_index(nb, tm, n_seq)
    ys2 = ys

    def dirspec(w, dd):
        return pl.BlockSpec((None, tm, w), lambda i: (dd, i, 0))

    def vec(w):
        return pl.BlockSpec((None, 1, w), lambda i: (layer, 0, 0))

    return pl.pallas_call(
        _out_kernel,
        grid=(t // tm,),
        in_specs=[pl.BlockSpec((tm, d), lambda i: (i, 0)),
                  pl.BlockSpec((None, N_MOD, d), lambda i: (midx(i), 0, 0)),
                  pl.BlockSpec((tm, GLA_W), lambda i: (i, 0)), pl.BlockSpec((tm, GLA_W), lambda i: (i, 0)),
                  pl.BlockSpec((tm, GDN_W), lambda i: (i, 0)), pl.BlockSpec((tm, GDN_W), lambda i: (i, 0)),
                  dirspec(128, 0), dirspec(128, 1),
                  pl.BlockSpec((tm, GLA_W), lambda i: (i, P_GOG // GLA_W)),
                  pl.BlockSpec((tm, S5_W), lambda i: (i, P_SU // S5_W)),
                  pl.BlockSpec((tm, GDN_W), lambda i: (i, P_DOG // GDN_W)),
                  pl.BlockSpec((GLA_W, GLA_W), lambda i: (0, 0)),
                  vec(GLA_W), vec(GDN_DV), vec(S5_W),
                  pl.BlockSpec((None, S5_W, S5_W), lambda i: (layer, 0, 0)),
                  vec(S5_W),
                  pl.BlockSpec((None, MIX_W, d), lambda i: (layer, 0, 0), pipeline_mode=pl.Buffered(1))],
        out_specs=pl.BlockSpec((tm, d), lambda i: (i, 0)),
        out_shape=jax.ShapeDtypeStruct((t, d), F32),
        scratch_shapes=[pltpu.VMEM((MIX_W, d), BF16)],
        compiler_params=_cparams(("arbitrary",)),
        name="mix_out",
    )(x, mods, og[0], og[1], od[0], od[1], ys2, ys2, proj, proj, proj,
      jnp.asarray(_C["ind64"], BF16),
      jnp.tile(gla_nw, (1, GLA_HEADS)).reshape(DEPTH, 1, GLA_W), gdn_nw.reshape(DEPTH, 1, GDN_DV),
      s5_d.reshape(DEPTH, 1, S5_W), w_glu, b_glu.reshape(DEPTH, 1, S5_W), w_out)


def _gla_params(gk_up, gk_bias, layer):
    up = jnp.zeros((2, 128, GLA_KW), F32).at[:, :GLA_RANK, :].set(gk_up[layer])
    return up, gk_bias[layer].reshape(2, 1, GLA_KW)


def _gdn_params(a_log, dt_bias, layer):
    arow = jnp.zeros((2, 1, 128), F32).at[:, 0, SM_A:SM_A + GDN_HEADS].set(a_log[layer])
    dtb = jnp.zeros((2, 1, 128), F32).at[:, 0, SM_A:SM_A + GDN_HEADS].set(dt_bias[layer])
    return arow, dtb


def _gla_state_in(s):
    eye = jnp.eye(GLA_HEADS, dtype=F32)
    st = jnp.einsum('bzhde,hg->zbhegd', s.astype(F32), eye)
    return st.reshape(2, s.shape[0], GLA_W, GLA_KW)


def _trunk(x3, mods, rows, states, p, s5c, w_in):
    nseq, n_seq, d = x3.shape
    x = x3.reshape(nseq * n_seq, d)
    fin = []
    gla_f = jnp.zeros((nseq, DEPTH, 2, GLA_HEADS, GLA_DK, GLA_DV), F32)
    gdn_f = jnp.zeros((nseq, DEPTH, 2, GDN_HEADS, GDN_DK, GDN_DV), F32)
    for l in range(DEPTH):
        m = mods[l]
        x = _ffn(x, m, p['norm_w'], p['ffn_w_gate'], p['ffn_w_up'], p['ffn_w_down'], p['final_norm_w'],
                 layer=l, which=0, n_seq=n_seq, final=False)
        proj = _proj(x, m, p['norm_w'], w_in, p['gdn_conv_w'], layer=l, n_seq=n_seq, rowlen=n_seq // rows)
        if states is None:
            st0 = s0 = h0 = None
        else:
            st_gla, st_gdn, st_re, st_im = states
            st0 = _gla_state_in(st_gla[:, l])
            s0 = st_gdn[:, l].transpose(1, 0, 2, 3, 4)
            h0 = jnp.stack([st_re[:, l].reshape(nseq, 2, S5_N), st_im[:, l].reshape(nseq, 2, S5_N)], axis=0)
            h0 = h0.transpose(2, 0, 1, 3).reshape(2, 2, nseq * S5_PAIRS, 128)
        up_p, bias_p = _gla_params(p['gla_gk_up'], p['gla_gk_bias'], l)
        og_f, og_b, gla_f = _gla(proj, up_p, bias_p, st0, gla_f, layer=l, nseq=nseq, n_seq=n_seq)
        og = (og_f, og_b)
        arow, dtb = _gdn_params(p['gdn_a_log'], p['gdn_dt_bias'], l)
        od_f, od_b, gdn_f = _gdn(proj, arow, dtb, s0, gdn_f, layer=l, nseq=nseq, n_seq=n_seq)
        od = (od_f, od_b)
        ys, s5_f = _s5(proj, s5c, h0, layer=l, nseq=nseq, n_seq=n_seq)
        x = _mix_out(x, m, proj, og, od, ys, p['gla_norm_w'], p['gdn_norm_w'], p['s5_d'], p['s5_w_glu'],
                     p['s5_b_glu'], p['w_out'], layer=l, n_seq=n_seq)
        x = _ffn(x, m, p['norm_w'], p['ffn_w_gate'], p['ffn_w_up'], p['ffn_w_down'], p['final_norm_w'],
                 layer=l, which=1, n_seq=n_seq, final=(l == DEPTH - 1))
        fin.append((gla_f, gdn_f, s5_f))
    return x.reshape(nseq, n_seq, d), fin


def kernel(x_prompt, x_sample, c, state_gla, state_gdn, state_s5_re, state_s5_im, c_ctx, w_ada, b_ada, norm_w, ffn_w_gate, ffn_w_up, ffn_w_down, w_in, gla_gk_up, gla_gk_bias, gla_norm_w, gdn_conv_w, gdn_a_log, gdn_dt_bias, gdn_norm_w, s5_lam_re, s5_lam_im, s5_log_step, s5_b_re, s5_b_im, s5_c_re, s5_c_im, s5_d, s5_w_glu, s5_b_glu, w_out, final_norm_w):
    p = dict(norm_w=norm_w, ffn_w_gate=ffn_w_gate, ffn_w_up=ffn_w_up, ffn_w_down=ffn_w_down,
             gla_gk_up=gla_gk_up, gla_gk_bias=gla_gk_bias, gla_norm_w=gla_norm_w, gdn_conv_w=gdn_conv_w,
             gdn_a_log=gdn_a_log, gdn_dt_bias=gdn_dt_bias, gdn_norm_w=gdn_norm_w, s5_c_re=s5_c_re,
             s5_c_im=s5_c_im, s5_d=s5_d, s5_w_glu=s5_w_glu, s5_b_glu=s5_b_glu, w_out=w_out,
             final_norm_w=final_norm_w)
    b_ctx = x_prompt.shape[0]
    b_dec = x_sample.shape[0]
    cond8 = jnp.zeros((8, D_MODEL), F32).at[0].set(c_ctx).at[1:1 + b_dec].set(c)
    mods = _ada(cond8, w_ada, b_ada)
    mods_ctx = mods[:, 0:1].reshape(DEPTH, 1, N_MOD, D_MODEL)
    mods_dec = mods[:, 1:1 + b_dec].reshape(DEPTH, b_dec, N_MOD, D_MODEL)
    s5c = _s5_consts(s5_lam_re, s5_lam_im, s5_log_step, s5_b_re, s5_b_im, s5_c_re, s5_c_im)

    y_prompt, fin = _trunk(x_prompt, mods_ctx, 1, None, p, s5c, w_in)
    new_gla, new_gdn = fin[-1][0], fin[-1][1]
    s5f = jnp.stack([f[2].reshape(2, 2, b_ctx, S5_GROUPS, S5_P).transpose(1, 2, 0, 3, 4) for f in fin], axis=2)
    new_re, new_im = s5f[0], s5f[1]

    rows = x_sample.shape[1] // 64
    y_sample, _ = _trunk(x_sample, mods_dec, rows, (state_gla, state_gdn, state_s5_re, state_s5_im),
                         p, s5c, w_in)
    return (y_prompt, y_sample, new_gla.astype(state_gla.dtype), new_gdn.astype(state_gdn.dtype),
            new_re.astype(state_s5_re.dtype), new_im.astype(state_s5_im.dtype))
```

```python
import functools
import math

import numpy as np
import jax
import jax.numpy as jnp
from jax import lax
from jax.experimental import pallas as pl
from jax.experimental.pallas import tpu as pltpu

F32 = jnp.float32
BF16 = jnp.bfloat16

D_MODEL = 1024
DEPTH = 2
FFN_DIM = 2816
N_MOD = 9
EPS = 1e-6
CHUNK = 64
GLA_HEADS, GLA_DK, GLA_DV, GLA_RANK = 4, 32, 64, 16
GLA_GATE_NORM = 16.0
GDN_HEADS, GDN_DK, GDN_DV, GDN_CONV = 4, 128, 128, 5
S5_GROUPS, S5_CH, S5_P = 16, 16, 64
GLA_W = GLA_HEADS * GLA_DV
GLA_KW = GLA_HEADS * GLA_DK
GDN_W = GDN_HEADS * GDN_DV
S5_W = S5_GROUPS * S5_CH
S5_N = S5_GROUPS * S5_P
S5_L = 16
S5_PAIRS = S5_GROUPS // 2
MIX_W = GLA_W + GDN_W + S5_W

_IN_SIZES = (GLA_KW, GLA_KW, GLA_W, GLA_RANK, GLA_RANK, GLA_W,
             GDN_W, GDN_W, GDN_W, GDN_HEADS, GDN_HEADS, GDN_HEADS, GDN_HEADS, GDN_W, S5_W)
_IN_OFF = tuple(int(v) for v in np.cumsum((0,) + _IN_SIZES))
(_GQ, _GK, _GV, _GLRF, _GLRB, _GOG, _DQ, _DK, _DV, _DAF, _DAB, _DBF, _DBB, _DOG, _SU) = range(15)

P_DQKV = 0
P_SMALL = 1536
P_GQK = 1792
P_GV = 2048
P_GOG = 2304
P_DOG = 2560
P_SU = 3072
PROJ_W = 3328
PROJ_TN = PROJ_W // 2
SM_A = GLA_RANK
SM_B = GLA_RANK + GDN_HEADS

SUB = 256
GLA_SUB = 256
FFN_TM = 2048
FFN_TF = 256
PROJ_TM = 1024
MIX_TM = 512
ADA_TN = 1024
V7X_VMEM_BYTES = 64 * 1024 * 1024
VMEM_LIMIT = V7X_VMEM_BYTES - 8 * 1024 * 1024


def _cparams(sem):
    return pltpu.CompilerParams(dimension_semantics=sem, vmem_limit_bytes=VMEM_LIMIT)


def _bdot(a, b):
    return jnp.dot(a.astype(BF16), b.astype(BF16), preferred_element_type=F32)


def _bdot_nt(a, b):
    return lax.dot_general(a.astype(BF16), b.astype(BF16), (((1,), (1,)), ((), ())),
                           preferred_element_type=F32)


def _split3(x):
    hi = x.astype(BF16)
    r1 = x - hi.astype(F32)
    mid = r1.astype(BF16)
    lo = (r1 - mid.astype(F32)).astype(BF16)
    return hi, mid, lo


def _sel_dot(m01, x):
    hi, mid, lo = _split3(x)
    return (jnp.dot(m01, hi, preferred_element_type=F32) + jnp.dot(m01, mid, preferred_element_type=F32)
            + jnp.dot(m01, lo, preferred_element_type=F32))


def _silu(x):
    return x * jax.nn.sigmoid(x)


def _softplus(x):
    return jnp.maximum(x, 0.0) + jnp.log1p(jnp.exp(-jnp.abs(x)))


def _log_sigmoid(x):
    return -_softplus(-x)


def _norm_mod(x, nw, scale, shift):
    y = x * lax.rsqrt(jnp.mean(x * x, axis=-1, keepdims=True) + EPS)
    return y * nw * (1.0 + scale) + shift


def _pick_tile(n, cap):
    t = cap
    while n % t:
        t //= 2
    return t


def _ada_kernel(c_ref, w_ref, b_ref, o_ref):
    o_ref[...] = _bdot(_silu(c_ref[...]), w_ref[...]) + b_ref[...]


def _ada(cond8, w_ada, b_ada):
    depth, d, nout = w_ada.shape
    tn = ADA_TN
    return pl.pallas_call(
        _ada_kernel,
        grid=(depth, nout // tn),
        in_specs=[pl.BlockSpec((8, d), lambda l, j: (0, 0)),
                  pl.BlockSpec((None, d, tn), lambda l, j: (l, 0, j)),
                  pl.BlockSpec((None, 1, tn), lambda l, j: (l, 0, j))],
        out_specs=pl.BlockSpec((None, 8, tn), lambda l, j: (l, 0, j)),
        out_shape=jax.ShapeDtypeStruct((depth, 8, nout), F32),
        compiler_params=_cparams(("parallel", "parallel")),
        name="ada",
    )(cond8, w_ada, b_ada.reshape(depth, 1, nout))


def _ffn_kernel(x_ref, mod_ref, nw_ref, wg_ref, wu_ref, wd_ref, fw_ref, o_ref, h_ref, *, mod_base, final):
    j = pl.program_id(1)

    @pl.when(j == 0)
    def _():
        m = mod_ref[...]
        h = _norm_mod(x_ref[...], nw_ref[...], m[mod_base + 1:mod_base + 2], m[mod_base:mod_base + 1])
        h_ref[...] = h.astype(BF16)
        o_ref[...] = jnp.zeros_like(o_ref)

    h = h_ref[...]
    g = jnp.dot(h, wg_ref[...].astype(BF16), preferred_element_type=F32)
    u = jnp.dot(h, wu_ref[...].astype(BF16), preferred_element_type=F32)
    o_ref[...] += _bdot(_silu(g) * u, wd_ref[...])

    @pl.when(j == pl.num_programs(1) - 1)
    def _():
        m = mod_ref[...]
        y = x_ref[...] + 0.5 * m[mod_base + 2:mod_base + 3] * o_ref[...]
        if final:
            y = y * lax.rsqrt(jnp.mean(y * y, axis=-1, keepdims=True) + EPS) * fw_ref[...]
        o_ref[...] = y


def _mod_index(nb, tm, n_seq):
    if nb == 1:
        return lambda i: 0
    return lambda i: (i * tm) // n_seq


def _ffn(x, mods, norm_w, w_gate, w_up, w_down, final_w, *, layer, which, n_seq, final):
    t, d = x.shape
    f = w_gate.shape[-1]
    nb = mods.shape[0]
    tm = _pick_tile(n_seq if nb > 1 else t, FFN_TM)
    tf = FFN_TF
    midx = _mod_index(nb, tm, n_seq)
    mod_base = 0 if which == 0 else 6
    return pl.pallas_call(
        functools.partial(_ffn_kernel, mod_base=mod_base, final=final),
        grid=(t // tm, f // tf),
        in_specs=[pl.BlockSpec((tm, d), lambda i, j: (i, 0)),
                  pl.BlockSpec((None, N_MOD, d), lambda i, j: (midx(i), 0, 0)),
                  pl.BlockSpec((None, None, 1, d), lambda i, j: (layer, 2 * which, 0, 0)),
                  pl.BlockSpec((None, None, d, tf), lambda i, j: (layer, which, 0, j)),
                  pl.BlockSpec((None, None, d, tf), lambda i, j: (layer, which, 0, j)),
                  pl.BlockSpec((None, None, tf, d), lambda i, j: (layer, which, j, 0)),
                  pl.BlockSpec((1, d), lambda i, j: (0, 0))],
        out_specs=pl.BlockSpec((tm, d), lambda i, j: (i, 0)),
        out_shape=jax.ShapeDtypeStruct((t, d), F32),
        scratch_shapes=[pltpu.VMEM((tm, d), BF16)],
        compiler_params=_cparams(("parallel", "arbitrary")),
        name="ffn",
    )(x, mods, norm_w.reshape(DEPTH, 3, 1, d), w_gate, w_up, w_down, final_w.reshape(1, d))


def _proj_kernel(x_ref, mod_ref, nw_ref, w_ref, cw_ref, o_ref, h_ref, wb_ref, st_ref, *, rowlen):
    j = pl.program_id(0)
    i = pl.program_id(1)

    for jt, segs in enumerate(_W_SEGS):
        @pl.when(jnp.logical_and(i == 0, j == jt))
        def _():
            for lo, src, width in segs:
                if width % 128 == 0:
                    wb_ref[lo:lo + width, :] = w_ref[src:src + width, :].astype(BF16)
                else:
                    base = lo // 128 * 128
                    if src is None:
                        st_ref[lo - base:lo - base + width, :] = jnp.zeros((width, st_ref.shape[1]), F32)
                        wb_ref[base:base + 128, :] = st_ref[...].astype(BF16)
                    else:
                        st_ref[lo - base:lo - base + width, :] = w_ref[src:src + width, :]

    @pl.when(j == 0)
    def _():
        m = mod_ref[...]
        h_ref[i] = _norm_mod(x_ref[...], nw_ref[...], m[4:5], m[3:4]).astype(BF16)
        o_ref[...] = _nt(h_ref[i], wb_ref[...])
        tm = o_ref.shape[0]
        pos = lax.broadcasted_iota(jnp.int32, (tm, 1), 0) % rowlen
        valid = [jnp.logical_and(pos + (t - GDN_CONV // 2) >= 0, pos + (t - GDN_CONV // 2) < rowlen)
                 for t in range(GDN_CONV)]
        for g in range(3 * GDN_HEADS):
            cols = slice(P_DQKV + g * GDN_DK, P_DQKV + (g + 1) * GDN_DK)
            xg = o_ref[:, cols]
            acc = xg * cw_ref[GDN_CONV // 2:GDN_CONV // 2 + 1, cols]
            for t in range(GDN_CONV):
                s = t - GDN_CONV // 2
                if s != 0:
                    xs = pltpu.roll(xg, (-s) % tm, axis=0)
                    acc = acc + jnp.where(valid[t], xs, 0.0) * cw_ref[t:t + 1, cols]
            y = _silu(acc)
            if g < 2 * GDN_HEADS:
                y = y * lax.rsqrt(jnp.sum(y * y, axis=-1, keepdims=True) + EPS)
                if g < GDN_HEADS:
                    y = y * (GDN_DK ** -0.5)
            o_ref[:, cols] = y

    @pl.when(j == 1)
    def _():
        o_ref[...] = _nt(h_ref[i], wb_ref[...])


def _proj(x, mods, norm_w, w_in, conv_w, *, layer, n_seq, rowlen):
    t, d = x.shape
    nb = mods.shape[0]
    tm = _pick_tile(n_seq if nb > 1 else t, PROJ_TM)
    midx = _mod_index(nb, tm, n_seq)
    return pl.pallas_call(
        functools.partial(_proj_kernel, rowlen=rowlen),
        grid=(PROJ_W // PROJ_TN, t // tm),
        in_specs=[pl.BlockSpec((tm, d), lambda j, i: (i * (1 - j), 0)),
                  pl.BlockSpec((None, N_MOD, d), lambda j, i: (midx(i), 0, 0)),
                  pl.BlockSpec((None, None, 1, d), lambda j, i: (layer, 1, 0, 0)),
                  pl.BlockSpec((None, w_in.shape[-1], d), lambda j, i: (layer, 0, 0), pipeline_mode=pl.Buffered(1)),
                  pl.BlockSpec((None, GDN_CONV, 3 * GDN_W), lambda j, i: (layer, 0, 0))],
        out_specs=pl.BlockSpec((tm, PROJ_TN), lambda j, i: (i, j)),
        out_shape=jax.ShapeDtypeStruct((t, PROJ_W), F32),
        scratch_shapes=[pltpu.VMEM((t // tm, tm, d), BF16), pltpu.VMEM((PROJ_TN, d), BF16),
                        pltpu.VMEM((128, d), F32)],
        compiler_params=_cparams(("arbitrary", "arbitrary")),
        name="proj",
    )(x, mods, norm_w.reshape(DEPTH, 3, 1, d), jnp.swapaxes(w_in, 1, 2), conv_w)


def _w_in_segments():
    order = [_DQ, _DK, _DV, _GLRF, _DAF, _DBF, None, _GLRB, _DAB, _DBB, None,
             _GQ, _GK, _GV, _GOG, _DOG, _SU]
    sm_pad = 128 - GLA_RANK - 2 * GDN_HEADS
    tiles = [[] for _ in range(PROJ_W // PROJ_TN)]
    dst = 0
    for s in order:
        width = sm_pad if s is None else _IN_SIZES[s]
        src = None if s is None else _IN_OFF[s]
        tile = tiles[dst // PROJ_TN]
        lo = dst % PROJ_TN
        if tile and src is not None and tile[-1][1] is not None and (
                tile[-1][0] + tile[-1][2] == lo and tile[-1][1] + tile[-1][2] == src):
            tile[-1] = (tile[-1][0], tile[-1][1], tile[-1][2] + width)
        else:
            tile.append((lo, src, width))
        dst += width
        assert (dst - 1) // PROJ_TN == (dst - width) // PROJ_TN, "a segment may not straddle column tiles"
    assert dst == PROJ_W
    return tiles


_W_SEGS = _w_in_segments()


def _np_consts():
    t = np.arange(CHUNK)
    lower = (t[:, None] >= t[None, :])
    incl = np.stack([lower, lower.T])
    strict = np.stack([t[:, None] > t[None, :], t[:, None] < t[None, :]])
    eye4 = np.eye(4, dtype=bool)
    blk = np.kron(eye4, np.ones((CHUNK, CHUNK), bool))
    incl_bd = np.stack([np.kron(eye4, incl[d]) for d in range(2)])
    strict_bd = np.stack([np.kron(eye4, strict[d]) for d in range(2)])
    levels = []
    for d in range(2):
        per = []
        for lv in range(6):
            s = 1 << lv
            same = (t[:, None] // (2 * s)) == (t[None, :] // (2 * s))
            hi = (t % (2 * s)) >= s
            m = same & hi[:, None] & (~hi)[None, :]
            if d == 1:
                m = m.T
            per.append(np.kron(eye4, m))
        levels.append(np.stack(per))
    levels = np.stack(levels)
    assert (levels.sum(1) == strict_bd).all() and blk.any()
    hm_k = np.kron(eye4, np.ones((1, GLA_DK)))
    hm_v = np.kron(eye4, np.ones((1, GLA_DV)))
    st_mask = np.kron(eye4, np.ones((GLA_DV, GLA_DK)))
    ind64 = np.kron(eye4, np.full((GLA_DV, GLA_DV), 1.0 / GLA_DV))
    gla_sel, gla_later, gla_earlier, gla_pair = [], [], [], []
    for d in range(2):
        pos = t if d == 0 else CHUNK - 1 - t
        cum = (pos[None, :] <= pos[:, None]).astype(np.float64)
        sel, later, earlier, pair = [cum], [], [], []
        for lv in range(6):
            s = 1 << lv
            ref_pos = (pos // (2 * s)) * (2 * s) + s - 1
            pick = (pos[None, :] == ref_pos[:, None]).astype(np.float64)
            sel.append(pick @ cum)
            is_later = (pos % (2 * s)) >= s
            same = (pos[:, None] // (2 * s)) == (pos[None, :] // (2 * s))
            later.append(np.repeat(is_later[:, None], GLA_KW, axis=1))
            earlier.append(np.repeat(~is_later[:, None], GLA_KW, axis=1))
            pair.append(np.tile(same & is_later[:, None] & (~is_later)[None, :], (1, 4)))
        gla_sel.append(np.concatenate(sel, axis=0))
        gla_later.append(np.stack(later))
        gla_earlier.append(np.stack(earlier))
        gla_pair.append(np.stack(pair))
        assert (np.stack(pair).sum(0) == np.tile(pos[:, None] > pos[None, :], (1, 4))).all()
    head_sum = np.kron(eye4, np.ones((GLA_DK, GLA_DV)))
    return dict(incl=incl, incl_bd=incl_bd, strict_bd=strict_bd, levels=levels,
                hm_k=hm_k, hm_v=hm_v, st_mask=st_mask, ind64=ind64, eye256=np.eye(256),
                gla_sel=np.stack(gla_sel), gla_later=np.stack(gla_later), gla_earlier=np.stack(gla_earlier),
                gla_pair=np.stack(gla_pair), head_sum=head_sum)


_C = _np_consts()


def _row_block_index(nblk):
    return lambda d, b, i: b * nblk + i + d * (nblk - 1 - 2 * i)


def _gla_kernel(qkf_ref, qkb_ref, vf_ref, vb_ref, smf_ref, smb_ref, up_ref, bias_ref, sel_ref, later_ref,
                earlier_ref, pair_ref, hmk_ref, hmv_ref, stm_ref, hsum_ref, *rest, nchunk, zero_init):
    rest = list(rest)
    st0_ref = None if zero_init else rest.pop(0)
    rest.pop(0)
    of_ref, ob_ref, stf_ref, st_ref, qd_sc, m_sc, dec_sc = rest
    ins =((qkf_ref, vf_ref, smf_ref, of_ref), (qkb_ref, vb_ref, smb_ref, ob_ref))

    @pl.when(pl.program_id(1) == 0)
    def _():
        st_ref[...] = jnp.zeros_like(st_ref) if zero_init else st0_ref[...]

    hmkb = hmk_ref[...].astype(BF16)
    hmv = hmv_ref[...]
    stm = stm_ref[...]

    units = [(dd, c) for c in range(nchunk) for dd in range(2)]
    st1 = []
    for dd, c in units:
        qk_ref, v_ref, sm_ref, _ = ins[dd]
        rows = pl.ds(c * CHUNK, CHUNK)
        qk = qk_ref[rows, :]
        q = qk[:, 0:GLA_KW] * (GLA_DK ** -0.5)
        k = qk[:, GLA_KW:2 * GLA_KW]
        x = _bdot(sm_ref[rows, :], up_ref[dd]) + bias_ref[dd]
        lg = _log_sigmoid(x) * (1.0 / GLA_GATE_NORM)
        ball = _sel_dot(sel_ref[dd], lg)
        st1.append((q, k, v_ref[rows, :], ball, jnp.sum(lg, axis=0, keepdims=True)))
    prods = []
    for lv in range(6):
        for (dd, c), (q, k, _, ball, _) in zip(units, st1):
            b = ball[0:CHUNK]
            ref = ball[(lv + 1) * CHUNK:(lv + 2) * CHUNK]
            ql = q * jnp.where(later_ref[dd, lv] > 0.5, jnp.exp(jnp.minimum(b - ref, 0.0)), 0.0)
            kl = (k * jnp.where(earlier_ref[dd, lv] > 0.5, jnp.exp(jnp.minimum(ref - b, 0.0)), 0.0)).astype(BF16)
            kexp = jnp.concatenate([kl * hmkb[h:h + 1] for h in range(GLA_HEADS)], axis=0)
            prods.append(_bdot_nt(ql, kexp))
    for i, ((dd, c), (q, k, v, ball, btot)) in enumerate(zip(units, st1)):
        o_ref = ins[dd][3]
        n = dd * nchunk + c
        b = ball[0:CHUNK]
        att = jnp.zeros((CHUNK, GLA_HEADS * CHUNK), F32)
        for lv in range(6):
            att = jnp.where(pair_ref[dd, lv] > 0.5, prods[lv * len(units) + i], att)
        vexp = jnp.concatenate([v * hmv[h:h + 1] for h in range(GLA_HEADS)], axis=0)
        o_ref[pl.ds(c * CHUNK, CHUNK), :] = _bdot(att, vexp) + _bdot(q * k, hsum_ref[...]) * v
        qd_sc[n] = (q * jnp.exp(b)).astype(BF16)
        kd = k * jnp.exp(btot - b)
        m_sc[n] = _bdot(v.T, kd) * stm
        dec_sc[n] = jnp.broadcast_to(jnp.exp(btot), (8, GLA_KW))

    for dd in range(2):
        o_ref = ins[dd][3]
        st = st_ref[dd]
        for ci in range(nchunk):
            c = ci if dd == 0 else nchunk - 1 - ci
            n = dd * nchunk + c
            rows = pl.ds(c * CHUNK, CHUNK)
            o_ref[rows, :] += _nt(qd_sc[n], st.astype(BF16))
            st = st * dec_sc[n][0:1] + m_sc[n]
        st_ref[dd] = st
        for h in range(GLA_HEADS):
            stf_ref[dd, h] = st[h * GLA_DV:(h + 1) * GLA_DV, h * GLA_DK:(h + 1) * GLA_DK].T


def _gla(proj, up_p, bias_p, st0, fin, *, layer, nseq, n_seq):
    t = proj.shape[0]
    sub = min(n_seq, GLA_SUB)
    nblk = n_seq // sub
    nck = sub // CHUNK
    rbi = _row_block_index(nblk)
    c = _C
    consts = [jnp.asarray(c["gla_sel"], BF16), jnp.asarray(c["gla_later"], F32), jnp.asarray(c["gla_earlier"], F32),
              jnp.asarray(c["gla_pair"], F32), jnp.asarray(c["hm_k"], F32), jnp.asarray(c["hm_v"], F32),
              jnp.asarray(c["st_mask"], F32), jnp.asarray(c["head_sum"], BF16)]
    st_spec = pl.BlockSpec((2, None, GLA_W, GLA_KW), lambda b, i: (0, b, 0, 0))
    full = lambda a: pl.BlockSpec(a.shape, lambda b, i: (0,) * a.ndim)
    zero_init = st0 is None
    nu = 2 * nck
    rows = (lambda b, i: rbi(0, b, i), lambda b, i: rbi(1, b, i))

    def blk(width, col, dd):
        return pl.BlockSpec((sub, width), lambda b, i: (rows[dd](b, i), col // width))

    hs = (GLA_HEADS, GLA_DK, GLA_DV)
    args = [proj, proj, proj, proj, proj, proj, up_p, bias_p, *consts] + ([] if zero_init else [st0])
    in_specs = ([blk(2 * GLA_KW, P_GQK, 0), blk(2 * GLA_KW, P_GQK, 1), blk(GLA_W, P_GV, 0), blk(GLA_W, P_GV, 1),
                 blk(128, P_SMALL, 0), blk(128, P_SMALL + 128, 1), full(up_p), full(bias_p)]
                + [full(a) for a in consts] + ([] if zero_init else [st_spec]))
    return pl.pallas_call(
        functools.partial(_gla_kernel, nchunk=nck, zero_init=zero_init),
        grid=(nseq, nblk),
        in_specs=in_specs + [pl.BlockSpec(memory_space=pl.ANY)],
        out_specs=[pl.BlockSpec((sub, GLA_W), lambda b, i: (rows[0](b, i), 0)),
                   pl.BlockSpec((sub, GLA_W), lambda b, i: (rows[1](b, i), 0)), _fin_spec(hs, layer)],
        out_shape=[jax.ShapeDtypeStruct((t, GLA_W), F32), jax.ShapeDtypeStruct((t, GLA_W), F32),
                   jax.ShapeDtypeStruct((nseq, DEPTH, 2) + hs, F32)],
        scratch_shapes=[pltpu.VMEM((2, GLA_W, GLA_KW), F32), pltpu.VMEM((nu, CHUNK, GLA_KW), BF16),
                        pltpu.VMEM((nu, GLA_W, GLA_KW), F32), pltpu.VMEM((nu, 8, GLA_KW), F32)],
        input_output_aliases={len(args): 2},
        compiler_params=_cparams(("arbitrary", "arbitrary")),
        name="gla",
    )(*args, fin)


def _fin_spec(hs, layer):
    return pl.BlockSpec((None, None, 2) + hs, lambda b, i: (b, layer, 0) + (0,) * len(hs))


def _gdn_kernel(qkvf_ref, qkvb_ref, smf_ref, smb_ref, arow_ref, dtb_ref, tri_ref, incl_ref, strict_ref, lvl_ref,
                eye_ref, *rest, nchunk, zero_init):
    rest = list(rest)
    s0_ref = None if zero_init else rest.pop(0)
    rest.pop(0)
    of_ref, ob_ref, sf_ref, s_ref, la_ref, be_ref, w_sc, u0_sc, qkd_sc, qe_sc, kdt_sc, egl_sc = rest
    qkv_refs = (qkvf_ref, qkvb_ref)
    o_refs = (of_ref, ob_ref)

    @pl.when(pl.program_id(1) == 0)
    def _():
        s_ref[...] = jnp.zeros_like(s_ref) if zero_init else s0_ref[...]

    for dd, sm_ref in enumerate((smf_ref, smb_ref)):
        sm = sm_ref[...]
        la_ref[dd] = -jnp.exp(arow_ref[dd]) * _softplus(sm + dtb_ref[dd])
        be_ref[dd] = jax.nn.sigmoid(sm)

    eye = eye_ref[...]

    def stack(ref, r0, lo, width):
        return jnp.concatenate([ref[pl.ds(r0, CHUNK), lo + h * width:lo + (h + 1) * width]
                                for h in range(GDN_HEADS)], axis=0)

    def colstack(a, lo):
        return jnp.concatenate([a[:, lo + h:lo + h + 1] for h in range(GDN_HEADS)], axis=0)

    units = [(dd, c) for c in range(nchunk) for dd in range(2)]
    st1 = []
    for dd, c in units:
        r0 = c * CHUNK
        qkv_ref = qkv_refs[dd]
        qs = stack(qkv_ref, r0, 0, GDN_DK)
        ks = stack(qkv_ref, r0, GDN_W, GDN_DK)
        la = la_ref[dd, pl.ds(r0, CHUNK), :]
        g = _sel_dot(tri_ref[dd], la)
        ksb = ks.astype(BF16)
        st1.append((qs, ks, la, g, _bdot_nt(ksb, ksb), _bdot_nt(qs, ksb)))
    loc = []
    for (dd, c), (qs, ks, la, g, kk, qk) in zip(units, st1):
        r0 = c * CHUNK
        n = dd * nchunk + c
        vs = stack(qkv_refs[dd], r0, 2 * GDN_W, GDN_DV)
        gtot = jnp.sum(la, axis=0, keepdims=True)
        gcol = colstack(g, SM_A)
        kfcol = colstack(jnp.exp(gtot - g), SM_A)
        egcol = colstack(jnp.exp(g), SM_A)
        bcol = colstack(be_ref[dd, pl.ds(r0, CHUNK), :], SM_B)
        grow = jnp.sum(eye * gcol, axis=0, keepdims=True)
        lm_rows, qkd_rows = [], []
        for h in range(GDN_HEADS):
            rs = slice(h * CHUNK, (h + 1) * CHUNK)
            lt = (h * CHUNK) // 128
            ls = slice(lt * 128, (lt + 1) * 128)
            dec = incl_ref[dd, rs, ls] * jnp.exp(jnp.minimum(gcol[rs] - grow[:, ls], 0.0))
            lm = strict_ref[dd, rs, ls] * dec * kk[rs, ls] * bcol[rs]
            qd = qk[rs, ls] * dec
            zero = jnp.zeros_like(lm)
            lm_rows.append(jnp.concatenate([lm, zero] if lt == 0 else [zero, lm], axis=1))
            qkd_rows.append(jnp.concatenate([qd, zero] if lt == 0 else [zero, qd], axis=1))
        lmat = jnp.concatenate(lm_rows, axis=0)
        rhs = jnp.concatenate([ks * (bcol * egcol), vs * bcol], axis=1).astype(BF16)
        qkd_sc[n] = jnp.concatenate(qkd_rows, axis=0).astype(BF16)
        qe_sc[n] = (qs * egcol).astype(BF16)
        kd = ks * kfcol
        for h in range(GDN_HEADS):
            kdt_sc[n * GDN_HEADS + h] = kd[h * CHUNK:(h + 1) * CHUNK].T.astype(BF16)
        egl_sc[n] = jnp.broadcast_to(jnp.exp(gtot), (8, 128))
        loc.append((lmat.astype(BF16), rhs))
    def inv_start(wave):
        return [eye - (loc[i][0] * lvl_ref[units[i][0], 0]).astype(F32) for i in wave]

    def inv_level(wave, tinv, lv):
        tb = [t_.astype(BF16) for t_ in tinv]
        cs = [jnp.dot(loc[i][0] * lvl_ref[units[i][0], lv], t_, preferred_element_type=F32)
              for i, t_ in zip(wave, tb)]
        return [t32 - jnp.dot(t_, c_.astype(BF16), preferred_element_type=F32)
                for t32, t_, c_ in zip(tinv, tb, cs)]

    def solve(wave, tinv):
        for i, t_ in zip(wave, tinv):
            dd, c = units[i]
            sol = jnp.dot(t_.astype(BF16), loc[i][1], preferred_element_type=F32)
            w_sc[dd * nchunk + c] = sol[:, :GDN_DK].astype(BF16)
            u0_sc[dd * nchunk + c] = sol[:, GDN_DK:]

    s_cur = [[s_ref[dd, h] for h in range(GDN_HEADS)] for dd in range(2)]

    def recur(ci):
        step = []
        for dd in range(2):
            c = ci if dd == 0 else nchunk - 1 - ci
            n = dd * nchunk + c
            sb = [s_.astype(BF16) for s_ in s_cur[dd]]
            ub = []
            for h in range(GDN_HEADS):
                rows = pl.ds(h * CHUNK, CHUNK)
                u = u0_sc[n, rows, :] - jnp.dot(w_sc[n, rows, :], sb[h], preferred_element_type=F32)
                ub.append(u.astype(BF16))
            step.append((c, n, sb, ub))
        for dd, (c, n, sb, ub) in enumerate(step):
            egl = egl_sc[n]
            for h in range(GDN_HEADS):
                s_cur[dd][h] = (egl[0:1, SM_A + h:SM_A + h + 1] * s_cur[dd][h]
                                + jnp.dot(kdt_sc[n * GDN_HEADS + h], ub[h], preferred_element_type=F32))
        for dd, (c, n, sb, ub) in enumerate(step):
            o2 = jnp.dot(qkd_sc[n], jnp.concatenate(ub, axis=0), preferred_element_type=F32)
            for h in range(GDN_HEADS):
                rows = pl.ds(h * CHUNK, CHUNK)
                o_refs[dd][pl.ds(c * CHUNK, CHUNK), h * GDN_DV:(h + 1) * GDN_DV] = (
                    jnp.dot(qe_sc[n, rows, :], sb[h], preferred_element_type=F32) + o2[h * CHUNK:(h + 1) * CHUNK])

    every = list(range(len(units)))
    tinv = inv_start(every)
    for lv in range(1, 6):
        tinv = inv_level(every, tinv, lv)
    solve(every, tinv)
    for ci in range(nchunk):
        recur(ci)
    for dd in range(2):
        for h in range(GDN_HEADS):
            s_ref[dd, h] = s_cur[dd][h]
    sf_ref[...] = s_ref[...]


def _gdn(proj, arow, dtb, s0, fin, *, layer, nseq, n_seq):
    t = proj.shape[0]
    nblk = n_seq // SUB
    rbi = _row_block_index(nblk)
    c = _C
    consts = [jnp.asarray(c["incl"], BF16), jnp.asarray(c["incl_bd"], F32), jnp.asarray(c["strict_bd"], F32),
              jnp.asarray(c["levels"], BF16), jnp.asarray(c["eye256"], F32)]
    hs = (GDN_HEADS, GDN_DK, GDN_DV)
    nck = SUB // CHUNK
    nu = 2 * nck
    zero_init = s0 is None
    s_spec = pl.BlockSpec((2, None) + hs, lambda b, i: (0, b, 0, 0, 0))
    full = lambda shape: pl.BlockSpec(shape, lambda b, i: (0,) * len(shape))
    rows = (lambda b, i: rbi(0, b, i), lambda b, i: rbi(1, b, i))
    args = [proj, proj, proj, proj, arow, dtb, *consts] + ([] if zero_init else [s0])
    return pl.pallas_call(
        functools.partial(_gdn_kernel, nchunk=nck, zero_init=zero_init),
        grid=(nseq, nblk),
        in_specs=[pl.BlockSpec((SUB, 3 * GDN_W), lambda b, i: (rows[0](b, i), P_DQKV // (3 * GDN_W))),
                  pl.BlockSpec((SUB, 3 * GDN_W), lambda b, i: (rows[1](b, i), P_DQKV // (3 * GDN_W))),
                  pl.BlockSpec((SUB, 128), lambda b, i: (rows[0](b, i), P_SMALL // 128)),
                  pl.BlockSpec((SUB, 128), lambda b, i: (rows[1](b, i), P_SMALL // 128 + 1)),
                  full((2, 1, 128)), full((2, 1, 128)), full((2, CHUNK, CHUNK)),
                  full((2, 256, 256)), full((2, 256, 256)), full((2, 6, 256, 256)), full((256, 256))]
                 + ([] if zero_init else [s_spec]) + [pl.BlockSpec(memory_space=pl.ANY)],
        out_specs=[pl.BlockSpec((SUB, GDN_W), lambda b, i: (rows[0](b, i), 0)),
                   pl.BlockSpec((SUB, GDN_W), lambda b, i: (rows[1](b, i), 0)), _fin_spec(hs, layer)],
        out_shape=[jax.ShapeDtypeStruct((t, GDN_W), F32), jax.ShapeDtypeStruct((t, GDN_W), F32),
                   jax.ShapeDtypeStruct((nseq, DEPTH, 2) + hs, F32)],
        input_output_aliases={len(args): 2},
        scratch_shapes=[pltpu.VMEM((2,) + hs, F32),
                        pltpu.VMEM((2, SUB, 128), F32), pltpu.VMEM((2, SUB, 128), F32),
                        pltpu.VMEM((nu, 256, GDN_DK), BF16), pltpu.VMEM((nu, 256, GDN_DV), F32),
                        pltpu.VMEM((nu, 256, 256), BF16), pltpu.VMEM((nu, 256, GDN_DK), BF16),
                        pltpu.VMEM((nu * GDN_HEADS, GDN_DK, CHUNK), BF16), pltpu.VMEM((nu, 8, 128), F32)],
        compiler_params=_cparams(("arbitrary", "arbitrary")),
        name="gdn",
    )(*args, fin)


def _cmul(ar, ai, br, bi):
    return ar * br - ai * bi, ar * bi + ai * br


def _split2(x):
    hi = x.astype(BF16)
    return hi, (x - hi.astype(F32)).astype(BF16)


def _s5_consts_kernel(lr_ref, li_ref, st_ref, br_ref, bi_ref, cr_ref, ci_ref, gm_ref, bm_ref,
                      laml_ref, wexp_ref, vexp_ref, tt_ref, w_ref, v_ref, a_ref, q_ref):
    d = pl.program_id(0) % 2
    lc = S5_L * S5_CH

    def trows(s):
        return pl.ds(pl.multiple_of((s + d * (S5_L - 1 - 2 * s)) * S5_CH, S5_CH), S5_CH)

    lr = lr_ref[...]
    li = li_ref[...]
    step = jnp.exp(st_ref[...])
    mag = jnp.exp(lr * step)
    lbr = mag * jnp.cos(li * step)
    lbi = mag * jnp.sin(li * step)
    nr = lbr - 1.0
    den = lr * lr + li * li
    fr = (nr * lr + lbi * li) / den
    fi = (lbi * lr - nr * li) / den
    bbr, bbi = _cmul(fr, fi, br_ref[...], bi_ref[...])
    cr = cr_ref[...]
    ci = ci_ref[...]
    pr = jnp.ones_like(lr)
    pi = jnp.zeros_like(lr)
    for s in range(S5_L):
        ar, ai = _cmul(cr, ci, pr, pi)
        a_ref[0, trows(s), :] = ar
        a_ref[1, trows(s), :] = ai
        wr, wi = _cmul(pr, pi, bbr, bbi)
        w_ref[0, trows(S5_L - 1 - s), :] = wr
        w_ref[1, trows(S5_L - 1 - s), :] = wi
        pr, pi = _cmul(pr, pi, lbr, lbi)
        vr, vi = _cmul(cr, ci, pr, pi)
        v_ref[0, trows(s), :] = vr
        v_ref[1, trows(s), :] = -vi
    laml_ref[0:1, :] = pr
    laml_ref[1:2, :] = pi

    lane = lax.broadcasted_iota(jnp.int32, (1, 128), 1)
    for gp in range(S5_PAIRS):
        lanes = slice(gp * 128, (gp + 1) * 128)
        for r in range(2):
            for src, dst in ((w_ref, wexp_ref), (v_ref, vexp_ref)):
                slab = src[r, :, lanes]
                dst[r, gp] = jnp.concatenate([jnp.where(lane < S5_P, slab, 0.0),
                                              jnp.where(lane >= S5_P, slab, 0.0)], axis=0).astype(BF16)

    gm = gm_ref[...]
    kall = None
    for r, b, sign in ((0, bbr, 1.0), (1, bbi, -1.0)):
        ah, al = _split2(a_ref[r])
        bh, bl = _split2(jnp.concatenate([b] * S5_GROUPS, axis=0) * gm)
        nt = lambda x, y: lax.dot_general(x, y, (((1,), (1,)), ((), ())), preferred_element_type=F32)
        term = (nt(ah, bh) + nt(ah, bl) + nt(al, bh)) * sign
        kall = term if kall is None else kall + term

    zeros = jnp.zeros((lc, S5_W), BF16)
    kall = kall.astype(BF16)
    for delta in range(S5_GROUPS):
        q_ref[delta, 0:lc, :] = zeros
        q_ref[delta, lc:2 * lc, :] = kall if delta == 0 else pltpu.roll(kall, delta * S5_CH, axis=1)
        q_ref[delta, 2 * lc:3 * lc, :] = zeros
    for g in range(S5_GROUPS):
        acc = zeros
        for s in range(S5_L):
            start = pl.multiple_of(lc - S5_CH * s + d * (S5_CH * (S5_L - 1)), S5_CH)
            blk = q_ref[(s - g) % S5_GROUPS, pl.ds(start, lc), :]
            acc = jnp.where(bm_ref[s:s + 1, :] > 0.5, blk, acc)
        tt_ref[g] = acc


def _s5_consts(lam_re, lam_im, log_step, b_re, b_im, c_re, c_im):
    n = DEPTH * 2
    lc = S5_L * S5_CH
    lr = lam_re.reshape(n, 1, S5_N)
    li = lam_im.reshape(n, 1, S5_N)
    st = jnp.repeat(log_step.reshape(n, S5_GROUPS), S5_P, axis=-1).reshape(n, 1, S5_N)
    br = b_re.reshape(n, S5_N, S5_CH).transpose(0, 2, 1)
    bi = b_im.reshape(n, S5_N, S5_CH).transpose(0, 2, 1)
    cr = c_re.reshape(n, S5_GROUPS, S5_CH, S5_P).transpose(0, 2, 1, 3).reshape(n, S5_CH, S5_N)
    ci = c_im.reshape(n, S5_GROUPS, S5_CH, S5_P).transpose(0, 2, 1, 3).reshape(n, S5_CH, S5_N)
    gm = jnp.asarray(np.kron(np.eye(S5_GROUPS), np.ones((S5_CH, S5_P))), F32)
    row = pl.BlockSpec((None, 1, S5_N), lambda i: (i, 0, 0))
    mat = pl.BlockSpec((None, S5_CH, S5_N), lambda i: (i, 0, 0))
    exp = pl.BlockSpec((None, 2, S5_PAIRS, 2 * lc, 128), lambda i: (i, 0, 0, 0, 0))
    laml, wexp, vexp, tt = pl.pallas_call(
        _s5_consts_kernel,
        grid=(n,),
        in_specs=[row, row, row, mat, mat, mat, mat, pl.BlockSpec((S5_W, S5_N), lambda i: (0, 0)),
                  pl.BlockSpec((S5_GROUPS, S5_W), lambda i: (0, 0))],
        out_specs=[pl.BlockSpec((None, 2, S5_N), lambda i: (i, 0, 0)), exp, exp,
                   pl.BlockSpec((None, S5_GROUPS, lc, lc), lambda i: (i, 0, 0, 0))],
        out_shape=[jax.ShapeDtypeStruct((n, 2, S5_N), F32),
                   jax.ShapeDtypeStruct((n, 2, S5_PAIRS, 2 * lc, 128), BF16),
                   jax.ShapeDtypeStruct((n, 2, S5_PAIRS, 2 * lc, 128), BF16),
                   jax.ShapeDtypeStruct((n, S5_GROUPS, lc, lc), BF16)],
        scratch_shapes=[pltpu.VMEM((2, lc, S5_N), F32), pltpu.VMEM((2, lc, S5_N), F32),
                        pltpu.VMEM((2, lc, S5_N), F32), pltpu.VMEM((S5_GROUPS, 3 * lc, S5_W), BF16)],
        compiler_params=_cparams(("arbitrary",)),
        name="s5_consts",
    )(lr, li, st, br, bi, cr, ci, gm, _lane_block_masks())
    return laml.reshape(n, 2, S5_PAIRS, 128), wexp, vexp, tt


def _lane_block_masks():
    return jnp.asarray(np.kron(np.eye(S5_GROUPS), np.ones((1, S5_CH))), F32)


def _nt(a, b):
    return lax.dot_general(a, b, (((1,), (1,)), ((), ())), preferred_element_type=F32)


def _lane_block_transpose(arrs):
    n = len(arrs)
    width = arrs[0].shape[1]
    blk = lax.broadcasted_iota(jnp.int32, (1, width), 1) // (width // n)
    arrs = list(arrs)
    h = n // 2
    while h >= 1:
        hi = (blk & h) != 0
        for a in range(n):
            if a & h:
                continue
            lo_arr, hi_arr = arrs[a], arrs[a + h]
            arrs[a] = jnp.where(hi, pltpu.roll(hi_arr, h * (width // n), axis=1), lo_arr)
            arrs[a + h] = jnp.where(hi, hi_arr, pltpu.roll(lo_arr, width - h * (width // n), axis=1))
        h //= 2
    return arrs


def _s5_kernel(u0_ref, u1_ref, tt_ref, wexp_ref, vexp_ref, laml_ref, *rest, nseq, nblk, zero_init):
    if zero_init:
        y_ref, hf_ref, ug_sc, yg_sc, hin_re, hin_im, hp_re, hp_im = rest
    else:
        h0_ref, y_ref, hf_ref, ug_sc, yg_sc, hin_re, hin_im, hp_re, hp_im = rest
    d = pl.program_id(0)
    r = nblk * nseq
    lc = S5_L * S5_CH

    @pl.when(d == 0)
    def _():
        xs = []
        for s in range(S5_L):
            rows = pl.ds(s, r, stride=S5_L)
            xs.append(jnp.concatenate([u0_ref[rows, :], u1_ref[rows, :]], axis=1).astype(BF16))
        for g, ug in enumerate(_lane_block_transpose(xs)):
            ug_sc[g] = ug

    for gp in range(S5_PAIRS):
        up = jnp.concatenate([ug_sc[2 * gp], ug_sc[2 * gp + 1]], axis=1)
        rows = pl.ds(gp, r, stride=S5_PAIRS)
        hin_re[rows, :] = jnp.dot(up, wexp_ref[0, gp], preferred_element_type=F32)
        hin_im[rows, :] = jnp.dot(up, wexp_ref[1, gp], preferred_element_type=F32)

    lr = laml_ref[0]
    li = laml_ref[1]

    def step(kk, carry):
        k = kk + d * (nblk - 1 - 2 * kk)
        new = []
        for b in range(nseq):
            rows = pl.ds(pl.multiple_of((b * nblk + k) * S5_PAIRS, S5_PAIRS), S5_PAIRS)
            hr, hi = carry[2 * b], carry[2 * b + 1]
            hp_re[rows, :] = hr
            hp_im[rows, :] = hi
            new += [lr * hr - li * hi + hin_re[rows, :], lr * hi + li * hr + hin_im[rows, :]]
        return tuple(new)

    init = []
    for b in range(nseq):
        rows = slice(b * S5_PAIRS, (b + 1) * S5_PAIRS)
        init += ([jnp.zeros((S5_PAIRS, 128), F32)] * 2 if zero_init else [h0_ref[0, rows, :], h0_ref[1, rows, :]])
    fin = lax.fori_loop(0, nblk, step, tuple(init))
    for b in range(nseq):
        rows = slice(b * S5_PAIRS, (b + 1) * S5_PAIRS)
        hf_ref[0, rows, :] = fin[2 * b]
        hf_ref[1, rows, :] = fin[2 * b + 1]

    for gp in range(S5_PAIRS):
        rows = pl.ds(gp, r, stride=S5_PAIRS)
        ys = (_nt(hp_re[rows, :].astype(BF16), vexp_ref[0, gp])
              + _nt(hp_im[rows, :].astype(BF16), vexp_ref[1, gp]))
        for j in range(2):
            g = 2 * gp + j
            yg = _nt(ug_sc[g], tt_ref[g]) + ys[:, j * lc:(j + 1) * lc]

            @pl.when(d == 0)
            def _():
                yg_sc[g] = yg

            @pl.when(d == 1)
            def _():
                yg_sc[g] += yg

    @pl.when(d == 1)
    def _():
        for t, yt in enumerate(_lane_block_transpose([yg_sc[g] for g in range(S5_GROUPS)])):
            rows = pl.ds(t, r, stride=S5_L)
            y_ref[0, rows, :] = yt[:, 0:128]
            y_ref[1, rows, :] = yt[:, 128:2 * 128]


def _s5(proj, consts, h0, *, layer, nseq, n_seq):
    laml, wexp, vexp, tt = consts
    t = proj.shape[0]
    nblk = n_seq // S5_L
    r = nblk * nseq
    lc = S5_L * S5_CH
    bs = nseq * S5_PAIRS
    zero_init = h0 is None
    h_spec = pl.BlockSpec((None, 2, bs, 128), lambda d: (d, 0, 0, 0))
    return pl.pallas_call(
        functools.partial(_s5_kernel, nseq=nseq, nblk=nblk, zero_init=zero_init),
        grid=(2,),
        in_specs=[pl.BlockSpec((t, 128), lambda d: (0, P_SU // 128)),
                  pl.BlockSpec((t, 128), lambda d: (0, P_SU // 128 + 1)),
                  pl.BlockSpec((None, S5_GROUPS, lc, lc), lambda d: (2 * layer + d, 0, 0, 0)),
                  pl.BlockSpec((None, 2, S5_PAIRS, 2 * lc, 128), lambda d: (2 * layer + d, 0, 0, 0, 0)),
                  pl.BlockSpec((None, 2, S5_PAIRS, 2 * lc, 128), lambda d: (2 * layer + d, 0, 0, 0, 0)),
                  pl.BlockSpec((None, 2, S5_PAIRS, 128), lambda d: (2 * layer + d, 0, 0, 0))]
                 + ([] if zero_init else [h_spec]),
        out_specs=[pl.BlockSpec((2, t, 128), lambda d: (0, 0, 0)), h_spec],
        out_shape=[jax.ShapeDtypeStruct((2, t, 128), F32),
                   jax.ShapeDtypeStruct((2, 2, bs, 128), F32)],
        scratch_shapes=[pltpu.VMEM((S5_GROUPS, r, lc), BF16), pltpu.VMEM((S5_GROUPS, r, lc), F32)]
                       + [pltpu.VMEM((r * S5_PAIRS, 128), F32) for _ in range(4)],
        compiler_params=_cparams(("arbitrary",)),
        name="s5",
    )(proj, proj, tt, wexp, vexp, laml, *([] if zero_init else [h0]))


def _out_kernel(x_ref, mod_ref, ogf_ref, ogb_ref, odf_ref, odb_ref, ysf_ref, ysb_ref, gog_ref, su_ref, dog_ref,
                ind_ref, gw_ref, dw_ref, sd_ref, wglu_ref, bglu_ref, wout32_ref, o_ref, wout_ref):
    @pl.when(pl.program_id(0) == 0)
    def _():
        wout_ref[...] = wout32_ref[...].astype(BF16)

    og = ogf_ref[...] + ogb_ref[...]
    hi, mid, _ = _split3(og * og)
    ms = jnp.dot(hi, ind_ref[...], preferred_element_type=F32) + jnp.dot(mid, ind_ref[...], preferred_element_type=F32)
    og = og * lax.rsqrt(ms + EPS) * gw_ref[...] * _silu(gog_ref[...])
    od = odf_ref[...] + odb_ref[...]
    dog = dog_ref[...]
    parts = []
    for h in range(GDN_HEADS):
        z = od[:, h * GDN_DV:(h + 1) * GDN_DV]
        z = z * lax.rsqrt(jnp.mean(z * z, axis=-1, keepdims=True) + EPS) * dw_ref[...]
        parts.append(z * _silu(dog[:, h * GDN_DV:(h + 1) * GDN_DV]))
    y = su_ref[...] * sd_ref[...] + jnp.concatenate([ysf_ref[...], ysb_ref[...]], axis=1)
    g = 0.5 * y * (1.0 + jnp.tanh(math.sqrt(2.0 / math.pi) * (y + 0.044715 * (y * y * y))))
    os_ = g * jax.nn.sigmoid(_bdot(g, wglu_ref[...]) + bglu_ref[...])
    out = _bdot(og, wout_ref[0:GLA_W, :]) + _bdot(os_, wout_ref[GLA_W + GDN_W:MIX_W, :])
    for h in range(GDN_HEADS):
        lo = GLA_W + h * GDN_DV
        out = out + _bdot(parts[h], wout_ref[lo:lo + GDN_DV, :])
    o_ref[...] = x_ref[...] + mod_ref[5:6, :] * out


def _mix_out(x, mods, proj, og, od, ys, gla_nw, gdn_nw, s5_d, w_glu, b_glu, w_out, *, layer, n_seq):
    t, d = x.shape
    nb = mods.shape[0]
    tm = _pick_tile(n_seq if nb > 1 else t, MIX_TM)
    midx = _mod_index(nb, tm, n_seq)
    ys2 = ys

    def dirspec(w, dd):
        return pl.BlockSpec((None, tm, w), lambda i: (dd, i, 0))

    def vec(w):
        return pl.BlockSpec((None, 1, w), lambda i: (layer, 0, 0))

    return pl.pallas_call(
        _out_kernel,
        grid=(t // tm,),
        in_specs=[pl.BlockSpec((tm, d), lambda i: (i, 0)),
                  pl.BlockSpec((None, N_MOD, d), lambda i: (midx(i), 0, 0)),
                  pl.BlockSpec((tm, GLA_W), lambda i: (i, 0)), pl.BlockSpec((tm, GLA_W), lambda i: (i, 0)),
                  pl.BlockSpec((tm, GDN_W), lambda i: (i, 0)), pl.BlockSpec((tm, GDN_W), lambda i: (i, 0)),
                  dirspec(128, 0), dirspec(128, 1),
                  pl.BlockSpec((tm, GLA_W), lambda i: (i, P_GOG // GLA_W)),
                  pl.BlockSpec((tm, S5_W), lambda i: (i, P_SU // S5_W)),
                  pl.BlockSpec((tm, GDN_W), lambda i: (i, P_DOG // GDN_W)),
                  pl.BlockSpec((GLA_W, GLA_W), lambda i: (0, 0)),
                  vec(GLA_W), vec(GDN_DV), vec(S5_W),
                  pl.BlockSpec((None, S5_W, S5_W), lambda i: (layer, 0, 0)),
                  vec(S5_W),
                  pl.BlockSpec((None, MIX_W, d), lambda i: (layer, 0, 0), pipeline_mode=pl.Buffered(1))],
        out_specs=pl.BlockSpec((tm, d), lambda i: (i, 0)),
        out_shape=jax.ShapeDtypeStruct((t, d), F32),
        scratch_shapes=[pltpu.VMEM((MIX_W, d), BF16)],
        compiler_params=_cparams(("arbitrary",)),
        name="mix_out",
    )(x, mods, og[0], og[1], od[0], od[1], ys2, ys2, proj, proj, proj,
      jnp.asarray(_C["ind64"], BF16),
      jnp.tile(gla_nw, (1, GLA_HEADS)).reshape(DEPTH, 1, GLA_W), gdn_nw.reshape(DEPTH, 1, GDN_DV),
      s5_d.reshape(DEPTH, 1, S5_W), w_glu, b_glu.reshape(DEPTH, 1, S5_W), w_out)


def _gla_params(gk_up, gk_bias, layer):
    up = jnp.zeros((2, 128, GLA_KW), F32).at[:, :GLA_RANK, :].set(gk_up[layer])
    return up, gk_bias[layer].reshape(2, 1, GLA_KW)


def _gdn_params(a_log, dt_bias, layer):
    arow = jnp.zeros((2, 1, 128), F32).at[:, 0, SM_A:SM_A + GDN_HEADS].set(a_log[layer])
    dtb = jnp.zeros((2, 1, 128), F32).at[:, 0, SM_A:SM_A + GDN_HEADS].set(dt_bias[layer])
    return arow, dtb


def _gla_state_in(s):
    eye = jnp.eye(GLA_HEADS, dtype=F32)
    st = jnp.einsum('bzhde,hg->zbhegd', s.astype(F32), eye)
    return st.reshape(2, s.shape[0], GLA_W, GLA_KW)


def _trunk(x3, mods, rows, states, p, s5c, w_in):
    nseq, n_seq, d = x3.shape
    x = x3.reshape(nseq * n_seq, d)
    fin = []
    gla_f = jnp.zeros((nseq, DEPTH, 2, GLA_HEADS, GLA_DK, GLA_DV), F32)
    gdn_f = jnp.zeros((nseq, DEPTH, 2, GDN_HEADS, GDN_DK, GDN_DV), F32)
    for l in range(DEPTH):
        m = mods[l]
        x = _ffn(x, m, p['norm_w'], p['ffn_w_gate'], p['ffn_w_up'], p['ffn_w_down'], p['final_norm_w'],
                 layer=l, which=0, n_seq=n_seq, final=False)
        proj = _proj(x, m, p['norm_w'], w_in, p['gdn_conv_w'], layer=l, n_seq=n_seq, rowlen=n_seq // rows)
        if states is None:
            st0 = s0 = h0 = None
        else:
            st_gla, st_gdn, st_re, st_im = states
            st0 = _gla_state_in(st_gla[:, l])
            s0 = st_gdn[:, l].transpose(1, 0, 2, 3, 4)
            h0 = jnp.stack([st_re[:, l].reshape(nseq, 2, S5_N), st_im[:, l].reshape(nseq, 2, S5_N)], axis=0)
            h0 = h0.transpose(2, 0, 1, 3).reshape(2, 2, nseq * S5_PAIRS, 128)
        up_p, bias_p = _gla_params(p['gla_gk_up'], p['gla_gk_bias'], l)
        og_f, og_b, gla_f = _gla(proj, up_p, bias_p, st0, gla_f, layer=l, nseq=nseq, n_seq=n_seq)
        og = (og_f, og_b)
        arow, dtb = _gdn_params(p['gdn_a_log'], p['gdn_dt_bias'], l)
        od_f, od_b, gdn_f = _gdn(proj, arow, dtb, s0, gdn_f, layer=l, nseq=nseq, n_seq=n_seq)
        od = (od_f, od_b)
        ys, s5_f = _s5(proj, s5c, h0, layer=l, nseq=nseq, n_seq=n_seq)
        x = _mix_out(x, m, proj, og, od, ys, p['gla_norm_w'], p['gdn_norm_w'], p['s5_d'], p['s5_w_glu'],
                     p['s5_b_glu'], p['w_out'], layer=l, n_seq=n_seq)
        x = _ffn(x, m, p['norm_w'], p['ffn_w_gate'], p['ffn_w_up'], p['ffn_w_down'], p['final_norm_w'],
                 layer=l, which=1, n_seq=n_seq, final=(l == DEPTH - 1))
        fin.append((gla_f, gdn_f, s5_f))
    return x.reshape(nseq, n_seq, d), fin


def kernel(x_prompt, x_sample, c, state_gla, state_gdn, state_s5_re, state_s5_im, c_ctx, w_ada, b_ada, norm_w, ffn_w_gate, ffn_w_up, ffn_w_down, w_in, gla_gk_up, gla_gk_bias, gla_norm_w, gdn_conv_w, gdn_a_log, gdn_dt_bias, gdn_norm_w, s5_lam_re, s5_lam_im, s5_log_step, s5_b_re, s5_b_im, s5_c_re, s5_c_im, s5_d, s5_w_glu, s5_b_glu, w_out, final_norm_w):
    p = dict(norm_w=norm_w, ffn_w_gate=ffn_w_gate, ffn_w_up=ffn_w_up, ffn_w_down=ffn_w_down,
             gla_gk_up=gla_gk_up, gla_gk_bias=gla_gk_bias, gla_norm_w=gla_norm_w, gdn_conv_w=gdn_conv_w,
             gdn_a_log=gdn_a_log, gdn_dt_bias=gdn_dt_bias, gdn_norm_w=gdn_norm_w, s5_c_re=s5_c_re,
             s5_c_im=s5_c_im, s5_d=s5_d, s5_w_glu=s5_w_glu, s5_b_glu=s5_b_glu, w_out=w_out,
             final_norm_w=final_norm_w)
    b_ctx = x_prompt.shape[0]
    b_dec = x_sample.shape[0]
    cond8 = jnp.zeros((8, D_MODEL), F32).at[0].set(c_ctx).at[1:1 + b_dec].set(c)
    mods = _ada(cond8, w_ada, b_ada)
    mods_ctx = mods[:, 0:1].reshape(DEPTH, 1, N_MOD, D_MODEL)
    mods_dec = mods[:, 1:1 + b_dec].reshape(DEPTH, b_dec, N_MOD, D_MODEL)
    s5c = _s5_consts(s5_lam_re, s5_lam_im, s5_log_step, s5_b_re, s5_b_im, s5_c_re, s5_c_im)

    y_prompt, fin = _trunk(x_prompt, mods_ctx, 1, None, p, s5c, w_in)
    new_gla, new_gdn = fin[-1][0], fin[-1][1]
    s5f = jnp.stack([f[2].reshape(2, 2, b_ctx, S5_GROUPS, S5_P).transpose(1, 2, 0, 3, 4) for f in fin], axis=2)
    new_re, new_im = s5f[0], s5f[1]

    rows = x_sample.shape[1] // 64
    y_sample, _ = _trunk(x_sample, mods_dec, rows, (state_gla, state_gdn, state_s5_re, state_s5_im),
                         p, s5c, w_in)
    return (y_prompt, y_sample, new_gla.astype(state_gla.dtype), new_gdn.astype(state_gdn.dtype),
            new_re.astype(state_s5_re.dtype), new_im.astype(state_s5_im.dtype))
```

```python
import functools
import math

import numpy as np
import jax
import jax.numpy as jnp
from jax import lax
from jax.experimental import pallas as pl
from jax.experimental.pallas import tpu as pltpu

F32 = jnp.float32
BF16 = jnp.bfloat16

D_MODEL = 1024
DEPTH = 2
FFN_DIM = 2816
N_MOD = 9
EPS = 1e-6
CHUNK = 64
GLA_HEADS, GLA_DK, GLA_DV, GLA_RANK = 4, 32, 64, 16
GLA_GATE_NORM = 16.0
GDN_HEADS, GDN_DK, GDN_DV, GDN_CONV = 4, 128, 128, 5
S5_GROUPS, S5_CH, S5_P = 16, 16, 64
GLA_W = GLA_HEADS * GLA_DV
GLA_KW = GLA_HEADS * GLA_DK
GDN_W = GDN_HEADS * GDN_DV
S5_W = S5_GROUPS * S5_CH
S5_N = S5_GROUPS * S5_P
S5_L = 16
S5_PAIRS = S5_GROUPS // 2
MIX_W = GLA_W + GDN_W + S5_W

_IN_SIZES = (GLA_KW, GLA_KW, GLA_W, GLA_RANK, GLA_RANK, GLA_W,
             GDN_W, GDN_W, GDN_W, GDN_HEADS, GDN_HEADS, GDN_HEADS, GDN_HEADS, GDN_W, S5_W)
_IN_OFF = tuple(int(v) for v in np.cumsum((0,) + _IN_SIZES))
(_GQ, _GK, _GV, _GLRF, _GLRB, _GOG, _DQ, _DK, _DV, _DAF, _DAB, _DBF, _DBB, _DOG, _SU) = range(15)

P_DQKV = 0
P_SMALL = 1536
P_GQK = 1792
P_GV = 2048
P_GOG = 2304
P_DOG = 2560
P_SU = 3072
PROJ_W = 3328
PROJ_TN = PROJ_W // 2
SM_A = GLA_RANK
SM_B = GLA_RANK + GDN_HEADS

SUB = 256
GLA_SUB = 256
FFN_TM = 2048
FFN_TF = 256
PROJ_TM = 1024
MIX_TM = 512
ADA_TN = 1024
V7X_VMEM_BYTES = 64 * 1024 * 1024
VMEM_LIMIT = V7X_VMEM_BYTES - 8 * 1024 * 1024


def _cparams(sem):
    return pltpu.CompilerParams(dimension_semantics=sem, vmem_limit_bytes=VMEM_LIMIT)


def _bdot(a, b):
    return jnp.dot(a.astype(BF16), b.astype(BF16), preferred_element_type=F32)


def _bdot_nt(a, b):
    return lax.dot_general(a.astype(BF16), b.astype(BF16), (((1,), (1,)), ((), ())),
                           preferred_element_type=F32)


def _split3(x):
    hi = x.astype(BF16)
    r1 = x - hi.astype(F32)
    mid = r1.astype(BF16)
    lo = (r1 - mid.astype(F32)).astype(BF16)
    return hi, mid, lo


def _sel_dot(m01, x):
    hi, mid, lo = _split3(x)
    return (jnp.dot(m01, hi, preferred_element_type=F32) + jnp.dot(m01, mid, preferred_element_type=F32)
            + jnp.dot(m01, lo, preferred_element_type=F32))


def _silu(x):
    return x * jax.nn.sigmoid(x)


def _softplus(x):
    return jnp.maximum(x, 0.0) + jnp.log1p(jnp.exp(-jnp.abs(x)))


def _log_sigmoid(x):
    return -_softplus(-x)


def _norm_mod(x, nw, scale, shift):
    y = x * lax.rsqrt(jnp.mean(x * x, axis=-1, keepdims=True) + EPS)
    return y * nw * (1.0 + scale) + shift


def _pick_tile(n, cap):
    t = cap
    while n % t:
        t //= 2
    return t


def _ada_kernel(c_ref, w_ref, b_ref, o_ref):
    o_ref[...] = _bdot(_silu(c_ref[...]), w_ref[...]) + b_ref[...]


def _ada(cond8, w_ada, b_ada):
    depth, d, nout = w_ada.shape
    tn = ADA_TN
    return pl.pallas_call(
        _ada_kernel,
        grid=(depth, nout // tn),
        in_specs=[pl.BlockSpec((8, d), lambda l, j: (0, 0)),
                  pl.BlockSpec((None, d, tn), lambda l, j: (l, 0, j)),
                  pl.BlockSpec((None, 1, tn), lambda l, j: (l, 0, j))],
        out_specs=pl.BlockSpec((None, 8, tn), lambda l, j: (l, 0, j)),
        out_shape=jax.ShapeDtypeStruct((depth, 8, nout), F32),
        compiler_params=_cparams(("parallel", "parallel")),
        name="ada",
    )(cond8, w_ada, b_ada.reshape(depth, 1, nout))


def _ffn_kernel(x_ref, mod_ref, nw_ref, wg_ref, wu_ref, wd_ref, fw_ref, o_ref, h_ref, *, mod_base, final):
    j = pl.program_id(1)

    @pl.when(j == 0)
    def _():
        m = mod_ref[...]
        h = _norm_mod(x_ref[...], nw_ref[...], m[mod_base + 1:mod_base + 2], m[mod_base:mod_base + 1])
        h_ref[...] = h.astype(BF16)
        o_ref[...] = jnp.zeros_like(o_ref)

    h = h_ref[...]
    g = jnp.dot(h, wg_ref[...].astype(BF16), preferred_element_type=F32)
    u = jnp.dot(h, wu_ref[...].astype(BF16), preferred_element_type=F32)
    o_ref[...] += _bdot(_silu(g) * u, wd_ref[...])

    @pl.when(j == pl.num_programs(1) - 1)
    def _():
        m = mod_ref[...]
        y = x_ref[...] + 0.5 * m[mod_base + 2:mod_base + 3] * o_ref[...]
        if final:
            y = y * lax.rsqrt(jnp.mean(y * y, axis=-1, keepdims=True) + EPS) * fw_ref[...]
        o_ref[...] = y


def _mod_index(nb, tm, n_seq):
    if nb == 1:
        return lambda i: 0
    return lambda i: (i * tm) // n_seq


def _ffn(x, mods, norm_w, w_gate, w_up, w_down, final_w, *, layer, which, n_seq, final):
    t, d = x.shape
    f = w_gate.shape[-1]
    nb = mods.shape[0]
    tm = _pick_tile(n_seq if nb > 1 else t, FFN_TM)
    tf = FFN_TF
    midx = _mod_index(nb, tm, n_seq)
    mod_base = 0 if which == 0 else 6
    return pl.pallas_call(
        functools.partial(_ffn_kernel, mod_base=mod_base, final=final),
        grid=(t // tm, f // tf),
        in_specs=[pl.BlockSpec((tm, d), lambda i, j: (i, 0)),
                  pl.BlockSpec((None, N_MOD, d), lambda i, j: (midx(i), 0, 0)),
                  pl.BlockSpec((None, None, 1, d), lambda i, j: (layer, 2 * which, 0, 0)),
                  pl.BlockSpec((None, None, d, tf), lambda i, j: (layer, which, 0, j)),
                  pl.BlockSpec((None, None, d, tf), lambda i, j: (layer, which, 0, j)),
                  pl.BlockSpec((None, None, tf, d), lambda i, j: (layer, which, j, 0)),
                  pl.BlockSpec((1, d), lambda i, j: (0, 0))],
        out_specs=pl.BlockSpec((tm, d), lambda i, j: (i, 0)),
        out_shape=jax.ShapeDtypeStruct((t, d), F32),
        scratch_shapes=[pltpu.VMEM((tm, d), BF16)],
        compiler_params=_cparams(("parallel", "arbitrary")),
        name="ffn",
    )(x, mods, norm_w.reshape(DEPTH, 3, 1, d), w_gate, w_up, w_down, final_w.reshape(1, d))


def _proj_kernel(x_ref, mod_ref, nw_ref, w_ref, cw_ref, o_ref, h_ref, wb_ref, st_ref, *, rowlen):
    j = pl.program_id(0)
    i = pl.program_id(1)

    for jt, segs in enumerate(_W_SEGS):
        @pl.when(jnp.logical_and(i == 0, j == jt))
        def _():
            for lo, src, width in segs:
                if width % 128 == 0:
                    wb_ref[lo:lo + width, :] = w_ref[src:src + width, :].astype(BF16)
                else:
                    base = lo // 128 * 128
                    if src is None:
                        st_ref[lo - base:lo - base + width, :] = jnp.zeros((width, st_ref.shape[1]), F32)
                        wb_ref[base:base + 128, :] = st_ref[...].astype(BF16)
                    else:
                        st_ref[lo - base:lo - base + width, :] = w_ref[src:src + width, :]

    @pl.when(j == 0)
    def _():
        m = mod_ref[...]
        h_ref[i] = _norm_mod(x_ref[...], nw_ref[...], m[4:5], m[3:4]).astype(BF16)
        o_ref[...] = _nt(h_ref[i], wb_ref[...])
        tm = o_ref.shape[0]
        pos = lax.broadcasted_iota(jnp.int32, (tm, 1), 0) % rowlen
        valid = [jnp.logical_and(pos + (t - GDN_CONV // 2) >= 0, pos + (t - GDN_CONV // 2) < rowlen)
                 for t in range(GDN_CONV)]
        for g in range(3 * GDN_HEADS):
            cols = slice(P_DQKV + g * GDN_DK, P_DQKV + (g + 1) * GDN_DK)
            xg = o_ref[:, cols]
            acc = xg * cw_ref[GDN_CONV // 2:GDN_CONV // 2 + 1, cols]
            for t in range(GDN_CONV):
                s = t - GDN_CONV // 2
                if s != 0:
                    xs = pltpu.roll(xg, (-s) % tm, axis=0)
                    acc = acc + jnp.where(valid[t], xs, 0.0) * cw_ref[t:t + 1, cols]
            y = _silu(acc)
            if g < 2 * GDN_HEADS:
                y = y * lax.rsqrt(jnp.sum(y * y, axis=-1, keepdims=True) + EPS)
                if g < GDN_HEADS:
                    y = y * (GDN_DK ** -0.5)
            o_ref[:, cols] = y

    @pl.when(j == 1)
    def _():
        o_ref[...] = _nt(h_ref[i], wb_ref[...])


def _proj(x, mods, norm_w, w_in, conv_w, *, layer, n_seq, rowlen):
    t, d = x.shape
    nb = mods.shape[0]
    tm = _pick_tile(n_seq if nb > 1 else t, PROJ_TM)
    midx = _mod_index(nb, tm, n_seq)
    return pl.pallas_call(
        functools.partial(_proj_kernel, rowlen=rowlen),
        grid=(PROJ_W // PROJ_TN, t // tm),
        in_specs=[pl.BlockSpec((tm, d), lambda j, i: (i * (1 - j), 0)),
                  pl.BlockSpec((None, N_MOD, d), lambda j, i: (midx(i), 0, 0)),
                  pl.BlockSpec((None, None, 1, d), lambda j, i: (layer, 1, 0, 0)),
                  pl.BlockSpec((None, w_in.shape[-1], d), lambda j, i: (layer, 0, 0), pipeline_mode=pl.Buffered(1)),
                  pl.BlockSpec((None, GDN_CONV, 3 * GDN_W), lambda j, i: (layer, 0, 0))],
        out_specs=pl.BlockSpec((tm, PROJ_TN), lambda j, i: (i, j)),
        out_shape=jax.ShapeDtypeStruct((t, PROJ_W), F32),
        scratch_shapes=[pltpu.VMEM((t // tm, tm, d), BF16), pltpu.VMEM((PROJ_TN, d), BF16),
                        pltpu.VMEM((128, d), F32)],
        compiler_params=_cparams(("arbitrary", "arbitrary")),
        name="proj",
    )(x, mods, norm_w.reshape(DEPTH, 3, 1, d), jnp.swapaxes(w_in, 1, 2), conv_w)


def _w_in_segments():
    order = [_DQ, _DK, _DV, _GLRF, _DAF, _DBF, None, _GLRB, _DAB, _DBB, None,
             _GQ, _GK, _GV, _GOG, _DOG, _SU]
    sm_pad = 128 - GLA_RANK - 2 * GDN_HEADS
    tiles = [[] for _ in range(PROJ_W // PROJ_TN)]
    dst = 0
    for s in order:
        width = sm_pad if s is None else _IN_SIZES[s]
        src = None if s is None else _IN_OFF[s]
        tile = tiles[dst // PROJ_TN]
        lo = dst % PROJ_TN
        if tile and src is not None and tile[-1][1] is not None and (
                tile[-1][0] + tile[-1][2] == lo and tile[-1][1] + tile[-1][2] == src):
            tile[-1] = (tile[-1][0], tile[-1][1], tile[-1][2] + width)
        else:
            tile.append((lo, src, width))
        dst += width
        assert (dst - 1) // PROJ_TN == (dst - width) // PROJ_TN, "a segment may not straddle column tiles"
    assert dst == PROJ_W
    return tiles


_W_SEGS = _w_in_segments()


def _np_consts():
    t = np.arange(CHUNK)
    lower = (t[:, None] >= t[None, :])
    incl = np.stack([lower, lower.T])
    strict = np.stack([t[:, None] > t[None, :], t[:, None] < t[None, :]])
    eye4 = np.eye(4, dtype=bool)
    blk = np.kron(eye4, np.ones((CHUNK, CHUNK), bool))
    incl_bd = np.stack([np.kron(eye4, incl[d]) for d in range(2)])
    strict_bd = np.stack([np.kron(eye4, strict[d]) for d in range(2)])
    levels = []
    for d in range(2):
        per = []
        for lv in range(6):
            s = 1 << lv
            same = (t[:, None] // (2 * s)) == (t[None, :] // (2 * s))
            hi = (t % (2 * s)) >= s
            m = same & hi[:, None] & (~hi)[None, :]
            if d == 1:
                m = m.T
            per.append(np.kron(eye4, m))
        levels.append(np.stack(per))
    levels = np.stack(levels)
    assert (levels.sum(1) == strict_bd).all() and blk.any()
    hm_k = np.kron(eye4, np.ones((1, GLA_DK)))
    hm_v = np.kron(eye4, np.ones((1, GLA_DV)))
    st_mask = np.kron(eye4, np.ones((GLA_DV, GLA_DK)))
    ind64 = np.kron(eye4, np.full((GLA_DV, GLA_DV), 1.0 / GLA_DV))
    gla_sel, gla_later, gla_earlier, gla_pair = [], [], [], []
    for d in range(2):
        pos = t if d == 0 else CHUNK - 1 - t
        cum = (pos[None, :] <= pos[:, None]).astype(np.float64)
        sel, later, earlier, pair = [cum], [], [], []
        for lv in range(6):
            s = 1 << lv
            ref_pos = (pos // (2 * s)) * (2 * s) + s - 1
            pick = (pos[None, :] == ref_pos[:, None]).astype(np.float64)
            sel.append(pick @ cum)
            is_later = (pos % (2 * s)) >= s
            same = (pos[:, None] // (2 * s)) == (pos[None, :] // (2 * s))
            later.append(np.repeat(is_later[:, None], GLA_KW, axis=1))
            earlier.append(np.repeat(~is_later[:, None], GLA_KW, axis=1))
            pair.append(np.tile(same & is_later[:, None] & (~is_later)[None, :], (1, 4)))
        gla_sel.append(np.concatenate(sel, axis=0))
        gla_later.append(np.stack(later))
        gla_earlier.append(np.stack(earlier))
        gla_pair.append(np.stack(pair))
        assert (np.stack(pair).sum(0) == np.tile(pos[:, None] > pos[None, :], (1, 4))).all()
    head_sum = np.kron(eye4, np.ones((GLA_DK, GLA_DV)))
    return dict(incl=incl, incl_bd=incl_bd, strict_bd=strict_bd, levels=levels,
                hm_k=hm_k, hm_v=hm_v, st_mask=st_mask, ind64=ind64, eye256=np.eye(256),
                gla_sel=np.stack(gla_sel), gla_later=np.stack(gla_later), gla_earlier=np.stack(gla_earlier),
                gla_pair=np.stack(gla_pair), head_sum=head_sum)


_C = _np_consts()


def _row_block_index(nblk):
    return lambda d, b, i: b * nblk + i + d * (nblk - 1 - 2 * i)


def _gla_kernel(qkf_ref, qkb_ref, vf_ref, vb_ref, smf_ref, smb_ref, up_ref, bias_ref, sel_ref, later_ref,
                earlier_ref, pair_ref, hmk_ref, hmv_ref, stm_ref, hsum_ref, *rest, nchunk, zero_init):
    rest = list(rest)
    st0_ref = None if zero_init else rest.pop(0)
    rest.pop(0)
    of_ref, ob_ref, stf_ref, st_ref, qd_sc, m_sc, dec_sc, oi_sc = rest
    ins =((qkf_ref, vf_ref, smf_ref, of_ref), (qkb_ref, vb_ref, smb_ref, ob_ref))

    @pl.when(pl.program_id(1) == 0)
    def _():
        st_ref[...] = jnp.zeros_like(st_ref) if zero_init else st0_ref[...]

    hmkb = hmk_ref[...].astype(BF16)
    hmv = hmv_ref[...]
    stm = stm_ref[...]

    units = [(dd, c) for c in range(nchunk) for dd in range(2)]
    st1 = []
    for dd, c in units:
        qk_ref, v_ref, sm_ref, _ = ins[dd]
        rows = pl.ds(c * CHUNK, CHUNK)
        qk = qk_ref[rows, :]
        q = qk[:, 0:GLA_KW] * (GLA_DK ** -0.5)
        k = qk[:, GLA_KW:2 * GLA_KW]
        x = _bdot(sm_ref[rows, :], up_ref[dd]) + bias_ref[dd]
        lg = _log_sigmoid(x) * (1.0 / GLA_GATE_NORM)
        ball = _sel_dot(sel_ref[dd], lg)
        st1.append((q, k, v_ref[rows, :], ball, jnp.sum(lg, axis=0, keepdims=True)))
    prods = []
    for lv in range(6):
        for (dd, c), (q, k, _, ball, _) in zip(units, st1):
            b = ball[0:CHUNK]
            ref = ball[(lv + 1) * CHUNK:(lv + 2) * CHUNK]
            ql = q * jnp.where(later_ref[dd, lv] > 0.5, jnp.exp(jnp.minimum(b - ref, 0.0)), 0.0)
            kl = (k * jnp.where(earlier_ref[dd, lv] > 0.5, jnp.exp(jnp.minimum(ref - b, 0.0)), 0.0)).astype(BF16)
            kexp = jnp.concatenate([kl * hmkb[h:h + 1] for h in range(GLA_HEADS)], axis=0)
            prods.append(_bdot_nt(ql, kexp))
    for i, ((dd, c), (q, k, v, ball, btot)) in enumerate(zip(units, st1)):
        o_ref = ins[dd][3]
        n = dd * nchunk + c
        b = ball[0:CHUNK]
        att = jnp.zeros((CHUNK, GLA_HEADS * CHUNK), F32)
        for lv in range(6):
            att = jnp.where(pair_ref[dd, lv] > 0.5, prods[lv * len(units) + i], att)
        vexp = jnp.concatenate([v * hmv[h:h + 1] for h in range(GLA_HEADS)], axis=0)
        oi_sc[n] = _bdot(att, vexp) + _bdot(q * k, hsum_ref[...]) * v
        qd_sc[n] = (q * jnp.exp(b)).astype(BF16)
        kd = k * jnp.exp(btot - b)
        m_sc[n] = _bdot(v.T, kd) * stm
        dec_sc[n] = jnp.broadcast_to(jnp.exp(btot), (8, GLA_KW))

    for dd in range(2):
        o_ref = ins[dd][3]
        st = st_ref[dd]
        for ci in range(nchunk):
            c = ci if dd == 0 else nchunk - 1 - ci
            n = dd * nchunk + c
            rows = pl.ds(c * CHUNK, CHUNK)
            o_ref[rows, :] = (oi_sc[n] + _nt(qd_sc[n], st.astype(BF16))).astype(BF16)
            st = st * dec_sc[n][0:1] + m_sc[n]
        st_ref[dd] = st
        for h in range(GLA_HEADS):
            stf_ref[dd, h] = st[h * GLA_DV:(h + 1) * GLA_DV, h * GLA_DK:(h + 1) * GLA_DK].T


def _gla(proj, up_p, bias_p, st0, fin, *, layer, nseq, n_seq):
    t = proj.shape[0]
    sub = min(n_seq, GLA_SUB)
    nblk = n_seq // sub
    nck = sub // CHUNK
    rbi = _row_block_index(nblk)
    c = _C
    consts = [jnp.asarray(c["gla_sel"], BF16), jnp.asarray(c["gla_later"], F32), jnp.asarray(c["gla_earlier"], F32),
              jnp.asarray(c["gla_pair"], F32), jnp.asarray(c["hm_k"], F32), jnp.asarray(c["hm_v"], F32),
              jnp.asarray(c["st_mask"], F32), jnp.asarray(c["head_sum"], BF16)]
    st_spec = pl.BlockSpec((2, None, GLA_W, GLA_KW), lambda b, i: (0, b, 0, 0))
    full = lambda a: pl.BlockSpec(a.shape, lambda b, i: (0,) * a.ndim)
    zero_init = st0 is None
    nu = 2 * nck
    rows = (lambda b, i: rbi(0, b, i), lambda b, i: rbi(1, b, i))

    def blk(width, col, dd):
        return pl.BlockSpec((sub, width), lambda b, i: (rows[dd](b, i), col // width))

    hs = (GLA_HEADS, GLA_DK, GLA_DV)
    args = [proj, proj, proj, proj, proj, proj, up_p, bias_p, *consts] + ([] if zero_init else [st0])
    in_specs = ([blk(2 * GLA_KW, P_GQK, 0), blk(2 * GLA_KW, P_GQK, 1), blk(GLA_W, P_GV, 0), blk(GLA_W, P_GV, 1),
                 blk(128, P_SMALL, 0), blk(128, P_SMALL + 128, 1), full(up_p), full(bias_p)]
                + [full(a) for a in consts] + ([] if zero_init else [st_spec]))
    return pl.pallas_call(
        functools.partial(_gla_kernel, nchunk=nck, zero_init=zero_init),
        grid=(nseq, nblk),
        in_specs=in_specs + [pl.BlockSpec(memory_space=pl.ANY)],
        out_specs=[pl.BlockSpec((sub, GLA_W), lambda b, i: (rows[0](b, i), 0)),
                   pl.BlockSpec((sub, GLA_W), lambda b, i: (rows[1](b, i), 0)), _fin_spec(hs, layer)],
        out_shape=[jax.ShapeDtypeStruct((t, GLA_W), BF16), jax.ShapeDtypeStruct((t, GLA_W), BF16),
                   jax.ShapeDtypeStruct((nseq, DEPTH, 2) + hs, F32)],
        scratch_shapes=[pltpu.VMEM((2, GLA_W, GLA_KW), F32), pltpu.VMEM((nu, CHUNK, GLA_KW), BF16),
                        pltpu.VMEM((nu, GLA_W, GLA_KW), F32), pltpu.VMEM((nu, 8, GLA_KW), F32),
                        pltpu.VMEM((nu, CHUNK, GLA_W), F32)],
        input_output_aliases={len(args): 2},
        compiler_params=_cparams(("arbitrary", "arbitrary")),
        name="gla",
    )(*args, fin)


def _fin_spec(hs, layer):
    return pl.BlockSpec((None, None, 2) + hs, lambda b, i: (b, layer, 0) + (0,) * len(hs))


def _gdn_kernel(qkvf_ref, qkvb_ref, smf_ref, smb_ref, arow_ref, dtb_ref, tri_ref, incl_ref, strict_ref, lvl_ref,
                eye_ref, *rest, nchunk, zero_init):
    rest = list(rest)
    s0_ref = None if zero_init else rest.pop(0)
    rest.pop(0)
    of_ref, ob_ref, sf_ref, s_ref, la_ref, be_ref, w_sc, u0_sc, qkd_sc, qe_sc, kdt_sc, egl_sc = rest
    qkv_refs = (qkvf_ref, qkvb_ref)
    o_refs = (of_ref, ob_ref)

    @pl.when(pl.program_id(1) == 0)
    def _():
        s_ref[...] = jnp.zeros_like(s_ref) if zero_init else s0_ref[...]

    for dd, sm_ref in enumerate((smf_ref, smb_ref)):
        sm = sm_ref[...]
        la_ref[dd] = -jnp.exp(arow_ref[dd]) * _softplus(sm + dtb_ref[dd])
        be_ref[dd] = jax.nn.sigmoid(sm)

    eye = eye_ref[...]

    def stack(ref, r0, lo, width):
        return jnp.concatenate([ref[pl.ds(r0, CHUNK), lo + h * width:lo + (h + 1) * width]
                                for h in range(GDN_HEADS)], axis=0)

    def colstack(a, lo):
        return jnp.concatenate([a[:, lo + h:lo + h + 1] for h in range(GDN_HEADS)], axis=0)

    units = [(dd, c) for c in range(nchunk) for dd in range(2)]
    st1 = []
    for dd, c in units:
        r0 = c * CHUNK
        qkv_ref = qkv_refs[dd]
        qs = stack(qkv_ref, r0, 0, GDN_DK)
        ks = stack(qkv_ref, r0, GDN_W, GDN_DK)
        la = la_ref[dd, pl.ds(r0, CHUNK), :]
        g = _sel_dot(tri_ref[dd], la)
        ksb = ks.astype(BF16)
        st1.append((qs, ks, la, g, _bdot_nt(ksb, ksb), _bdot_nt(qs, ksb)))
    loc = []
    for (dd, c), (qs, ks, la, g, kk, qk) in zip(units, st1):
        r0 = c * CHUNK
        n = dd * nchunk + c
        vs = stack(qkv_refs[dd], r0, 2 * GDN_W, GDN_DV)
        gtot = jnp.sum(la, axis=0, keepdims=True)
        gcol = colstack(g, SM_A)
        kfcol = colstack(jnp.exp(gtot - g), SM_A)
        egcol = colstack(jnp.exp(g), SM_A)
        bcol = colstack(be_ref[dd, pl.ds(r0, CHUNK), :], SM_B)
        grow = jnp.sum(eye * gcol, axis=0, keepdims=True)
        lm_rows, qkd_rows = [], []
        for h in range(GDN_HEADS):
            rs = slice(h * CHUNK, (h + 1) * CHUNK)
            lt = (h * CHUNK) // 128
            ls = slice(lt * 128, (lt + 1) * 128)
            dec = incl_ref[dd, rs, ls] * jnp.exp(jnp.minimum(gcol[rs] - grow[:, ls], 0.0))
            lm = strict_ref[dd, rs, ls] * dec * kk[rs, ls] * bcol[rs]
            qd = qk[rs, ls] * dec
            zero = jnp.zeros_like(lm)
            lm_rows.append(jnp.concatenate([lm, zero] if lt == 0 else [zero, lm], axis=1))
            qkd_rows.append(jnp.concatenate([qd, zero] if lt == 0 else [zero, qd], axis=1))
        lmat = jnp.concatenate(lm_rows, axis=0)
        rhs = jnp.concatenate([ks * (bcol * egcol), vs * bcol], axis=1).astype(BF16)
        qkd_sc[n] = jnp.concatenate(qkd_rows, axis=0).astype(BF16)
        qe_sc[n] = (qs * egcol).astype(BF16)
        kd = ks * kfcol
        for h in range(GDN_HEADS):
            kdt_sc[n * GDN_HEADS + h] = kd[h * CHUNK:(h + 1) * CHUNK].T.astype(BF16)
        egl_sc[n] = jnp.broadcast_to(jnp.exp(gtot), (8, 128))
        loc.append((lmat.astype(BF16), rhs))
    def inv_start(wave):
        return [eye - (loc[i][0] * lvl_ref[units[i][0], 0]).astype(F32) for i in wave]

    def inv_level(wave, tinv, lv):
        tb = [t_.astype(BF16) for t_ in tinv]
        cs = [jnp.dot(loc[i][0] * lvl_ref[units[i][0], lv], t_, preferred_element_type=F32)
              for i, t_ in zip(wave, tb)]
        return [t32 - jnp.dot(t_, c_.astype(BF16), preferred_element_type=F32)
                for t32, t_, c_ in zip(tinv, tb, cs)]

    def solve(wave, tinv):
        for i, t_ in zip(wave, tinv):
            dd, c = units[i]
            sol = jnp.dot(t_.astype(BF16), loc[i][1], preferred_element_type=F32)
            w_sc[dd * nchunk + c] = sol[:, :GDN_DK].astype(BF16)
            u0_sc[dd * nchunk + c] = sol[:, GDN_DK:]

    s_cur = [[s_ref[dd, h] for h in range(GDN_HEADS)] for dd in range(2)]

    def recur(ci):
        step = []
        for dd in range(2):
            c = ci if dd == 0 else nchunk - 1 - ci
            n = dd * nchunk + c
            sb = [s_.astype(BF16) for s_ in s_cur[dd]]
            ub = []
            for h in range(GDN_HEADS):
                rows = pl.ds(h * CHUNK, CHUNK)
                u = u0_sc[n, rows, :] - jnp.dot(w_sc[n, rows, :], sb[h], preferred_element_type=F32)
                ub.append(u.astype(BF16))
            step.append((c, n, sb, ub))
        for dd, (c, n, sb, ub) in enumerate(step):
            egl = egl_sc[n]
            for h in range(GDN_HEADS):
                s_cur[dd][h] = (egl[0:1, SM_A + h:SM_A + h + 1] * s_cur[dd][h]
                                + jnp.dot(kdt_sc[n * GDN_HEADS + h], ub[h], preferred_element_type=F32))
        for dd, (c, n, sb, ub) in enumerate(step):
            o2 = jnp.dot(qkd_sc[n], jnp.concatenate(ub, axis=0), preferred_element_type=F32)
            for h in range(GDN_HEADS):
                rows = pl.ds(h * CHUNK, CHUNK)
                o_refs[dd][pl.ds(c * CHUNK, CHUNK), h * GDN_DV:(h + 1) * GDN_DV] = (
                    jnp.dot(qe_sc[n, rows, :], sb[h], preferred_element_type=F32)
                    + o2[h * CHUNK:(h + 1) * CHUNK]).astype(BF16)

    every = list(range(len(units)))
    tinv = inv_start(every)
    for lv in range(1, 6):
        tinv = inv_level(every, tinv, lv)
    solve(every, tinv)
    for ci in range(nchunk):
        recur(ci)
    for dd in range(2):
        for h in range(GDN_HEADS):
            s_ref[dd, h] = s_cur[dd][h]
    sf_ref[...] = s_ref[...]


def _gdn(proj, arow, dtb, s0, fin, *, layer, nseq, n_seq):
    t = proj.shape[0]
    nblk = n_seq // SUB
    rbi = _row_block_index(nblk)
    c = _C
    consts = [jnp.asarray(c["incl"], BF16), jnp.asarray(c["incl_bd"], F32), jnp.asarray(c["strict_bd"], F32),
              jnp.asarray(c["levels"], BF16), jnp.asarray(c["eye256"], F32)]
    hs = (GDN_HEADS, GDN_DK, GDN_DV)
    nck = SUB // CHUNK
    nu = 2 * nck
    zero_init = s0 is None
    s_spec = pl.BlockSpec((2, None) + hs, lambda b, i: (0, b, 0, 0, 0))
    full = lambda shape: pl.BlockSpec(shape, lambda b, i: (0,) * len(shape))
    rows = (lambda b, i: rbi(0, b, i), lambda b, i: rbi(1, b, i))
    args = [proj, proj, proj, proj, arow, dtb, *consts] + ([] if zero_init else [s0])
    return pl.pallas_call(
        functools.partial(_gdn_kernel, nchunk=nck, zero_init=zero_init),
        grid=(nseq, nblk),
        in_specs=[pl.BlockSpec((SUB, 3 * GDN_W), lambda b, i: (rows[0](b, i), P_DQKV // (3 * GDN_W))),
                  pl.BlockSpec((SUB, 3 * GDN_W), lambda b, i: (rows[1](b, i), P_DQKV // (3 * GDN_W))),
                  pl.BlockSpec((SUB, 128), lambda b, i: (rows[0](b, i), P_SMALL // 128)),
                  pl.BlockSpec((SUB, 128), lambda b, i: (rows[1](b, i), P_SMALL // 128 + 1)),
                  full((2, 1, 128)), full((2, 1, 128)), full((2, CHUNK, CHUNK)),
                  full((2, 256, 256)), full((2, 256, 256)), full((2, 6, 256, 256)), full((256, 256))]
                 + ([] if zero_init else [s_spec]) + [pl.BlockSpec(memory_space=pl.ANY)],
        out_specs=[pl.BlockSpec((SUB, GDN_W), lambda b, i: (rows[0](b, i), 0)),
                   pl.BlockSpec((SUB, GDN_W), lambda b, i: (rows[1](b, i), 0)), _fin_spec(hs, layer)],
        out_shape=[jax.ShapeDtypeStruct((t, GDN_W), BF16), jax.ShapeDtypeStruct((t, GDN_W), BF16),
                   jax.ShapeDtypeStruct((nseq, DEPTH, 2) + hs, F32)],
        input_output_aliases={len(args): 2},
        scratch_shapes=[pltpu.VMEM((2,) + hs, F32),
                        pltpu.VMEM((2, SUB, 128), F32), pltpu.VMEM((2, SUB, 128), F32),
                        pltpu.VMEM((nu, 256, GDN_DK), BF16), pltpu.VMEM((nu, 256, GDN_DV), F32),
                        pltpu.VMEM((nu, 256, 256), BF16), pltpu.VMEM((nu, 256, GDN_DK), BF16),
                        pltpu.VMEM((nu * GDN_HEADS, GDN_DK, CHUNK), BF16), pltpu.VMEM((nu, 8, 128), F32)],
        compiler_params=_cparams(("arbitrary", "arbitrary")),
        name="gdn",
    )(*args, fin)


def _cmul(ar, ai, br, bi):
    return ar * br - ai * bi, ar * bi + ai * br


def _split2(x):
    hi = x.astype(BF16)
    return hi, (x - hi.astype(F32)).astype(BF16)


def _s5_consts_kernel(lr_ref, li_ref, st_ref, br_ref, bi_ref, cr_ref, ci_ref, gm_ref, bm_ref,
                      laml_ref, wexp_ref, vexp_ref, tt_ref, w_ref, v_ref, a_ref, q_ref):
    d = pl.program_id(0) % 2
    lc = S5_L * S5_CH

    def trows(s):
        return pl.ds(pl.multiple_of((s + d * (S5_L - 1 - 2 * s)) * S5_CH, S5_CH), S5_CH)

    lr = lr_ref[...]
    li = li_ref[...]
    step = jnp.exp(st_ref[...])
    mag = jnp.exp(lr * step)
    lbr = mag * jnp.cos(li * step)
    lbi = mag * jnp.sin(li * step)
    nr = lbr - 1.0
    den = lr * lr + li * li
    fr = (nr * lr + lbi * li) / den
    fi = (lbi * lr - nr * li) / den
    bbr, bbi = _cmul(fr, fi, br_ref[...], bi_ref[...])
    cr = cr_ref[...]
    ci = ci_ref[...]
    pr = jnp.ones_like(lr)
    pi = jnp.zeros_like(lr)
    for s in range(S5_L):
        ar, ai = _cmul(cr, ci, pr, pi)
        a_ref[0, trows(s), :] = ar
        a_ref[1, trows(s), :] = ai
        wr, wi = _cmul(pr, pi, bbr, bbi)
        w_ref[0, trows(S5_L - 1 - s), :] = wr
        w_ref[1, trows(S5_L - 1 - s), :] = wi
        pr, pi = _cmul(pr, pi, lbr, lbi)
        vr, vi = _cmul(cr, ci, pr, pi)
        v_ref[0, trows(s), :] = vr
        v_ref[1, trows(s), :] = -vi
    laml_ref[0:1, :] = pr
    laml_ref[1:2, :] = pi

    lane = lax.broadcasted_iota(jnp.int32, (1, 128), 1)
    for gp in range(S5_PAIRS):
        lanes = slice(gp * 128, (gp + 1) * 128)
        for r in range(2):
            for src, dst in ((w_ref, wexp_ref), (v_ref, vexp_ref)):
                slab = src[r, :, lanes]
                dst[r, gp] = jnp.concatenate([jnp.where(lane < S5_P, slab, 0.0),
                                              jnp.where(lane >= S5_P, slab, 0.0)], axis=0).astype(BF16)

    gm = gm_ref[...]
    kall = None
    for r, b, sign in ((0, bbr, 1.0), (1, bbi, -1.0)):
        ah, al = _split2(a_ref[r])
        bh, bl = _split2(jnp.concatenate([b] * S5_GROUPS, axis=0) * gm)
        nt = lambda x, y: lax.dot_general(x, y, (((1,), (1,)), ((), ())), preferred_element_type=F32)
        term = (nt(ah, bh) + nt(ah, bl) + nt(al, bh)) * sign
        kall = term if kall is None else kall + term

    zeros = jnp.zeros((lc, S5_W), BF16)
    kall = kall.astype(BF16)
    for delta in range(S5_GROUPS):
        q_ref[delta, 0:lc, :] = zeros
        q_ref[delta, lc:2 * lc, :] = kall if delta == 0 else pltpu.roll(kall, delta * S5_CH, axis=1)
        q_ref[delta, 2 * lc:3 * lc, :] = zeros
    for g in range(S5_GROUPS):
        acc = zeros
        for s in range(S5_L):
            start = pl.multiple_of(lc - S5_CH * s + d * (S5_CH * (S5_L - 1)), S5_CH)
            blk = q_ref[(s - g) % S5_GROUPS, pl.ds(start, lc), :]
            acc = jnp.where(bm_ref[s:s + 1, :] > 0.5, blk, acc)
        tt_ref[g] = acc


def _s5_consts(lam_re, lam_im, log_step, b_re, b_im, c_re, c_im):
    n = DEPTH * 2
    lc = S5_L * S5_CH
    lr = lam_re.reshape(n, 1, S5_N)
    li = lam_im.reshape(n, 1, S5_N)
    st = jnp.repeat(log_step.reshape(n, S5_GROUPS), S5_P, axis=-1).reshape(n, 1, S5_N)
    br = b_re.reshape(n, S5_N, S5_CH).transpose(0, 2, 1)
    bi = b_im.reshape(n, S5_N, S5_CH).transpose(0, 2, 1)
    cr = c_re.reshape(n, S5_GROUPS, S5_CH, S5_P).transpose(0, 2, 1, 3).reshape(n, S5_CH, S5_N)
    ci = c_im.reshape(n, S5_GROUPS, S5_CH, S5_P).transpose(0, 2, 1, 3).reshape(n, S5_CH, S5_N)
    gm = jnp.asarray(np.kron(np.eye(S5_GROUPS), np.ones((S5_CH, S5_P))), F32)
    row = pl.BlockSpec((None, 1, S5_N), lambda i: (i, 0, 0))
    mat = pl.BlockSpec((None, S5_CH, S5_N), lambda i: (i, 0, 0))
    exp = pl.BlockSpec((None, 2, S5_PAIRS, 2 * lc, 128), lambda i: (i, 0, 0, 0, 0))
    laml, wexp, vexp, tt = pl.pallas_call(
        _s5_consts_kernel,
        grid=(n,),
        in_specs=[row, row, row, mat, mat, mat, mat, pl.BlockSpec((S5_W, S5_N), lambda i: (0, 0)),
                  pl.BlockSpec((S5_GROUPS, S5_W), lambda i: (0, 0))],
        out_specs=[pl.BlockSpec((None, 2, S5_N), lambda i: (i, 0, 0)), exp, exp,
                   pl.BlockSpec((None, S5_GROUPS, lc, lc), lambda i: (i, 0, 0, 0))],
        out_shape=[jax.ShapeDtypeStruct((n, 2, S5_N), F32),
                   jax.ShapeDtypeStruct((n, 2, S5_PAIRS, 2 * lc, 128), BF16),
                   jax.ShapeDtypeStruct((n, 2, S5_PAIRS, 2 * lc, 128), BF16),
                   jax.ShapeDtypeStruct((n, S5_GROUPS, lc, lc), BF16)],
        scratch_shapes=[pltpu.VMEM((2, lc, S5_N), F32), pltpu.VMEM((2, lc, S5_N), F32),
                        pltpu.VMEM((2, lc, S5_N), F32), pltpu.VMEM((S5_GROUPS, 3 * lc, S5_W), BF16)],
        compiler_params=_cparams(("arbitrary",)),
        name="s5_consts",
    )(lr, li, st, br, bi, cr, ci, gm, _lane_block_masks())
    return laml.reshape(n, 2, S5_PAIRS, 128), wexp, vexp, tt


def _lane_block_masks():
    return jnp.asarray(np.kron(np.eye(S5_GROUPS), np.ones((1, S5_CH))), F32)


def _nt(a, b):
    return lax.dot_general(a, b, (((1,), (1,)), ((), ())), preferred_element_type=F32)


def _lane_block_transpose(arrs):
    n = len(arrs)
    width = arrs[0].shape[1]
    blk = lax.broadcasted_iota(jnp.int32, (1, width), 1) // (width // n)
    arrs = list(arrs)
    h = n // 2
    while h >= 1:
        hi = (blk & h) != 0
        for a in range(n):
            if a & h:
                continue
            lo_arr, hi_arr = arrs[a], arrs[a + h]
            arrs[a] = jnp.where(hi, pltpu.roll(hi_arr, h * (width // n), axis=1), lo_arr)
            arrs[a + h] = jnp.where(hi, hi_arr, pltpu.roll(lo_arr, width - h * (width // n), axis=1))
        h //= 2
    return arrs


def _s5_kernel(u0_ref, u1_ref, tt_ref, wexp_ref, vexp_ref, laml_ref, *rest, nseq, nblk, zero_init):
    if zero_init:
        y_ref, hf_ref, ug_sc, yg_sc, hin_re, hin_im, hp_re, hp_im = rest
    else:
        h0_ref, y_ref, hf_ref, ug_sc, yg_sc, hin_re, hin_im, hp_re, hp_im = rest
    d = pl.program_id(0)
    r = nblk * nseq
    lc = S5_L * S5_CH

    @pl.when(d == 0)
    def _():
        xs = []
        for s in range(S5_L):
            rows = pl.ds(s, r, stride=S5_L)
            xs.append(jnp.concatenate([u0_ref[rows, :], u1_ref[rows, :]], axis=1).astype(BF16))
        for g, ug in enumerate(_lane_block_transpose(xs)):
            ug_sc[g] = ug

    for gp in range(S5_PAIRS):
        up = jnp.concatenate([ug_sc[2 * gp], ug_sc[2 * gp + 1]], axis=1)
        rows = pl.ds(gp, r, stride=S5_PAIRS)
        hin_re[rows, :] = jnp.dot(up, wexp_ref[0, gp], preferred_element_type=F32)
        hin_im[rows, :] = jnp.dot(up, wexp_ref[1, gp], preferred_element_type=F32)

    lr = laml_ref[0]
    li = laml_ref[1]

    def step(kk, carry):
        k = kk + d * (nblk - 1 - 2 * kk)
        new = []
        for b in range(nseq):
            rows = pl.ds(pl.multiple_of((b * nblk + k) * S5_PAIRS, S5_PAIRS), S5_PAIRS)
            hr, hi = carry[2 * b], carry[2 * b + 1]
            hp_re[rows, :] = hr
            hp_im[rows, :] = hi
            new += [lr * hr - li * hi + hin_re[rows, :], lr * hi + li * hr + hin_im[rows, :]]
        return tuple(new)

    init = []
    for b in range(nseq):
        rows = slice(b * S5_PAIRS, (b + 1) * S5_PAIRS)
        init += ([jnp.zeros((S5_PAIRS, 128), F32)] * 2 if zero_init else [h0_ref[0, rows, :], h0_ref[1, rows, :]])
    fin = lax.fori_loop(0, nblk, step, tuple(init))
    for b in range(nseq):
        rows = slice(b * S5_PAIRS, (b + 1) * S5_PAIRS)
        hf_ref[0, rows, :] = fin[2 * b]
        hf_ref[1, rows, :] = fin[2 * b + 1]

    for gp in range(S5_PAIRS):
        rows = pl.ds(gp, r, stride=S5_PAIRS)
        ys = (_nt(hp_re[rows, :].astype(BF16), vexp_ref[0, gp])
              + _nt(hp_im[rows, :].astype(BF16), vexp_ref[1, gp]))
        for j in range(2):
            g = 2 * gp + j
            yg = _nt(ug_sc[g], tt_ref[g]) + ys[:, j * lc:(j + 1) * lc]

            @pl.when(d == 0)
            def _():
                yg_sc[g] = yg

            @pl.when(d == 1)
            def _():
                yg_sc[g] += yg

    @pl.when(d == 1)
    def _():
        for t, yt in enumerate(_lane_block_transpose([yg_sc[g] for g in range(S5_GROUPS)])):
            rows = pl.ds(t, r, stride=S5_L)
            y_ref[0, rows, :] = yt[:, 0:128]
            y_ref[1, rows, :] = yt[:, 128:2 * 128]


def _s5(proj, consts, h0, *, layer, nseq, n_seq):
    laml, wexp, vexp, tt = consts
    t = proj.shape[0]
    nblk = n_seq // S5_L
    r = nblk * nseq
    lc = S5_L * S5_CH
    bs = nseq * S5_PAIRS
    zero_init = h0 is None
    h_spec = pl.BlockSpec((None, 2, bs, 128), lambda d: (d, 0, 0, 0))
    return pl.pallas_call(
        functools.partial(_s5_kernel, nseq=nseq, nblk=nblk, zero_init=zero_init),
        grid=(2,),
        in_specs=[pl.BlockSpec((t, 128), lambda d: (0, P_SU // 128)),
                  pl.BlockSpec((t, 128), lambda d: (0, P_SU // 128 + 1)),
                  pl.BlockSpec((None, S5_GROUPS, lc, lc), lambda d: (2 * layer + d, 0, 0, 0)),
                  pl.BlockSpec((None, 2, S5_PAIRS, 2 * lc, 128), lambda d: (2 * layer + d, 0, 0, 0, 0)),
                  pl.BlockSpec((None, 2, S5_PAIRS, 2 * lc, 128), lambda d: (2 * layer + d, 0, 0, 0, 0)),
                  pl.BlockSpec((None, 2, S5_PAIRS, 128), lambda d: (2 * layer + d, 0, 0, 0))]
                 + ([] if zero_init else [h_spec]),
        out_specs=[pl.BlockSpec((2, t, 128), lambda d: (0, 0, 0)), h_spec],
        out_shape=[jax.ShapeDtypeStruct((2, t, 128), F32),
                   jax.ShapeDtypeStruct((2, 2, bs, 128), F32)],
        scratch_shapes=[pltpu.VMEM((S5_GROUPS, r, lc), BF16), pltpu.VMEM((S5_GROUPS, r, lc), F32)]
                       + [pltpu.VMEM((r * S5_PAIRS, 128), F32) for _ in range(4)],
        compiler_params=_cparams(("arbitrary",)),
        name="s5",
    )(proj, proj, tt, wexp, vexp, laml, *([] if zero_init else [h0]))


def _out_kernel(x_ref, mod_ref, ogf_ref, ogb_ref, odf_ref, odb_ref, ysf_ref, ysb_ref, gog_ref, su_ref, dog_ref,
                ind_ref, gw_ref, dw_ref, sd_ref, wglu_ref, bglu_ref, wout32_ref, o_ref, wout_ref):
    @pl.when(pl.program_id(0) == 0)
    def _():
        wout_ref[...] = wout32_ref[...].astype(BF16)

    og = ogf_ref[...].astype(F32) + ogb_ref[...].astype(F32)
    hi, mid, _ = _split3(og * og)
    ms = jnp.dot(hi, ind_ref[...], preferred_element_type=F32) + jnp.dot(mid, ind_ref[...], preferred_element_type=F32)
    og = og * lax.rsqrt(ms + EPS) * gw_ref[...] * _silu(gog_ref[...])
    od = odf_ref[...].astype(F32) + odb_ref[...].astype(F32)
    dog = dog_ref[...]
    parts = []
    for h in range(GDN_HEADS):
        z = od[:, h * GDN_DV:(h + 1) * GDN_DV]
        z = z * lax.rsqrt(jnp.mean(z * z, axis=-1, keepdims=True) + EPS) * dw_ref[...]
        parts.append(z * _silu(dog[:, h * GDN_DV:(h + 1) * GDN_DV]))
    y = su_ref[...] * sd_ref[...] + jnp.concatenate([ysf_ref[...], ysb_ref[...]], axis=1)
    g = 0.5 * y * (1.0 + jnp.tanh(math.sqrt(2.0 / math.pi) * (y + 0.044715 * (y * y * y))))
    os_ = g * jax.nn.sigmoid(_bdot(g, wglu_ref[...]) + bglu_ref[...])
    out = _bdot(og, wout_ref[0:GLA_W, :]) + _bdot(os_, wout_ref[GLA_W + GDN_W:MIX_W, :])
    for h in range(GDN_HEADS):
        lo = GLA_W + h * GDN_DV
        out = out + _bdot(parts[h], wout_ref[lo:lo + GDN_DV, :])
    o_ref[...] = x_ref[...] + mod_ref[5:6, :] * out


def _mix_out(x, mods, proj, og, od, ys, gla_nw, gdn_nw, s5_d, w_glu, b_glu, w_out, *, layer, n_seq):
    t, d = x.shape
    nb = mods.shape[0]
    tm = _pick_tile(n_seq if nb > 1 else t, MIX_TM)
    midx = _mod_index(nb, tm, n_seq)
    ys2 = ys

    def dirspec(w, dd):
        return pl.BlockSpec((None, tm, w), lambda i: (dd, i, 0))

    def vec(w):
        return pl.BlockSpec((None, 1, w), lambda i: (layer, 0, 0))

    return pl.pallas_call(
        _out_kernel,
        grid=(t // tm,),
        in_specs=[pl.BlockSpec((tm, d), lambda i: (i, 0)),
                  pl.BlockSpec((None, N_MOD, d), lambda i: (midx(i), 0, 0)),
                  pl.BlockSpec((tm, GLA_W), lambda i: (i, 0)), pl.BlockSpec((tm, GLA_W), lambda i: (i, 0)),
                  pl.BlockSpec((tm, GDN_W), lambda i: (i, 0)), pl.BlockSpec((tm, GDN_W), lambda i: (i, 0)),
                  dirspec(128, 0), dirspec(128, 1),
                  pl.BlockSpec((tm, GLA_W), lambda i: (i, P_GOG // GLA_W)),
                  pl.BlockSpec((tm, S5_W), lambda i: (i, P_SU // S5_W)),
                  pl.BlockSpec((tm, GDN_W), lambda i: (i, P_DOG // GDN_W)),
                  pl.BlockSpec((GLA_W, GLA_W), lambda i: (0, 0)),
                  vec(GLA_W), vec(GDN_DV), vec(S5_W),
                  pl.BlockSpec((None, S5_W, S5_W), lambda i: (layer, 0, 0)),
                  vec(S5_W),
                  pl.BlockSpec((None, MIX_W, d), lambda i: (layer, 0, 0), pipeline_mode=pl.Buffered(1))],
        out_specs=pl.BlockSpec((tm, d), lambda i: (i, 0)),
        out_shape=jax.ShapeDtypeStruct((t, d), F32),
        scratch_shapes=[pltpu.VMEM((MIX_W, d), BF16)],
        compiler_params=_cparams(("arbitrary",)),
        name="mix_out",
    )(x, mods, og[0], og[1], od[0], od[1], ys2, ys2, proj, proj, proj,
      jnp.asarray(_C["ind64"], BF16),
      jnp.tile(gla_nw, (1, GLA_HEADS)).reshape(DEPTH, 1, GLA_W), gdn_nw.reshape(DEPTH, 1, GDN_DV),
      s5_d.reshape(DEPTH, 1, S5_W), w_glu, b_glu.reshape(DEPTH, 1, S5_W), w_out)


def _gla_params(gk_up, gk_bias, layer):
    up = jnp.zeros((2, 128, GLA_KW), F32).at[:, :GLA_RANK, :].set(gk_up[layer])
    return up, gk_bias[layer].reshape(2, 1, GLA_KW)


def _gdn_params(a_log, dt_bias, layer):
    arow = jnp.zeros((2, 1, 128), F32).at[:, 0, SM_A:SM_A + GDN_HEADS].set(a_log[layer])
    dtb = jnp.zeros((2, 1, 128), F32).at[:, 0, SM_A:SM_A + GDN_HEADS].set(dt_bias[layer])
    return arow, dtb


def _gla_state_in(s):
    eye = jnp.eye(GLA_HEADS, dtype=F32)
    st = jnp.einsum('bzhde,hg->zbhegd', s.astype(F32), eye)
    return st.reshape(2, s.shape[0], GLA_W, GLA_KW)


def _trunk(x3, mods, rows, states, p, s5c, w_in):
    nseq, n_seq, d = x3.shape
    x = x3.reshape(nseq * n_seq, d)
    fin = []
    gla_f = jnp.zeros((nseq, DEPTH, 2, GLA_HEADS, GLA_DK, GLA_DV), F32)
    gdn_f = jnp.zeros((nseq, DEPTH, 2, GDN_HEADS, GDN_DK, GDN_DV), F32)
    for l in range(DEPTH):
        m = mods[l]
        x = _ffn(x, m, p['norm_w'], p['ffn_w_gate'], p['ffn_w_up'], p['ffn_w_down'], p['final_norm_w'],
                 layer=l, which=0, n_seq=n_seq, final=False)
        proj = _proj(x, m, p['norm_w'], w_in, p['gdn_conv_w'], layer=l, n_seq=n_seq, rowlen=n_seq // rows)
        if states is None:
            st0 = s0 = h0 = None
        else:
            st_gla, st_gdn, st_re, st_im = states
            st0 = _gla_state_in(st_gla[:, l])
            s0 = st_gdn[:, l].transpose(1, 0, 2, 3, 4)
            h0 = jnp.stack([st_re[:, l].reshape(nseq, 2, S5_N), st_im[:, l].reshape(nseq, 2, S5_N)], axis=0)
            h0 = h0.transpose(2, 0, 1, 3).reshape(2, 2, nseq * S5_PAIRS, 128)
        up_p, bias_p = _gla_params(p['gla_gk_up'], p['gla_gk_bias'], l)
        og_f, og_b, gla_f = _gla(proj, up_p, bias_p, st0, gla_f, layer=l, nseq=nseq, n_seq=n_seq)
        og = (og_f, og_b)
        arow, dtb = _gdn_params(p['gdn_a_log'], p['gdn_dt_bias'], l)
        od_f, od_b, gdn_f = _gdn(proj, arow, dtb, s0, gdn_f, layer=l, nseq=nseq, n_seq=n_seq)
        od = (od_f, od_b)
        ys, s5_f = _s5(proj, s5c, h0, layer=l, nseq=nseq, n_seq=n_seq)
        x = _mix_out(x, m, proj, og, od, ys, p['gla_norm_w'], p['gdn_norm_w'], p['s5_d'], p['s5_w_glu'],
                     p['s5_b_glu'], p['w_out'], layer=l, n_seq=n_seq)
        x = _ffn(x, m, p['norm_w'], p['ffn_w_gate'], p['ffn_w_up'], p['ffn_w_down'], p['final_norm_w'],
                 layer=l, which=1, n_seq=n_seq, final=(l == DEPTH - 1))
        fin.append((gla_f, gdn_f, s5_f))
    return x.reshape(nseq, n_seq, d), fin


def kernel(x_prompt, x_sample, c, state_gla, state_gdn, state_s5_re, state_s5_im, c_ctx, w_ada, b_ada, norm_w, ffn_w_gate, ffn_w_up, ffn_w_down, w_in, gla_gk_up, gla_gk_bias, gla_norm_w, gdn_conv_w, gdn_a_log, gdn_dt_bias, gdn_norm_w, s5_lam_re, s5_lam_im, s5_log_step, s5_b_re, s5_b_im, s5_c_re, s5_c_im, s5_d, s5_w_glu, s5_b_glu, w_out, final_norm_w):
    p = dict(norm_w=norm_w, ffn_w_gate=ffn_w_gate, ffn_w_up=ffn_w_up, ffn_w_down=ffn_w_down,
             gla_gk_up=gla_gk_up, gla_gk_bias=gla_gk_bias, gla_norm_w=gla_norm_w, gdn_conv_w=gdn_conv_w,
             gdn_a_log=gdn_a_log, gdn_dt_bias=gdn_dt_bias, gdn_norm_w=gdn_norm_w, s5_c_re=s5_c_re,
             s5_c_im=s5_c_im, s5_d=s5_d, s5_w_glu=s5_w_glu, s5_b_glu=s5_b_glu, w_out=w_out,
             final_norm_w=final_norm_w)
    b_ctx = x_prompt.shape[0]
    b_dec = x_sample.shape[0]
    cond8 = jnp.zeros((8, D_MODEL), F32).at[0].set(c_ctx).at[1:1 + b_dec].set(c)
    mods = _ada(cond8, w_ada, b_ada)
    mods_ctx = mods[:, 0:1].reshape(DEPTH, 1, N_MOD, D_MODEL)
    mods_dec = mods[:, 1:1 + b_dec].reshape(DEPTH, b_dec, N_MOD, D_MODEL)
    s5c = _s5_consts(s5_lam_re, s5_lam_im, s5_log_step, s5_b_re, s5_b_im, s5_c_re, s5_c_im)

    y_prompt, fin = _trunk(x_prompt, mods_ctx, 1, None, p, s5c, w_in)
    new_gla, new_gdn = fin[-1][0], fin[-1][1]
    s5f = jnp.stack([f[2].reshape(2, 2, b_ctx, S5_GROUPS, S5_P).transpose(1, 2, 0, 3, 4) for f in fin], axis=2)
    new_re, new_im = s5f[0], s5f[1]

    rows = x_sample.shape[1] // 64
    y_sample, _ = _trunk(x_sample, mods_dec, rows, (state_gla, state_gdn, state_s5_re, state_s5_im),
                         p, s5c, w_in)
    return (y_prompt, y_sample, new_gla.astype(state_gla.dtype), new_gdn.astype(state_gdn.dtype),
            new_re.astype(state_s5_re.dtype), new_im.astype(state_s5_im.dtype))
```

```python
import functools
import math

import numpy as np
import jax
import jax.numpy as jnp
from jax import lax
from jax.experimental import pallas as pl
from jax.experimental.pallas import tpu as pltpu

F32 = jnp.float32
BF16 = jnp.bfloat16

D_MODEL = 1024
DEPTH = 2
FFN_DIM = 2816
N_MOD = 9
EPS = 1e-6
CHUNK = 64
GLA_HEADS, GLA_DK, GLA_DV, GLA_RANK = 4, 32, 64, 16
GLA_GATE_NORM = 16.0
GDN_HEADS, GDN_DK, GDN_DV, GDN_CONV = 4, 128, 128, 5
S5_GROUPS, S5_CH, S5_P = 16, 16, 64
GLA_W = GLA_HEADS * GLA_DV
GLA_KW = GLA_HEADS * GLA_DK
GDN_W = GDN_HEADS * GDN_DV
S5_W = S5_GROUPS * S5_CH
S5_N = S5_GROUPS * S5_P
S5_L = 16
S5_PAIRS = S5_GROUPS // 2
MIX_W = GLA_W + GDN_W + S5_W

_IN_SIZES = (GLA_KW, GLA_KW, GLA_W, GLA_RANK, GLA_RANK, GLA_W,
             GDN_W, GDN_W, GDN_W, GDN_HEADS, GDN_HEADS, GDN_HEADS, GDN_HEADS, GDN_W, S5_W)
_IN_OFF = tuple(int(v) for v in np.cumsum((0,) + _IN_SIZES))
(_GQ, _GK, _GV, _GLRF, _GLRB, _GOG, _DQ, _DK, _DV, _DAF, _DAB, _DBF, _DBB, _DOG, _SU) = range(15)

P_DQKV = 0
P_SMALL = 1536
P_GQK = 1792
P_GV = 2048
P_GOG = 2304
P_DOG = 2560
P_SU = 3072
PROJ_W = 3328
PROJ_TN = PROJ_W // 2
SM_A = GLA_RANK
SM_B = GLA_RANK + GDN_HEADS

SUB = 256
GLA_SUB = 256
FFN_TM = 2048
FFN_TF = 256
PROJ_TM = 1024
MIX_TM = 512
ADA_TN = 1024
V7X_VMEM_BYTES = 64 * 1024 * 1024
VMEM_LIMIT = V7X_VMEM_BYTES - 8 * 1024 * 1024


def _cparams(sem):
    return pltpu.CompilerParams(dimension_semantics=sem, vmem_limit_bytes=VMEM_LIMIT)


def _bdot(a, b):
    return jnp.dot(a.astype(BF16), b.astype(BF16), preferred_element_type=F32)


def _bdot_nt(a, b):
    return lax.dot_general(a.astype(BF16), b.astype(BF16), (((1,), (1,)), ((), ())),
                           preferred_element_type=F32)


def _split3(x):
    hi = x.astype(BF16)
    r1 = x - hi.astype(F32)
    mid = r1.astype(BF16)
    lo = (r1 - mid.astype(F32)).astype(BF16)
    return hi, mid, lo


def _sel_dot(m01, x):
    hi, mid, lo = _split3(x)
    return (jnp.dot(m01, hi, preferred_element_type=F32) + jnp.dot(m01, mid, preferred_element_type=F32)
            + jnp.dot(m01, lo, preferred_element_type=F32))


def _silu(x):
    return x * jax.nn.sigmoid(x)


def _softplus(x):
    return jnp.maximum(x, 0.0) + jnp.log1p(jnp.exp(-jnp.abs(x)))


def _log_sigmoid(x):
    return -_softplus(-x)


def _norm_mod(x, nw, scale, shift):
    y = x * lax.rsqrt(jnp.mean(x * x, axis=-1, keepdims=True) + EPS)
    return y * nw * (1.0 + scale) + shift


def _pick_tile(n, cap):
    t = cap
    while n % t:
        t //= 2
    return t


def _ada_kernel(c_ref, w_ref, b_ref, o_ref):
    o_ref[...] = _bdot(_silu(c_ref[...]), w_ref[...]) + b_ref[...]


def _ada(cond8, w_ada, b_ada):
    depth, d, nout = w_ada.shape
    tn = ADA_TN
    return pl.pallas_call(
        _ada_kernel,
        grid=(depth, nout // tn),
        in_specs=[pl.BlockSpec((8, d), lambda l, j: (0, 0)),
                  pl.BlockSpec((None, d, tn), lambda l, j: (l, 0, j)),
                  pl.BlockSpec((None, 1, tn), lambda l, j: (l, 0, j))],
        out_specs=pl.BlockSpec((None, 8, tn), lambda l, j: (l, 0, j)),
        out_shape=jax.ShapeDtypeStruct((depth, 8, nout), F32),
        compiler_params=_cparams(("parallel", "parallel")),
        name="ada",
    )(cond8, w_ada, b_ada.reshape(depth, 1, nout))


def _ffn_kernel(x_ref, mod_ref, nw_ref, wg_ref, wu_ref, wd_ref, fw_ref, o_ref, h_ref, *, mod_base, final):
    j = pl.program_id(1)

    @pl.when(j == 0)
    def _():
        m = mod_ref[...]
        h = _norm_mod(x_ref[...], nw_ref[...], m[mod_base + 1:mod_base + 2], m[mod_base:mod_base + 1])
        h_ref[...] = h.astype(BF16)
        o_ref[...] = jnp.zeros_like(o_ref)

    h = h_ref[...]
    g = jnp.dot(h, wg_ref[...].astype(BF16), preferred_element_type=F32)
    u = jnp.dot(h, wu_ref[...].astype(BF16), preferred_element_type=F32)
    o_ref[...] += _bdot(_silu(g) * u, wd_ref[...])

    @pl.when(j == pl.num_programs(1) - 1)
    def _():
        m = mod_ref[...]
        y = x_ref[...] + 0.5 * m[mod_base + 2:mod_base + 3] * o_ref[...]
        if final:
            y = y * lax.rsqrt(jnp.mean(y * y, axis=-1, keepdims=True) + EPS) * fw_ref[...]
        o_ref[...] = y


def _mod_index(nb, tm, n_seq):
    if nb == 1:
        return lambda i: 0
    return lambda i: (i * tm) // n_seq


def _ffn(x, mods, norm_w, w_gate, w_up, w_down, final_w, *, layer, which, n_seq, final):
    t, d = x.shape
    f = w_gate.shape[-1]
    nb = mods.shape[0]
    tm = _pick_tile(n_seq if nb > 1 else t, FFN_TM)
    tf = FFN_TF
    midx = _mod_index(nb, tm, n_seq)
    mod_base = 0 if which == 0 else 6
    return pl.pallas_call(
        functools.partial(_ffn_kernel, mod_base=mod_base, final=final),
        grid=(t // tm, f // tf),
        in_specs=[pl.BlockSpec((tm, d), lambda i, j: (i, 0)),
                  pl.BlockSpec((None, N_MOD, d), lambda i, j: (midx(i), 0, 0)),
                  pl.BlockSpec((None, None, 1, d), lambda i, j: (layer, 2 * which, 0, 0)),
                  pl.BlockSpec((None, None, d, tf), lambda i, j: (layer, which, 0, j)),
                  pl.BlockSpec((None, None, d, tf), lambda i, j: (layer, which, 0, j)),
                  pl.BlockSpec((None, None, tf, d), lambda i, j: (layer, which, j, 0)),
                  pl.BlockSpec((1, d), lambda i, j: (0, 0))],
        out_specs=pl.BlockSpec((tm, d), lambda i, j: (i, 0)),
        out_shape=jax.ShapeDtypeStruct((t, d), F32),
        scratch_shapes=[pltpu.VMEM((tm, d), BF16)],
        compiler_params=_cparams(("parallel", "arbitrary")),
        name="ffn",
    )(x, mods, norm_w.reshape(DEPTH, 3, 1, d), w_gate, w_up, w_down, final_w.reshape(1, d))


def _proj_kernel(x_ref, mod_ref, nw_ref, w_ref, cw_ref, o_ref, h_ref, wb_ref, st_ref, *, rowlen):
    j = pl.program_id(0)
    i = pl.program_id(1)

    for jt, segs in enumerate(_W_SEGS):
        @pl.when(jnp.logical_and(i == 0, j == jt))
        def _():
            for lo, src, width in segs:
                if width % 128 == 0:
                    wb_ref[lo:lo + width, :] = w_ref[src:src + width, :].astype(BF16)
                else:
                    base = lo // 128 * 128
                    if src is None:
                        st_ref[lo - base:lo - base + width, :] = jnp.zeros((width, st_ref.shape[1]), F32)
                        wb_ref[base:base + 128, :] = st_ref[...].astype(BF16)
                    else:
                        st_ref[lo - base:lo - base + width, :] = w_ref[src:src + width, :]

    @pl.when(j == 0)
    def _():
        m = mod_ref[...]
        h_ref[i] = _norm_mod(x_ref[...], nw_ref[...], m[4:5], m[3:4]).astype(BF16)
        o_ref[...] = _nt(h_ref[i], wb_ref[...])
        tm = o_ref.shape[0]
        pos = lax.broadcasted_iota(jnp.int32, (tm, 1), 0) % rowlen
        valid = [jnp.logical_and(pos + (t - GDN_CONV // 2) >= 0, pos + (t - GDN_CONV // 2) < rowlen)
                 for t in range(GDN_CONV)]
        for g in range(3 * GDN_HEADS):
            cols = slice(P_DQKV + g * GDN_DK, P_DQKV + (g + 1) * GDN_DK)
            xg = o_ref[:, cols]
            acc = xg * cw_ref[GDN_CONV // 2:GDN_CONV // 2 + 1, cols]
            for t in range(GDN_CONV):
                s = t - GDN_CONV // 2
                if s != 0:
                    xs = pltpu.roll(xg, (-s) % tm, axis=0)
                    acc = acc + jnp.where(valid[t], xs, 0.0) * cw_ref[t:t + 1, cols]
            y = _silu(acc)
            if g < 2 * GDN_HEADS:
                y = y * lax.rsqrt(jnp.sum(y * y, axis=-1, keepdims=True) + EPS)
                if g < GDN_HEADS:
                    y = y * (GDN_DK ** -0.5)
            o_ref[:, cols] = y

    @pl.when(j == 1)
    def _():
        o_ref[...] = _nt(h_ref[i], wb_ref[...])


def _proj(x, mods, norm_w, w_in, conv_w, *, layer, n_seq, rowlen):
    t, d = x.shape
    nb = mods.shape[0]
    tm = _pick_tile(n_seq if nb > 1 else t, PROJ_TM)
    midx = _mod_index(nb, tm, n_seq)
    return pl.pallas_call(
        functools.partial(_proj_kernel, rowlen=rowlen),
        grid=(PROJ_W // PROJ_TN, t // tm),
        in_specs=[pl.BlockSpec((tm, d), lambda j, i: (i * (1 - j), 0)),
                  pl.BlockSpec((None, N_MOD, d), lambda j, i: (midx(i), 0, 0)),
                  pl.BlockSpec((None, None, 1, d), lambda j, i: (layer, 1, 0, 0)),
                  pl.BlockSpec((None, w_in.shape[-1], d), lambda j, i: (layer, 0, 0), pipeline_mode=pl.Buffered(1)),
                  pl.BlockSpec((None, GDN_CONV, 3 * GDN_W), lambda j, i: (layer, 0, 0))],
        out_specs=pl.BlockSpec((tm, PROJ_TN), lambda j, i: (i, j)),
        out_shape=jax.ShapeDtypeStruct((t, PROJ_W), F32),
        scratch_shapes=[pltpu.VMEM((t // tm, tm, d), BF16), pltpu.VMEM((PROJ_TN, d), BF16),
                        pltpu.VMEM((128, d), F32)],
        compiler_params=_cparams(("arbitrary", "arbitrary")),
        name="proj",
    )(x, mods, norm_w.reshape(DEPTH, 3, 1, d), jnp.swapaxes(w_in, 1, 2), conv_w)


def _w_in_segments():
    order = [_DQ, _DK, _DV, _GLRF, _DAF, _DBF, None, _GLRB, _DAB, _DBB, None,
             _GQ, _GK, _GV, _GOG, _DOG, _SU]
    sm_pad = 128 - GLA_RANK - 2 * GDN_HEADS
    tiles = [[] for _ in range(PROJ_W // PROJ_TN)]
    dst = 0
    for s in order:
        width = sm_pad if s is None else _IN_SIZES[s]
        src = None if s is None else _IN_OFF[s]
        tile = tiles[dst // PROJ_TN]
        lo = dst % PROJ_TN
        if tile and src is not None and tile[-1][1] is not None and (
                tile[-1][0] + tile[-1][2] == lo and tile[-1][1] + tile[-1][2] == src):
            tile[-1] = (tile[-1][0], tile[-1][1], tile[-1][2] + width)
        else:
            tile.append((lo, src, width))
        dst += width
        assert (dst - 1) // PROJ_TN == (dst - width) // PROJ_TN, "a segment may not straddle column tiles"
    assert dst == PROJ_W
    return tiles


_W_SEGS = _w_in_segments()


def _np_consts():
    t = np.arange(CHUNK)
    lower = (t[:, None] >= t[None, :])
    incl = np.stack([lower, lower.T])
    strict = np.stack([t[:, None] > t[None, :], t[:, None] < t[None, :]])
    eye4 = np.eye(4, dtype=bool)
    blk = np.kron(eye4, np.ones((CHUNK, CHUNK), bool))
    incl_bd = np.stack([np.kron(eye4, incl[d]) for d in range(2)])
    strict_bd = np.stack([np.kron(eye4, strict[d]) for d in range(2)])
    levels = []
    for d in range(2):
        per = []
        for lv in range(6):
            s = 1 << lv
            same = (t[:, None] // (2 * s)) == (t[None, :] // (2 * s))
            hi = (t % (2 * s)) >= s
            m = same & hi[:, None] & (~hi)[None, :]
            if d == 1:
                m = m.T
            per.append(np.kron(eye4, m))
        levels.append(np.stack(per))
    levels = np.stack(levels)
    assert (levels.sum(1) == strict_bd).all() and blk.any()
    hm_k = np.kron(eye4, np.ones((1, GLA_DK)))
    hm_v = np.kron(eye4, np.ones((1, GLA_DV)))
    st_mask = np.kron(eye4, np.ones((GLA_DV, GLA_DK)))
    ind64 = np.kron(eye4, np.full((GLA_DV, GLA_DV), 1.0 / GLA_DV))
    gla_sel, gla_later, gla_earlier, gla_pair = [], [], [], []
    for d in range(2):
        pos = t if d == 0 else CHUNK - 1 - t
        cum = (pos[None, :] <= pos[:, None]).astype(np.float64)
        sel, later, earlier, pair = [cum], [], [], []
        for lv in range(6):
            s = 1 << lv
            ref_pos = (pos // (2 * s)) * (2 * s) + s - 1
            pick = (pos[None, :] == ref_pos[:, None]).astype(np.float64)
            sel.append(pick @ cum)
            is_later = (pos % (2 * s)) >= s
            same = (pos[:, None] // (2 * s)) == (pos[None, :] // (2 * s))
            later.append(np.repeat(np.where(is_later, 0.0, -1e30)[:, None], GLA_KW, axis=1))
            earlier.append(np.repeat(np.where(is_later, -1e30, 0.0)[:, None], GLA_KW, axis=1))
            pair.append(np.tile(same & is_later[:, None] & (~is_later)[None, :], (1, 4)))
        gla_sel.append(np.concatenate(sel, axis=0))
        gla_later.append(np.stack(later))
        gla_earlier.append(np.stack(earlier))
        gla_pair.append(np.stack(pair))
        assert (np.stack(pair).sum(0) == np.tile(pos[:, None] > pos[None, :], (1, 4))).all()
    head_sum = np.kron(eye4, np.ones((GLA_DK, GLA_DV)))
    return dict(incl=incl, incl_bd=incl_bd, strict_bd=strict_bd, levels=levels,
                hm_k=hm_k, hm_v=hm_v, st_mask=st_mask, ind64=ind64, eye256=np.eye(256),
                gla_sel=np.stack(gla_sel), gla_later=np.stack(gla_later), gla_earlier=np.stack(gla_earlier),
                gla_pair=np.stack(gla_pair), head_sum=head_sum)


_C = _np_consts()


def _row_block_index(nblk):
    return lambda d, b, i: b * nblk + i + d * (nblk - 1 - 2 * i)


def _gla_kernel(qkf_ref, qkb_ref, vf_ref, vb_ref, smf_ref, smb_ref, up_ref, bias_ref, sel_ref, later_ref,
                earlier_ref, pair_ref, hmk_ref, hmv_ref, stm_ref, hsum_ref, *rest, nchunk, zero_init):
    rest = list(rest)
    st0_ref = None if zero_init else rest.pop(0)
    rest.pop(0)
    of_ref, ob_ref, stf_ref, st_ref, qd_sc, m_sc, dec_sc, oi_sc = rest
    ins =((qkf_ref, vf_ref, smf_ref, of_ref), (qkb_ref, vb_ref, smb_ref, ob_ref))

    @pl.when(pl.program_id(1) == 0)
    def _():
        st_ref[...] = jnp.zeros_like(st_ref) if zero_init else st0_ref[...]

    hmkb = hmk_ref[...].astype(BF16)
    hmv = hmv_ref[...]
    stm = stm_ref[...]

    units = [(dd, c) for c in range(nchunk) for dd in range(2)]
    st1 = []
    for dd, c in units:
        qk_ref, v_ref, sm_ref, _ = ins[dd]
        rows = pl.ds(c * CHUNK, CHUNK)
        qk = qk_ref[rows, :]
        q = qk[:, 0:GLA_KW] * (GLA_DK ** -0.5)
        k = qk[:, GLA_KW:2 * GLA_KW]
        x = _bdot(sm_ref[rows, :], up_ref[dd]) + bias_ref[dd]
        lg = _log_sigmoid(x) * (1.0 / GLA_GATE_NORM)
        ball = _sel_dot(sel_ref[dd], lg)
        st1.append((q, k, v_ref[rows, :], ball, jnp.sum(lg, axis=0, keepdims=True)))
    prods = []
    for lv in range(6):
        for (dd, c), (q, k, _, ball, _) in zip(units, st1):
            b = ball[0:CHUNK]
            ref = ball[(lv + 1) * CHUNK:(lv + 2) * CHUNK]
            ql = q * jnp.exp((b - ref) + later_ref[dd, lv])
            kl = (k * jnp.exp((ref - b) + earlier_ref[dd, lv])).astype(BF16)
            kexp = jnp.concatenate([kl * hmkb[h:h + 1] for h in range(GLA_HEADS)], axis=0)
            prods.append(_bdot_nt(ql, kexp))
    for i, ((dd, c), (q, k, v, ball, btot)) in enumerate(zip(units, st1)):
        o_ref = ins[dd][3]
        n = dd * nchunk + c
        b = ball[0:CHUNK]
        att = jnp.zeros((CHUNK, GLA_HEADS * CHUNK), F32)
        for lv in range(6):
            att = jnp.where(pair_ref[dd, lv] > 0.5, prods[lv * len(units) + i], att)
        vexp = jnp.concatenate([v * hmv[h:h + 1] for h in range(GLA_HEADS)], axis=0)
        oi_sc[n] = _bdot(att, vexp) + _bdot(q * k, hsum_ref[...]) * v
        qd_sc[n] = (q * jnp.exp(b)).astype(BF16)
        kd = k * jnp.exp(btot - b)
        m_sc[n] = _bdot(v.T, kd) * stm
        dec_sc[n] = jnp.broadcast_to(jnp.exp(btot), (8, GLA_KW))

    for dd in range(2):
        o_ref = ins[dd][3]
        st = st_ref[dd]
        for ci in range(nchunk):
            c = ci if dd == 0 else nchunk - 1 - ci
            n = dd * nchunk + c
            rows = pl.ds(c * CHUNK, CHUNK)
            o_ref[rows, :] = (oi_sc[n] + _nt(qd_sc[n], st.astype(BF16))).astype(BF16)
            st = st * dec_sc[n][0:1] + m_sc[n]
        st_ref[dd] = st
        for h in range(GLA_HEADS):
            stf_ref[dd, h] = st[h * GLA_DV:(h + 1) * GLA_DV, h * GLA_DK:(h + 1) * GLA_DK].T


def _gla(proj, up_p, bias_p, st0, fin, *, layer, nseq, n_seq):
    t = proj.shape[0]
    sub = min(n_seq, GLA_SUB)
    nblk = n_seq // sub
    nck = sub // CHUNK
    rbi = _row_block_index(nblk)
    c = _C
    consts = [jnp.asarray(c["gla_sel"], BF16), jnp.asarray(c["gla_later"], F32), jnp.asarray(c["gla_earlier"], F32),
              jnp.asarray(c["gla_pair"], F32), jnp.asarray(c["hm_k"], F32), jnp.asarray(c["hm_v"], F32),
              jnp.asarray(c["st_mask"], F32), jnp.asarray(c["head_sum"], BF16)]
    st_spec = pl.BlockSpec((2, None, GLA_W, GLA_KW), lambda b, i: (0, b, 0, 0))
    full = lambda a: pl.BlockSpec(a.shape, lambda b, i: (0,) * a.ndim)
    zero_init = st0 is None
    nu = 2 * nck
    rows = (lambda b, i: rbi(0, b, i), lambda b, i: rbi(1, b, i))

    def blk(width, col, dd):
        return pl.BlockSpec((sub, width), lambda b, i: (rows[dd](b, i), col // width))

    hs = (GLA_HEADS, GLA_DK, GLA_DV)
    args = [proj, proj, proj, proj, proj, proj, up_p, bias_p, *consts] + ([] if zero_init else [st0])
    in_specs = ([blk(2 * GLA_KW, P_GQK, 0), blk(2 * GLA_KW, P_GQK, 1), blk(GLA_W, P_GV, 0), blk(GLA_W, P_GV, 1),
                 blk(128, P_SMALL, 0), blk(128, P_SMALL + 128, 1), full(up_p), full(bias_p)]
                + [full(a) for a in consts] + ([] if zero_init else [st_spec]))
    return pl.pallas_call(
        functools.partial(_gla_kernel, nchunk=nck, zero_init=zero_init),
        grid=(nseq, nblk),
        in_specs=in_specs + [pl.BlockSpec(memory_space=pl.ANY)],
        out_specs=[pl.BlockSpec((sub, GLA_W), lambda b, i: (rows[0](b, i), 0)),
                   pl.BlockSpec((sub, GLA_W), lambda b, i: (rows[1](b, i), 0)), _fin_spec(hs, layer)],
        out_shape=[jax.ShapeDtypeStruct((t, GLA_W), BF16), jax.ShapeDtypeStruct((t, GLA_W), BF16),
                   jax.ShapeDtypeStruct((nseq, DEPTH, 2) + hs, F32)],
        scratch_shapes=[pltpu.VMEM((2, GLA_W, GLA_KW), F32), pltpu.VMEM((nu, CHUNK, GLA_KW), BF16),
                        pltpu.VMEM((nu, GLA_W, GLA_KW), F32), pltpu.VMEM((nu, 8, GLA_KW), F32),
                        pltpu.VMEM((nu, CHUNK, GLA_W), F32)],
        input_output_aliases={len(args): 2},
        compiler_params=_cparams(("arbitrary", "arbitrary")),
        name="gla",
    )(*args, fin)


def _fin_spec(hs, layer):
    return pl.BlockSpec((None, None, 2) + hs, lambda b, i: (b, layer, 0) + (0,) * len(hs))


def _gdn_kernel(qkvf_ref, qkvb_ref, smf_ref, smb_ref, arow_ref, dtb_ref, tri_ref, incl_ref, strict_ref, lvl_ref,
                eye_ref, *rest, nchunk, zero_init):
    rest = list(rest)
    s0_ref = None if zero_init else rest.pop(0)
    rest.pop(0)
    of_ref, ob_ref, sf_ref, s_ref, la_ref, be_ref, w_sc, u0_sc, qkd_sc, qe_sc, kdt_sc, egl_sc = rest
    qkv_refs = (qkvf_ref, qkvb_ref)
    o_refs = (of_ref, ob_ref)

    @pl.when(pl.program_id(1) == 0)
    def _():
        s_ref[...] = jnp.zeros_like(s_ref) if zero_init else s0_ref[...]

    for dd, sm_ref in enumerate((smf_ref, smb_ref)):
        sm = sm_ref[...]
        la_ref[dd] = -jnp.exp(arow_ref[dd]) * _softplus(sm + dtb_ref[dd])
        be_ref[dd] = jax.nn.sigmoid(sm)

    eye = eye_ref[...]

    def stack(ref, r0, lo, width):
        return jnp.concatenate([ref[pl.ds(r0, CHUNK), lo + h * width:lo + (h + 1) * width]
                                for h in range(GDN_HEADS)], axis=0)

    def colstack(a, lo):
        return jnp.concatenate([a[:, lo + h:lo + h + 1] for h in range(GDN_HEADS)], axis=0)

    units = [(dd, c) for c in range(nchunk) for dd in range(2)]
    st1 = []
    for dd, c in units:
        r0 = c * CHUNK
        qkv_ref = qkv_refs[dd]
        qs = stack(qkv_ref, r0, 0, GDN_DK)
        ks = stack(qkv_ref, r0, GDN_W, GDN_DK)
        la = la_ref[dd, pl.ds(r0, CHUNK), :]
        g = _sel_dot(tri_ref[dd], la)
        ksb = ks.astype(BF16)
        st1.append((qs, ks, la, g, _bdot_nt(ksb, ksb), _bdot_nt(qs, ksb)))
    loc = []
    for (dd, c), (qs, ks, la, g, kk, qk) in zip(units, st1):
        r0 = c * CHUNK
        n = dd * nchunk + c
        vs = stack(qkv_refs[dd], r0, 2 * GDN_W, GDN_DV)
        gtot = jnp.sum(la, axis=0, keepdims=True)
        gcol = colstack(g, SM_A)
        kfcol = colstack(jnp.exp(gtot - g), SM_A)
        egcol = colstack(jnp.exp(g), SM_A)
        bcol = colstack(be_ref[dd, pl.ds(r0, CHUNK), :], SM_B)
        grow = jnp.sum(eye * gcol, axis=0, keepdims=True)
        lm_rows, qkd_rows = [], []
        for h in range(GDN_HEADS):
            rs = slice(h * CHUNK, (h + 1) * CHUNK)
            lt = (h * CHUNK) // 128
            ls = slice(lt * 128, (lt + 1) * 128)
            dec = incl_ref[dd, rs, ls] * jnp.exp(jnp.minimum(gcol[rs] - grow[:, ls], 0.0))
            lm = strict_ref[dd, rs, ls] * dec * kk[rs, ls] * bcol[rs]
            qd = qk[rs, ls] * dec
            zero = jnp.zeros_like(lm)
            lm_rows.append(jnp.concatenate([lm, zero] if lt == 0 else [zero, lm], axis=1))
            qkd_rows.append(jnp.concatenate([qd, zero] if lt == 0 else [zero, qd], axis=1))
        lmat = jnp.concatenate(lm_rows, axis=0)
        rhs = jnp.concatenate([ks * (bcol * egcol), vs * bcol], axis=1).astype(BF16)
        qkd_sc[n] = jnp.concatenate(qkd_rows, axis=0).astype(BF16)
        qe_sc[n] = (qs * egcol).astype(BF16)
        kd = ks * kfcol
        for h in range(GDN_HEADS):
            kdt_sc[n * GDN_HEADS + h] = kd[h * CHUNK:(h + 1) * CHUNK].T.astype(BF16)
        egl_sc[n] = jnp.broadcast_to(jnp.exp(gtot), (8, 128))
        loc.append((lmat.astype(BF16), rhs))
    def inv_start(wave):
        return [eye - (loc[i][0] * lvl_ref[units[i][0], 0]).astype(F32) for i in wave]

    def inv_level(wave, tinv, lv):
        tb = [t_.astype(BF16) for t_ in tinv]
        cs = [jnp.dot(loc[i][0] * lvl_ref[units[i][0], lv], t_, preferred_element_type=F32)
              for i, t_ in zip(wave, tb)]
        return [t32 - jnp.dot(t_, c_.astype(BF16), preferred_element_type=F32)
                for t32, t_, c_ in zip(tinv, tb, cs)]

    def solve(wave, tinv):
        for i, t_ in zip(wave, tinv):
            dd, c = units[i]
            sol = jnp.dot(t_.astype(BF16), loc[i][1], preferred_element_type=F32)
            w_sc[dd * nchunk + c] = sol[:, :GDN_DK].astype(BF16)
            u0_sc[dd * nchunk + c] = sol[:, GDN_DK:]

    s_cur = [[s_ref[dd, h] for h in range(GDN_HEADS)] for dd in range(2)]

    def recur(ci):
        step = []
        for dd in range(2):
            c = ci if dd == 0 else nchunk - 1 - ci
            n = dd * nchunk + c
            sb = [s_.astype(BF16) for s_ in s_cur[dd]]
            ub = []
            for h in range(GDN_HEADS):
                rows = pl.ds(h * CHUNK, CHUNK)
                u = u0_sc[n, rows, :] - jnp.dot(w_sc[n, rows, :], sb[h], preferred_element_type=F32)
                ub.append(u.astype(BF16))
            step.append((c, n, sb, ub))
        for dd, (c, n, sb, ub) in enumerate(step):
            egl = egl_sc[n]
            for h in range(GDN_HEADS):
                s_cur[dd][h] = (egl[0:1, SM_A + h:SM_A + h + 1] * s_cur[dd][h]
                                + jnp.dot(kdt_sc[n * GDN_HEADS + h], ub[h], preferred_element_type=F32))
        for dd, (c, n, sb, ub) in enumerate(step):
            o2 = jnp.dot(qkd_sc[n], jnp.concatenate(ub, axis=0), preferred_element_type=F32)
            for h in range(GDN_HEADS):
                rows = pl.ds(h * CHUNK, CHUNK)
                o_refs[dd][pl.ds(c * CHUNK, CHUNK), h * GDN_DV:(h + 1) * GDN_DV] = (
                    jnp.dot(qe_sc[n, rows, :], sb[h], preferred_element_type=F32)
                    + o2[h * CHUNK:(h + 1) * CHUNK]).astype(BF16)

    every = list(range(len(units)))
    tinv = inv_start(every)
    for lv in range(1, 6):
        tinv = inv_level(every, tinv, lv)
    solve(every, tinv)
    for ci in range(nchunk):
        recur(ci)
    for dd in range(2):
        for h in range(GDN_HEADS):
            s_ref[dd, h] = s_cur[dd][h]
    sf_ref[...] = s_ref[...]


def _gdn(proj, arow, dtb, s0, fin, *, layer, nseq, n_seq):
    t = proj.shape[0]
    nblk = n_seq // SUB
    rbi = _row_block_index(nblk)
    c = _C
    consts = [jnp.asarray(c["incl"], BF16), jnp.asarray(c["incl_bd"], F32), jnp.asarray(c["strict_bd"], F32),
              jnp.asarray(c["levels"], BF16), jnp.asarray(c["eye256"], F32)]
    hs = (GDN_HEADS, GDN_DK, GDN_DV)
    nck = SUB // CHUNK
    nu = 2 * nck
    zero_init = s0 is None
    s_spec = pl.BlockSpec((2, None) + hs, lambda b, i: (0, b, 0, 0, 0))
    full = lambda shape: pl.BlockSpec(shape, lambda b, i: (0,) * len(shape))
    rows = (lambda b, i: rbi(0, b, i), lambda b, i: rbi(1, b, i))
    args = [proj, proj, proj, proj, arow, dtb, *consts] + ([] if zero_init else [s0])
    return pl.pallas_call(
        functools.partial(_gdn_kernel, nchunk=nck, zero_init=zero_init),
        grid=(nseq, nblk),
        in_specs=[pl.BlockSpec((SUB, 3 * GDN_W), lambda b, i: (rows[0](b, i), P_DQKV // (3 * GDN_W))),
                  pl.BlockSpec((SUB, 3 * GDN_W), lambda b, i: (rows[1](b, i), P_DQKV // (3 * GDN_W))),
                  pl.BlockSpec((SUB, 128), lambda b, i: (rows[0](b, i), P_SMALL // 128)),
                  pl.BlockSpec((SUB, 128), lambda b, i: (rows[1](b, i), P_SMALL // 128 + 1)),
                  full((2, 1, 128)), full((2, 1, 128)), full((2, CHUNK, CHUNK)),
                  full((2, 256, 256)), full((2, 256, 256)), full((2, 6, 256, 256)), full((256, 256))]
                 + ([] if zero_init else [s_spec]) + [pl.BlockSpec(memory_space=pl.ANY)],
        out_specs=[pl.BlockSpec((SUB, GDN_W), lambda b, i: (rows[0](b, i), 0)),
                   pl.BlockSpec((SUB, GDN_W), lambda b, i: (rows[1](b, i), 0)), _fin_spec(hs, layer)],
        out_shape=[jax.ShapeDtypeStruct((t, GDN_W), BF16), jax.ShapeDtypeStruct((t, GDN_W), BF16),
                   jax.ShapeDtypeStruct((nseq, DEPTH, 2) + hs, F32)],
        input_output_aliases={len(args): 2},
        scratch_shapes=[pltpu.VMEM((2,) + hs, F32),
                        pltpu.VMEM((2, SUB, 128), F32), pltpu.VMEM((2, SUB, 128), F32),
                        pltpu.VMEM((nu, 256, GDN_DK), BF16), pltpu.VMEM((nu, 256, GDN_DV), F32),
                        pltpu.VMEM((nu, 256, 256), BF16), pltpu.VMEM((nu, 256, GDN_DK), BF16),
                        pltpu.VMEM((nu * GDN_HEADS, GDN_DK, CHUNK), BF16), pltpu.VMEM((nu, 8, 128), F32)],
        compiler_params=_cparams(("arbitrary", "arbitrary")),
        name="gdn",
    )(*args, fin)


def _cmul(ar, ai, br, bi):
    return ar * br - ai * bi, ar * bi + ai * br


def _split2(x):
    hi = x.astype(BF16)
    return hi, (x - hi.astype(F32)).astype(BF16)


def _s5_consts_kernel(lr_ref, li_ref, st_ref, br_ref, bi_ref, cr_ref, ci_ref, gm_ref, bm_ref,
                      laml_ref, wexp_ref, vexp_ref, tt_ref, w_ref, v_ref, a_ref, q_ref):
    d = pl.program_id(0) % 2
    lc = S5_L * S5_CH

    def trows(s):
        return pl.ds(pl.multiple_of((s + d * (S5_L - 1 - 2 * s)) * S5_CH, S5_CH), S5_CH)

    lr = lr_ref[...]
    li = li_ref[...]
    step = jnp.exp(st_ref[...])
    mag = jnp.exp(lr * step)
    lbr = mag * jnp.cos(li * step)
    lbi = mag * jnp.sin(li * step)
    nr = lbr - 1.0
    den = lr * lr + li * li
    fr = (nr * lr + lbi * li) / den
    fi = (lbi * lr - nr * li) / den
    bbr, bbi = _cmul(fr, fi, br_ref[...], bi_ref[...])
    cr = cr_ref[...]
    ci = ci_ref[...]
    pr = jnp.ones_like(lr)
    pi = jnp.zeros_like(lr)
    for s in range(S5_L):
        ar, ai = _cmul(cr, ci, pr, pi)
        a_ref[0, trows(s), :] = ar
        a_ref[1, trows(s), :] = ai
        wr, wi = _cmul(pr, pi, bbr, bbi)
        w_ref[0, trows(S5_L - 1 - s), :] = wr
        w_ref[1, trows(S5_L - 1 - s), :] = wi
        pr, pi = _cmul(pr, pi, lbr, lbi)
        vr, vi = _cmul(cr, ci, pr, pi)
        v_ref[0, trows(s), :] = vr
        v_ref[1, trows(s), :] = -vi
    laml_ref[0:1, :] = pr
    laml_ref[1:2, :] = pi

    lane = lax.broadcasted_iota(jnp.int32, (1, 128), 1)
    for gp in range(S5_PAIRS):
        lanes = slice(gp * 128, (gp + 1) * 128)
        for r in range(2):
            for src, dst in ((w_ref, wexp_ref), (v_ref, vexp_ref)):
                slab = src[r, :, lanes]
                dst[r, gp] = jnp.concatenate([jnp.where(lane < S5_P, slab, 0.0),
                                              jnp.where(lane >= S5_P, slab, 0.0)], axis=0).astype(BF16)

    gm = gm_ref[...]
    kall = None
    for r, b, sign in ((0, bbr, 1.0), (1, bbi, -1.0)):
        ah, al = _split2(a_ref[r])
        bh, bl = _split2(jnp.concatenate([b] * S5_GROUPS, axis=0) * gm)
        nt = lambda x, y: lax.dot_general(x, y, (((1,), (1,)), ((), ())), preferred_element_type=F32)
        term = (nt(ah, bh) + nt(ah, bl) + nt(al, bh)) * sign
        kall = term if kall is None else kall + term

    zeros = jnp.zeros((lc, S5_W), BF16)
    kall = kall.astype(BF16)
    for delta in range(S5_GROUPS):
        q_ref[delta, 0:lc, :] = zeros
        q_ref[delta, lc:2 * lc, :] = kall if delta == 0 else pltpu.roll(kall, delta * S5_CH, axis=1)
        q_ref[delta, 2 * lc:3 * lc, :] = zeros
    for g in range(S5_GROUPS):
        acc = zeros
        for s in range(S5_L):
            start = pl.multiple_of(lc - S5_CH * s + d * (S5_CH * (S5_L - 1)), S5_CH)
            blk = q_ref[(s - g) % S5_GROUPS, pl.ds(start, lc), :]
            acc = jnp.where(bm_ref[s:s + 1, :] > 0.5, blk, acc)
        tt_ref[g] = acc


def _s5_consts(lam_re, lam_im, log_step, b_re, b_im, c_re, c_im):
    n = DEPTH * 2
    lc = S5_L * S5_CH
    lr = lam_re.reshape(n, 1, S5_N)
    li = lam_im.reshape(n, 1, S5_N)
    st = jnp.repeat(log_step.reshape(n, S5_GROUPS), S5_P, axis=-1).reshape(n, 1, S5_N)
    br = b_re.reshape(n, S5_N, S5_CH).transpose(0, 2, 1)
    bi = b_im.reshape(n, S5_N, S5_CH).transpose(0, 2, 1)
    cr = c_re.reshape(n, S5_GROUPS, S5_CH, S5_P).transpose(0, 2, 1, 3).reshape(n, S5_CH, S5_N)
    ci = c_im.reshape(n, S5_GROUPS, S5_CH, S5_P).transpose(0, 2, 1, 3).reshape(n, S5_CH, S5_N)
    gm = jnp.asarray(np.kron(np.eye(S5_GROUPS), np.ones((S5_CH, S5_P))), F32)
    row = pl.BlockSpec((None, 1, S5_N), lambda i: (i, 0, 0))
    mat = pl.BlockSpec((None, S5_CH, S5_N), lambda i: (i, 0, 0))
    exp = pl.BlockSpec((None, 2, S5_PAIRS, 2 * lc, 128), lambda i: (i, 0, 0, 0, 0))
    laml, wexp, vexp, tt = pl.pallas_call(
        _s5_consts_kernel,
        grid=(n,),
        in_specs=[row, row, row, mat, mat, mat, mat, pl.BlockSpec((S5_W, S5_N), lambda i: (0, 0)),
                  pl.BlockSpec((S5_GROUPS, S5_W), lambda i: (0, 0))],
        out_specs=[pl.BlockSpec((None, 2, S5_N), lambda i: (i, 0, 0)), exp, exp,
                   pl.BlockSpec((None, S5_GROUPS, lc, lc), lambda i: (i, 0, 0, 0))],
        out_shape=[jax.ShapeDtypeStruct((n, 2, S5_N), F32),
                   jax.ShapeDtypeStruct((n, 2, S5_PAIRS, 2 * lc, 128), BF16),
                   jax.ShapeDtypeStruct((n, 2, S5_PAIRS, 2 * lc, 128), BF16),
                   jax.ShapeDtypeStruct((n, S5_GROUPS, lc, lc), BF16)],
        scratch_shapes=[pltpu.VMEM((2, lc, S5_N), F32), pltpu.VMEM((2, lc, S5_N), F32),
                        pltpu.VMEM((2, lc, S5_N), F32), pltpu.VMEM((S5_GROUPS, 3 * lc, S5_W), BF16)],
        compiler_params=_cparams(("arbitrary",)),
        name="s5_consts",
    )(lr, li, st, br, bi, cr, ci, gm, _lane_block_masks())
    return laml.reshape(n, 2, S5_PAIRS, 128), wexp, vexp, tt


def _lane_block_masks():
    return jnp.asarray(np.kron(np.eye(S5_GROUPS), np.ones((1, S5_CH))), F32)


def _nt(a, b):
    return lax.dot_general(a, b, (((1,), (1,)), ((), ())), preferred_element_type=F32)


def _lane_block_transpose(arrs):
    n = len(arrs)
    width = arrs[0].shape[1]
    blk = lax.broadcasted_iota(jnp.int32, (1, width), 1) // (width // n)
    arrs = list(arrs)
    h = n // 2
    while h >= 1:
        hi = (blk & h) != 0
        for a in range(n):
            if a & h:
                continue
            lo_arr, hi_arr = arrs[a], arrs[a + h]
            arrs[a] = jnp.where(hi, pltpu.roll(hi_arr, h * (width // n), axis=1), lo_arr)
            arrs[a + h] = jnp.where(hi, hi_arr, pltpu.roll(lo_arr, width - h * (width // n), axis=1))
        h //= 2
    return arrs


def _s5_kernel(u0_ref, u1_ref, tt_ref, wexp_ref, vexp_ref, laml_ref, *rest, nseq, nblk, zero_init):
    if zero_init:
        y_ref, hf_ref, ug_sc, yg_sc, hin_re, hin_im, hp_re, hp_im = rest
    else:
        h0_ref, y_ref, hf_ref, ug_sc, yg_sc, hin_re, hin_im, hp_re, hp_im = rest
    d = pl.program_id(0)
    r = nblk * nseq
    lc = S5_L * S5_CH

    @pl.when(d == 0)
    def _():
        xs = []
        for s in range(S5_L):
            rows = pl.ds(s, r, stride=S5_L)
            xs.append(jnp.concatenate([u0_ref[rows, :], u1_ref[rows, :]], axis=1).astype(BF16))
        for g, ug in enumerate(_lane_block_transpose(xs)):
            ug_sc[g] = ug

    for gp in range(S5_PAIRS):
        up = jnp.concatenate([ug_sc[2 * gp], ug_sc[2 * gp + 1]], axis=1)
        rows = pl.ds(gp, r, stride=S5_PAIRS)
        hin_re[rows, :] = jnp.dot(up, wexp_ref[0, gp], preferred_element_type=F32)
        hin_im[rows, :] = jnp.dot(up, wexp_ref[1, gp], preferred_element_type=F32)

    lr = laml_ref[0]
    li = laml_ref[1]

    def step(kk, carry):
        k = kk + d * (nblk - 1 - 2 * kk)
        new = []
        for b in range(nseq):
            rows = pl.ds(pl.multiple_of((b * nblk + k) * S5_PAIRS, S5_PAIRS), S5_PAIRS)
            hr, hi = carry[2 * b], carry[2 * b + 1]
            hp_re[rows, :] = hr
            hp_im[rows, :] = hi
            new += [lr * hr - li * hi + hin_re[rows, :], lr * hi + li * hr + hin_im[rows, :]]
        return tuple(new)

    init = []
    for b in range(nseq):
        rows = slice(b * S5_PAIRS, (b + 1) * S5_PAIRS)
        init += ([jnp.zeros((S5_PAIRS, 128), F32)] * 2 if zero_init else [h0_ref[0, rows, :], h0_ref[1, rows, :]])
    fin = lax.fori_loop(0, nblk, step, tuple(init))
    for b in range(nseq):
        rows = slice(b * S5_PAIRS, (b + 1) * S5_PAIRS)
        hf_ref[0, rows, :] = fin[2 * b]
        hf_ref[1, rows, :] = fin[2 * b + 1]

    for gp in range(S5_PAIRS):
        rows = pl.ds(gp, r, stride=S5_PAIRS)
        ys = (_nt(hp_re[rows, :].astype(BF16), vexp_ref[0, gp])
              + _nt(hp_im[rows, :].astype(BF16), vexp_ref[1, gp]))
        for j in range(2):
            g = 2 * gp + j
            yg = _nt(ug_sc[g], tt_ref[g]) + ys[:, j * lc:(j + 1) * lc]

            @pl.when(d == 0)
            def _():
                yg_sc[g] = yg

            @pl.when(d == 1)
            def _():
                yg_sc[g] += yg

    @pl.when(d == 1)
    def _():
        for t, yt in enumerate(_lane_block_transpose([yg_sc[g] for g in range(S5_GROUPS)])):
            rows = pl.ds(t, r, stride=S5_L)
            y_ref[0, rows, :] = yt[:, 0:128]
            y_ref[1, rows, :] = yt[:, 128:2 * 128]


def _s5(proj, consts, h0, *, layer, nseq, n_seq):
    laml, wexp, vexp, tt = consts
    t = proj.shape[0]
    nblk = n_seq // S5_L
    r = nblk * nseq
    lc = S5_L * S5_CH
    bs = nseq * S5_PAIRS
    zero_init = h0 is None
    h_spec = pl.BlockSpec((None, 2, bs, 128), lambda d: (d, 0, 0, 0))
    return pl.pallas_call(
        functools.partial(_s5_kernel, nseq=nseq, nblk=nblk, zero_init=zero_init),
        grid=(2,),
        in_specs=[pl.BlockSpec((t, 128), lambda d: (0, P_SU // 128)),
                  pl.BlockSpec((t, 128), lambda d: (0, P_SU // 128 + 1)),
                  pl.BlockSpec((None, S5_GROUPS, lc, lc), lambda d: (2 * layer + d, 0, 0, 0)),
                  pl.BlockSpec((None, 2, S5_PAIRS, 2 * lc, 128), lambda d: (2 * layer + d, 0, 0, 0, 0)),
                  pl.BlockSpec((None, 2, S5_PAIRS, 2 * lc, 128), lambda d: (2 * layer + d, 0, 0, 0, 0)),
                  pl.BlockSpec((None, 2, S5_PAIRS, 128), lambda d: (2 * layer + d, 0, 0, 0))]
                 + ([] if zero_init else [h_spec]),
        out_specs=[pl.BlockSpec((2, t, 128), lambda d: (0, 0, 0)), h_spec],
        out_shape=[jax.ShapeDtypeStruct((2, t, 128), F32),
                   jax.ShapeDtypeStruct((2, 2, bs, 128), F32)],
        scratch_shapes=[pltpu.VMEM((S5_GROUPS, r, lc), BF16), pltpu.VMEM((S5_GROUPS, r, lc), F32)]
                       + [pltpu.VMEM((r * S5_PAIRS, 128), F32) for _ in range(4)],
        compiler_params=_cparams(("arbitrary",)),
        name="s5",
    )(proj, proj, tt, wexp, vexp, laml, *([] if zero_init else [h0]))


def _out_kernel(x_ref, mod_ref, ogf_ref, ogb_ref, odf_ref, odb_ref, ysf_ref, ysb_ref, gog_ref, su_ref, dog_ref,
                ind_ref, gw_ref, dw_ref, sd_ref, wglu_ref, bglu_ref, wout32_ref, o_ref, wout_ref):
    @pl.when(pl.program_id(0) == 0)
    def _():
        wout_ref[...] = wout32_ref[...].astype(BF16)

    og = ogf_ref[...].astype(F32) + ogb_ref[...].astype(F32)
    hi, mid, _ = _split3(og * og)
    ms = jnp.dot(hi, ind_ref[...], preferred_element_type=F32) + jnp.dot(mid, ind_ref[...], preferred_element_type=F32)
    og = og * lax.rsqrt(ms + EPS) * gw_ref[...] * _silu(gog_ref[...])
    od = odf_ref[...].astype(F32) + odb_ref[...].astype(F32)
    dog = dog_ref[...]
    parts = []
    for h in range(GDN_HEADS):
        z = od[:, h * GDN_DV:(h + 1) * GDN_DV]
        z = z * lax.rsqrt(jnp.mean(z * z, axis=-1, keepdims=True) + EPS) * dw_ref[...]
        parts.append(z * _silu(dog[:, h * GDN_DV:(h + 1) * GDN_DV]))
    y = su_ref[...] * sd_ref[...] + jnp.concatenate([ysf_ref[...], ysb_ref[...]], axis=1)
    g = 0.5 * y * (1.0 + jnp.tanh(math.sqrt(2.0 / math.pi) * (y + 0.044715 * (y * y * y))))
    os_ = g * jax.nn.sigmoid(_bdot(g, wglu_ref[...]) + bglu_ref[...])
    out = _bdot(og, wout_ref[0:GLA_W, :]) + _bdot(os_, wout_ref[GLA_W + GDN_W:MIX_W, :])
    for h in range(GDN_HEADS):
        lo = GLA_W + h * GDN_DV
        out = out + _bdot(parts[h], wout_ref[lo:lo + GDN_DV, :])
    o_ref[...] = x_ref[...] + mod_ref[5:6, :] * out


def _mix_out(x, mods, proj, og, od, ys, gla_nw, gdn_nw, s5_d, w_glu, b_glu, w_out, *, layer, n_seq):
    t, d = x.shape
    nb = mods.shape[0]
    tm = _pick_tile(n_seq if nb > 1 else t, MIX_TM)
    midx = _mod_index(nb, tm, n_seq)
    ys2 = ys

    def dirspec(w, dd):
        return pl.BlockSpec((None, tm, w), lambda i: (dd, i, 0))

    def vec(w):
        return pl.BlockSpec((None, 1, w), lambda i: (layer, 0, 0))

    return pl.pallas_call(
        _out_kernel,
        grid=(t // tm,),
        in_specs=[pl.BlockSpec((tm, d), lambda i: (i, 0)),
                  pl.BlockSpec((None, N_MOD, d), lambda i: (midx(i), 0, 0)),
                  pl.BlockSpec((tm, GLA_W), lambda i: (i, 0)), pl.BlockSpec((tm, GLA_W), lambda i: (i, 0)),
                  pl.BlockSpec((tm, GDN_W), lambda i: (i, 0)), pl.BlockSpec((tm, GDN_W), lambda i: (i, 0)),
                  dirspec(128, 0), dirspec(128, 1),
                  pl.BlockSpec((tm, GLA_W), lambda i: (i, P_GOG // GLA_W)),
                  pl.BlockSpec((tm, S5_W), lambda i: (i, P_SU // S5_W)),
                  pl.BlockSpec((tm, GDN_W), lambda i: (i, P_DOG // GDN_W)),
                  pl.BlockSpec((GLA_W, GLA_W), lambda i: (0, 0)),
                  vec(GLA_W), vec(GDN_DV), vec(S5_W),
                  pl.BlockSpec((None, S5_W, S5_W), lambda i: (layer, 0, 0)),
                  vec(S5_W),
                  pl.BlockSpec((None, MIX_W, d), lambda i: (layer, 0, 0), pipeline_mode=pl.Buffered(1))],
        out_specs=pl.BlockSpec((tm, d), lambda i: (i, 0)),
        out_shape=jax.ShapeDtypeStruct((t, d), F32),
        scratch_shapes=[pltpu.VMEM((MIX_W, d), BF16)],
        compiler_params=_cparams(("arbitrary",)),
        name="mix_out",
    )(x, mods, og[0], og[1], od[0], od[1], ys2, ys2, proj, proj, proj,
      jnp.asarray(_C["ind64"], BF16),
      jnp.tile(gla_nw, (1, GLA_HEADS)).reshape(DEPTH, 1, GLA_W), gdn_nw.reshape(DEPTH, 1, GDN_DV),
      s5_d.reshape(DEPTH, 1, S5_W), w_glu, b_glu.reshape(DEPTH, 1, S5_W), w_out)


def _gla_params(gk_up, gk_bias, layer):
    up = jnp.zeros((2, 128, GLA_KW), F32).at[:, :GLA_RANK, :].set(gk_up[layer])
    return up, gk_bias[layer].reshape(2, 1, GLA_KW)


def _gdn_params(a_log, dt_bias, layer):
    arow = jnp.zeros((2, 1, 128), F32).at[:, 0, SM_A:SM_A + GDN_HEADS].set(a_log[layer])
    dtb = jnp.zeros((2, 1, 128), F32).at[:, 0, SM_A:SM_A + GDN_HEADS].set(dt_bias[layer])
    return arow, dtb


def _gla_state_in(s):
    eye = jnp.eye(GLA_HEADS, dtype=F32)
    st = jnp.einsum('bzhde,hg->zbhegd', s.astype(F32), eye)
    return st.reshape(2, s.shape[0], GLA_W, GLA_KW)


def _trunk(x3, mods, rows, states, p, s5c, w_in):
    nseq, n_seq, d = x3.shape
    x = x3.reshape(nseq * n_seq, d)
    fin = []
    gla_f = jnp.zeros((nseq, DEPTH, 2, GLA_HEADS, GLA_DK, GLA_DV), F32)
    gdn_f = jnp.zeros((nseq, DEPTH, 2, GDN_HEADS, GDN_DK, GDN_DV), F32)
    for l in range(DEPTH):
        m = mods[l]
        x = _ffn(x, m, p['norm_w'], p['ffn_w_gate'], p['ffn_w_up'], p['ffn_w_down'], p['final_norm_w'],
                 layer=l, which=0, n_seq=n_seq, final=False)
        proj = _proj(x, m, p['norm_w'], w_in, p['gdn_conv_w'], layer=l, n_seq=n_seq, rowlen=n_seq // rows)
        if states is None:
            st0 = s0 = h0 = None
        else:
            st_gla, st_gdn, st_re, st_im = states
            st0 = _gla_state_in(st_gla[:, l])
            s0 = st_gdn[:, l].transpose(1, 0, 2, 3, 4)
            h0 = jnp.stack([st_re[:, l].reshape(nseq, 2, S5_N), st_im[:, l].reshape(nseq, 2, S5_N)], axis=0)
            h0 = h0.transpose(2, 0, 1, 3).reshape(2, 2, nseq * S5_PAIRS, 128)
        up_p, bias_p = _gla_params(p['gla_gk_up'], p['gla_gk_bias'], l)
        og_f, og_b, gla_f = _gla(proj, up_p, bias_p, st0, gla_f, layer=l, nseq=nseq, n_seq=n_seq)
        og = (og_f, og_b)
        arow, dtb = _gdn_params(p['gdn_a_log'], p['gdn_dt_bias'], l)
        od_f, od_b, gdn_f = _gdn(proj, arow, dtb, s0, gdn_f, layer=l, nseq=nseq, n_seq=n_seq)
        od = (od_f, od_b)
        ys, s5_f = _s5(proj, s5c, h0, layer=l, nseq=nseq, n_seq=n_seq)
        x = _mix_out(x, m, proj, og, od, ys, p['gla_norm_w'], p['gdn_norm_w'], p['s5_d'], p['s5_w_glu'],
                     p['s5_b_glu'], p['w_out'], layer=l, n_seq=n_seq)
        x = _ffn(x, m, p['norm_w'], p['ffn_w_gate'], p['ffn_w_up'], p['ffn_w_down'], p['final_norm_w'],
                 layer=l, which=1, n_seq=n_seq, final=(l == DEPTH - 1))
        fin.append((gla_f, gdn_f, s5_f))
    return x.reshape(nseq, n_seq, d), fin


def kernel(x_prompt, x_sample, c, state_gla, state_gdn, state_s5_re, state_s5_im, c_ctx, w_ada, b_ada, norm_w, ffn_w_gate, ffn_w_up, ffn_w_down, w_in, gla_gk_up, gla_gk_bias, gla_norm_w, gdn_conv_w, gdn_a_log, gdn_dt_bias, gdn_norm_w, s5_lam_re, s5_lam_im, s5_log_step, s5_b_re, s5_b_im, s5_c_re, s5_c_im, s5_d, s5_w_glu, s5_b_glu, w_out, final_norm_w):
    p = dict(norm_w=norm_w, ffn_w_gate=ffn_w_gate, ffn_w_up=ffn_w_up, ffn_w_down=ffn_w_down,
             gla_gk_up=gla_gk_up, gla_gk_bias=gla_gk_bias, gla_norm_w=gla_norm_w, gdn_conv_w=gdn_conv_w,
             gdn_a_log=gdn_a_log, gdn_dt_bias=gdn_dt_bias, gdn_norm_w=gdn_norm_w, s5_c_re=s5_c_re,
             s5_c_im=s5_c_im, s5_d=s5_d, s5_w_glu=s5_w_glu, s5_b_glu=s5_b_glu, w_out=w_out,
             final_norm_w=final_norm_w)
    b_ctx = x_prompt.shape[0]
    b_dec = x_sample.shape[0]
    cond8 = jnp.zeros((8, D_MODEL), F32).at[0].set(c_ctx).at[1:1 + b_dec].set(c)
    mods = _ada(cond8, w_ada, b_ada)
    mods_ctx = mods[:, 0:1].reshape(DEPTH, 1, N_MOD, D_MODEL)
    mods_dec = mods[:, 1:1 + b_dec].reshape(DEPTH, b_dec, N_MOD, D_MODEL)
    s5c = _s5_consts(s5_lam_re, s5_lam_im, s5_log_step, s5_b_re, s5_b_im, s5_c_re, s5_c_im)

    y_prompt, fin = _trunk(x_prompt, mods_ctx, 1, None, p, s5c, w_in)
    new_gla, new_gdn = fin[-1][0], fin[-1][1]
    s5f = jnp.stack([f[2].reshape(2, 2, b_ctx, S5_GROUPS, S5_P).transpose(1, 2, 0, 3, 4) for f in fin], axis=2)
    new_re, new_im = s5f[0], s5f[1]

    rows = x_sample.shape[1] // 64
    y_sample, _ = _trunk(x_sample, mods_dec, rows, (state_gla, state_gdn, state_s5_re, state_s5_im),
                         p, s5c, w_in)
    return (y_prompt, y_sample, new_gla.astype(state_gla.dtype), new_gdn.astype(state_gdn.dtype),
            new_re.astype(state_s5_re.dtype), new_im.astype(state_s5_im.dtype))
```

```python
import functools
import math

import numpy as np
import jax
import jax.numpy as jnp
from jax import lax
from jax.experimental import pallas as pl
from jax.experimental.pallas import tpu as pltpu

F32 = jnp.float32
BF16 = jnp.bfloat16

D_MODEL = 1024
DEPTH = 2
FFN_DIM = 2816
N_MOD = 9
EPS = 1e-6
CHUNK = 64
GLA_HEADS, GLA_DK, GLA_DV, GLA_RANK = 4, 32, 64, 16
GLA_GATE_NORM = 16.0
GDN_HEADS, GDN_DK, GDN_DV, GDN_CONV = 4, 128, 128, 5
S5_GROUPS, S5_CH, S5_P = 16, 16, 64
GLA_W = GLA_HEADS * GLA_DV
GLA_KW = GLA_HEADS * GLA_DK
GDN_W = GDN_HEADS * GDN_DV
S5_W = S5_GROUPS * S5_CH
S5_N = S5_GROUPS * S5_P
S5_L = 16
S5_PAIRS = S5_GROUPS // 2
MIX_W = GLA_W + GDN_W + S5_W

_IN_SIZES = (GLA_KW, GLA_KW, GLA_W, GLA_RANK, GLA_RANK, GLA_W,
             GDN_W, GDN_W, GDN_W, GDN_HEADS, GDN_HEADS, GDN_HEADS, GDN_HEADS, GDN_W, S5_W)
_IN_OFF = tuple(int(v) for v in np.cumsum((0,) + _IN_SIZES))
(_GQ, _GK, _GV, _GLRF, _GLRB, _GOG, _DQ, _DK, _DV, _DAF, _DAB, _DBF, _DBB, _DOG, _SU) = range(15)

P_DQKV = 0
P_SMALL = 1536
P_GQK = 1792
P_GV = 2048
P_GOG = 2304
P_DOG = 2560
P_SU = 3072
PROJ_W = 3328
PROJ_TN = PROJ_W // 2
SM_A = GLA_RANK
SM_B = GLA_RANK + GDN_HEADS

SUB = 256
GLA_SUB = 256
FFN_TM = 2048
FFN_TF = 256
PROJ_TM = 1024
MIX_TM = 256
ADA_TN = 1024
V7X_VMEM_BYTES = 64 * 1024 * 1024
VMEM_LIMIT = V7X_VMEM_BYTES - 8 * 1024 * 1024


def _cparams(sem):
    return pltpu.CompilerParams(dimension_semantics=sem, vmem_limit_bytes=VMEM_LIMIT)


def _bdot(a, b):
    return jnp.dot(a.astype(BF16), b.astype(BF16), preferred_element_type=F32)


def _bdot_nt(a, b):
    return lax.dot_general(a.astype(BF16), b.astype(BF16), (((1,), (1,)), ((), ())),
                           preferred_element_type=F32)


def _split3(x):
    hi = x.astype(BF16)
    r1 = x - hi.astype(F32)
    mid = r1.astype(BF16)
    lo = (r1 - mid.astype(F32)).astype(BF16)
    return hi, mid, lo


def _sel_dot(m01, x):
    hi, mid, lo = _split3(x)
    return (jnp.dot(m01, hi, preferred_element_type=F32) + jnp.dot(m01, mid, preferred_element_type=F32)
            + jnp.dot(m01, lo, preferred_element_type=F32))


def _silu(x):
    return x * jax.nn.sigmoid(x)


def _softplus(x):
    return jnp.maximum(x, 0.0) + jnp.log1p(jnp.exp(-jnp.abs(x)))


def _log_sigmoid(x):
    return -_softplus(-x)


def _norm_mod(x, nw, scale, shift):
    y = x * lax.rsqrt(jnp.mean(x * x, axis=-1, keepdims=True) + EPS)
    return y * nw * (1.0 + scale) + shift


def _pick_tile(n, cap):
    t = cap
    while n % t:
        t //= 2
    return t


def _ada_kernel(c_ref, w_ref, b_ref, o_ref):
    o_ref[...] = _bdot(_silu(c_ref[...]), w_ref[...]) + b_ref[...]


def _ada(cond8, w_ada, b_ada):
    depth, d, nout = w_ada.shape
    tn = ADA_TN
    return pl.pallas_call(
        _ada_kernel,
        grid=(depth, nout // tn),
        in_specs=[pl.BlockSpec((8, d), lambda l, j: (0, 0)),
                  pl.BlockSpec((None, d, tn), lambda l, j: (l, 0, j)),
                  pl.BlockSpec((None, 1, tn), lambda l, j: (l, 0, j))],
        out_specs=pl.BlockSpec((None, 8, tn), lambda l, j: (l, 0, j)),
        out_shape=jax.ShapeDtypeStruct((depth, 8, nout), F32),
        compiler_params=_cparams(("parallel", "parallel")),
        name="ada",
    )(cond8, w_ada, b_ada.reshape(depth, 1, nout))


def _ffn_kernel(x_ref, mod_ref, nw_ref, wg_ref, wu_ref, wd_ref, fw_ref, o_ref, h_ref, *, mod_base, final):
    j = pl.program_id(1)

    @pl.when(j == 0)
    def _():
        m = mod_ref[...]
        h = _norm_mod(x_ref[...], nw_ref[...], m[mod_base + 1:mod_base + 2], m[mod_base:mod_base + 1])
        h_ref[...] = h.astype(BF16)
        o_ref[...] = jnp.zeros_like(o_ref)

    h = h_ref[...]
    g = jnp.dot(h, wg_ref[...].astype(BF16), preferred_element_type=F32)
    u = jnp.dot(h, wu_ref[...].astype(BF16), preferred_element_type=F32)
    o_ref[...] += _bdot(_silu(g) * u, wd_ref[...])

    @pl.when(j == pl.num_programs(1) - 1)
    def _():
        m = mod_ref[...]
        y = x_ref[...] + 0.5 * m[mod_base + 2:mod_base + 3] * o_ref[...]
        if final:
            y = y * lax.rsqrt(jnp.mean(y * y, axis=-1, keepdims=True) + EPS) * fw_ref[...]
        o_ref[...] = y


def _mod_index(nb, tm, n_seq):
    if nb == 1:
        return lambda i: 0
    return lambda i: (i * tm) // n_seq


def _ffn(x, mods, norm_w, w_gate, w_up, w_down, final_w, *, layer, which, n_seq, final):
    t, d = x.shape
    f = w_gate.shape[-1]
    nb = mods.shape[0]
    tm = _pick_tile(n_seq if nb > 1 else t, FFN_TM)
    tf = FFN_TF
    midx = _mod_index(nb, tm, n_seq)
    mod_base = 0 if which == 0 else 6
    return pl.pallas_call(
        functools.partial(_ffn_kernel, mod_base=mod_base, final=final),
        grid=(t // tm, f // tf),
        in_specs=[pl.BlockSpec((tm, d), lambda i, j: (i, 0)),
                  pl.BlockSpec((None, N_MOD, d), lambda i, j: (midx(i), 0, 0)),
                  pl.BlockSpec((None, None, 1, d), lambda i, j: (layer, 2 * which, 0, 0)),
                  pl.BlockSpec((None, None, d, tf), lambda i, j: (layer, which, 0, j)),
                  pl.BlockSpec((None, None, d, tf), lambda i, j: (layer, which, 0, j)),
                  pl.BlockSpec((None, None, tf, d), lambda i, j: (layer, which, j, 0)),
                  pl.BlockSpec((1, d), lambda i, j: (0, 0))],
        out_specs=pl.BlockSpec((tm, d), lambda i, j: (i, 0)),
        out_shape=jax.ShapeDtypeStruct((t, d), F32),
        scratch_shapes=[pltpu.VMEM((tm, d), BF16)],
        compiler_params=_cparams(("parallel", "arbitrary")),
        name="ffn",
    )(x, mods, norm_w.reshape(DEPTH, 3, 1, d), w_gate, w_up, w_down, final_w.reshape(1, d))


def _proj_kernel(x_ref, mod_ref, nw_ref, w_ref, cw_ref, o_ref, h_ref, wb_ref, st_ref, *, rowlen):
    j = pl.program_id(0)
    i = pl.program_id(1)

    for jt, segs in enumerate(_W_SEGS):
        @pl.when(jnp.logical_and(i == 0, j == jt))
        def _():
            for lo, src, width in segs:
                if width % 128 == 0:
                    wb_ref[lo:lo + width, :] = w_ref[src:src + width, :].astype(BF16)
                else:
                    base = lo // 128 * 128
                    if src is None:
                        st_ref[lo - base:lo - base + width, :] = jnp.zeros((width, st_ref.shape[1]), F32)
                        wb_ref[base:base + 128, :] = st_ref[...].astype(BF16)
                    else:
                        st_ref[lo - base:lo - base + width, :] = w_ref[src:src + width, :]

    @pl.when(j == 0)
    def _():
        m = mod_ref[...]
        h_ref[i] = _norm_mod(x_ref[...], nw_ref[...], m[4:5], m[3:4]).astype(BF16)
        o_ref[...] = _nt(h_ref[i], wb_ref[...])
        tm = o_ref.shape[0]
        pos = lax.broadcasted_iota(jnp.int32, (tm, 1), 0) % rowlen
        valid = [jnp.logical_and(pos + (t - GDN_CONV // 2) >= 0, pos + (t - GDN_CONV // 2) < rowlen)
                 for t in range(GDN_CONV)]
        for g in range(3 * GDN_HEADS):
            cols = slice(P_DQKV + g * GDN_DK, P_DQKV + (g + 1) * GDN_DK)
            xg = o_ref[:, cols]
            acc = xg * cw_ref[GDN_CONV // 2:GDN_CONV // 2 + 1, cols]
            for t in range(GDN_CONV):
                s = t - GDN_CONV // 2
                if s != 0:
                    xs = pltpu.roll(xg, (-s) % tm, axis=0)
                    acc = acc + jnp.where(valid[t], xs, 0.0) * cw_ref[t:t + 1, cols]
            y = _silu(acc)
            if g < 2 * GDN_HEADS:
                y = y * lax.rsqrt(jnp.sum(y * y, axis=-1, keepdims=True) + EPS)
                if g < GDN_HEADS:
                    y = y * (GDN_DK ** -0.5)
            o_ref[:, cols] = y

    @pl.when(j == 1)
    def _():
        o_ref[...] = _nt(h_ref[i], wb_ref[...])


def _proj(x, mods, norm_w, w_in, conv_w, *, layer, n_seq, rowlen):
    t, d = x.shape
    nb = mods.shape[0]
    tm = _pick_tile(n_seq if nb > 1 else t, PROJ_TM)
    midx = _mod_index(nb, tm, n_seq)
    return pl.pallas_call(
        functools.partial(_proj_kernel, rowlen=rowlen),
        grid=(PROJ_W // PROJ_TN, t // tm),
        in_specs=[pl.BlockSpec((tm, d), lambda j, i: (i * (1 - j), 0)),
                  pl.BlockSpec((None, N_MOD, d), lambda j, i: (midx(i), 0, 0)),
                  pl.BlockSpec((None, None, 1, d), lambda j, i: (layer, 1, 0, 0)),
                  pl.BlockSpec((None, w_in.shape[-1], d), lambda j, i: (layer, 0, 0), pipeline_mode=pl.Buffered(1)),
                  pl.BlockSpec((None, GDN_CONV, 3 * GDN_W), lambda j, i: (layer, 0, 0))],
        out_specs=pl.BlockSpec((tm, PROJ_TN), lambda j, i: (i, j)),
        out_shape=jax.ShapeDtypeStruct((t, PROJ_W), F32),
        scratch_shapes=[pltpu.VMEM((t // tm, tm, d), BF16), pltpu.VMEM((PROJ_TN, d), BF16),
                        pltpu.VMEM((128, d), F32)],
        compiler_params=_cparams(("arbitrary", "arbitrary")),
        name="proj",
    )(x, mods, norm_w.reshape(DEPTH, 3, 1, d), jnp.swapaxes(w_in, 1, 2), conv_w)


def _w_in_segments():
    order = [_DQ, _DK, _DV, _GLRF, _DAF, _DBF, None, _GLRB, _DAB, _DBB, None,
             _GQ, _GK, _GV, _GOG, _DOG, _SU]
    sm_pad = 128 - GLA_RANK - 2 * GDN_HEADS
    tiles = [[] for _ in range(PROJ_W // PROJ_TN)]
    dst = 0
    for s in order:
        width = sm_pad if s is None else _IN_SIZES[s]
        src = None if s is None else _IN_OFF[s]
        tile = tiles[dst // PROJ_TN]
        lo = dst % PROJ_TN
        if tile and src is not None and tile[-1][1] is not None and (
                tile[-1][0] + tile[-1][2] == lo and tile[-1][1] + tile[-1][2] == src):
            tile[-1] = (tile[-1][0], tile[-1][1], tile[-1][2] + width)
        else:
            tile.append((lo, src, width))
        dst += width
        assert (dst - 1) // PROJ_TN == (dst - width) // PROJ_TN, "a segment may not straddle column tiles"
    assert dst == PROJ_W
    return tiles


_W_SEGS = _w_in_segments()


def _np_consts():
    t = np.arange(CHUNK)
    lower = (t[:, None] >= t[None, :])
    incl = np.stack([lower, lower.T])
    strict = np.stack([t[:, None] > t[None, :], t[:, None] < t[None, :]])
    eye4 = np.eye(4, dtype=bool)
    blk = np.kron(eye4, np.ones((CHUNK, CHUNK), bool))
    incl_bd = np.stack([np.kron(eye4, incl[d]) for d in range(2)])
    strict_bd = np.stack([np.kron(eye4, strict[d]) for d in range(2)])
    levels = []
    for d in range(2):
        per = []
        for lv in range(6):
            s = 1 << lv
            same = (t[:, None] // (2 * s)) == (t[None, :] // (2 * s))
            hi = (t % (2 * s)) >= s
            m = same & hi[:, None] & (~hi)[None, :]
            if d == 1:
                m = m.T
            per.append(np.kron(eye4, m))
        levels.append(np.stack(per))
    levels = np.stack(levels)
    assert (levels.sum(1) == strict_bd).all() and blk.any()
    hm_k = np.kron(eye4, np.ones((1, GLA_DK)))
    hm_v = np.kron(eye4, np.ones((1, GLA_DV)))
    st_mask = np.kron(eye4, np.ones((GLA_DV, GLA_DK)))
    ind64 = np.kron(eye4, np.full((GLA_DV, GLA_DV), 1.0 / GLA_DV))
    gla_sel, gla_later, gla_earlier, gla_pair = [], [], [], []
    for d in range(2):
        pos = t if d == 0 else CHUNK - 1 - t
        cum = (pos[None, :] <= pos[:, None]).astype(np.float64)
        sel, later, earlier, pair = [cum], [], [], []
        for lv in range(6):
            s = 1 << lv
            ref_pos = (pos // (2 * s)) * (2 * s) + s - 1
            pick = (pos[None, :] == ref_pos[:, None]).astype(np.float64)
            sel.append(pick @ cum)
            is_later = (pos % (2 * s)) >= s
            same = (pos[:, None] // (2 * s)) == (pos[None, :] // (2 * s))
            later.append(np.repeat(np.where(is_later, 0.0, -1e30)[:, None], GLA_KW, axis=1))
            earlier.append(np.repeat(np.where(is_later, -1e30, 0.0)[:, None], GLA_KW, axis=1))
            pair.append(np.tile(same & is_later[:, None] & (~is_later)[None, :], (1, 4)))
        gla_sel.append(np.concatenate(sel, axis=0))
        gla_later.append(np.stack(later))
        gla_earlier.append(np.stack(earlier))
        gla_pair.append(np.stack(pair))
        assert (np.stack(pair).sum(0) == np.tile(pos[:, None] > pos[None, :], (1, 4))).all()
    head_sum = np.kron(eye4, np.ones((GLA_DK, GLA_DV)))
    return dict(incl=incl, incl_bd=incl_bd, strict_bd=strict_bd, levels=levels,
                hm_k=hm_k, hm_v=hm_v, st_mask=st_mask, ind64=ind64, eye256=np.eye(256),
                gla_sel=np.stack(gla_sel), gla_later=np.stack(gla_later), gla_earlier=np.stack(gla_earlier),
                gla_pair=np.stack(gla_pair), head_sum=head_sum)


_C = _np_consts()


def _row_block_index(nblk):
    return lambda d, b, i: b * nblk + i + d * (nblk - 1 - 2 * i)


def _gla_kernel(qkf_ref, qkb_ref, vf_ref, vb_ref, smf_ref, smb_ref, up_ref, bias_ref, sel_ref, later_ref,
                earlier_ref, pair_ref, hmk_ref, hmv_ref, stm_ref, hsum_ref, *rest, nchunk, zero_init):
    rest = list(rest)
    st0_ref = None if zero_init else rest.pop(0)
    rest.pop(0)
    of_ref, ob_ref, stf_ref, st_ref, qd_sc, m_sc, dec_sc, oi_sc = rest
    ins =((qkf_ref, vf_ref, smf_ref, of_ref), (qkb_ref, vb_ref, smb_ref, ob_ref))

    @pl.when(pl.program_id(1) == 0)
    def _():
        st_ref[...] = jnp.zeros_like(st_ref) if zero_init else st0_ref[...]

    hmkb = hmk_ref[...].astype(BF16)
    hmv = hmv_ref[...]
    stm = stm_ref[...]

    units = [(dd, c) for c in range(nchunk) for dd in range(2)]
    st1 = []
    for dd, c in units:
        qk_ref, v_ref, sm_ref, _ = ins[dd]
        rows = pl.ds(c * CHUNK, CHUNK)
        qk = qk_ref[rows, :]
        q = qk[:, 0:GLA_KW] * (GLA_DK ** -0.5)
        k = qk[:, GLA_KW:2 * GLA_KW]
        x = _bdot(sm_ref[rows, :], up_ref[dd]) + bias_ref[dd]
        lg = _log_sigmoid(x) * (1.0 / GLA_GATE_NORM)
        ball = _sel_dot(sel_ref[dd], lg)
        st1.append((q, k, v_ref[rows, :], ball, jnp.sum(lg, axis=0, keepdims=True)))
    prods = []
    for lv in range(6):
        for (dd, c), (q, k, _, ball, _) in zip(units, st1):
            b = ball[0:CHUNK]
            ref = ball[(lv + 1) * CHUNK:(lv + 2) * CHUNK]
            ql = q * jnp.exp((b - ref) + later_ref[dd, lv])
            kl = (k * jnp.exp((ref - b) + earlier_ref[dd, lv])).astype(BF16)
            kexp = jnp.concatenate([kl * hmkb[h:h + 1] for h in range(GLA_HEADS)], axis=0)
            prods.append(_bdot_nt(ql, kexp))
    for i, ((dd, c), (q, k, v, ball, btot)) in enumerate(zip(units, st1)):
        o_ref = ins[dd][3]
        n = dd * nchunk + c
        b = ball[0:CHUNK]
        att = jnp.zeros((CHUNK, GLA_HEADS * CHUNK), F32)
        for lv in range(6):
            att = jnp.where(pair_ref[dd, lv] > 0.5, prods[lv * len(units) + i], att)
        vexp = jnp.concatenate([v * hmv[h:h + 1] for h in range(GLA_HEADS)], axis=0)
        oi_sc[n] = _bdot(att, vexp) + _bdot(q * k, hsum_ref[...]) * v
        qd_sc[n] = (q * jnp.exp(b)).astype(BF16)
        kd = k * jnp.exp(btot - b)
        m_sc[n] = _bdot(v.T, kd) * stm
        dec_sc[n] = jnp.broadcast_to(jnp.exp(btot), (8, GLA_KW))

    for dd in range(2):
        o_ref = ins[dd][3]
        st = st_ref[dd]
        for ci in range(nchunk):
            c = ci if dd == 0 else nchunk - 1 - ci
            n = dd * nchunk + c
            rows = pl.ds(c * CHUNK, CHUNK)
            o_ref[rows, :] = (oi_sc[n] + _nt(qd_sc[n], st.astype(BF16))).astype(BF16)
            st = st * dec_sc[n][0:1] + m_sc[n]
        st_ref[dd] = st
        for h in range(GLA_HEADS):
            stf_ref[dd, h] = st[h * GLA_DV:(h + 1) * GLA_DV, h * GLA_DK:(h + 1) * GLA_DK].T


def _gla(proj, up_p, bias_p, st0, fin, *, layer, nseq, n_seq):
    t = proj.shape[0]
    sub = min(n_seq, GLA_SUB)
    nblk = n_seq // sub
    nck = sub // CHUNK
    rbi = _row_block_index(nblk)
    c = _C
    consts = [jnp.asarray(c["gla_sel"], BF16), jnp.asarray(c["gla_later"], F32), jnp.asarray(c["gla_earlier"], F32),
              jnp.asarray(c["gla_pair"], F32), jnp.asarray(c["hm_k"], F32), jnp.asarray(c["hm_v"], F32),
              jnp.asarray(c["st_mask"], F32), jnp.asarray(c["head_sum"], BF16)]
    st_spec = pl.BlockSpec((2, None, GLA_W, GLA_KW), lambda b, i: (0, b, 0, 0))
    full = lambda a: pl.BlockSpec(a.shape, lambda b, i: (0,) * a.ndim)
    zero_init = st0 is None
    nu = 2 * nck
    rows = (lambda b, i: rbi(0, b, i), lambda b, i: rbi(1, b, i))

    def blk(width, col, dd):
        return pl.BlockSpec((sub, width), lambda b, i: (rows[dd](b, i), col // width))

    hs = (GLA_HEADS, GLA_DK, GLA_DV)
    args = [proj, proj, proj, proj, proj, proj, up_p, bias_p, *consts] + ([] if zero_init else [st0])
    in_specs = ([blk(2 * GLA_KW, P_GQK, 0), blk(2 * GLA_KW, P_GQK, 1), blk(GLA_W, P_GV, 0), blk(GLA_W, P_GV, 1),
                 blk(128, P_SMALL, 0), blk(128, P_SMALL + 128, 1), full(up_p), full(bias_p)]
                + [full(a) for a in consts] + ([] if zero_init else [st_spec]))
    return pl.pallas_call(
        functools.partial(_gla_kernel, nchunk=nck, zero_init=zero_init),
        grid=(nseq, nblk),
        in_specs=in_specs + [pl.BlockSpec(memory_space=pl.ANY)],
        out_specs=[pl.BlockSpec((sub, GLA_W), lambda b, i: (rows[0](b, i), 0)),
                   pl.BlockSpec((sub, GLA_W), lambda b, i: (rows[1](b, i), 0)), _fin_spec(hs, layer)],
        out_shape=[jax.ShapeDtypeStruct((t, GLA_W), BF16), jax.ShapeDtypeStruct((t, GLA_W), BF16),
                   jax.ShapeDtypeStruct((nseq, DEPTH, 2) + hs, F32)],
        scratch_shapes=[pltpu.VMEM((2, GLA_W, GLA_KW), F32), pltpu.VMEM((nu, CHUNK, GLA_KW), BF16),
                        pltpu.VMEM((nu, GLA_W, GLA_KW), F32), pltpu.VMEM((nu, 8, GLA_KW), F32),
                        pltpu.VMEM((nu, CHUNK, GLA_W), F32)],
        input_output_aliases={len(args): 2},
        compiler_params=_cparams(("arbitrary", "arbitrary")),
        name="gla",
    )(*args, fin)


def _fin_spec(hs, layer):
    return pl.BlockSpec((None, None, 2) + hs, lambda b, i: (b, layer, 0) + (0,) * len(hs))


def _gdn_kernel(qkvf_ref, qkvb_ref, smf_ref, smb_ref, arow_ref, dtb_ref, tri_ref, incl_ref, strict_ref, lvl_ref,
                eye_ref, *rest, nchunk, zero_init):
    rest = list(rest)
    s0_ref = None if zero_init else rest.pop(0)
    rest.pop(0)
    of_ref, ob_ref, sf_ref, s_ref, la_ref, be_ref, w_sc, u0_sc, qkd_sc, qe_sc, kdt_sc, egl_sc = rest
    qkv_refs = (qkvf_ref, qkvb_ref)
    o_refs = (of_ref, ob_ref)

    @pl.when(pl.program_id(1) == 0)
    def _():
        s_ref[...] = jnp.zeros_like(s_ref) if zero_init else s0_ref[...]

    for dd, sm_ref in enumerate((smf_ref, smb_ref)):
        sm = sm_ref[...]
        la_ref[dd] = -jnp.exp(arow_ref[dd]) * _softplus(sm + dtb_ref[dd])
        be_ref[dd] = jax.nn.sigmoid(sm)

    eye = eye_ref[...]

    def stack(ref, r0, lo, width):
        return jnp.concatenate([ref[pl.ds(r0, CHUNK), lo + h * width:lo + (h + 1) * width]
                                for h in range(GDN_HEADS)], axis=0)

    def colstack(a, lo):
        return jnp.concatenate([a[:, lo + h:lo + h + 1] for h in range(GDN_HEADS)], axis=0)

    units = [(dd, c) for c in range(nchunk) for dd in range(2)]
    st1 = []
    for dd, c in units:
        r0 = c * CHUNK
        qkv_ref = qkv_refs[dd]
        qs = stack(qkv_ref, r0, 0, GDN_DK)
        ks = stack(qkv_ref, r0, GDN_W, GDN_DK)
        la = la_ref[dd, pl.ds(r0, CHUNK), :]
        g = _sel_dot(tri_ref[dd], la)
        ksb = ks.astype(BF16)
        st1.append((qs, ks, la, g, _bdot_nt(ksb, ksb), _bdot_nt(qs, ksb)))
    loc = []
    for (dd, c), (qs, ks, la, g, kk, qk) in zip(units, st1):
        r0 = c * CHUNK
        n = dd * nchunk + c
        vs = stack(qkv_refs[dd], r0, 2 * GDN_W, GDN_DV)
        gtot = jnp.sum(la, axis=0, keepdims=True)
        gcol = colstack(g, SM_A)
        kfcol = colstack(jnp.exp(gtot - g), SM_A)
        egcol = colstack(jnp.exp(g), SM_A)
        bcol = colstack(be_ref[dd, pl.ds(r0, CHUNK), :], SM_B)
        grow = jnp.sum(eye * gcol, axis=0, keepdims=True)
        lm_rows, qkd_rows = [], []
        for h in range(GDN_HEADS):
            rs = slice(h * CHUNK, (h + 1) * CHUNK)
            lt = (h * CHUNK) // 128
            ls = slice(lt * 128, (lt + 1) * 128)
            dec = incl_ref[dd, rs, ls] * jnp.exp(jnp.minimum(gcol[rs] - grow[:, ls], 0.0))
            lm = strict_ref[dd, rs, ls] * dec * kk[rs, ls] * bcol[rs]
            qd = qk[rs, ls] * dec
            zero = jnp.zeros_like(lm)
            lm_rows.append(jnp.concatenate([lm, zero] if lt == 0 else [zero, lm], axis=1))
            qkd_rows.append(jnp.concatenate([qd, zero] if lt == 0 else [zero, qd], axis=1))
        lmat = jnp.concatenate(lm_rows, axis=0)
        rhs = jnp.concatenate([ks * (bcol * egcol), vs * bcol], axis=1).astype(BF16)
        qkd_sc[n] = jnp.concatenate(qkd_rows, axis=0).astype(BF16)
        qe_sc[n] = (qs * egcol).astype(BF16)
        kd = ks * kfcol
        for h in range(GDN_HEADS):
            kdt_sc[n * GDN_HEADS + h] = kd[h * CHUNK:(h + 1) * CHUNK].T.astype(BF16)
        egl_sc[n] = jnp.broadcast_to(jnp.exp(gtot), (8, 128))
        loc.append((lmat.astype(BF16), rhs))
    def inv_start(wave):
        return [eye - (loc[i][0] * lvl_ref[units[i][0], 0]).astype(F32) for i in wave]

    def inv_level(wave, tinv, lv):
        tb = [t_.astype(BF16) for t_ in tinv]
        cs = [jnp.dot(loc[i][0] * lvl_ref[units[i][0], lv], t_, preferred_element_type=F32)
              for i, t_ in zip(wave, tb)]
        return [t32 - jnp.dot(t_, c_.astype(BF16), preferred_element_type=F32)
                for t32, t_, c_ in zip(tinv, tb, cs)]

    def solve(wave, tinv):
        for i, t_ in zip(wave, tinv):
            dd, c = units[i]
            sol = jnp.dot(t_.astype(BF16), loc[i][1], preferred_element_type=F32)
            w_sc[dd * nchunk + c] = sol[:, :GDN_DK].astype(BF16)
            u0_sc[dd * nchunk + c] = sol[:, GDN_DK:]

    s_cur = [[s_ref[dd, h] for h in range(GDN_HEADS)] for dd in range(2)]

    def recur(ci):
        step = []
        for dd in range(2):
            c = ci if dd == 0 else nchunk - 1 - ci
            n = dd * nchunk + c
            sb = [s_.astype(BF16) for s_ in s_cur[dd]]
            ub = []
            for h in range(GDN_HEADS):
                rows = pl.ds(h * CHUNK, CHUNK)
                u = u0_sc[n, rows, :] - jnp.dot(w_sc[n, rows, :], sb[h], preferred_element_type=F32)
                ub.append(u.astype(BF16))
            step.append((c, n, sb, ub))
        for dd, (c, n, sb, ub) in enumerate(step):
            egl = egl_sc[n]
            for h in range(GDN_HEADS):
                s_cur[dd][h] = (egl[0:1, SM_A + h:SM_A + h + 1] * s_cur[dd][h]
                                + jnp.dot(kdt_sc[n * GDN_HEADS + h], ub[h], preferred_element_type=F32))
        for dd, (c, n, sb, ub) in enumerate(step):
            o2 = jnp.dot(qkd_sc[n], jnp.concatenate(ub, axis=0), preferred_element_type=F32)
            for h in range(GDN_HEADS):
                rows = pl.ds(h * CHUNK, CHUNK)
                o_refs[dd][pl.ds(c * CHUNK, CHUNK), h * GDN_DV:(h + 1) * GDN_DV] = (
                    jnp.dot(qe_sc[n, rows, :], sb[h], preferred_element_type=F32)
                    + o2[h * CHUNK:(h + 1) * CHUNK]).astype(BF16)

    every = list(range(len(units)))
    tinv = inv_start(every)
    for lv in range(1, 6):
        tinv = inv_level(every, tinv, lv)
    solve(every, tinv)
    for ci in range(nchunk):
        recur(ci)
    for dd in range(2):
        for h in range(GDN_HEADS):
            s_ref[dd, h] = s_cur[dd][h]
    sf_ref[...] = s_ref[...]


def _gdn(proj, arow, dtb, s0, fin, *, layer, nseq, n_seq):
    t = proj.shape[0]
    nblk = n_seq // SUB
    rbi = _row_block_index(nblk)
    c = _C
    consts = [jnp.asarray(c["incl"], BF16), jnp.asarray(c["incl_bd"], F32), jnp.asarray(c["strict_bd"], F32),
              jnp.asarray(c["levels"], BF16), jnp.asarray(c["eye256"], F32)]
    hs = (GDN_HEADS, GDN_DK, GDN_DV)
    nck = SUB // CHUNK
    nu = 2 * nck
    zero_init = s0 is None
    s_spec = pl.BlockSpec((2, None) + hs, lambda b, i: (0, b, 0, 0, 0))
    full = lambda shape: pl.BlockSpec(shape, lambda b, i: (0,) * len(shape))
    rows = (lambda b, i: rbi(0, b, i), lambda b, i: rbi(1, b, i))
    args = [proj, proj, proj, proj, arow, dtb, *consts] + ([] if zero_init else [s0])
    return pl.pallas_call(
        functools.partial(_gdn_kernel, nchunk=nck, zero_init=zero_init),
        grid=(nseq, nblk),
        in_specs=[pl.BlockSpec((SUB, 3 * GDN_W), lambda b, i: (rows[0](b, i), P_DQKV // (3 * GDN_W))),
                  pl.BlockSpec((SUB, 3 * GDN_W), lambda b, i: (rows[1](b, i), P_DQKV // (3 * GDN_W))),
                  pl.BlockSpec((SUB, 128), lambda b, i: (rows[0](b, i), P_SMALL // 128)),
                  pl.BlockSpec((SUB, 128), lambda b, i: (rows[1](b, i), P_SMALL // 128 + 1)),
                  full((2, 1, 128)), full((2, 1, 128)), full((2, CHUNK, CHUNK)),
                  full((2, 256, 256)), full((2, 256, 256)), full((2, 6, 256, 256)), full((256, 256))]
                 + ([] if zero_init else [s_spec]) + [pl.BlockSpec(memory_space=pl.ANY)],
        out_specs=[pl.BlockSpec((SUB, GDN_W), lambda b, i: (rows[0](b, i), 0)),
                   pl.BlockSpec((SUB, GDN_W), lambda b, i: (rows[1](b, i), 0)), _fin_spec(hs, layer)],
        out_shape=[jax.ShapeDtypeStruct((t, GDN_W), BF16), jax.ShapeDtypeStruct((t, GDN_W), BF16),
                   jax.ShapeDtypeStruct((nseq, DEPTH, 2) + hs, F32)],
        input_output_aliases={len(args): 2},
        scratch_shapes=[pltpu.VMEM((2,) + hs, F32),
                        pltpu.VMEM((2, SUB, 128), F32), pltpu.VMEM((2, SUB, 128), F32),
                        pltpu.VMEM((nu, 256, GDN_DK), BF16), pltpu.VMEM((nu, 256, GDN_DV), F32),
                        pltpu.VMEM((nu, 256, 256), BF16), pltpu.VMEM((nu, 256, GDN_DK), BF16),
                        pltpu.VMEM((nu * GDN_HEADS, GDN_DK, CHUNK), BF16), pltpu.VMEM((nu, 8, 128), F32)],
        compiler_params=_cparams(("arbitrary", "arbitrary")),
        name="gdn",
    )(*args, fin)


def _cmul(ar, ai, br, bi):
    return ar * br - ai * bi, ar * bi + ai * br


def _split2(x):
    hi = x.astype(BF16)
    return hi, (x - hi.astype(F32)).astype(BF16)


def _s5_consts_kernel(lr_ref, li_ref, st_ref, br_ref, bi_ref, cr_ref, ci_ref, gm_ref, bm_ref,
                      laml_ref, wexp_ref, vexp_ref, tt_ref, w_ref, v_ref, a_ref, q_ref):
    d = pl.program_id(0) % 2
    lc = S5_L * S5_CH

    def trows(s):
        return pl.ds(pl.multiple_of((s + d * (S5_L - 1 - 2 * s)) * S5_CH, S5_CH), S5_CH)

    lr = lr_ref[...]
    li = li_ref[...]
    step = jnp.exp(st_ref[...])
    mag = jnp.exp(lr * step)
    lbr = mag * jnp.cos(li * step)
    lbi = mag * jnp.sin(li * step)
    nr = lbr - 1.0
    den = lr * lr + li * li
    fr = (nr * lr + lbi * li) / den
    fi = (lbi * lr - nr * li) / den
    bbr, bbi = _cmul(fr, fi, br_ref[...], bi_ref[...])
    cr = cr_ref[...]
    ci = ci_ref[...]
    pr = jnp.ones_like(lr)
    pi = jnp.zeros_like(lr)
    for s in range(S5_L):
        ar, ai = _cmul(cr, ci, pr, pi)
        a_ref[0, trows(s), :] = ar
        a_ref[1, trows(s), :] = ai
        wr, wi = _cmul(pr, pi, bbr, bbi)
        w_ref[0, trows(S5_L - 1 - s), :] = wr
        w_ref[1, trows(S5_L - 1 - s), :] = wi
        pr, pi = _cmul(pr, pi, lbr, lbi)
        vr, vi = _cmul(cr, ci, pr, pi)
        v_ref[0, trows(s), :] = vr
        v_ref[1, trows(s), :] = -vi
    laml_ref[0:1, :] = pr
    laml_ref[1:2, :] = pi

    lane = lax.broadcasted_iota(jnp.int32, (1, 128), 1)
    for gp in range(S5_PAIRS):
        lanes = slice(gp * 128, (gp + 1) * 128)
        for r in range(2):
            for src, dst in ((w_ref, wexp_ref), (v_ref, vexp_ref)):
                slab = src[r, :, lanes]
                dst[r, gp] = jnp.concatenate([jnp.where(lane < S5_P, slab, 0.0),
                                              jnp.where(lane >= S5_P, slab, 0.0)], axis=0).astype(BF16)

    gm = gm_ref[...]
    kall = None
    for r, b, sign in ((0, bbr, 1.0), (1, bbi, -1.0)):
        ah, al = _split2(a_ref[r])
        bh, bl = _split2(jnp.concatenate([b] * S5_GROUPS, axis=0) * gm)
        nt = lambda x, y: lax.dot_general(x, y, (((1,), (1,)), ((), ())), preferred_element_type=F32)
        term = (nt(ah, bh) + nt(ah, bl) + nt(al, bh)) * sign
        kall = term if kall is None else kall + term

    zeros = jnp.zeros((lc, S5_W), BF16)
    kall = kall.astype(BF16)
    for delta in range(S5_GROUPS):
        q_ref[delta, 0:lc, :] = zeros
        q_ref[delta, lc:2 * lc, :] = kall if delta == 0 else pltpu.roll(kall, delta * S5_CH, axis=1)
        q_ref[delta, 2 * lc:3 * lc, :] = zeros
    for g in range(S5_GROUPS):
        acc = zeros
        for s in range(S5_L):
            start = pl.multiple_of(lc - S5_CH * s + d * (S5_CH * (S5_L - 1)), S5_CH)
            blk = q_ref[(s - g) % S5_GROUPS, pl.ds(start, lc), :]
            acc = jnp.where(bm_ref[s:s + 1, :] > 0.5, blk, acc)
        tt_ref[g] = acc


def _s5_consts(lam_re, lam_im, log_step, b_re, b_im, c_re, c_im):
    n = DEPTH * 2
    lc = S5_L * S5_CH
    lr = lam_re.reshape(n, 1, S5_N)
    li = lam_im.reshape(n, 1, S5_N)
    st = jnp.repeat(log_step.reshape(n, S5_GROUPS), S5_P, axis=-1).reshape(n, 1, S5_N)
    br = b_re.reshape(n, S5_N, S5_CH).transpose(0, 2, 1)
    bi = b_im.reshape(n, S5_N, S5_CH).transpose(0, 2, 1)
    cr = c_re.reshape(n, S5_GROUPS, S5_CH, S5_P).transpose(0, 2, 1, 3).reshape(n, S5_CH, S5_N)
    ci = c_im.reshape(n, S5_GROUPS, S5_CH, S5_P).transpose(0, 2, 1, 3).reshape(n, S5_CH, S5_N)
    gm = jnp.asarray(np.kron(np.eye(S5_GROUPS), np.ones((S5_CH, S5_P))), F32)
    row = pl.BlockSpec((None, 1, S5_N), lambda i: (i, 0, 0))
    mat = pl.BlockSpec((None, S5_CH, S5_N), lambda i: (i, 0, 0))
    exp = pl.BlockSpec((None, 2, S5_PAIRS, 2 * lc, 128), lambda i: (i, 0, 0, 0, 0))
    laml, wexp, vexp, tt = pl.pallas_call(
        _s5_consts_kernel,
        grid=(n,),
        in_specs=[row, row, row, mat, mat, mat, mat, pl.BlockSpec((S5_W, S5_N), lambda i: (0, 0)),
                  pl.BlockSpec((S5_GROUPS, S5_W), lambda i: (0, 0))],
        out_specs=[pl.BlockSpec((None, 2, S5_N), lambda i: (i, 0, 0)), exp, exp,
                   pl.BlockSpec((None, S5_GROUPS, lc, lc), lambda i: (i, 0, 0, 0))],
        out_shape=[jax.ShapeDtypeStruct((n, 2, S5_N), F32),
                   jax.ShapeDtypeStruct((n, 2, S5_PAIRS, 2 * lc, 128), BF16),
                   jax.ShapeDtypeStruct((n, 2, S5_PAIRS, 2 * lc, 128), BF16),
                   jax.ShapeDtypeStruct((n, S5_GROUPS, lc, lc), BF16)],
        scratch_shapes=[pltpu.VMEM((2, lc, S5_N), F32), pltpu.VMEM((2, lc, S5_N), F32),
                        pltpu.VMEM((2, lc, S5_N), F32), pltpu.VMEM((S5_GROUPS, 3 * lc, S5_W), BF16)],
        compiler_params=_cparams(("arbitrary",)),
        name="s5_consts",
    )(lr, li, st, br, bi, cr, ci, gm, _lane_block_masks())
    return laml.reshape(n, 2, S5_PAIRS, 128), wexp, vexp, tt


def _lane_block_masks():
    return jnp.asarray(np.kron(np.eye(S5_GROUPS), np.ones((1, S5_CH))), F32)


def _nt(a, b):
    return lax.dot_general(a, b, (((1,), (1,)), ((), ())), preferred_element_type=F32)


def _lane_block_transpose(arrs):
    n = len(arrs)
    width = arrs[0].shape[1]
    blk = lax.broadcasted_iota(jnp.int32, (1, width), 1) // (width // n)
    arrs = list(arrs)
    h = n // 2
    while h >= 1:
        hi = (blk & h) != 0
        for a in range(n):
            if a & h:
                continue
            lo_arr, hi_arr = arrs[a], arrs[a + h]
            arrs[a] = jnp.where(hi, pltpu.roll(hi_arr, h * (width // n), axis=1), lo_arr)
            arrs[a + h] = jnp.where(hi, hi_arr, pltpu.roll(lo_arr, width - h * (width // n), axis=1))
        h //= 2
    return arrs


def _s5_kernel(u0_ref, u1_ref, tt_ref, wexp_ref, vexp_ref, laml_ref, *rest, nseq, nblk, zero_init):
    if zero_init:
        y_ref, hf_ref, ug_sc, yg_sc, hin_re, hin_im, hp_re, hp_im = rest
    else:
        h0_ref, y_ref, hf_ref, ug_sc, yg_sc, hin_re, hin_im, hp_re, hp_im = rest
    d = pl.program_id(0)
    r = nblk * nseq
    lc = S5_L * S5_CH

    @pl.when(d == 0)
    def _():
        xs = []
        for s in range(S5_L):
            rows = pl.ds(s, r, stride=S5_L)
            xs.append(jnp.concatenate([u0_ref[rows, :], u1_ref[rows, :]], axis=1).astype(BF16))
        for g, ug in enumerate(_lane_block_transpose(xs)):
            ug_sc[g] = ug

    for gp in range(S5_PAIRS):
        up = jnp.concatenate([ug_sc[2 * gp], ug_sc[2 * gp + 1]], axis=1)
        rows = pl.ds(gp, r, stride=S5_PAIRS)
        hin_re[rows, :] = jnp.dot(up, wexp_ref[0, gp], preferred_element_type=F32)
        hin_im[rows, :] = jnp.dot(up, wexp_ref[1, gp], preferred_element_type=F32)

    lr = laml_ref[0]
    li = laml_ref[1]

    def step(kk, carry):
        k = kk + d * (nblk - 1 - 2 * kk)
        new = []
        for b in range(nseq):
            rows = pl.ds(pl.multiple_of((b * nblk + k) * S5_PAIRS, S5_PAIRS), S5_PAIRS)
            hr, hi = carry[2 * b], carry[2 * b + 1]
            hp_re[rows, :] = hr
            hp_im[rows, :] = hi
            new += [lr * hr - li * hi + hin_re[rows, :], lr * hi + li * hr + hin_im[rows, :]]
        return tuple(new)

    init = []
    for b in range(nseq):
        rows = slice(b * S5_PAIRS, (b + 1) * S5_PAIRS)
        init += ([jnp.zeros((S5_PAIRS, 128), F32)] * 2 if zero_init else [h0_ref[0, rows, :], h0_ref[1, rows, :]])
    fin = lax.fori_loop(0, nblk, step, tuple(init))
    for b in range(nseq):
        rows = slice(b * S5_PAIRS, (b + 1) * S5_PAIRS)
        hf_ref[0, rows, :] = fin[2 * b]
        hf_ref[1, rows, :] = fin[2 * b + 1]

    for gp in range(S5_PAIRS):
        rows = pl.ds(gp, r, stride=S5_PAIRS)
        ys = (_nt(hp_re[rows, :].astype(BF16), vexp_ref[0, gp])
              + _nt(hp_im[rows, :].astype(BF16), vexp_ref[1, gp]))
        for j in range(2):
            g = 2 * gp + j
            yg = _nt(ug_sc[g], tt_ref[g]) + ys[:, j * lc:(j + 1) * lc]

            @pl.when(d == 0)
            def _():
                yg_sc[g] = yg

            @pl.when(d == 1)
            def _():
                yg_sc[g] += yg

    @pl.when(d == 1)
    def _():
        for t, yt in enumerate(_lane_block_transpose([yg_sc[g] for g in range(S5_GROUPS)])):
            rows = pl.ds(t, r, stride=S5_L)
            y_ref[0, rows, :] = yt[:, 0:128]
            y_ref[1, rows, :] = yt[:, 128:2 * 128]


def _s5(proj, consts, h0, *, layer, nseq, n_seq):
    laml, wexp, vexp, tt = consts
    t = proj.shape[0]
    nblk = n_seq // S5_L
    r = nblk * nseq
    lc = S5_L * S5_CH
    bs = nseq * S5_PAIRS
    zero_init = h0 is None
    h_spec = pl.BlockSpec((None, 2, bs, 128), lambda d: (d, 0, 0, 0))
    return pl.pallas_call(
        functools.partial(_s5_kernel, nseq=nseq, nblk=nblk, zero_init=zero_init),
        grid=(2,),
        in_specs=[pl.BlockSpec((t, 128), lambda d: (0, P_SU // 128)),
                  pl.BlockSpec((t, 128), lambda d: (0, P_SU // 128 + 1)),
                  pl.BlockSpec((None, S5_GROUPS, lc, lc), lambda d: (2 * layer + d, 0, 0, 0)),
                  pl.BlockSpec((None, 2, S5_PAIRS, 2 * lc, 128), lambda d: (2 * layer + d, 0, 0, 0, 0)),
                  pl.BlockSpec((None, 2, S5_PAIRS, 2 * lc, 128), lambda d: (2 * layer + d, 0, 0, 0, 0)),
                  pl.BlockSpec((None, 2, S5_PAIRS, 128), lambda d: (2 * layer + d, 0, 0, 0))]
                 + ([] if zero_init else [h_spec]),
        out_specs=[pl.BlockSpec((2, t, 128), lambda d: (0, 0, 0)), h_spec],
        out_shape=[jax.ShapeDtypeStruct((2, t, 128), F32),
                   jax.ShapeDtypeStruct((2, 2, bs, 128), F32)],
        scratch_shapes=[pltpu.VMEM((S5_GROUPS, r, lc), BF16), pltpu.VMEM((S5_GROUPS, r, lc), F32)]
                       + [pltpu.VMEM((r * S5_PAIRS, 128), F32) for _ in range(4)],
        compiler_params=_cparams(("arbitrary",)),
        name="s5",
    )(proj, proj, tt, wexp, vexp, laml, *([] if zero_init else [h0]))


def _out_kernel(x_ref, mod_ref, ogf_ref, ogb_ref, odf_ref, odb_ref, ysf_ref, ysb_ref, gog_ref, su_ref, dog_ref,
                ind_ref, gw_ref, dw_ref, sd_ref, wglu_ref, bglu_ref, wout32_ref, o_ref, wout_ref):
    @pl.when(pl.program_id(0) == 0)
    def _():
        wout_ref[...] = wout32_ref[...].astype(BF16)

    og = ogf_ref[...].astype(F32) + ogb_ref[...].astype(F32)
    hi, mid, _ = _split3(og * og)
    ms = jnp.dot(hi, ind_ref[...], preferred_element_type=F32) + jnp.dot(mid, ind_ref[...], preferred_element_type=F32)
    og = og * lax.rsqrt(ms + EPS) * gw_ref[...] * _silu(gog_ref[...])
    od = odf_ref[...].astype(F32) + odb_ref[...].astype(F32)
    dog = dog_ref[...]
    parts = []
    for h in range(GDN_HEADS):
        z = od[:, h * GDN_DV:(h + 1) * GDN_DV]
        z = z * lax.rsqrt(jnp.mean(z * z, axis=-1, keepdims=True) + EPS) * dw_ref[...]
        parts.append(z * _silu(dog[:, h * GDN_DV:(h + 1) * GDN_DV]))
    y = su_ref[...] * sd_ref[...] + jnp.concatenate([ysf_ref[...], ysb_ref[...]], axis=1)
    g = 0.5 * y * (1.0 + jnp.tanh(math.sqrt(2.0 / math.pi) * (y + 0.044715 * (y * y * y))))
    os_ = g * jax.nn.sigmoid(_bdot(g, wglu_ref[...]) + bglu_ref[...])
    out = _bdot(og, wout_ref[0:GLA_W, :]) + _bdot(os_, wout_ref[GLA_W + GDN_W:MIX_W, :])
    for h in range(GDN_HEADS):
        lo = GLA_W + h * GDN_DV
        out = out + _bdot(parts[h], wout_ref[lo:lo + GDN_DV, :])
    o_ref[...] = x_ref[...] + mod_ref[5:6, :] * out


def _mix_out(x, mods, proj, og, od, ys, gla_nw, gdn_nw, s5_d, w_glu, b_glu, w_out, *, layer, n_seq):
    t, d = x.shape
    nb = mods.shape[0]
    tm = _pick_tile(n_seq if nb > 1 else t, MIX_TM)
    midx = _mod_index(nb, tm, n_seq)
    ys2 = ys

    def dirspec(w, dd):
        return pl.BlockSpec((None, tm, w), lambda i: (dd, i, 0))

    def vec(w):
        return pl.BlockSpec((None, 1, w), lambda i: (layer, 0, 0))

    return pl.pallas_call(
        _out_kernel,
        grid=(t // tm,),
        in_specs=[pl.BlockSpec((tm, d), lambda i: (i, 0)),
                  pl.BlockSpec((None, N_MOD, d), lambda i: (midx(i), 0, 0)),
                  pl.BlockSpec((tm, GLA_W), lambda i: (i, 0)), pl.BlockSpec((tm, GLA_W), lambda i: (i, 0)),
                  pl.BlockSpec((tm, GDN_W), lambda i: (i, 0)), pl.BlockSpec((tm, GDN_W), lambda i: (i, 0)),
                  dirspec(128, 0), dirspec(128, 1),
                  pl.BlockSpec((tm, GLA_W), lambda i: (i, P_GOG // GLA_W)),
                  pl.BlockSpec((tm, S5_W), lambda i: (i, P_SU // S5_W)),
                  pl.BlockSpec((tm, GDN_W), lambda i: (i, P_DOG // GDN_W)),
                  pl.BlockSpec((GLA_W, GLA_W), lambda i: (0, 0)),
                  vec(GLA_W), vec(GDN_DV), vec(S5_W),
                  pl.BlockSpec((None, S5_W, S5_W), lambda i: (layer, 0, 0)),
                  vec(S5_W),
                  pl.BlockSpec((None, MIX_W, d), lambda i: (layer, 0, 0), pipeline_mode=pl.Buffered(1))],
        out_specs=pl.BlockSpec((tm, d), lambda i: (i, 0)),
        out_shape=jax.ShapeDtypeStruct((t, d), F32),
        scratch_shapes=[pltpu.VMEM((MIX_W, d), BF16)],
        compiler_params=_cparams(("arbitrary",)),
        name="mix_out",
    )(x, mods, og[0], og[1], od[0], od[1], ys2, ys2, proj, proj, proj,
      jnp.asarray(_C["ind64"], BF16),
      jnp.tile(gla_nw, (1, GLA_HEADS)).reshape(DEPTH, 1, GLA_W), gdn_nw.reshape(DEPTH, 1, GDN_DV),
      s5_d.reshape(DEPTH, 1, S5_W), w_glu, b_glu.reshape(DEPTH, 1, S5_W), w_out)


def _gla_params(gk_up, gk_bias, layer):
    up = jnp.zeros((2, 128, GLA_KW), F32).at[:, :GLA_RANK, :].set(gk_up[layer])
    return up, gk_bias[layer].reshape(2, 1, GLA_KW)


def _gdn_params(a_log, dt_bias, layer):
    arow = jnp.zeros((2, 1, 128), F32).at[:, 0, SM_A:SM_A + GDN_HEADS].set(a_log[layer])
    dtb = jnp.zeros((2, 1, 128), F32).at[:, 0, SM_A:SM_A + GDN_HEADS].set(dt_bias[layer])
    return arow, dtb


def _gla_state_in(s):
    eye = jnp.eye(GLA_HEADS, dtype=F32)
    st = jnp.einsum('bzhde,hg->zbhegd', s.astype(F32), eye)
    return st.reshape(2, s.shape[0], GLA_W, GLA_KW)


def _trunk(x3, mods, rows, states, p, s5c, w_in):
    nseq, n_seq, d = x3.shape
    x = x3.reshape(nseq * n_seq, d)
    fin = []
    gla_f = jnp.zeros((nseq, DEPTH, 2, GLA_HEADS, GLA_DK, GLA_DV), F32)
    gdn_f = jnp.zeros((nseq, DEPTH, 2, GDN_HEADS, GDN_DK, GDN_DV), F32)
    for l in range(DEPTH):
        m = mods[l]
        x = _ffn(x, m, p['norm_w'], p['ffn_w_gate'], p['ffn_w_up'], p['ffn_w_down'], p['final_norm_w'],
                 layer=l, which=0, n_seq=n_seq, final=False)
        proj = _proj(x, m, p['norm_w'], w_in, p['gdn_conv_w'], layer=l, n_seq=n_seq, rowlen=n_seq // rows)
        if states is None:
            st0 = s0 = h0 = None
        else:
            st_gla, st_gdn, st_re, st_im = states
            st0 = _gla_state_in(st_gla[:, l])
            s0 = st_gdn[:, l].transpose(1, 0, 2, 3, 4)
            h0 = jnp.stack([st_re[:, l].reshape(nseq, 2, S5_N), st_im[:, l].reshape(nseq, 2, S5_N)], axis=0)
            h0 = h0.transpose(2, 0, 1, 3).reshape(2, 2, nseq * S5_PAIRS, 128)
        up_p, bias_p = _gla_params(p['gla_gk_up'], p['gla_gk_bias'], l)
        og_f, og_b, gla_f = _gla(proj, up_p, bias_p, st0, gla_f, layer=l, nseq=nseq, n_seq=n_seq)
        og = (og_f, og_b)
        arow, dtb = _gdn_params(p['gdn_a_log'], p['gdn_dt_bias'], l)
        od_f, od_b, gdn_f = _gdn(proj, arow, dtb, s0, gdn_f, layer=l, nseq=nseq, n_seq=n_seq)
        od = (od_f, od_b)
        ys, s5_f = _s5(proj, s5c, h0, layer=l, nseq=nseq, n_seq=n_seq)
        x = _mix_out(x, m, proj, og, od, ys, p['gla_norm_w'], p['gdn_norm_w'], p['s5_d'], p['s5_w_glu'],
                     p['s5_b_glu'], p['w_out'], layer=l, n_seq=n_seq)
        x = _ffn(x, m, p['norm_w'], p['ffn_w_gate'], p['ffn_w_up'], p['ffn_w_down'], p['final_norm_w'],
                 layer=l, which=1, n_seq=n_seq, final=(l == DEPTH - 1))
        fin.append((gla_f, gdn_f, s5_f))
    return x.reshape(nseq, n_seq, d), fin


def kernel(x_prompt, x_sample, c, state_gla, state_gdn, state_s5_re, state_s5_im, c_ctx, w_ada, b_ada, norm_w, ffn_w_gate, ffn_w_up, ffn_w_down, w_in, gla_gk_up, gla_gk_bias, gla_norm_w, gdn_conv_w, gdn_a_log, gdn_dt_bias, gdn_norm_w, s5_lam_re, s5_lam_im, s5_log_step, s5_b_re, s5_b_im, s5_c_re, s5_c_im, s5_d, s5_w_glu, s5_b_glu, w_out, final_norm_w):
    p = dict(norm_w=norm_w, ffn_w_gate=ffn_w_gate, ffn_w_up=ffn_w_up, ffn_w_down=ffn_w_down,
             gla_gk_up=gla_gk_up, gla_gk_bias=gla_gk_bias, gla_norm_w=gla_norm_w, gdn_conv_w=gdn_conv_w,
             gdn_a_log=gdn_a_log, gdn_dt_bias=gdn_dt_bias, gdn_norm_w=gdn_norm_w, s5_c_re=s5_c_re,
             s5_c_im=s5_c_im, s5_d=s5_d, s5_w_glu=s5_w_glu, s5_b_glu=s5_b_glu, w_out=w_out,
             final_norm_w=final_norm_w)
    b_ctx = x_prompt.shape[0]
    b_dec = x_sample.shape[0]
    cond8 = jnp.zeros((8, D_MODEL), F32).at[0].set(c_ctx).at[1:1 + b_dec].set(c)
    mods = _ada(cond8, w_ada, b_ada)
    mods_ctx = mods[:, 0:1].reshape(DEPTH, 1, N_MOD, D_MODEL)
    mods_dec = mods[:, 1:1 + b_dec].reshape(DEPTH, b_dec, N_MOD, D_MODEL)
    s5c = _s5_consts(s5_lam_re, s5_lam_im, s5_log_step, s5_b_re, s5_b_im, s5_c_re, s5_c_im)

    y_prompt, fin = _trunk(x_prompt, mods_ctx, 1, None, p, s5c, w_in)
    new_gla, new_gdn = fin[-1][0], fin[-1][1]
    s5f = jnp.stack([f[2].reshape(2, 2, b_ctx, S5_GROUPS, S5_P).transpose(1, 2, 0, 3, 4) for f in fin], axis=2)
    new_re, new_im = s5f[0], s5f[1]

    rows = x_sample.shape[1] // 64
    y_sample, _ = _trunk(x_sample, mods_dec, rows, (state_gla, state_gdn, state_s5_re, state_s5_im),
                         p, s5c, w_in)
    return (y_prompt, y_sample, new_gla.astype(state_gla.dtype), new_gdn.astype(state_gdn.dtype),
            new_re.astype(state_s5_re.dtype), new_im.astype(state_s5_im.dtype))
```
